```python
import math
import jax
import jax.numpy as jnp
from jax import lax
import numpy as np

D_MODEL = 1024
BATCH = 8
SEQ = 4096
DEPTH = 4

GRID_W = 64
CTX_LEN = 256
N_EVEN = (DEPTH + 1) // 2
N_ODD = DEPTH // 2
HEAD_DIM = 64
A_WIDTH = D_MODEL // 2
A_HEADS = A_WIDTH // HEAD_DIM
DECAY_LORA = 64
ICLR_LORA = 64
GATE_LORA = 128
A_IN = 3 * A_WIDTH + 2 * DECAY_LORA + 2 * ICLR_LORA + GATE_LORA
RWKV_GN_EPS = 64e-5
B_WIDTH = D_MODEL - A_WIDTH
B_Q_HEADS = B_WIDTH // HEAD_DIM
B_KV_HEADS = 2
B_GROUP = B_Q_HEADS // B_KV_HEADS
B_IN = (B_Q_HEADS + 2 * B_KV_HEADS) * HEAD_DIM
EVEN_IN = A_IN + B_IN
WINDOW = 128
BLOCK = 128
ROPE_THETA = 10000.0
MASK_VALUE = -1e30
HY_ORDER = 2
HY_EMB = 33
HY_BANDS = (HY_EMB - 1) // 2
HY_FFN = 64
HY_TARGET = 1e-2
HY_FAST_PCT = 0.3
HY_SLOW_PCT = 1.5
HY_MOD_SHIFT = 0.05
FFN_HIDDEN = 2816
NORM_EPS = 1e-6

kernel_name = 'hybrid_rwkv7_swa_hyena_dit_block'


def rms_norm(x, g):
    xf = x.astype(jnp.float32)
    xf = xf * lax.rsqrt(jnp.mean(jnp.square(xf), axis=-1, keepdims=True) + NORM_EPS)
    return (xf * g).astype(x.dtype)


def modulate(h, shift, scale):
    return h * (1.0 + scale) + shift


def neighbours(z):
    zp = jnp.pad(z, ((0, 0), (1, 1), (0, 0)))
    return zp[:, :-2], zp[:, 2:]


def dwconv3(z, w, b):
    prev, nxt = neighbours(z)
    return prev * w[0] + z * w[1] + nxt * w[2] + b


def token_shift(z, mu_prev, mu_next):
    prev, nxt = neighbours(z)
    return z + mu_prev * (prev - z) + mu_next * (nxt - z)


def axial_rope(n_tokens):
    rows = n_tokens // GRID_W
    row = jnp.repeat(jnp.arange(rows), GRID_W).astype(jnp.float32)
    col = jnp.tile(jnp.arange(GRID_W), rows).astype(jnp.float32)
    n_freq = HEAD_DIM // 4
    inv = ROPE_THETA ** (-jnp.arange(n_freq, dtype=jnp.float32) / n_freq)
    ang = jnp.concatenate([row[:, None] * inv, col[:, None] * inv], axis=-1)
    return jnp.cos(ang), jnp.sin(ang)


def apply_rope(x, cos, sin):
    half = HEAD_DIM // 2
    x1, x2 = x[..., :half], x[..., half:]
    cos = cos[None, :, None, :].astype(x.dtype)
    sin = sin[None, :, None, :].astype(x.dtype)
    return jnp.concatenate([x1 * cos - x2 * sin, x1 * sin + x2 * cos], axis=-1)


def softmax_with_sink(s, sink):
    m = jnp.maximum(jnp.max(s, axis=-1, keepdims=True), sink)
    e = jnp.exp(s - m)
    return e / (jnp.sum(e, axis=-1, keepdims=True) + jnp.exp(sink - m))


def attn_heads(pb):
    bsz, n = pb.shape[:2]
    nq = B_Q_HEADS * HEAD_DIM
    nk = B_KV_HEADS * HEAD_DIM
    q = pb[..., :nq].reshape(bsz, n, B_Q_HEADS, HEAD_DIM)
    k = pb[..., nq:nq + nk].reshape(bsz, n, B_KV_HEADS, HEAD_DIM)
    v = pb[..., nq + nk:].reshape(bsz, n, B_KV_HEADS, HEAD_DIM)
    return q, k, v


def window_attention(q, k, v, kc, vc, sink):
    bsz, n = q.shape[:2]
    nb = n // BLOCK
    scale = HEAD_DIM ** -0.5
    qb = q.reshape(bsz, nb, BLOCK, B_KV_HEADS, B_GROUP, HEAD_DIM).transpose(1, 0, 2, 3, 4, 5)

    def band(t):
        tp = jnp.pad(t, ((0, 0), (BLOCK, BLOCK), (0, 0), (0, 0)))
        tp = tp.reshape(bsz, nb + 2, BLOCK, B_KV_HEADS, HEAD_DIM)
        tb = jnp.concatenate([tp[:, :-2], tp[:, 1:-1], tp[:, 2:]], axis=2)
        return tb.transpose(1, 0, 2, 3, 4)

    kb, vb = band(k), band(v)
    qi = jnp.arange(BLOCK)[:, None]
    kj = jnp.arange(3 * BLOCK)[None, :]
    in_window = jnp.abs(kj - BLOCK - qi) <= WINDOW
    sink_g = sink.reshape(B_KV_HEADS, B_GROUP)[None, :, :, None, None].astype(jnp.float32)

    def one_block(args):
        blk, q_blk, k_blk, v_blk = args
        kpos = blk * BLOCK - BLOCK + kj
        valid = in_window & (kpos >= 0) & (kpos < n)
        s_loc = jnp.einsum('bqhgd,bkhd->bhgqk', q_blk, k_blk).astype(jnp.float32) * scale
        s_loc = jnp.where(valid, s_loc, MASK_VALUE)
        s_ctx = jnp.einsum('bqhgd,bkhd->bhgqk', q_blk, kc).astype(jnp.float32) * scale
        p = softmax_with_sink(jnp.concatenate([s_loc, s_ctx], axis=-1), sink_g).astype(v.dtype)
        o = jnp.einsum('bhgqk,bkhd->bqhgd', p[..., :3 * BLOCK], v_blk)
        return o + jnp.einsum('bhgqk,bkhd->bqhgd', p[..., 3 * BLOCK:], vc)

    out = lax.map(one_block, (jnp.arange(nb), qb, kb, vb))
    return out.transpose(1, 0, 2, 3, 4, 5).reshape(bsz, n, B_Q_HEADS * HEAD_DIM)


def context_attention(qc, kc, vc, sink):
    bsz, n = qc.shape[:2]
    qg = qc.reshape(bsz, n, B_KV_HEADS, B_GROUP, HEAD_DIM)
    s = jnp.einsum('bqhgd,bkhd->bhgqk', qg, kc).astype(jnp.float32) * (HEAD_DIM ** -0.5)
    sink_g = sink.reshape(B_KV_HEADS, B_GROUP)[None, :, :, None, None].astype(jnp.float32)
    p = softmax_with_sink(s, sink_g).astype(vc.dtype)
    return jnp.einsum('bhgqk,bkhd->bqhgd', p, vc).reshape(bsz, n, B_Q_HEADS * HEAD_DIM)


def rwkv7_inputs(za, ep):
    bsz, n = za.shape[:2]
    C = A_WIDTH
    r = za[..., :C]
    k = za[..., C:2 * C]
    v = za[..., 2 * C:3 * C]
    o = 3 * C
    wd = za[..., o:o + 2 * DECAY_LORA].reshape(bsz, n, 2, DECAY_LORA)
    o += 2 * DECAY_LORA
    ad = za[..., o:o + 2 * ICLR_LORA].reshape(bsz, n, 2, ICLR_LORA)
    o += 2 * ICLR_LORA
    gd = za[..., o:o + GATE_LORA]
    w_log = -jax.nn.softplus(-(ep['w0'] + jnp.einsum('btdr,drc->btdc', jnp.tanh(wd), ep['w2']))) - 0.5
    decay = jnp.exp(-jnp.exp(w_log.astype(jnp.float32)))
    a = jax.nn.sigmoid(ep['a0'] + jnp.einsum('btdr,drc->btdc', ad, ep['a2']))
    g = jax.nn.sigmoid(gd) @ ep['g2']
    kk = (k * ep['k_k']).astype(jnp.float32).reshape(bsz, n, A_HEADS, HEAD_DIM)
    kk = kk / jnp.maximum(jnp.sqrt(jnp.sum(kk * kk, axis=-1, keepdims=True)), 1e-12)
    kk = kk.reshape(bsz, n, C)
    k_dir = k[:, :, None, :] * (1.0 + (a - 1.0) * ep['k_a'])
    return {'r': r, 'decay': decay, 'k': k_dir, 'v': v, 'kk': kk, 'a': a, 'g': g}


def dir_layout(t):
    t = jnp.stack([t[:, :, 0], jnp.flip(t[:, :, 1], axis=1)], axis=0)
    _, bsz, n, _ = t.shape
    return t.reshape(2, bsz, n, A_HEADS, HEAD_DIM).transpose(2, 0, 1, 3, 4).astype(jnp.float32)


def shared_layout(t):
    return dir_layout(jnp.stack([t, t], axis=2))


def rwkv7_run(state0, inp):
    def step(S, xs):
        r_t, w_t, k_t, v_t, kk_t, a_t = xs
        sa = -jnp.einsum('dbhij,dbhj->dbhi', S, kk_t)
        S = S * w_t[..., None, :] + sa[..., None] * (kk_t * a_t)[..., None, :] + v_t[..., None] * k_t[..., None, :]
        return S, jnp.einsum('dbhij,dbhj->dbhi', S, r_t)

    xs = (shared_layout(inp['r']), dir_layout(inp['decay']), dir_layout(inp['k']),
          shared_layout(inp['v']), shared_layout(inp['kk']), dir_layout(inp['a']))
    S, ys = lax.scan(step, state0, xs)
    ys = ys.transpose(1, 2, 0, 3, 4)
    return S, ys[0] + jnp.flip(ys[1], axis=1)


def rwkv7_output(y, inp, ep, dtype):
    bsz, n = y.shape[:2]
    mu = jnp.mean(y, axis=-1, keepdims=True)
    var = jnp.var(y, axis=-1, keepdims=True)
    yn = ((y - mu) * lax.rsqrt(var + RWKV_GN_EPS)).reshape(bsz, n, A_WIDTH) * ep['ln_w'] + ep['ln_b']
    r = inp['r'].reshape(bsz, n, A_HEADS, HEAD_DIM)
    kd = inp['k'].reshape(bsz, n, 2, A_HEADS, HEAD_DIM)
    v = inp['v'].reshape(bsz, n, A_HEADS, HEAD_DIM)
    bonus = jnp.einsum('bthn,btdhn,hn->bth', r, kd, ep['r_k'])[..., None] * v
    return ((yn + bonus.reshape(bsz, n, A_WIDTH)) * inp['g']).astype(dtype)


def even_mixer(h_lat, h_ctx, ep, rope_cos, rope_sin, need_ctx):
    p_lat = h_lat @ ep['w_in']
    p_ctx = h_ctx @ ep['w_in']
    in_ctx = rwkv7_inputs(token_shift(p_ctx[..., :A_IN], ep['mu_prev'], ep['mu_next']), ep)
    in_lat = rwkv7_inputs(token_shift(p_lat[..., :A_IN], ep['mu_prev'], ep['mu_next']), ep)
    state0 = jnp.zeros((2, h_lat.shape[0], A_HEADS, HEAD_DIM, HEAD_DIM), jnp.float32)
    s_ctx, y_ctx = rwkv7_run(state0, in_ctx)
    _, y_lat = rwkv7_run(s_ctx, in_lat)
    a_lat = rwkv7_output(y_lat, in_lat, ep, h_lat.dtype)
    q_l, k_l, v_l = attn_heads(p_lat[..., A_IN:])
    q_c, k_c, v_c = attn_heads(p_ctx[..., A_IN:])
    q_l = apply_rope(rms_norm(q_l, ep['q_norm']), rope_cos, rope_sin)
    k_l = apply_rope(rms_norm(k_l, ep['k_norm']), rope_cos, rope_sin)
    k_c = rms_norm(k_c, ep['k_norm'])
    b_lat = window_attention(q_l, k_l, v_l, k_c, v_c, ep['sink'])
    out_lat = jnp.concatenate([a_lat, b_lat], axis=-1) @ ep['w_out']
    if not need_ctx:
        return out_lat, None
    a_ctx = rwkv7_output(y_ctx, in_ctx, ep, h_ctx.dtype)
    b_ctx = context_attention(rms_norm(q_c, ep['q_norm']), k_c, v_c, ep['sink'])
    out_ctx = jnp.concatenate([a_ctx, b_ctx], axis=-1) @ ep['w_out']
    return out_lat, out_ctx


def hyena_filter_spectra(n, op):
    t = jnp.linspace(0.0, 1.0, n, dtype=jnp.float32)[:, None]
    ang = 2.0 * math.pi * jnp.arange(n, dtype=jnp.float32)[:, None] / n
    f = jnp.linspace(1e-4, HY_BANDS - 1, HY_BANDS, dtype=jnp.float32)[None, :]
    z = jnp.concatenate([t, jnp.cos(f * ang), -jnp.sin(f * ang)], axis=-1)
    hdn = jnp.sin(op['f_freq'] * (z @ op['f_w1'] + op['f_b1']))
    hdn = jnp.sin(op['f_freq'] * (hdn @ op['f_w2'] + op['f_b2']))
    hdn = jnp.sin(op['f_freq'] * (hdn @ op['f_w3'] + op['f_b3']))
    h = (hdn @ op['f_out']).astype(jnp.float32).reshape(n, HY_ORDER, 2, D_MODEL)
    deltas = jnp.abs(jnp.linspace(math.log(HY_TARGET) / HY_SLOW_PCT, math.log(HY_TARGET) / HY_FAST_PCT,
                                  D_MODEL, dtype=jnp.float32))
    h = h * (jnp.exp(-t * deltas) + HY_MOD_SHIFT)[:, None, None, :]
    h_fwd, h_bwd = h[:, :, 0], h[:, :, 1]
    kern = jnp.concatenate([h_fwd[:1] + h_bwd[:1], h_fwd[1:],
                            jnp.zeros((1, HY_ORDER, D_MODEL), jnp.float32), jnp.flip(h_bwd[1:], axis=0)], axis=0)
    return jnp.fft.rfft(kern, axis=0)


def long_conv(u, kern_f, skip):
    n = u.shape[1]
    uf = jnp.fft.rfft(u.astype(jnp.float32), n=2 * n, axis=1)
    y = jnp.fft.irfft(uf * kern_f[None], n=2 * n, axis=1)[:, :n]
    return (y + u * skip).astype(u.dtype)


def hyena_mixer(h, op):
    z = dwconv3(h @ op['w_in'] + op['b_in'], op['conv_w'], op['conv_b'])
    v, x1, x2 = jnp.split(z, 3, axis=-1)
    kf = hyena_filter_spectra(h.shape[1], op)
    y = x1 * long_conv(v, kf[:, 0], op['skip'][0])
    y = x2 * long_conv(y, kf[:, 1], op['skip'][1])
    return y @ op['w_out'] + op['b_out']


def conv_ffn(h, w_up, conv_w, conv_b, w_down):
    u = dwconv3(h @ w_up, conv_w, conv_b)
    gate, val = jnp.split(u, 2, axis=-1)
    return (jax.nn.silu(gate) * val) @ w_down


def setup_inputs(seed: int = 0) -> dict:
    key = jax.random.key(seed)
    keys = jax.random.split(key, 64)
    counter = [0]

    def nxt():
        counter[0] += 1
        return keys[counter[0] - 1]

    def nrm(shape, scale):
        return jax.random.normal(nxt(), shape, jnp.float32) * scale

    def unif(shape, lo, hi):
        return jax.random.uniform(nxt(), shape, jnp.float32, lo, hi)

    D = D_MODEL
    F = FFN_HIDDEN
    E = N_EVEN
    O = N_ODD
    return {
        'x': nrm((BATCH, SEQ, D), 1.0),
        'c': nrm((BATCH, D), 1.0),
        'ctx': nrm((BATCH, CTX_LEN, D), 1.0),
        'c_ctx': nrm((D,), 1.0),
        'ada_w': nrm((DEPTH, D, 6 * D), 0.5 * D ** -0.5),
        'ada_b': nrm((DEPTH, 6 * D), 0.02),
        'norm1_g': 1.0 + nrm((DEPTH, D), 0.02),
        'norm2_g': 1.0 + nrm((DEPTH, D), 0.02),
        'ffn_up': nrm((DEPTH, D, 2 * F), D ** -0.5),
        'ffn_conv_w': nrm((DEPTH, 3, 2 * F), 3 ** -0.5),
        'ffn_conv_b': nrm((DEPTH, 2 * F), 0.02),
        'ffn_down': nrm((DEPTH, F, D), F ** -0.5),
        'ev_w_in': nrm((E, D, EVEN_IN), D ** -0.5),
        'ev_mu_prev': unif((E, A_IN), 0.0, 0.5),
        'ev_mu_next': unif((E, A_IN), 0.0, 0.5),
        'ev_w0': unif((E, 2, A_WIDTH), -6.0, -1.0),
        'ev_w2': nrm((E, 2, DECAY_LORA, A_WIDTH), 0.1 * DECAY_LORA ** -0.5),
        'ev_a0': nrm((E, 2, A_WIDTH), 0.1),
        'ev_a2': nrm((E, 2, ICLR_LORA, A_WIDTH), 0.3 * ICLR_LORA ** -0.5),
        'ev_g2': nrm((E, GATE_LORA, A_WIDTH), GATE_LORA ** -0.5),
        'ev_k_k': 0.85 + nrm((E, A_WIDTH), 0.05),
        'ev_k_a': 1.0 + nrm((E, A_WIDTH), 0.05),
        'ev_r_k': nrm((E, A_HEADS, HEAD_DIM), 0.1),
        'ev_ln_w': 1.0 + nrm((E, A_WIDTH), 0.02),
        'ev_ln_b': nrm((E, A_WIDTH), 0.02),
        'ev_q_norm': 1.0 + nrm((E, HEAD_DIM), 0.02),
        'ev_k_norm': 1.0 + nrm((E, HEAD_DIM), 0.02),
        'ev_sink': nrm((E, B_Q_HEADS), 0.5),
        'ev_w_out': nrm((E, D, D), D ** -0.5),
        'od_w_in': nrm((O, D, 3 * D), D ** -0.5),
        'od_b_in': nrm((O, 3 * D), 0.02),
        'od_conv_w': nrm((O, 3, 3 * D), 3 ** -0.5),
        'od_conv_b': nrm((O, 3 * D), 0.02),
        'od_f_w1': nrm((O, HY_EMB, HY_FFN), HY_EMB ** -0.5),
        'od_f_b1': nrm((O, HY_FFN), 1.0),
        'od_f_w2': nrm((O, HY_FFN, HY_FFN), HY_FFN ** -0.5),
        'od_f_b2': nrm((O, HY_FFN), 0.1),
        'od_f_w3': nrm((O, HY_FFN, HY_FFN), HY_FFN ** -0.5),
        'od_f_b3': nrm((O, HY_FFN), 0.1),
        'od_f_freq': 1.0 + nrm((O, HY_FFN), 0.1),
        'od_f_out': nrm((O, HY_FFN, HY_ORDER * 2 * D), 0.05 * HY_FFN ** -0.5),
        'od_skip': nrm((O, HY_ORDER, D), 0.1),
        'od_w_out': nrm((O, D, D), D ** -0.5),
        'od_b_out': nrm((O, D), 0.02),
    }


def reference(x, c, ctx, c_ctx, ada_w, ada_b, norm1_g, norm2_g, ffn_up, ffn_conv_w, ffn_conv_b, ffn_down,
              ev_w_in, ev_mu_prev, ev_mu_next, ev_w0, ev_w2, ev_a0, ev_a2, ev_g2, ev_k_k, ev_k_a, ev_r_k,
              ev_ln_w, ev_ln_b, ev_q_norm, ev_k_norm, ev_sink, ev_w_out,
              od_w_in, od_b_in, od_conv_w, od_conv_b, od_f_w1, od_f_b1, od_f_w2, od_f_b2, od_f_w3, od_f_b3,
              od_f_freq, od_f_out, od_skip, od_w_out, od_b_out):
    rope_cos, rope_sin = axial_rope(x.shape[1])
    cond_lat = jax.nn.silu(c)
    cond_ctx = jax.nn.silu(c_ctx)
    for layer in range(DEPTH):
        need_ctx = layer < DEPTH - 1
        even = layer % 2 == 0
        j = layer // 2
        mod_l = jnp.split((cond_lat @ ada_w[layer] + ada_b[layer])[:, None, :], 6, axis=-1)
        h_lat = modulate(rms_norm(x, norm1_g[layer]), mod_l[0], mod_l[1])
        if even or need_ctx:
            mod_c = jnp.split((cond_ctx @ ada_w[layer] + ada_b[layer])[None, None, :], 6, axis=-1)
            h_ctx = modulate(rms_norm(ctx, norm1_g[layer]), mod_c[0], mod_c[1])
        if even:
            ep = {'w_in': ev_w_in[j], 'mu_prev': ev_mu_prev[j], 'mu_next': ev_mu_next[j],
                  'w0': ev_w0[j], 'w2': ev_w2[j], 'a0': ev_a0[j], 'a2': ev_a2[j], 'g2': ev_g2[j],
                  'k_k': ev_k_k[j], 'k_a': ev_k_a[j], 'r_k': ev_r_k[j], 'ln_w': ev_ln_w[j], 'ln_b': ev_ln_b[j],
                  'q_norm': ev_q_norm[j], 'k_norm': ev_k_norm[j], 'sink': ev_sink[j], 'w_out': ev_w_out[j]}
            o_lat, o_ctx = even_mixer(h_lat, h_ctx, ep, rope_cos, rope_sin, need_ctx)
        else:
            op = {'w_in': od_w_in[j], 'b_in': od_b_in[j], 'conv_w': od_conv_w[j], 'conv_b': od_conv_b[j],
                  'f_w1': od_f_w1[j], 'f_b1': od_f_b1[j], 'f_w2': od_f_w2[j], 'f_b2': od_f_b2[j],
                  'f_w3': od_f_w3[j], 'f_b3': od_f_b3[j], 'f_freq': od_f_freq[j], 'f_out': od_f_out[j],
                  'skip': od_skip[j], 'w_out': od_w_out[j], 'b_out': od_b_out[j]}
            o_lat = hyena_mixer(h_lat, op)
            o_ctx = hyena_mixer(h_ctx, op) if need_ctx else None
        x = x + mod_l[2] * o_lat
        x = x + mod_l[5] * conv_ffn(modulate(rms_norm(x, norm2_g[layer]), mod_l[3], mod_l[4]),
                                    ffn_up[layer], ffn_conv_w[layer], ffn_conv_b[layer], ffn_down[layer])
        if need_ctx:
            ctx = ctx + mod_c[2] * o_ctx
            ctx = ctx + mod_c[5] * conv_ffn(modulate(rms_norm(ctx, norm2_g[layer]), mod_c[3], mod_c[4]),
                                            ffn_up[layer], ffn_conv_w[layer], ffn_conv_b[layer], ffn_down[layer])
    return x
```

```python
import functools
import math

import numpy as np
import jax
import jax.numpy as jnp
from jax import lax
from jax.experimental import pallas as pl
from jax.experimental.pallas import tpu as pltpu

F32 = jnp.float32
BF16 = jnp.bfloat16

HEAD_DIM = 64
GRID_W = 64
DECAY_LORA = 64
ICLR_LORA = 64
GATE_LORA = 128
RWKV_GN_EPS = 64e-5
B_KV_HEADS = 2
WINDOW = 128
BLOCK = 128
ROPE_THETA = 10000.0
MASK_VALUE = -1e30
HY_ORDER = 2
HY_EMB = 33
HY_BANDS = (HY_EMB - 1) // 2
HY_TARGET = 1e-2
HY_FAST_PCT = 0.3
HY_SLOW_PCT = 1.5
HY_MOD_SHIFT = 0.05
NORM_EPS = 1e-6

V7X_VMEM_BYTES = 64 * 1024 * 1024
VMEM_LIMIT_BYTES = V7X_VMEM_BYTES * 3 // 4
LANES = 128
BF16_ROWS = 16
RWKV_CHUNK = 64
RWKV_INV_BASE = 8
FFT_N2 = 128
DENSE_FFT_MAX = 512

HIGHEST = lax.Precision.HIGHEST


def _cparams(*sem):
    return pltpu.CompilerParams(dimension_semantics=sem, vmem_limit_bytes=VMEM_LIMIT_BYTES)


def _dot(a, b):
    return jnp.dot(a.astype(BF16), b.astype(BF16), preferred_element_type=F32)


def _dot_nt(a, b):
    return lax.dot_general(a.astype(BF16), b.astype(BF16), (((1,), (1,)), ((), ())), preferred_element_type=F32)


def _dot_tn(a, b):
    return lax.dot_general(a.astype(BF16), b.astype(BF16), (((0,), (0,)), ((), ())), preferred_element_type=F32)


def _dot_hi(a, b):
    return jnp.dot(a, b, preferred_element_type=F32, precision=HIGHEST)


def _dot_nt_hi(a, b):
    return lax.dot_general(a, b, (((1,), (1,)), ((), ())), preferred_element_type=F32, precision=HIGHEST)


def _dot_tn_hi(a, b):
    return lax.dot_general(a, b, (((0,), (0,)), ((), ())), preferred_element_type=F32, precision=HIGHEST)


def _segsum(x, e_ref):
    hi = x.astype(BF16)
    lo = (x - hi.astype(F32)).astype(BF16)
    e = e_ref[...]
    return jnp.dot(hi, e, preferred_element_type=F32) + jnp.dot(lo, e, preferred_element_type=F32)


def _norm_mod(x, g, shift, scale):
    xn = x * lax.rsqrt(jnp.mean(x * x, axis=-1, keepdims=True) + NORM_EPS)
    return (xn * g) * (1.0 + scale) + shift


def _row_tile(t, want):
    tm = min(t, want)
    assert t % tm == 0 and tm % BF16_ROWS == 0
    return tm


def _ada_kernel(c_ref, w_ref, b_ref, o_ref):
    c = c_ref[...]
    s = c * jax.nn.sigmoid(c)
    o_ref[0] = _dot(s, w_ref[0]) + b_ref[0]


def _ada_mod(cond, ada_w, ada_b):
    depth, d, n = ada_w.shape
    tn = n // 4
    rows = cond.shape[0]
    return pl.pallas_call(
        _ada_kernel,
        grid=(depth, n // tn),
        in_specs=[pl.BlockSpec((rows, d), lambda l, j: (0, 0)),
                  pl.BlockSpec((1, d, tn), lambda l, j: (l, 0, j)),
                  pl.BlockSpec((1, 1, tn), lambda l, j: (l, 0, j))],
        out_specs=pl.BlockSpec((1, rows, tn), lambda l, j: (l, 0, j)),
        out_shape=jax.ShapeDtypeStruct((depth, rows, n), F32),
        compiler_params=_cparams("parallel", "parallel"),
        name="ada_mod",
    )(cond, ada_w, ada_b.reshape(depth, 1, n))


def _proj_kernel(x_ref, g_ref, mod_ref, w_ref, o_ref, h_scr, *, d, slot):
    @pl.when(pl.program_id(2) == 0)
    def _():
        m = mod_ref[0]
        h = _norm_mod(x_ref[0], g_ref[...], m[:, slot * d:(slot + 1) * d], m[:, (slot + 1) * d:(slot + 2) * d])
        h_scr[...] = h.astype(BF16)

    o_ref[0] = jnp.dot(h_scr[...], w_ref[...], preferred_element_type=F32)


def _proj(x, g, mod, w, slot):
    b, t, d = x.shape
    n = w.shape[1]
    tm = _row_tile(t, 512)
    tn = 128 * math.gcd(n // 128, 7 if n % 896 == 0 else 4)
    return pl.pallas_call(
        functools.partial(_proj_kernel, d=d, slot=slot),
        grid=(b, t // tm, n // tn),
        in_specs=[pl.BlockSpec((1, tm, d), lambda bi, i, j: (bi, i, 0)),
                  pl.BlockSpec((1, d), lambda bi, i, j: (0, 0)),
                  pl.BlockSpec((1, 1, mod.shape[-1]), lambda bi, i, j: (bi, 0, 0)),
                  pl.BlockSpec((d, tn), lambda bi, i, j: (0, j))],
        out_specs=pl.BlockSpec((1, tm, tn), lambda bi, i, j: (bi, i, j)),
        out_shape=jax.ShapeDtypeStruct((b, t, n), F32),
        scratch_shapes=[pltpu.VMEM((tm, d), BF16)],
        compiler_params=_cparams("parallel", "parallel", "arbitrary"),
        name="norm_mod_proj",
    )(x, g.reshape(1, d), mod, w)


def _halo_specs(tm, t, d, nargs):
    r = tm // BF16_ROWS
    last = t // BF16_ROWS - 1
    if nargs == 3:
        prev = pl.BlockSpec((1, BF16_ROWS, d), lambda bi, i, j: (bi, jnp.maximum(i * r - 1, 0), 0))
        nxt = pl.BlockSpec((1, BF16_ROWS, d), lambda bi, i, j: (bi, jnp.minimum((i + 1) * r, last), 0))
    else:
        prev = pl.BlockSpec((1, BF16_ROWS, d), lambda bi, i: (bi, jnp.maximum(i * r - 1, 0), 0))
        nxt = pl.BlockSpec((1, BF16_ROWS, d), lambda bi, i: (bi, jnp.minimum((i + 1) * r, last), 0))
    return prev, nxt


def _fill_h(x_ref, xp_ref, xn_ref, g, shift, scale, h_scr, tm):
    h_scr[0:BF16_ROWS] = _norm_mod(xp_ref[0], g, shift, scale).astype(BF16)
    h_scr[BF16_ROWS:BF16_ROWS + tm] = _norm_mod(x_ref[0], g, shift, scale).astype(BF16)
    h_scr[BF16_ROWS + tm:2 * BF16_ROWS + tm] = _norm_mod(xn_ref[0], g, shift, scale).astype(BF16)


def _conv3_rows(u, u_scr, cw_ref, cb_ref, tm, t_total):
    i = pl.program_id(1)
    rows = lax.broadcasted_iota(jnp.int32, (tm + 2 * BF16_ROWS, 1), 0) + (i * tm - BF16_ROWS)
    u_scr[...] = jnp.where((rows >= 0) & (rows < t_total), u, 0.0)
    cw = cw_ref[...]
    o = BF16_ROWS
    return (u_scr[o - 1:o - 1 + tm] * cw[0:1] + u_scr[o:o + tm] * cw[1:2] + u_scr[o + 1:o + 1 + tm] * cw[2:3]
            + cb_ref[...])


def _proj_conv_kernel(x_ref, xp_ref, xn_ref, g_ref, mod_ref, w_ref, b_ref, cw_ref, cb_ref, o_ref, h_scr, u_scr,
                      *, d, tm, t_total):
    @pl.when(pl.program_id(2) == 0)
    def _():
        m = mod_ref[0]
        _fill_h(x_ref, xp_ref, xn_ref, g_ref[...], m[:, 0:d], m[:, d:2 * d], h_scr, tm)

    u = jnp.dot(h_scr[...], w_ref[...], preferred_element_type=F32) + b_ref[...]
    o_ref[0, 0] = _conv3_rows(u, u_scr, cw_ref, cb_ref, tm, t_total)


def _proj_conv(x, g, mod, w, bias, cw, cb):
    b, t, d = x.shape
    n = w.shape[1]
    tm = _row_tile(t, 512)
    tn = d
    prev, nxt = _halo_specs(tm, t, d, 3)
    return pl.pallas_call(
        functools.partial(_proj_conv_kernel, d=d, tm=tm, t_total=t),
        grid=(b, t // tm, n // tn),
        in_specs=[pl.BlockSpec((1, tm, d), lambda bi, i, j: (bi, i, 0)), prev, nxt,
                  pl.BlockSpec((1, d), lambda bi, i, j: (0, 0)),
                  pl.BlockSpec((1, 1, mod.shape[-1]), lambda bi, i, j: (bi, 0, 0)),
                  pl.BlockSpec((d, tn), lambda bi, i, j: (0, j)),
                  pl.BlockSpec((1, tn), lambda bi, i, j: (0, j)),
                  pl.BlockSpec((3, tn), lambda bi, i, j: (0, j)),
                  pl.BlockSpec((1, tn), lambda bi, i, j: (0, j))],
        out_specs=pl.BlockSpec((1, 1, tm, tn), lambda bi, i, j: (j, bi, i, 0)),
        out_shape=jax.ShapeDtypeStruct((n // tn, b, t, tn), F32),
        scratch_shapes=[pltpu.VMEM((tm + 2 * BF16_ROWS, d), BF16), pltpu.VMEM((tm + 2 * BF16_ROWS, tn), F32)],
        compiler_params=_cparams("parallel", "parallel", "arbitrary"),
        name="hyena_in_proj",
    )(x, x, x, g.reshape(1, d), mod, w, bias.reshape(1, n), cw, cb.reshape(1, n))


def _ffn_kernel(x_ref, xp_ref, xn_ref, g_ref, mod_ref, wg_ref, wv_ref, cwg_ref, cwv_ref, cbg_ref, cbv_ref, wd_ref,
                o_ref, h_scr, ug_scr, uv_scr, acc_scr, *, d, tm, t_total):
    j = pl.program_id(2)

    @pl.when(j == 0)
    def _():
        m = mod_ref[0]
        _fill_h(x_ref, xp_ref, xn_ref, g_ref[...], m[:, 3 * d:4 * d], m[:, 4 * d:5 * d], h_scr, tm)
        acc_scr[...] = jnp.zeros_like(acc_scr)

    h = h_scr[...]
    gate = _conv3_rows(jnp.dot(h, wg_ref[...], preferred_element_type=F32), ug_scr, cwg_ref, cbg_ref, tm, t_total)
    val = _conv3_rows(jnp.dot(h, wv_ref[...], preferred_element_type=F32), uv_scr, cwv_ref, cbv_ref, tm, t_total)
    act = (gate * jax.nn.sigmoid(gate)) * val
    acc_scr[...] += jnp.dot(act.astype(BF16), wd_ref[...], preferred_element_type=F32)

    @pl.when(j == pl.num_programs(2) - 1)
    def _():
        o_ref[0] = x_ref[0] + mod_ref[0][:, 5 * d:6 * d] * acc_scr[...]


def _conv_ffn(x, g, mod, w_up, cw, cb, w_down):
    b, t, d = x.shape
    f = w_down.shape[0]
    tm = _row_tile(t, 1024)
    tf = 256 if f % 256 == 0 else 128
    nf = f // tf
    prev, nxt = _halo_specs(tm, t, d, 3)
    cb2 = cb.reshape(1, 2 * f)
    return pl.pallas_call(
        functools.partial(_ffn_kernel, d=d, tm=tm, t_total=t),
        grid=(b, t // tm, nf),
        in_specs=[pl.BlockSpec((1, tm, d), lambda bi, i, j: (bi, i, 0)), prev, nxt,
                  pl.BlockSpec((1, d), lambda bi, i, j: (0, 0)),
                  pl.BlockSpec((1, 1, mod.shape[-1]), lambda bi, i, j: (bi, 0, 0)),
                  pl.BlockSpec((d, tf), lambda bi, i, j: (0, j)),
                  pl.BlockSpec((d, tf), lambda bi, i, j: (0, nf + j)),
                  pl.BlockSpec((3, tf), lambda bi, i, j: (0, j)),
                  pl.BlockSpec((3, tf), lambda bi, i, j: (0, nf + j)),
                  pl.BlockSpec((1, tf), lambda bi, i, j: (0, j)),
                  pl.BlockSpec((1, tf), lambda bi, i, j: (0, nf + j)),
                  pl.BlockSpec((tf, d), lambda bi, i, j: (j, 0))],
        out_specs=pl.BlockSpec((1, tm, d), lambda bi, i, j: (bi, i, 0)),
        out_shape=jax.ShapeDtypeStruct((b, t, d), F32),
        scratch_shapes=[pltpu.VMEM((tm + 2 * BF16_ROWS, d), BF16),
                        pltpu.VMEM((tm + 2 * BF16_ROWS, tf), F32),
                        pltpu.VMEM((tm + 2 * BF16_ROWS, tf), F32),
                        pltpu.VMEM((tm, d), F32)],
        compiler_params=_cparams("parallel", "parallel", "arbitrary"),
        name="conv_ffn",
    )(x, x, x, g.reshape(1, d), mod, w_up, w_up, cw, cw, cb2, cb2, w_down)


def _out_res_kernel(y_ref, x_ref, mod_ref, w_ref, b_ref, o_ref, *, d):
    o = _dot(y_ref[0], w_ref[...]) + b_ref[...]
    o_ref[0] = x_ref[0] + mod_ref[0][:, 2 * d:3 * d] * o


def _out_res(y, x, mod, w, bias):
    b, t, d = x.shape
    tm = _row_tile(t, 512)
    return pl.pallas_call(
        functools.partial(_out_res_kernel, d=d),
        grid=(b, t // tm),
        in_specs=[pl.BlockSpec((1, tm, d), lambda bi, i: (bi, i, 0)),
                  pl.BlockSpec((1, tm, d), lambda bi, i: (bi, i, 0)),
                  pl.BlockSpec((1, 1, mod.shape[-1]), lambda bi, i: (bi, 0, 0)),
                  pl.BlockSpec((d, d), lambda bi, i: (0, 0)),
                  pl.BlockSpec((1, d), lambda bi, i: (0, 0))],
        out_specs=pl.BlockSpec((1, tm, d), lambda bi, i: (bi, i, 0)),
        out_shape=jax.ShapeDtypeStruct((b, t, d), F32),
        compiler_params=_cparams("parallel", "parallel"),
        name="out_proj_residual",
    )(y, x, mod, w, bias.reshape(1, d))


def _softplus(z):
    return jnp.maximum(z, 0.0) + jnp.log1p(jnp.exp(-jnp.abs(z)))


def _even_prep_kernel(p_ref, pp_ref, pn_ref, mup_ref, mun_ref, w0_ref, w2_ref, a0_ref, a2_ref, g2_ref, kk_ref,
                      ka_ref, qn_ref, kn_ref, cos_ref, sin_ref, e_ref,
                      r_ref, v_ref, kkn_ref, g_ref, lw_ref, a_ref, kd_ref, q_ref, katt_ref, vatt_ref,
                      *, tm, c, a_in, rope):
    i = pl.program_id(1)
    nt = pl.num_programs(1)
    pa = p_ref[0][:, 0:a_in]
    rows = lax.broadcasted_iota(jnp.int32, (tm, 1), 0)
    prev_row = jnp.where(i > 0, pp_ref[0][BF16_ROWS - 1:BF16_ROWS, 0:a_in], 0.0)
    next_row = jnp.where(i < nt - 1, pn_ref[0][0:1, 0:a_in], 0.0)
    prev = jnp.where(rows == 0, prev_row, pltpu.roll(pa, 1, 0))
    nxt = jnp.where(rows == tm - 1, next_row, pltpu.roll(pa, tm - 1, 0))
    za = pa + mup_ref[...] * (prev - pa) + mun_ref[...] * (nxt - pa)

    r = za[:, 0:c]
    k = za[:, c:2 * c]
    v = za[:, 2 * c:3 * c]
    wd = jnp.tanh(za[:, 3 * c:3 * c + 2 * DECAY_LORA])
    ad = za[:, 3 * c + 2 * DECAY_LORA:3 * c + 2 * DECAY_LORA + 2 * ICLR_LORA]
    gd = za[:, 3 * c + 2 * DECAY_LORA + 2 * ICLR_LORA:a_in]
    r_ref[0] = r
    v_ref[0] = v
    g_ref[0] = _dot(jax.nn.sigmoid(gd), g2_ref[...])
    kkv = k * kk_ref[...]
    ss = _segsum(kkv * kkv, e_ref)
    kkn_ref[0] = kkv / jnp.maximum(jnp.sqrt(ss), 1e-12)
    for dd in range(2):
        w_log = -_softplus(-(w0_ref[dd:dd + 1] + _dot(wd, w2_ref[dd]))) - 0.5
        lw_ref[dd, 0] = -jnp.exp(w_log)
        a = jax.nn.sigmoid(a0_ref[dd:dd + 1] + _dot(ad, a2_ref[dd]))
        a_ref[dd, 0] = a
        kd_ref[dd, 0] = k * (1.0 + (a - 1.0) * ka_ref[...])

    pb = p_ref[0][:, a_in:]
    nq = q_ref.shape[-1]
    nk = katt_ref.shape[-1]
    q = pb[:, 0:nq]
    kat = pb[:, nq:nq + nk]
    vatt_ref[0] = pb[:, nq + nk:]
    half = HEAD_DIM // 2

    def norm_rope(x, gain, e):
        w = x.shape[-1]
        ms = _segsum(x * x, e) * (1.0 / HEAD_DIM)
        xn = x * lax.rsqrt(ms + NORM_EPS) * gain
        if not rope:
            return xn
        lane = lax.broadcasted_iota(jnp.int32, (1, w), 1)
        first = (lane % HEAD_DIM) < half
        swapped = jnp.where(first, pltpu.roll(xn, w - half, 1), pltpu.roll(xn, half, 1))
        reps = w // LANES
        cos = jnp.concatenate([cos_ref[...]] * reps, axis=1) if reps > 1 else cos_ref[...]
        sin = jnp.concatenate([sin_ref[...]] * reps, axis=1) if reps > 1 else sin_ref[...]
        return xn * cos + swapped * sin

    q_ref[0] = norm_rope(q, qn_ref[...], e_ref[...])
    katt_ref[0] = norm_rope(kat, kn_ref[...], e_ref[0:nk, 0:nk])


def _even_prep(p, ep, tabs, rope):
    b, t, n = p.shape
    c = ep['k_k'].shape[-1]
    a_in = ep['mu_prev'].shape[-1]
    nq = ep['nq']
    nk = ep['nk']
    tm = _row_tile(t, 256)
    prev, nxt = _halo_specs(tm, t, n, 2)
    full = lambda *shape: pl.BlockSpec(shape, lambda bi, i: (0,) * len(shape))
    tok = lambda w: pl.BlockSpec((1, tm, w), lambda bi, i: (bi, i, 0))
    tok2 = lambda w: pl.BlockSpec((2, 1, tm, w), lambda bi, i: (0, bi, i, 0))
    sd = lambda *shape: jax.ShapeDtypeStruct(shape, F32)
    return pl.pallas_call(
        functools.partial(_even_prep_kernel, tm=tm, c=c, a_in=a_in, rope=rope),
        grid=(b, t // tm),
        in_specs=[tok(n), prev, nxt, full(1, a_in), full(1, a_in), full(2, c), full(2, 2 * DECAY_LORA, c),
                  full(2, c), full(2, 2 * ICLR_LORA, c), full(GATE_LORA, c), full(1, c), full(1, c),
                  full(1, nq), full(1, nk),
                  pl.BlockSpec((tm, LANES), lambda bi, i: (i, 0)), pl.BlockSpec((tm, LANES), lambda bi, i: (i, 0)),
                  full(nq, nq)],
        out_specs=[tok(c), tok(c), tok(c), tok(c), tok2(c), tok2(c), tok2(c), tok(nq), tok(nk), tok(nk)],
        out_shape=[sd(b, t, c), sd(b, t, c), sd(b, t, c), sd(b, t, c), sd(2, b, t, c), sd(2, b, t, c),
                   sd(2, b, t, c), sd(b, t, nq), sd(b, t, nk), sd(b, t, nk)],
        compiler_params=_cparams("parallel", "parallel"),
        name="even_prep",
    )(p, p, p, ep['mu_prev'], ep['mu_next'], ep['w0'], ep['w2pad'], ep['a0'], ep['a2pad'], ep['g2'], ep['k_k'],
      ep['k_a'], ep['q_norm_t'], ep['k_norm_t'], tabs['cos'][:t], tabs['sin'][:t], tabs['e'])


def _rwkv_kernel(r_ref, v_ref, kk_ref, lw_ref, a_ref, kd_ref, h0_ref, y_ref, ht_ref, s_scr, *, cs, nh):
    dirn = pl.program_id(0)
    ci = pl.program_id(2)

    @pl.when(ci == 0)
    def _():
        s_scr[...] = h0_ref[0, 0]

    r = r_ref[0]
    v = v_ref[0]
    kk = kk_ref[0]
    lw = lw_ref[0, 0]
    a = a_ref[0, 0]
    kd = kd_ref[0, 0]
    row = lax.broadcasted_iota(jnp.int32, (cs, cs), 0)
    col = lax.broadcasted_iota(jnp.int32, (cs, cs), 1)
    diff = (row - col) * (1 - 2 * dirn)
    incl = diff >= 0
    eye = (diff == 0).astype(F32)
    g = _dot_hi(incl.astype(F32), lw)
    gp = g - lw
    gref = g[cs // 2:cs // 2 + 1]
    gend = jnp.sum(lw, axis=0, keepdims=True)
    bvec = kk * a
    a_t = -kk * jnp.exp(gp - gref)
    a_0 = -kk * jnp.exp(gp)
    b_t = bvec * jnp.exp(gref - g)
    b_e = bvec * jnp.exp(gend - g)
    k_t = kd * jnp.exp(gref - g)
    k_e = kd * jnp.exp(gend - g)
    r_t = r * jnp.exp(g - gref)
    r_0 = r * jnp.exp(g)
    s_all = s_scr[...]
    wend = jnp.exp(gend)
    row2 = lax.broadcasted_iota(jnp.int32, (2 * cs, 2 * cs), 0)
    col2 = lax.broadcasted_iota(jnp.int32, (2 * cs, 2 * cs), 1)
    diff2 = ((row2 % cs) - (col2 % cs)) * (1 - 2 * dirn)
    mask2 = (diff2 > 0) | ((row2 >= cs) & (diff2 == 0))
    base = min(RWKV_INV_BASE, cs)
    same_base = (row // base) == (col // base)
    ys = []
    ss = []
    for h in range(nh):
        sl = slice(h * HEAD_DIM, (h + 1) * HEAD_DIM)
        gram = _dot_nt_hi(jnp.concatenate([a_t[:, sl], r_t[:, sl]], axis=0),
                          jnp.concatenate([b_t[:, sl], k_t[:, sl]], axis=0))
        gram = jnp.where(mask2, gram, 0.0)
        nmat = gram[0:cs, 0:cs]
        a_ak = gram[0:cs, cs:2 * cs]
        a_r = gram[cs:2 * cs, :]
        npow = jnp.where(same_base, nmat, 0.0)
        p = eye + npow
        for _ in range(int(math.log2(base)) - 1):
            npow = _dot_hi(npow, npow)
            p = p + _dot_hi(npow, p)
        m = base
        while m < cs:
            off = ((row // m) != (col // m)) & ((row // (2 * m)) == (col // (2 * m)))
            p = p + _dot_hi(p, _dot_hi(jnp.where(off, nmat, 0.0), p))
            m *= 2
        s_h = s_all[:, sl]
        vh = v[:, sl]
        u = _dot_hi(p, _dot_nt_hi(a_0[:, sl], s_h) + _dot_hi(a_ak, vh))
        uv = jnp.concatenate([u, vh], axis=0)
        ys.append(_dot_nt_hi(r_0[:, sl], s_h) + _dot_hi(a_r, uv))
        ss.append(s_h * wend[:, sl] + _dot_tn_hi(uv, jnp.concatenate([b_e[:, sl], k_e[:, sl]], axis=0)))
    y_ref[0, 0] = jnp.concatenate(ys, axis=1)
    s_new = jnp.concatenate(ss, axis=1)
    s_scr[...] = s_new

    @pl.when(ci == pl.num_programs(2) - 1)
    def _():
        ht_ref[0, 0] = s_new


def _rwkv(r, v, kk, lw, a, kd, h0):
    b, t, c = r.shape
    cs = min(RWKV_CHUNK, t)
    nc = t // cs
    nh = c // HEAD_DIM

    def cidx(d, ci):
        return ci + d * (nc - 1 - 2 * ci)

    tok = pl.BlockSpec((1, cs, c), lambda d, bi, ci: (bi, cidx(d, ci), 0))
    tok2 = pl.BlockSpec((1, 1, cs, c), lambda d, bi, ci: (d, bi, cidx(d, ci), 0))
    st = pl.BlockSpec((1, 1, HEAD_DIM, c), lambda d, bi, ci: (d, bi, 0, 0))
    return pl.pallas_call(
        functools.partial(_rwkv_kernel, cs=cs, nh=nh),
        grid=(2, b, nc),
        in_specs=[tok, tok, tok, tok2, tok2, tok2, st],
        out_specs=[tok2, st],
        out_shape=[jax.ShapeDtypeStruct((2, b, t, c), F32), jax.ShapeDtypeStruct((2, b, HEAD_DIM, c), F32)],
        scratch_shapes=[pltpu.VMEM((HEAD_DIM, c), F32)],
        compiler_params=_cparams("parallel", "parallel", "arbitrary"),
        name="rwkv7_chunk_scan",
    )(r, v, kk, lw, a, kd, h0)


def _attn_kernel(sink_ref, q_ref, kp_ref, kc_ref, kn_ref, vp_ref, vc_ref, vn_ref, kx_ref, vx_ref, o_ref,
                 *, local, group):
    i = pl.program_id(1)
    nb = pl.num_programs(1)
    scale = HEAD_DIM ** -0.5
    q = q_ref[0]
    if local:
        k_all = jnp.concatenate([kp_ref[0], kc_ref[0], kn_ref[0], kx_ref[0]], axis=0)
        v_all = jnp.concatenate([vp_ref[0], vc_ref[0], vn_ref[0], vx_ref[0]], axis=0)
        nkeys = k_all.shape[0]
        qi = lax.broadcasted_iota(jnp.int32, (BLOCK, nkeys), 0)
        kj = lax.broadcasted_iota(jnp.int32, (BLOCK, nkeys), 1)
        kpos = kj + (i - 1) * BLOCK
        valid = (jnp.abs(kj - BLOCK - qi) <= WINDOW) & (kpos >= 0) & (kpos < nb * BLOCK)
        valid = valid | (kj >= 3 * BLOCK)
    else:
        k_all = kx_ref[0]
        v_all = vx_ref[0]
    outs = []
    for h in range(q.shape[-1] // HEAD_DIM):
        hk = h // group
        kh = k_all[:, hk * HEAD_DIM:(hk + 1) * HEAD_DIM]
        vh = v_all[:, hk * HEAD_DIM:(hk + 1) * HEAD_DIM]
        s = _dot_nt(q[:, h * HEAD_DIM:(h + 1) * HEAD_DIM], kh) * scale
        if local:
            s = jnp.where(valid, s, MASK_VALUE)
        sink = sink_ref[h]
        m = jnp.maximum(jnp.max(s, axis=-1, keepdims=True), sink)
        e = jnp.exp(s - m)
        p = e / (jnp.sum(e, axis=-1, keepdims=True) + jnp.exp(sink - m))
        outs.append(_dot(p, vh))
    o_ref[0] = jnp.concatenate(outs, axis=1)


def _attention(q, k, v, kx, vx, sink, local):
    b, t, nq = q.shape
    nk = k.shape[-1]
    nb = t // BLOCK
    group = (nq // HEAD_DIM) // (nk // HEAD_DIM)
    lx = kx.shape[1]
    kv = lambda f: pl.BlockSpec((1, BLOCK, nk), f)
    pf = lambda bi, i: (bi, jnp.maximum(i - 1, 0), 0)
    cf = lambda bi, i: (bi, i, 0)
    nf = lambda bi, i: (bi, jnp.minimum(i + 1, nb - 1), 0)
    ctx = pl.BlockSpec((1, lx, nk), lambda bi, i: (bi, 0, 0))
    return pl.pallas_call(
        functools.partial(_attn_kernel, local=local, group=group),
        grid=(b, nb),
        in_specs=[pl.BlockSpec(memory_space=pltpu.SMEM),
                  pl.BlockSpec((1, BLOCK, nq), cf), kv(pf), kv(cf), kv(nf), kv(pf), kv(cf), kv(nf), ctx, ctx],
        out_specs=pl.BlockSpec((1, BLOCK, nq), cf),
        out_shape=jax.ShapeDtypeStruct((b, t, nq), F32),
        compiler_params=_cparams("parallel", "parallel"),
        name="window_attention" if local else "context_attention",
    )(sink, q, k, k, k, v, v, v, kx, vx)


def _even_out_kernel(y_ref, r_ref, v_ref, g_ref, kd_ref, batt_ref, x_ref, mod_ref, lnw_ref, lnb_ref, rk_ref, e_ref,
                     w_ref, o_ref, *, d, c):
    y = y_ref[0, 0] + y_ref[1, 0]
    inv = 1.0 / HEAD_DIM
    mu = _segsum(y, e_ref) * inv
    yc = y - mu
    var = _segsum(yc * yc, e_ref) * inv
    yn = yc * lax.rsqrt(var + RWKV_GN_EPS) * lnw_ref[...] + lnb_ref[...]
    bonus = _segsum(r_ref[0] * (kd_ref[0, 0] + kd_ref[1, 0]) * rk_ref[...], e_ref)
    a_out = (yn + bonus * v_ref[0]) * g_ref[0]
    o = _dot(a_out, w_ref[0:c]) + _dot(batt_ref[0], w_ref[c:])
    o_ref[0] = x_ref[0] + mod_ref[0][:, 2 * d:3 * d] * o


def _even_out(y, r, v, g, kd, batt, x, mod, ep, tabs):
    b, t, d = x.shape
    c = r.shape[-1]
    nq = batt.shape[-1]
    tm = _row_tile(t, 512)
    tok = lambda w: pl.BlockSpec((1, tm, w), lambda bi, i: (bi, i, 0))
    tok2 = lambda w: pl.BlockSpec((2, 1, tm, w), lambda bi, i: (0, bi, i, 0))
    full = lambda *shape: pl.BlockSpec(shape, lambda bi, i: (0,) * len(shape))
    return pl.pallas_call(
        functools.partial(_even_out_kernel, d=d, c=c),
        grid=(b, t // tm),
        in_specs=[tok2(c), tok(c), tok(c), tok(c), tok2(c), tok(nq), tok(d),
                  pl.BlockSpec((1, 1, mod.shape[-1]), lambda bi, i: (bi, 0, 0)),
                  full(1, c), full(1, c), full(1, c), full(c, c), full(c + nq, d)],
        out_specs=tok(d),
        out_shape=jax.ShapeDtypeStruct((b, t, d), F32),
        compiler_params=_cparams("parallel", "parallel"),
        name="even_out_proj",
    )(y, r, v, g, kd, batt, x, mod, ep['ln_w'], ep['ln_b'], ep['r_k'], tabs['e'], ep['w_out'])


def _filter_kernel(z_ref, t_ref, w1_ref, b1_ref, w2_ref, b2_ref, w3_ref, b3_ref, fr_ref, wo_ref, dl_ref, o_ref, *, d):
    fr = fr_ref[...]
    h = jnp.sin(fr * (_dot_hi(z_ref[...], w1_ref[...]) + b1_ref[...]))
    h = jnp.sin(fr * (_dot_hi(h, w2_ref[...]) + b2_ref[...]))
    h = jnp.sin(fr * (_dot_hi(h, w3_ref[...]) + b3_ref[...]))
    filt = _dot_hi(h, wo_ref[...])
    modu = jnp.exp(-t_ref[...] * dl_ref[...]) + HY_MOD_SHIFT
    for q in range(o_ref.shape[0]):
        o_ref[q] = filt[:, q * d:(q + 1) * d] * modu


def _hyena_filters(n, op, d):
    t = np.linspace(0.0, 1.0, n, dtype=np.float32)[:, None]
    ang = (2.0 * math.pi * np.arange(n, dtype=np.float32)[:, None] / np.float32(n)).astype(np.float32)
    f = np.linspace(1e-4, HY_BANDS - 1, HY_BANDS, dtype=np.float32)[None, :]
    zfeat = jnp.concatenate([jnp.asarray(t), jnp.cos(jnp.asarray(f * ang)), -jnp.sin(jnp.asarray(f * ang))], axis=-1)
    emb_pad = op['f_w1'].shape[0]
    zfeat = jnp.pad(zfeat, ((0, 0), (0, emb_pad - HY_EMB)))
    deltas = np.abs(np.linspace(math.log(HY_TARGET) / HY_SLOW_PCT, math.log(HY_TARGET) / HY_FAST_PCT, d,
                                dtype=np.float32))[None, :]
    tn = min(n, 256)
    nq = 2 * HY_ORDER
    hf = op['f_w2'].shape[0]
    full = lambda *shape: pl.BlockSpec(shape, lambda i: (0,) * len(shape))
    return pl.pallas_call(
        functools.partial(_filter_kernel, d=d),
        grid=(n // tn,),
        in_specs=[pl.BlockSpec((tn, emb_pad), lambda i: (i, 0)), pl.BlockSpec((tn, 1), lambda i: (i, 0)),
                  full(emb_pad, hf), full(1, hf), full(hf, hf), full(1, hf), full(hf, hf), full(1, hf), full(1, hf),
                  full(hf, nq * d), full(1, d)],
        out_specs=pl.BlockSpec((nq, tn, d), lambda i: (0, i, 0)),
        out_shape=jax.ShapeDtypeStruct((nq, n, d), F32),
        compiler_params=_cparams("parallel"),
        name="hyena_filter",
    )(zfeat, jnp.asarray(t), op['f_w1'], op['f_b1'], op['f_w2'], op['f_b2'], op['f_w3'], op['f_b3'], op['f_freq'],
      op['f_out'], jnp.asarray(deltas))


def _dft(n, rows, cols, sign=-1.0):
    k = np.arange(rows, dtype=np.float64)[:, None]
    m = np.arange(cols, dtype=np.float64)[None, :]
    ang = sign * 2.0 * np.pi * ((k * m) % n) / n
    return np.cos(ang), np.sin(ang)


def _stack(re, im):
    return jnp.asarray(np.concatenate([re, im], axis=0).astype(np.float32)).astype(BF16)


def _fft_tables(n_seq):
    n = 2 * n_seq
    n1 = n // FFT_N2
    f1r, f1i = _dft(n1, n1, n1 // 2)
    h1r, h1i = _dft(n1, n1 // 2, n1, sign=1.0)
    k1 = np.arange(n1, dtype=np.float64)[:, None, None]
    k2 = np.arange(FFT_N2, dtype=np.float64)[None, :, None]
    j2 = np.arange(FFT_N2, dtype=np.float64)[None, None, :]
    ang = -2.0 * np.pi * (((k2 * j2 * n1) + k1 * j2) % n) / n
    gr, gi = np.cos(ang), np.sin(ang)
    g_fwd = np.concatenate([gr, gi], axis=1)
    g_inv = np.concatenate([np.swapaxes(gr, 1, 2), np.swapaxes(gi, 1, 2)], axis=1)
    return {'f1': _stack(f1r, f1i), 'h1': _stack(h1r, h1i),
            'g_fwd': jnp.asarray(g_fwd.astype(np.float32)).astype(BF16),
            'g_inv': jnp.asarray(g_inv.astype(np.float32)).astype(BF16), 'n1': n1}


def _dense_tables(n_seq):
    n = 2 * n_seq
    fr, fi = _dft(n, n, n_seq)
    hr, hi = _dft(n, n_seq, n, sign=1.0)
    return {'f': _stack(fr, fi), 'h': _stack(hr, hi)}


def _fft_a_kernel(f_ref, zr_ref, zi_ref, ar_ref, ai_ref, *, n1, cplx):
    f = f_ref[...]
    p = jnp.dot(f, zr_ref[0].astype(BF16), preferred_element_type=F32)
    if cplx:
        q = jnp.dot(f, zi_ref[0].astype(BF16), preferred_element_type=F32)
        ar_ref[0] = p[0:n1] - q[n1:]
        ai_ref[0] = p[n1:] + q[0:n1]
    else:
        ar_ref[0] = p[0:n1]
        ai_ref[0] = p[n1:]


def _fft_a(u_flat, tabs, cplx):
    s, half, w = u_flat.shape
    n1 = tabs['n1']
    assert half == n1 // 2
    sp = s // 2 if cplx else s
    tl = min(w, 8192)
    zi_map = (lambda si, j: (si + sp, 0, j)) if cplx else (lambda si, j: (si, 0, j))
    out = pl.BlockSpec((1, n1, tl), lambda si, j: (si, 0, j))
    return pl.pallas_call(
        functools.partial(_fft_a_kernel, n1=n1, cplx=cplx),
        grid=(sp, w // tl),
        in_specs=[pl.BlockSpec((2 * n1, half), lambda si, j: (0, 0)),
                  pl.BlockSpec((1, half, tl), lambda si, j: (si, 0, j)),
                  pl.BlockSpec((1, half, tl), zi_map)],
        out_specs=[out, out],
        out_shape=[jax.ShapeDtypeStruct((sp, n1, w), F32)] * 2,
        compiler_params=_cparams("parallel", "parallel"),
        name="fft_stage1",
    )(tabs['f1'], u_flat, u_flat)


def _cplx_mm(s, xr, xi, conj):
    p = jnp.dot(s, xr.astype(BF16), preferred_element_type=F32)
    q = jnp.dot(s, xi.astype(BF16), preferred_element_type=F32)
    m = s.shape[0] // 2
    if conj:
        return p[0:m] + q[m:], q[0:m] - p[m:]
    return p[0:m] - q[m:], p[m:] + q[0:m]


def _fft_b_kernel(gf_ref, gi_ref, ar_ref, ai_ref, kr_ref, ki_ref, dr_ref, di_ref):
    cr, ci = _cplx_mm(gf_ref[0], ar_ref[0, 0], ai_ref[0, 0], False)
    kr = kr_ref[0]
    ki = ki_ref[0]
    er = cr * kr - ci * ki
    ei = cr * ki + ci * kr
    dr, di = _cplx_mm(gi_ref[0], er, ei, True)
    dr_ref[0, 0] = dr
    di_ref[0, 0] = di


def _fft_b(ar, ai, kr, ki, tabs, d):
    sp, n1, w = ar.shape
    ar4 = ar.reshape(sp, n1, FFT_N2, d)
    ai4 = ai.reshape(sp, n1, FFT_N2, d)
    blk = pl.BlockSpec((1, 1, FFT_N2, d), lambda k1, si: (si, k1, 0, 0))
    tab = pl.BlockSpec((1, 2 * FFT_N2, FFT_N2), lambda k1, si: (k1, 0, 0))
    kb = pl.BlockSpec((1, FFT_N2, d), lambda k1, si: (k1, 0, 0))
    dr, di = pl.pallas_call(
        _fft_b_kernel,
        grid=(n1, sp),
        in_specs=[tab, tab, blk, blk, kb, kb],
        out_specs=[blk, blk],
        out_shape=[jax.ShapeDtypeStruct((sp, n1, FFT_N2, d), F32)] * 2,
        compiler_params=_cparams("parallel", "arbitrary"),
        name="fft_stage2_filter",
    )(tabs['g_fwd'], tabs['g_inv'], ar4, ai4, kr, ki)
    return dr.reshape(sp, n1, w), di.reshape(sp, n1, w)


def _fft_c_kernel(h_ref, dr_ref, di_ref, u_ref, x_ref, sk_ref, o_ref):
    yr, yi = _cplx_mm(h_ref[...], dr_ref[0], di_ref[0], False)
    sk = sk_ref[...]
    o_ref[0, 0] = x_ref[0, 0] * (yr + u_ref[0, 0] * sk)
    o_ref[1, 0] = x_ref[1, 0] * (yi + u_ref[1, 0] * sk)


def _fft_c(dr, di, u_flat, x_flat, skip_t, tabs):
    sp, n1, w = dr.shape
    half = n1 // 2
    tl = skip_t.shape[-1]
    u4 = u_flat.reshape(2, sp, half, w)
    x4 = x_flat.reshape(2, sp, half, w)
    pair = pl.BlockSpec((2, 1, half, tl), lambda si, j: (0, si, 0, j))
    dblk = pl.BlockSpec((1, n1, tl), lambda si, j: (si, 0, j))
    out = pl.pallas_call(
        _fft_c_kernel,
        grid=(sp, w // tl),
        in_specs=[pl.BlockSpec((2 * half, n1), lambda si, j: (0, 0)), dblk, dblk, pair, pair,
                  pl.BlockSpec((1, tl), lambda si, j: (0, 0))],
        out_specs=pair,
        out_shape=jax.ShapeDtypeStruct((2, sp, half, w), F32),
        compiler_params=_cparams("parallel", "parallel"),
        name="fft_inverse_stage1_gate",
    )(tabs['h1'], dr, di, u4, x4, skip_t)
    return out.reshape(2 * sp, half, w)


def _spec_b_kernel(gf_ref, ar_ref, ai_ref, kr_ref, ki_ref, *, scale):
    fr, fi = _cplx_mm(gf_ref[0], ar_ref[0, 0], ai_ref[0, 0], False)
    gr, gi = _cplx_mm(gf_ref[0], ar_ref[1, 0], ai_ref[1, 0], False)
    kr_ref[0, 0] = (fr + gr) * scale
    ki_ref[0, 0] = (fi - gi) * scale


def _filter_spectrum_fft(filt, tabs, d):
    nq, n, _ = filt.shape
    n1 = tabs['n1']
    ar, ai = _fft_a(filt.reshape(nq, n1 // 2, FFT_N2 * d), tabs, cplx=False)
    orders = nq // 2
    ar5 = ar.reshape(orders, 2, n1, FFT_N2, d)
    ai5 = ai.reshape(orders, 2, n1, FFT_N2, d)
    blk = pl.BlockSpec((None, 2, 1, FFT_N2, d), lambda o, k1: (o, 0, k1, 0, 0))
    out = pl.BlockSpec((1, 1, FFT_N2, d), lambda o, k1: (o, k1, 0, 0))
    return pl.pallas_call(
        functools.partial(_spec_b_kernel, scale=1.0 / (2 * n)),
        grid=(orders, n1),
        in_specs=[pl.BlockSpec((1, 2 * FFT_N2, FFT_N2), lambda o, k1: (k1, 0, 0)), blk, blk],
        out_specs=[out, out],
        out_shape=[jax.ShapeDtypeStruct((orders, n1, FFT_N2, d), F32)] * 2,
        compiler_params=_cparams("parallel", "parallel"),
        name="filter_spectrum",
    )(tabs['g_fwd'], ar5, ai5)


def _long_conv_fft(u, x, skip, kr, ki, tabs):
    b, n, d = u.shape
    n1 = tabs['n1']
    w = FFT_N2 * d
    u_flat = u.reshape(b, n1 // 2, w)
    ar, ai = _fft_a(u_flat, tabs, cplx=True)
    dr, di = _fft_b(ar, ai, kr, ki, tabs, d)
    tl = min(w, 8192)
    skip_t = jnp.tile(skip.reshape(1, d), (1, tl // d))
    return _fft_c(dr, di, u_flat, x.reshape(b, n1 // 2, w), skip_t, tabs).reshape(b, n, d)


def _dense_spec_kernel(f_ref, filt_ref, kr_ref, ki_ref, *, scale):
    f = f_ref[...]
    n = f.shape[0] // 2
    pf = jnp.dot(f, filt_ref[0, 0].astype(BF16), preferred_element_type=F32)
    pg = jnp.dot(f, filt_ref[0, 1].astype(BF16), preferred_element_type=F32)
    kr_ref[0] = (pf[0:n] + pg[0:n]) * scale
    ki_ref[0] = (pf[n:] - pg[n:]) * scale


def _filter_spectrum_dense(filt, tabs, d):
    nq, n, _ = filt.shape
    orders = nq // 2
    f4 = filt.reshape(orders, 2, n, d)
    out = pl.BlockSpec((1, 2 * n, d), lambda o: (o, 0, 0))
    return pl.pallas_call(
        functools.partial(_dense_spec_kernel, scale=1.0 / (2 * n)),
        grid=(orders,),
        in_specs=[pl.BlockSpec((4 * n, n), lambda o: (0, 0)), pl.BlockSpec((1, 2, n, d), lambda o: (o, 0, 0, 0))],
        out_specs=[out, out],
        out_shape=[jax.ShapeDtypeStruct((orders, 2 * n, d), F32)] * 2,
        compiler_params=_cparams("parallel"),
        name="filter_spectrum_dense",
    )(tabs['f'], f4)


def _dense_conv_kernel(f_ref, h_ref, u_ref, x_ref, kr_ref, ki_ref, sk_ref, o_ref):
    cr, ci = _cplx_mm(f_ref[...], u_ref[0, 0], u_ref[1, 0], False)
    kr = kr_ref[...]
    ki = ki_ref[...]
    yr, yi = _cplx_mm(h_ref[...], cr * kr - ci * ki, cr * ki + ci * kr, False)
    sk = sk_ref[...]
    o_ref[0, 0] = x_ref[0, 0] * (yr + u_ref[0, 0] * sk)
    o_ref[1, 0] = x_ref[1, 0] * (yi + u_ref[1, 0] * sk)


def _long_conv_dense(u, x, skip, kr, ki, tabs):
    b, n, d = u.shape
    sp = b // 2
    pair = pl.BlockSpec((2, 1, n, d), lambda si: (0, si, 0, 0))
    kb = pl.BlockSpec((2 * n, d), lambda si: (0, 0))
    out = pl.pallas_call(
        _dense_conv_kernel,
        grid=(sp,),
        in_specs=[pl.BlockSpec((4 * n, n), lambda si: (0, 0)), pl.BlockSpec((2 * n, 2 * n), lambda si: (0, 0)),
                  pair, pair, kb, kb, pl.BlockSpec((1, d), lambda si: (0, 0))],
        out_specs=pair,
        out_shape=jax.ShapeDtypeStruct((2, sp, n, d), F32),
        compiler_params=_cparams("parallel"),
        name="long_conv_dense",
    )(tabs['f'], tabs['h'], u.reshape(2, sp, n, d), x.reshape(2, sp, n, d), kr, ki, skip.reshape(1, d))
    return out.reshape(b, n, d)


def _hyena_mixer(x, g, mod, op, fft_tabs, dense_tabs):
    b, n, d = x.shape
    z = _proj_conv(x, g, mod, op['w_in'], op['b_in'], op['conv_w'], op['conv_b'])
    filt = _hyena_filters(n, op, d)
    if n <= DENSE_FFT_MAX:
        kr, ki = _filter_spectrum_dense(filt, dense_tabs, d)
        y = _long_conv_dense(z[0], z[1], op['skip'][0], kr[0], ki[0], dense_tabs)
        y = _long_conv_dense(y, z[2], op['skip'][1], kr[1], ki[1], dense_tabs)
    else:
        kr, ki = _filter_spectrum_fft(filt, fft_tabs, d)
        y = _long_conv_fft(z[0], z[1], op['skip'][0], kr[0], ki[0], fft_tabs)
        y = _long_conv_fft(y, z[2], op['skip'][1], kr[1], ki[1], fft_tabs)
    return _out_res(y, x, mod, op['w_out'], op['b_out'])


def _even_mixer(x, ctx, g, mod_l, mod_c, ep, tabs, need_ctx):
    p_lat = _proj(x, g, mod_l, ep['w_in'], 0)
    p_ctx = _proj(ctx, g, mod_c, ep['w_in'], 0)
    rc, vc, kkc, gc, lwc, ac, kdc, qc, kac, vac = _even_prep(p_ctx, ep, tabs, rope=False)
    rl, vl, kkl, gl, lwl, al, kdl, ql, kal, val = _even_prep(p_lat, ep, tabs, rope=True)
    b = x.shape[0]
    c = rl.shape[-1]
    h0 = jnp.zeros((2, b, HEAD_DIM, c), F32)
    y_ctx, s_ctx = _rwkv(rc, vc, kkc, lwc, ac, kdc, h0)
    y_lat, _ = _rwkv(rl, vl, kkl, lwl, al, kdl, s_ctx)
    b_lat = _attention(ql, kal, val, kac, vac, ep['sink'], local=True)
    x_new = _even_out(y_lat, rl, vl, gl, kdl, b_lat, x, mod_l, ep, tabs)
    if not need_ctx:
        return x_new, None
    b_ctx = _attention(qc, kac, vac, kac, vac, ep['sink'], local=False)
    ctx_new = _even_out(y_ctx, rc, vc, gc, kdc, b_ctx, ctx, mod_c, ep, tabs)
    return x_new, ctx_new


def _rope_tables(n_tokens):
    rows = n_tokens // GRID_W
    row = jnp.repeat(jnp.arange(rows), GRID_W).astype(F32)
    col = jnp.tile(jnp.arange(GRID_W), rows).astype(F32)
    n_freq = HEAD_DIM // 4
    inv = ROPE_THETA ** (-jnp.arange(n_freq, dtype=F32) / n_freq)
    ang = jnp.concatenate([row[:, None] * inv, col[:, None] * inv], axis=-1)
    cos, sin = jnp.cos(ang), jnp.sin(ang)
    reps = LANES // HEAD_DIM
    cos_t = jnp.tile(jnp.concatenate([cos, cos], axis=-1), (1, reps))
    sin_t = jnp.tile(jnp.concatenate([-sin, sin], axis=-1), (1, reps))
    return cos_t, sin_t


def _block_ones(width):
    idx = np.arange(width) // HEAD_DIM
    return jnp.asarray((idx[:, None] == idx[None, :]).astype(np.float32)).astype(BF16)


def _lora_pad(w):
    z = jnp.zeros_like(w[0])
    return jnp.stack([jnp.concatenate([w[0], z], axis=0), jnp.concatenate([z, w[1]], axis=0)], axis=0)


def kernel(x, c, ctx, c_ctx, ada_w, ada_b, norm1_g, norm2_g, ffn_up, ffn_conv_w, ffn_conv_b, ffn_down, ev_w_in, ev_mu_prev, ev_mu_next, ev_w0, ev_w2, ev_a0, ev_a2, ev_g2, ev_k_k, ev_k_a, ev_r_k, ev_ln_w, ev_ln_b, ev_q_norm, ev_k_norm, ev_sink, ev_w_out, od_w_in, od_b_in, od_conv_w, od_conv_b, od_f_w1, od_f_b1, od_f_w2, od_f_b2, od_f_w3, od_f_b3, od_f_freq, od_f_out, od_skip, od_w_out, od_b_out):
    bsz, seq, d = x.shape
    lc = ctx.shape[1]
    depth = ada_w.shape[0]
    a_width = ev_k_k.shape[-1]
    nq = ev_sink.shape[-1] * HEAD_DIM
    nk = B_KV_HEADS * HEAD_DIM

    cond = jnp.zeros((BF16_ROWS, d), F32).at[:bsz].set(c).at[bsz].set(c_ctx)
    mod = _ada_mod(cond, ada_w, ada_b)

    cos_t, sin_t = _rope_tables(seq)
    tabs = {'cos': cos_t, 'sin': sin_t, 'e': _block_ones(nq)}
    fft_tabs = _fft_tables(seq) if seq > DENSE_FFT_MAX else None
    dense_lat = _dense_tables(seq) if seq <= DENSE_FFT_MAX else None
    fft_ctx = _fft_tables(lc) if lc > DENSE_FFT_MAX else None
    dense_ctx = _dense_tables(lc) if lc <= DENSE_FFT_MAX else None

    for layer in range(depth):
        need_ctx = layer < depth - 1
        even = layer % 2 == 0
        j = layer // 2
        mod_l = mod[layer, :bsz].reshape(bsz, 1, 6 * d)
        mod_c = jnp.broadcast_to(mod[layer, bsz].reshape(1, 1, 6 * d), (bsz, 1, 6 * d))
        if even:
            ep = {'w_in': ev_w_in[j].astype(BF16), 'mu_prev': ev_mu_prev[j][None], 'mu_next': ev_mu_next[j][None],
                  'w0': ev_w0[j], 'w2pad': _lora_pad(ev_w2[j]).astype(BF16), 'a0': ev_a0[j],
                  'a2pad': _lora_pad(ev_a2[j]).astype(BF16), 'g2': ev_g2[j].astype(BF16),
                  'k_k': ev_k_k[j][None], 'k_a': ev_k_a[j][None], 'r_k': ev_r_k[j].reshape(1, a_width),
                  'ln_w': ev_ln_w[j][None], 'ln_b': ev_ln_b[j][None],
                  'q_norm_t': jnp.tile(ev_q_norm[j], nq // HEAD_DIM)[None],
                  'k_norm_t': jnp.tile(ev_k_norm[j], nk // HEAD_DIM)[None],
                  'sink': ev_sink[j], 'w_out': ev_w_out[j].astype(BF16), 'nq': nq, 'nk': nk}
            x, ctx_new = _even_mixer(x, ctx, norm1_g[layer], mod_l, mod_c, ep, tabs, need_ctx)
        else:
            emb_pad = HEAD_DIM
            op = {'w_in': od_w_in[j].astype(BF16), 'b_in': od_b_in[j], 'conv_w': od_conv_w[j],
                  'conv_b': od_conv_b[j],
                  'f_w1': jnp.pad(od_f_w1[j], ((0, emb_pad - HY_EMB), (0, 0))), 'f_b1': od_f_b1[j][None],
                  'f_w2': od_f_w2[j], 'f_b2': od_f_b2[j][None], 'f_w3': od_f_w3[j], 'f_b3': od_f_b3[j][None],
                  'f_freq': od_f_freq[j][None], 'f_out': od_f_out[j], 'skip': od_skip[j],
                  'w_out': od_w_out[j].astype(BF16), 'b_out': od_b_out[j]}
            ctx_new = _hyena_mixer(ctx, norm1_g[layer], mod_c, op, fft_ctx, dense_ctx) if need_ctx else None
            x = _hyena_mixer(x, norm1_g[layer], mod_l, op, fft_tabs, dense_lat)
        w_up = ffn_up[layer].astype(BF16)
        w_down = ffn_down[layer].astype(BF16)
        x = _conv_ffn(x, norm2_g[layer], mod_l, w_up, ffn_conv_w[layer], ffn_conv_b[layer], w_down)
        if need_ctx:
            ctx = _conv_ffn(ctx_new, norm2_g[layer], mod_c, w_up, ffn_conv_w[layer], ffn_conv_b[layer], w_down)
    return x
```

```python
import functools
import math

import numpy as np
import jax
import jax.numpy as jnp
from jax import lax
from jax.experimental import pallas as pl
from jax.experimental.pallas import tpu as pltpu

F32 = jnp.float32
BF16 = jnp.bfloat16

HEAD_DIM = 64
GRID_W = 64
DECAY_LORA = 64
ICLR_LORA = 64
GATE_LORA = 128
RWKV_GN_EPS = 64e-5
B_KV_HEADS = 2
WINDOW = 128
BLOCK = 128
ROPE_THETA = 10000.0
MASK_VALUE = -1e30
HY_ORDER = 2
HY_EMB = 33
HY_BANDS = (HY_EMB - 1) // 2
HY_TARGET = 1e-2
HY_FAST_PCT = 0.3
HY_SLOW_PCT = 1.5
HY_MOD_SHIFT = 0.05
NORM_EPS = 1e-6

V7X_VMEM_BYTES = 64 * 1024 * 1024
VMEM_LIMIT_BYTES = V7X_VMEM_BYTES * 3 // 4
LANES = 128
BF16_ROWS = 16
RWKV_CHUNK = 64
RWKV_PREP_CHUNKS = 4
RWKV_INV_BASE = 8
RWKV_GRAM_PASSES = 2
RWKV_INV_PASSES = 2
RWKV_REST_PASSES = 1
RWKV_SCAN_PASSES = 2
FFT_N2 = 128
DENSE_FFT_MAX = 512

HIGHEST = lax.Precision.HIGHEST


def _cparams(*sem):
    return pltpu.CompilerParams(dimension_semantics=sem, vmem_limit_bytes=VMEM_LIMIT_BYTES)


def _dot(a, b):
    return jnp.dot(a.astype(BF16), b.astype(BF16), preferred_element_type=F32)


def _dot_nt(a, b):
    return lax.dot_general(a.astype(BF16), b.astype(BF16), (((1,), (1,)), ((), ())), preferred_element_type=F32)


def _dot_tn(a, b):
    return lax.dot_general(a.astype(BF16), b.astype(BF16), (((0,), (0,)), ((), ())), preferred_element_type=F32)


def _dot_hi(a, b):
    return jnp.dot(a, b, preferred_element_type=F32, precision=HIGHEST)


def _dot_nt_hi(a, b):
    return lax.dot_general(a, b, (((1,), (1,)), ((), ())), preferred_element_type=F32, precision=HIGHEST)


def _dot_tn_hi(a, b):
    return lax.dot_general(a, b, (((0,), (0,)), ((), ())), preferred_element_type=F32, precision=HIGHEST)


def _segsum(x, e_ref):
    hi = x.astype(BF16)
    lo = (x - hi.astype(F32)).astype(BF16)
    e = e_ref[...]
    return jnp.dot(hi, e, preferred_element_type=F32) + jnp.dot(lo, e, preferred_element_type=F32)


def _norm_mod(x, g, shift, scale):
    xn = x * lax.rsqrt(jnp.mean(x * x, axis=-1, keepdims=True) + NORM_EPS)
    return (xn * g) * (1.0 + scale) + shift


def _row_tile(t, want):
    tm = min(t, want)
    assert t % tm == 0 and tm % BF16_ROWS == 0
    return tm


def _ada_kernel(c_ref, w_ref, b_ref, o_ref):
    c = c_ref[...]
    s = c * jax.nn.sigmoid(c)
    o_ref[0] = _dot(s, w_ref[0]) + b_ref[0]


def _ada_mod(cond, ada_w, ada_b):
    depth, d, n = ada_w.shape
    tn = n // 4
    rows = cond.shape[0]
    return pl.pallas_call(
        _ada_kernel,
        grid=(depth, n // tn),
        in_specs=[pl.BlockSpec((rows, d), lambda l, j: (0, 0)),
                  pl.BlockSpec((1, d, tn), lambda l, j: (l, 0, j)),
                  pl.BlockSpec((1, 1, tn), lambda l, j: (l, 0, j))],
        out_specs=pl.BlockSpec((1, rows, tn), lambda l, j: (l, 0, j)),
        out_shape=jax.ShapeDtypeStruct((depth, rows, n), F32),
        compiler_params=_cparams("parallel", "parallel"),
        name="ada_mod",
    )(cond, ada_w, ada_b.reshape(depth, 1, n))


def _proj_kernel(x_ref, g_ref, mod_ref, w_ref, o_ref, h_scr, *, d, slot):
    @pl.when(pl.program_id(2) == 0)
    def _():
        m = mod_ref[0]
        h = _norm_mod(x_ref[0], g_ref[...], m[:, slot * d:(slot + 1) * d], m[:, (slot + 1) * d:(slot + 2) * d])
        h_scr[...] = h.astype(BF16)

    o_ref[0] = jnp.dot(h_scr[...], w_ref[...], preferred_element_type=F32)


def _proj(x, g, mod, w, slot):
    b, t, d = x.shape
    n = w.shape[1]
    tm = _row_tile(t, 512)
    tn = 128 * math.gcd(n // 128, 7 if n % 896 == 0 else 4)
    return pl.pallas_call(
        functools.partial(_proj_kernel, d=d, slot=slot),
        grid=(b, t // tm, n // tn),
        in_specs=[pl.BlockSpec((1, tm, d), lambda bi, i, j: (bi, i, 0)),
                  pl.BlockSpec((1, d), lambda bi, i, j: (0, 0)),
                  pl.BlockSpec((1, 1, mod.shape[-1]), lambda bi, i, j: (bi, 0, 0)),
                  pl.BlockSpec((d, tn), lambda bi, i, j: (0, j))],
        out_specs=pl.BlockSpec((1, tm, tn), lambda bi, i, j: (bi, i, j)),
        out_shape=jax.ShapeDtypeStruct((b, t, n), F32),
        scratch_shapes=[pltpu.VMEM((tm, d), BF16)],
        compiler_params=_cparams("parallel", "parallel", "arbitrary"),
        name="norm_mod_proj",
    )(x, g.reshape(1, d), mod, w)


def _halo_specs(tm, t, d, nargs):
    r = tm // BF16_ROWS
    last = t // BF16_ROWS - 1
    if nargs == 3:
        prev = pl.BlockSpec((1, BF16_ROWS, d), lambda bi, i, j: (bi, jnp.maximum(i * r - 1, 0), 0))
        nxt = pl.BlockSpec((1, BF16_ROWS, d), lambda bi, i, j: (bi, jnp.minimum((i + 1) * r, last), 0))
    else:
        prev = pl.BlockSpec((1, BF16_ROWS, d), lambda bi, i: (bi, jnp.maximum(i * r - 1, 0), 0))
        nxt = pl.BlockSpec((1, BF16_ROWS, d), lambda bi, i: (bi, jnp.minimum((i + 1) * r, last), 0))
    return prev, nxt


def _fill_h(x_ref, xp_ref, xn_ref, g, shift, scale, h_scr, tm):
    h_scr[0:BF16_ROWS] = _norm_mod(xp_ref[0], g, shift, scale).astype(BF16)
    h_scr[BF16_ROWS:BF16_ROWS + tm] = _norm_mod(x_ref[0], g, shift, scale).astype(BF16)
    h_scr[BF16_ROWS + tm:2 * BF16_ROWS + tm] = _norm_mod(xn_ref[0], g, shift, scale).astype(BF16)


def _conv3_rows(u, u_scr, cw_ref, cb_ref, tm, t_total):
    i = pl.program_id(1)
    rows = lax.broadcasted_iota(jnp.int32, (tm + 2 * BF16_ROWS, 1), 0) + (i * tm - BF16_ROWS)
    u_scr[...] = jnp.where((rows >= 0) & (rows < t_total), u, 0.0)
    cw = cw_ref[...]
    o = BF16_ROWS
    return (u_scr[o - 1:o - 1 + tm] * cw[0:1] + u_scr[o:o + tm] * cw[1:2] + u_scr[o + 1:o + 1 + tm] * cw[2:3]
            + cb_ref[...])


def _proj_conv_kernel(x_ref, xp_ref, xn_ref, g_ref, mod_ref, w_ref, b_ref, cw_ref, cb_ref, o_ref, h_scr, u_scr,
                      *, d, tm, t_total):
    @pl.when(pl.program_id(2) == 0)
    def _():
        m = mod_ref[0]
        _fill_h(x_ref, xp_ref, xn_ref, g_ref[...], m[:, 0:d], m[:, d:2 * d], h_scr, tm)

    u = jnp.dot(h_scr[...], w_ref[...], preferred_element_type=F32) + b_ref[...]
    o_ref[0, 0] = _conv3_rows(u, u_scr, cw_ref, cb_ref, tm, t_total)


def _proj_conv(x, g, mod, w, bias, cw, cb):
    b, t, d = x.shape
    n = w.shape[1]
    tm = _row_tile(t, 512)
    tn = d
    prev, nxt = _halo_specs(tm, t, d, 3)
    return pl.pallas_call(
        functools.partial(_proj_conv_kernel, d=d, tm=tm, t_total=t),
        grid=(b, t // tm, n // tn),
        in_specs=[pl.BlockSpec((1, tm, d), lambda bi, i, j: (bi, i, 0)), prev, nxt,
                  pl.BlockSpec((1, d), lambda bi, i, j: (0, 0)),
                  pl.BlockSpec((1, 1, mod.shape[-1]), lambda bi, i, j: (bi, 0, 0)),
                  pl.BlockSpec((d, tn), lambda bi, i, j: (0, j)),
                  pl.BlockSpec((1, tn), lambda bi, i, j: (0, j)),
                  pl.BlockSpec((3, tn), lambda bi, i, j: (0, j)),
                  pl.BlockSpec((1, tn), lambda bi, i, j: (0, j))],
        out_specs=pl.BlockSpec((1, 1, tm, tn), lambda bi, i, j: (j, bi, i, 0)),
        out_shape=jax.ShapeDtypeStruct((n // tn, b, t, tn), F32),
        scratch_shapes=[pltpu.VMEM((tm + 2 * BF16_ROWS, d), BF16), pltpu.VMEM((tm + 2 * BF16_ROWS, tn), F32)],
        compiler_params=_cparams("parallel", "parallel", "arbitrary"),
        name="hyena_in_proj",
    )(x, x, x, g.reshape(1, d), mod, w, bias.reshape(1, n), cw, cb.reshape(1, n))


def _ffn_kernel(x_ref, xp_ref, xn_ref, g_ref, mod_ref, wg_ref, wv_ref, cwg_ref, cwv_ref, cbg_ref, cbv_ref, wd_ref,
                o_ref, h_scr, ug_scr, uv_scr, acc_scr, *, d, tm, t_total):
    j = pl.program_id(2)

    @pl.when(j == 0)
    def _():
        m = mod_ref[0]
        _fill_h(x_ref, xp_ref, xn_ref, g_ref[...], m[:, 3 * d:4 * d], m[:, 4 * d:5 * d], h_scr, tm)
        acc_scr[...] = jnp.zeros_like(acc_scr)

    h = h_scr[...]
    gate = _conv3_rows(jnp.dot(h, wg_ref[...], preferred_element_type=F32), ug_scr, cwg_ref, cbg_ref, tm, t_total)
    val = _conv3_rows(jnp.dot(h, wv_ref[...], preferred_element_type=F32), uv_scr, cwv_ref, cbv_ref, tm, t_total)
    act = (gate * jax.nn.sigmoid(gate)) * val
    acc_scr[...] += jnp.dot(act.astype(BF16), wd_ref[...], preferred_element_type=F32)

    @pl.when(j == pl.num_programs(2) - 1)
    def _():
        o_ref[0] = x_ref[0] + mod_ref[0][:, 5 * d:6 * d] * acc_scr[...]


def _conv_ffn(x, g, mod, w_up, cw, cb, w_down):
    b, t, d = x.shape
    f = w_down.shape[0]
    tm = _row_tile(t, 1024)
    tf = 256 if f % 256 == 0 else 128
    nf = f // tf
    prev, nxt = _halo_specs(tm, t, d, 3)
    cb2 = cb.reshape(1, 2 * f)
    return pl.pallas_call(
        functools.partial(_ffn_kernel, d=d, tm=tm, t_total=t),
        grid=(b, t // tm, nf),
        in_specs=[pl.BlockSpec((1, tm, d), lambda bi, i, j: (bi, i, 0)), prev, nxt,
                  pl.BlockSpec((1, d), lambda bi, i, j: (0, 0)),
                  pl.BlockSpec((1, 1, mod.shape[-1]), lambda bi, i, j: (bi, 0, 0)),
                  pl.BlockSpec((d, tf), lambda bi, i, j: (0, j)),
                  pl.BlockSpec((d, tf), lambda bi, i, j: (0, nf + j)),
                  pl.BlockSpec((3, tf), lambda bi, i, j: (0, j)),
                  pl.BlockSpec((3, tf), lambda bi, i, j: (0, nf + j)),
                  pl.BlockSpec((1, tf), lambda bi, i, j: (0, j)),
                  pl.BlockSpec((1, tf), lambda bi, i, j: (0, nf + j)),
                  pl.BlockSpec((tf, d), lambda bi, i, j: (j, 0))],
        out_specs=pl.BlockSpec((1, tm, d), lambda bi, i, j: (bi, i, 0)),
        out_shape=jax.ShapeDtypeStruct((b, t, d), F32),
        scratch_shapes=[pltpu.VMEM((tm + 2 * BF16_ROWS, d), BF16),
                        pltpu.VMEM((tm + 2 * BF16_ROWS, tf), F32),
                        pltpu.VMEM((tm + 2 * BF16_ROWS, tf), F32),
                        pltpu.VMEM((tm, d), F32)],
        compiler_params=_cparams("parallel", "parallel", "arbitrary"),
        name="conv_ffn",
    )(x, x, x, g.reshape(1, d), mod, w_up, w_up, cw, cw, cb2, cb2, w_down)


def _out_res_kernel(y_ref, x_ref, mod_ref, w_ref, b_ref, o_ref, *, d):
    o = _dot(y_ref[0], w_ref[...]) + b_ref[...]
    o_ref[0] = x_ref[0] + mod_ref[0][:, 2 * d:3 * d] * o


def _out_res(y, x, mod, w, bias):
    b, t, d = x.shape
    tm = _row_tile(t, 512)
    return pl.pallas_call(
        functools.partial(_out_res_kernel, d=d),
        grid=(b, t // tm),
        in_specs=[pl.BlockSpec((1, tm, d), lambda bi, i: (bi, i, 0)),
                  pl.BlockSpec((1, tm, d), lambda bi, i: (bi, i, 0)),
                  pl.BlockSpec((1, 1, mod.shape[-1]), lambda bi, i: (bi, 0, 0)),
                  pl.BlockSpec((d, d), lambda bi, i: (0, 0)),
                  pl.BlockSpec((1, d), lambda bi, i: (0, 0))],
        out_specs=pl.BlockSpec((1, tm, d), lambda bi, i: (bi, i, 0)),
        out_shape=jax.ShapeDtypeStruct((b, t, d), F32),
        compiler_params=_cparams("parallel", "parallel"),
        name="out_proj_residual",
    )(y, x, mod, w, bias.reshape(1, d))


def _softplus(z):
    return jnp.maximum(z, 0.0) + jnp.log1p(jnp.exp(-jnp.abs(z)))


def _even_prep_kernel(p_ref, pp_ref, pn_ref, mup_ref, mun_ref, w0_ref, w2_ref, a0_ref, a2_ref, g2_ref, kk_ref,
                      ka_ref, qn_ref, kn_ref, cos_ref, sin_ref, e_ref,
                      r_ref, v_ref, kkn_ref, g_ref, lw_ref, a_ref, kd_ref, q_ref, katt_ref, vatt_ref,
                      *, tm, c, a_in, rope):
    i = pl.program_id(1)
    nt = pl.num_programs(1)
    pa = p_ref[0][:, 0:a_in]
    rows = lax.broadcasted_iota(jnp.int32, (tm, 1), 0)
    prev_row = jnp.where(i > 0, pp_ref[0][BF16_ROWS - 1:BF16_ROWS, 0:a_in], 0.0)
    next_row = jnp.where(i < nt - 1, pn_ref[0][0:1, 0:a_in], 0.0)
    prev = jnp.where(rows == 0, prev_row, pltpu.roll(pa, 1, 0))
    nxt = jnp.where(rows == tm - 1, next_row, pltpu.roll(pa, tm - 1, 0))
    za = pa + mup_ref[...] * (prev - pa) + mun_ref[...] * (nxt - pa)

    r = za[:, 0:c]
    k = za[:, c:2 * c]
    v = za[:, 2 * c:3 * c]
    wd = jnp.tanh(za[:, 3 * c:3 * c + 2 * DECAY_LORA])
    ad = za[:, 3 * c + 2 * DECAY_LORA:3 * c + 2 * DECAY_LORA + 2 * ICLR_LORA]
    gd = za[:, 3 * c + 2 * DECAY_LORA + 2 * ICLR_LORA:a_in]
    r_ref[0] = r
    v_ref[0] = v
    g_ref[0] = _dot(jax.nn.sigmoid(gd), g2_ref[...])
    kkv = k * kk_ref[...]
    ss = _segsum(kkv * kkv, e_ref)
    kkn_ref[0] = kkv / jnp.maximum(jnp.sqrt(ss), 1e-12)
    for dd in range(2):
        w_log = -_softplus(-(w0_ref[dd:dd + 1] + _dot(wd, w2_ref[dd]))) - 0.5
        lw_ref[dd, 0] = -jnp.exp(w_log)
        a = jax.nn.sigmoid(a0_ref[dd:dd + 1] + _dot(ad, a2_ref[dd]))
        a_ref[dd, 0] = a
        kd_ref[dd, 0] = k * (1.0 + (a - 1.0) * ka_ref[...])

    pb = p_ref[0][:, a_in:]
    nq = q_ref.shape[-1]
    nk = katt_ref.shape[-1]
    q = pb[:, 0:nq]
    kat = pb[:, nq:nq + nk]
    vatt_ref[0] = pb[:, nq + nk:]
    half = HEAD_DIM // 2

    def norm_rope(x, gain, e):
        w = x.shape[-1]
        ms = _segsum(x * x, e) * (1.0 / HEAD_DIM)
        xn = x * lax.rsqrt(ms + NORM_EPS) * gain
        if not rope:
            return xn
        lane = lax.broadcasted_iota(jnp.int32, (1, w), 1)
        first = (lane % HEAD_DIM) < half
        swapped = jnp.where(first, pltpu.roll(xn, w - half, 1), pltpu.roll(xn, half, 1))
        reps = w // LANES
        cos = jnp.concatenate([cos_ref[...]] * reps, axis=1) if reps > 1 else cos_ref[...]
        sin = jnp.concatenate([sin_ref[...]] * reps, axis=1) if reps > 1 else sin_ref[...]
        return xn * cos + swapped * sin

    q_ref[0] = norm_rope(q, qn_ref[...], e_ref[...])
    katt_ref[0] = norm_rope(kat, kn_ref[...], e_ref[0:nk, 0:nk])


def _even_prep(p, ep, tabs, rope):
    b, t, n = p.shape
    c = ep['k_k'].shape[-1]
    a_in = ep['mu_prev'].shape[-1]
    nq = ep['nq']
    nk = ep['nk']
    tm = _row_tile(t, 256)
    prev, nxt = _halo_specs(tm, t, n, 2)
    full = lambda *shape: pl.BlockSpec(shape, lambda bi, i: (0,) * len(shape))
    tok = lambda w: pl.BlockSpec((1, tm, w), lambda bi, i: (bi, i, 0))
    tok2 = lambda w: pl.BlockSpec((2, 1, tm, w), lambda bi, i: (0, bi, i, 0))
    sd = lambda *shape: jax.ShapeDtypeStruct(shape, F32)
    return pl.pallas_call(
        functools.partial(_even_prep_kernel, tm=tm, c=c, a_in=a_in, rope=rope),
        grid=(b, t // tm),
        in_specs=[tok(n), prev, nxt, full(1, a_in), full(1, a_in), full(2, c), full(2, 2 * DECAY_LORA, c),
                  full(2, c), full(2, 2 * ICLR_LORA, c), full(GATE_LORA, c), full(1, c), full(1, c),
                  full(1, nq), full(1, nk),
                  pl.BlockSpec((tm, LANES), lambda bi, i: (i, 0)), pl.BlockSpec((tm, LANES), lambda bi, i: (i, 0)),
                  full(nq, nq)],
        out_specs=[tok(c), tok(c), tok(c), tok(c), tok2(c), tok2(c), tok2(c), tok(nq), tok(nk), tok(nk)],
        out_shape=[sd(b, t, c), sd(b, t, c), sd(b, t, c), sd(b, t, c), sd(2, b, t, c), sd(2, b, t, c),
                   sd(2, b, t, c), sd(b, t, nq), sd(b, t, nk), sd(b, t, nk)],
        compiler_params=_cparams("parallel", "parallel"),
        name="even_prep",
    )(p, p, p, ep['mu_prev'], ep['mu_next'], ep['w0'], ep['w2pad'], ep['a0'], ep['a2pad'], ep['g2'], ep['k_k'],
      ep['k_a'], ep['q_norm_t'], ep['k_norm_t'], tabs['cos'][:t], tabs['sin'][:t], tabs['e'])


HEADS_PER_GROUP = 4
GROUP_LANES = HEADS_PER_GROUP * HEAD_DIM


def _split_bf16(x):
    hi = x.astype(BF16)
    return hi, (x - hi.astype(F32)).astype(BF16)


def _block_diag(x, bmask):
    return jnp.where(bmask, jnp.concatenate([x] * HEADS_PER_GROUP, axis=0), jnp.zeros((), x.dtype))


def _head_mm(lhs, rhs, bmask, passes, nt=False):
    dn = (((1,), (1,)), ((), ())) if nt else (((1,), (0,)), ((), ()))
    if passes == 1:
        return lax.dot_general(lhs.astype(BF16), _block_diag(rhs.astype(BF16), bmask), dn, preferred_element_type=F32)
    lh, ll = _split_bf16(lhs)
    rh, rl = _split_bf16(rhs)
    m = lhs.shape[0]
    top = lax.dot_general(jnp.concatenate([lh, ll], axis=0), _block_diag(rh, bmask), dn, preferred_element_type=F32)
    return top[0:m] + top[m:] + lax.dot_general(lh, _block_diag(rl, bmask), dn, preferred_element_type=F32)


def _head_mm_tn(lhs, rhs, lane_head, passes):
    dn = (((0,), (0,)), ((), ()))
    if passes == 1:
        full = lax.dot_general(lhs.astype(BF16), rhs.astype(BF16), dn, preferred_element_type=F32)
    else:
        lh, ll = _split_bf16(lhs)
        rh, rl = _split_bf16(rhs)
        full = (lax.dot_general(jnp.concatenate([lh, ll], axis=0), jnp.concatenate([rh, rh], axis=0), dn,
                                preferred_element_type=F32)
                + lax.dot_general(lh, rl, dn, preferred_element_type=F32))
    out = jnp.where(lane_head == 0, full[0:HEAD_DIM], 0.0)
    for h in range(1, HEADS_PER_GROUP):
        out = out + jnp.where(lane_head == h, full[h * HEAD_DIM:(h + 1) * HEAD_DIM], 0.0)
    return out


def _group_masks():
    r = lax.broadcasted_iota(jnp.int32, (GROUP_LANES, GROUP_LANES), 0)
    c = lax.broadcasted_iota(jnp.int32, (GROUP_LANES, GROUP_LANES), 1)
    bmask = (r // HEAD_DIM) == (c // HEAD_DIM)
    lane_head = lax.broadcasted_iota(jnp.int32, (1, GROUP_LANES), 1) // HEAD_DIM
    return bmask, lane_head


def _rwkv_prep_kernel(r_ref, v_ref, kk_ref, lw_ref, a_ref, kd_ref, r2_ref, yl_ref, t_ref, z_ref, *, cs):
    sgn = 1 - 2 * pl.program_id(0)
    row = lax.broadcasted_iota(jnp.int32, (cs, cs), 0)
    col = lax.broadcasted_iota(jnp.int32, (cs, cs), 1)
    tri = (((row - col) * sgn) >= 0).astype(F32)
    bmask, lane_head = _group_masks()
    t_idx = lax.broadcasted_iota(jnp.int32, (cs, GROUP_LANES), 0)
    s_idx = lax.broadcasted_iota(jnp.int32, (cs, GROUP_LANES), 1) % HEAD_DIM
    diff = (t_idx - s_idx) * sgn
    strict = diff > 0
    incl = diff >= 0
    eye = (diff == 0).astype(F32)
    base = min(RWKV_INV_BASE, cs)
    same_base = (t_idx // base) == (s_idx // base)
    groups = r_ref.shape[-1] // GROUP_LANES

    ch = []
    for c0 in range(0, r_ref.shape[1], cs):
        rows = slice(c0, c0 + cs)
        r, v, kk = r_ref[0, rows], v_ref[0, rows], kk_ref[0, rows]
        lw, a, kd = lw_ref[0, 0, rows], a_ref[0, 0, rows], kd_ref[0, 0, rows]
        g = _dot_hi(tri, lw)
        gp = g - lw
        gref = g[cs // 2:cs // 2 + 1]
        gend = jnp.sum(lw, axis=0, keepdims=True)
        bvec = kk * a
        full = {'a_t': -kk * jnp.exp(gp - gref), 'a_0': -kk * jnp.exp(gp),
                'b_t': bvec * jnp.exp(gref - g), 'b_e': bvec * jnp.exp(gend - g),
                'k_t': kd * jnp.exp(gref - g), 'k_e': kd * jnp.exp(gend - g),
                'r_t': r * jnp.exp(g - gref), 'r_0': r * jnp.exp(g), 'v': v,
                'wend': jnp.broadcast_to(jnp.exp(gend), (cs, v.shape[-1]))}
        for gi in range(groups):
            sl = slice(gi * GROUP_LANES, (gi + 1) * GROUP_LANES)
            c = {k: x[:, sl] for k, x in full.items()}
            c['rows'], c['sl'] = rows, sl
            ch.append(c)

    mm = lambda x, y, passes, nt=False: [_head_mm(p_, q_, bmask, passes, nt) for p_, q_ in zip(x, y)]
    get = lambda k: [c[k] for c in ch]
    ar = [jnp.concatenate([c['a_t'], c['r_t']], axis=0) for c in ch]
    gb = mm(ar, get('b_t'), RWKV_GRAM_PASSES, True)
    gk = mm(ar, get('k_t'), RWKV_GRAM_PASSES, True)
    nmat = [jnp.where(strict, x[0:cs], 0.0) for x in gb]
    g_ak = [jnp.where(strict, x[0:cs], 0.0) for x in gk]
    g_rb = [jnp.where(incl, x[cs:], 0.0) for x in gb]
    g_rk = [jnp.where(incl, x[cs:], 0.0) for x in gk]
    npow = [jnp.where(same_base, x, 0.0) for x in nmat]
    p = [eye + x for x in npow]
    for _ in range(int(math.log2(base)) - 1):
        npow = mm(npow, npow, RWKV_INV_PASSES)
        p = [x + y for x, y in zip(p, mm(npow, p, RWKV_INV_PASSES))]
    m = base
    while m < cs:
        off = ((t_idx // m) != (s_idx // m)) & ((t_idx // (2 * m)) == (s_idx // (2 * m)))
        q = mm([jnp.where(off, x, 0.0) for x in nmat], p, RWKV_INV_PASSES)
        p = [x + y for x, y in zip(p, mm(p, q, RWKV_INV_PASSES))]
        m *= 2
    vg = get('v')
    a2 = mm(p, get('a_0'), RWKV_REST_PASSES)
    u_v = mm(p, mm(g_ak, vg, RWKV_REST_PASSES), RWKV_REST_PASSES)
    r2 = mm(g_rb, a2, RWKV_REST_PASSES)
    yl1 = mm(g_rb, u_v, RWKV_REST_PASSES)
    yl2 = mm(g_rk, vg, RWKV_REST_PASSES)
    tt = [_head_mm_tn(x, c['b_e'], lane_head, RWKV_REST_PASSES) for x, c in zip(a2, ch)]
    zz = [_head_mm_tn(jnp.concatenate([u, c['v']], axis=0), jnp.concatenate([c['b_e'], c['k_e']], axis=0),
                      lane_head, RWKV_REST_PASSES) for u, c in zip(u_v, ch)]
    for i, c in enumerate(ch):
        rows, sl = c['rows'], c['sl']
        r2_ref[0, 0, rows, sl] = c['r_0'] + r2[i]
        yl_ref[0, 0, rows, sl] = yl1[i] + yl2[i]
        t_ref[0, 0, rows, sl] = eye * c['wend'] + tt[i]
        z_ref[0, 0, rows, sl] = zz[i]


def _rwkv_scan_kernel(h0_ref, r2f_ref, ylf_ref, tf_ref, zf_ref, r2b_ref, ylb_ref, tb_ref, zb_ref,
                      yf_ref, yb_ref, ht_ref, s_scr):
    ci = pl.program_id(0)

    @pl.when(ci == 0)
    def _():
        s_scr[...] = h0_ref[...]

    bmask, _ = _group_masks()
    ins = ((r2f_ref, ylf_ref, tf_ref, zf_ref, yf_ref), (r2b_ref, ylb_ref, tb_ref, zb_ref, yb_ref))
    for d, (r2_ref, yl_ref, t_ref, z_ref, y_ref) in enumerate(ins):
        for b in range(s_scr.shape[1]):
            for gi in range(s_scr.shape[-1] // GROUP_LANES):
                sl = slice(gi * GROUP_LANES, (gi + 1) * GROUP_LANES)
                s = s_scr[d, b, :, sl]
                y_ref[0, b, :, sl] = yl_ref[0, b, :, sl] + _head_mm(r2_ref[0, b, :, sl], s, bmask, 1, nt=True)
                s_scr[d, b, :, sl] = (_head_mm(s, t_ref[0, b, :, sl], bmask, RWKV_SCAN_PASSES)
                                      + z_ref[0, b, :, sl])

    @pl.when(ci == pl.num_programs(0) - 1)
    def _():
        ht_ref[...] = s_scr[...]


def _rwkv(r, v, kk, lw, a, kd, h0):
    b, t, c = r.shape
    cs = min(RWKV_CHUNK, t)
    assert cs == HEAD_DIM and c % GROUP_LANES == 0
    nc = t // cs
    rows = cs * math.gcd(nc, RWKV_PREP_CHUNKS)
    tok = pl.BlockSpec((1, rows, c), lambda d, bi, ci: (bi, ci, 0))
    tok2 = pl.BlockSpec((1, 1, rows, c), lambda d, bi, ci: (d, bi, ci, 0))
    wide = jax.ShapeDtypeStruct((2, b, t, c), F32)
    r2, yl, tt, zz = pl.pallas_call(
        functools.partial(_rwkv_prep_kernel, cs=cs),
        grid=(2, b, t // rows),
        in_specs=[tok, tok, tok, tok2, tok2, tok2],
        out_specs=[tok2, tok2, tok2, tok2],
        out_shape=[wide, wide, wide, wide],
        compiler_params=_cparams("parallel", "parallel", "parallel"),
        name="rwkv7_chunk_prep",
    )(r, v, kk, lw, a, kd)

    fwd = pl.BlockSpec((1, b, cs, c), lambda ci: (0, 0, ci, 0))
    bwd = pl.BlockSpec((1, b, cs, c), lambda ci: (1, 0, nc - 1 - ci, 0))
    st = pl.BlockSpec((2, b, HEAD_DIM, c), lambda ci: (0, 0, 0, 0))
    yf, yb, ht = pl.pallas_call(
        _rwkv_scan_kernel,
        grid=(nc,),
        in_specs=[st, fwd, fwd, fwd, fwd, bwd, bwd, bwd, bwd],
        out_specs=[pl.BlockSpec((1, b, cs, c), lambda ci: (0, 0, ci, 0)),
                   pl.BlockSpec((1, b, cs, c), lambda ci: (0, 0, nc - 1 - ci, 0)), st],
        out_shape=[jax.ShapeDtypeStruct((1, b, t, c), F32), jax.ShapeDtypeStruct((1, b, t, c), F32),
                   jax.ShapeDtypeStruct((2, b, HEAD_DIM, c), F32)],
        scratch_shapes=[pltpu.VMEM((2, b, HEAD_DIM, c), F32)],
        compiler_params=_cparams("arbitrary"),
        name="rwkv7_state_scan",
    )(h0, r2, yl, tt, zz, r2, yl, tt, zz)
    return jnp.concatenate([yf, yb], axis=0), ht


def _attn_kernel(sink_ref, q_ref, kp_ref, kc_ref, kn_ref, vp_ref, vc_ref, vn_ref, kx_ref, vx_ref, o_ref,
                 *, local, group):
    i = pl.program_id(1)
    nb = pl.num_programs(1)
    scale = HEAD_DIM ** -0.5
    q = q_ref[0]
    if local:
        k_all = jnp.concatenate([kp_ref[0], kc_ref[0], kn_ref[0], kx_ref[0]], axis=0)
        v_all = jnp.concatenate([vp_ref[0], vc_ref[0], vn_ref[0], vx_ref[0]], axis=0)
        nkeys = k_all.shape[0]
        qi = lax.broadcasted_iota(jnp.int32, (BLOCK, nkeys), 0)
        kj = lax.broadcasted_iota(jnp.int32, (BLOCK, nkeys), 1)
        kpos = kj + (i - 1) * BLOCK
        valid = (jnp.abs(kj - BLOCK - qi) <= WINDOW) & (kpos >= 0) & (kpos < nb * BLOCK)
        valid = valid | (kj >= 3 * BLOCK)
    else:
        k_all = kx_ref[0]
        v_all = vx_ref[0]
    outs = []
    for h in range(q.shape[-1] // HEAD_DIM):
        hk = h // group
        kh = k_all[:, hk * HEAD_DIM:(hk + 1) * HEAD_DIM]
        vh = v_all[:, hk * HEAD_DIM:(hk + 1) * HEAD_DIM]
        s = _dot_nt(q[:, h * HEAD_DIM:(h + 1) * HEAD_DIM], kh) * scale
        if local:
            s = jnp.where(valid, s, MASK_VALUE)
        sink = sink_ref[h]
        m = jnp.maximum(jnp.max(s, axis=-1, keepdims=True), sink)
        e = jnp.exp(s - m)
        p = e / (jnp.sum(e, axis=-1, keepdims=True) + jnp.exp(sink - m))
        outs.append(_dot(p, vh))
    o_ref[0] = jnp.concatenate(outs, axis=1)


def _attention(q, k, v, kx, vx, sink, local):
    b, t, nq = q.shape
    nk = k.shape[-1]
    nb = t // BLOCK
    group = (nq // HEAD_DIM) // (nk // HEAD_DIM)
    lx = kx.shape[1]
    kv = lambda f: pl.BlockSpec((1, BLOCK, nk), f)
    pf = lambda bi, i: (bi, jnp.maximum(i - 1, 0), 0)
    cf = lambda bi, i: (bi, i, 0)
    nf = lambda bi, i: (bi, jnp.minimum(i + 1, nb - 1), 0)
    ctx = pl.BlockSpec((1, lx, nk), lambda bi, i: (bi, 0, 0))
    return pl.pallas_call(
        functools.partial(_attn_kernel, local=local, group=group),
        grid=(b, nb),
        in_specs=[pl.BlockSpec(memory_space=pltpu.SMEM),
                  pl.BlockSpec((1, BLOCK, nq), cf), kv(pf), kv(cf), kv(nf), kv(pf), kv(cf), kv(nf), ctx, ctx],
        out_specs=pl.BlockSpec((1, BLOCK, nq), cf),
        out_shape=jax.ShapeDtypeStruct((b, t, nq), F32),
        compiler_params=_cparams("parallel", "parallel"),
        name="window_attention" if local else "context_attention",
    )(sink, q, k, k, k, v, v, v, kx, vx)


def _even_out_kernel(y_ref, r_ref, v_ref, g_ref, kd_ref, batt_ref, x_ref, mod_ref, lnw_ref, lnb_ref, rk_ref, e_ref,
                     w_ref, o_ref, *, d, c):
    y = y_ref[0, 0] + y_ref[1, 0]
    inv = 1.0 / HEAD_DIM
    mu = _segsum(y, e_ref) * inv
    yc = y - mu
    var = _segsum(yc * yc, e_ref) * inv
    yn = yc * lax.rsqrt(var + RWKV_GN_EPS) * lnw_ref[...] + lnb_ref[...]
    bonus = _segsum(r_ref[0] * (kd_ref[0, 0] + kd_ref[1, 0]) * rk_ref[...], e_ref)
    a_out = (yn + bonus * v_ref[0]) * g_ref[0]
    o = _dot(a_out, w_ref[0:c]) + _dot(batt_ref[0], w_ref[c:])
    o_ref[0] = x_ref[0] + mod_ref[0][:, 2 * d:3 * d] * o


def _even_out(y, r, v, g, kd, batt, x, mod, ep, tabs):
    b, t, d = x.shape
    c = r.shape[-1]
    nq = batt.shape[-1]
    tm = _row_tile(t, 512)
    tok = lambda w: pl.BlockSpec((1, tm, w), lambda bi, i: (bi, i, 0))
    tok2 = lambda w: pl.BlockSpec((2, 1, tm, w), lambda bi, i: (0, bi, i, 0))
    full = lambda *shape: pl.BlockSpec(shape, lambda bi, i: (0,) * len(shape))
    return pl.pallas_call(
        functools.partial(_even_out_kernel, d=d, c=c),
        grid=(b, t // tm),
        in_specs=[tok2(c), tok(c), tok(c), tok(c), tok2(c), tok(nq), tok(d),
                  pl.BlockSpec((1, 1, mod.shape[-1]), lambda bi, i: (bi, 0, 0)),
                  full(1, c), full(1, c), full(1, c), full(c, c), full(c + nq, d)],
        out_specs=tok(d),
        out_shape=jax.ShapeDtypeStruct((b, t, d), F32),
        compiler_params=_cparams("parallel", "parallel"),
        name="even_out_proj",
    )(y, r, v, g, kd, batt, x, mod, ep['ln_w'], ep['ln_b'], ep['r_k'], tabs['e'], ep['w_out'])


def _filter_kernel(z_ref, t_ref, w1_ref, b1_ref, w2_ref, b2_ref, w3_ref, b3_ref, fr_ref, wo_ref, dl_ref, o_ref, *, d):
    fr = fr_ref[...]
    h = jnp.sin(fr * (_dot_hi(z_ref[...], w1_ref[...]) + b1_ref[...]))
    h = jnp.sin(fr * (_dot_hi(h, w2_ref[...]) + b2_ref[...]))
    h = jnp.sin(fr * (_dot_hi(h, w3_ref[...]) + b3_ref[...]))
    filt = _dot_hi(h, wo_ref[...])
    modu = jnp.exp(-t_ref[...] * dl_ref[...]) + HY_MOD_SHIFT
    for q in range(o_ref.shape[0]):
        o_ref[q] = filt[:, q * d:(q + 1) * d] * modu


def _hyena_filters(n, op, d):
    t = np.linspace(0.0, 1.0, n, dtype=np.float32)[:, None]
    ang = (2.0 * math.pi * np.arange(n, dtype=np.float32)[:, None] / np.float32(n)).astype(np.float32)
    f = np.linspace(1e-4, HY_BANDS - 1, HY_BANDS, dtype=np.float32)[None, :]
    zfeat = jnp.concatenate([jnp.asarray(t), jnp.cos(jnp.asarray(f * ang)), -jnp.sin(jnp.asarray(f * ang))], axis=-1)
    emb_pad = op['f_w1'].shape[0]
    zfeat = jnp.pad(zfeat, ((0, 0), (0, emb_pad - HY_EMB)))
    deltas = np.abs(np.linspace(math.log(HY_TARGET) / HY_SLOW_PCT, math.log(HY_TARGET) / HY_FAST_PCT, d,
                                dtype=np.float32))[None, :]
    tn = min(n, 256)
    nq = 2 * HY_ORDER
    hf = op['f_w2'].shape[0]
    full = lambda *shape: pl.BlockSpec(shape, lambda i: (0,) * len(shape))
    return pl.pallas_call(
        functools.partial(_filter_kernel, d=d),
        grid=(n // tn,),
        in_specs=[pl.BlockSpec((tn, emb_pad), lambda i: (i, 0)), pl.BlockSpec((tn, 1), lambda i: (i, 0)),
                  full(emb_pad, hf), full(1, hf), full(hf, hf), full(1, hf), full(hf, hf), full(1, hf), full(1, hf),
                  full(hf, nq * d), full(1, d)],
        out_specs=pl.BlockSpec((nq, tn, d), lambda i: (0, i, 0)),
        out_shape=jax.ShapeDtypeStruct((nq, n, d), F32),
        compiler_params=_cparams("parallel"),
        name="hyena_filter",
    )(zfeat, jnp.asarray(t), op['f_w1'], op['f_b1'], op['f_w2'], op['f_b2'], op['f_w3'], op['f_b3'], op['f_freq'],
      op['f_out'], jnp.asarray(deltas))


def _dft(n, rows, cols, sign=-1.0):
    k = np.arange(rows, dtype=np.float64)[:, None]
    m = np.arange(cols, dtype=np.float64)[None, :]
    ang = sign * 2.0 * np.pi * ((k * m) % n) / n
    return np.cos(ang), np.sin(ang)


def _stack(re, im):
    return jnp.asarray(np.concatenate([re, im], axis=0).astype(np.float32)).astype(BF16)


def _fft_tables(n_seq):
    n = 2 * n_seq
    n1 = n // FFT_N2
    f1r, f1i = _dft(n1, n1, n1 // 2)
    h1r, h1i = _dft(n1, n1 // 2, n1, sign=1.0)
    k1 = np.arange(n1, dtype=np.float64)[:, None, None]
    k2 = np.arange(FFT_N2, dtype=np.float64)[None, :, None]
    j2 = np.arange(FFT_N2, dtype=np.float64)[None, None, :]
    ang = -2.0 * np.pi * (((k2 * j2 * n1) + k1 * j2) % n) / n
    gr, gi = np.cos(ang), np.sin(ang)
    g_fwd = np.concatenate([gr, gi], axis=1)
    g_inv = np.concatenate([np.swapaxes(gr, 1, 2), np.swapaxes(gi, 1, 2)], axis=1)
    return {'f1': _stack(f1r, f1i), 'h1': _stack(h1r, h1i),
            'g_fwd': jnp.asarray(g_fwd.astype(np.float32)).astype(BF16),
            'g_inv': jnp.asarray(g_inv.astype(np.float32)).astype(BF16), 'n1': n1}


def _dense_tables(n_seq):
    n = 2 * n_seq
    fr, fi = _dft(n, n, n_seq)
    hr, hi = _dft(n, n_seq, n, sign=1.0)
    return {'f': _stack(fr, fi), 'h': _stack(hr, hi)}


def _fft_a_kernel(f_ref, zr_ref, zi_ref, ar_ref, ai_ref, *, n1, cplx):
    f = f_ref[...]
    p = jnp.dot(f, zr_ref[0].astype(BF16), preferred_element_type=F32)
    if cplx:
        q = jnp.dot(f, zi_ref[0].astype(BF16), preferred_element_type=F32)
        ar_ref[0] = p[0:n1] - q[n1:]
        ai_ref[0] = p[n1:] + q[0:n1]
    else:
        ar_ref[0] = p[0:n1]
        ai_ref[0] = p[n1:]


def _fft_a(u_flat, tabs, cplx):
    s, half, w = u_flat.shape
    n1 = tabs['n1']
    assert half == n1 // 2
    sp = s // 2 if cplx else s
    tl = min(w, 8192)
    zi_map = (lambda si, j: (si + sp, 0, j)) if cplx else (lambda si, j: (si, 0, j))
    out = pl.BlockSpec((1, n1, tl), lambda si, j: (si, 0, j))
    return pl.pallas_call(
        functools.partial(_fft_a_kernel, n1=n1, cplx=cplx),
        grid=(sp, w // tl),
        in_specs=[pl.BlockSpec((2 * n1, half), lambda si, j: (0, 0)),
                  pl.BlockSpec((1, half, tl), lambda si, j: (si, 0, j)),
                  pl.BlockSpec((1, half, tl), zi_map)],
        out_specs=[out, out],
        out_shape=[jax.ShapeDtypeStruct((sp, n1, w), F32)] * 2,
        compiler_params=_cparams("parallel", "parallel"),
        name="fft_stage1",
    )(tabs['f1'], u_flat, u_flat)


def _cplx_mm(s, xr, xi, conj):
    p = jnp.dot(s, xr.astype(BF16), preferred_element_type=F32)
    q = jnp.dot(s, xi.astype(BF16), preferred_element_type=F32)
    m = s.shape[0] // 2
    if conj:
        return p[0:m] + q[m:], q[0:m] - p[m:]
    return p[0:m] - q[m:], p[m:] + q[0:m]


def _fft_b_kernel(gf_ref, gi_ref, ar_ref, ai_ref, kr_ref, ki_ref, dr_ref, di_ref):
    cr, ci = _cplx_mm(gf_ref[0], ar_ref[0, 0], ai_ref[0, 0], False)
    kr = kr_ref[0]
    ki = ki_ref[0]
    er = cr * kr - ci * ki
    ei = cr * ki + ci * kr
    dr, di = _cplx_mm(gi_ref[0], er, ei, True)
    dr_ref[0, 0] = dr
    di_ref[0, 0] = di


def _fft_b(ar, ai, kr, ki, tabs, d):
    sp, n1, w = ar.shape
    ar4 = ar.reshape(sp, n1, FFT_N2, d)
    ai4 = ai.reshape(sp, n1, FFT_N2, d)
    blk = pl.BlockSpec((1, 1, FFT_N2, d), lambda k1, si: (si, k1, 0, 0))
    tab = pl.BlockSpec((1, 2 * FFT_N2, FFT_N2), lambda k1, si: (k1, 0, 0))
    kb = pl.BlockSpec((1, FFT_N2, d), lambda k1, si: (k1, 0, 0))
    dr, di = pl.pallas_call(
        _fft_b_kernel,
        grid=(n1, sp),
        in_specs=[tab, tab, blk, blk, kb, kb],
        out_specs=[blk, blk],
        out_shape=[jax.ShapeDtypeStruct((sp, n1, FFT_N2, d), F32)] * 2,
        compiler_params=_cparams("parallel", "arbitrary"),
        name="fft_stage2_filter",
    )(tabs['g_fwd'], tabs['g_inv'], ar4, ai4, kr, ki)
    return dr.reshape(sp, n1, w), di.reshape(sp, n1, w)


def _fft_c_kernel(h_ref, dr_ref, di_ref, u_ref, x_ref, sk_ref, o_ref):
    yr, yi = _cplx_mm(h_ref[...], dr_ref[0], di_ref[0], False)
    sk = sk_ref[...]
    o_ref[0, 0] = x_ref[0, 0] * (yr + u_ref[0, 0] * sk)
    o_ref[1, 0] = x_ref[1, 0] * (yi + u_ref[1, 0] * sk)


def _fft_c(dr, di, u_flat, x_flat, skip_t, tabs):
    sp, n1, w = dr.shape
    half = n1 // 2
    tl = skip_t.shape[-1]
    u4 = u_flat.reshape(2, sp, half, w)
    x4 = x_flat.reshape(2, sp, half, w)
    pair = pl.BlockSpec((2, 1, half, tl), lambda si, j: (0, si, 0, j))
    dblk = pl.BlockSpec((1, n1, tl), lambda si, j: (si, 0, j))
    out = pl.pallas_call(
        _fft_c_kernel,
        grid=(sp, w // tl),
        in_specs=[pl.BlockSpec((2 * half, n1), lambda si, j: (0, 0)), dblk, dblk, pair, pair,
                  pl.BlockSpec((1, tl), lambda si, j: (0, 0))],
        out_specs=pair,
        out_shape=jax.ShapeDtypeStruct((2, sp, half, w), F32),
        compiler_params=_cparams("parallel", "parallel"),
        name="fft_inverse_stage1_gate",
    )(tabs['h1'], dr, di, u4, x4, skip_t)
    return out.reshape(2 * sp, half, w)


def _spec_b_kernel(gf_ref, ar_ref, ai_ref, kr_ref, ki_ref, *, scale):
    fr, fi = _cplx_mm(gf_ref[0], ar_ref[0, 0], ai_ref[0, 0], False)
    gr, gi = _cplx_mm(gf_ref[0], ar_ref[1, 0], ai_ref[1, 0], False)
    kr_ref[0, 0] = (fr + gr) * scale
    ki_ref[0, 0] = (fi - gi) * scale


def _filter_spectrum_fft(filt, tabs, d):
    nq, n, _ = filt.shape
    n1 = tabs['n1']
    ar, ai = _fft_a(filt.reshape(nq, n1 // 2, FFT_N2 * d), tabs, cplx=False)
    orders = nq // 2
    ar5 = ar.reshape(orders, 2, n1, FFT_N2, d)
    ai5 = ai.reshape(orders, 2, n1, FFT_N2, d)
    blk = pl.BlockSpec((None, 2, 1, FFT_N2, d), lambda o, k1: (o, 0, k1, 0, 0))
    out = pl.BlockSpec((1, 1, FFT_N2, d), lambda o, k1: (o, k1, 0, 0))
    return pl.pallas_call(
        functools.partial(_spec_b_kernel, scale=1.0 / (2 * n)),
        grid=(orders, n1),
        in_specs=[pl.BlockSpec((1, 2 * FFT_N2, FFT_N2), lambda o, k1: (k1, 0, 0)), blk, blk],
        out_specs=[out, out],
        out_shape=[jax.ShapeDtypeStruct((orders, n1, FFT_N2, d), F32)] * 2,
        compiler_params=_cparams("parallel", "parallel"),
        name="filter_spectrum",
    )(tabs['g_fwd'], ar5, ai5)


def _long_conv_fft(u, x, skip, kr, ki, tabs):
    b, n, d = u.shape
    n1 = tabs['n1']
    w = FFT_N2 * d
    u_flat = u.reshape(b, n1 // 2, w)
    ar, ai = _fft_a(u_flat, tabs, cplx=True)
    dr, di = _fft_b(ar, ai, kr, ki, tabs, d)
    tl = min(w, 8192)
    skip_t = jnp.tile(skip.reshape(1, d), (1, tl // d))
    return _fft_c(dr, di, u_flat, x.reshape(b, n1 // 2, w), skip_t, tabs).reshape(b, n, d)


def _dense_spec_kernel(f_ref, filt_ref, kr_ref, ki_ref, *, scale):
    f = f_ref[...]
    n = f.shape[0] // 2
    pf = jnp.dot(f, filt_ref[0, 0].astype(BF16), preferred_element_type=F32)
    pg = jnp.dot(f, filt_ref[0, 1].astype(BF16), preferred_element_type=F32)
    kr_ref[0] = (pf[0:n] + pg[0:n]) * scale
    ki_ref[0] = (pf[n:] - pg[n:]) * scale


def _filter_spectrum_dense(filt, tabs, d):
    nq, n, _ = filt.shape
    orders = nq // 2
    f4 = filt.reshape(orders, 2, n, d)
    out = pl.BlockSpec((1, 2 * n, d), lambda o: (o, 0, 0))
    return pl.pallas_call(
        functools.partial(_dense_spec_kernel, scale=1.0 / (2 * n)),
        grid=(orders,),
        in_specs=[pl.BlockSpec((4 * n, n), lambda o: (0, 0)), pl.BlockSpec((1, 2, n, d), lambda o: (o, 0, 0, 0))],
        out_specs=[out, out],
        out_shape=[jax.ShapeDtypeStruct((orders, 2 * n, d), F32)] * 2,
        compiler_params=_cparams("parallel"),
        name="filter_spectrum_dense",
    )(tabs['f'], f4)


def _dense_conv_kernel(f_ref, h_ref, u_ref, x_ref, kr_ref, ki_ref, sk_ref, o_ref):
    cr, ci = _cplx_mm(f_ref[...], u_ref[0, 0], u_ref[1, 0], False)
    kr = kr_ref[...]
    ki = ki_ref[...]
    yr, yi = _cplx_mm(h_ref[...], cr * kr - ci * ki, cr * ki + ci * kr, False)
    sk = sk_ref[...]
    o_ref[0, 0] = x_ref[0, 0] * (yr + u_ref[0, 0] * sk)
    o_ref[1, 0] = x_ref[1, 0] * (yi + u_ref[1, 0] * sk)


def _long_conv_dense(u, x, skip, kr, ki, tabs):
    b, n, d = u.shape
    sp = b // 2
    pair = pl.BlockSpec((2, 1, n, d), lambda si: (0, si, 0, 0))
    kb = pl.BlockSpec((2 * n, d), lambda si: (0, 0))
    out = pl.pallas_call(
        _dense_conv_kernel,
        grid=(sp,),
        in_specs=[pl.BlockSpec((4 * n, n), lambda si: (0, 0)), pl.BlockSpec((2 * n, 2 * n), lambda si: (0, 0)),
                  pair, pair, kb, kb, pl.BlockSpec((1, d), lambda si: (0, 0))],
        out_specs=pair,
        out_shape=jax.ShapeDtypeStruct((2, sp, n, d), F32),
        compiler_params=_cparams("parallel"),
        name="long_conv_dense",
    )(tabs['f'], tabs['h'], u.reshape(2, sp, n, d), x.reshape(2, sp, n, d), kr, ki, skip.reshape(1, d))
    return out.reshape(b, n, d)


def _hyena_mixer(x, g, mod, op, fft_tabs, dense_tabs):
    b, n, d = x.shape
    z = _proj_conv(x, g, mod, op['w_in'], op['b_in'], op['conv_w'], op['conv_b'])
    filt = _hyena_filters(n, op, d)
    if n <= DENSE_FFT_MAX:
        kr, ki = _filter_spectrum_dense(filt, dense_tabs, d)
        y = _long_conv_dense(z[0], z[1], op['skip'][0], kr[0], ki[0], dense_tabs)
        y = _long_conv_dense(y, z[2], op['skip'][1], kr[1], ki[1], dense_tabs)
    else:
        kr, ki = _filter_spectrum_fft(filt, fft_tabs, d)
        y = _long_conv_fft(z[0], z[1], op['skip'][0], kr[0], ki[0], fft_tabs)
        y = _long_conv_fft(y, z[2], op['skip'][1], kr[1], ki[1], fft_tabs)
    return _out_res(y, x, mod, op['w_out'], op['b_out'])


def _even_mixer(x, ctx, g, mod_l, mod_c, ep, tabs, need_ctx):
    p_lat = _proj(x, g, mod_l, ep['w_in'], 0)
    p_ctx = _proj(ctx, g, mod_c, ep['w_in'], 0)
    rc, vc, kkc, gc, lwc, ac, kdc, qc, kac, vac = _even_prep(p_ctx, ep, tabs, rope=False)
    rl, vl, kkl, gl, lwl, al, kdl, ql, kal, val = _even_prep(p_lat, ep, tabs, rope=True)
    b = x.shape[0]
    c = rl.shape[-1]
    h0 = jnp.zeros((2, b, HEAD_DIM, c), F32)
    y_ctx, s_ctx = _rwkv(rc, vc, kkc, lwc, ac, kdc, h0)
    y_lat, _ = _rwkv(rl, vl, kkl, lwl, al, kdl, s_ctx)
    b_lat = _attention(ql, kal, val, kac, vac, ep['sink'], local=True)
    x_new = _even_out(y_lat, rl, vl, gl, kdl, b_lat, x, mod_l, ep, tabs)
    if not need_ctx:
        return x_new, None
    b_ctx = _attention(qc, kac, vac, kac, vac, ep['sink'], local=False)
    ctx_new = _even_out(y_ctx, rc, vc, gc, kdc, b_ctx, ctx, mod_c, ep, tabs)
    return x_new, ctx_new


def _rope_tables(n_tokens):
    rows = n_tokens // GRID_W
    row = jnp.repeat(jnp.arange(rows), GRID_W).astype(F32)
    col = jnp.tile(jnp.arange(GRID_W), rows).astype(F32)
    n_freq = HEAD_DIM // 4
    inv = ROPE_THETA ** (-jnp.arange(n_freq, dtype=F32) / n_freq)
    ang = jnp.concatenate([row[:, None] * inv, col[:, None] * inv], axis=-1)
    cos, sin = jnp.cos(ang), jnp.sin(ang)
    reps = LANES // HEAD_DIM
    cos_t = jnp.tile(jnp.concatenate([cos, cos], axis=-1), (1, reps))
    sin_t = jnp.tile(jnp.concatenate([-sin, sin], axis=-1), (1, reps))
    return cos_t, sin_t


def _block_ones(width):
    idx = np.arange(width) // HEAD_DIM
    return jnp.asarray((idx[:, None] == idx[None, :]).astype(np.float32)).astype(BF16)


def _lora_pad(w):
    z = jnp.zeros_like(w[0])
    return jnp.stack([jnp.concatenate([w[0], z], axis=0), jnp.concatenate([z, w[1]], axis=0)], axis=0)


def kernel(x, c, ctx, c_ctx, ada_w, ada_b, norm1_g, norm2_g, ffn_up, ffn_conv_w, ffn_conv_b, ffn_down, ev_w_in, ev_mu_prev, ev_mu_next, ev_w0, ev_w2, ev_a0, ev_a2, ev_g2, ev_k_k, ev_k_a, ev_r_k, ev_ln_w, ev_ln_b, ev_q_norm, ev_k_norm, ev_sink, ev_w_out, od_w_in, od_b_in, od_conv_w, od_conv_b, od_f_w1, od_f_b1, od_f_w2, od_f_b2, od_f_w3, od_f_b3, od_f_freq, od_f_out, od_skip, od_w_out, od_b_out):
    bsz, seq, d = x.shape
    lc = ctx.shape[1]
    depth = ada_w.shape[0]
    a_width = ev_k_k.shape[-1]
    nq = ev_sink.shape[-1] * HEAD_DIM
    nk = B_KV_HEADS * HEAD_DIM

    cond = jnp.zeros((BF16_ROWS, d), F32).at[:bsz].set(c).at[bsz].set(c_ctx)
    mod = _ada_mod(cond, ada_w, ada_b)

    cos_t, sin_t = _rope_tables(seq)
    tabs = {'cos': cos_t, 'sin': sin_t, 'e': _block_ones(nq)}
    fft_tabs = _fft_tables(seq) if seq > DENSE_FFT_MAX else None
    dense_lat = _dense_tables(seq) if seq <= DENSE_FFT_MAX else None
    fft_ctx = _fft_tables(lc) if lc > DENSE_FFT_MAX else None
    dense_ctx = _dense_tables(lc) if lc <= DENSE_FFT_MAX else None

    for layer in range(depth):
        need_ctx = layer < depth - 1
        even = layer % 2 == 0
        j = layer // 2
        mod_l = mod[layer, :bsz].reshape(bsz, 1, 6 * d)
        mod_c = jnp.broadcast_to(mod[layer, bsz].reshape(1, 1, 6 * d), (bsz, 1, 6 * d))
        if even:
            ep = {'w_in': ev_w_in[j].astype(BF16), 'mu_prev': ev_mu_prev[j][None], 'mu_next': ev_mu_next[j][None],
                  'w0': ev_w0[j], 'w2pad': _lora_pad(ev_w2[j]).astype(BF16), 'a0': ev_a0[j],
                  'a2pad': _lora_pad(ev_a2[j]).astype(BF16), 'g2': ev_g2[j].astype(BF16),
                  'k_k': ev_k_k[j][None], 'k_a': ev_k_a[j][None], 'r_k': ev_r_k[j].reshape(1, a_width),
                  'ln_w': ev_ln_w[j][None], 'ln_b': ev_ln_b[j][None],
                  'q_norm_t': jnp.tile(ev_q_norm[j], nq // HEAD_DIM)[None],
                  'k_norm_t': jnp.tile(ev_k_norm[j], nk // HEAD_DIM)[None],
                  'sink': ev_sink[j], 'w_out': ev_w_out[j].astype(BF16), 'nq': nq, 'nk': nk}
            x, ctx_new = _even_mixer(x, ctx, norm1_g[layer], mod_l, mod_c, ep, tabs, need_ctx)
        else:
            emb_pad = HEAD_DIM
            op = {'w_in': od_w_in[j].astype(BF16), 'b_in': od_b_in[j], 'conv_w': od_conv_w[j],
                  'conv_b': od_conv_b[j],
                  'f_w1': jnp.pad(od_f_w1[j], ((0, emb_pad - HY_EMB), (0, 0))), 'f_b1': od_f_b1[j][None],
                  'f_w2': od_f_w2[j], 'f_b2': od_f_b2[j][None], 'f_w3': od_f_w3[j], 'f_b3': od_f_b3[j][None],
                  'f_freq': od_f_freq[j][None], 'f_out': od_f_out[j], 'skip': od_skip[j],
                  'w_out': od_w_out[j].astype(BF16), 'b_out': od_b_out[j]}
            ctx_new = _hyena_mixer(ctx, norm1_g[layer], mod_c, op, fft_ctx, dense_ctx) if need_ctx else None
            x = _hyena_mixer(x, norm1_g[layer], mod_l, op, fft_tabs, dense_lat)
        w_up = ffn_up[layer].astype(BF16)
        w_down = ffn_down[layer].astype(BF16)
        x = _conv_ffn(x, norm2_g[layer], mod_l, w_up, ffn_conv_w[layer], ffn_conv_b[layer], w_down)
        if need_ctx:
            ctx = _conv_ffn(ctx_new, norm2_g[layer], mod_c, w_up, ffn_conv_w[layer], ffn_conv_b[layer], w_down)
    return x
```

```python
import functools
import math

import numpy as np
import jax
import jax.numpy as jnp
from jax import lax
from jax.experimental import pallas as pl
from jax.experimental.pallas import tpu as pltpu

F32 = jnp.float32
BF16 = jnp.bfloat16

HEAD_DIM = 64
GRID_W = 64
DECAY_LORA = 64
ICLR_LORA = 64
GATE_LORA = 128
RWKV_GN_EPS = 64e-5
B_KV_HEADS = 2
WINDOW = 128
BLOCK = 128
ROPE_THETA = 10000.0
MASK_VALUE = -1e30
HY_ORDER = 2
HY_EMB = 33
HY_BANDS = (HY_EMB - 1) // 2
HY_TARGET = 1e-2
HY_FAST_PCT = 0.3
HY_SLOW_PCT = 1.5
HY_MOD_SHIFT = 0.05
NORM_EPS = 1e-6

V7X_VMEM_BYTES = 64 * 1024 * 1024
VMEM_LIMIT_BYTES = V7X_VMEM_BYTES * 3 // 4
LANES = 128
SUBLANES = 8
BF16_ROWS = 16
RWKV_CHUNK = 64
RWKV_PREP_CHUNKS = 4
RWKV_INV_BASE = 8
RWKV_GRAM_PASSES = 1
RWKV_INV_PASSES = 1
RWKV_REST_PASSES = 1
RWKV_SCAN_PASSES = 2
FFT_N2 = 128
DENSE_FFT_MAX = 512

HIGHEST = lax.Precision.HIGHEST


def _cparams(*sem):
    return pltpu.CompilerParams(dimension_semantics=sem, vmem_limit_bytes=VMEM_LIMIT_BYTES)


def _dot(a, b):
    return jnp.dot(a.astype(BF16), b.astype(BF16), preferred_element_type=F32)


def _dot_nt(a, b):
    return lax.dot_general(a.astype(BF16), b.astype(BF16), (((1,), (1,)), ((), ())), preferred_element_type=F32)


def _dot_tn(a, b):
    return lax.dot_general(a.astype(BF16), b.astype(BF16), (((0,), (0,)), ((), ())), preferred_element_type=F32)


def _dot_hi(a, b):
    return jnp.dot(a, b, preferred_element_type=F32, precision=HIGHEST)


def _dot_nt_hi(a, b):
    return lax.dot_general(a, b, (((1,), (1,)), ((), ())), preferred_element_type=F32, precision=HIGHEST)


def _dot_tn_hi(a, b):
    return lax.dot_general(a, b, (((0,), (0,)), ((), ())), preferred_element_type=F32, precision=HIGHEST)


def _segsum(x, e_ref):
    hi = x.astype(BF16)
    lo = (x - hi.astype(F32)).astype(BF16)
    e = e_ref[...]
    return jnp.dot(hi, e, preferred_element_type=F32) + jnp.dot(lo, e, preferred_element_type=F32)


def _norm_mod(x, g, shift, scale):
    xn = x * lax.rsqrt(jnp.mean(x * x, axis=-1, keepdims=True) + NORM_EPS)
    return (xn * g) * (1.0 + scale) + shift


def _row_tile(t, want):
    tm = min(t, want)
    assert t % tm == 0 and tm % BF16_ROWS == 0
    return tm


def _ada_kernel(c_ref, w_ref, b_ref, o_ref):
    c = c_ref[...]
    s = c * jax.nn.sigmoid(c)
    o_ref[0] = _dot(s, w_ref[0]) + b_ref[0]


def _ada_mod(cond, ada_w, ada_b):
    depth, d, n = ada_w.shape
    tn = n // 4
    rows = cond.shape[0]
    return pl.pallas_call(
        _ada_kernel,
        grid=(depth, n // tn),
        in_specs=[pl.BlockSpec((rows, d), lambda l, j: (0, 0)),
                  pl.BlockSpec((1, d, tn), lambda l, j: (l, 0, j)),
                  pl.BlockSpec((1, 1, tn), lambda l, j: (l, 0, j))],
        out_specs=pl.BlockSpec((1, rows, tn), lambda l, j: (l, 0, j)),
        out_shape=jax.ShapeDtypeStruct((depth, rows, n), F32),
        compiler_params=_cparams("parallel", "parallel"),
        name="ada_mod",
    )(cond, ada_w, ada_b.reshape(depth, 1, n))


def _proj_kernel(x_ref, g_ref, mod_ref, w_ref, o_ref, h_scr, *, d, slot):
    @pl.when(pl.program_id(2) == 0)
    def _():
        m = mod_ref[0]
        h = _norm_mod(x_ref[0], g_ref[...], m[:, slot * d:(slot + 1) * d], m[:, (slot + 1) * d:(slot + 2) * d])
        h_scr[...] = h.astype(BF16)

    o_ref[0] = jnp.dot(h_scr[...], w_ref[...], preferred_element_type=F32)


def _proj(x, g, mod, w, slot):
    b, t, d = x.shape
    n = w.shape[1]
    tm = _row_tile(t, 512)
    tn = 128 * math.gcd(n // 128, 7 if n % 896 == 0 else 4)
    return pl.pallas_call(
        functools.partial(_proj_kernel, d=d, slot=slot),
        grid=(b, t // tm, n // tn),
        in_specs=[pl.BlockSpec((1, tm, d), lambda bi, i, j: (bi, i, 0)),
                  pl.BlockSpec((1, d), lambda bi, i, j: (0, 0)),
                  pl.BlockSpec((1, 1, mod.shape[-1]), lambda bi, i, j: (bi, 0, 0)),
                  pl.BlockSpec((d, tn), lambda bi, i, j: (0, j))],
        out_specs=pl.BlockSpec((1, tm, tn), lambda bi, i, j: (bi, i, j)),
        out_shape=jax.ShapeDtypeStruct((b, t, n), F32),
        scratch_shapes=[pltpu.VMEM((tm, d), BF16)],
        compiler_params=_cparams("parallel", "parallel", "arbitrary"),
        name="norm_mod_proj",
    )(x, g.reshape(1, d), mod, w)


def _halo_specs(tm, t, d, nargs):
    r = tm // BF16_ROWS
    last = t // BF16_ROWS - 1
    if nargs == 3:
        prev = pl.BlockSpec((1, BF16_ROWS, d), lambda bi, i, j: (bi, jnp.maximum(i * r - 1, 0), 0))
        nxt = pl.BlockSpec((1, BF16_ROWS, d), lambda bi, i, j: (bi, jnp.minimum((i + 1) * r, last), 0))
    else:
        prev = pl.BlockSpec((1, BF16_ROWS, d), lambda bi, i: (bi, jnp.maximum(i * r - 1, 0), 0))
        nxt = pl.BlockSpec((1, BF16_ROWS, d), lambda bi, i: (bi, jnp.minimum((i + 1) * r, last), 0))
    return prev, nxt


def _fill_h(x_ref, xp_ref, xn_ref, g, shift, scale, h_scr, tm):
    h_scr[0:BF16_ROWS] = _norm_mod(xp_ref[0], g, shift, scale).astype(BF16)
    h_scr[BF16_ROWS:BF16_ROWS + tm] = _norm_mod(x_ref[0], g, shift, scale).astype(BF16)
    h_scr[BF16_ROWS + tm:2 * BF16_ROWS + tm] = _norm_mod(xn_ref[0], g, shift, scale).astype(BF16)


def _conv3_rows(u, u_scr, cw_ref, cb_ref, tm, t_total):
    i = pl.program_id(1)
    rows = lax.broadcasted_iota(jnp.int32, (tm + 2 * BF16_ROWS, 1), 0) + (i * tm - BF16_ROWS)
    u_scr[...] = jnp.where((rows >= 0) & (rows < t_total), u, 0.0)
    cw = cw_ref[...]
    o = BF16_ROWS
    return (u_scr[o - 1:o - 1 + tm] * cw[0:1] + u_scr[o:o + tm] * cw[1:2] + u_scr[o + 1:o + 1 + tm] * cw[2:3]
            + cb_ref[...])


def _proj_conv_kernel(x_ref, xp_ref, xn_ref, g_ref, mod_ref, w_ref, b_ref, cw_ref, cb_ref, o_ref, h_scr, u_scr,
                      *, d, tm, t_total):
    @pl.when(pl.program_id(2) == 0)
    def _():
        m = mod_ref[0]
        _fill_h(x_ref, xp_ref, xn_ref, g_ref[...], m[:, 0:d], m[:, d:2 * d], h_scr, tm)

    u = jnp.dot(h_scr[...], w_ref[...], preferred_element_type=F32) + b_ref[...]
    o_ref[0, 0] = _conv3_rows(u, u_scr, cw_ref, cb_ref, tm, t_total)


def _proj_conv(x, g, mod, w, bias, cw, cb):
    b, t, d = x.shape
    n = w.shape[1]
    tm = _row_tile(t, 512)
    tn = d
    prev, nxt = _halo_specs(tm, t, d, 3)
    return pl.pallas_call(
        functools.partial(_proj_conv_kernel, d=d, tm=tm, t_total=t),
        grid=(b, t // tm, n // tn),
        in_specs=[pl.BlockSpec((1, tm, d), lambda bi, i, j: (bi, i, 0)), prev, nxt,
                  pl.BlockSpec((1, d), lambda bi, i, j: (0, 0)),
                  pl.BlockSpec((1, 1, mod.shape[-1]), lambda bi, i, j: (bi, 0, 0)),
                  pl.BlockSpec((d, tn), lambda bi, i, j: (0, j)),
                  pl.BlockSpec((1, tn), lambda bi, i, j: (0, j)),
                  pl.BlockSpec((3, tn), lambda bi, i, j: (0, j)),
                  pl.BlockSpec((1, tn), lambda bi, i, j: (0, j))],
        out_specs=pl.BlockSpec((1, 1, tm, tn), lambda bi, i, j: (j, bi, i, 0)),
        out_shape=jax.ShapeDtypeStruct((n // tn, b, t, tn), F32),
        scratch_shapes=[pltpu.VMEM((tm + 2 * BF16_ROWS, d), BF16), pltpu.VMEM((tm + 2 * BF16_ROWS, tn), F32)],
        compiler_params=_cparams("parallel", "parallel", "arbitrary"),
        name="hyena_in_proj",
    )(x, x, x, g.reshape(1, d), mod, w, bias.reshape(1, n), cw, cb.reshape(1, n))


def _ffn_kernel(x_ref, xp_ref, xn_ref, g_ref, mod_ref, wg_ref, wv_ref, cwg_ref, cwv_ref, cbg_ref, cbv_ref, wd_ref,
                o_ref, h_scr, ug_scr, uv_scr, acc_scr, *, d, tm, t_total):
    j = pl.program_id(2)

    @pl.when(j == 0)
    def _():
        m = mod_ref[0]
        _fill_h(x_ref, xp_ref, xn_ref, g_ref[...], m[:, 3 * d:4 * d], m[:, 4 * d:5 * d], h_scr, tm)
        acc_scr[...] = jnp.zeros_like(acc_scr)

    h = h_scr[...]
    gate = _conv3_rows(jnp.dot(h, wg_ref[...], preferred_element_type=F32), ug_scr, cwg_ref, cbg_ref, tm, t_total)
    val = _conv3_rows(jnp.dot(h, wv_ref[...], preferred_element_type=F32), uv_scr, cwv_ref, cbv_ref, tm, t_total)
    act = (gate * jax.nn.sigmoid(gate)) * val
    acc_scr[...] += jnp.dot(act.astype(BF16), wd_ref[...], preferred_element_type=F32)

    @pl.when(j == pl.num_programs(2) - 1)
    def _():
        o_ref[0] = x_ref[0] + mod_ref[0][:, 5 * d:6 * d] * acc_scr[...]


def _conv_ffn(x, g, mod, w_up, cw, cb, w_down):
    b, t, d = x.shape
    f = w_down.shape[0]
    tm = _row_tile(t, 1024)
    tf = 256 if f % 256 == 0 else 128
    nf = f // tf
    prev, nxt = _halo_specs(tm, t, d, 3)
    cb2 = cb.reshape(1, 2 * f)
    return pl.pallas_call(
        functools.partial(_ffn_kernel, d=d, tm=tm, t_total=t),
        grid=(b, t // tm, nf),
        in_specs=[pl.BlockSpec((1, tm, d), lambda bi, i, j: (bi, i, 0)), prev, nxt,
                  pl.BlockSpec((1, d), lambda bi, i, j: (0, 0)),
                  pl.BlockSpec((1, 1, mod.shape[-1]), lambda bi, i, j: (bi, 0, 0)),
                  pl.BlockSpec((d, tf), lambda bi, i, j: (0, j)),
                  pl.BlockSpec((d, tf), lambda bi, i, j: (0, nf + j)),
                  pl.BlockSpec((3, tf), lambda bi, i, j: (0, j)),
                  pl.BlockSpec((3, tf), lambda bi, i, j: (0, nf + j)),
                  pl.BlockSpec((1, tf), lambda bi, i, j: (0, j)),
                  pl.BlockSpec((1, tf), lambda bi, i, j: (0, nf + j)),
                  pl.BlockSpec((tf, d), lambda bi, i, j: (j, 0))],
        out_specs=pl.BlockSpec((1, tm, d), lambda bi, i, j: (bi, i, 0)),
        out_shape=jax.ShapeDtypeStruct((b, t, d), F32),
        scratch_shapes=[pltpu.VMEM((tm + 2 * BF16_ROWS, d), BF16),
                        pltpu.VMEM((tm + 2 * BF16_ROWS, tf), F32),
                        pltpu.VMEM((tm + 2 * BF16_ROWS, tf), F32),
                        pltpu.VMEM((tm, d), F32)],
        compiler_params=_cparams("parallel", "parallel", "arbitrary"),
        name="conv_ffn",
    )(x, x, x, g.reshape(1, d), mod, w_up, w_up, cw, cw, cb2, cb2, w_down)


def _out_res_kernel(y_ref, x_ref, mod_ref, w_ref, b_ref, o_ref, *, d):
    o = _dot(y_ref[0], w_ref[...]) + b_ref[...]
    o_ref[0] = x_ref[0] + mod_ref[0][:, 2 * d:3 * d] * o


def _out_res(y, x, mod, w, bias):
    b, t, d = x.shape
    tm = _row_tile(t, 512)
    return pl.pallas_call(
        functools.partial(_out_res_kernel, d=d),
        grid=(b, t // tm),
        in_specs=[pl.BlockSpec((1, tm, d), lambda bi, i: (bi, i, 0)),
                  pl.BlockSpec((1, tm, d), lambda bi, i: (bi, i, 0)),
                  pl.BlockSpec((1, 1, mod.shape[-1]), lambda bi, i: (bi, 0, 0)),
                  pl.BlockSpec((d, d), lambda bi, i: (0, 0)),
                  pl.BlockSpec((1, d), lambda bi, i: (0, 0))],
        out_specs=pl.BlockSpec((1, tm, d), lambda bi, i: (bi, i, 0)),
        out_shape=jax.ShapeDtypeStruct((b, t, d), F32),
        compiler_params=_cparams("parallel", "parallel"),
        name="out_proj_residual",
    )(y, x, mod, w, bias.reshape(1, d))


def _softplus(z):
    return jnp.maximum(z, 0.0) + jnp.log1p(jnp.exp(-jnp.abs(z)))


def _even_prep_kernel(p_ref, pp_ref, pn_ref, mup_ref, mun_ref, w0_ref, w2_ref, a0_ref, a2_ref, g2_ref, kk_ref,
                      ka_ref, qn_ref, kn_ref, cos_ref, sin_ref, e_ref,
                      r_ref, v_ref, kkn_ref, g_ref, lw_ref, a_ref, kd_ref, q_ref, katt_ref, vatt_ref,
                      *, tm, c, a_in, rope):
    i = pl.program_id(1)
    nt = pl.num_programs(1)
    pa = p_ref[0][:, 0:a_in]
    rows = lax.broadcasted_iota(jnp.int32, (tm, 1), 0)
    prev_row = jnp.where(i > 0, pp_ref[0][BF16_ROWS - 1:BF16_ROWS, 0:a_in], 0.0)
    next_row = jnp.where(i < nt - 1, pn_ref[0][0:1, 0:a_in], 0.0)
    prev = jnp.where(rows == 0, prev_row, pltpu.roll(pa, 1, 0))
    nxt = jnp.where(rows == tm - 1, next_row, pltpu.roll(pa, tm - 1, 0))
    za = pa + mup_ref[...] * (prev - pa) + mun_ref[...] * (nxt - pa)

    r = za[:, 0:c]
    k = za[:, c:2 * c]
    v = za[:, 2 * c:3 * c]
    wd = jnp.tanh(za[:, 3 * c:3 * c + 2 * DECAY_LORA])
    ad = za[:, 3 * c + 2 * DECAY_LORA:3 * c + 2 * DECAY_LORA + 2 * ICLR_LORA]
    gd = za[:, 3 * c + 2 * DECAY_LORA + 2 * ICLR_LORA:a_in]
    r_ref[0] = r
    v_ref[0] = v
    g_ref[0] = _dot(jax.nn.sigmoid(gd), g2_ref[...])
    kkv = k * kk_ref[...]
    ss = _segsum(kkv * kkv, e_ref)
    kkn_ref[0] = kkv / jnp.maximum(jnp.sqrt(ss), 1e-12)
    for dd in range(2):
        w_log = -_softplus(-(w0_ref[dd:dd + 1] + _dot(wd, w2_ref[dd]))) - 0.5
        lw_ref[dd, 0] = -jnp.exp(w_log)
        a = jax.nn.sigmoid(a0_ref[dd:dd + 1] + _dot(ad, a2_ref[dd]))
        a_ref[dd, 0] = a
        kd_ref[dd, 0] = k * (1.0 + (a - 1.0) * ka_ref[...])

    pb = p_ref[0][:, a_in:]
    nq = q_ref.shape[-1]
    nk = katt_ref.shape[-1]
    q = pb[:, 0:nq]
    kat = pb[:, nq:nq + nk]
    vatt_ref[0] = pb[:, nq + nk:]
    half = HEAD_DIM // 2

    def norm_rope(x, gain, e):
        w = x.shape[-1]
        ms = _segsum(x * x, e) * (1.0 / HEAD_DIM)
        xn = x * lax.rsqrt(ms + NORM_EPS) * gain
        if not rope:
            return xn
        lane = lax.broadcasted_iota(jnp.int32, (1, w), 1)
        first = (lane % HEAD_DIM) < half
        swapped = jnp.where(first, pltpu.roll(xn, w - half, 1), pltpu.roll(xn, half, 1))
        reps = w // LANES
        cos = jnp.concatenate([cos_ref[...]] * reps, axis=1) if reps > 1 else cos_ref[...]
        sin = jnp.concatenate([sin_ref[...]] * reps, axis=1) if reps > 1 else sin_ref[...]
        return xn * cos + swapped * sin

    q_ref[0] = norm_rope(q, qn_ref[...], e_ref[...])
    katt_ref[0] = norm_rope(kat, kn_ref[...], e_ref[0:nk, 0:nk])


def _even_prep(p, ep, tabs, rope):
    b, t, n = p.shape
    c = ep['k_k'].shape[-1]
    a_in = ep['mu_prev'].shape[-1]
    nq = ep['nq']
    nk = ep['nk']
    tm = _row_tile(t, 256)
    prev, nxt = _halo_specs(tm, t, n, 2)
    full = lambda *shape: pl.BlockSpec(shape, lambda bi, i: (0,) * len(shape))
    tok = lambda w: pl.BlockSpec((1, tm, w), lambda bi, i: (bi, i, 0))
    tok2 = lambda w: pl.BlockSpec((2, 1, tm, w), lambda bi, i: (0, bi, i, 0))
    sd = lambda *shape: jax.ShapeDtypeStruct(shape, F32)
    return pl.pallas_call(
        functools.partial(_even_prep_kernel, tm=tm, c=c, a_in=a_in, rope=rope),
        grid=(b, t // tm),
        in_specs=[tok(n), prev, nxt, full(1, a_in), full(1, a_in), full(2, c), full(2, 2 * DECAY_LORA, c),
                  full(2, c), full(2, 2 * ICLR_LORA, c), full(GATE_LORA, c), full(1, c), full(1, c),
                  full(1, nq), full(1, nk),
                  pl.BlockSpec((tm, LANES), lambda bi, i: (i, 0)), pl.BlockSpec((tm, LANES), lambda bi, i: (i, 0)),
                  full(nq, nq)],
        out_specs=[tok(c), tok(c), tok(c), tok(c), tok2(c), tok2(c), tok2(c), tok(nq), tok(nk), tok(nk)],
        out_shape=[sd(b, t, c), sd(b, t, c), sd(b, t, c), sd(b, t, c), sd(2, b, t, c), sd(2, b, t, c),
                   sd(2, b, t, c), sd(b, t, nq), sd(b, t, nk), sd(b, t, nk)],
        compiler_params=_cparams("parallel", "parallel"),
        name="even_prep",
    )(p, p, p, ep['mu_prev'], ep['mu_next'], ep['w0'], ep['w2pad'], ep['a0'], ep['a2pad'], ep['g2'], ep['k_k'],
      ep['k_a'], ep['q_norm_t'], ep['k_norm_t'], tabs['cos'][:t], tabs['sin'][:t], tabs['e'])


HEADS_PER_GROUP = 4
GROUP_LANES = HEADS_PER_GROUP * HEAD_DIM


def _split_bf16(x):
    hi = x.astype(BF16)
    return hi, (x - hi.astype(F32)).astype(BF16)


def _block_diag(x, bmask):
    return jnp.where(bmask, jnp.concatenate([x] * HEADS_PER_GROUP, axis=0), jnp.zeros((), x.dtype))


def _head_mm(lhs, rhs, bmask, passes, nt=False):
    dn = (((1,), (1,)), ((), ())) if nt else (((1,), (0,)), ((), ()))
    if passes == 1:
        return lax.dot_general(lhs.astype(BF16), _block_diag(rhs.astype(BF16), bmask), dn, preferred_element_type=F32)
    lh, ll = _split_bf16(lhs)
    rh, rl = _split_bf16(rhs)
    m = lhs.shape[0]
    top = lax.dot_general(jnp.concatenate([lh, ll], axis=0), _block_diag(rh, bmask), dn, preferred_element_type=F32)
    return top[0:m] + top[m:] + lax.dot_general(lh, _block_diag(rl, bmask), dn, preferred_element_type=F32)


def _head_mm_tn(lhs, rhs, lane_head, passes):
    dn = (((0,), (0,)), ((), ()))
    if passes == 1:
        full = lax.dot_general(lhs.astype(BF16), rhs.astype(BF16), dn, preferred_element_type=F32)
    else:
        lh, ll = _split_bf16(lhs)
        rh, rl = _split_bf16(rhs)
        full = (lax.dot_general(jnp.concatenate([lh, ll], axis=0), jnp.concatenate([rh, rh], axis=0), dn,
                                preferred_element_type=F32)
                + lax.dot_general(lh, rl, dn, preferred_element_type=F32))
    out = jnp.where(lane_head == 0, full[0:HEAD_DIM], 0.0)
    for h in range(1, HEADS_PER_GROUP):
        out = out + jnp.where(lane_head == h, full[h * HEAD_DIM:(h + 1) * HEAD_DIM], 0.0)
    return out


def _group_masks():
    r = lax.broadcasted_iota(jnp.int32, (GROUP_LANES, GROUP_LANES), 0)
    c = lax.broadcasted_iota(jnp.int32, (GROUP_LANES, GROUP_LANES), 1)
    bmask = (r // HEAD_DIM) == (c // HEAD_DIM)
    lane_head = lax.broadcasted_iota(jnp.int32, (1, GROUP_LANES), 1) // HEAD_DIM
    return bmask, lane_head


def _rwkv_prep_kernel(r_ref, v_ref, kk_ref, lw_ref, a_ref, kd_ref, r2_ref, yl_ref, t_ref, z_ref, *, cs):
    sgn = 1 - 2 * pl.program_id(0)
    row = lax.broadcasted_iota(jnp.int32, (cs, cs), 0)
    col = lax.broadcasted_iota(jnp.int32, (cs, cs), 1)
    tri = (((row - col) * sgn) >= 0).astype(F32)
    bmask, lane_head = _group_masks()
    t_idx = lax.broadcasted_iota(jnp.int32, (cs, GROUP_LANES), 0)
    s_idx = lax.broadcasted_iota(jnp.int32, (cs, GROUP_LANES), 1) % HEAD_DIM
    diff = (t_idx - s_idx) * sgn
    strict = diff > 0
    incl = diff >= 0
    eye = (diff == 0).astype(F32)
    base = min(RWKV_INV_BASE, cs)
    same_base = (t_idx // base) == (s_idx // base)
    groups = r_ref.shape[-1] // GROUP_LANES

    ch = []
    for c0 in range(0, r_ref.shape[1], cs):
        rows = slice(c0, c0 + cs)
        r, v, kk = r_ref[0, rows], v_ref[0, rows], kk_ref[0, rows]
        lw, a, kd = lw_ref[0, 0, rows], a_ref[0, 0, rows], kd_ref[0, 0, rows]
        g = _dot_hi(tri, lw)
        gp = g - lw
        gref = g[cs // 2:cs // 2 + 1]
        gend = jnp.sum(lw, axis=0, keepdims=True)
        bvec = kk * a
        full = {'a_t': -kk * jnp.exp(gp - gref), 'a_0': -kk * jnp.exp(gp),
                'b_t': bvec * jnp.exp(gref - g), 'b_e': bvec * jnp.exp(gend - g),
                'k_t': kd * jnp.exp(gref - g), 'k_e': kd * jnp.exp(gend - g),
                'r_t': r * jnp.exp(g - gref), 'r_0': r * jnp.exp(g), 'v': v,
                'wend': jnp.broadcast_to(jnp.exp(gend), (cs, v.shape[-1]))}
        for gi in range(groups):
            sl = slice(gi * GROUP_LANES, (gi + 1) * GROUP_LANES)
            c = {k: x[:, sl] for k, x in full.items()}
            c['rows'], c['sl'] = rows, sl
            ch.append(c)

    mm = lambda x, y, passes, nt=False: [_head_mm(p_, q_, bmask, passes, nt) for p_, q_ in zip(x, y)]
    get = lambda k: [c[k] for c in ch]
    ar = [jnp.concatenate([c['a_t'], c['r_t']], axis=0) for c in ch]
    gb = mm(ar, get('b_t'), RWKV_GRAM_PASSES, True)
    gk = mm(ar, get('k_t'), RWKV_GRAM_PASSES, True)
    nmat = [jnp.where(strict, x[0:cs], 0.0) for x in gb]
    g_ak = [jnp.where(strict, x[0:cs], 0.0) for x in gk]
    g_rb = [jnp.where(incl, x[cs:], 0.0) for x in gb]
    g_rk = [jnp.where(incl, x[cs:], 0.0) for x in gk]
    npow = [jnp.where(same_base, x, 0.0) for x in nmat]
    p = [eye + x for x in npow]
    for _ in range(int(math.log2(base)) - 1):
        npow = mm(npow, npow, RWKV_INV_PASSES)
        p = [x + y for x, y in zip(p, mm(npow, p, RWKV_INV_PASSES))]
    m = base
    while m < cs:
        off = ((t_idx // m) != (s_idx // m)) & ((t_idx // (2 * m)) == (s_idx // (2 * m)))
        q = mm([jnp.where(off, x, 0.0) for x in nmat], p, RWKV_INV_PASSES)
        p = [x + y for x, y in zip(p, mm(p, q, RWKV_INV_PASSES))]
        m *= 2
    vg = get('v')
    a2 = mm(p, get('a_0'), RWKV_REST_PASSES)
    u_v = mm(p, mm(g_ak, vg, RWKV_REST_PASSES), RWKV_REST_PASSES)
    r2 = mm(g_rb, a2, RWKV_REST_PASSES)
    yl1 = mm(g_rb, u_v, RWKV_REST_PASSES)
    yl2 = mm(g_rk, vg, RWKV_REST_PASSES)
    tt = [_head_mm_tn(x, c['b_e'], lane_head, RWKV_REST_PASSES) for x, c in zip(a2, ch)]
    zz = [_head_mm_tn(jnp.concatenate([u, c['v']], axis=0), jnp.concatenate([c['b_e'], c['k_e']], axis=0),
                      lane_head, RWKV_REST_PASSES) for u, c in zip(u_v, ch)]
    for i, c in enumerate(ch):
        rows, sl = c['rows'], c['sl']
        r2_ref[0, 0, rows, sl] = c['r_0'] + r2[i]
        yl_ref[0, 0, rows, sl] = yl1[i] + yl2[i]
        t_ref[0, 0, rows, sl] = eye * c['wend'] + tt[i]
        z_ref[0, 0, rows, sl] = zz[i]


def _rwkv_scan_kernel(h0_ref, r2f_ref, ylf_ref, tf_ref, zf_ref, r2b_ref, ylb_ref, tb_ref, zb_ref,
                      yf_ref, yb_ref, ht_ref, s_scr):
    ci = pl.program_id(0)

    @pl.when(ci == 0)
    def _():
        s_scr[...] = h0_ref[...]

    bmask, _ = _group_masks()
    ins = ((r2f_ref, ylf_ref, tf_ref, zf_ref, yf_ref), (r2b_ref, ylb_ref, tb_ref, zb_ref, yb_ref))
    for d, (r2_ref, yl_ref, t_ref, z_ref, y_ref) in enumerate(ins):
        for b in range(s_scr.shape[1]):
            for gi in range(s_scr.shape[-1] // GROUP_LANES):
                sl = slice(gi * GROUP_LANES, (gi + 1) * GROUP_LANES)
                s = s_scr[d, b, :, sl]
                y_ref[0, b, :, sl] = yl_ref[0, b, :, sl] + _head_mm(r2_ref[0, b, :, sl], s, bmask, 1, nt=True)
                s_scr[d, b, :, sl] = (_head_mm(s, t_ref[0, b, :, sl], bmask, RWKV_SCAN_PASSES)
                                      + z_ref[0, b, :, sl])

    @pl.when(ci == pl.num_programs(0) - 1)
    def _():
        ht_ref[...] = s_scr[...]


def _rwkv(r, v, kk, lw, a, kd, h0):
    b, t, c = r.shape
    cs = min(RWKV_CHUNK, t)
    assert cs == HEAD_DIM and c % GROUP_LANES == 0
    nc = t // cs
    rows = cs * math.gcd(nc, RWKV_PREP_CHUNKS)
    tok = pl.BlockSpec((1, rows, c), lambda d, bi, ci: (bi, ci, 0))
    tok2 = pl.BlockSpec((1, 1, rows, c), lambda d, bi, ci: (d, bi, ci, 0))
    wide = jax.ShapeDtypeStruct((2, b, t, c), F32)
    r2, yl, tt, zz = pl.pallas_call(
        functools.partial(_rwkv_prep_kernel, cs=cs),
        grid=(2, b, t // rows),
        in_specs=[tok, tok, tok, tok2, tok2, tok2],
        out_specs=[tok2, tok2, tok2, tok2],
        out_shape=[wide, wide, wide, wide],
        compiler_params=_cparams("parallel", "parallel", "parallel"),
        name="rwkv7_chunk_prep",
    )(r, v, kk, lw, a, kd)

    fwd = pl.BlockSpec((1, b, cs, c), lambda ci: (0, 0, ci, 0))
    bwd = pl.BlockSpec((1, b, cs, c), lambda ci: (1, 0, nc - 1 - ci, 0))
    st = pl.BlockSpec((2, b, HEAD_DIM, c), lambda ci: (0, 0, 0, 0))
    yf, yb, ht = pl.pallas_call(
        _rwkv_scan_kernel,
        grid=(nc,),
        in_specs=[st, fwd, fwd, fwd, fwd, bwd, bwd, bwd, bwd],
        out_specs=[pl.BlockSpec((1, b, cs, c), lambda ci: (0, 0, ci, 0)),
                   pl.BlockSpec((1, b, cs, c), lambda ci: (0, 0, nc - 1 - ci, 0)), st],
        out_shape=[jax.ShapeDtypeStruct((1, b, t, c), F32), jax.ShapeDtypeStruct((1, b, t, c), F32),
                   jax.ShapeDtypeStruct((2, b, HEAD_DIM, c), F32)],
        scratch_shapes=[pltpu.VMEM((2, b, HEAD_DIM, c), F32)],
        compiler_params=_cparams("arbitrary"),
        name="rwkv7_state_scan",
    )(h0, r2, yl, tt, zz, r2, yl, tt, zz)
    return (yf.reshape(b, t, c), yb.reshape(b, t, c)), ht


def _attn_kernel(sink_ref, q_ref, kp_ref, kc_ref, kn_ref, vp_ref, vc_ref, vn_ref, kx_ref, vx_ref, o_ref,
                 *, local, group):
    i = pl.program_id(1)
    nb = pl.num_programs(1)
    scale = HEAD_DIM ** -0.5
    q = q_ref[0]
    if local:
        k_all = jnp.concatenate([kp_ref[0], kc_ref[0], kn_ref[0], kx_ref[0]], axis=0)
        v_all = jnp.concatenate([vp_ref[0], vc_ref[0], vn_ref[0], vx_ref[0]], axis=0)
        nkeys = k_all.shape[0]
        qi = lax.broadcasted_iota(jnp.int32, (BLOCK, nkeys), 0)
        kj = lax.broadcasted_iota(jnp.int32, (BLOCK, nkeys), 1)
        kpos = kj + (i - 1) * BLOCK
        valid = (jnp.abs(kj - BLOCK - qi) <= WINDOW) & (kpos >= 0) & (kpos < nb * BLOCK)
        valid = valid | (kj >= 3 * BLOCK)
    else:
        k_all = kx_ref[0]
        v_all = vx_ref[0]
    heads = range(q.shape[-1] // HEAD_DIM)
    kv = lambda x, h: x[:, (h // group) * HEAD_DIM:(h // group + 1) * HEAD_DIM]
    s = [_dot_nt(q[:, h * HEAD_DIM:(h + 1) * HEAD_DIM], kv(k_all, h)) * scale for h in heads]
    if local:
        s = [jnp.where(valid, x, MASK_VALUE) for x in s]
    sink = [sink_ref[h] for h in heads]
    m = [jnp.maximum(jnp.max(x, axis=-1, keepdims=True), sk) for x, sk in zip(s, sink)]
    e = [jnp.exp(x - y) for x, y in zip(s, m)]
    den = [jnp.sum(x, axis=-1, keepdims=True) + jnp.exp(sk - y) for x, y, sk in zip(e, m, sink)]
    outs = [_dot(x, kv(v_all, h)) / dn for x, h, dn in zip(e, heads, den)]
    o_ref[0] = jnp.concatenate(outs, axis=1)


def _attention(q, k, v, kx, vx, sink, local):
    b, t, nq = q.shape
    nk = k.shape[-1]
    nb = t // BLOCK
    group = (nq // HEAD_DIM) // (nk // HEAD_DIM)
    lx = kx.shape[1]
    kv = lambda f: pl.BlockSpec((1, BLOCK, nk), f)
    pf = lambda bi, i: (bi, jnp.maximum(i - 1, 0), 0)
    cf = lambda bi, i: (bi, i, 0)
    nf = lambda bi, i: (bi, jnp.minimum(i + 1, nb - 1), 0)
    ctx = pl.BlockSpec((1, lx, nk), lambda bi, i: (bi, 0, 0))
    return pl.pallas_call(
        functools.partial(_attn_kernel, local=local, group=group),
        grid=(b, nb),
        in_specs=[pl.BlockSpec(memory_space=pltpu.SMEM),
                  pl.BlockSpec((1, BLOCK, nq), cf), kv(pf), kv(cf), kv(nf), kv(pf), kv(cf), kv(nf), ctx, ctx],
        out_specs=pl.BlockSpec((1, BLOCK, nq), cf),
        out_shape=jax.ShapeDtypeStruct((b, t, nq), F32),
        compiler_params=_cparams("parallel", "parallel"),
        name="window_attention" if local else "context_attention",
    )(sink, q, k, k, k, v, v, v, kx, vx)


def _even_out_kernel(yf_ref, yb_ref, r_ref, v_ref, g_ref, kd_ref, batt_ref, x_ref, mod_ref, lnw_ref, lnb_ref, rk_ref, e_ref,
                     w_ref, o_ref, *, d, c):
    y = yf_ref[0] + yb_ref[0]
    inv = 1.0 / HEAD_DIM
    mu = _segsum(y, e_ref) * inv
    yc = y - mu
    var = _segsum(yc * yc, e_ref) * inv
    yn = yc * lax.rsqrt(var + RWKV_GN_EPS) * lnw_ref[...] + lnb_ref[...]
    bonus = _segsum(r_ref[0] * (kd_ref[0, 0] + kd_ref[1, 0]) * rk_ref[...], e_ref)
    a_out = (yn + bonus * v_ref[0]) * g_ref[0]
    o = _dot(a_out, w_ref[0:c]) + _dot(batt_ref[0], w_ref[c:])
    o_ref[0] = x_ref[0] + mod_ref[0][:, 2 * d:3 * d] * o


def _even_out(y, r, v, g, kd, batt, x, mod, ep, tabs):
    b, t, d = x.shape
    c = r.shape[-1]
    nq = batt.shape[-1]
    tm = _row_tile(t, 512)
    tok = lambda w: pl.BlockSpec((1, tm, w), lambda bi, i: (bi, i, 0))
    tok2 = lambda w: pl.BlockSpec((2, 1, tm, w), lambda bi, i: (0, bi, i, 0))
    full = lambda *shape: pl.BlockSpec(shape, lambda bi, i: (0,) * len(shape))
    return pl.pallas_call(
        functools.partial(_even_out_kernel, d=d, c=c),
        grid=(b, t // tm),
        in_specs=[tok(c), tok(c), tok(c), tok(c), tok(c), tok2(c), tok(nq), tok(d),
                  pl.BlockSpec((1, 1, mod.shape[-1]), lambda bi, i: (bi, 0, 0)),
                  full(1, c), full(1, c), full(1, c), full(c, c), full(c + nq, d)],
        out_specs=tok(d),
        out_shape=jax.ShapeDtypeStruct((b, t, d), F32),
        compiler_params=_cparams("parallel", "parallel"),
        name="even_out_proj",
    )(y[0], y[1], r, v, g, kd, batt, x, mod, ep['ln_w'], ep['ln_b'], ep['r_k'], tabs['e'], ep['w_out'])


def _filter_kernel(z_ref, t_ref, w1_ref, b1_ref, w2_ref, b2_ref, w3_ref, b3_ref, fr_ref, wo_ref, dl_ref, o_ref, *, d):
    fr = fr_ref[...]
    h = jnp.sin(fr * (_dot_hi(z_ref[...], w1_ref[...]) + b1_ref[...]))
    h = jnp.sin(fr * (_dot_hi(h, w2_ref[...]) + b2_ref[...]))
    h = jnp.sin(fr * (_dot_hi(h, w3_ref[...]) + b3_ref[...]))
    filt = _dot_hi(h, wo_ref[...])
    modu = jnp.exp(-t_ref[...] * dl_ref[...]) + HY_MOD_SHIFT
    for q in range(o_ref.shape[0]):
        o_ref[q] = filt[:, q * d:(q + 1) * d] * modu


def _hyena_filters(n, op, d):
    t = np.linspace(0.0, 1.0, n, dtype=np.float32)[:, None]
    ang = (2.0 * math.pi * np.arange(n, dtype=np.float32)[:, None] / np.float32(n)).astype(np.float32)
    f = np.linspace(1e-4, HY_BANDS - 1, HY_BANDS, dtype=np.float32)[None, :]
    zfeat = jnp.concatenate([jnp.asarray(t), jnp.cos(jnp.asarray(f * ang)), -jnp.sin(jnp.asarray(f * ang))], axis=-1)
    emb_pad = op['f_w1'].shape[0]
    zfeat = jnp.pad(zfeat, ((0, 0), (0, emb_pad - HY_EMB)))
    deltas = np.abs(np.linspace(math.log(HY_TARGET) / HY_SLOW_PCT, math.log(HY_TARGET) / HY_FAST_PCT, d,
                                dtype=np.float32))[None, :]
    tn = min(n, 256)
    nq = 2 * HY_ORDER
    hf = op['f_w2'].shape[0]
    full = lambda *shape: pl.BlockSpec(shape, lambda i: (0,) * len(shape))
    return pl.pallas_call(
        functools.partial(_filter_kernel, d=d),
        grid=(n // tn,),
        in_specs=[pl.BlockSpec((tn, emb_pad), lambda i: (i, 0)), pl.BlockSpec((tn, 1), lambda i: (i, 0)),
                  full(emb_pad, hf), full(1, hf), full(hf, hf), full(1, hf), full(hf, hf), full(1, hf), full(1, hf),
                  full(hf, nq * d), full(1, d)],
        out_specs=pl.BlockSpec((nq, tn, d), lambda i: (0, i, 0)),
        out_shape=jax.ShapeDtypeStruct((nq, n, d), F32),
        compiler_params=_cparams("parallel"),
        name="hyena_filter",
    )(zfeat, jnp.asarray(t), op['f_w1'], op['f_b1'], op['f_w2'], op['f_b2'], op['f_w3'], op['f_b3'], op['f_freq'],
      op['f_out'], jnp.asarray(deltas))


def _dft(n, rows, cols, sign=-1.0):
    k = np.arange(rows, dtype=np.float64)[:, None]
    m = np.arange(cols, dtype=np.float64)[None, :]
    ang = sign * 2.0 * np.pi * ((k * m) % n) / n
    return np.cos(ang), np.sin(ang)


def _stack(re, im):
    return jnp.asarray(np.concatenate([re, im], axis=0).astype(np.float32)).astype(BF16)


def _fft_tables(n_seq):
    n = 2 * n_seq
    n1 = n // FFT_N2
    f1r, f1i = _dft(n1, n1, n1 // 2)
    h1r, h1i = _dft(n1, n1 // 2, n1, sign=1.0)
    k1 = np.arange(n1, dtype=np.float64)[:, None, None]
    k2 = np.arange(FFT_N2, dtype=np.float64)[None, :, None]
    j2 = np.arange(FFT_N2, dtype=np.float64)[None, None, :]
    ang = -2.0 * np.pi * (((k2 * j2 * n1) + k1 * j2) % n) / n
    gr, gi = np.cos(ang), np.sin(ang)
    g_fwd = np.concatenate([gr, gi], axis=1)
    g_inv = np.concatenate([np.swapaxes(gr, 1, 2), np.swapaxes(gi, 1, 2)], axis=1)
    eye = np.eye(SUBLANES)
    kron = lambda m: np.kron(m, eye)
    return {'f1': _stack(kron(f1r), kron(f1i)), 'h1': _stack(kron(h1r), kron(h1i)),
            'g_fwd': jnp.asarray(g_fwd.astype(np.float32)).astype(BF16),
            'g_inv': jnp.asarray(g_inv.astype(np.float32)).astype(BF16), 'n1': n1}


def _dense_tables(n_seq):
    n = 2 * n_seq
    fr, fi = _dft(n, n, n_seq)
    hr, hi = _dft(n, n_seq, n, sign=1.0)
    return {'f': _stack(fr, fi), 'h': _stack(hr, hi)}


def _fft_a_kernel(f_ref, zr_ref, zi_ref, ar_ref, ai_ref, *, n1, cplx):
    f = f_ref[...]
    half, tj, d = zr_ref.shape[1:]
    m = n1 * SUBLANES
    for s0 in range(0, tj, SUBLANES):
        js = slice(s0, s0 + SUBLANES)
        zr = zr_ref[0, :, js, :].reshape(half * SUBLANES, d)
        p = jnp.dot(f, zr.astype(BF16), preferred_element_type=F32)
        if cplx:
            zi = zi_ref[0, :, js, :].reshape(half * SUBLANES, d)
            q = jnp.dot(f, zi.astype(BF16), preferred_element_type=F32)
            ar_ref[0, :, js, :] = (p[0:m] - q[m:]).reshape(n1, SUBLANES, d)
            ai_ref[0, :, js, :] = (p[m:] + q[0:m]).reshape(n1, SUBLANES, d)
        else:
            ar_ref[0, :, js, :] = p[0:m].reshape(n1, SUBLANES, d)
            ai_ref[0, :, js, :] = p[m:].reshape(n1, SUBLANES, d)


def _fft_a(u5, q, tabs, cplx):
    _, s, half, n2, d = u5.shape
    n1 = tabs['n1']
    assert half == n1 // 2 and n2 == FFT_N2
    sp = s // 2 if cplx else s
    tj = 2 * SUBLANES
    zi_map = (lambda si, j: (q, si + sp, 0, j, 0)) if cplx else (lambda si, j: (q, si, 0, j, 0))
    out = pl.BlockSpec((1, n1, tj, d), lambda si, j: (si, 0, j, 0))
    return pl.pallas_call(
        functools.partial(_fft_a_kernel, n1=n1, cplx=cplx),
        grid=(sp, n2 // tj),
        in_specs=[pl.BlockSpec((2 * n1 * SUBLANES, half * SUBLANES), lambda si, j: (0, 0)),
                  pl.BlockSpec((None, 1, half, tj, d), lambda si, j: (q, si, 0, j, 0)),
                  pl.BlockSpec((None, 1, half, tj, d), zi_map)],
        out_specs=[out, out],
        out_shape=[jax.ShapeDtypeStruct((sp, n1, n2, d), F32)] * 2,
        compiler_params=_cparams("parallel", "parallel"),
        name="fft_stage1",
    )(tabs['f1'], u5, u5)


def _cplx_mm(s, xr, xi, conj):
    p = jnp.dot(s, xr.astype(BF16), preferred_element_type=F32)
    q = jnp.dot(s, xi.astype(BF16), preferred_element_type=F32)
    m = s.shape[0] // 2
    if conj:
        return p[0:m] + q[m:], q[0:m] - p[m:]
    return p[0:m] - q[m:], p[m:] + q[0:m]


def _fft_b_kernel(gf_ref, gi_ref, ar_ref, ai_ref, kr_ref, ki_ref, dr_ref, di_ref):
    cr, ci = _cplx_mm(gf_ref[0], ar_ref[0, 0], ai_ref[0, 0], False)
    kr = kr_ref[0]
    ki = ki_ref[0]
    er = cr * kr - ci * ki
    ei = cr * ki + ci * kr
    dr, di = _cplx_mm(gi_ref[0], er, ei, True)
    dr_ref[0, 0] = dr
    di_ref[0, 0] = di


def _fft_b(ar4, ai4, kr, ki, order, tabs):
    sp, n1, _, d = ar4.shape
    blk = pl.BlockSpec((1, 1, FFT_N2, d), lambda k1, si: (si, k1, 0, 0))
    tab = pl.BlockSpec((1, 2 * FFT_N2, FFT_N2), lambda k1, si: (k1, 0, 0))
    kb = pl.BlockSpec((None, 1, FFT_N2, d), lambda k1, si: (order, k1, 0, 0))
    return pl.pallas_call(
        _fft_b_kernel,
        grid=(n1, sp),
        in_specs=[tab, tab, blk, blk, kb, kb],
        out_specs=[blk, blk],
        out_shape=[jax.ShapeDtypeStruct((sp, n1, FFT_N2, d), F32)] * 2,
        compiler_params=_cparams("parallel", "arbitrary"),
        name="fft_stage2_filter",
    )(tabs['g_fwd'], tabs['g_inv'], ar4, ai4, kr, ki)


def _fft_c_kernel(h_ref, dr_ref, di_ref, u_ref, x_ref, sk_ref, o_ref):
    h = h_ref[...]
    n1, tj, d = dr_ref.shape[1:]
    half = n1 // 2
    sk = sk_ref[...]
    for s0 in range(0, tj, SUBLANES):
        js = slice(s0, s0 + SUBLANES)
        yr, yi = _cplx_mm(h, dr_ref[0, :, js, :].reshape(n1 * SUBLANES, d),
                          di_ref[0, :, js, :].reshape(n1 * SUBLANES, d), False)
        for part, y in enumerate((yr, yi)):
            u = u_ref[part, 0, :, js, :]
            o_ref[part, 0, :, js, :] = x_ref[part, 0, :, js, :] * (y.reshape(half, SUBLANES, d) + u * sk)


def _fft_c(dr, di, u6, uq, x6, xq, skip, tabs):
    sp, n1, n2, d = dr.shape
    half = n1 // 2
    tj = 2 * SUBLANES
    pair = lambda q: pl.BlockSpec((None, 2, 1, half, tj, d), lambda si, j: (q, 0, si, 0, j, 0))
    dblk = pl.BlockSpec((1, n1, tj, d), lambda si, j: (si, 0, j, 0))
    out = pl.pallas_call(
        _fft_c_kernel,
        grid=(sp, n2 // tj),
        in_specs=[pl.BlockSpec((2 * half * SUBLANES, n1 * SUBLANES), lambda si, j: (0, 0)), dblk, dblk,
                  pair(uq), pair(xq), pl.BlockSpec((1, d), lambda si, j: (0, 0))],
        out_specs=pl.BlockSpec((2, 1, half, tj, d), lambda si, j: (0, si, 0, j, 0)),
        out_shape=jax.ShapeDtypeStruct((2, sp, half, n2, d), F32),
        compiler_params=_cparams("parallel", "parallel"),
        name="fft_inverse_stage1_gate",
    )(tabs['h1'], dr, di, u6, x6, skip.reshape(1, d))
    return out


def _spec_b_kernel(gf_ref, ar_ref, ai_ref, kr_ref, ki_ref, *, scale):
    fr, fi = _cplx_mm(gf_ref[0], ar_ref[0, 0], ai_ref[0, 0], False)
    gr, gi = _cplx_mm(gf_ref[0], ar_ref[1, 0], ai_ref[1, 0], False)
    kr_ref[0, 0] = (fr + gr) * scale
    ki_ref[0, 0] = (fi - gi) * scale


def _filter_spectrum_fft(filt, tabs, d):
    nq, n, _ = filt.shape
    n1 = tabs['n1']
    ar, ai = _fft_a(filt.reshape(1, nq, n1 // 2, FFT_N2, d), 0, tabs, cplx=False)
    orders = nq // 2
    ar5 = ar.reshape(orders, 2, n1, FFT_N2, d)
    ai5 = ai.reshape(orders, 2, n1, FFT_N2, d)
    blk = pl.BlockSpec((None, 2, 1, FFT_N2, d), lambda o, k1: (o, 0, k1, 0, 0))
    out = pl.BlockSpec((1, 1, FFT_N2, d), lambda o, k1: (o, k1, 0, 0))
    return pl.pallas_call(
        functools.partial(_spec_b_kernel, scale=1.0 / (2 * n)),
        grid=(orders, n1),
        in_specs=[pl.BlockSpec((1, 2 * FFT_N2, FFT_N2), lambda o, k1: (k1, 0, 0)), blk, blk],
        out_specs=[out, out],
        out_shape=[jax.ShapeDtypeStruct((orders, n1, FFT_N2, d), F32)] * 2,
        compiler_params=_cparams("parallel", "parallel"),
        name="filter_spectrum",
    )(tabs['g_fwd'], ar5, ai5)


def _long_conv_fft(u, uq, x, xq, skip, kr, ki, order, tabs):
    _, b, n, d = u.shape
    half = tabs['n1'] // 2
    ar, ai = _fft_a(u.reshape(u.shape[0], b, half, FFT_N2, d), uq, tabs, cplx=True)
    dr, di = _fft_b(ar, ai, kr, ki, order, tabs)
    six = lambda a: a.reshape(a.shape[0], 2, b // 2, half, FFT_N2, d)
    return _fft_c(dr, di, six(u), uq, six(x), xq, skip, tabs).reshape(b, n, d)


def _dense_spec_kernel(f_ref, filt_ref, kr_ref, ki_ref, *, scale):
    f = f_ref[...]
    n = f.shape[0] // 2
    pf = jnp.dot(f, filt_ref[0, 0].astype(BF16), preferred_element_type=F32)
    pg = jnp.dot(f, filt_ref[0, 1].astype(BF16), preferred_element_type=F32)
    kr_ref[0] = (pf[0:n] + pg[0:n]) * scale
    ki_ref[0] = (pf[n:] - pg[n:]) * scale


def _filter_spectrum_dense(filt, tabs, d):
    nq, n, _ = filt.shape
    orders = nq // 2
    f4 = filt.reshape(orders, 2, n, d)
    out = pl.BlockSpec((1, 2 * n, d), lambda o: (o, 0, 0))
    return pl.pallas_call(
        functools.partial(_dense_spec_kernel, scale=1.0 / (2 * n)),
        grid=(orders,),
        in_specs=[pl.BlockSpec((4 * n, n), lambda o: (0, 0)), pl.BlockSpec((1, 2, n, d), lambda o: (o, 0, 0, 0))],
        out_specs=[out, out],
        out_shape=[jax.ShapeDtypeStruct((orders, 2 * n, d), F32)] * 2,
        compiler_params=_cparams("parallel"),
        name="filter_spectrum_dense",
    )(tabs['f'], f4)


def _dense_conv_kernel(f_ref, h_ref, u_ref, x_ref, kr_ref, ki_ref, sk_ref, o_ref):
    cr, ci = _cplx_mm(f_ref[...], u_ref[0, 0], u_ref[1, 0], False)
    kr = kr_ref[...]
    ki = ki_ref[...]
    yr, yi = _cplx_mm(h_ref[...], cr * kr - ci * ki, cr * ki + ci * kr, False)
    sk = sk_ref[...]
    o_ref[0, 0] = x_ref[0, 0] * (yr + u_ref[0, 0] * sk)
    o_ref[1, 0] = x_ref[1, 0] * (yi + u_ref[1, 0] * sk)


def _long_conv_dense(u, x, skip, kr, ki, tabs):
    b, n, d = u.shape
    sp = b // 2
    pair = pl.BlockSpec((2, 1, n, d), lambda si: (0, si, 0, 0))
    kb = pl.BlockSpec((2 * n, d), lambda si: (0, 0))
    out = pl.pallas_call(
        _dense_conv_kernel,
        grid=(sp,),
        in_specs=[pl.BlockSpec((4 * n, n), lambda si: (0, 0)), pl.BlockSpec((2 * n, 2 * n), lambda si: (0, 0)),
                  pair, pair, kb, kb, pl.BlockSpec((1, d), lambda si: (0, 0))],
        out_specs=pair,
        out_shape=jax.ShapeDtypeStruct((2, sp, n, d), F32),
        compiler_params=_cparams("parallel"),
        name="long_conv_dense",
    )(tabs['f'], tabs['h'], u.reshape(2, sp, n, d), x.reshape(2, sp, n, d), kr, ki, skip.reshape(1, d))
    return out.reshape(b, n, d)


def _hyena_mixer(x, g, mod, op, fft_tabs, dense_tabs):
    b, n, d = x.shape
    z = _proj_conv(x, g, mod, op['w_in'], op['b_in'], op['conv_w'], op['conv_b'])
    filt = _hyena_filters(n, op, d)
    if n <= DENSE_FFT_MAX:
        kr, ki = _filter_spectrum_dense(filt, dense_tabs, d)
        y = _long_conv_dense(z[0], z[1], op['skip'][0], kr[0], ki[0], dense_tabs)
        y = _long_conv_dense(y, z[2], op['skip'][1], kr[1], ki[1], dense_tabs)
    else:
        kr, ki = _filter_spectrum_fft(filt, fft_tabs, d)
        y = _long_conv_fft(z, 0, z, 1, op['skip'][0], kr, ki, 0, fft_tabs)
        y = _long_conv_fft(y[None], 0, z, 2, op['skip'][1], kr, ki, 1, fft_tabs)
    return _out_res(y, x, mod, op['w_out'], op['b_out'])


def _even_mixer(x, ctx, g, mod_l, mod_c, ep, tabs, need_ctx):
    p_lat = _proj(x, g, mod_l, ep['w_in'], 0)
    p_ctx = _proj(ctx, g, mod_c, ep['w_in'], 0)
    rc, vc, kkc, gc, lwc, ac, kdc, qc, kac, vac = _even_prep(p_ctx, ep, tabs, rope=False)
    rl, vl, kkl, gl, lwl, al, kdl, ql, kal, val = _even_prep(p_lat, ep, tabs, rope=True)
    b = x.shape[0]
    c = rl.shape[-1]
    h0 = jnp.zeros((2, b, HEAD_DIM, c), F32)
    y_ctx, s_ctx = _rwkv(rc, vc, kkc, lwc, ac, kdc, h0)
    y_lat, _ = _rwkv(rl, vl, kkl, lwl, al, kdl, s_ctx)
    b_lat = _attention(ql, kal, val, kac, vac, ep['sink'], local=True)
    x_new = _even_out(y_lat, rl, vl, gl, kdl, b_lat, x, mod_l, ep, tabs)
    if not need_ctx:
        return x_new, None
    b_ctx = _attention(qc, kac, vac, kac, vac, ep['sink'], local=False)
    ctx_new = _even_out(y_ctx, rc, vc, gc, kdc, b_ctx, ctx, mod_c, ep, tabs)
    return x_new, ctx_new


def _rope_tables(n_tokens):
    rows = n_tokens // GRID_W
    row = jnp.repeat(jnp.arange(rows), GRID_W).astype(F32)
    col = jnp.tile(jnp.arange(GRID_W), rows).astype(F32)
    n_freq = HEAD_DIM // 4
    inv = ROPE_THETA ** (-jnp.arange(n_freq, dtype=F32) / n_freq)
    ang = jnp.concatenate([row[:, None] * inv, col[:, None] * inv], axis=-1)
    cos, sin = jnp.cos(ang), jnp.sin(ang)
    reps = LANES // HEAD_DIM
    cos_t = jnp.tile(jnp.concatenate([cos, cos], axis=-1), (1, reps))
    sin_t = jnp.tile(jnp.concatenate([-sin, sin], axis=-1), (1, reps))
    return cos_t, sin_t


def _block_ones(width):
    idx = np.arange(width) // HEAD_DIM
    return jnp.asarray((idx[:, None] == idx[None, :]).astype(np.float32)).astype(BF16)


def _lora_pad(w):
    z = jnp.zeros_like(w[0])
    return jnp.stack([jnp.concatenate([w[0], z], axis=0), jnp.concatenate([z, w[1]], axis=0)], axis=0)


def kernel(x, c, ctx, c_ctx, ada_w, ada_b, norm1_g, norm2_g, ffn_up, ffn_conv_w, ffn_conv_b, ffn_down, ev_w_in, ev_mu_prev, ev_mu_next, ev_w0, ev_w2, ev_a0, ev_a2, ev_g2, ev_k_k, ev_k_a, ev_r_k, ev_ln_w, ev_ln_b, ev_q_norm, ev_k_norm, ev_sink, ev_w_out, od_w_in, od_b_in, od_conv_w, od_conv_b, od_f_w1, od_f_b1, od_f_w2, od_f_b2, od_f_w3, od_f_b3, od_f_freq, od_f_out, od_skip, od_w_out, od_b_out):
    bsz, seq, d = x.shape
    lc = ctx.shape[1]
    depth = ada_w.shape[0]
    a_width = ev_k_k.shape[-1]
    nq = ev_sink.shape[-1] * HEAD_DIM
    nk = B_KV_HEADS * HEAD_DIM

    cond = jnp.zeros((BF16_ROWS, d), F32).at[:bsz].set(c).at[bsz].set(c_ctx)
    mod = _ada_mod(cond, ada_w, ada_b)

    cos_t, sin_t = _rope_tables(seq)
    tabs = {'cos': cos_t, 'sin': sin_t, 'e': _block_ones(nq)}
    fft_tabs = _fft_tables(seq) if seq > DENSE_FFT_MAX else None
    dense_lat = _dense_tables(seq) if seq <= DENSE_FFT_MAX else None
    fft_ctx = _fft_tables(lc) if lc > DENSE_FFT_MAX else None
    dense_ctx = _dense_tables(lc) if lc <= DENSE_FFT_MAX else None

    for layer in range(depth):
        need_ctx = layer < depth - 1
        even = layer % 2 == 0
        j = layer // 2
        mod_l = mod[layer, :bsz].reshape(bsz, 1, 6 * d)
        mod_c = jnp.broadcast_to(mod[layer, bsz].reshape(1, 1, 6 * d), (bsz, 1, 6 * d))
        if even:
            ep = {'w_in': ev_w_in[j].astype(BF16), 'mu_prev': ev_mu_prev[j][None], 'mu_next': ev_mu_next[j][None],
                  'w0': ev_w0[j], 'w2pad': _lora_pad(ev_w2[j]).astype(BF16), 'a0': ev_a0[j],
                  'a2pad': _lora_pad(ev_a2[j]).astype(BF16), 'g2': ev_g2[j].astype(BF16),
                  'k_k': ev_k_k[j][None], 'k_a': ev_k_a[j][None], 'r_k': ev_r_k[j].reshape(1, a_width),
                  'ln_w': ev_ln_w[j][None], 'ln_b': ev_ln_b[j][None],
                  'q_norm_t': jnp.tile(ev_q_norm[j], nq // HEAD_DIM)[None],
                  'k_norm_t': jnp.tile(ev_k_norm[j], nk // HEAD_DIM)[None],
                  'sink': ev_sink[j], 'w_out': ev_w_out[j].astype(BF16), 'nq': nq, 'nk': nk}
            x, ctx_new = _even_mixer(x, ctx, norm1_g[layer], mod_l, mod_c, ep, tabs, need_ctx)
        else:
            emb_pad = HEAD_DIM
            op = {'w_in': od_w_in[j].astype(BF16), 'b_in': od_b_in[j], 'conv_w': od_conv_w[j],
                  'conv_b': od_conv_b[j],
                  'f_w1': jnp.pad(od_f_w1[j], ((0, emb_pad - HY_EMB), (0, 0))), 'f_b1': od_f_b1[j][None],
                  'f_w2': od_f_w2[j], 'f_b2': od_f_b2[j][None], 'f_w3': od_f_w3[j], 'f_b3': od_f_b3[j][None],
                  'f_freq': od_f_freq[j][None], 'f_out': od_f_out[j], 'skip': od_skip[j],
                  'w_out': od_w_out[j].astype(BF16), 'b_out': od_b_out[j]}
            ctx_new = _hyena_mixer(ctx, norm1_g[layer], mod_c, op, fft_ctx, dense_ctx) if need_ctx else None
            x = _hyena_mixer(x, norm1_g[layer], mod_l, op, fft_tabs, dense_lat)
        w_up = ffn_up[layer].astype(BF16)
        w_down = ffn_down[layer].astype(BF16)
        x = _conv_ffn(x, norm2_g[layer], mod_l, w_up, ffn_conv_w[layer], ffn_conv_b[layer], w_down)
        if need_ctx:
            ctx = _conv_ffn(ctx_new, norm2_g[layer], mod_c, w_up, ffn_conv_w[layer], ffn_conv_b[layer], w_down)
    return x
```

```python
import functools
import math

import numpy as np
import jax
import jax.numpy as jnp
from jax import lax
from jax.experimental import pallas as pl
from jax.experimental.pallas import tpu as pltpu

F32 = jnp.float32
BF16 = jnp.bfloat16

HEAD_DIM = 64
GRID_W = 64
DECAY_LORA = 64
ICLR_LORA = 64
GATE_LORA = 128
RWKV_GN_EPS = 64e-5
B_KV_HEADS = 2
WINDOW = 128
BLOCK = 128
ROPE_THETA = 10000.0
MASK_VALUE = -1e30
HY_ORDER = 2
HY_EMB = 33
HY_BANDS = (HY_EMB - 1) // 2
HY_TARGET = 1e-2
HY_FAST_PCT = 0.3
HY_SLOW_PCT = 1.5
HY_MOD_SHIFT = 0.05
NORM_EPS = 1e-6

V7X_VMEM_BYTES = 64 * 1024 * 1024
VMEM_LIMIT_BYTES = V7X_VMEM_BYTES * 3 // 4
LANES = 128
SUBLANES = 8
BF16_ROWS = 16
RWKV_CHUNK = 64
RWKV_PREP_CHUNKS = 4
RWKV_INV_BASE = 8
RWKV_GRAM_PASSES = 1
RWKV_INV_PASSES = 1
RWKV_REST_PASSES = 1
RWKV_SCAN_PASSES = 2
FFT_N2 = 128
DENSE_FFT_MAX = 512

HIGHEST = lax.Precision.HIGHEST


def _cparams(*sem):
    return pltpu.CompilerParams(dimension_semantics=sem, vmem_limit_bytes=VMEM_LIMIT_BYTES)


def _dot(a, b):
    return jnp.dot(a.astype(BF16), b.astype(BF16), preferred_element_type=F32)


def _dot_nt(a, b):
    return lax.dot_general(a.astype(BF16), b.astype(BF16), (((1,), (1,)), ((), ())), preferred_element_type=F32)


def _dot_tn(a, b):
    return lax.dot_general(a.astype(BF16), b.astype(BF16), (((0,), (0,)), ((), ())), preferred_element_type=F32)


def _dot_hi(a, b):
    return jnp.dot(a, b, preferred_element_type=F32, precision=HIGHEST)


def _dot_nt_hi(a, b):
    return lax.dot_general(a, b, (((1,), (1,)), ((), ())), preferred_element_type=F32, precision=HIGHEST)


def _dot_tn_hi(a, b):
    return lax.dot_general(a, b, (((0,), (0,)), ((), ())), preferred_element_type=F32, precision=HIGHEST)


def _segsum(x, e_ref):
    hi = x.astype(BF16)
    lo = (x - hi.astype(F32)).astype(BF16)
    e = e_ref[...]
    return jnp.dot(hi, e, preferred_element_type=F32) + jnp.dot(lo, e, preferred_element_type=F32)


def _norm_mod(x, g, shift, scale):
    xn = x * lax.rsqrt(jnp.mean(x * x, axis=-1, keepdims=True) + NORM_EPS)
    return (xn * g) * (1.0 + scale) + shift


def _row_tile(t, want):
    tm = min(t, want)
    assert t % tm == 0 and tm % BF16_ROWS == 0
    return tm


def _ada_kernel(c_ref, w_ref, b_ref, o_ref):
    c = c_ref[...]
    s = c * jax.nn.sigmoid(c)
    o_ref[0] = _dot(s, w_ref[0]) + b_ref[0]


def _ada_mod(cond, ada_w, ada_b):
    depth, d, n = ada_w.shape
    tn = n // 4
    rows = cond.shape[0]
    return pl.pallas_call(
        _ada_kernel,
        grid=(depth, n // tn),
        in_specs=[pl.BlockSpec((rows, d), lambda l, j: (0, 0)),
                  pl.BlockSpec((1, d, tn), lambda l, j: (l, 0, j)),
                  pl.BlockSpec((1, 1, tn), lambda l, j: (l, 0, j))],
        out_specs=pl.BlockSpec((1, rows, tn), lambda l, j: (l, 0, j)),
        out_shape=jax.ShapeDtypeStruct((depth, rows, n), F32),
        compiler_params=_cparams("parallel", "parallel"),
        name="ada_mod",
    )(cond, ada_w, ada_b.reshape(depth, 1, n))


def _proj_kernel(x_ref, g_ref, mod_ref, w_ref, o_ref, h_scr, *, d, slot):
    @pl.when(pl.program_id(2) == 0)
    def _():
        m = mod_ref[0]
        h = _norm_mod(x_ref[0], g_ref[...], m[:, slot * d:(slot + 1) * d], m[:, (slot + 1) * d:(slot + 2) * d])
        h_scr[...] = h.astype(BF16)

    o_ref[0] = jnp.dot(h_scr[...], w_ref[...], preferred_element_type=F32)


def _proj(x, g, mod, w, slot):
    b, t, d = x.shape
    n = w.shape[1]
    tm = _row_tile(t, 512)
    tn = 128 * math.gcd(n // 128, 7 if n % 896 == 0 else 4)
    return pl.pallas_call(
        functools.partial(_proj_kernel, d=d, slot=slot),
        grid=(b, t // tm, n // tn),
        in_specs=[pl.BlockSpec((1, tm, d), lambda bi, i, j: (bi, i, 0)),
                  pl.BlockSpec((1, d), lambda bi, i, j: (0, 0)),
                  pl.BlockSpec((1, 1, mod.shape[-1]), lambda bi, i, j: (bi, 0, 0)),
                  pl.BlockSpec((d, tn), lambda bi, i, j: (0, j))],
        out_specs=pl.BlockSpec((1, tm, tn), lambda bi, i, j: (bi, i, j)),
        out_shape=jax.ShapeDtypeStruct((b, t, n), F32),
        scratch_shapes=[pltpu.VMEM((tm, d), BF16)],
        compiler_params=_cparams("parallel", "parallel", "arbitrary"),
        name="norm_mod_proj",
    )(x, g.reshape(1, d), mod, w)


def _halo_specs(tm, t, d, nargs):
    r = tm // BF16_ROWS
    last = t // BF16_ROWS - 1
    if nargs == 3:
        prev = pl.BlockSpec((1, BF16_ROWS, d), lambda bi, i, j: (bi, jnp.maximum(i * r - 1, 0), 0))
        nxt = pl.BlockSpec((1, BF16_ROWS, d), lambda bi, i, j: (bi, jnp.minimum((i + 1) * r, last), 0))
    else:
        prev = pl.BlockSpec((1, BF16_ROWS, d), lambda bi, i: (bi, jnp.maximum(i * r - 1, 0), 0))
        nxt = pl.BlockSpec((1, BF16_ROWS, d), lambda bi, i: (bi, jnp.minimum((i + 1) * r, last), 0))
    return prev, nxt


def _fill_h(x_ref, xp_ref, xn_ref, g, shift, scale, h_scr, tm):
    h_scr[0:BF16_ROWS] = _norm_mod(xp_ref[0], g, shift, scale).astype(BF16)
    h_scr[BF16_ROWS:BF16_ROWS + tm] = _norm_mod(x_ref[0], g, shift, scale).astype(BF16)
    h_scr[BF16_ROWS + tm:2 * BF16_ROWS + tm] = _norm_mod(xn_ref[0], g, shift, scale).astype(BF16)


def _conv3_rows(u, u_scr, cw_ref, cb_ref, tm, t_total):
    i = pl.program_id(1)
    o = BF16_ROWS
    u_scr[0:o] = jnp.where(i > 0, u[0:o], 0.0)
    u_scr[o:o + tm] = u[o:o + tm]
    u_scr[o + tm:] = jnp.where(i < t_total // tm - 1, u[o + tm:], 0.0)
    cw = cw_ref[...]
    return (u_scr[o - 1:o - 1 + tm] * cw[0:1] + u_scr[o:o + tm] * cw[1:2] + u_scr[o + 1:o + 1 + tm] * cw[2:3]
            + cb_ref[...])


def _proj_conv_kernel(x_ref, xp_ref, xn_ref, g_ref, mod_ref, w_ref, b_ref, cw_ref, cb_ref, o_ref, h_scr, u_scr,
                      *, d, tm, t_total):
    @pl.when(pl.program_id(2) == 0)
    def _():
        m = mod_ref[0]
        _fill_h(x_ref, xp_ref, xn_ref, g_ref[...], m[:, 0:d], m[:, d:2 * d], h_scr, tm)

    u = jnp.dot(h_scr[...], w_ref[...], preferred_element_type=F32) + b_ref[...]
    o_ref[0, 0] = _conv3_rows(u, u_scr, cw_ref, cb_ref, tm, t_total)


def _proj_conv(x, g, mod, w, bias, cw, cb):
    b, t, d = x.shape
    n = w.shape[1]
    tm = _row_tile(t, 512)
    tn = d
    prev, nxt = _halo_specs(tm, t, d, 3)
    return pl.pallas_call(
        functools.partial(_proj_conv_kernel, d=d, tm=tm, t_total=t),
        grid=(b, t // tm, n // tn),
        in_specs=[pl.BlockSpec((1, tm, d), lambda bi, i, j: (bi, i, 0)), prev, nxt,
                  pl.BlockSpec((1, d), lambda bi, i, j: (0, 0)),
                  pl.BlockSpec((1, 1, mod.shape[-1]), lambda bi, i, j: (bi, 0, 0)),
                  pl.BlockSpec((d, tn), lambda bi, i, j: (0, j)),
                  pl.BlockSpec((1, tn), lambda bi, i, j: (0, j)),
                  pl.BlockSpec((3, tn), lambda bi, i, j: (0, j)),
                  pl.BlockSpec((1, tn), lambda bi, i, j: (0, j))],
        out_specs=pl.BlockSpec((1, 1, tm, tn), lambda bi, i, j: (j, bi, i, 0)),
        out_shape=jax.ShapeDtypeStruct((n // tn, b, t, tn), F32),
        scratch_shapes=[pltpu.VMEM((tm + 2 * BF16_ROWS, d), BF16), pltpu.VMEM((tm + 2 * BF16_ROWS, tn), F32)],
        compiler_params=_cparams("parallel", "parallel", "arbitrary"),
        name="hyena_in_proj",
    )(x, x, x, g.reshape(1, d), mod, w, bias.reshape(1, n), cw, cb.reshape(1, n))


def _ffn_kernel(x_ref, xp_ref, xn_ref, g_ref, mod_ref, wg_ref, wv_ref, cwg_ref, cwv_ref, cbg_ref, cbv_ref, wd_ref,
                o_ref, h_scr, ug_scr, uv_scr, acc_scr, *, d, tm, t_total):
    j = pl.program_id(2)

    @pl.when(j == 0)
    def _():
        m = mod_ref[0]
        _fill_h(x_ref, xp_ref, xn_ref, g_ref[...], m[:, 3 * d:4 * d], m[:, 4 * d:5 * d], h_scr, tm)
        acc_scr[...] = jnp.zeros_like(acc_scr)

    h = h_scr[...]
    gate = _conv3_rows(jnp.dot(h, wg_ref[...], preferred_element_type=F32), ug_scr, cwg_ref, cbg_ref, tm, t_total)
    val = _conv3_rows(jnp.dot(h, wv_ref[...], preferred_element_type=F32), uv_scr, cwv_ref, cbv_ref, tm, t_total)
    act = (gate * jax.nn.sigmoid(gate)) * val
    acc_scr[...] += jnp.dot(act.astype(BF16), wd_ref[...], preferred_element_type=F32)

    @pl.when(j == pl.num_programs(2) - 1)
    def _():
        o_ref[0] = x_ref[0] + mod_ref[0][:, 5 * d:6 * d] * acc_scr[...]


def _conv_ffn(x, g, mod, w_up, cw, cb, w_down):
    b, t, d = x.shape
    f = w_down.shape[0]
    tm = _row_tile(t, 1024)
    tf = 256 if f % 256 == 0 else 128
    nf = f // tf
    prev, nxt = _halo_specs(tm, t, d, 3)
    cb2 = cb.reshape(1, 2 * f)
    return pl.pallas_call(
        functools.partial(_ffn_kernel, d=d, tm=tm, t_total=t),
        grid=(b, t // tm, nf),
        in_specs=[pl.BlockSpec((1, tm, d), lambda bi, i, j: (bi, i, 0)), prev, nxt,
                  pl.BlockSpec((1, d), lambda bi, i, j: (0, 0)),
                  pl.BlockSpec((1, 1, mod.shape[-1]), lambda bi, i, j: (bi, 0, 0)),
                  pl.BlockSpec((d, tf), lambda bi, i, j: (0, j)),
                  pl.BlockSpec((d, tf), lambda bi, i, j: (0, nf + j)),
                  pl.BlockSpec((3, tf), lambda bi, i, j: (0, j)),
                  pl.BlockSpec((3, tf), lambda bi, i, j: (0, nf + j)),
                  pl.BlockSpec((1, tf), lambda bi, i, j: (0, j)),
                  pl.BlockSpec((1, tf), lambda bi, i, j: (0, nf + j)),
                  pl.BlockSpec((tf, d), lambda bi, i, j: (j, 0))],
        out_specs=pl.BlockSpec((1, tm, d), lambda bi, i, j: (bi, i, 0)),
        out_shape=jax.ShapeDtypeStruct((b, t, d), F32),
        scratch_shapes=[pltpu.VMEM((tm + 2 * BF16_ROWS, d), BF16),
                        pltpu.VMEM((tm + 2 * BF16_ROWS, tf), F32),
                        pltpu.VMEM((tm + 2 * BF16_ROWS, tf), F32),
                        pltpu.VMEM((tm, d), F32)],
        compiler_params=_cparams("parallel", "parallel", "arbitrary"),
        name="conv_ffn",
    )(x, x, x, g.reshape(1, d), mod, w_up, w_up, cw, cw, cb2, cb2, w_down)


def _out_res_kernel(y_ref, x_ref, mod_ref, w_ref, b_ref, o_ref, *, d):
    o = _dot(y_ref[0], w_ref[...]) + b_ref[...]
    o_ref[0] = x_ref[0] + mod_ref[0][:, 2 * d:3 * d] * o


def _out_res(y, x, mod, w, bias):
    b, t, d = x.shape
    tm = _row_tile(t, 512)
    return pl.pallas_call(
        functools.partial(_out_res_kernel, d=d),
        grid=(b, t // tm),
        in_specs=[pl.BlockSpec((1, tm, d), lambda bi, i: (bi, i, 0)),
                  pl.BlockSpec((1, tm, d), lambda bi, i: (bi, i, 0)),
                  pl.BlockSpec((1, 1, mod.shape[-1]), lambda bi, i: (bi, 0, 0)),
                  pl.BlockSpec((d, d), lambda bi, i: (0, 0)),
                  pl.BlockSpec((1, d), lambda bi, i: (0, 0))],
        out_specs=pl.BlockSpec((1, tm, d), lambda bi, i: (bi, i, 0)),
        out_shape=jax.ShapeDtypeStruct((b, t, d), F32),
        compiler_params=_cparams("parallel", "parallel"),
        name="out_proj_residual",
    )(y, x, mod, w, bias.reshape(1, d))


def _softplus(z):
    return jnp.maximum(z, 0.0) + jnp.log1p(jnp.exp(-jnp.abs(z)))


def _even_prep_kernel(p_ref, pp_ref, pn_ref, mup_ref, mun_ref, w0_ref, w2_ref, a0_ref, a2_ref, g2_ref, kk_ref,
                      ka_ref, qn_ref, kn_ref, cos_ref, sin_ref, e_ref,
                      r_ref, v_ref, kkn_ref, g_ref, lw_ref, a_ref, kd_ref, q_ref, katt_ref, vatt_ref,
                      *, tm, c, a_in, rope):
    i = pl.program_id(1)
    nt = pl.num_programs(1)
    pa = p_ref[0][:, 0:a_in]
    rows = lax.broadcasted_iota(jnp.int32, (tm, 1), 0)
    prev_row = jnp.where(i > 0, pp_ref[0][BF16_ROWS - 1:BF16_ROWS, 0:a_in], 0.0)
    next_row = jnp.where(i < nt - 1, pn_ref[0][0:1, 0:a_in], 0.0)
    prev = jnp.where(rows == 0, prev_row, pltpu.roll(pa, 1, 0))
    nxt = jnp.where(rows == tm - 1, next_row, pltpu.roll(pa, tm - 1, 0))
    za = pa + mup_ref[...] * (prev - pa) + mun_ref[...] * (nxt - pa)

    r = za[:, 0:c]
    k = za[:, c:2 * c]
    v = za[:, 2 * c:3 * c]
    wd = jnp.tanh(za[:, 3 * c:3 * c + 2 * DECAY_LORA])
    ad = za[:, 3 * c + 2 * DECAY_LORA:3 * c + 2 * DECAY_LORA + 2 * ICLR_LORA]
    gd = za[:, 3 * c + 2 * DECAY_LORA + 2 * ICLR_LORA:a_in]
    r_ref[0] = r
    v_ref[0] = v
    g_ref[0] = _dot(jax.nn.sigmoid(gd), g2_ref[...])
    kkv = k * kk_ref[...]
    ss = _segsum(kkv * kkv, e_ref)
    kkn_ref[0] = kkv / jnp.maximum(jnp.sqrt(ss), 1e-12)
    for dd in range(2):
        w_log = -_softplus(-(w0_ref[dd:dd + 1] + _dot(wd, w2_ref[dd]))) - 0.5
        lw_ref[dd, 0] = -jnp.exp(w_log)
        a = jax.nn.sigmoid(a0_ref[dd:dd + 1] + _dot(ad, a2_ref[dd]))
        a_ref[dd, 0] = a
        kd_ref[dd, 0] = k * (1.0 + (a - 1.0) * ka_ref[...])

    pb = p_ref[0][:, a_in:]
    nq = q_ref.shape[-1]
    nk = katt_ref.shape[-1]
    q = pb[:, 0:nq]
    kat = pb[:, nq:nq + nk]
    vatt_ref[0] = pb[:, nq + nk:]
    half = HEAD_DIM // 2

    def norm_rope(x, gain, e):
        w = x.shape[-1]
        ms = _segsum(x * x, e) * (1.0 / HEAD_DIM)
        xn = x * lax.rsqrt(ms + NORM_EPS) * gain
        if not rope:
            return xn
        lane = lax.broadcasted_iota(jnp.int32, (1, w), 1)
        first = (lane % HEAD_DIM) < half
        swapped = jnp.where(first, pltpu.roll(xn, w - half, 1), pltpu.roll(xn, half, 1))
        reps = w // LANES
        cos = jnp.concatenate([cos_ref[...]] * reps, axis=1) if reps > 1 else cos_ref[...]
        sin = jnp.concatenate([sin_ref[...]] * reps, axis=1) if reps > 1 else sin_ref[...]
        return xn * cos + swapped * sin

    q_ref[0] = norm_rope(q, qn_ref[...], e_ref[...])
    katt_ref[0] = norm_rope(kat, kn_ref[...], e_ref[0:nk, 0:nk])


def _even_prep(p, ep, tabs, rope):
    b, t, n = p.shape
    c = ep['k_k'].shape[-1]
    a_in = ep['mu_prev'].shape[-1]
    nq = ep['nq']
    nk = ep['nk']
    tm = _row_tile(t, 256)
    prev, nxt = _halo_specs(tm, t, n, 2)
    full = lambda *shape: pl.BlockSpec(shape, lambda bi, i: (0,) * len(shape))
    tok = lambda w: pl.BlockSpec((1, tm, w), lambda bi, i: (bi, i, 0))
    tok2 = lambda w: pl.BlockSpec((2, 1, tm, w), lambda bi, i: (0, bi, i, 0))
    sd = lambda *shape: jax.ShapeDtypeStruct(shape, F32)
    return pl.pallas_call(
        functools.partial(_even_prep_kernel, tm=tm, c=c, a_in=a_in, rope=rope),
        grid=(b, t // tm),
        in_specs=[tok(n), prev, nxt, full(1, a_in), full(1, a_in), full(2, c), full(2, 2 * DECAY_LORA, c),
                  full(2, c), full(2, 2 * ICLR_LORA, c), full(GATE_LORA, c), full(1, c), full(1, c),
                  full(1, nq), full(1, nk),
                  pl.BlockSpec((tm, LANES), lambda bi, i: (i, 0)), pl.BlockSpec((tm, LANES), lambda bi, i: (i, 0)),
                  full(nq, nq)],
        out_specs=[tok(c), tok(c), tok(c), tok(c), tok2(c), tok2(c), tok2(c), tok(nq), tok(nk), tok(nk)],
        out_shape=[sd(b, t, c), sd(b, t, c), sd(b, t, c), sd(b, t, c), sd(2, b, t, c), sd(2, b, t, c),
                   sd(2, b, t, c), sd(b, t, nq), sd(b, t, nk), sd(b, t, nk)],
        compiler_params=_cparams("parallel", "parallel"),
        name="even_prep",
    )(p, p, p, ep['mu_prev'], ep['mu_next'], ep['w0'], ep['w2pad'], ep['a0'], ep['a2pad'], ep['g2'], ep['k_k'],
      ep['k_a'], ep['q_norm_t'], ep['k_norm_t'], tabs['cos'][:t], tabs['sin'][:t], tabs['e'])


HEADS_PER_GROUP = 4
GROUP_LANES = HEADS_PER_GROUP * HEAD_DIM


def _split_bf16(x):
    hi = x.astype(BF16)
    return hi, (x - hi.astype(F32)).astype(BF16)


def _block_diag(x, bmask):
    return jnp.where(bmask, jnp.concatenate([x] * HEADS_PER_GROUP, axis=0), jnp.zeros((), x.dtype))


def _head_mm(lhs, rhs, bmask, passes, nt=False):
    dn = (((1,), (1,)), ((), ())) if nt else (((1,), (0,)), ((), ()))
    if passes == 1:
        return lax.dot_general(lhs.astype(BF16), _block_diag(rhs.astype(BF16), bmask), dn, preferred_element_type=F32)
    lh, ll = _split_bf16(lhs)
    rh, rl = _split_bf16(rhs)
    m = lhs.shape[0]
    top = lax.dot_general(jnp.concatenate([lh, ll], axis=0), _block_diag(rh, bmask), dn, preferred_element_type=F32)
    return top[0:m] + top[m:] + lax.dot_general(lh, _block_diag(rl, bmask), dn, preferred_element_type=F32)


def _head_mm_tn(lhs, rhs, lane_head, passes):
    dn = (((0,), (0,)), ((), ()))
    if passes == 1:
        full = lax.dot_general(lhs.astype(BF16), rhs.astype(BF16), dn, preferred_element_type=F32)
    else:
        lh, ll = _split_bf16(lhs)
        rh, rl = _split_bf16(rhs)
        full = (lax.dot_general(jnp.concatenate([lh, ll], axis=0), jnp.concatenate([rh, rh], axis=0), dn,
                                preferred_element_type=F32)
                + lax.dot_general(lh, rl, dn, preferred_element_type=F32))
    out = jnp.where(lane_head == 0, full[0:HEAD_DIM], 0.0)
    for h in range(1, HEADS_PER_GROUP):
        out = out + jnp.where(lane_head == h, full[h * HEAD_DIM:(h + 1) * HEAD_DIM], 0.0)
    return out


def _group_masks():
    r = lax.broadcasted_iota(jnp.int32, (GROUP_LANES, GROUP_LANES), 0)
    c = lax.broadcasted_iota(jnp.int32, (GROUP_LANES, GROUP_LANES), 1)
    bmask = (r // HEAD_DIM) == (c // HEAD_DIM)
    lane_head = lax.broadcasted_iota(jnp.int32, (1, GROUP_LANES), 1) // HEAD_DIM
    return bmask, lane_head


def _rwkv_prep_kernel(r_ref, v_ref, kk_ref, lw_ref, a_ref, kd_ref, r2_ref, yl_ref, t_ref, z_ref, *, cs):
    sgn = 1 - 2 * pl.program_id(0)
    row = lax.broadcasted_iota(jnp.int32, (cs, cs), 0)
    col = lax.broadcasted_iota(jnp.int32, (cs, cs), 1)
    tri = (((row - col) * sgn) >= 0).astype(F32)
    bmask, lane_head = _group_masks()
    t_idx = lax.broadcasted_iota(jnp.int32, (cs, GROUP_LANES), 0)
    s_idx = lax.broadcasted_iota(jnp.int32, (cs, GROUP_LANES), 1) % HEAD_DIM
    diff = (t_idx - s_idx) * sgn
    strict = diff > 0
    incl = diff >= 0
    eye = (diff == 0).astype(F32)
    base = min(RWKV_INV_BASE, cs)
    same_base = (t_idx // base) == (s_idx // base)
    groups = r_ref.shape[-1] // GROUP_LANES

    ch = []
    for c0 in range(0, r_ref.shape[1], cs):
        rows = slice(c0, c0 + cs)
        r, v, kk = r_ref[0, rows], v_ref[0, rows], kk_ref[0, rows]
        lw, a, kd = lw_ref[0, 0, rows], a_ref[0, 0, rows], kd_ref[0, 0, rows]
        g = _dot_hi(tri, lw)
        gp = g - lw
        gref = g[cs // 2:cs // 2 + 1]
        gend = jnp.sum(lw, axis=0, keepdims=True)
        bvec = kk * a
        full = {'a_t': -kk * jnp.exp(gp - gref), 'a_0': -kk * jnp.exp(gp),
                'b_t': bvec * jnp.exp(gref - g), 'b_e': bvec * jnp.exp(gend - g),
                'k_t': kd * jnp.exp(gref - g), 'k_e': kd * jnp.exp(gend - g),
                'r_t': r * jnp.exp(g - gref), 'r_0': r * jnp.exp(g), 'v': v,
                'wend': jnp.broadcast_to(jnp.exp(gend), (cs, v.shape[-1]))}
        for gi in range(groups):
            sl = slice(gi * GROUP_LANES, (gi + 1) * GROUP_LANES)
            c = {k: x[:, sl] for k, x in full.items()}
            c['rows'], c['sl'] = rows, sl
            ch.append(c)

    mm = lambda x, y, passes, nt=False: [_head_mm(p_, q_, bmask, passes, nt) for p_, q_ in zip(x, y)]
    get = lambda k: [c[k] for c in ch]
    ar = [jnp.concatenate([c['a_t'], c['r_t']], axis=0) for c in ch]
    gb = mm(ar, get('b_t'), RWKV_GRAM_PASSES, True)
    gk = mm(ar, get('k_t'), RWKV_GRAM_PASSES, True)
    nmat = [jnp.where(strict, x[0:cs], 0.0) for x in gb]
    g_ak = [jnp.where(strict, x[0:cs], 0.0) for x in gk]
    g_rb = [jnp.where(incl, x[cs:], 0.0) for x in gb]
    g_rk = [jnp.where(incl, x[cs:], 0.0) for x in gk]
    npow = [jnp.where(same_base, x, 0.0) for x in nmat]
    p = [eye + x for x in npow]
    for _ in range(int(math.log2(base)) - 1):
        npow = mm(npow, npow, RWKV_INV_PASSES)
        p = [x + y for x, y in zip(p, mm(npow, p, RWKV_INV_PASSES))]
    m = base
    while m < cs:
        off = ((t_idx // m) != (s_idx // m)) & ((t_idx // (2 * m)) == (s_idx // (2 * m)))
        q = mm([jnp.where(off, x, 0.0) for x in nmat], p, RWKV_INV_PASSES)
        p = [x + y for x, y in zip(p, mm(p, q, RWKV_INV_PASSES))]
        m *= 2
    vg = get('v')
    a2 = mm(p, get('a_0'), RWKV_REST_PASSES)
    u_v = mm(p, mm(g_ak, vg, RWKV_REST_PASSES), RWKV_REST_PASSES)
    r2 = mm(g_rb, a2, RWKV_REST_PASSES)
    yl1 = mm(g_rb, u_v, RWKV_REST_PASSES)
    yl2 = mm(g_rk, vg, RWKV_REST_PASSES)
    tt = [_head_mm_tn(x, c['b_e'], lane_head, RWKV_REST_PASSES) for x, c in zip(a2, ch)]
    zz = [_head_mm_tn(jnp.concatenate([u, c['v']], axis=0), jnp.concatenate([c['b_e'], c['k_e']], axis=0),
                      lane_head, RWKV_REST_PASSES) for u, c in zip(u_v, ch)]
    for i, c in enumerate(ch):
        rows, sl = c['rows'], c['sl']
        r2_ref[0, 0, rows, sl] = c['r_0'] + r2[i]
        yl_ref[0, 0, rows, sl] = yl1[i] + yl2[i]
        t_ref[0, 0, rows, sl] = eye * c['wend'] + tt[i]
        z_ref[0, 0, rows, sl] = zz[i]


def _rwkv_scan_kernel(h0_ref, r2f_ref, ylf_ref, tf_ref, zf_ref, r2b_ref, ylb_ref, tb_ref, zb_ref,
                      yf_ref, yb_ref, ht_ref, s_scr):
    ci = pl.program_id(0)

    @pl.when(ci == 0)
    def _():
        s_scr[...] = h0_ref[...]

    bmask, _ = _group_masks()
    ins = ((r2f_ref, ylf_ref, tf_ref, zf_ref, yf_ref), (r2b_ref, ylb_ref, tb_ref, zb_ref, yb_ref))
    for d, (r2_ref, yl_ref, t_ref, z_ref, y_ref) in enumerate(ins):
        for b in range(s_scr.shape[1]):
            for gi in range(s_scr.shape[-1] // GROUP_LANES):
                sl = slice(gi * GROUP_LANES, (gi + 1) * GROUP_LANES)
                s = s_scr[d, b, :, sl]
                y_ref[0, b, :, sl] = yl_ref[0, b, :, sl] + _head_mm(r2_ref[0, b, :, sl], s, bmask, 1, nt=True)
                s_scr[d, b, :, sl] = (_head_mm(s, t_ref[0, b, :, sl], bmask, RWKV_SCAN_PASSES)
                                      + z_ref[0, b, :, sl])

    @pl.when(ci == pl.num_programs(0) - 1)
    def _():
        ht_ref[...] = s_scr[...]


def _rwkv(r, v, kk, lw, a, kd, h0):
    b, t, c = r.shape
    cs = min(RWKV_CHUNK, t)
    assert cs == HEAD_DIM and c % GROUP_LANES == 0
    nc = t // cs
    rows = cs * math.gcd(nc, RWKV_PREP_CHUNKS)
    tok = pl.BlockSpec((1, rows, c), lambda d, bi, ci: (bi, ci, 0))
    tok2 = pl.BlockSpec((1, 1, rows, c), lambda d, bi, ci: (d, bi, ci, 0))
    wide = jax.ShapeDtypeStruct((2, b, t, c), F32)
    r2, yl, tt, zz = pl.pallas_call(
        functools.partial(_rwkv_prep_kernel, cs=cs),
        grid=(2, b, t // rows),
        in_specs=[tok, tok, tok, tok2, tok2, tok2],
        out_specs=[tok2, tok2, tok2, tok2],
        out_shape=[wide, wide, wide, wide],
        compiler_params=_cparams("parallel", "parallel", "parallel"),
        name="rwkv7_chunk_prep",
    )(r, v, kk, lw, a, kd)

    fwd = pl.BlockSpec((1, b, cs, c), lambda ci: (0, 0, ci, 0))
    bwd = pl.BlockSpec((1, b, cs, c), lambda ci: (1, 0, nc - 1 - ci, 0))
    st = pl.BlockSpec((2, b, HEAD_DIM, c), lambda ci: (0, 0, 0, 0))
    yf, yb, ht = pl.pallas_call(
        _rwkv_scan_kernel,
        grid=(nc,),
        in_specs=[st, fwd, fwd, fwd, fwd, bwd, bwd, bwd, bwd],
        out_specs=[pl.BlockSpec((1, b, cs, c), lambda ci: (0, 0, ci, 0)),
                   pl.BlockSpec((1, b, cs, c), lambda ci: (0, 0, nc - 1 - ci, 0)), st],
        out_shape=[jax.ShapeDtypeStruct((1, b, t, c), F32), jax.ShapeDtypeStruct((1, b, t, c), F32),
                   jax.ShapeDtypeStruct((2, b, HEAD_DIM, c), F32)],
        scratch_shapes=[pltpu.VMEM((2, b, HEAD_DIM, c), F32)],
        compiler_params=_cparams("arbitrary"),
        name="rwkv7_state_scan",
    )(h0, r2, yl, tt, zz, r2, yl, tt, zz)
    return (yf.reshape(b, t, c), yb.reshape(b, t, c)), ht


def _attn_kernel(sink_ref, q_ref, kp_ref, kc_ref, kn_ref, vp_ref, vc_ref, vn_ref, kx_ref, vx_ref, o_ref,
                 *, local, group):
    i = pl.program_id(1)
    nb = pl.num_programs(1)
    scale = HEAD_DIM ** -0.5
    q = q_ref[0]
    if local:
        k_all = jnp.concatenate([kp_ref[0], kc_ref[0], kn_ref[0], kx_ref[0]], axis=0)
        v_all = jnp.concatenate([vp_ref[0], vc_ref[0], vn_ref[0], vx_ref[0]], axis=0)
        nkeys = k_all.shape[0]
        qi = lax.broadcasted_iota(jnp.int32, (BLOCK, nkeys), 0)
        kj = lax.broadcasted_iota(jnp.int32, (BLOCK, nkeys), 1)
        kpos = kj + (i - 1) * BLOCK
        valid = (jnp.abs(kj - BLOCK - qi) <= WINDOW) & (kpos >= 0) & (kpos < nb * BLOCK)
        valid = valid | (kj >= 3 * BLOCK)
    else:
        k_all = kx_ref[0]
        v_all = vx_ref[0]
    heads = range(q.shape[-1] // HEAD_DIM)
    kv = lambda x, h: x[:, (h // group) * HEAD_DIM:(h // group + 1) * HEAD_DIM]
    s = [_dot_nt(q[:, h * HEAD_DIM:(h + 1) * HEAD_DIM], kv(k_all, h)) * scale for h in heads]
    if local:
        s = [jnp.where(valid, x, MASK_VALUE) for x in s]
    sink = [sink_ref[h] for h in heads]
    m = [jnp.maximum(jnp.max(x, axis=-1, keepdims=True), sk) for x, sk in zip(s, sink)]
    e = [jnp.exp(x - y) for x, y in zip(s, m)]
    den = [jnp.sum(x, axis=-1, keepdims=True) + jnp.exp(sk - y) for x, y, sk in zip(e, m, sink)]
    outs = [_dot(x, kv(v_all, h)) / dn for x, h, dn in zip(e, heads, den)]
    o_ref[0] = jnp.concatenate(outs, axis=1)


def _attention(q, k, v, kx, vx, sink, local):
    b, t, nq = q.shape
    nk = k.shape[-1]
    nb = t // BLOCK
    group = (nq // HEAD_DIM) // (nk // HEAD_DIM)
    lx = kx.shape[1]
    kv = lambda f: pl.BlockSpec((1, BLOCK, nk), f)
    pf = lambda bi, i: (bi, jnp.maximum(i - 1, 0), 0)
    cf = lambda bi, i: (bi, i, 0)
    nf = lambda bi, i: (bi, jnp.minimum(i + 1, nb - 1), 0)
    ctx = pl.BlockSpec((1, lx, nk), lambda bi, i: (bi, 0, 0))
    return pl.pallas_call(
        functools.partial(_attn_kernel, local=local, group=group),
        grid=(b, nb),
        in_specs=[pl.BlockSpec(memory_space=pltpu.SMEM),
                  pl.BlockSpec((1, BLOCK, nq), cf), kv(pf), kv(cf), kv(nf), kv(pf), kv(cf), kv(nf), ctx, ctx],
        out_specs=pl.BlockSpec((1, BLOCK, nq), cf),
        out_shape=jax.ShapeDtypeStruct((b, t, nq), F32),
        compiler_params=_cparams("parallel", "parallel"),
        name="window_attention" if local else "context_attention",
    )(sink, q, k, k, k, v, v, v, kx, vx)


def _even_out_kernel(yf_ref, yb_ref, r_ref, v_ref, g_ref, kd_ref, batt_ref, x_ref, mod_ref, lnw_ref, lnb_ref, rk_ref, e_ref,
                     w_ref, o_ref, *, d, c):
    y = yf_ref[0] + yb_ref[0]
    inv = 1.0 / HEAD_DIM
    mu = _segsum(y, e_ref) * inv
    yc = y - mu
    var = _segsum(yc * yc, e_ref) * inv
    yn = yc * lax.rsqrt(var + RWKV_GN_EPS) * lnw_ref[...] + lnb_ref[...]
    bonus = _segsum(r_ref[0] * (kd_ref[0, 0] + kd_ref[1, 0]) * rk_ref[...], e_ref)
    a_out = (yn + bonus * v_ref[0]) * g_ref[0]
    o = _dot(a_out, w_ref[0:c]) + _dot(batt_ref[0], w_ref[c:])
    o_ref[0] = x_ref[0] + mod_ref[0][:, 2 * d:3 * d] * o


def _even_out(y, r, v, g, kd, batt, x, mod, ep, tabs):
    b, t, d = x.shape
    c = r.shape[-1]
    nq = batt.shape[-1]
    tm = _row_tile(t, 512)
    tok = lambda w: pl.BlockSpec((1, tm, w), lambda bi, i: (bi, i, 0))
    tok2 = lambda w: pl.BlockSpec((2, 1, tm, w), lambda bi, i: (0, bi, i, 0))
    full = lambda *shape: pl.BlockSpec(shape, lambda bi, i: (0,) * len(shape))
    return pl.pallas_call(
        functools.partial(_even_out_kernel, d=d, c=c),
        grid=(b, t // tm),
        in_specs=[tok(c), tok(c), tok(c), tok(c), tok(c), tok2(c), tok(nq), tok(d),
                  pl.BlockSpec((1, 1, mod.shape[-1]), lambda bi, i: (bi, 0, 0)),
                  full(1, c), full(1, c), full(1, c), full(c, c), full(c + nq, d)],
        out_specs=tok(d),
        out_shape=jax.ShapeDtypeStruct((b, t, d), F32),
        compiler_params=_cparams("parallel", "parallel"),
        name="even_out_proj",
    )(y[0], y[1], r, v, g, kd, batt, x, mod, ep['ln_w'], ep['ln_b'], ep['r_k'], tabs['e'], ep['w_out'])


def _filter_kernel(z_ref, t_ref, w1_ref, b1_ref, w2_ref, b2_ref, w3_ref, b3_ref, fr_ref, wo_ref, dl_ref, o_ref, *, d):
    fr = fr_ref[...]
    h = jnp.sin(fr * (_dot_hi(z_ref[...], w1_ref[...]) + b1_ref[...]))
    h = jnp.sin(fr * (_dot_hi(h, w2_ref[...]) + b2_ref[...]))
    h = jnp.sin(fr * (_dot_hi(h, w3_ref[...]) + b3_ref[...]))
    filt = _dot_hi(h, wo_ref[...])
    modu = jnp.exp(-t_ref[...] * dl_ref[...]) + HY_MOD_SHIFT
    for q in range(o_ref.shape[0]):
        o_ref[q] = filt[:, q * d:(q + 1) * d] * modu


def _hyena_filters(n, op, d):
    t = np.linspace(0.0, 1.0, n, dtype=np.float32)[:, None]
    ang = (2.0 * math.pi * np.arange(n, dtype=np.float32)[:, None] / np.float32(n)).astype(np.float32)
    f = np.linspace(1e-4, HY_BANDS - 1, HY_BANDS, dtype=np.float32)[None, :]
    zfeat = jnp.concatenate([jnp.asarray(t), jnp.cos(jnp.asarray(f * ang)), -jnp.sin(jnp.asarray(f * ang))], axis=-1)
    emb_pad = op['f_w1'].shape[0]
    zfeat = jnp.pad(zfeat, ((0, 0), (0, emb_pad - HY_EMB)))
    deltas = np.abs(np.linspace(math.log(HY_TARGET) / HY_SLOW_PCT, math.log(HY_TARGET) / HY_FAST_PCT, d,
                                dtype=np.float32))[None, :]
    tn = min(n, 256)
    nq = 2 * HY_ORDER
    hf = op['f_w2'].shape[0]
    full = lambda *shape: pl.BlockSpec(shape, lambda i: (0,) * len(shape))
    return pl.pallas_call(
        functools.partial(_filter_kernel, d=d),
        grid=(n // tn,),
        in_specs=[pl.BlockSpec((tn, emb_pad), lambda i: (i, 0)), pl.BlockSpec((tn, 1), lambda i: (i, 0)),
                  full(emb_pad, hf), full(1, hf), full(hf, hf), full(1, hf), full(hf, hf), full(1, hf), full(1, hf),
                  full(hf, nq * d), full(1, d)],
        out_specs=pl.BlockSpec((nq, tn, d), lambda i: (0, i, 0)),
        out_shape=jax.ShapeDtypeStruct((nq, n, d), F32),
        compiler_params=_cparams("parallel"),
        name="hyena_filter",
    )(zfeat, jnp.asarray(t), op['f_w1'], op['f_b1'], op['f_w2'], op['f_b2'], op['f_w3'], op['f_b3'], op['f_freq'],
      op['f_out'], jnp.asarray(deltas))


def _dft(n, rows, cols, sign=-1.0):
    k = np.arange(rows, dtype=np.float64)[:, None]
    m = np.arange(cols, dtype=np.float64)[None, :]
    ang = sign * 2.0 * np.pi * ((k * m) % n) / n
    return np.cos(ang), np.sin(ang)


def _stack(re, im):
    return jnp.asarray(np.concatenate([re, im], axis=0).astype(np.float32)).astype(BF16)


def _fft_tables(n_seq):
    n = 2 * n_seq
    n1 = n // FFT_N2
    f1r, f1i = _dft(n1, n1, n1 // 2)
    h1r, h1i = _dft(n1, n1 // 2, n1, sign=1.0)
    k1 = np.arange(n1, dtype=np.float64)[:, None, None]
    k2 = np.arange(FFT_N2, dtype=np.float64)[None, :, None]
    j2 = np.arange(FFT_N2, dtype=np.float64)[None, None, :]
    ang = -2.0 * np.pi * (((k2 * j2 * n1) + k1 * j2) % n) / n
    gr, gi = np.cos(ang), np.sin(ang)
    g_fwd = np.concatenate([gr, gi], axis=1)
    g_inv = np.concatenate([np.swapaxes(gr, 1, 2), np.swapaxes(gi, 1, 2)], axis=1)
    eye = np.eye(SUBLANES)
    kron = lambda m: np.kron(m, eye)
    return {'f1': _stack(kron(f1r), kron(f1i)), 'h1': _stack(kron(h1r), kron(h1i)),
            'g_fwd': jnp.asarray(g_fwd.astype(np.float32)).astype(BF16),
            'g_inv': jnp.asarray(g_inv.astype(np.float32)).astype(BF16), 'n1': n1}


def _dense_tables(n_seq):
    n = 2 * n_seq
    fr, fi = _dft(n, n, n_seq)
    hr, hi = _dft(n, n_seq, n, sign=1.0)
    return {'f': _stack(fr, fi), 'h': _stack(hr, hi)}


def _fft_a_kernel(f_ref, zr_ref, zi_ref, ar_ref, ai_ref, *, n1, cplx):
    f = f_ref[...]
    half, tj, d = zr_ref.shape[1:]
    m = n1 * SUBLANES
    res_r, res_i = [], []
    for s0 in range(0, tj, SUBLANES):
        js = slice(s0, s0 + SUBLANES)
        zr = zr_ref[0, :, js, :].reshape(half * SUBLANES, d)
        p = jnp.dot(f, zr.astype(BF16), preferred_element_type=F32)
        if cplx:
            zi = zi_ref[0, :, js, :].reshape(half * SUBLANES, d)
            q = jnp.dot(f, zi.astype(BF16), preferred_element_type=F32)
            res_r.append((p[0:m] - q[m:]).reshape(n1, SUBLANES, d))
            res_i.append((p[m:] + q[0:m]).reshape(n1, SUBLANES, d))
        else:
            res_r.append(p[0:m].reshape(n1, SUBLANES, d))
            res_i.append(p[m:].reshape(n1, SUBLANES, d))
    ar_ref[0] = jnp.concatenate(res_r, axis=1).astype(BF16)
    ai_ref[0] = jnp.concatenate(res_i, axis=1).astype(BF16)


def _fft_a(u5, q, tabs, cplx):
    _, s, half, n2, d = u5.shape
    n1 = tabs['n1']
    assert half == n1 // 2 and n2 == FFT_N2
    sp = s // 2 if cplx else s
    tj = 2 * SUBLANES
    zi_map = (lambda si, j: (q, si + sp, 0, j, 0)) if cplx else (lambda si, j: (q, si, 0, j, 0))
    out = pl.BlockSpec((1, n1, tj, d), lambda si, j: (si, 0, j, 0))
    return pl.pallas_call(
        functools.partial(_fft_a_kernel, n1=n1, cplx=cplx),
        grid=(sp, n2 // tj),
        in_specs=[pl.BlockSpec((2 * n1 * SUBLANES, half * SUBLANES), lambda si, j: (0, 0)),
                  pl.BlockSpec((None, 1, half, tj, d), lambda si, j: (q, si, 0, j, 0)),
                  pl.BlockSpec((None, 1, half, tj, d), zi_map)],
        out_specs=[out, out],
        out_shape=[jax.ShapeDtypeStruct((sp, n1, n2, d), BF16)] * 2,
        compiler_params=_cparams("parallel", "parallel"),
        name="fft_stage1",
    )(tabs['f1'], u5, u5)


def _cplx_mm(s, xr, xi, conj):
    p = jnp.dot(s, xr.astype(BF16), preferred_element_type=F32)
    q = jnp.dot(s, xi.astype(BF16), preferred_element_type=F32)
    m = s.shape[0] // 2
    if conj:
        return p[0:m] + q[m:], q[0:m] - p[m:]
    return p[0:m] - q[m:], p[m:] + q[0:m]


def _fft_b_kernel(gf_ref, gi_ref, ar_ref, ai_ref, kr_ref, ki_ref, dr_ref, di_ref):
    cr, ci = _cplx_mm(gf_ref[0], ar_ref[0, 0], ai_ref[0, 0], False)
    kr = kr_ref[0]
    ki = ki_ref[0]
    er = cr * kr - ci * ki
    ei = cr * ki + ci * kr
    dr, di = _cplx_mm(gi_ref[0], er, ei, True)
    dr_ref[0, 0] = dr.astype(BF16)
    di_ref[0, 0] = di.astype(BF16)


def _fft_b(ar4, ai4, kr, ki, order, tabs):
    sp, n1, _, d = ar4.shape
    blk = pl.BlockSpec((1, 1, FFT_N2, d), lambda k1, si: (si, k1, 0, 0))
    tab = pl.BlockSpec((1, 2 * FFT_N2, FFT_N2), lambda k1, si: (k1, 0, 0))
    kb = pl.BlockSpec((None, 1, FFT_N2, d), lambda k1, si: (order, k1, 0, 0))
    return pl.pallas_call(
        _fft_b_kernel,
        grid=(n1, sp),
        in_specs=[tab, tab, blk, blk, kb, kb],
        out_specs=[blk, blk],
        out_shape=[jax.ShapeDtypeStruct((sp, n1, FFT_N2, d), BF16)] * 2,
        compiler_params=_cparams("parallel", "arbitrary"),
        name="fft_stage2_filter",
    )(tabs['g_fwd'], tabs['g_inv'], ar4, ai4, kr, ki)


def _fft_c_kernel(h_ref, dr_ref, di_ref, u_ref, x_ref, sk_ref, o_ref):
    h = h_ref[...]
    n1, tj, d = dr_ref.shape[1:]
    half = n1 // 2
    sk = sk_ref[...]
    dr_all = dr_ref[0].astype(F32)
    di_all = di_ref[0].astype(F32)
    for s0 in range(0, tj, SUBLANES):
        js = slice(s0, s0 + SUBLANES)
        yr, yi = _cplx_mm(h, dr_all[:, js, :].reshape(n1 * SUBLANES, d),
                          di_all[:, js, :].reshape(n1 * SUBLANES, d), False)
        for part, y in enumerate((yr, yi)):
            u = u_ref[part, 0, :, js, :]
            o_ref[part, 0, :, js, :] = x_ref[part, 0, :, js, :] * (y.reshape(half, SUBLANES, d) + u * sk)


def _fft_c(dr, di, u6, uq, x6, xq, skip, tabs):
    sp, n1, n2, d = dr.shape
    half = n1 // 2
    tj = 2 * SUBLANES
    pair = lambda q: pl.BlockSpec((None, 2, 1, half, tj, d), lambda si, j: (q, 0, si, 0, j, 0))
    dblk = pl.BlockSpec((1, n1, tj, d), lambda si, j: (si, 0, j, 0))
    out = pl.pallas_call(
        _fft_c_kernel,
        grid=(sp, n2 // tj),
        in_specs=[pl.BlockSpec((2 * half * SUBLANES, n1 * SUBLANES), lambda si, j: (0, 0)), dblk, dblk,
                  pair(uq), pair(xq), pl.BlockSpec((1, d), lambda si, j: (0, 0))],
        out_specs=pl.BlockSpec((2, 1, half, tj, d), lambda si, j: (0, si, 0, j, 0)),
        out_shape=jax.ShapeDtypeStruct((2, sp, half, n2, d), F32),
        compiler_params=_cparams("parallel", "parallel"),
        name="fft_inverse_stage1_gate",
    )(tabs['h1'], dr, di, u6, x6, skip.reshape(1, d))
    return out


def _spec_b_kernel(gf_ref, ar_ref, ai_ref, kr_ref, ki_ref, *, scale):
    fr, fi = _cplx_mm(gf_ref[0], ar_ref[0, 0], ai_ref[0, 0], False)
    gr, gi = _cplx_mm(gf_ref[0], ar_ref[1, 0], ai_ref[1, 0], False)
    kr_ref[0, 0] = (fr + gr) * scale
    ki_ref[0, 0] = (fi - gi) * scale


def _filter_spectrum_fft(filt, tabs, d):
    nq, n, _ = filt.shape
    n1 = tabs['n1']
    ar, ai = _fft_a(filt.reshape(1, nq, n1 // 2, FFT_N2, d), 0, tabs, cplx=False)
    orders = nq // 2
    ar5 = ar.reshape(orders, 2, n1, FFT_N2, d)
    ai5 = ai.reshape(orders, 2, n1, FFT_N2, d)
    blk = pl.BlockSpec((None, 2, 1, FFT_N2, d), lambda o, k1: (o, 0, k1, 0, 0))
    out = pl.BlockSpec((1, 1, FFT_N2, d), lambda o, k1: (o, k1, 0, 0))
    return pl.pallas_call(
        functools.partial(_spec_b_kernel, scale=1.0 / (2 * n)),
        grid=(orders, n1),
        in_specs=[pl.BlockSpec((1, 2 * FFT_N2, FFT_N2), lambda o, k1: (k1, 0, 0)), blk, blk],
        out_specs=[out, out],
        out_shape=[jax.ShapeDtypeStruct((orders, n1, FFT_N2, d), F32)] * 2,
        compiler_params=_cparams("parallel", "parallel"),
        name="filter_spectrum",
    )(tabs['g_fwd'], ar5, ai5)


def _long_conv_fft(u, uq, x, xq, skip, kr, ki, order, tabs):
    _, b, n, d = u.shape
    half = tabs['n1'] // 2
    ar, ai = _fft_a(u.reshape(u.shape[0], b, half, FFT_N2, d), uq, tabs, cplx=True)
    dr, di = _fft_b(ar, ai, kr, ki, order, tabs)
    six = lambda a: a.reshape(a.shape[0], 2, b // 2, half, FFT_N2, d)
    return _fft_c(dr, di, six(u), uq, six(x), xq, skip, tabs).reshape(b, n, d)


def _dense_spec_kernel(f_ref, filt_ref, kr_ref, ki_ref, *, scale):
    f = f_ref[...]
    n = f.shape[0] // 2
    pf = jnp.dot(f, filt_ref[0, 0].astype(BF16), preferred_element_type=F32)
    pg = jnp.dot(f, filt_ref[0, 1].astype(BF16), preferred_element_type=F32)
    kr_ref[0] = (pf[0:n] + pg[0:n]) * scale
    ki_ref[0] = (pf[n:] - pg[n:]) * scale


def _filter_spectrum_dense(filt, tabs, d):
    nq, n, _ = filt.shape
    orders = nq // 2
    f4 = filt.reshape(orders, 2, n, d)
    out = pl.BlockSpec((1, 2 * n, d), lambda o: (o, 0, 0))
    return pl.pallas_call(
        functools.partial(_dense_spec_kernel, scale=1.0 / (2 * n)),
        grid=(orders,),
        in_specs=[pl.BlockSpec((4 * n, n), lambda o: (0, 0)), pl.BlockSpec((1, 2, n, d), lambda o: (o, 0, 0, 0))],
        out_specs=[out, out],
        out_shape=[jax.ShapeDtypeStruct((orders, 2 * n, d), F32)] * 2,
        compiler_params=_cparams("parallel"),
        name="filter_spectrum_dense",
    )(tabs['f'], f4)


def _dense_conv_kernel(f_ref, h_ref, u_ref, x_ref, kr_ref, ki_ref, sk_ref, o_ref):
    cr, ci = _cplx_mm(f_ref[...], u_ref[0, 0], u_ref[1, 0], False)
    kr = kr_ref[...]
    ki = ki_ref[...]
    yr, yi = _cplx_mm(h_ref[...], cr * kr - ci * ki, cr * ki + ci * kr, False)
    sk = sk_ref[...]
    o_ref[0, 0] = x_ref[0, 0] * (yr + u_ref[0, 0] * sk)
    o_ref[1, 0] = x_ref[1, 0] * (yi + u_ref[1, 0] * sk)


def _long_conv_dense(u, x, skip, kr, ki, tabs):
    b, n, d = u.shape
    sp = b // 2
    pair = pl.BlockSpec((2, 1, n, d), lambda si: (0, si, 0, 0))
    kb = pl.BlockSpec((2 * n, d), lambda si: (0, 0))
    out = pl.pallas_call(
        _dense_conv_kernel,
        grid=(sp,),
        in_specs=[pl.BlockSpec((4 * n, n), lambda si: (0, 0)), pl.BlockSpec((2 * n, 2 * n), lambda si: (0, 0)),
                  pair, pair, kb, kb, pl.BlockSpec((1, d), lambda si: (0, 0))],
        out_specs=pair,
        out_shape=jax.ShapeDtypeStruct((2, sp, n, d), F32),
        compiler_params=_cparams("parallel"),
        name="long_conv_dense",
    )(tabs['f'], tabs['h'], u.reshape(2, sp, n, d), x.reshape(2, sp, n, d), kr, ki, skip.reshape(1, d))
    return out.reshape(b, n, d)


def _hyena_mixer(x, g, mod, op, fft_tabs, dense_tabs):
    b, n, d = x.shape
    z = _proj_conv(x, g, mod, op['w_in'], op['b_in'], op['conv_w'], op['conv_b'])
    filt = _hyena_filters(n, op, d)
    if n <= DENSE_FFT_MAX:
        kr, ki = _filter_spectrum_dense(filt, dense_tabs, d)
        y = _long_conv_dense(z[0], z[1], op['skip'][0], kr[0], ki[0], dense_tabs)
        y = _long_conv_dense(y, z[2], op['skip'][1], kr[1], ki[1], dense_tabs)
    else:
        kr, ki = _filter_spectrum_fft(filt, fft_tabs, d)
        y = _long_conv_fft(z, 0, z, 1, op['skip'][0], kr, ki, 0, fft_tabs)
        y = _long_conv_fft(y[None], 0, z, 2, op['skip'][1], kr, ki, 1, fft_tabs)
    return _out_res(y, x, mod, op['w_out'], op['b_out'])


def _even_mixer(x, ctx, g, mod_l, mod_c, ep, tabs, need_ctx):
    p_lat = _proj(x, g, mod_l, ep['w_in'], 0)
    p_ctx = _proj(ctx, g, mod_c, ep['w_in'], 0)
    rc, vc, kkc, gc, lwc, ac, kdc, qc, kac, vac = _even_prep(p_ctx, ep, tabs, rope=False)
    rl, vl, kkl, gl, lwl, al, kdl, ql, kal, val = _even_prep(p_lat, ep, tabs, rope=True)
    b = x.shape[0]
    c = rl.shape[-1]
    h0 = jnp.zeros((2, b, HEAD_DIM, c), F32)
    y_ctx, s_ctx = _rwkv(rc, vc, kkc, lwc, ac, kdc, h0)
    y_lat, _ = _rwkv(rl, vl, kkl, lwl, al, kdl, s_ctx)
    b_lat = _attention(ql, kal, val, kac, vac, ep['sink'], local=True)
    x_new = _even_out(y_lat, rl, vl, gl, kdl, b_lat, x, mod_l, ep, tabs)
    if not need_ctx:
        return x_new, None
    b_ctx = _attention(qc, kac, vac, kac, vac, ep['sink'], local=False)
    ctx_new = _even_out(y_ctx, rc, vc, gc, kdc, b_ctx, ctx, mod_c, ep, tabs)
    return x_new, ctx_new


def _rope_tables(n_tokens):
    rows = n_tokens // GRID_W
    row = jnp.repeat(jnp.arange(rows), GRID_W).astype(F32)
    col = jnp.tile(jnp.arange(GRID_W), rows).astype(F32)
    n_freq = HEAD_DIM // 4
    inv = ROPE_THETA ** (-jnp.arange(n_freq, dtype=F32) / n_freq)
    ang = jnp.concatenate([row[:, None] * inv, col[:, None] * inv], axis=-1)
    cos, sin = jnp.cos(ang), jnp.sin(ang)
    reps = LANES // HEAD_DIM
    cos_t = jnp.tile(jnp.concatenate([cos, cos], axis=-1), (1, reps))
    sin_t = jnp.tile(jnp.concatenate([-sin, sin], axis=-1), (1, reps))
    return cos_t, sin_t


def _block_ones(width):
    idx = np.arange(width) // HEAD_DIM
    return jnp.asarray((idx[:, None] == idx[None, :]).astype(np.float32)).astype(BF16)


def _lora_pad(w):
    z = jnp.zeros_like(w[0])
    return jnp.stack([jnp.concatenate([w[0], z], axis=0), jnp.concatenate([z, w[1]], axis=0)], axis=0)


def kernel(x, c, ctx, c_ctx, ada_w, ada_b, norm1_g, norm2_g, ffn_up, ffn_conv_w, ffn_conv_b, ffn_down, ev_w_in, ev_mu_prev, ev_mu_next, ev_w0, ev_w2, ev_a0, ev_a2, ev_g2, ev_k_k, ev_k_a, ev_r_k, ev_ln_w, ev_ln_b, ev_q_norm, ev_k_norm, ev_sink, ev_w_out, od_w_in, od_b_in, od_conv_w, od_conv_b, od_f_w1, od_f_b1, od_f_w2, od_f_b2, od_f_w3, od_f_b3, od_f_freq, od_f_out, od_skip, od_w_out, od_b_out):
    bsz, seq, d = x.shape
    lc = ctx.shape[1]
    depth = ada_w.shape[0]
    a_width = ev_k_k.shape[-1]
    nq = ev_sink.shape[-1] * HEAD_DIM
    nk = B_KV_HEADS * HEAD_DIM

    cond = jnp.zeros((BF16_ROWS, d), F32).at[:bsz].set(c).at[bsz].set(c_ctx)
    mod = _ada_mod(cond, ada_w, ada_b)

    cos_t, sin_t = _rope_tables(seq)
    tabs = {'cos': cos_t, 'sin': sin_t, 'e': _block_ones(nq)}
    fft_tabs = _fft_tables(seq) if seq > DENSE_FFT_MAX else None
    dense_lat = _dense_tables(seq) if seq <= DENSE_FFT_MAX else None
    fft_ctx = _fft_tables(lc) if lc > DENSE_FFT_MAX else None
    dense_ctx = _dense_tables(lc) if lc <= DENSE_FFT_MAX else None

    for layer in range(depth):
        need_ctx = layer < depth - 1
        even = layer % 2 == 0
        j = layer // 2
        mod_l = mod[layer, :bsz].reshape(bsz, 1, 6 * d)
        mod_c = jnp.broadcast_to(mod[layer, bsz].reshape(1, 1, 6 * d), (bsz, 1, 6 * d))
        if even:
            ep = {'w_in': ev_w_in[j].astype(BF16), 'mu_prev': ev_mu_prev[j][None], 'mu_next': ev_mu_next[j][None],
                  'w0': ev_w0[j], 'w2pad': _lora_pad(ev_w2[j]).astype(BF16), 'a0': ev_a0[j],
                  'a2pad': _lora_pad(ev_a2[j]).astype(BF16), 'g2': ev_g2[j].astype(BF16),
                  'k_k': ev_k_k[j][None], 'k_a': ev_k_a[j][None], 'r_k': ev_r_k[j].reshape(1, a_width),
                  'ln_w': ev_ln_w[j][None], 'ln_b': ev_ln_b[j][None],
                  'q_norm_t': jnp.tile(ev_q_norm[j], nq // HEAD_DIM)[None],
                  'k_norm_t': jnp.tile(ev_k_norm[j], nk // HEAD_DIM)[None],
                  'sink': ev_sink[j], 'w_out': ev_w_out[j].astype(BF16), 'nq': nq, 'nk': nk}
            x, ctx_new = _even_mixer(x, ctx, norm1_g[layer], mod_l, mod_c, ep, tabs, need_ctx)
        else:
            emb_pad = HEAD_DIM
            op = {'w_in': od_w_in[j].astype(BF16), 'b_in': od_b_in[j], 'conv_w': od_conv_w[j],
                  'conv_b': od_conv_b[j],
                  'f_w1': jnp.pad(od_f_w1[j], ((0, emb_pad - HY_EMB), (0, 0))), 'f_b1': od_f_b1[j][None],
                  'f_w2': od_f_w2[j], 'f_b2': od_f_b2[j][None], 'f_w3': od_f_w3[j], 'f_b3': od_f_b3[j][None],
                  'f_freq': od_f_freq[j][None], 'f_out': od_f_out[j], 'skip': od_skip[j],
                  'w_out': od_w_out[j].astype(BF16), 'b_out': od_b_out[j]}
            ctx_new = _hyena_mixer(ctx, norm1_g[layer], mod_c, op, fft_ctx, dense_ctx) if need_ctx else None
            x = _hyena_mixer(x, norm1_g[layer], mod_l, op, fft_tabs, dense_lat)
        w_up = ffn_up[layer].astype(BF16)
        w_down = ffn_down[layer].astype(BF16)
        x = _conv_ffn(x, norm2_g[layer], mod_l, w_up, ffn_conv_w[layer], ffn_conv_b[layer], w_down)
        if need_ctx:
            ctx = _conv_ffn(ctx_new, norm2_g[layer], mod_c, w_up, ffn_conv_w[layer], ffn_conv_b[layer], w_down)
    return x
```

```python
import functools
import math

import numpy as np
import jax
import jax.numpy as jnp
from jax import lax
from jax.experimental import pallas as pl
from jax.experimental.pallas import tpu as pltpu

F32 = jnp.float32
BF16 = jnp.bfloat16

HEAD_DIM = 64
GRID_W = 64
DECAY_LORA = 64
ICLR_LORA = 64
GATE_LORA = 128
RWKV_GN_EPS = 64e-5
B_KV_HEADS = 2
WINDOW = 128
BLOCK = 128
ROPE_THETA = 10000.0
MASK_VALUE = -1e30
HY_ORDER = 2
HY_EMB = 33
HY_BANDS = (HY_EMB - 1) // 2
HY_TARGET = 1e-2
HY_FAST_PCT = 0.3
HY_SLOW_PCT = 1.5
HY_MOD_SHIFT = 0.05
NORM_EPS = 1e-6

V7X_VMEM_BYTES = 64 * 1024 * 1024
VMEM_LIMIT_BYTES = V7X_VMEM_BYTES * 3 // 4
LANES = 128
SUBLANES = 8
BF16_ROWS = 16
RWKV_CHUNK = 64
RWKV_PREP_CHUNKS = 4
RWKV_INV_BASE = 8
RWKV_GRAM_PASSES = 1
RWKV_INV_PASSES = 1
RWKV_REST_PASSES = 1
RWKV_SCAN_PASSES = 2
FFT_N2 = 128
DENSE_FFT_MAX = 512

HIGHEST = lax.Precision.HIGHEST


def _cparams(*sem):
    return pltpu.CompilerParams(dimension_semantics=sem, vmem_limit_bytes=VMEM_LIMIT_BYTES)


def _dot(a, b):
    return jnp.dot(a.astype(BF16), b.astype(BF16), preferred_element_type=F32)


def _dot_nt(a, b):
    return lax.dot_general(a.astype(BF16), b.astype(BF16), (((1,), (1,)), ((), ())), preferred_element_type=F32)


def _dot_tn(a, b):
    return lax.dot_general(a.astype(BF16), b.astype(BF16), (((0,), (0,)), ((), ())), preferred_element_type=F32)


def _dot_hi(a, b):
    return jnp.dot(a, b, preferred_element_type=F32, precision=HIGHEST)


def _dot_nt_hi(a, b):
    return lax.dot_general(a, b, (((1,), (1,)), ((), ())), preferred_element_type=F32, precision=HIGHEST)


def _dot_tn_hi(a, b):
    return lax.dot_general(a, b, (((0,), (0,)), ((), ())), preferred_element_type=F32, precision=HIGHEST)


def _segsum(x, e_ref):
    hi = x.astype(BF16)
    lo = (x - hi.astype(F32)).astype(BF16)
    e = e_ref[...]
    return jnp.dot(hi, e, preferred_element_type=F32) + jnp.dot(lo, e, preferred_element_type=F32)


def _norm_mod(x, g, shift, scale):
    xn = x * lax.rsqrt(jnp.mean(x * x, axis=-1, keepdims=True) + NORM_EPS)
    return (xn * g) * (1.0 + scale) + shift


def _row_tile(t, want):
    tm = min(t, want)
    assert t % tm == 0 and tm % BF16_ROWS == 0
    return tm


def _ada_kernel(c_ref, w_ref, b_ref, o_ref):
    c = c_ref[...]
    s = c * jax.nn.sigmoid(c)
    o_ref[0] = _dot(s, w_ref[0]) + b_ref[0]


def _ada_mod(cond, ada_w, ada_b):
    depth, d, n = ada_w.shape
    tn = n // 4
    rows = cond.shape[0]
    return pl.pallas_call(
        _ada_kernel,
        grid=(depth, n // tn),
        in_specs=[pl.BlockSpec((rows, d), lambda l, j: (0, 0)),
                  pl.BlockSpec((1, d, tn), lambda l, j: (l, 0, j)),
                  pl.BlockSpec((1, 1, tn), lambda l, j: (l, 0, j))],
        out_specs=pl.BlockSpec((1, rows, tn), lambda l, j: (l, 0, j)),
        out_shape=jax.ShapeDtypeStruct((depth, rows, n), F32),
        compiler_params=_cparams("parallel", "parallel"),
        name="ada_mod",
    )(cond, ada_w, ada_b.reshape(depth, 1, n))


def _halo_specs(tm, t, d, nargs):
    r = tm // BF16_ROWS
    last = t // BF16_ROWS - 1
    if nargs == 3:
        prev = pl.BlockSpec((1, BF16_ROWS, d), lambda bi, i, j: (bi, jnp.maximum(i * r - 1, 0), 0))
        nxt = pl.BlockSpec((1, BF16_ROWS, d), lambda bi, i, j: (bi, jnp.minimum((i + 1) * r, last), 0))
    else:
        prev = pl.BlockSpec((1, BF16_ROWS, d), lambda bi, i: (bi, jnp.maximum(i * r - 1, 0), 0))
        nxt = pl.BlockSpec((1, BF16_ROWS, d), lambda bi, i: (bi, jnp.minimum((i + 1) * r, last), 0))
    return prev, nxt


def _fill_h(x_ref, xp_ref, xn_ref, g, shift, scale, h_scr, tm):
    h_scr[0:BF16_ROWS] = _norm_mod(xp_ref[0], g, shift, scale).astype(BF16)
    h_scr[BF16_ROWS:BF16_ROWS + tm] = _norm_mod(x_ref[0], g, shift, scale).astype(BF16)
    h_scr[BF16_ROWS + tm:2 * BF16_ROWS + tm] = _norm_mod(xn_ref[0], g, shift, scale).astype(BF16)


def _conv3_rows(u, u_scr, cw_ref, cb_ref, tm, t_total):
    i = pl.program_id(1)
    o = BF16_ROWS
    u_scr[0:o] = jnp.where(i > 0, u[0:o], 0.0)
    u_scr[o:o + tm] = u[o:o + tm]
    u_scr[o + tm:] = jnp.where(i < t_total // tm - 1, u[o + tm:], 0.0)
    cw = cw_ref[...]
    return (u_scr[o - 1:o - 1 + tm] * cw[0:1] + u_scr[o:o + tm] * cw[1:2] + u_scr[o + 1:o + 1 + tm] * cw[2:3]
            + cb_ref[...])


def _proj_conv_kernel(x_ref, xp_ref, xn_ref, g_ref, mod_ref, w_ref, b_ref, cw_ref, cb_ref, o_ref, h_scr, u_scr,
                      *, d, tm, t_total):
    @pl.when(pl.program_id(2) == 0)
    def _():
        m = mod_ref[0]
        _fill_h(x_ref, xp_ref, xn_ref, g_ref[...], m[:, 0:d], m[:, d:2 * d], h_scr, tm)

    u = jnp.dot(h_scr[...], w_ref[...], preferred_element_type=F32) + b_ref[...]
    o_ref[0, 0] = _conv3_rows(u, u_scr, cw_ref, cb_ref, tm, t_total)


def _proj_conv(x, g, mod, w, bias, cw, cb):
    b, t, d = x.shape
    n = w.shape[1]
    tm = _row_tile(t, 512)
    tn = d
    prev, nxt = _halo_specs(tm, t, d, 3)
    return pl.pallas_call(
        functools.partial(_proj_conv_kernel, d=d, tm=tm, t_total=t),
        grid=(b, t // tm, n // tn),
        in_specs=[pl.BlockSpec((1, tm, d), lambda bi, i, j: (bi, i, 0)), prev, nxt,
                  pl.BlockSpec((1, d), lambda bi, i, j: (0, 0)),
                  pl.BlockSpec((1, 1, mod.shape[-1]), lambda bi, i, j: (bi, 0, 0)),
                  pl.BlockSpec((d, tn), lambda bi, i, j: (0, j)),
                  pl.BlockSpec((1, tn), lambda bi, i, j: (0, j)),
                  pl.BlockSpec((3, tn), lambda bi, i, j: (0, j)),
                  pl.BlockSpec((1, tn), lambda bi, i, j: (0, j))],
        out_specs=pl.BlockSpec((1, 1, tm, tn), lambda bi, i, j: (j, bi, i, 0)),
        out_shape=jax.ShapeDtypeStruct((n // tn, b, t, tn), F32),
        scratch_shapes=[pltpu.VMEM((tm + 2 * BF16_ROWS, d), BF16), pltpu.VMEM((tm + 2 * BF16_ROWS, tn), F32)],
        compiler_params=_cparams("parallel", "parallel", "arbitrary"),
        name="hyena_in_proj",
    )(x, x, x, g.reshape(1, d), mod, w, bias.reshape(1, n), cw, cb.reshape(1, n))


def _ffn_kernel(x_ref, xp_ref, xn_ref, g_ref, mod_ref, wg_ref, wv_ref, cwg_ref, cwv_ref, cbg_ref, cbv_ref, wd_ref,
                o_ref, h_scr, ug_scr, uv_scr, acc_scr, *, d, tm, t_total):
    j = pl.program_id(2)

    @pl.when(j == 0)
    def _():
        m = mod_ref[0]
        _fill_h(x_ref, xp_ref, xn_ref, g_ref[...], m[:, 3 * d:4 * d], m[:, 4 * d:5 * d], h_scr, tm)
        acc_scr[...] = jnp.zeros_like(acc_scr)

    h = h_scr[...]
    gate = _conv3_rows(jnp.dot(h, wg_ref[...], preferred_element_type=F32), ug_scr, cwg_ref, cbg_ref, tm, t_total)
    val = _conv3_rows(jnp.dot(h, wv_ref[...], preferred_element_type=F32), uv_scr, cwv_ref, cbv_ref, tm, t_total)
    act = (gate * jax.nn.sigmoid(gate)) * val
    acc_scr[...] += jnp.dot(act.astype(BF16), wd_ref[...], preferred_element_type=F32)

    @pl.when(j == pl.num_programs(2) - 1)
    def _():
        o_ref[0] = x_ref[0] + mod_ref[0][:, 5 * d:6 * d] * acc_scr[...]


def _conv_ffn(x, g, mod, w_up, cw, cb, w_down):
    b, t, d = x.shape
    f = w_down.shape[0]
    tm = _row_tile(t, 1024)
    tf = 256 if f % 256 == 0 else 128
    nf = f // tf
    prev, nxt = _halo_specs(tm, t, d, 3)
    cb2 = cb.reshape(1, 2 * f)
    return pl.pallas_call(
        functools.partial(_ffn_kernel, d=d, tm=tm, t_total=t),
        grid=(b, t // tm, nf),
        in_specs=[pl.BlockSpec((1, tm, d), lambda bi, i, j: (bi, i, 0)), prev, nxt,
                  pl.BlockSpec((1, d), lambda bi, i, j: (0, 0)),
                  pl.BlockSpec((1, 1, mod.shape[-1]), lambda bi, i, j: (bi, 0, 0)),
                  pl.BlockSpec((d, tf), lambda bi, i, j: (0, j)),
                  pl.BlockSpec((d, tf), lambda bi, i, j: (0, nf + j)),
                  pl.BlockSpec((3, tf), lambda bi, i, j: (0, j)),
                  pl.BlockSpec((3, tf), lambda bi, i, j: (0, nf + j)),
                  pl.BlockSpec((1, tf), lambda bi, i, j: (0, j)),
                  pl.BlockSpec((1, tf), lambda bi, i, j: (0, nf + j)),
                  pl.BlockSpec((tf, d), lambda bi, i, j: (j, 0))],
        out_specs=pl.BlockSpec((1, tm, d), lambda bi, i, j: (bi, i, 0)),
        out_shape=jax.ShapeDtypeStruct((b, t, d), F32),
        scratch_shapes=[pltpu.VMEM((tm + 2 * BF16_ROWS, d), BF16),
                        pltpu.VMEM((tm + 2 * BF16_ROWS, tf), F32),
                        pltpu.VMEM((tm + 2 * BF16_ROWS, tf), F32),
                        pltpu.VMEM((tm, d), F32)],
        compiler_params=_cparams("parallel", "parallel", "arbitrary"),
        name="conv_ffn",
    )(x, x, x, g.reshape(1, d), mod, w_up, w_up, cw, cw, cb2, cb2, w_down)


def _out_res_kernel(y_ref, x_ref, mod_ref, w_ref, b_ref, o_ref, *, d):
    o = _dot(y_ref[0], w_ref[...]) + b_ref[...]
    o_ref[0] = x_ref[0] + mod_ref[0][:, 2 * d:3 * d] * o


def _out_res(y, x, mod, w, bias):
    b, t, d = x.shape
    tm = _row_tile(t, 512)
    return pl.pallas_call(
        functools.partial(_out_res_kernel, d=d),
        grid=(b, t // tm),
        in_specs=[pl.BlockSpec((1, tm, d), lambda bi, i: (bi, i, 0)),
                  pl.BlockSpec((1, tm, d), lambda bi, i: (bi, i, 0)),
                  pl.BlockSpec((1, 1, mod.shape[-1]), lambda bi, i: (bi, 0, 0)),
                  pl.BlockSpec((d, d), lambda bi, i: (0, 0)),
                  pl.BlockSpec((1, d), lambda bi, i: (0, 0))],
        out_specs=pl.BlockSpec((1, tm, d), lambda bi, i: (bi, i, 0)),
        out_shape=jax.ShapeDtypeStruct((b, t, d), F32),
        compiler_params=_cparams("parallel", "parallel"),
        name="out_proj_residual",
    )(y, x, mod, w, bias.reshape(1, d))


def _softplus(z):
    return jnp.maximum(z, 0.0) + jnp.log1p(jnp.exp(-jnp.abs(z)))


def _even_prep_kernel(x_ref, xp_ref, xn_ref, gn_ref, mod_ref, win_ref, mup_ref, mun_ref, w0_ref, w2_ref, a0_ref,
                      a2_ref, g2_ref, kk_ref, ka_ref, qn_ref, kn_ref, cos_ref, sin_ref, e_ref,
                      r_ref, v_ref, kkn_ref, g_ref, lw_ref, a_ref, kd_ref, q_ref, katt_ref, vatt_ref,
                      h_scr, p_scr, *, tm, d, c, a_in, rope):
    i = pl.program_id(1)
    nt = pl.num_programs(1)
    m = mod_ref[0]
    _fill_h(x_ref, xp_ref, xn_ref, gn_ref[...], m[:, 0:d], m[:, d:2 * d], h_scr, tm)
    p_scr[...] = jnp.dot(h_scr[...], win_ref[...], preferred_element_type=F32)
    o = BF16_ROWS
    pa = p_scr[o:o + tm, 0:a_in]
    rows = lax.broadcasted_iota(jnp.int32, (tm, 1), 0)
    prev = jnp.where((rows == 0) & (i == 0), 0.0, p_scr[o - 1:o - 1 + tm, 0:a_in])
    nxt = jnp.where((rows == tm - 1) & (i == nt - 1), 0.0, p_scr[o + 1:o + 1 + tm, 0:a_in])
    za = pa + mup_ref[...] * (prev - pa) + mun_ref[...] * (nxt - pa)

    r = za[:, 0:c]
    k = za[:, c:2 * c]
    v = za[:, 2 * c:3 * c]
    wd = jnp.tanh(za[:, 3 * c:3 * c + 2 * DECAY_LORA])
    ad = za[:, 3 * c + 2 * DECAY_LORA:3 * c + 2 * DECAY_LORA + 2 * ICLR_LORA]
    gd = za[:, 3 * c + 2 * DECAY_LORA + 2 * ICLR_LORA:a_in]
    r_ref[0] = r
    v_ref[0] = v
    g_ref[0] = _dot(jax.nn.sigmoid(gd), g2_ref[...])
    kkv = k * kk_ref[...]
    ss = _segsum(kkv * kkv, e_ref)
    kkn_ref[0] = kkv / jnp.maximum(jnp.sqrt(ss), 1e-12)
    for dd in range(2):
        w_log = -_softplus(-(w0_ref[dd:dd + 1] + _dot(wd, w2_ref[dd]))) - 0.5
        lw_ref[dd, 0] = -jnp.exp(w_log)
        a = jax.nn.sigmoid(a0_ref[dd:dd + 1] + _dot(ad, a2_ref[dd]))
        a_ref[dd, 0] = a
        kd_ref[dd, 0] = k * (1.0 + (a - 1.0) * ka_ref[...])

    pb = p_scr[o:o + tm, a_in:]
    nq = q_ref.shape[-1]
    nk = katt_ref.shape[-1]
    q = pb[:, 0:nq]
    kat = pb[:, nq:nq + nk]
    vatt_ref[0] = pb[:, nq + nk:]
    half = HEAD_DIM // 2

    def norm_rope(x, gain, e):
        w = x.shape[-1]
        ms = _segsum(x * x, e) * (1.0 / HEAD_DIM)
        xn = x * lax.rsqrt(ms + NORM_EPS) * gain
        if not rope:
            return xn
        lane = lax.broadcasted_iota(jnp.int32, (1, w), 1)
        first = (lane % HEAD_DIM) < half
        swapped = jnp.where(first, pltpu.roll(xn, w - half, 1), pltpu.roll(xn, half, 1))
        reps = w // LANES
        cos = jnp.concatenate([cos_ref[...]] * reps, axis=1) if reps > 1 else cos_ref[...]
        sin = jnp.concatenate([sin_ref[...]] * reps, axis=1) if reps > 1 else sin_ref[...]
        return xn * cos + swapped * sin

    q_ref[0] = norm_rope(q, qn_ref[...], e_ref[...])
    katt_ref[0] = norm_rope(kat, kn_ref[...], e_ref[0:nk, 0:nk])


def _even_prep(x, g, mod, ep, tabs, rope):
    b, t, d = x.shape
    n = ep['w_in'].shape[1]
    c = ep['k_k'].shape[-1]
    a_in = ep['mu_prev'].shape[-1]
    nq = ep['nq']
    nk = ep['nk']
    tm = _row_tile(t, 256)
    prev, nxt = _halo_specs(tm, t, d, 2)
    full = lambda *shape: pl.BlockSpec(shape, lambda bi, i: (0,) * len(shape))
    tok = lambda w: pl.BlockSpec((1, tm, w), lambda bi, i: (bi, i, 0))
    tok2 = lambda w: pl.BlockSpec((2, 1, tm, w), lambda bi, i: (0, bi, i, 0))
    sd = lambda *shape: jax.ShapeDtypeStruct(shape, F32)
    return pl.pallas_call(
        functools.partial(_even_prep_kernel, tm=tm, d=d, c=c, a_in=a_in, rope=rope),
        grid=(b, t // tm),
        in_specs=[tok(d), prev, nxt, full(1, d), pl.BlockSpec((1, 1, mod.shape[-1]), lambda bi, i: (bi, 0, 0)),
                  full(d, n), full(1, a_in), full(1, a_in), full(2, c), full(2, 2 * DECAY_LORA, c),
                  full(2, c), full(2, 2 * ICLR_LORA, c), full(GATE_LORA, c), full(1, c), full(1, c),
                  full(1, nq), full(1, nk),
                  pl.BlockSpec((tm, LANES), lambda bi, i: (i, 0)), pl.BlockSpec((tm, LANES), lambda bi, i: (i, 0)),
                  full(nq, nq)],
        out_specs=[tok(c), tok(c), tok(c), tok(c), tok2(c), tok2(c), tok2(c), tok(nq), tok(nk), tok(nk)],
        out_shape=[sd(b, t, c), sd(b, t, c), sd(b, t, c), sd(b, t, c), sd(2, b, t, c), sd(2, b, t, c),
                   sd(2, b, t, c), sd(b, t, nq), sd(b, t, nk), sd(b, t, nk)],
        scratch_shapes=[pltpu.VMEM((tm + 2 * BF16_ROWS, d), BF16), pltpu.VMEM((tm + 2 * BF16_ROWS, n), F32)],
        compiler_params=_cparams("parallel", "parallel"),
        name="even_in_proj_prep",
    )(x, x, x, g.reshape(1, d), mod, ep['w_in'], ep['mu_prev'], ep['mu_next'], ep['w0'], ep['w2pad'], ep['a0'], ep['a2pad'], ep['g2'], ep['k_k'],
      ep['k_a'], ep['q_norm_t'], ep['k_norm_t'], tabs['cos'][:t], tabs['sin'][:t], tabs['e'])


HEADS_PER_GROUP = 4
GROUP_LANES = HEADS_PER_GROUP * HEAD_DIM


def _split_bf16(x):
    hi = x.astype(BF16)
    return hi, (x - hi.astype(F32)).astype(BF16)


def _block_diag(x, bmask):
    return jnp.where(bmask, jnp.concatenate([x] * HEADS_PER_GROUP, axis=0), jnp.zeros((), x.dtype))


def _head_mm(lhs, rhs, bmask, passes, nt=False):
    dn = (((1,), (1,)), ((), ())) if nt else (((1,), (0,)), ((), ()))
    if passes == 1:
        return lax.dot_general(lhs.astype(BF16), _block_diag(rhs.astype(BF16), bmask), dn, preferred_element_type=F32)
    lh, ll = _split_bf16(lhs)
    rh, rl = _split_bf16(rhs)
    m = lhs.shape[0]
    top = lax.dot_general(jnp.concatenate([lh, ll], axis=0), _block_diag(rh, bmask), dn, preferred_element_type=F32)
    return top[0:m] + top[m:] + lax.dot_general(lh, _block_diag(rl, bmask), dn, preferred_element_type=F32)


def _head_mm_tn(lhs, rhs, lane_head, passes):
    dn = (((0,), (0,)), ((), ()))
    if passes == 1:
        full = lax.dot_general(lhs.astype(BF16), rhs.astype(BF16), dn, preferred_element_type=F32)
    else:
        lh, ll = _split_bf16(lhs)
        rh, rl = _split_bf16(rhs)
        full = (lax.dot_general(jnp.concatenate([lh, ll], axis=0), jnp.concatenate([rh, rh], axis=0), dn,
                                preferred_element_type=F32)
                + lax.dot_general(lh, rl, dn, preferred_element_type=F32))
    out = jnp.where(lane_head == 0, full[0:HEAD_DIM], 0.0)
    for h in range(1, HEADS_PER_GROUP):
        out = out + jnp.where(lane_head == h, full[h * HEAD_DIM:(h + 1) * HEAD_DIM], 0.0)
    return out


def _group_masks():
    r = lax.broadcasted_iota(jnp.int32, (GROUP_LANES, GROUP_LANES), 0)
    c = lax.broadcasted_iota(jnp.int32, (GROUP_LANES, GROUP_LANES), 1)
    bmask = (r // HEAD_DIM) == (c // HEAD_DIM)
    lane_head = lax.broadcasted_iota(jnp.int32, (1, GROUP_LANES), 1) // HEAD_DIM
    return bmask, lane_head


def _rwkv_prep_kernel(r_ref, v_ref, kk_ref, lw_ref, a_ref, kd_ref, r2_ref, yl_ref, t_ref, z_ref, *, cs):
    sgn = 1 - 2 * pl.program_id(0)
    row = lax.broadcasted_iota(jnp.int32, (cs, cs), 0)
    col = lax.broadcasted_iota(jnp.int32, (cs, cs), 1)
    tri = (((row - col) * sgn) >= 0).astype(F32)
    bmask, lane_head = _group_masks()
    t_idx = lax.broadcasted_iota(jnp.int32, (cs, GROUP_LANES), 0)
    s_idx = lax.broadcasted_iota(jnp.int32, (cs, GROUP_LANES), 1) % HEAD_DIM
    diff = (t_idx - s_idx) * sgn
    strict = diff > 0
    incl = diff >= 0
    eye = (diff == 0).astype(F32)
    base = min(RWKV_INV_BASE, cs)
    same_base = (t_idx // base) == (s_idx // base)
    groups = r_ref.shape[-1] // GROUP_LANES

    ch = []
    for c0 in range(0, r_ref.shape[1], cs):
        rows = slice(c0, c0 + cs)
        r, v, kk = r_ref[0, rows], v_ref[0, rows], kk_ref[0, rows]
        lw, a, kd = lw_ref[0, 0, rows], a_ref[0, 0, rows], kd_ref[0, 0, rows]
        g = _dot_hi(tri, lw)
        gp = g - lw
        gref = g[cs // 2:cs // 2 + 1]
        gend = jnp.sum(lw, axis=0, keepdims=True)
        bvec = kk * a
        full = {'a_t': -kk * jnp.exp(gp - gref), 'a_0': -kk * jnp.exp(gp),
                'b_t': bvec * jnp.exp(gref - g), 'b_e': bvec * jnp.exp(gend - g),
                'k_t': kd * jnp.exp(gref - g), 'k_e': kd * jnp.exp(gend - g),
                'r_t': r * jnp.exp(g - gref), 'r_0': r * jnp.exp(g), 'v': v,
                'wend': jnp.broadcast_to(jnp.exp(gend), (cs, v.shape[-1]))}
        for gi in range(groups):
            sl = slice(gi * GROUP_LANES, (gi + 1) * GROUP_LANES)
            c = {k: x[:, sl] for k, x in full.items()}
            c['rows'], c['sl'] = rows, sl
            ch.append(c)

    mm = lambda x, y, passes, nt=False: [_head_mm(p_, q_, bmask, passes, nt) for p_, q_ in zip(x, y)]
    get = lambda k: [c[k] for c in ch]
    ar = [jnp.concatenate([c['a_t'], c['r_t']], axis=0) for c in ch]
    gb = mm(ar, get('b_t'), RWKV_GRAM_PASSES, True)
    gk = mm(ar, get('k_t'), RWKV_GRAM_PASSES, True)
    nmat = [jnp.where(strict, x[0:cs], 0.0) for x in gb]
    g_ak = [jnp.where(strict, x[0:cs], 0.0) for x in gk]
    g_rb = [jnp.where(incl, x[cs:], 0.0) for x in gb]
    g_rk = [jnp.where(incl, x[cs:], 0.0) for x in gk]
    npow = [jnp.where(same_base, x, 0.0) for x in nmat]
    p = [eye + x for x in npow]
    for _ in range(int(math.log2(base)) - 1):
        npow = mm(npow, npow, RWKV_INV_PASSES)
        p = [x + y for x, y in zip(p, mm(npow, p, RWKV_INV_PASSES))]
    m = base
    while m < cs:
        off = ((t_idx // m) != (s_idx // m)) & ((t_idx // (2 * m)) == (s_idx // (2 * m)))
        q = mm([jnp.where(off, x, 0.0) for x in nmat], p, RWKV_INV_PASSES)
        p = [x + y for x, y in zip(p, mm(p, q, RWKV_INV_PASSES))]
        m *= 2
    vg = get('v')
    a2 = mm(p, get('a_0'), RWKV_REST_PASSES)
    u_v = mm(p, mm(g_ak, vg, RWKV_REST_PASSES), RWKV_REST_PASSES)
    r2 = mm(g_rb, a2, RWKV_REST_PASSES)
    yl1 = mm(g_rb, u_v, RWKV_REST_PASSES)
    yl2 = mm(g_rk, vg, RWKV_REST_PASSES)
    tt = [_head_mm_tn(x, c['b_e'], lane_head, RWKV_REST_PASSES) for x, c in zip(a2, ch)]
    zz = [_head_mm_tn(jnp.concatenate([u, c['v']], axis=0), jnp.concatenate([c['b_e'], c['k_e']], axis=0),
                      lane_head, RWKV_REST_PASSES) for u, c in zip(u_v, ch)]
    for i, c in enumerate(ch):
        rows, sl = c['rows'], c['sl']
        r2_ref[0, 0, rows, sl] = c['r_0'] + r2[i]
        yl_ref[0, 0, rows, sl] = yl1[i] + yl2[i]
        t_ref[0, 0, rows, sl] = eye * c['wend'] + tt[i]
        z_ref[0, 0, rows, sl] = zz[i]


def _rwkv_scan_kernel(h0_ref, r2f_ref, ylf_ref, tf_ref, zf_ref, r2b_ref, ylb_ref, tb_ref, zb_ref,
                      yf_ref, yb_ref, ht_ref, s_scr):
    ci = pl.program_id(0)

    @pl.when(ci == 0)
    def _():
        s_scr[...] = h0_ref[...]

    bmask, _ = _group_masks()
    ins = ((r2f_ref, ylf_ref, tf_ref, zf_ref, yf_ref), (r2b_ref, ylb_ref, tb_ref, zb_ref, yb_ref))
    for d, (r2_ref, yl_ref, t_ref, z_ref, y_ref) in enumerate(ins):
        for b in range(s_scr.shape[1]):
            for gi in range(s_scr.shape[-1] // GROUP_LANES):
                sl = slice(gi * GROUP_LANES, (gi + 1) * GROUP_LANES)
                s = s_scr[d, b, :, sl]
                y_ref[0, b, :, sl] = yl_ref[0, b, :, sl] + _head_mm(r2_ref[0, b, :, sl], s, bmask, 1, nt=True)
                s_scr[d, b, :, sl] = (_head_mm(s, t_ref[0, b, :, sl], bmask, RWKV_SCAN_PASSES)
                                      + z_ref[0, b, :, sl])

    @pl.when(ci == pl.num_programs(0) - 1)
    def _():
        ht_ref[...] = s_scr[...]


def _rwkv(r, v, kk, lw, a, kd, h0):
    b, t, c = r.shape
    cs = min(RWKV_CHUNK, t)
    assert cs == HEAD_DIM and c % GROUP_LANES == 0
    nc = t // cs
    rows = cs * math.gcd(nc, RWKV_PREP_CHUNKS)
    tok = pl.BlockSpec((1, rows, c), lambda d, bi, ci: (bi, ci, 0))
    tok2 = pl.BlockSpec((1, 1, rows, c), lambda d, bi, ci: (d, bi, ci, 0))
    wide = jax.ShapeDtypeStruct((2, b, t, c), F32)
    r2, yl, tt, zz = pl.pallas_call(
        functools.partial(_rwkv_prep_kernel, cs=cs),
        grid=(2, b, t // rows),
        in_specs=[tok, tok, tok, tok2, tok2, tok2],
        out_specs=[tok2, tok2, tok2, tok2],
        out_shape=[wide, wide, wide, wide],
        compiler_params=_cparams("parallel", "parallel", "parallel"),
        name="rwkv7_chunk_prep",
    )(r, v, kk, lw, a, kd)

    fwd = pl.BlockSpec((1, b, cs, c), lambda ci: (0, 0, ci, 0))
    bwd = pl.BlockSpec((1, b, cs, c), lambda ci: (1, 0, nc - 1 - ci, 0))
    st = pl.BlockSpec((2, b, HEAD_DIM, c), lambda ci: (0, 0, 0, 0))
    yf, yb, ht = pl.pallas_call(
        _rwkv_scan_kernel,
        grid=(nc,),
        in_specs=[st, fwd, fwd, fwd, fwd, bwd, bwd, bwd, bwd],
        out_specs=[pl.BlockSpec((1, b, cs, c), lambda ci: (0, 0, ci, 0)),
                   pl.BlockSpec((1, b, cs, c), lambda ci: (0, 0, nc - 1 - ci, 0)), st],
        out_shape=[jax.ShapeDtypeStruct((1, b, t, c), F32), jax.ShapeDtypeStruct((1, b, t, c), F32),
                   jax.ShapeDtypeStruct((2, b, HEAD_DIM, c), F32)],
        scratch_shapes=[pltpu.VMEM((2, b, HEAD_DIM, c), F32)],
        compiler_params=_cparams("arbitrary"),
        name="rwkv7_state_scan",
    )(h0, r2, yl, tt, zz, r2, yl, tt, zz)
    return (yf.reshape(b, t, c), yb.reshape(b, t, c)), ht


def _attn_kernel(sink_ref, q_ref, kp_ref, kc_ref, kn_ref, vp_ref, vc_ref, vn_ref, kx_ref, vx_ref, o_ref,
                 *, local, group):
    i = pl.program_id(1)
    nb = pl.num_programs(1)
    scale = HEAD_DIM ** -0.5
    q = q_ref[0]
    if local:
        k_all = jnp.concatenate([kp_ref[0], kc_ref[0], kn_ref[0], kx_ref[0]], axis=0)
        v_all = jnp.concatenate([vp_ref[0], vc_ref[0], vn_ref[0], vx_ref[0]], axis=0)
        nkeys = k_all.shape[0]
        qi = lax.broadcasted_iota(jnp.int32, (BLOCK, nkeys), 0)
        kj = lax.broadcasted_iota(jnp.int32, (BLOCK, nkeys), 1)
        kpos = kj + (i - 1) * BLOCK
        valid = (jnp.abs(kj - BLOCK - qi) <= WINDOW) & (kpos >= 0) & (kpos < nb * BLOCK)
        valid = valid | (kj >= 3 * BLOCK)
    else:
        k_all = kx_ref[0]
        v_all = vx_ref[0]
    heads = range(q.shape[-1] // HEAD_DIM)
    kv = lambda x, h: x[:, (h // group) * HEAD_DIM:(h // group + 1) * HEAD_DIM]
    s = [_dot_nt(q[:, h * HEAD_DIM:(h + 1) * HEAD_DIM], kv(k_all, h)) * scale for h in heads]
    if local:
        s = [jnp.where(valid, x, MASK_VALUE) for x in s]
    sink = [sink_ref[h] for h in heads]
    m = [jnp.maximum(jnp.max(x, axis=-1, keepdims=True), sk) for x, sk in zip(s, sink)]
    e = [jnp.exp(x - y) for x, y in zip(s, m)]
    den = [jnp.sum(x, axis=-1, keepdims=True) + jnp.exp(sk - y) for x, y, sk in zip(e, m, sink)]
    outs = [_dot(x, kv(v_all, h)) / dn for x, h, dn in zip(e, heads, den)]
    o_ref[0] = jnp.concatenate(outs, axis=1)


def _attention(q, k, v, kx, vx, sink, local):
    b, t, nq = q.shape
    nk = k.shape[-1]
    nb = t // BLOCK
    group = (nq // HEAD_DIM) // (nk // HEAD_DIM)
    lx = kx.shape[1]
    kv = lambda f: pl.BlockSpec((1, BLOCK, nk), f)
    pf = lambda bi, i: (bi, jnp.maximum(i - 1, 0), 0)
    cf = lambda bi, i: (bi, i, 0)
    nf = lambda bi, i: (bi, jnp.minimum(i + 1, nb - 1), 0)
    ctx = pl.BlockSpec((1, lx, nk), lambda bi, i: (bi, 0, 0))
    return pl.pallas_call(
        functools.partial(_attn_kernel, local=local, group=group),
        grid=(b, nb),
        in_specs=[pl.BlockSpec(memory_space=pltpu.SMEM),
                  pl.BlockSpec((1, BLOCK, nq), cf), kv(pf), kv(cf), kv(nf), kv(pf), kv(cf), kv(nf), ctx, ctx],
        out_specs=pl.BlockSpec((1, BLOCK, nq), cf),
        out_shape=jax.ShapeDtypeStruct((b, t, nq), F32),
        compiler_params=_cparams("parallel", "parallel"),
        name="window_attention" if local else "context_attention",
    )(sink, q, k, k, k, v, v, v, kx, vx)


def _even_out_kernel(yf_ref, yb_ref, r_ref, v_ref, g_ref, kd_ref, batt_ref, x_ref, mod_ref, lnw_ref, lnb_ref, rk_ref, e_ref,
                     w_ref, o_ref, *, d, c):
    y = yf_ref[0] + yb_ref[0]
    inv = 1.0 / HEAD_DIM
    mu = _segsum(y, e_ref) * inv
    yc = y - mu
    var = _segsum(yc * yc, e_ref) * inv
    yn = yc * lax.rsqrt(var + RWKV_GN_EPS) * lnw_ref[...] + lnb_ref[...]
    bonus = _segsum(r_ref[0] * (kd_ref[0, 0] + kd_ref[1, 0]) * rk_ref[...], e_ref)
    a_out = (yn + bonus * v_ref[0]) * g_ref[0]
    o = _dot(a_out, w_ref[0:c]) + _dot(batt_ref[0], w_ref[c:])
    o_ref[0] = x_ref[0] + mod_ref[0][:, 2 * d:3 * d] * o


def _even_out(y, r, v, g, kd, batt, x, mod, ep, tabs):
    b, t, d = x.shape
    c = r.shape[-1]
    nq = batt.shape[-1]
    tm = _row_tile(t, 512)
    tok = lambda w: pl.BlockSpec((1, tm, w), lambda bi, i: (bi, i, 0))
    tok2 = lambda w: pl.BlockSpec((2, 1, tm, w), lambda bi, i: (0, bi, i, 0))
    full = lambda *shape: pl.BlockSpec(shape, lambda bi, i: (0,) * len(shape))
    return pl.pallas_call(
        functools.partial(_even_out_kernel, d=d, c=c),
        grid=(b, t // tm),
        in_specs=[tok(c), tok(c), tok(c), tok(c), tok(c), tok2(c), tok(nq), tok(d),
                  pl.BlockSpec((1, 1, mod.shape[-1]), lambda bi, i: (bi, 0, 0)),
                  full(1, c), full(1, c), full(1, c), full(c, c), full(c + nq, d)],
        out_specs=tok(d),
        out_shape=jax.ShapeDtypeStruct((b, t, d), F32),
        compiler_params=_cparams("parallel", "parallel"),
        name="even_out_proj",
    )(y[0], y[1], r, v, g, kd, batt, x, mod, ep['ln_w'], ep['ln_b'], ep['r_k'], tabs['e'], ep['w_out'])


def _filter_kernel(z_ref, t_ref, w1_ref, b1_ref, w2_ref, b2_ref, w3_ref, b3_ref, fr_ref, wo_ref, dl_ref, o_ref, *, d):
    fr = fr_ref[...]
    h = jnp.sin(fr * (_dot_hi(z_ref[...], w1_ref[...]) + b1_ref[...]))
    h = jnp.sin(fr * (_dot_hi(h, w2_ref[...]) + b2_ref[...]))
    h = jnp.sin(fr * (_dot_hi(h, w3_ref[...]) + b3_ref[...]))
    filt = _dot_hi(h, wo_ref[...])
    modu = jnp.exp(-t_ref[...] * dl_ref[...]) + HY_MOD_SHIFT
    for q in range(o_ref.shape[0]):
        o_ref[q] = filt[:, q * d:(q + 1) * d] * modu


def _hyena_filters(n, op, d):
    t = np.linspace(0.0, 1.0, n, dtype=np.float32)[:, None]
    ang = (2.0 * math.pi * np.arange(n, dtype=np.float32)[:, None] / np.float32(n)).astype(np.float32)
    f = np.linspace(1e-4, HY_BANDS - 1, HY_BANDS, dtype=np.float32)[None, :]
    zfeat = jnp.concatenate([jnp.asarray(t), jnp.cos(jnp.asarray(f * ang)), -jnp.sin(jnp.asarray(f * ang))], axis=-1)
    emb_pad = op['f_w1'].shape[0]
    zfeat = jnp.pad(zfeat, ((0, 0), (0, emb_pad - HY_EMB)))
    deltas = np.abs(np.linspace(math.log(HY_TARGET) / HY_SLOW_PCT, math.log(HY_TARGET) / HY_FAST_PCT, d,
                                dtype=np.float32))[None, :]
    tn = min(n, 256)
    nq = 2 * HY_ORDER
    hf = op['f_w2'].shape[0]
    full = lambda *shape: pl.BlockSpec(shape, lambda i: (0,) * len(shape))
    return pl.pallas_call(
        functools.partial(_filter_kernel, d=d),
        grid=(n // tn,),
        in_specs=[pl.BlockSpec((tn, emb_pad), lambda i: (i, 0)), pl.BlockSpec((tn, 1), lambda i: (i, 0)),
                  full(emb_pad, hf), full(1, hf), full(hf, hf), full(1, hf), full(hf, hf), full(1, hf), full(1, hf),
                  full(hf, nq * d), full(1, d)],
        out_specs=pl.BlockSpec((nq, tn, d), lambda i: (0, i, 0)),
        out_shape=jax.ShapeDtypeStruct((nq, n, d), F32),
        compiler_params=_cparams("parallel"),
        name="hyena_filter",
    )(zfeat, jnp.asarray(t), op['f_w1'], op['f_b1'], op['f_w2'], op['f_b2'], op['f_w3'], op['f_b3'], op['f_freq'],
      op['f_out'], jnp.asarray(deltas))


def _dft(n, rows, cols, sign=-1.0):
    k = np.arange(rows, dtype=np.float64)[:, None]
    m = np.arange(cols, dtype=np.float64)[None, :]
    ang = sign * 2.0 * np.pi * ((k * m) % n) / n
    return np.cos(ang), np.sin(ang)


def _stack(re, im):
    return jnp.asarray(np.concatenate([re, im], axis=0).astype(np.float32)).astype(BF16)


def _fft_tables(n_seq):
    n = 2 * n_seq
    n1 = n // FFT_N2
    f1r, f1i = _dft(n1, n1, n1 // 2)
    h1r, h1i = _dft(n1, n1 // 2, n1, sign=1.0)
    k1 = np.arange(n1, dtype=np.float64)[:, None, None]
    k2 = np.arange(FFT_N2, dtype=np.float64)[None, :, None]
    j2 = np.arange(FFT_N2, dtype=np.float64)[None, None, :]
    ang = -2.0 * np.pi * (((k2 * j2 * n1) + k1 * j2) % n) / n
    gr, gi = np.cos(ang), np.sin(ang)
    g_fwd = np.concatenate([gr, gi], axis=1)
    g_inv = np.concatenate([np.swapaxes(gr, 1, 2), np.swapaxes(gi, 1, 2)], axis=1)
    eye = np.eye(SUBLANES)
    kron = lambda m: np.kron(m, eye)
    return {'f1': _stack(kron(f1r), kron(f1i)), 'h1': _stack(kron(h1r), kron(h1i)),
            'g_fwd': jnp.asarray(g_fwd.astype(np.float32)).astype(BF16),
            'g_inv': jnp.asarray(g_inv.astype(np.float32)).astype(BF16), 'n1': n1}


def _dense_tables(n_seq):
    n = 2 * n_seq
    fr, fi = _dft(n, n, n_seq)
    hr, hi = _dft(n, n_seq, n, sign=1.0)
    return {'f': _stack(fr, fi), 'h': _stack(hr, hi)}


def _fft_a_kernel(f_ref, zr_ref, zi_ref, ar_ref, ai_ref, *, n1, cplx):
    f = f_ref[...]
    half, tj, d = zr_ref.shape[1:]
    m = n1 * SUBLANES
    res_r, res_i = [], []
    for s0 in range(0, tj, SUBLANES):
        js = slice(s0, s0 + SUBLANES)
        zr = zr_ref[0, :, js, :].reshape(half * SUBLANES, d)
        p = jnp.dot(f, zr.astype(BF16), preferred_element_type=F32)
        if cplx:
            zi = zi_ref[0, :, js, :].reshape(half * SUBLANES, d)
            q = jnp.dot(f, zi.astype(BF16), preferred_element_type=F32)
            res_r.append((p[0:m] - q[m:]).reshape(n1, SUBLANES, d))
            res_i.append((p[m:] + q[0:m]).reshape(n1, SUBLANES, d))
        else:
            res_r.append(p[0:m].reshape(n1, SUBLANES, d))
            res_i.append(p[m:].reshape(n1, SUBLANES, d))
    ar_ref[0] = jnp.concatenate(res_r, axis=1).astype(BF16)
    ai_ref[0] = jnp.concatenate(res_i, axis=1).astype(BF16)


def _fft_a(u5, q, tabs, cplx):
    _, s, half, n2, d = u5.shape
    n1 = tabs['n1']
    assert half == n1 // 2 and n2 == FFT_N2
    sp = s // 2 if cplx else s
    tj = 2 * SUBLANES
    zi_map = (lambda si, j: (q, si + sp, 0, j, 0)) if cplx else (lambda si, j: (q, si, 0, j, 0))
    out = pl.BlockSpec((1, n1, tj, d), lambda si, j: (si, 0, j, 0))
    return pl.pallas_call(
        functools.partial(_fft_a_kernel, n1=n1, cplx=cplx),
        grid=(sp, n2 // tj),
        in_specs=[pl.BlockSpec((2 * n1 * SUBLANES, half * SUBLANES), lambda si, j: (0, 0)),
                  pl.BlockSpec((None, 1, half, tj, d), lambda si, j: (q, si, 0, j, 0)),
                  pl.BlockSpec((None, 1, half, tj, d), zi_map)],
        out_specs=[out, out],
        out_shape=[jax.ShapeDtypeStruct((sp, n1, n2, d), BF16)] * 2,
        compiler_params=_cparams("parallel", "parallel"),
        name="fft_stage1",
    )(tabs['f1'], u5, u5)


def _cplx_mm(s, xr, xi, conj):
    p = jnp.dot(s, xr.astype(BF16), preferred_element_type=F32)
    q = jnp.dot(s, xi.astype(BF16), preferred_element_type=F32)
    m = s.shape[0] // 2
    if conj:
        return p[0:m] + q[m:], q[0:m] - p[m:]
    return p[0:m] - q[m:], p[m:] + q[0:m]


def _fft_b_kernel(gf_ref, gi_ref, ar_ref, ai_ref, kr_ref, ki_ref, dr_ref, di_ref):
    kr = kr_ref[0]
    ki = ki_ref[0]
    seqs = range(ar_ref.shape[0])
    c = [_cplx_mm(gf_ref[0], ar_ref[s, 0], ai_ref[s, 0], False) for s in seqs]
    e = [(cr * kr - ci * ki, cr * ki + ci * kr) for cr, ci in c]
    dd = [_cplx_mm(gi_ref[0], er, ei, True) for er, ei in e]
    for s, (dr, di) in zip(seqs, dd):
        dr_ref[s, 0] = dr.astype(BF16)
        di_ref[s, 0] = di.astype(BF16)


def _fft_b(ar4, ai4, kr, ki, order, tabs):
    sp, n1, _, d = ar4.shape
    ns = 2 if sp % 2 == 0 else 1
    blk = pl.BlockSpec((ns, 1, FFT_N2, d), lambda k1, si: (si, k1, 0, 0))
    tab = pl.BlockSpec((1, 2 * FFT_N2, FFT_N2), lambda k1, si: (k1, 0, 0))
    kb = pl.BlockSpec((None, 1, FFT_N2, d), lambda k1, si: (order, k1, 0, 0))
    return pl.pallas_call(
        _fft_b_kernel,
        grid=(n1, sp // ns),
        in_specs=[tab, tab, blk, blk, kb, kb],
        out_specs=[blk, blk],
        out_shape=[jax.ShapeDtypeStruct((sp, n1, FFT_N2, d), BF16)] * 2,
        compiler_params=_cparams("parallel", "arbitrary"),
        name="fft_stage2_filter",
    )(tabs['g_fwd'], tabs['g_inv'], ar4, ai4, kr, ki)


def _fft_c_kernel(h_ref, dr_ref, di_ref, u_ref, x_ref, sk_ref, o_ref):
    h = h_ref[...]
    n1, tj, d = dr_ref.shape[1:]
    half = n1 // 2
    sk = sk_ref[...]
    dr_all = dr_ref[0].astype(F32)
    di_all = di_ref[0].astype(F32)
    for s0 in range(0, tj, SUBLANES):
        js = slice(s0, s0 + SUBLANES)
        yr, yi = _cplx_mm(h, dr_all[:, js, :].reshape(n1 * SUBLANES, d),
                          di_all[:, js, :].reshape(n1 * SUBLANES, d), False)
        for part, y in enumerate((yr, yi)):
            u = u_ref[part, 0, :, js, :]
            o_ref[part, 0, :, js, :] = x_ref[part, 0, :, js, :] * (y.reshape(half, SUBLANES, d) + u * sk)


def _fft_c(dr, di, u6, uq, x6, xq, skip, tabs):
    sp, n1, n2, d = dr.shape
    half = n1 // 2
    tj = 2 * SUBLANES
    pair = lambda q: pl.BlockSpec((None, 2, 1, half, tj, d), lambda si, j: (q, 0, si, 0, j, 0))
    dblk = pl.BlockSpec((1, n1, tj, d), lambda si, j: (si, 0, j, 0))
    out = pl.pallas_call(
        _fft_c_kernel,
        grid=(sp, n2 // tj),
        in_specs=[pl.BlockSpec((2 * half * SUBLANES, n1 * SUBLANES), lambda si, j: (0, 0)), dblk, dblk,
                  pair(uq), pair(xq), pl.BlockSpec((1, d), lambda si, j: (0, 0))],
        out_specs=pl.BlockSpec((2, 1, half, tj, d), lambda si, j: (0, si, 0, j, 0)),
        out_shape=jax.ShapeDtypeStruct((2, sp, half, n2, d), F32),
        compiler_params=_cparams("parallel", "parallel"),
        name="fft_inverse_stage1_gate",
    )(tabs['h1'], dr, di, u6, x6, skip.reshape(1, d))
    return out


def _spec_b_kernel(gf_ref, ar_ref, ai_ref, kr_ref, ki_ref, *, scale):
    fr, fi = _cplx_mm(gf_ref[0], ar_ref[0, 0], ai_ref[0, 0], False)
    gr, gi = _cplx_mm(gf_ref[0], ar_ref[1, 0], ai_ref[1, 0], False)
    kr_ref[0, 0] = (fr + gr) * scale
    ki_ref[0, 0] = (fi - gi) * scale


def _filter_spectrum_fft(filt, tabs, d):
    nq, n, _ = filt.shape
    n1 = tabs['n1']
    ar, ai = _fft_a(filt.reshape(1, nq, n1 // 2, FFT_N2, d), 0, tabs, cplx=False)
    orders = nq // 2
    ar5 = ar.reshape(orders, 2, n1, FFT_N2, d)
    ai5 = ai.reshape(orders, 2, n1, FFT_N2, d)
    blk = pl.BlockSpec((None, 2, 1, FFT_N2, d), lambda o, k1: (o, 0, k1, 0, 0))
    out = pl.BlockSpec((1, 1, FFT_N2, d), lambda o, k1: (o, k1, 0, 0))
    return pl.pallas_call(
        functools.partial(_spec_b_kernel, scale=1.0 / (2 * n)),
        grid=(orders, n1),
        in_specs=[pl.BlockSpec((1, 2 * FFT_N2, FFT_N2), lambda o, k1: (k1, 0, 0)), blk, blk],
        out_specs=[out, out],
        out_shape=[jax.ShapeDtypeStruct((orders, n1, FFT_N2, d), F32)] * 2,
        compiler_params=_cparams("parallel", "parallel"),
        name="filter_spectrum",
    )(tabs['g_fwd'], ar5, ai5)


def _long_conv_fft(u, uq, x, xq, skip, kr, ki, order, tabs):
    _, b, n, d = u.shape
    half = tabs['n1'] // 2
    ar, ai = _fft_a(u.reshape(u.shape[0], b, half, FFT_N2, d), uq, tabs, cplx=True)
    dr, di = _fft_b(ar, ai, kr, ki, order, tabs)
    six = lambda a: a.reshape(a.shape[0], 2, b // 2, half, FFT_N2, d)
    return _fft_c(dr, di, six(u), uq, six(x), xq, skip, tabs).reshape(b, n, d)


def _dense_spec_kernel(f_ref, filt_ref, kr_ref, ki_ref, *, scale):
    f = f_ref[...]
    n = f.shape[0] // 2
    pf = jnp.dot(f, filt_ref[0, 0].astype(BF16), preferred_element_type=F32)
    pg = jnp.dot(f, filt_ref[0, 1].astype(BF16), preferred_element_type=F32)
    kr_ref[0] = (pf[0:n] + pg[0:n]) * scale
    ki_ref[0] = (pf[n:] - pg[n:]) * scale


def _filter_spectrum_dense(filt, tabs, d):
    nq, n, _ = filt.shape
    orders = nq // 2
    f4 = filt.reshape(orders, 2, n, d)
    out = pl.BlockSpec((1, 2 * n, d), lambda o: (o, 0, 0))
    return pl.pallas_call(
        functools.partial(_dense_spec_kernel, scale=1.0 / (2 * n)),
        grid=(orders,),
        in_specs=[pl.BlockSpec((4 * n, n), lambda o: (0, 0)), pl.BlockSpec((1, 2, n, d), lambda o: (o, 0, 0, 0))],
        out_specs=[out, out],
        out_shape=[jax.ShapeDtypeStruct((orders, 2 * n, d), F32)] * 2,
        compiler_params=_cparams("parallel"),
        name="filter_spectrum_dense",
    )(tabs['f'], f4)


def _dense_conv_kernel(f_ref, h_ref, u_ref, x_ref, kr_ref, ki_ref, sk_ref, o_ref):
    cr, ci = _cplx_mm(f_ref[...], u_ref[0, 0], u_ref[1, 0], False)
    kr = kr_ref[...]
    ki = ki_ref[...]
    yr, yi = _cplx_mm(h_ref[...], cr * kr - ci * ki, cr * ki + ci * kr, False)
    sk = sk_ref[...]
    o_ref[0, 0] = x_ref[0, 0] * (yr + u_ref[0, 0] * sk)
    o_ref[1, 0] = x_ref[1, 0] * (yi + u_ref[1, 0] * sk)


def _long_conv_dense(u, x, skip, kr, ki, tabs):
    b, n, d = u.shape
    sp = b // 2
    pair = pl.BlockSpec((2, 1, n, d), lambda si: (0, si, 0, 0))
    kb = pl.BlockSpec((2 * n, d), lambda si: (0, 0))
    out = pl.pallas_call(
        _dense_conv_kernel,
        grid=(sp,),
        in_specs=[pl.BlockSpec((4 * n, n), lambda si: (0, 0)), pl.BlockSpec((2 * n, 2 * n), lambda si: (0, 0)),
                  pair, pair, kb, kb, pl.BlockSpec((1, d), lambda si: (0, 0))],
        out_specs=pair,
        out_shape=jax.ShapeDtypeStruct((2, sp, n, d), F32),
        compiler_params=_cparams("parallel"),
        name="long_conv_dense",
    )(tabs['f'], tabs['h'], u.reshape(2, sp, n, d), x.reshape(2, sp, n, d), kr, ki, skip.reshape(1, d))
    return out.reshape(b, n, d)


def _hyena_mixer(x, g, mod, op, fft_tabs, dense_tabs):
    b, n, d = x.shape
    z = _proj_conv(x, g, mod, op['w_in'], op['b_in'], op['conv_w'], op['conv_b'])
    filt = _hyena_filters(n, op, d)
    if n <= DENSE_FFT_MAX:
        kr, ki = _filter_spectrum_dense(filt, dense_tabs, d)
        y = _long_conv_dense(z[0], z[1], op['skip'][0], kr[0], ki[0], dense_tabs)
        y = _long_conv_dense(y, z[2], op['skip'][1], kr[1], ki[1], dense_tabs)
    else:
        kr, ki = _filter_spectrum_fft(filt, fft_tabs, d)
        y = _long_conv_fft(z, 0, z, 1, op['skip'][0], kr, ki, 0, fft_tabs)
        y = _long_conv_fft(y[None], 0, z, 2, op['skip'][1], kr, ki, 1, fft_tabs)
    return _out_res(y, x, mod, op['w_out'], op['b_out'])


def _even_mixer(x, ctx, g, mod_l, mod_c, ep, tabs, need_ctx):
    rc, vc, kkc, gc, lwc, ac, kdc, qc, kac, vac = _even_prep(ctx, g, mod_c, ep, tabs, rope=False)
    rl, vl, kkl, gl, lwl, al, kdl, ql, kal, val = _even_prep(x, g, mod_l, ep, tabs, rope=True)
    b = x.shape[0]
    c = rl.shape[-1]
    h0 = jnp.zeros((2, b, HEAD_DIM, c), F32)
    y_ctx, s_ctx = _rwkv(rc, vc, kkc, lwc, ac, kdc, h0)
    y_lat, _ = _rwkv(rl, vl, kkl, lwl, al, kdl, s_ctx)
    b_lat = _attention(ql, kal, val, kac, vac, ep['sink'], local=True)
    x_new = _even_out(y_lat, rl, vl, gl, kdl, b_lat, x, mod_l, ep, tabs)
    if not need_ctx:
        return x_new, None
    b_ctx = _attention(qc, kac, vac, kac, vac, ep['sink'], local=False)
    ctx_new = _even_out(y_ctx, rc, vc, gc, kdc, b_ctx, ctx, mod_c, ep, tabs)
    return x_new, ctx_new


def _rope_tables(n_tokens):
    rows = n_tokens // GRID_W
    row = jnp.repeat(jnp.arange(rows), GRID_W).astype(F32)
    col = jnp.tile(jnp.arange(GRID_W), rows).astype(F32)
    n_freq = HEAD_DIM // 4
    inv = ROPE_THETA ** (-jnp.arange(n_freq, dtype=F32) / n_freq)
    ang = jnp.concatenate([row[:, None] * inv, col[:, None] * inv], axis=-1)
    cos, sin = jnp.cos(ang), jnp.sin(ang)
    reps = LANES // HEAD_DIM
    cos_t = jnp.tile(jnp.concatenate([cos, cos], axis=-1), (1, reps))
    sin_t = jnp.tile(jnp.concatenate([-sin, sin], axis=-1), (1, reps))
    return cos_t, sin_t


def _block_ones(width):
    idx = np.arange(width) // HEAD_DIM
    return jnp.asarray((idx[:, None] == idx[None, :]).astype(np.float32)).astype(BF16)


def _lora_pad(w):
    z = jnp.zeros_like(w[0])
    return jnp.stack([jnp.concatenate([w[0], z], axis=0), jnp.concatenate([z, w[1]], axis=0)], axis=0)


def kernel(x, c, ctx, c_ctx, ada_w, ada_b, norm1_g, norm2_g, ffn_up, ffn_conv_w, ffn_conv_b, ffn_down, ev_w_in, ev_mu_prev, ev_mu_next, ev_w0, ev_w2, ev_a0, ev_a2, ev_g2, ev_k_k, ev_k_a, ev_r_k, ev_ln_w, ev_ln_b, ev_q_norm, ev_k_norm, ev_sink, ev_w_out, od_w_in, od_b_in, od_conv_w, od_conv_b, od_f_w1, od_f_b1, od_f_w2, od_f_b2, od_f_w3, od_f_b3, od_f_freq, od_f_out, od_skip, od_w_out, od_b_out):
    bsz, seq, d = x.shape
    lc = ctx.shape[1]
    depth = ada_w.shape[0]
    a_width = ev_k_k.shape[-1]
    nq = ev_sink.shape[-1] * HEAD_DIM
    nk = B_KV_HEADS * HEAD_DIM

    cond = jnp.zeros((BF16_ROWS, d), F32).at[:bsz].set(c).at[bsz].set(c_ctx)
    mod = _ada_mod(cond, ada_w, ada_b)

    cos_t, sin_t = _rope_tables(seq)
    tabs = {'cos': cos_t, 'sin': sin_t, 'e': _block_ones(nq)}
    fft_tabs = _fft_tables(seq) if seq > DENSE_FFT_MAX else None
    dense_lat = _dense_tables(seq) if seq <= DENSE_FFT_MAX else None
    fft_ctx = _fft_tables(lc) if lc > DENSE_FFT_MAX else None
    dense_ctx = _dense_tables(lc) if lc <= DENSE_FFT_MAX else None

    for layer in range(depth):
        need_ctx = layer < depth - 1
        even = layer % 2 == 0
        j = layer // 2
        mod_l = mod[layer, :bsz].reshape(bsz, 1, 6 * d)
        mod_c = jnp.broadcast_to(mod[layer, bsz].reshape(1, 1, 6 * d), (bsz, 1, 6 * d))
        if even:
            ep = {'w_in': ev_w_in[j].astype(BF16), 'mu_prev': ev_mu_prev[j][None], 'mu_next': ev_mu_next[j][None],
                  'w0': ev_w0[j], 'w2pad': _lora_pad(ev_w2[j]).astype(BF16), 'a0': ev_a0[j],
                  'a2pad': _lora_pad(ev_a2[j]).astype(BF16), 'g2': ev_g2[j].astype(BF16),
                  'k_k': ev_k_k[j][None], 'k_a': ev_k_a[j][None], 'r_k': ev_r_k[j].reshape(1, a_width),
                  'ln_w': ev_ln_w[j][None], 'ln_b': ev_ln_b[j][None],
                  'q_norm_t': jnp.tile(ev_q_norm[j], nq // HEAD_DIM)[None],
                  'k_norm_t': jnp.tile(ev_k_norm[j], nk // HEAD_DIM)[None],
                  'sink': ev_sink[j], 'w_out': ev_w_out[j].astype(BF16), 'nq': nq, 'nk': nk}
            x, ctx_new = _even_mixer(x, ctx, norm1_g[layer], mod_l, mod_c, ep, tabs, need_ctx)
        else:
            emb_pad = HEAD_DIM
            op = {'w_in': od_w_in[j].astype(BF16), 'b_in': od_b_in[j], 'conv_w': od_conv_w[j],
                  'conv_b': od_conv_b[j],
                  'f_w1': jnp.pad(od_f_w1[j], ((0, emb_pad - HY_EMB), (0, 0))), 'f_b1': od_f_b1[j][None],
                  'f_w2': od_f_w2[j], 'f_b2': od_f_b2[j][None], 'f_w3': od_f_w3[j], 'f_b3': od_f_b3[j][None],
                  'f_freq': od_f_freq[j][None], 'f_out': od_f_out[j], 'skip': od_skip[j],
                  'w_out': od_w_out[j].astype(BF16), 'b_out': od_b_out[j]}
            ctx_new = _hyena_mixer(ctx, norm1_g[layer], mod_c, op, fft_ctx, dense_ctx) if need_ctx else None
            x = _hyena_mixer(x, norm1_g[layer], mod_l, op, fft_tabs, dense_lat)
        w_up = ffn_up[layer].astype(BF16)
        w_down = ffn_down[layer].astype(BF16)
        x = _conv_ffn(x, norm2_g[layer], mod_l, w_up, ffn_conv_w[layer], ffn_conv_b[layer], w_down)
        if need_ctx:
            ctx = _conv_ffn(ctx_new, norm2_g[layer], mod_c, w_up, ffn_conv_w[layer], ffn_conv_b[layer], w_down)
    return x
```

```python
import functools
import math

import numpy as np
import jax
import jax.numpy as jnp
from jax import lax
from jax.experimental import pallas as pl
from jax.experimental.pallas import tpu as pltpu

F32 = jnp.float32
BF16 = jnp.bfloat16

HEAD_DIM = 64
GRID_W = 64
DECAY_LORA = 64
ICLR_LORA = 64
GATE_LORA = 128
RWKV_GN_EPS = 64e-5
B_KV_HEADS = 2
WINDOW = 128
BLOCK = 128
ROPE_THETA = 10000.0
MASK_VALUE = -1e30
HY_ORDER = 2
HY_EMB = 33
HY_BANDS = (HY_EMB - 1) // 2
HY_TARGET = 1e-2
HY_FAST_PCT = 0.3
HY_SLOW_PCT = 1.5
HY_MOD_SHIFT = 0.05
NORM_EPS = 1e-6

V7X_VMEM_BYTES = 64 * 1024 * 1024
VMEM_LIMIT_BYTES = V7X_VMEM_BYTES * 3 // 4
LANES = 128
SUBLANES = 8
BF16_ROWS = 16
FFN_ROW_PARTS = 2
RWKV_CHUNK = 64
RWKV_PREP_CHUNKS = 4
RWKV_INV_BASE = 8
RWKV_GRAM_PASSES = 1
RWKV_INV_PASSES = 1
RWKV_REST_PASSES = 1
RWKV_SCAN_PASSES = 1
FFT_N2 = 128
DENSE_FFT_MAX = 512

HIGHEST = lax.Precision.HIGHEST


def _cparams(*sem):
    return pltpu.CompilerParams(dimension_semantics=sem, vmem_limit_bytes=VMEM_LIMIT_BYTES)


def _dot(a, b):
    return jnp.dot(a.astype(BF16), b.astype(BF16), preferred_element_type=F32)


def _dot_nt(a, b):
    return lax.dot_general(a.astype(BF16), b.astype(BF16), (((1,), (1,)), ((), ())), preferred_element_type=F32)


def _dot_tn(a, b):
    return lax.dot_general(a.astype(BF16), b.astype(BF16), (((0,), (0,)), ((), ())), preferred_element_type=F32)


def _dot_hi(a, b):
    return jnp.dot(a, b, preferred_element_type=F32, precision=HIGHEST)


def _dot_nt_hi(a, b):
    return lax.dot_general(a, b, (((1,), (1,)), ((), ())), preferred_element_type=F32, precision=HIGHEST)


def _dot_tn_hi(a, b):
    return lax.dot_general(a, b, (((0,), (0,)), ((), ())), preferred_element_type=F32, precision=HIGHEST)


def _segsum(x, e_ref):
    hi = x.astype(BF16)
    lo = (x - hi.astype(F32)).astype(BF16)
    e = e_ref[...]
    return jnp.dot(hi, e, preferred_element_type=F32) + jnp.dot(lo, e, preferred_element_type=F32)


def _norm_mod(x, g, shift, scale):
    xn = x * lax.rsqrt(jnp.mean(x * x, axis=-1, keepdims=True) + NORM_EPS)
    return (xn * g) * (1.0 + scale) + shift


def _row_tile(t, want):
    tm = min(t, want)
    assert t % tm == 0 and tm % BF16_ROWS == 0
    return tm


def _ada_kernel(c_ref, w_ref, b_ref, o_ref):
    c = c_ref[...]
    s = c * jax.nn.sigmoid(c)
    o_ref[0] = _dot(s, w_ref[0]) + b_ref[0]


def _ada_mod(cond, ada_w, ada_b):
    depth, d, n = ada_w.shape
    tn = n // 4
    rows = cond.shape[0]
    return pl.pallas_call(
        _ada_kernel,
        grid=(depth, n // tn),
        in_specs=[pl.BlockSpec((rows, d), lambda l, j: (0, 0)),
                  pl.BlockSpec((1, d, tn), lambda l, j: (l, 0, j)),
                  pl.BlockSpec((1, 1, tn), lambda l, j: (l, 0, j))],
        out_specs=pl.BlockSpec((1, rows, tn), lambda l, j: (l, 0, j)),
        out_shape=jax.ShapeDtypeStruct((depth, rows, n), F32),
        compiler_params=_cparams("parallel", "parallel"),
        name="ada_mod",
    )(cond, ada_w, ada_b.reshape(depth, 1, n))


def _halo_specs(tm, t, d, nargs):
    r = tm // BF16_ROWS
    last = t // BF16_ROWS - 1
    if nargs == 3:
        prev = pl.BlockSpec((1, BF16_ROWS, d), lambda bi, i, j: (bi, jnp.maximum(i * r - 1, 0), 0))
        nxt = pl.BlockSpec((1, BF16_ROWS, d), lambda bi, i, j: (bi, jnp.minimum((i + 1) * r, last), 0))
    else:
        prev = pl.BlockSpec((1, BF16_ROWS, d), lambda bi, i: (bi, jnp.maximum(i * r - 1, 0), 0))
        nxt = pl.BlockSpec((1, BF16_ROWS, d), lambda bi, i: (bi, jnp.minimum((i + 1) * r, last), 0))
    return prev, nxt


def _fill_h(x_ref, xp_ref, xn_ref, g, shift, scale, h_scr, tm):
    h_scr[0:BF16_ROWS] = _norm_mod(xp_ref[0], g, shift, scale).astype(BF16)
    h_scr[BF16_ROWS:BF16_ROWS + tm] = _norm_mod(x_ref[0], g, shift, scale).astype(BF16)
    h_scr[BF16_ROWS + tm:2 * BF16_ROWS + tm] = _norm_mod(xn_ref[0], g, shift, scale).astype(BF16)


def _conv3_rows(u, u_scr, cw_ref, cb_ref, tm, t_total, part=0, nparts=1):
    i = pl.program_id(1)
    o = BF16_ROWS
    rows = tm // nparts
    base = part * (rows + 2 * o)
    head_ok = (i > 0) if part == 0 else True
    tail_ok = (i < t_total // tm - 1) if part == nparts - 1 else True
    u_scr[base:base + o] = jnp.where(head_ok, u[0:o], 0.0)
    u_scr[base + o:base + o + rows] = u[o:o + rows]
    u_scr[base + o + rows:base + 2 * o + rows] = jnp.where(tail_ok, u[o + rows:], 0.0)
    cw = cw_ref[...]
    s = base + o
    return (u_scr[s - 1:s - 1 + rows] * cw[0:1] + u_scr[s:s + rows] * cw[1:2] + u_scr[s + 1:s + 1 + rows] * cw[2:3]
            + cb_ref[...])


def _proj_conv_kernel(x_ref, xp_ref, xn_ref, g_ref, mod_ref, w_ref, b_ref, cw_ref, cb_ref, o_ref, h_scr, u_scr,
                      *, d, tm, t_total):
    @pl.when(pl.program_id(2) == 0)
    def _():
        m = mod_ref[0]
        _fill_h(x_ref, xp_ref, xn_ref, g_ref[...], m[:, 0:d], m[:, d:2 * d], h_scr, tm)

    nparts = FFN_ROW_PARTS
    rows = tm // nparts
    us = [jnp.dot(h_scr[p * rows:(p + 1) * rows + 2 * BF16_ROWS], w_ref[...], preferred_element_type=F32)
          + b_ref[...] for p in range(nparts)]
    for p in range(nparts):
        o_ref[0, 0, p * rows:(p + 1) * rows] = _conv3_rows(us[p], u_scr, cw_ref, cb_ref, tm, t_total, p, nparts)


def _proj_conv(x, g, mod, w, bias, cw, cb):
    b, t, d = x.shape
    n = w.shape[1]
    tm = _row_tile(t, 512)
    tn = d
    prev, nxt = _halo_specs(tm, t, d, 3)
    return pl.pallas_call(
        functools.partial(_proj_conv_kernel, d=d, tm=tm, t_total=t),
        grid=(b, t // tm, n // tn),
        in_specs=[pl.BlockSpec((1, tm, d), lambda bi, i, j: (bi, i, 0)), prev, nxt,
                  pl.BlockSpec((1, d), lambda bi, i, j: (0, 0)),
                  pl.BlockSpec((1, 1, mod.shape[-1]), lambda bi, i, j: (bi, 0, 0)),
                  pl.BlockSpec((d, tn), lambda bi, i, j: (0, j)),
                  pl.BlockSpec((1, tn), lambda bi, i, j: (0, j)),
                  pl.BlockSpec((3, tn), lambda bi, i, j: (0, j)),
                  pl.BlockSpec((1, tn), lambda bi, i, j: (0, j))],
        out_specs=pl.BlockSpec((1, 1, tm, tn), lambda bi, i, j: (j, bi, i, 0)),
        out_shape=jax.ShapeDtypeStruct((n // tn, b, t, tn), F32),
        scratch_shapes=[pltpu.VMEM((tm + 2 * BF16_ROWS, d), BF16),
                        pltpu.VMEM((tm + 2 * BF16_ROWS * FFN_ROW_PARTS, tn), F32)],
        compiler_params=_cparams("parallel", "parallel", "arbitrary"),
        name="hyena_in_proj",
    )(x, x, x, g.reshape(1, d), mod, w, bias.reshape(1, n), cw, cb.reshape(1, n))


def _ffn_kernel(x_ref, xp_ref, xn_ref, g_ref, mod_ref, wg_ref, wv_ref, cwg_ref, cwv_ref, cbg_ref, cbv_ref, wd_ref,
                o_ref, h_scr, ug_scr, uv_scr, acc_scr, *, d, tm, t_total):
    j = pl.program_id(2)

    @pl.when(j == 0)
    def _():
        m = mod_ref[0]
        _fill_h(x_ref, xp_ref, xn_ref, g_ref[...], m[:, 3 * d:4 * d], m[:, 4 * d:5 * d], h_scr, tm)
        acc_scr[...] = jnp.zeros_like(acc_scr)

    nparts = FFN_ROW_PARTS
    rows = tm // nparts
    hs = [h_scr[p * rows:(p + 1) * rows + 2 * BF16_ROWS] for p in range(nparts)]
    ug = [jnp.dot(h, wg_ref[...], preferred_element_type=F32) for h in hs]
    uv = [jnp.dot(h, wv_ref[...], preferred_element_type=F32) for h in hs]
    for p in range(nparts):
        gate = _conv3_rows(ug[p], ug_scr, cwg_ref, cbg_ref, tm, t_total, p, nparts)
        val = _conv3_rows(uv[p], uv_scr, cwv_ref, cbv_ref, tm, t_total, p, nparts)
        act = (gate * jax.nn.sigmoid(gate)) * val
        acc_scr[p * rows:(p + 1) * rows] += jnp.dot(act.astype(BF16), wd_ref[...], preferred_element_type=F32)

    @pl.when(j == pl.num_programs(2) - 1)
    def _():
        o_ref[0] = x_ref[0] + mod_ref[0][:, 5 * d:6 * d] * acc_scr[...]


def _conv_ffn(x, g, mod, w_up, cw, cb, w_down):
    b, t, d = x.shape
    f = w_down.shape[0]
    tm = _row_tile(t, 1024)
    tf = 256 if f % 256 == 0 else 128
    nf = f // tf
    prev, nxt = _halo_specs(tm, t, d, 3)
    cb2 = cb.reshape(1, 2 * f)
    return pl.pallas_call(
        functools.partial(_ffn_kernel, d=d, tm=tm, t_total=t),
        grid=(b, t // tm, nf),
        in_specs=[pl.BlockSpec((1, tm, d), lambda bi, i, j: (bi, i, 0)), prev, nxt,
                  pl.BlockSpec((1, d), lambda bi, i, j: (0, 0)),
                  pl.BlockSpec((1, 1, mod.shape[-1]), lambda bi, i, j: (bi, 0, 0)),
                  pl.BlockSpec((d, tf), lambda bi, i, j: (0, j)),
                  pl.BlockSpec((d, tf), lambda bi, i, j: (0, nf + j)),
                  pl.BlockSpec((3, tf), lambda bi, i, j: (0, j)),
                  pl.BlockSpec((3, tf), lambda bi, i, j: (0, nf + j)),
                  pl.BlockSpec((1, tf), lambda bi, i, j: (0, j)),
                  pl.BlockSpec((1, tf), lambda bi, i, j: (0, nf + j)),
                  pl.BlockSpec((tf, d), lambda bi, i, j: (j, 0))],
        out_specs=pl.BlockSpec((1, tm, d), lambda bi, i, j: (bi, i, 0)),
        out_shape=jax.ShapeDtypeStruct((b, t, d), F32),
        scratch_shapes=[pltpu.VMEM((tm + 2 * BF16_ROWS, d), BF16),
                        pltpu.VMEM((tm + 2 * BF16_ROWS * FFN_ROW_PARTS, tf), F32),
                        pltpu.VMEM((tm + 2 * BF16_ROWS * FFN_ROW_PARTS, tf), F32),
                        pltpu.VMEM((tm, d), F32)],
        compiler_params=_cparams("parallel", "parallel", "arbitrary"),
        name="conv_ffn",
    )(x, x, x, g.reshape(1, d), mod, w_up, w_up, cw, cw, cb2, cb2, w_down)


def _out_res_kernel(y_ref, x_ref, mod_ref, w_ref, b_ref, o_ref, *, d):
    o = _dot(y_ref[0], w_ref[...]) + b_ref[...]
    o_ref[0] = x_ref[0] + mod_ref[0][:, 2 * d:3 * d] * o


def _out_res(y, x, mod, w, bias):
    b, t, d = x.shape
    tm = _row_tile(t, 512)
    return pl.pallas_call(
        functools.partial(_out_res_kernel, d=d),
        grid=(b, t // tm),
        in_specs=[pl.BlockSpec((1, tm, d), lambda bi, i: (bi, i, 0)),
                  pl.BlockSpec((1, tm, d), lambda bi, i: (bi, i, 0)),
                  pl.BlockSpec((1, 1, mod.shape[-1]), lambda bi, i: (bi, 0, 0)),
                  pl.BlockSpec((d, d), lambda bi, i: (0, 0)),
                  pl.BlockSpec((1, d), lambda bi, i: (0, 0))],
        out_specs=pl.BlockSpec((1, tm, d), lambda bi, i: (bi, i, 0)),
        out_shape=jax.ShapeDtypeStruct((b, t, d), F32),
        compiler_params=_cparams("parallel", "parallel"),
        name="out_proj_residual",
    )(y, x, mod, w, bias.reshape(1, d))


def _even_prep_kernel(x_ref, xp_ref, xn_ref, gn_ref, mod_ref, win_ref, mup_ref, mun_ref, w0_ref, w2_ref, a0_ref,
                      a2_ref, g2_ref, kk_ref, ka_ref, qn_ref, kn_ref, cos_ref, sin_ref, e_ref,
                      r_ref, v_ref, kkn_ref, g_ref, lw_ref, a_ref, kd_ref, q_ref, katt_ref, vatt_ref,
                      h_scr, p_scr, *, tm, d, c, a_in, rope):
    i = pl.program_id(1)
    nt = pl.num_programs(1)
    m = mod_ref[0]
    _fill_h(x_ref, xp_ref, xn_ref, gn_ref[...], m[:, 0:d], m[:, d:2 * d], h_scr, tm)
    p_scr[...] = jnp.dot(h_scr[...], win_ref[...], preferred_element_type=F32)
    o = BF16_ROWS
    pa = p_scr[o:o + tm, 0:a_in]
    rows = lax.broadcasted_iota(jnp.int32, (tm, 1), 0)
    prev = jnp.where((rows == 0) & (i == 0), 0.0, p_scr[o - 1:o - 1 + tm, 0:a_in])
    nxt = jnp.where((rows == tm - 1) & (i == nt - 1), 0.0, p_scr[o + 1:o + 1 + tm, 0:a_in])
    za = pa + mup_ref[...] * (prev - pa) + mun_ref[...] * (nxt - pa)

    r = za[:, 0:c]
    k = za[:, c:2 * c]
    v = za[:, 2 * c:3 * c]
    wd = jnp.tanh(za[:, 3 * c:3 * c + 2 * DECAY_LORA])
    ad = za[:, 3 * c + 2 * DECAY_LORA:3 * c + 2 * DECAY_LORA + 2 * ICLR_LORA]
    gd = za[:, 3 * c + 2 * DECAY_LORA + 2 * ICLR_LORA:a_in]
    r_ref[0] = r
    v_ref[0] = v
    g_ref[0] = _dot(jax.nn.sigmoid(gd), g2_ref[...])
    kkv = k * kk_ref[...]
    ss = _segsum(kkv * kkv, e_ref)
    kkn_ref[0] = kkv / jnp.maximum(jnp.sqrt(ss), 1e-12)
    for dd in range(2):
        y = w0_ref[dd:dd + 1] + _dot(wd, w2_ref[dd])
        lw_ref[dd, 0] = -math.exp(-0.5) * jax.nn.sigmoid(y)
        a = jax.nn.sigmoid(a0_ref[dd:dd + 1] + _dot(ad, a2_ref[dd]))
        a_ref[dd, 0] = a
        kd_ref[dd, 0] = k * (1.0 + (a - 1.0) * ka_ref[...])

    pb = p_scr[o:o + tm, a_in:]
    nq = q_ref.shape[-1]
    nk = katt_ref.shape[-1]
    q = pb[:, 0:nq]
    kat = pb[:, nq:nq + nk]
    vatt_ref[0] = pb[:, nq + nk:]
    half = HEAD_DIM // 2

    def norm_rope(x, gain, e):
        w = x.shape[-1]
        ms = _segsum(x * x, e) * (1.0 / HEAD_DIM)
        xn = x * lax.rsqrt(ms + NORM_EPS) * gain
        if not rope:
            return xn
        lane = lax.broadcasted_iota(jnp.int32, (1, w), 1)
        first = (lane % HEAD_DIM) < half
        swapped = jnp.where(first, pltpu.roll(xn, w - half, 1), pltpu.roll(xn, half, 1))
        reps = w // LANES
        cos = jnp.concatenate([cos_ref[...]] * reps, axis=1) if reps > 1 else cos_ref[...]
        sin = jnp.concatenate([sin_ref[...]] * reps, axis=1) if reps > 1 else sin_ref[...]
        return xn * cos + swapped * sin

    q_ref[0] = norm_rope(q, qn_ref[...], e_ref[...])
    katt_ref[0] = norm_rope(kat, kn_ref[...], e_ref[0:nk, 0:nk])


def _even_prep(x, g, mod, ep, tabs, rope):
    b, t, d = x.shape
    n = ep['w_in'].shape[1]
    c = ep['k_k'].shape[-1]
    a_in = ep['mu_prev'].shape[-1]
    nq = ep['nq']
    nk = ep['nk']
    tm = _row_tile(t, 256)
    prev, nxt = _halo_specs(tm, t, d, 2)
    full = lambda *shape: pl.BlockSpec(shape, lambda bi, i: (0,) * len(shape))
    tok = lambda w: pl.BlockSpec((1, tm, w), lambda bi, i: (bi, i, 0))
    tok2 = lambda w: pl.BlockSpec((2, 1, tm, w), lambda bi, i: (0, bi, i, 0))
    sd = lambda *shape: jax.ShapeDtypeStruct(shape, F32)
    return pl.pallas_call(
        functools.partial(_even_prep_kernel, tm=tm, d=d, c=c, a_in=a_in, rope=rope),
        grid=(b, t // tm),
        in_specs=[tok(d), prev, nxt, full(1, d), pl.BlockSpec((1, 1, mod.shape[-1]), lambda bi, i: (bi, 0, 0)),
                  full(d, n), full(1, a_in), full(1, a_in), full(2, c), full(2, 2 * DECAY_LORA, c),
                  full(2, c), full(2, 2 * ICLR_LORA, c), full(GATE_LORA, c), full(1, c), full(1, c),
                  full(1, nq), full(1, nk),
                  pl.BlockSpec((tm, LANES), lambda bi, i: (i, 0)), pl.BlockSpec((tm, LANES), lambda bi, i: (i, 0)),
                  full(nq, nq)],
        out_specs=[tok(c), tok(c), tok(c), tok(c), tok2(c), tok2(c), tok2(c), tok(nq), tok(nk), tok(nk)],
        out_shape=[sd(b, t, c), sd(b, t, c), sd(b, t, c), sd(b, t, c), sd(2, b, t, c), sd(2, b, t, c),
                   sd(2, b, t, c), sd(b, t, nq), sd(b, t, nk), sd(b, t, nk)],
        scratch_shapes=[pltpu.VMEM((tm + 2 * BF16_ROWS, d), BF16), pltpu.VMEM((tm + 2 * BF16_ROWS, n), F32)],
        compiler_params=_cparams("parallel", "parallel"),
        name="even_in_proj_prep",
    )(x, x, x, g.reshape(1, d), mod, ep['w_in'], ep['mu_prev'], ep['mu_next'], ep['w0'], ep['w2pad'], ep['a0'], ep['a2pad'], ep['g2'], ep['k_k'],
      ep['k_a'], ep['q_norm_t'], ep['k_norm_t'], tabs['cos'][:t], tabs['sin'][:t], tabs['e'])


HEADS_PER_GROUP = 4
GROUP_LANES = HEADS_PER_GROUP * HEAD_DIM


def _split_bf16(x):
    hi = x.astype(BF16)
    return hi, (x - hi.astype(F32)).astype(BF16)


def _block_diag(x, bmask):
    return jnp.where(bmask, jnp.concatenate([x] * HEADS_PER_GROUP, axis=0), jnp.zeros((), x.dtype))


def _head_mm(lhs, rhs, bmask, passes, nt=False):
    dn = (((1,), (1,)), ((), ())) if nt else (((1,), (0,)), ((), ()))
    if passes == 1:
        return lax.dot_general(lhs.astype(BF16), _block_diag(rhs.astype(BF16), bmask), dn, preferred_element_type=F32)
    lh, ll = _split_bf16(lhs)
    rh, rl = _split_bf16(rhs)
    m = lhs.shape[0]
    top = lax.dot_general(jnp.concatenate([lh, ll], axis=0), _block_diag(rh, bmask), dn, preferred_element_type=F32)
    return top[0:m] + top[m:] + lax.dot_general(lh, _block_diag(rl, bmask), dn, preferred_element_type=F32)


def _head_mm_tn(lhs, rhs, lane_head, passes):
    dn = (((0,), (0,)), ((), ()))
    if passes == 1:
        full = lax.dot_general(lhs.astype(BF16), rhs.astype(BF16), dn, preferred_element_type=F32)
    else:
        lh, ll = _split_bf16(lhs)
        rh, rl = _split_bf16(rhs)
        full = (lax.dot_general(jnp.concatenate([lh, ll], axis=0), jnp.concatenate([rh, rh], axis=0), dn,
                                preferred_element_type=F32)
                + lax.dot_general(lh, rl, dn, preferred_element_type=F32))
    out = jnp.where(lane_head == 0, full[0:HEAD_DIM], 0.0)
    for h in range(1, HEADS_PER_GROUP):
        out = out + jnp.where(lane_head == h, full[h * HEAD_DIM:(h + 1) * HEAD_DIM], 0.0)
    return out


def _group_masks():
    r = lax.broadcasted_iota(jnp.int32, (GROUP_LANES, GROUP_LANES), 0)
    c = lax.broadcasted_iota(jnp.int32, (GROUP_LANES, GROUP_LANES), 1)
    bmask = (r // HEAD_DIM) == (c // HEAD_DIM)
    lane_head = lax.broadcasted_iota(jnp.int32, (1, GROUP_LANES), 1) // HEAD_DIM
    return bmask, lane_head


def _rwkv_prep_kernel(r_ref, v_ref, kk_ref, lw_ref, a_ref, kd_ref, r2_ref, yl_ref, t_ref, z_ref, *, cs):
    sgn = 1 - 2 * pl.program_id(0)
    row = lax.broadcasted_iota(jnp.int32, (cs, cs), 0)
    col = lax.broadcasted_iota(jnp.int32, (cs, cs), 1)
    tri = (((row - col) * sgn) >= 0).astype(F32)
    bmask, lane_head = _group_masks()
    t_idx = lax.broadcasted_iota(jnp.int32, (cs, GROUP_LANES), 0)
    s_idx = lax.broadcasted_iota(jnp.int32, (cs, GROUP_LANES), 1) % HEAD_DIM
    diff = (t_idx - s_idx) * sgn
    strict = diff > 0
    incl = diff >= 0
    eye = (diff == 0).astype(F32)
    base = min(RWKV_INV_BASE, cs)
    same_base = (t_idx // base) == (s_idx // base)
    groups = r_ref.shape[-1] // GROUP_LANES

    ch = []
    for c0 in range(0, r_ref.shape[1], cs):
        rows = slice(c0, c0 + cs)
        r, v, kk = r_ref[0, rows], v_ref[0, rows], kk_ref[0, rows]
        lw, a, kd = lw_ref[0, 0, rows], a_ref[0, 0, rows], kd_ref[0, 0, rows]
        g = _dot_hi(tri, lw)
        gp = g - lw
        gref = g[cs // 2:cs // 2 + 1]
        gend = jnp.sum(lw, axis=0, keepdims=True)
        bvec = kk * a
        full = {'a_t': -kk * jnp.exp(gp - gref), 'a_0': -kk * jnp.exp(gp),
                'b_t': bvec * jnp.exp(gref - g), 'b_e': bvec * jnp.exp(gend - g),
                'k_t': kd * jnp.exp(gref - g), 'k_e': kd * jnp.exp(gend - g),
                'r_t': r * jnp.exp(g - gref), 'r_0': r * jnp.exp(g), 'v': v,
                'wend': jnp.broadcast_to(jnp.exp(gend), (cs, v.shape[-1]))}
        for gi in range(groups):
            sl = slice(gi * GROUP_LANES, (gi + 1) * GROUP_LANES)
            c = {k: x[:, sl] for k, x in full.items()}
            c['rows'], c['sl'] = rows, sl
            ch.append(c)

    mm = lambda x, y, passes, nt=False: [_head_mm(p_, q_, bmask, passes, nt) for p_, q_ in zip(x, y)]
    get = lambda k: [c[k] for c in ch]
    ar = [jnp.concatenate([c['a_t'], c['r_t']], axis=0) for c in ch]
    gb = mm(ar, get('b_t'), RWKV_GRAM_PASSES, True)
    gk = mm(ar, get('k_t'), RWKV_GRAM_PASSES, True)
    nmat = [jnp.where(strict, x[0:cs], 0.0) for x in gb]
    g_ak = [jnp.where(strict, x[0:cs], 0.0) for x in gk]
    g_rb = [jnp.where(incl, x[cs:], 0.0) for x in gb]
    g_rk = [jnp.where(incl, x[cs:], 0.0) for x in gk]
    npow = [jnp.where(same_base, x, 0.0) for x in nmat]
    p = [eye + x for x in npow]
    for _ in range(int(math.log2(base)) - 1):
        npow = mm(npow, npow, RWKV_INV_PASSES)
        p = [x + y for x, y in zip(p, mm(npow, p, RWKV_INV_PASSES))]
    m = base
    while m < cs:
        off = ((t_idx // m) != (s_idx // m)) & ((t_idx // (2 * m)) == (s_idx // (2 * m)))
        q = mm([jnp.where(off, x, 0.0) for x in nmat], p, RWKV_INV_PASSES)
        p = [x + y for x, y in zip(p, mm(p, q, RWKV_INV_PASSES))]
        m *= 2
    vg = get('v')
    a2 = mm(p, get('a_0'), RWKV_REST_PASSES)
    u_v = mm(p, mm(g_ak, vg, RWKV_REST_PASSES), RWKV_REST_PASSES)
    r2 = mm(g_rb, a2, RWKV_REST_PASSES)
    yl1 = mm(g_rb, u_v, RWKV_REST_PASSES)
    yl2 = mm(g_rk, vg, RWKV_REST_PASSES)
    tt = [_head_mm_tn(x, c['b_e'], lane_head, RWKV_REST_PASSES) for x, c in zip(a2, ch)]
    zz = [_head_mm_tn(jnp.concatenate([u, c['v']], axis=0), jnp.concatenate([c['b_e'], c['k_e']], axis=0),
                      lane_head, RWKV_REST_PASSES) for u, c in zip(u_v, ch)]
    for i, c in enumerate(ch):
        rows, sl = c['rows'], c['sl']
        r2_ref[0, 0, rows, sl] = c['r_0'] + r2[i]
        yl_ref[0, 0, rows, sl] = yl1[i] + yl2[i]
        t_ref[0, 0, rows, sl] = eye * c['wend'] + tt[i]
        z_ref[0, 0, rows, sl] = zz[i]


def _rwkv_scan_kernel(h0_ref, r2f_ref, ylf_ref, tf_ref, zf_ref, r2b_ref, ylb_ref, tb_ref, zb_ref,
                      yf_ref, yb_ref, ht_ref, s_scr):
    ci = pl.program_id(0)

    @pl.when(ci == 0)
    def _():
        s_scr[...] = h0_ref[...]

    bmask, _ = _group_masks()
    ins = ((r2f_ref, ylf_ref, tf_ref, zf_ref, yf_ref), (r2b_ref, ylb_ref, tb_ref, zb_ref, yb_ref))
    for d, (r2_ref, yl_ref, t_ref, z_ref, y_ref) in enumerate(ins):
        for b in range(s_scr.shape[1]):
            for gi in range(s_scr.shape[-1] // GROUP_LANES):
                sl = slice(gi * GROUP_LANES, (gi + 1) * GROUP_LANES)
                s = s_scr[d, b, :, sl]
                y_ref[0, b, :, sl] = yl_ref[0, b, :, sl] + _head_mm(r2_ref[0, b, :, sl], s, bmask, 1, nt=True)
                s_scr[d, b, :, sl] = (_head_mm(s, t_ref[0, b, :, sl], bmask, RWKV_SCAN_PASSES)
                                      + z_ref[0, b, :, sl])

    @pl.when(ci == pl.num_programs(0) - 1)
    def _():
        ht_ref[...] = s_scr[...]


def _rwkv(r, v, kk, lw, a, kd, h0):
    b, t, c = r.shape
    cs = min(RWKV_CHUNK, t)
    assert cs == HEAD_DIM and c % GROUP_LANES == 0
    nc = t // cs
    rows = cs * math.gcd(nc, RWKV_PREP_CHUNKS)
    tok = pl.BlockSpec((1, rows, c), lambda d, bi, ci: (bi, ci, 0))
    tok2 = pl.BlockSpec((1, 1, rows, c), lambda d, bi, ci: (d, bi, ci, 0))
    wide = jax.ShapeDtypeStruct((2, b, t, c), F32)
    r2, yl, tt, zz = pl.pallas_call(
        functools.partial(_rwkv_prep_kernel, cs=cs),
        grid=(2, b, t // rows),
        in_specs=[tok, tok, tok, tok2, tok2, tok2],
        out_specs=[tok2, tok2, tok2, tok2],
        out_shape=[wide, wide, wide, wide],
        compiler_params=_cparams("parallel", "parallel", "parallel"),
        name="rwkv7_chunk_prep",
    )(r, v, kk, lw, a, kd)

    fwd = pl.BlockSpec((1, b, cs, c), lambda ci: (0, 0, ci, 0))
    bwd = pl.BlockSpec((1, b, cs, c), lambda ci: (1, 0, nc - 1 - ci, 0))
    st = pl.BlockSpec((2, b, HEAD_DIM, c), lambda ci: (0, 0, 0, 0))
    yf, yb, ht = pl.pallas_call(
        _rwkv_scan_kernel,
        grid=(nc,),
        in_specs=[st, fwd, fwd, fwd, fwd, bwd, bwd, bwd, bwd],
        out_specs=[pl.BlockSpec((1, b, cs, c), lambda ci: (0, 0, ci, 0)),
                   pl.BlockSpec((1, b, cs, c), lambda ci: (0, 0, nc - 1 - ci, 0)), st],
        out_shape=[jax.ShapeDtypeStruct((1, b, t, c), F32), jax.ShapeDtypeStruct((1, b, t, c), F32),
                   jax.ShapeDtypeStruct((2, b, HEAD_DIM, c), F32)],
        scratch_shapes=[pltpu.VMEM((2, b, HEAD_DIM, c), F32)],
        compiler_params=_cparams("arbitrary"),
        name="rwkv7_state_scan",
    )(h0, r2, yl, tt, zz, r2, yl, tt, zz)
    return (yf.reshape(b, t, c), yb.reshape(b, t, c)), ht


def _attn_kernel(sink_ref, q_ref, kp_ref, kc_ref, kn_ref, vp_ref, vc_ref, vn_ref, kx_ref, vx_ref, o_ref,
                 *, local, group):
    i = pl.program_id(1)
    nb = pl.num_programs(1)
    scale = HEAD_DIM ** -0.5
    q = q_ref[0]
    if local:
        k_all = jnp.concatenate([kp_ref[0], kc_ref[0], kn_ref[0], kx_ref[0]], axis=0)
        v_all = jnp.concatenate([vp_ref[0], vc_ref[0], vn_ref[0], vx_ref[0]], axis=0)
        nkeys = k_all.shape[0]
        qi = lax.broadcasted_iota(jnp.int32, (BLOCK, nkeys), 0)
        kj = lax.broadcasted_iota(jnp.int32, (BLOCK, nkeys), 1)
        kpos = kj + (i - 1) * BLOCK
        valid = (jnp.abs(kj - BLOCK - qi) <= WINDOW) & (kpos >= 0) & (kpos < nb * BLOCK)
        valid = valid | (kj >= 3 * BLOCK)
    else:
        k_all = kx_ref[0]
        v_all = vx_ref[0]
    heads = range(q.shape[-1] // HEAD_DIM)
    kv = lambda x, h: x[:, (h // group) * HEAD_DIM:(h // group + 1) * HEAD_DIM]
    s = [_dot_nt(q[:, h * HEAD_DIM:(h + 1) * HEAD_DIM], kv(k_all, h)) * scale for h in heads]
    if local:
        s = [jnp.where(valid, x, MASK_VALUE) for x in s]
    sink = [sink_ref[h] for h in heads]
    m = [jnp.maximum(jnp.max(x, axis=-1, keepdims=True), sk) for x, sk in zip(s, sink)]
    e = [jnp.exp(x - y) for x, y in zip(s, m)]
    den = [jnp.sum(x, axis=-1, keepdims=True) + jnp.exp(sk - y) for x, y, sk in zip(e, m, sink)]
    outs = [_dot(x, kv(v_all, h)) / dn for x, h, dn in zip(e, heads, den)]
    o_ref[0] = jnp.concatenate(outs, axis=1)


def _attention(q, k, v, kx, vx, sink, local):
    b, t, nq = q.shape
    nk = k.shape[-1]
    nb = t // BLOCK
    group = (nq // HEAD_DIM) // (nk // HEAD_DIM)
    lx = kx.shape[1]
    kv = lambda f: pl.BlockSpec((1, BLOCK, nk), f)
    pf = lambda bi, i: (bi, jnp.maximum(i - 1, 0), 0)
    cf = lambda bi, i: (bi, i, 0)
    nf = lambda bi, i: (bi, jnp.minimum(i + 1, nb - 1), 0)
    ctx = pl.BlockSpec((1, lx, nk), lambda bi, i: (bi, 0, 0))
    return pl.pallas_call(
        functools.partial(_attn_kernel, local=local, group=group),
        grid=(b, nb),
        in_specs=[pl.BlockSpec(memory_space=pltpu.SMEM),
                  pl.BlockSpec((1, BLOCK, nq), cf), kv(pf), kv(cf), kv(nf), kv(pf), kv(cf), kv(nf), ctx, ctx],
        out_specs=pl.BlockSpec((1, BLOCK, nq), cf),
        out_shape=jax.ShapeDtypeStruct((b, t, nq), F32),
        compiler_params=_cparams("parallel", "parallel"),
        name="window_attention" if local else "context_attention",
    )(sink, q, k, k, k, v, v, v, kx, vx)


def _even_out_kernel(yf_ref, yb_ref, r_ref, v_ref, g_ref, kd_ref, batt_ref, x_ref, mod_ref, lnw_ref, lnb_ref, rk_ref, e_ref,
                     w_ref, o_ref, *, d, c):
    y = yf_ref[0] + yb_ref[0]
    inv = 1.0 / HEAD_DIM
    mu = _segsum(y, e_ref) * inv
    yc = y - mu
    var = _segsum(yc * yc, e_ref) * inv
    yn = yc * lax.rsqrt(var + RWKV_GN_EPS) * lnw_ref[...] + lnb_ref[...]
    bonus = _segsum(r_ref[0] * (kd_ref[0, 0] + kd_ref[1, 0]) * rk_ref[...], e_ref)
    a_out = (yn + bonus * v_ref[0]) * g_ref[0]
    o = _dot(a_out, w_ref[0:c]) + _dot(batt_ref[0], w_ref[c:])
    o_ref[0] = x_ref[0] + mod_ref[0][:, 2 * d:3 * d] * o


def _even_out(y, r, v, g, kd, batt, x, mod, ep, tabs):
    b, t, d = x.shape
    c = r.shape[-1]
    nq = batt.shape[-1]
    tm = _row_tile(t, 512)
    tok = lambda w: pl.BlockSpec((1, tm, w), lambda bi, i: (bi, i, 0))
    tok2 = lambda w: pl.BlockSpec((2, 1, tm, w), lambda bi, i: (0, bi, i, 0))
    full = lambda *shape: pl.BlockSpec(shape, lambda bi, i: (0,) * len(shape))
    return pl.pallas_call(
        functools.partial(_even_out_kernel, d=d, c=c),
        grid=(b, t // tm),
        in_specs=[tok(c), tok(c), tok(c), tok(c), tok(c), tok2(c), tok(nq), tok(d),
                  pl.BlockSpec((1, 1, mod.shape[-1]), lambda bi, i: (bi, 0, 0)),
                  full(1, c), full(1, c), full(1, c), full(c, c), full(c + nq, d)],
        out_specs=tok(d),
        out_shape=jax.ShapeDtypeStruct((b, t, d), F32),
        compiler_params=_cparams("parallel", "parallel"),
        name="even_out_proj",
    )(y[0], y[1], r, v, g, kd, batt, x, mod, ep['ln_w'], ep['ln_b'], ep['r_k'], tabs['e'], ep['w_out'])


def _filter_kernel(z_ref, t_ref, w1_ref, b1_ref, w2_ref, b2_ref, w3_ref, b3_ref, fr_ref, wo_ref, dl_ref, o_ref, *, d):
    fr = fr_ref[...]
    h = jnp.sin(fr * (_dot_hi(z_ref[...], w1_ref[...]) + b1_ref[...]))
    h = jnp.sin(fr * (_dot_hi(h, w2_ref[...]) + b2_ref[...]))
    h = jnp.sin(fr * (_dot_hi(h, w3_ref[...]) + b3_ref[...]))
    filt = _dot_hi(h, wo_ref[...])
    modu = jnp.exp(-t_ref[...] * dl_ref[...]) + HY_MOD_SHIFT
    for q in range(o_ref.shape[0]):
        o_ref[q] = filt[:, q * d:(q + 1) * d] * modu


def _hyena_filters(n, op, d):
    t = np.linspace(0.0, 1.0, n, dtype=np.float32)[:, None]
    ang = (2.0 * math.pi * np.arange(n, dtype=np.float32)[:, None] / np.float32(n)).astype(np.float32)
    f = np.linspace(1e-4, HY_BANDS - 1, HY_BANDS, dtype=np.float32)[None, :]
    zfeat = jnp.concatenate([jnp.asarray(t), jnp.cos(jnp.asarray(f * ang)), -jnp.sin(jnp.asarray(f * ang))], axis=-1)
    emb_pad = op['f_w1'].shape[0]
    zfeat = jnp.pad(zfeat, ((0, 0), (0, emb_pad - HY_EMB)))
    deltas = np.abs(np.linspace(math.log(HY_TARGET) / HY_SLOW_PCT, math.log(HY_TARGET) / HY_FAST_PCT, d,
                                dtype=np.float32))[None, :]
    tn = min(n, 256)
    nq = 2 * HY_ORDER
    hf = op['f_w2'].shape[0]
    full = lambda *shape: pl.BlockSpec(shape, lambda i: (0,) * len(shape))
    return pl.pallas_call(
        functools.partial(_filter_kernel, d=d),
        grid=(n // tn,),
        in_specs=[pl.BlockSpec((tn, emb_pad), lambda i: (i, 0)), pl.BlockSpec((tn, 1), lambda i: (i, 0)),
                  full(emb_pad, hf), full(1, hf), full(hf, hf), full(1, hf), full(hf, hf), full(1, hf), full(1, hf),
                  full(hf, nq * d), full(1, d)],
        out_specs=pl.BlockSpec((nq, tn, d), lambda i: (0, i, 0)),
        out_shape=jax.ShapeDtypeStruct((nq, n, d), F32),
        compiler_params=_cparams("parallel"),
        name="hyena_filter",
    )(zfeat, jnp.asarray(t), op['f_w1'], op['f_b1'], op['f_w2'], op['f_b2'], op['f_w3'], op['f_b3'], op['f_freq'],
      op['f_out'], jnp.asarray(deltas))


def _dft(n, rows, cols, sign=-1.0):
    k = np.arange(rows, dtype=np.float64)[:, None]
    m = np.arange(cols, dtype=np.float64)[None, :]
    ang = sign * 2.0 * np.pi * ((k * m) % n) / n
    return np.cos(ang), np.sin(ang)


def _stack(re, im):
    return jnp.asarray(np.concatenate([re, im], axis=0).astype(np.float32)).astype(BF16)


def _fft_tables(n_seq):
    n = 2 * n_seq
    n1 = n // FFT_N2
    f1r, f1i = _dft(n1, n1, n1 // 2)
    h1r, h1i = _dft(n1, n1 // 2, n1, sign=1.0)
    k1 = np.arange(n1, dtype=np.float64)[:, None, None]
    k2 = np.arange(FFT_N2, dtype=np.float64)[None, :, None]
    j2 = np.arange(FFT_N2, dtype=np.float64)[None, None, :]
    ang = -2.0 * np.pi * (((k2 * j2 * n1) + k1 * j2) % n) / n
    gr, gi = np.cos(ang), np.sin(ang)
    g_fwd = np.concatenate([gr, gi], axis=1)
    g_inv = np.concatenate([np.swapaxes(gr, 1, 2), np.swapaxes(gi, 1, 2)], axis=1)
    eye = np.eye(SUBLANES)
    kron = lambda m: np.kron(m, eye)
    return {'f1': _stack(kron(f1r), kron(f1i)), 'h1': _stack(kron(h1r), kron(h1i)),
            'g_fwd': jnp.asarray(g_fwd.astype(np.float32)).astype(BF16),
            'g_inv': jnp.asarray(g_inv.astype(np.float32)).astype(BF16), 'n1': n1}


def _dense_tables(n_seq):
    n = 2 * n_seq
    fr, fi = _dft(n, n, n_seq)
    hr, hi = _dft(n, n_seq, n, sign=1.0)
    return {'f': _stack(fr, fi), 'h': _stack(hr, hi)}


def _fft_a_kernel(f_ref, zr_ref, zi_ref, ar_ref, ai_ref, *, n1, cplx):
    f = f_ref[...]
    half, tj, d = zr_ref.shape[1:]
    m = n1 * SUBLANES
    res_r, res_i = [], []
    for s0 in range(0, tj, SUBLANES):
        js = slice(s0, s0 + SUBLANES)
        zr = zr_ref[0, :, js, :].reshape(half * SUBLANES, d)
        p = jnp.dot(f, zr.astype(BF16), preferred_element_type=F32)
        if cplx:
            zi = zi_ref[0, :, js, :].reshape(half * SUBLANES, d)
            q = jnp.dot(f, zi.astype(BF16), preferred_element_type=F32)
            res_r.append((p[0:m] - q[m:]).reshape(n1, SUBLANES, d))
            res_i.append((p[m:] + q[0:m]).reshape(n1, SUBLANES, d))
        else:
            res_r.append(p[0:m].reshape(n1, SUBLANES, d))
            res_i.append(p[m:].reshape(n1, SUBLANES, d))
    ar_ref[0] = jnp.concatenate(res_r, axis=1).astype(BF16)
    ai_ref[0] = jnp.concatenate(res_i, axis=1).astype(BF16)


def _fft_a(u5, q, tabs, cplx):
    _, s, half, n2, d = u5.shape
    n1 = tabs['n1']
    assert half == n1 // 2 and n2 == FFT_N2
    sp = s // 2 if cplx else s
    tj = 2 * SUBLANES
    zi_map = (lambda si, j: (q, si + sp, 0, j, 0)) if cplx else (lambda si, j: (q, si, 0, j, 0))
    out = pl.BlockSpec((1, n1, tj, d), lambda si, j: (si, 0, j, 0))
    return pl.pallas_call(
        functools.partial(_fft_a_kernel, n1=n1, cplx=cplx),
        grid=(sp, n2 // tj),
        in_specs=[pl.BlockSpec((2 * n1 * SUBLANES, half * SUBLANES), lambda si, j: (0, 0)),
                  pl.BlockSpec((None, 1, half, tj, d), lambda si, j: (q, si, 0, j, 0)),
                  pl.BlockSpec((None, 1, half, tj, d), zi_map)],
        out_specs=[out, out],
        out_shape=[jax.ShapeDtypeStruct((sp, n1, n2, d), BF16)] * 2,
        compiler_params=_cparams("parallel", "parallel"),
        name="fft_stage1",
    )(tabs['f1'], u5, u5)


def _cplx_mm(s, xr, xi, conj):
    p = jnp.dot(s, xr.astype(BF16), preferred_element_type=F32)
    q = jnp.dot(s, xi.astype(BF16), preferred_element_type=F32)
    m = s.shape[0] // 2
    if conj:
        return p[0:m] + q[m:], q[0:m] - p[m:]
    return p[0:m] - q[m:], p[m:] + q[0:m]


def _fft_b_kernel(gf_ref, gi_ref, ar_ref, ai_ref, kr_ref, ki_ref, dr_ref, di_ref):
    kr = kr_ref[0]
    ki = ki_ref[0]
    seqs = range(ar_ref.shape[0])
    c = [_cplx_mm(gf_ref[0], ar_ref[s, 0], ai_ref[s, 0], False) for s in seqs]
    e = [(cr * kr - ci * ki, cr * ki + ci * kr) for cr, ci in c]
    dd = [_cplx_mm(gi_ref[0], er, ei, True) for er, ei in e]
    for s, (dr, di) in zip(seqs, dd):
        dr_ref[s, 0] = dr.astype(BF16)
        di_ref[s, 0] = di.astype(BF16)


def _fft_b(ar4, ai4, kr, ki, order, tabs):
    sp, n1, _, d = ar4.shape
    ns = 2 if sp % 2 == 0 else 1
    blk = pl.BlockSpec((ns, 1, FFT_N2, d), lambda k1, si: (si, k1, 0, 0))
    tab = pl.BlockSpec((1, 2 * FFT_N2, FFT_N2), lambda k1, si: (k1, 0, 0))
    kb = pl.BlockSpec((None, 1, FFT_N2, d), lambda k1, si: (order, k1, 0, 0))
    return pl.pallas_call(
        _fft_b_kernel,
        grid=(n1, sp // ns),
        in_specs=[tab, tab, blk, blk, kb, kb],
        out_specs=[blk, blk],
        out_shape=[jax.ShapeDtypeStruct((sp, n1, FFT_N2, d), BF16)] * 2,
        compiler_params=_cparams("parallel", "arbitrary"),
        name="fft_stage2_filter",
    )(tabs['g_fwd'], tabs['g_inv'], ar4, ai4, kr, ki)


def _fft_c_kernel(h_ref, dr_ref, di_ref, u_ref, x_ref, sk_ref, o_ref):
    h = h_ref[...]
    n1, tj, d = dr_ref.shape[1:]
    half = n1 // 2
    sk = sk_ref[...]
    dr_all = dr_ref[0].astype(F32)
    di_all = di_ref[0].astype(F32)
    for s0 in range(0, tj, SUBLANES):
        js = slice(s0, s0 + SUBLANES)
        yr, yi = _cplx_mm(h, dr_all[:, js, :].reshape(n1 * SUBLANES, d),
                          di_all[:, js, :].reshape(n1 * SUBLANES, d), False)
        for part, y in enumerate((yr, yi)):
            u = u_ref[part, 0, :, js, :]
            o_ref[part, 0, :, js, :] = x_ref[part, 0, :, js, :] * (y.reshape(half, SUBLANES, d) + u * sk)


def _fft_c(dr, di, u6, uq, x6, xq, skip, tabs):
    sp, n1, n2, d = dr.shape
    half = n1 // 2
    tj = 2 * SUBLANES
    pair = lambda q: pl.BlockSpec((None, 2, 1, half, tj, d), lambda si, j: (q, 0, si, 0, j, 0))
    dblk = pl.BlockSpec((1, n1, tj, d), lambda si, j: (si, 0, j, 0))
    out = pl.pallas_call(
        _fft_c_kernel,
        grid=(sp, n2 // tj),
        in_specs=[pl.BlockSpec((2 * half * SUBLANES, n1 * SUBLANES), lambda si, j: (0, 0)), dblk, dblk,
                  pair(uq), pair(xq), pl.BlockSpec((1, d), lambda si, j: (0, 0))],
        out_specs=pl.BlockSpec((2, 1, half, tj, d), lambda si, j: (0, si, 0, j, 0)),
        out_shape=jax.ShapeDtypeStruct((2, sp, half, n2, d), F32),
        compiler_params=_cparams("parallel", "parallel"),
        name="fft_inverse_stage1_gate",
    )(tabs['h1'], dr, di, u6, x6, skip.reshape(1, d))
    return out


def _spec_b_kernel(gf_ref, ar_ref, ai_ref, kr_ref, ki_ref, *, scale):
    fr, fi = _cplx_mm(gf_ref[0], ar_ref[0, 0], ai_ref[0, 0], False)
    gr, gi = _cplx_mm(gf_ref[0], ar_ref[1, 0], ai_ref[1, 0], False)
    kr_ref[0, 0] = (fr + gr) * scale
    ki_ref[0, 0] = (fi - gi) * scale


def _filter_spectrum_fft(filt, tabs, d):
    nq, n, _ = filt.shape
    n1 = tabs['n1']
    ar, ai = _fft_a(filt.reshape(1, nq, n1 // 2, FFT_N2, d), 0, tabs, cplx=False)
    orders = nq // 2
    ar5 = ar.reshape(orders, 2, n1, FFT_N2, d)
    ai5 = ai.reshape(orders, 2, n1, FFT_N2, d)
    blk = pl.BlockSpec((None, 2, 1, FFT_N2, d), lambda o, k1: (o, 0, k1, 0, 0))
    out = pl.BlockSpec((1, 1, FFT_N2, d), lambda o, k1: (o, k1, 0, 0))
    return pl.pallas_call(
        functools.partial(_spec_b_kernel, scale=1.0 / (2 * n)),
        grid=(orders, n1),
        in_specs=[pl.BlockSpec((1, 2 * FFT_N2, FFT_N2), lambda o, k1: (k1, 0, 0)), blk, blk],
        out_specs=[out, out],
        out_shape=[jax.ShapeDtypeStruct((orders, n1, FFT_N2, d), F32)] * 2,
        compiler_params=_cparams("parallel", "parallel"),
        name="filter_spectrum",
    )(tabs['g_fwd'], ar5, ai5)


def _long_conv_fft(u, uq, x, xq, skip, kr, ki, order, tabs):
    _, b, n, d = u.shape
    half = tabs['n1'] // 2
    ar, ai = _fft_a(u.reshape(u.shape[0], b, half, FFT_N2, d), uq, tabs, cplx=True)
    dr, di = _fft_b(ar, ai, kr, ki, order, tabs)
    six = lambda a: a.reshape(a.shape[0], 2, b // 2, half, FFT_N2, d)
    return _fft_c(dr, di, six(u), uq, six(x), xq, skip, tabs).reshape(b, n, d)


def _dense_spec_kernel(f_ref, filt_ref, kr_ref, ki_ref, *, scale):
    f = f_ref[...]
    n = f.shape[0] // 2
    pf = jnp.dot(f, filt_ref[0, 0].astype(BF16), preferred_element_type=F32)
    pg = jnp.dot(f, filt_ref[0, 1].astype(BF16), preferred_element_type=F32)
    kr_ref[0] = (pf[0:n] + pg[0:n]) * scale
    ki_ref[0] = (pf[n:] - pg[n:]) * scale


def _filter_spectrum_dense(filt, tabs, d):
    nq, n, _ = filt.shape
    orders = nq // 2
    f4 = filt.reshape(orders, 2, n, d)
    out = pl.BlockSpec((1, 2 * n, d), lambda o: (o, 0, 0))
    return pl.pallas_call(
        functools.partial(_dense_spec_kernel, scale=1.0 / (2 * n)),
        grid=(orders,),
        in_specs=[pl.BlockSpec((4 * n, n), lambda o: (0, 0)), pl.BlockSpec((1, 2, n, d), lambda o: (o, 0, 0, 0))],
        out_specs=[out, out],
        out_shape=[jax.ShapeDtypeStruct((orders, 2 * n, d), F32)] * 2,
        compiler_params=_cparams("parallel"),
        name="filter_spectrum_dense",
    )(tabs['f'], f4)


def _dense_conv_kernel(f_ref, h_ref, u_ref, x_ref, kr_ref, ki_ref, sk_ref, o_ref):
    cr, ci = _cplx_mm(f_ref[...], u_ref[0, 0], u_ref[1, 0], False)
    kr = kr_ref[...]
    ki = ki_ref[...]
    yr, yi = _cplx_mm(h_ref[...], cr * kr - ci * ki, cr * ki + ci * kr, False)
    sk = sk_ref[...]
    o_ref[0, 0] = x_ref[0, 0] * (yr + u_ref[0, 0] * sk)
    o_ref[1, 0] = x_ref[1, 0] * (yi + u_ref[1, 0] * sk)


def _long_conv_dense(u, x, skip, kr, ki, tabs):
    b, n, d = u.shape
    sp = b // 2
    pair = pl.BlockSpec((2, 1, n, d), lambda si: (0, si, 0, 0))
    kb = pl.BlockSpec((2 * n, d), lambda si: (0, 0))
    out = pl.pallas_call(
        _dense_conv_kernel,
        grid=(sp,),
        in_specs=[pl.BlockSpec((4 * n, n), lambda si: (0, 0)), pl.BlockSpec((2 * n, 2 * n), lambda si: (0, 0)),
                  pair, pair, kb, kb, pl.BlockSpec((1, d), lambda si: (0, 0))],
        out_specs=pair,
        out_shape=jax.ShapeDtypeStruct((2, sp, n, d), F32),
        compiler_params=_cparams("parallel"),
        name="long_conv_dense",
    )(tabs['f'], tabs['h'], u.reshape(2, sp, n, d), x.reshape(2, sp, n, d), kr, ki, skip.reshape(1, d))
    return out.reshape(b, n, d)


def _hyena_mixer(x, g, mod, op, fft_tabs, dense_tabs):
    b, n, d = x.shape
    z = _proj_conv(x, g, mod, op['w_in'], op['b_in'], op['conv_w'], op['conv_b'])
    filt = _hyena_filters(n, op, d)
    if n <= DENSE_FFT_MAX:
        kr, ki = _filter_spectrum_dense(filt, dense_tabs, d)
        y = _long_conv_dense(z[0], z[1], op['skip'][0], kr[0], ki[0], dense_tabs)
        y = _long_conv_dense(y, z[2], op['skip'][1], kr[1], ki[1], dense_tabs)
    else:
        kr, ki = _filter_spectrum_fft(filt, fft_tabs, d)
        y = _long_conv_fft(z, 0, z, 1, op['skip'][0], kr, ki, 0, fft_tabs)
        y = _long_conv_fft(y[None], 0, z, 2, op['skip'][1], kr, ki, 1, fft_tabs)
    return _out_res(y, x, mod, op['w_out'], op['b_out'])


def _even_mixer(x, ctx, g, mod_l, mod_c, ep, tabs, need_ctx):
    rc, vc, kkc, gc, lwc, ac, kdc, qc, kac, vac = _even_prep(ctx, g, mod_c, ep, tabs, rope=False)
    rl, vl, kkl, gl, lwl, al, kdl, ql, kal, val = _even_prep(x, g, mod_l, ep, tabs, rope=True)
    b = x.shape[0]
    c = rl.shape[-1]
    h0 = jnp.zeros((2, b, HEAD_DIM, c), F32)
    y_ctx, s_ctx = _rwkv(rc, vc, kkc, lwc, ac, kdc, h0)
    y_lat, _ = _rwkv(rl, vl, kkl, lwl, al, kdl, s_ctx)
    b_lat = _attention(ql, kal, val, kac, vac, ep['sink'], local=True)
    x_new = _even_out(y_lat, rl, vl, gl, kdl, b_lat, x, mod_l, ep, tabs)
    if not need_ctx:
        return x_new, None
    b_ctx = _attention(qc, kac, vac, kac, vac, ep['sink'], local=False)
    ctx_new = _even_out(y_ctx, rc, vc, gc, kdc, b_ctx, ctx, mod_c, ep, tabs)
    return x_new, ctx_new


def _rope_tables(n_tokens):
    rows = n_tokens // GRID_W
    row = jnp.repeat(jnp.arange(rows), GRID_W).astype(F32)
    col = jnp.tile(jnp.arange(GRID_W), rows).astype(F32)
    n_freq = HEAD_DIM // 4
    inv = ROPE_THETA ** (-jnp.arange(n_freq, dtype=F32) / n_freq)
    ang = jnp.concatenate([row[:, None] * inv, col[:, None] * inv], axis=-1)
    cos, sin = jnp.cos(ang), jnp.sin(ang)
    reps = LANES // HEAD_DIM
    cos_t = jnp.tile(jnp.concatenate([cos, cos], axis=-1), (1, reps))
    sin_t = jnp.tile(jnp.concatenate([-sin, sin], axis=-1), (1, reps))
    return cos_t, sin_t


def _block_ones(width):
    idx = np.arange(width) // HEAD_DIM
    return jnp.asarray((idx[:, None] == idx[None, :]).astype(np.float32)).astype(BF16)


def _lora_pad(w):
    z = jnp.zeros_like(w[0])
    return jnp.stack([jnp.concatenate([w[0], z], axis=0), jnp.concatenate([z, w[1]], axis=0)], axis=0)


def kernel(x, c, ctx, c_ctx, ada_w, ada_b, norm1_g, norm2_g, ffn_up, ffn_conv_w, ffn_conv_b, ffn_down, ev_w_in, ev_mu_prev, ev_mu_next, ev_w0, ev_w2, ev_a0, ev_a2, ev_g2, ev_k_k, ev_k_a, ev_r_k, ev_ln_w, ev_ln_b, ev_q_norm, ev_k_norm, ev_sink, ev_w_out, od_w_in, od_b_in, od_conv_w, od_conv_b, od_f_w1, od_f_b1, od_f_w2, od_f_b2, od_f_w3, od_f_b3, od_f_freq, od_f_out, od_skip, od_w_out, od_b_out):
    bsz, seq, d = x.shape
    lc = ctx.shape[1]
    depth = ada_w.shape[0]
    a_width = ev_k_k.shape[-1]
    nq = ev_sink.shape[-1] * HEAD_DIM
    nk = B_KV_HEADS * HEAD_DIM

    cond = jnp.zeros((BF16_ROWS, d), F32).at[:bsz].set(c).at[bsz].set(c_ctx)
    mod = _ada_mod(cond, ada_w, ada_b)

    cos_t, sin_t = _rope_tables(seq)
    tabs = {'cos': cos_t, 'sin': sin_t, 'e': _block_ones(nq)}
    fft_tabs = _fft_tables(seq) if seq > DENSE_FFT_MAX else None
    dense_lat = _dense_tables(seq) if seq <= DENSE_FFT_MAX else None
    fft_ctx = _fft_tables(lc) if lc > DENSE_FFT_MAX else None
    dense_ctx = _dense_tables(lc) if lc <= DENSE_FFT_MAX else None

    for layer in range(depth):
        need_ctx = layer < depth - 1
        even = layer % 2 == 0
        j = layer // 2
        mod_l = mod[layer, :bsz].reshape(bsz, 1, 6 * d)
        mod_c = jnp.broadcast_to(mod[layer, bsz].reshape(1, 1, 6 * d), (bsz, 1, 6 * d))
        if even:
            ep = {'w_in': ev_w_in[j].astype(BF16), 'mu_prev': ev_mu_prev[j][None], 'mu_next': ev_mu_next[j][None],
                  'w0': ev_w0[j], 'w2pad': _lora_pad(ev_w2[j]).astype(BF16), 'a0': ev_a0[j],
                  'a2pad': _lora_pad(ev_a2[j]).astype(BF16), 'g2': ev_g2[j].astype(BF16),
                  'k_k': ev_k_k[j][None], 'k_a': ev_k_a[j][None], 'r_k': ev_r_k[j].reshape(1, a_width),
                  'ln_w': ev_ln_w[j][None], 'ln_b': ev_ln_b[j][None],
                  'q_norm_t': jnp.tile(ev_q_norm[j], nq // HEAD_DIM)[None],
                  'k_norm_t': jnp.tile(ev_k_norm[j], nk // HEAD_DIM)[None],
                  'sink': ev_sink[j], 'w_out': ev_w_out[j].astype(BF16), 'nq': nq, 'nk': nk}
            x, ctx_new = _even_mixer(x, ctx, norm1_g[layer], mod_l, mod_c, ep, tabs, need_ctx)
        else:
            emb_pad = HEAD_DIM
            op = {'w_in': od_w_in[j].astype(BF16), 'b_in': od_b_in[j], 'conv_w': od_conv_w[j],
                  'conv_b': od_conv_b[j],
                  'f_w1': jnp.pad(od_f_w1[j], ((0, emb_pad - HY_EMB), (0, 0))), 'f_b1': od_f_b1[j][None],
                  'f_w2': od_f_w2[j], 'f_b2': od_f_b2[j][None], 'f_w3': od_f_w3[j], 'f_b3': od_f_b3[j][None],
                  'f_freq': od_f_freq[j][None], 'f_out': od_f_out[j], 'skip': od_skip[j],
                  'w_out': od_w_out[j].astype(BF16), 'b_out': od_b_out[j]}
            ctx_new = _hyena_mixer(ctx, norm1_g[layer], mod_c, op, fft_ctx, dense_ctx) if need_ctx else None
            x = _hyena_mixer(x, norm1_g[layer], mod_l, op, fft_tabs, dense_lat)
        w_up = ffn_up[layer].astype(BF16)
        w_down = ffn_down[layer].astype(BF16)
        x = _conv_ffn(x, norm2_g[layer], mod_l, w_up, ffn_conv_w[layer], ffn_conv_b[layer], w_down)
        if need_ctx:
            ctx = _conv_ffn(ctx_new, norm2_g[layer], mod_c, w_up, ffn_conv_w[layer], ffn_conv_b[layer], w_down)
    return x
```

```python
import functools
import math

import numpy as np
import jax
import jax.numpy as jnp
from jax import lax
from jax.experimental import pallas as pl
from jax.experimental.pallas import tpu as pltpu

F32 = jnp.float32
BF16 = jnp.bfloat16

HEAD_DIM = 64
GRID_W = 64
DECAY_LORA = 64
ICLR_LORA = 64
GATE_LORA = 128
RWKV_GN_EPS = 64e-5
B_KV_HEADS = 2
WINDOW = 128
BLOCK = 128
ROPE_THETA = 10000.0
MASK_VALUE = -1e30
HY_ORDER = 2
HY_EMB = 33
HY_BANDS = (HY_EMB - 1) // 2
HY_TARGET = 1e-2
HY_FAST_PCT = 0.3
HY_SLOW_PCT = 1.5
HY_MOD_SHIFT = 0.05
NORM_EPS = 1e-6

V7X_VMEM_BYTES = 64 * 1024 * 1024
VMEM_LIMIT_BYTES = V7X_VMEM_BYTES * 3 // 4
LANES = 128
SUBLANES = 8
BF16_ROWS = 16
FFN_ROW_PARTS = 2
RWKV_CHUNK = 64
RWKV_PREP_CHUNKS = 4
RWKV_INV_BASE = 8
RWKV_GRAM_PASSES = 1
RWKV_INV_PASSES = 1
RWKV_REST_PASSES = 1
RWKV_SCAN_PASSES = 1
FFT_N2 = 128
DENSE_FFT_MAX = 512

HIGHEST = lax.Precision.HIGHEST


def _cparams(*sem):
    return pltpu.CompilerParams(dimension_semantics=sem, vmem_limit_bytes=VMEM_LIMIT_BYTES)


def _dot(a, b):
    return jnp.dot(a.astype(BF16), b.astype(BF16), preferred_element_type=F32)


def _dot_nt(a, b):
    return lax.dot_general(a.astype(BF16), b.astype(BF16), (((1,), (1,)), ((), ())), preferred_element_type=F32)


def _dot_tn(a, b):
    return lax.dot_general(a.astype(BF16), b.astype(BF16), (((0,), (0,)), ((), ())), preferred_element_type=F32)


def _dot_hi(a, b):
    return jnp.dot(a, b, preferred_element_type=F32, precision=HIGHEST)


def _dot_nt_hi(a, b):
    return lax.dot_general(a, b, (((1,), (1,)), ((), ())), preferred_element_type=F32, precision=HIGHEST)


def _dot_tn_hi(a, b):
    return lax.dot_general(a, b, (((0,), (0,)), ((), ())), preferred_element_type=F32, precision=HIGHEST)


def _segsum(x, e_ref):
    hi = x.astype(BF16)
    lo = (x - hi.astype(F32)).astype(BF16)
    e = e_ref[...]
    return jnp.dot(hi, e, preferred_element_type=F32) + jnp.dot(lo, e, preferred_element_type=F32)


def _norm_mod(x, g, shift, scale):
    xn = x * lax.rsqrt(jnp.mean(x * x, axis=-1, keepdims=True) + NORM_EPS)
    return (xn * g) * (1.0 + scale) + shift


def _row_tile(t, want):
    tm = min(t, want)
    assert t % tm == 0 and tm % BF16_ROWS == 0
    return tm


def _ada_kernel(c_ref, w_ref, b_ref, o_ref):
    c = c_ref[...]
    s = c * jax.nn.sigmoid(c)
    o_ref[0] = _dot(s, w_ref[0]) + b_ref[0]


def _ada_mod(cond, ada_w, ada_b):
    depth, d, n = ada_w.shape
    tn = n // 4
    rows = cond.shape[0]
    return pl.pallas_call(
        _ada_kernel,
        grid=(depth, n // tn),
        in_specs=[pl.BlockSpec((rows, d), lambda l, j: (0, 0)),
                  pl.BlockSpec((1, d, tn), lambda l, j: (l, 0, j)),
                  pl.BlockSpec((1, 1, tn), lambda l, j: (l, 0, j))],
        out_specs=pl.BlockSpec((1, rows, tn), lambda l, j: (l, 0, j)),
        out_shape=jax.ShapeDtypeStruct((depth, rows, n), F32),
        compiler_params=_cparams("parallel", "parallel"),
        name="ada_mod",
    )(cond, ada_w, ada_b.reshape(depth, 1, n))


def _halo_specs(tm, t, d, nargs):
    r = tm // BF16_ROWS
    last = t // BF16_ROWS - 1
    if nargs == 3:
        prev = pl.BlockSpec((1, BF16_ROWS, d), lambda bi, i, j: (bi, jnp.maximum(i * r - 1, 0), 0))
        nxt = pl.BlockSpec((1, BF16_ROWS, d), lambda bi, i, j: (bi, jnp.minimum((i + 1) * r, last), 0))
    else:
        prev = pl.BlockSpec((1, BF16_ROWS, d), lambda bi, i: (bi, jnp.maximum(i * r - 1, 0), 0))
        nxt = pl.BlockSpec((1, BF16_ROWS, d), lambda bi, i: (bi, jnp.minimum((i + 1) * r, last), 0))
    return prev, nxt


def _fill_h(x_ref, xp_ref, xn_ref, g, shift, scale, h_scr, tm):
    h_scr[0:BF16_ROWS] = _norm_mod(xp_ref[0], g, shift, scale).astype(BF16)
    h_scr[BF16_ROWS:BF16_ROWS + tm] = _norm_mod(x_ref[0], g, shift, scale).astype(BF16)
    h_scr[BF16_ROWS + tm:2 * BF16_ROWS + tm] = _norm_mod(xn_ref[0], g, shift, scale).astype(BF16)


def _conv3_rows(u, u_scr, cw, cb, tm, t_total, part=0, nparts=1):
    i = pl.program_id(1)
    o = BF16_ROWS
    rows = tm // nparts
    base = part * (rows + 2 * o)
    head_ok = (i > 0) if part == 0 else True
    tail_ok = (i < t_total // tm - 1) if part == nparts - 1 else True
    u_scr[base:base + o] = jnp.where(head_ok, u[0:o], 0.0)
    u_scr[base + o:base + o + rows] = u[o:o + rows]
    u_scr[base + o + rows:base + 2 * o + rows] = jnp.where(tail_ok, u[o + rows:], 0.0)
    s = base + o
    return (u_scr[s - 1:s - 1 + rows] * cw[0:1] + u_scr[s:s + rows] * cw[1:2] + u_scr[s + 1:s + 1 + rows] * cw[2:3]
            + cb)


def _proj_conv_kernel(x_ref, xp_ref, xn_ref, g_ref, mod_ref, w_ref, b_ref, cw_ref, cb_ref, o_ref, h_scr, u_scr,
                      *, d, tm, t_total):
    @pl.when(pl.program_id(2) == 0)
    def _():
        m = mod_ref[0]
        _fill_h(x_ref, xp_ref, xn_ref, g_ref[...], m[:, 0:d], m[:, d:2 * d], h_scr, tm)

    nparts = FFN_ROW_PARTS
    rows = tm // nparts
    us = [jnp.dot(h_scr[p * rows:(p + 1) * rows + 2 * BF16_ROWS], w_ref[...], preferred_element_type=F32)
          + b_ref[...] for p in range(nparts)]
    for p in range(nparts):
        o_ref[0, 0, p * rows:(p + 1) * rows] = _conv3_rows(us[p], u_scr, cw_ref[...], cb_ref[...], tm, t_total, p,
                                                           nparts)


def _proj_conv(x, g, mod, w, bias, cw, cb):
    b, t, d = x.shape
    n = w.shape[1]
    tm = _row_tile(t, 512)
    tn = d
    prev, nxt = _halo_specs(tm, t, d, 3)
    return pl.pallas_call(
        functools.partial(_proj_conv_kernel, d=d, tm=tm, t_total=t),
        grid=(b, t // tm, n // tn),
        in_specs=[pl.BlockSpec((1, tm, d), lambda bi, i, j: (bi, i, 0)), prev, nxt,
                  pl.BlockSpec((1, d), lambda bi, i, j: (0, 0)),
                  pl.BlockSpec((1, 1, mod.shape[-1]), lambda bi, i, j: (bi, 0, 0)),
                  pl.BlockSpec((d, tn), lambda bi, i, j: (0, j)),
                  pl.BlockSpec((1, tn), lambda bi, i, j: (0, j)),
                  pl.BlockSpec((3, tn), lambda bi, i, j: (0, j)),
                  pl.BlockSpec((1, tn), lambda bi, i, j: (0, j))],
        out_specs=pl.BlockSpec((1, 1, tm, tn), lambda bi, i, j: (j, bi, i, 0)),
        out_shape=jax.ShapeDtypeStruct((n // tn, b, t, tn), F32),
        scratch_shapes=[pltpu.VMEM((tm + 2 * BF16_ROWS, d), BF16),
                        pltpu.VMEM((tm + 2 * BF16_ROWS * FFN_ROW_PARTS, tn), F32)],
        compiler_params=_cparams("parallel", "parallel", "arbitrary"),
        name="hyena_in_proj",
    )(x, x, x, g.reshape(1, d), mod, w, bias.reshape(1, n), cw, cb.reshape(1, n))


def _ffn_kernel(x_ref, xp_ref, xn_ref, g_ref, mod_ref, wup_ref, cw_ref, cb_ref, wd_ref,
                o_ref, h_scr, ug_scr, uv_scr, acc_scr, *, d, tm, t_total, f, tf):
    m = mod_ref[0]
    _fill_h(x_ref, xp_ref, xn_ref, g_ref[...], m[:, 3 * d:4 * d], m[:, 4 * d:5 * d], h_scr, tm)
    nparts = FFN_ROW_PARTS
    rows = tm // nparts
    for j in range(f // tf):
        gs = slice(j * tf, (j + 1) * tf)
        vs = slice(f + j * tf, f + (j + 1) * tf)
        hs = [h_scr[p * rows:(p + 1) * rows + 2 * BF16_ROWS] for p in range(nparts)]
        ug = [jnp.dot(h, wup_ref[:, gs], preferred_element_type=F32) for h in hs]
        uv = [jnp.dot(h, wup_ref[:, vs], preferred_element_type=F32) for h in hs]
        for p in range(nparts):
            gate = _conv3_rows(ug[p], ug_scr.at[j % 2], cw_ref[:, gs], cb_ref[:, gs], tm, t_total, p, nparts)
            val = _conv3_rows(uv[p], uv_scr.at[j % 2], cw_ref[:, vs], cb_ref[:, vs], tm, t_total, p, nparts)
            act = (gate * jax.nn.sigmoid(gate)) * val
            down = jnp.dot(act.astype(BF16), wd_ref[gs, :], preferred_element_type=F32)
            part = slice(p * rows, (p + 1) * rows)
            if j == 0:
                acc_scr[part] = down
            else:
                acc_scr[part] += down
    o_ref[0] = x_ref[0] + m[:, 5 * d:6 * d] * acc_scr[...]


def _conv_ffn(x, g, mod, w_up, cw, cb, w_down):
    b, t, d = x.shape
    f = w_down.shape[0]
    tm = _row_tile(t, 1024)
    tf = 256 if f % 256 == 0 else 128
    prev, nxt = _halo_specs(tm, t, d, 2)
    resident = lambda *shape: pl.BlockSpec(shape, lambda bi, i: (0,) * len(shape), pipeline_mode=pl.Buffered(1))
    u_rows = tm + 2 * BF16_ROWS * FFN_ROW_PARTS
    return pl.pallas_call(
        functools.partial(_ffn_kernel, d=d, tm=tm, t_total=t, f=f, tf=tf),
        grid=(b, t // tm),
        in_specs=[pl.BlockSpec((1, tm, d), lambda bi, i: (bi, i, 0)), prev, nxt,
                  pl.BlockSpec((1, d), lambda bi, i: (0, 0)),
                  pl.BlockSpec((1, 1, mod.shape[-1]), lambda bi, i: (bi, 0, 0)),
                  resident(d, 2 * f), resident(3, 2 * f), resident(1, 2 * f), resident(f, d)],
        out_specs=pl.BlockSpec((1, tm, d), lambda bi, i: (bi, i, 0)),
        out_shape=jax.ShapeDtypeStruct((b, t, d), F32),
        scratch_shapes=[pltpu.VMEM((tm + 2 * BF16_ROWS, d), BF16),
                        pltpu.VMEM((2, u_rows, tf), F32),
                        pltpu.VMEM((2, u_rows, tf), F32),
                        pltpu.VMEM((tm, d), F32)],
        compiler_params=_cparams("parallel", "parallel"),
        name="conv_ffn",
    )(x, x, x, g.reshape(1, d), mod, w_up, cw, cb.reshape(1, 2 * f), w_down)


def _out_res_kernel(y_ref, x_ref, mod_ref, w_ref, b_ref, o_ref, *, d):
    o = _dot(y_ref[0], w_ref[...]) + b_ref[...]
    o_ref[0] = x_ref[0] + mod_ref[0][:, 2 * d:3 * d] * o


def _out_res(y, x, mod, w, bias):
    b, t, d = x.shape
    tm = _row_tile(t, 512)
    return pl.pallas_call(
        functools.partial(_out_res_kernel, d=d),
        grid=(b, t // tm),
        in_specs=[pl.BlockSpec((1, tm, d), lambda bi, i: (bi, i, 0)),
                  pl.BlockSpec((1, tm, d), lambda bi, i: (bi, i, 0)),
                  pl.BlockSpec((1, 1, mod.shape[-1]), lambda bi, i: (bi, 0, 0)),
                  pl.BlockSpec((d, d), lambda bi, i: (0, 0)),
                  pl.BlockSpec((1, d), lambda bi, i: (0, 0))],
        out_specs=pl.BlockSpec((1, tm, d), lambda bi, i: (bi, i, 0)),
        out_shape=jax.ShapeDtypeStruct((b, t, d), F32),
        compiler_params=_cparams("parallel", "parallel"),
        name="out_proj_residual",
    )(y, x, mod, w, bias.reshape(1, d))


def _even_prep_kernel(x_ref, xp_ref, xn_ref, gn_ref, mod_ref, win_ref, mup_ref, mun_ref, w0_ref, w2_ref, a0_ref,
                      a2_ref, g2_ref, kk_ref, ka_ref, qn_ref, kn_ref, cos_ref, sin_ref, e_ref,
                      r_ref, v_ref, kkn_ref, g_ref, lw_ref, a_ref, kd_ref, q_ref, katt_ref, vatt_ref,
                      h_scr, p_scr, *, tm, d, c, a_in, rope):
    i = pl.program_id(1)
    nt = pl.num_programs(1)
    m = mod_ref[0]
    _fill_h(x_ref, xp_ref, xn_ref, gn_ref[...], m[:, 0:d], m[:, d:2 * d], h_scr, tm)
    p_scr[...] = jnp.dot(h_scr[...], win_ref[...], preferred_element_type=F32)
    o = BF16_ROWS
    pa = p_scr[o:o + tm, 0:a_in]
    rows = lax.broadcasted_iota(jnp.int32, (tm, 1), 0)
    prev = jnp.where((rows == 0) & (i == 0), 0.0, p_scr[o - 1:o - 1 + tm, 0:a_in])
    nxt = jnp.where((rows == tm - 1) & (i == nt - 1), 0.0, p_scr[o + 1:o + 1 + tm, 0:a_in])
    za = pa + mup_ref[...] * (prev - pa) + mun_ref[...] * (nxt - pa)

    r = za[:, 0:c]
    k = za[:, c:2 * c]
    v = za[:, 2 * c:3 * c]
    wd = jnp.tanh(za[:, 3 * c:3 * c + 2 * DECAY_LORA])
    ad = za[:, 3 * c + 2 * DECAY_LORA:3 * c + 2 * DECAY_LORA + 2 * ICLR_LORA]
    gd = za[:, 3 * c + 2 * DECAY_LORA + 2 * ICLR_LORA:a_in]
    r_ref[0] = r
    v_ref[0] = v
    g_ref[0] = _dot(jax.nn.sigmoid(gd), g2_ref[...])
    kkv = k * kk_ref[...]
    ss = _segsum(kkv * kkv, e_ref)
    kkn_ref[0] = kkv / jnp.maximum(jnp.sqrt(ss), 1e-12)
    for dd in range(2):
        y = w0_ref[dd:dd + 1] + _dot(wd, w2_ref[dd])
        lw_ref[dd, 0] = -math.exp(-0.5) * jax.nn.sigmoid(y)
        a = jax.nn.sigmoid(a0_ref[dd:dd + 1] + _dot(ad, a2_ref[dd]))
        a_ref[dd, 0] = a
        kd_ref[dd, 0] = k * (1.0 + (a - 1.0) * ka_ref[...])

    pb = p_scr[o:o + tm, a_in:]
    nq = q_ref.shape[-1]
    nk = katt_ref.shape[-1]
    q = pb[:, 0:nq]
    kat = pb[:, nq:nq + nk]
    vatt_ref[0] = pb[:, nq + nk:]
    half = HEAD_DIM // 2

    def norm_rope(x, gain, e):
        w = x.shape[-1]
        ms = _segsum(x * x, e) * (1.0 / HEAD_DIM)
        xn = x * lax.rsqrt(ms + NORM_EPS) * gain
        if not rope:
            return xn
        lane = lax.broadcasted_iota(jnp.int32, (1, w), 1)
        first = (lane % HEAD_DIM) < half
        swapped = jnp.where(first, pltpu.roll(xn, w - half, 1), pltpu.roll(xn, half, 1))
        reps = w // LANES
        cos = jnp.concatenate([cos_ref[...]] * reps, axis=1) if reps > 1 else cos_ref[...]
        sin = jnp.concatenate([sin_ref[...]] * reps, axis=1) if reps > 1 else sin_ref[...]
        return xn * cos + swapped * sin

    q_ref[0] = norm_rope(q, qn_ref[...], e_ref[...])
    katt_ref[0] = norm_rope(kat, kn_ref[...], e_ref[0:nk, 0:nk])


def _even_prep(x, g, mod, ep, tabs, rope):
    b, t, d = x.shape
    n = ep['w_in'].shape[1]
    c = ep['k_k'].shape[-1]
    a_in = ep['mu_prev'].shape[-1]
    nq = ep['nq']
    nk = ep['nk']
    tm = _row_tile(t, 256)
    prev, nxt = _halo_specs(tm, t, d, 2)
    full = lambda *shape: pl.BlockSpec(shape, lambda bi, i: (0,) * len(shape))
    tok = lambda w: pl.BlockSpec((1, tm, w), lambda bi, i: (bi, i, 0))
    tok2 = lambda w: pl.BlockSpec((2, 1, tm, w), lambda bi, i: (0, bi, i, 0))
    sd = lambda *shape: jax.ShapeDtypeStruct(shape, F32)
    return pl.pallas_call(
        functools.partial(_even_prep_kernel, tm=tm, d=d, c=c, a_in=a_in, rope=rope),
        grid=(b, t // tm),
        in_specs=[tok(d), prev, nxt, full(1, d), pl.BlockSpec((1, 1, mod.shape[-1]), lambda bi, i: (bi, 0, 0)),
                  full(d, n), full(1, a_in), full(1, a_in), full(2, c), full(2, 2 * DECAY_LORA, c),
                  full(2, c), full(2, 2 * ICLR_LORA, c), full(GATE_LORA, c), full(1, c), full(1, c),
                  full(1, nq), full(1, nk),
                  pl.BlockSpec((tm, LANES), lambda bi, i: (i, 0)), pl.BlockSpec((tm, LANES), lambda bi, i: (i, 0)),
                  full(nq, nq)],
        out_specs=[tok(c), tok(c), tok(c), tok(c), tok2(c), tok2(c), tok2(c), tok(nq), tok(nk), tok(nk)],
        out_shape=[sd(b, t, c), sd(b, t, c), sd(b, t, c), sd(b, t, c), sd(2, b, t, c), sd(2, b, t, c),
                   sd(2, b, t, c), sd(b, t, nq), sd(b, t, nk), sd(b, t, nk)],
        scratch_shapes=[pltpu.VMEM((tm + 2 * BF16_ROWS, d), BF16), pltpu.VMEM((tm + 2 * BF16_ROWS, n), F32)],
        compiler_params=_cparams("parallel", "parallel"),
        name="even_in_proj_prep",
    )(x, x, x, g.reshape(1, d), mod, ep['w_in'], ep['mu_prev'], ep['mu_next'], ep['w0'], ep['w2pad'], ep['a0'], ep['a2pad'], ep['g2'], ep['k_k'],
      ep['k_a'], ep['q_norm_t'], ep['k_norm_t'], tabs['cos'][:t], tabs['sin'][:t], tabs['e'])


HEADS_PER_GROUP = 4
GROUP_LANES = HEADS_PER_GROUP * HEAD_DIM


def _split_bf16(x):
    hi = x.astype(BF16)
    return hi, (x - hi.astype(F32)).astype(BF16)


def _block_diag(x, bmask):
    return jnp.where(bmask, jnp.concatenate([x] * HEADS_PER_GROUP, axis=0), jnp.zeros((), x.dtype))


def _head_mm(lhs, rhs, bmask, passes, nt=False):
    dn = (((1,), (1,)), ((), ())) if nt else (((1,), (0,)), ((), ()))
    if passes == 1:
        return lax.dot_general(lhs.astype(BF16), _block_diag(rhs.astype(BF16), bmask), dn, preferred_element_type=F32)
    lh, ll = _split_bf16(lhs)
    rh, rl = _split_bf16(rhs)
    m = lhs.shape[0]
    top = lax.dot_general(jnp.concatenate([lh, ll], axis=0), _block_diag(rh, bmask), dn, preferred_element_type=F32)
    return top[0:m] + top[m:] + lax.dot_general(lh, _block_diag(rl, bmask), dn, preferred_element_type=F32)


def _head_mm_tn(lhs, rhs, lane_head, passes):
    dn = (((0,), (0,)), ((), ()))
    if passes == 1:
        full = lax.dot_general(lhs.astype(BF16), rhs.astype(BF16), dn, preferred_element_type=F32)
    else:
        lh, ll = _split_bf16(lhs)
        rh, rl = _split_bf16(rhs)
        full = (lax.dot_general(jnp.concatenate([lh, ll], axis=0), jnp.concatenate([rh, rh], axis=0), dn,
                                preferred_element_type=F32)
                + lax.dot_general(lh, rl, dn, preferred_element_type=F32))
    out = jnp.where(lane_head == 0, full[0:HEAD_DIM], 0.0)
    for h in range(1, HEADS_PER_GROUP):
        out = out + jnp.where(lane_head == h, full[h * HEAD_DIM:(h + 1) * HEAD_DIM], 0.0)
    return out


def _group_masks():
    r = lax.broadcasted_iota(jnp.int32, (GROUP_LANES, GROUP_LANES), 0)
    c = lax.broadcasted_iota(jnp.int32, (GROUP_LANES, GROUP_LANES), 1)
    bmask = (r // HEAD_DIM) == (c // HEAD_DIM)
    lane_head = lax.broadcasted_iota(jnp.int32, (1, GROUP_LANES), 1) // HEAD_DIM
    return bmask, lane_head


def _rwkv_prep_kernel(r_ref, v_ref, kk_ref, lw_ref, a_ref, kd_ref, r2_ref, yl_ref, t_ref, z_ref, *, cs):
    sgn = 1 - 2 * pl.program_id(0)
    row = lax.broadcasted_iota(jnp.int32, (cs, cs), 0)
    col = lax.broadcasted_iota(jnp.int32, (cs, cs), 1)
    tri = (((row - col) * sgn) >= 0).astype(F32)
    bmask, lane_head = _group_masks()
    t_idx = lax.broadcasted_iota(jnp.int32, (cs, GROUP_LANES), 0)
    s_idx = lax.broadcasted_iota(jnp.int32, (cs, GROUP_LANES), 1) % HEAD_DIM
    diff = (t_idx - s_idx) * sgn
    strict = diff > 0
    incl = diff >= 0
    eye = (diff == 0).astype(F32)
    base = min(RWKV_INV_BASE, cs)
    same_base = (t_idx // base) == (s_idx // base)
    groups = r_ref.shape[-1] // GROUP_LANES

    ch = []
    for c0 in range(0, r_ref.shape[1], cs):
        rows = slice(c0, c0 + cs)
        r, v, kk = r_ref[0, rows], v_ref[0, rows], kk_ref[0, rows]
        lw, a, kd = lw_ref[0, 0, rows], a_ref[0, 0, rows], kd_ref[0, 0, rows]
        g = _dot_hi(tri, lw)
        gp = g - lw
        gref = g[cs // 2:cs // 2 + 1]
        gend = jnp.sum(lw, axis=0, keepdims=True)
        bvec = kk * a
        full = {'a_t': -kk * jnp.exp(gp - gref), 'a_0': -kk * jnp.exp(gp),
                'b_t': bvec * jnp.exp(gref - g), 'b_e': bvec * jnp.exp(gend - g),
                'k_t': kd * jnp.exp(gref - g), 'k_e': kd * jnp.exp(gend - g),
                'r_t': r * jnp.exp(g - gref), 'r_0': r * jnp.exp(g), 'v': v,
                'wend': jnp.broadcast_to(jnp.exp(gend), (cs, v.shape[-1]))}
        for gi in range(groups):
            sl = slice(gi * GROUP_LANES, (gi + 1) * GROUP_LANES)
            c = {k: x[:, sl] for k, x in full.items()}
            c['rows'], c['sl'] = rows, sl
            ch.append(c)

    mm = lambda x, y, passes, nt=False: [_head_mm(p_, q_, bmask, passes, nt) for p_, q_ in zip(x, y)]
    get = lambda k: [c[k] for c in ch]
    ar = [jnp.concatenate([c['a_t'], c['r_t']], axis=0) for c in ch]
    gb = mm(ar, get('b_t'), RWKV_GRAM_PASSES, True)
    gk = mm(ar, get('k_t'), RWKV_GRAM_PASSES, True)
    nmat = [jnp.where(strict, x[0:cs], 0.0) for x in gb]
    g_ak = [jnp.where(strict, x[0:cs], 0.0) for x in gk]
    g_rb = [jnp.where(incl, x[cs:], 0.0) for x in gb]
    g_rk = [jnp.where(incl, x[cs:], 0.0) for x in gk]
    npow = [jnp.where(same_base, x, 0.0) for x in nmat]
    p = [eye + x for x in npow]
    for _ in range(int(math.log2(base)) - 1):
        npow = mm(npow, npow, RWKV_INV_PASSES)
        p = [x + y for x, y in zip(p, mm(npow, p, RWKV_INV_PASSES))]
    m = base
    while m < cs:
        off = ((t_idx // m) != (s_idx // m)) & ((t_idx // (2 * m)) == (s_idx // (2 * m)))
        q = mm([jnp.where(off, x, 0.0) for x in nmat], p, RWKV_INV_PASSES)
        p = [x + y for x, y in zip(p, mm(p, q, RWKV_INV_PASSES))]
        m *= 2
    vg = get('v')
    a2 = mm(p, get('a_0'), RWKV_REST_PASSES)
    u_v = mm(p, mm(g_ak, vg, RWKV_REST_PASSES), RWKV_REST_PASSES)
    r2 = mm(g_rb, a2, RWKV_REST_PASSES)
    yl1 = mm(g_rb, u_v, RWKV_REST_PASSES)
    yl2 = mm(g_rk, vg, RWKV_REST_PASSES)
    tt = [_head_mm_tn(x, c['b_e'], lane_head, RWKV_REST_PASSES) for x, c in zip(a2, ch)]
    zz = [_head_mm_tn(jnp.concatenate([u, c['v']], axis=0), jnp.concatenate([c['b_e'], c['k_e']], axis=0),
                      lane_head, RWKV_REST_PASSES) for u, c in zip(u_v, ch)]
    for i, c in enumerate(ch):
        rows, sl = c['rows'], c['sl']
        r2_ref[0, 0, rows, sl] = c['r_0'] + r2[i]
        yl_ref[0, 0, rows, sl] = yl1[i] + yl2[i]
        t_ref[0, 0, rows, sl] = eye * c['wend'] + tt[i]
        z_ref[0, 0, rows, sl] = zz[i]


def _rwkv_scan_kernel(h0_ref, r2f_ref, ylf_ref, tf_ref, zf_ref, r2b_ref, ylb_ref, tb_ref, zb_ref,
                      yf_ref, yb_ref, ht_ref, s_scr):
    ci = pl.program_id(0)

    @pl.when(ci == 0)
    def _():
        s_scr[...] = h0_ref[...]

    bmask, _ = _group_masks()
    ins = ((r2f_ref, ylf_ref, tf_ref, zf_ref, yf_ref), (r2b_ref, ylb_ref, tb_ref, zb_ref, yb_ref))
    for d, (r2_ref, yl_ref, t_ref, z_ref, y_ref) in enumerate(ins):
        for b in range(s_scr.shape[1]):
            for gi in range(s_scr.shape[-1] // GROUP_LANES):
                sl = slice(gi * GROUP_LANES, (gi + 1) * GROUP_LANES)
                s = s_scr[d, b, :, sl]
                y_ref[0, b, :, sl] = yl_ref[0, b, :, sl] + _head_mm(r2_ref[0, b, :, sl], s, bmask, 1, nt=True)
                s_scr[d, b, :, sl] = (_head_mm(s, t_ref[0, b, :, sl], bmask, RWKV_SCAN_PASSES)
                                      + z_ref[0, b, :, sl])

    @pl.when(ci == pl.num_programs(0) - 1)
    def _():
        ht_ref[...] = s_scr[...]


def _rwkv(r, v, kk, lw, a, kd, h0):
    b, t, c = r.shape
    cs = min(RWKV_CHUNK, t)
    assert cs == HEAD_DIM and c % GROUP_LANES == 0
    nc = t // cs
    rows = cs * math.gcd(nc, RWKV_PREP_CHUNKS)
    tok = pl.BlockSpec((1, rows, c), lambda d, bi, ci: (bi, ci, 0))
    tok2 = pl.BlockSpec((1, 1, rows, c), lambda d, bi, ci: (d, bi, ci, 0))
    wide = jax.ShapeDtypeStruct((2, b, t, c), F32)
    r2, yl, tt, zz = pl.pallas_call(
        functools.partial(_rwkv_prep_kernel, cs=cs),
        grid=(2, b, t // rows),
        in_specs=[tok, tok, tok, tok2, tok2, tok2],
        out_specs=[tok2, tok2, tok2, tok2],
        out_shape=[wide, wide, wide, wide],
        compiler_params=_cparams("parallel", "parallel", "parallel"),
        name="rwkv7_chunk_prep",
    )(r, v, kk, lw, a, kd)

    fwd = pl.BlockSpec((1, b, cs, c), lambda ci: (0, 0, ci, 0))
    bwd = pl.BlockSpec((1, b, cs, c), lambda ci: (1, 0, nc - 1 - ci, 0))
    st = pl.BlockSpec((2, b, HEAD_DIM, c), lambda ci: (0, 0, 0, 0))
    yf, yb, ht = pl.pallas_call(
        _rwkv_scan_kernel,
        grid=(nc,),
        in_specs=[st, fwd, fwd, fwd, fwd, bwd, bwd, bwd, bwd],
        out_specs=[pl.BlockSpec((1, b, cs, c), lambda ci: (0, 0, ci, 0)),
                   pl.BlockSpec((1, b, cs, c), lambda ci: (0, 0, nc - 1 - ci, 0)), st],
        out_shape=[jax.ShapeDtypeStruct((1, b, t, c), F32), jax.ShapeDtypeStruct((1, b, t, c), F32),
                   jax.ShapeDtypeStruct((2, b, HEAD_DIM, c), F32)],
        scratch_shapes=[pltpu.VMEM((2, b, HEAD_DIM, c), F32)],
        compiler_params=_cparams("arbitrary"),
        name="rwkv7_state_scan",
    )(h0, r2, yl, tt, zz, r2, yl, tt, zz)
    return (yf.reshape(b, t, c), yb.reshape(b, t, c)), ht


def _attn_kernel(sink_ref, q_ref, kp_ref, kc_ref, kn_ref, vp_ref, vc_ref, vn_ref, kx_ref, vx_ref, o_ref,
                 *, local, group):
    i = pl.program_id(1)
    nb = pl.num_programs(1)
    scale = HEAD_DIM ** -0.5
    q = q_ref[0]
    if local:
        k_all = jnp.concatenate([kp_ref[0], kc_ref[0], kn_ref[0], kx_ref[0]], axis=0)
        v_all = jnp.concatenate([vp_ref[0], vc_ref[0], vn_ref[0], vx_ref[0]], axis=0)
        nkeys = k_all.shape[0]
        qi = lax.broadcasted_iota(jnp.int32, (BLOCK, nkeys), 0)
        kj = lax.broadcasted_iota(jnp.int32, (BLOCK, nkeys), 1)
        kpos = kj + (i - 1) * BLOCK
        valid = (jnp.abs(kj - BLOCK - qi) <= WINDOW) & (kpos >= 0) & (kpos < nb * BLOCK)
        valid = valid | (kj >= 3 * BLOCK)
    else:
        k_all = kx_ref[0]
        v_all = vx_ref[0]
    heads = range(q.shape[-1] // HEAD_DIM)
    kv = lambda x, h: x[:, (h // group) * HEAD_DIM:(h // group + 1) * HEAD_DIM]
    s = [_dot_nt(q[:, h * HEAD_DIM:(h + 1) * HEAD_DIM], kv(k_all, h)) * scale for h in heads]
    if local:
        s = [jnp.where(valid, x, MASK_VALUE) for x in s]
    sink = [sink_ref[h] for h in heads]
    m = [jnp.maximum(jnp.max(x, axis=-1, keepdims=True), sk) for x, sk in zip(s, sink)]
    e = [jnp.exp(x - y) for x, y in zip(s, m)]
    den = [jnp.sum(x, axis=-1, keepdims=True) + jnp.exp(sk - y) for x, y, sk in zip(e, m, sink)]
    outs = [_dot(x, kv(v_all, h)) / dn for x, h, dn in zip(e, heads, den)]
    o_ref[0] = jnp.concatenate(outs, axis=1)


def _attention(q, k, v, kx, vx, sink, local):
    b, t, nq = q.shape
    nk = k.shape[-1]
    nb = t // BLOCK
    group = (nq // HEAD_DIM) // (nk // HEAD_DIM)
    lx = kx.shape[1]
    kv = lambda f: pl.BlockSpec((1, BLOCK, nk), f)
    pf = lambda bi, i: (bi, jnp.maximum(i - 1, 0), 0)
    cf = lambda bi, i: (bi, i, 0)
    nf = lambda bi, i: (bi, jnp.minimum(i + 1, nb - 1), 0)
    ctx = pl.BlockSpec((1, lx, nk), lambda bi, i: (bi, 0, 0))
    return pl.pallas_call(
        functools.partial(_attn_kernel, local=local, group=group),
        grid=(b, nb),
        in_specs=[pl.BlockSpec(memory_space=pltpu.SMEM),
                  pl.BlockSpec((1, BLOCK, nq), cf), kv(pf), kv(cf), kv(nf), kv(pf), kv(cf), kv(nf), ctx, ctx],
        out_specs=pl.BlockSpec((1, BLOCK, nq), cf),
        out_shape=jax.ShapeDtypeStruct((b, t, nq), F32),
        compiler_params=_cparams("parallel", "parallel"),
        name="window_attention" if local else "context_attention",
    )(sink, q, k, k, k, v, v, v, kx, vx)


def _even_out_kernel(yf_ref, yb_ref, r_ref, v_ref, g_ref, kd_ref, batt_ref, x_ref, mod_ref, lnw_ref, lnb_ref, rk_ref, e_ref,
                     w_ref, o_ref, *, d, c):
    y = yf_ref[0] + yb_ref[0]
    inv = 1.0 / HEAD_DIM
    mu = _segsum(y, e_ref) * inv
    yc = y - mu
    var = _segsum(yc * yc, e_ref) * inv
    yn = yc * lax.rsqrt(var + RWKV_GN_EPS) * lnw_ref[...] + lnb_ref[...]
    bonus = _segsum(r_ref[0] * (kd_ref[0, 0] + kd_ref[1, 0]) * rk_ref[...], e_ref)
    a_out = (yn + bonus * v_ref[0]) * g_ref[0]
    o = _dot(a_out, w_ref[0:c]) + _dot(batt_ref[0], w_ref[c:])
    o_ref[0] = x_ref[0] + mod_ref[0][:, 2 * d:3 * d] * o


def _even_out(y, r, v, g, kd, batt, x, mod, ep, tabs):
    b, t, d = x.shape
    c = r.shape[-1]
    nq = batt.shape[-1]
    tm = _row_tile(t, 512)
    tok = lambda w: pl.BlockSpec((1, tm, w), lambda bi, i: (bi, i, 0))
    tok2 = lambda w: pl.BlockSpec((2, 1, tm, w), lambda bi, i: (0, bi, i, 0))
    full = lambda *shape: pl.BlockSpec(shape, lambda bi, i: (0,) * len(shape))
    return pl.pallas_call(
        functools.partial(_even_out_kernel, d=d, c=c),
        grid=(b, t // tm),
        in_specs=[tok(c), tok(c), tok(c), tok(c), tok(c), tok2(c), tok(nq), tok(d),
                  pl.BlockSpec((1, 1, mod.shape[-1]), lambda bi, i: (bi, 0, 0)),
                  full(1, c), full(1, c), full(1, c), full(c, c), full(c + nq, d)],
        out_specs=tok(d),
        out_shape=jax.ShapeDtypeStruct((b, t, d), F32),
        compiler_params=_cparams("parallel", "parallel"),
        name="even_out_proj",
    )(y[0], y[1], r, v, g, kd, batt, x, mod, ep['ln_w'], ep['ln_b'], ep['r_k'], tabs['e'], ep['w_out'])


def _filter_kernel(z_ref, t_ref, w1_ref, b1_ref, w2_ref, b2_ref, w3_ref, b3_ref, fr_ref, wo_ref, dl_ref, o_ref, *, d):
    fr = fr_ref[...]
    h = jnp.sin(fr * (_dot_hi(z_ref[...], w1_ref[...]) + b1_ref[...]))
    h = jnp.sin(fr * (_dot_hi(h, w2_ref[...]) + b2_ref[...]))
    h = jnp.sin(fr * (_dot_hi(h, w3_ref[...]) + b3_ref[...]))
    filt = _dot_hi(h, wo_ref[...])
    modu = jnp.exp(-t_ref[...] * dl_ref[...]) + HY_MOD_SHIFT
    for q in range(o_ref.shape[0]):
        o_ref[q] = filt[:, q * d:(q + 1) * d] * modu


def _hyena_filters(n, op, d):
    t = np.linspace(0.0, 1.0, n, dtype=np.float32)[:, None]
    ang = (2.0 * math.pi * np.arange(n, dtype=np.float32)[:, None] / np.float32(n)).astype(np.float32)
    f = np.linspace(1e-4, HY_BANDS - 1, HY_BANDS, dtype=np.float32)[None, :]
    zfeat = jnp.concatenate([jnp.asarray(t), jnp.cos(jnp.asarray(f * ang)), -jnp.sin(jnp.asarray(f * ang))], axis=-1)
    emb_pad = op['f_w1'].shape[0]
    zfeat = jnp.pad(zfeat, ((0, 0), (0, emb_pad - HY_EMB)))
    deltas = np.abs(np.linspace(math.log(HY_TARGET) / HY_SLOW_PCT, math.log(HY_TARGET) / HY_FAST_PCT, d,
                                dtype=np.float32))[None, :]
    tn = min(n, 256)
    nq = 2 * HY_ORDER
    hf = op['f_w2'].shape[0]
    full = lambda *shape: pl.BlockSpec(shape, lambda i: (0,) * len(shape))
    return pl.pallas_call(
        functools.partial(_filter_kernel, d=d),
        grid=(n // tn,),
        in_specs=[pl.BlockSpec((tn, emb_pad), lambda i: (i, 0)), pl.BlockSpec((tn, 1), lambda i: (i, 0)),
                  full(emb_pad, hf), full(1, hf), full(hf, hf), full(1, hf), full(hf, hf), full(1, hf), full(1, hf),
                  full(hf, nq * d), full(1, d)],
        out_specs=pl.BlockSpec((nq, tn, d), lambda i: (0, i, 0)),
        out_shape=jax.ShapeDtypeStruct((nq, n, d), F32),
        compiler_params=_cparams("parallel"),
        name="hyena_filter",
    )(zfeat, jnp.asarray(t), op['f_w1'], op['f_b1'], op['f_w2'], op['f_b2'], op['f_w3'], op['f_b3'], op['f_freq'],
      op['f_out'], jnp.asarray(deltas))


def _dft(n, rows, cols, sign=-1.0):
    k = np.arange(rows, dtype=np.float64)[:, None]
    m = np.arange(cols, dtype=np.float64)[None, :]
    ang = sign * 2.0 * np.pi * ((k * m) % n) / n
    return np.cos(ang), np.sin(ang)


def _stack(re, im):
    return jnp.asarray(np.concatenate([re, im], axis=0).astype(np.float32)).astype(BF16)


def _fft_tables(n_seq):
    n = 2 * n_seq
    n1 = n // FFT_N2
    f1r, f1i = _dft(n1, n1, n1 // 2)
    h1r, h1i = _dft(n1, n1 // 2, n1, sign=1.0)
    k1 = np.arange(n1, dtype=np.float64)[:, None, None]
    k2 = np.arange(FFT_N2, dtype=np.float64)[None, :, None]
    j2 = np.arange(FFT_N2, dtype=np.float64)[None, None, :]
    ang = -2.0 * np.pi * (((k2 * j2 * n1) + k1 * j2) % n) / n
    gr, gi = np.cos(ang), np.sin(ang)
    g_fwd = np.concatenate([gr, gi], axis=1)
    g_inv = np.concatenate([np.swapaxes(gr, 1, 2), np.swapaxes(gi, 1, 2)], axis=1)
    eye = np.eye(SUBLANES)
    kron = lambda m: np.kron(m, eye)
    return {'f1': _stack(kron(f1r), kron(f1i)), 'h1': _stack(kron(h1r), kron(h1i)),
            'g_fwd': jnp.asarray(g_fwd.astype(np.float32)).astype(BF16),
            'g_inv': jnp.asarray(g_inv.astype(np.float32)).astype(BF16), 'n1': n1}


def _dense_tables(n_seq):
    n = 2 * n_seq
    fr, fi = _dft(n, n, n_seq)
    hr, hi = _dft(n, n_seq, n, sign=1.0)
    return {'f': _stack(fr, fi), 'h': _stack(hr, hi)}


def _fft_a_kernel(f_ref, zr_ref, zi_ref, ar_ref, ai_ref, *, n1, cplx):
    f = f_ref[...]
    half, tj, d = zr_ref.shape[1:]
    m = n1 * SUBLANES
    res_r, res_i = [], []
    for s0 in range(0, tj, SUBLANES):
        js = slice(s0, s0 + SUBLANES)
        zr = zr_ref[0, :, js, :].reshape(half * SUBLANES, d)
        p = jnp.dot(f, zr.astype(BF16), preferred_element_type=F32)
        if cplx:
            zi = zi_ref[0, :, js, :].reshape(half * SUBLANES, d)
            q = jnp.dot(f, zi.astype(BF16), preferred_element_type=F32)
            res_r.append((p[0:m] - q[m:]).reshape(n1, SUBLANES, d))
            res_i.append((p[m:] + q[0:m]).reshape(n1, SUBLANES, d))
        else:
            res_r.append(p[0:m].reshape(n1, SUBLANES, d))
            res_i.append(p[m:].reshape(n1, SUBLANES, d))
    ar_ref[0] = jnp.concatenate(res_r, axis=1).astype(BF16)
    ai_ref[0] = jnp.concatenate(res_i, axis=1).astype(BF16)


def _fft_a(u5, q, tabs, cplx):
    _, s, half, n2, d = u5.shape
    n1 = tabs['n1']
    assert half == n1 // 2 and n2 == FFT_N2
    sp = s // 2 if cplx else s
    tj = 2 * SUBLANES
    zi_map = (lambda si, j: (q, si + sp, 0, j, 0)) if cplx else (lambda si, j: (q, si, 0, j, 0))
    out = pl.BlockSpec((1, n1, tj, d), lambda si, j: (si, 0, j, 0))
    return pl.pallas_call(
        functools.partial(_fft_a_kernel, n1=n1, cplx=cplx),
        grid=(sp, n2 // tj),
        in_specs=[pl.BlockSpec((2 * n1 * SUBLANES, half * SUBLANES), lambda si, j: (0, 0)),
                  pl.BlockSpec((None, 1, half, tj, d), lambda si, j: (q, si, 0, j, 0)),
                  pl.BlockSpec((None, 1, half, tj, d), zi_map)],
        out_specs=[out, out],
        out_shape=[jax.ShapeDtypeStruct((sp, n1, n2, d), BF16)] * 2,
        compiler_params=_cparams("parallel", "parallel"),
        name="fft_stage1",
    )(tabs['f1'], u5, u5)


def _cplx_mm(s, xr, xi, conj):
    p = jnp.dot(s, xr.astype(BF16), preferred_element_type=F32)
    q = jnp.dot(s, xi.astype(BF16), preferred_element_type=F32)
    m = s.shape[0] // 2
    if conj:
        return p[0:m] + q[m:], q[0:m] - p[m:]
    return p[0:m] - q[m:], p[m:] + q[0:m]


def _fft_b_kernel(gf_ref, gi_ref, ar_ref, ai_ref, kr_ref, ki_ref, dr_ref, di_ref):
    kr = kr_ref[0]
    ki = ki_ref[0]
    seqs = range(ar_ref.shape[0])
    c = [_cplx_mm(gf_ref[0], ar_ref[s, 0], ai_ref[s, 0], False) for s in seqs]
    e = [(cr * kr - ci * ki, cr * ki + ci * kr) for cr, ci in c]
    dd = [_cplx_mm(gi_ref[0], er, ei, True) for er, ei in e]
    for s, (dr, di) in zip(seqs, dd):
        dr_ref[s, 0] = dr.astype(BF16)
        di_ref[s, 0] = di.astype(BF16)


def _fft_b(ar4, ai4, kr, ki, order, tabs):
    sp, n1, _, d = ar4.shape
    ns = 2 if sp % 2 == 0 else 1
    blk = pl.BlockSpec((ns, 1, FFT_N2, d), lambda k1, si: (si, k1, 0, 0))
    tab = pl.BlockSpec((1, 2 * FFT_N2, FFT_N2), lambda k1, si: (k1, 0, 0))
    kb = pl.BlockSpec((None, 1, FFT_N2, d), lambda k1, si: (order, k1, 0, 0))
    return pl.pallas_call(
        _fft_b_kernel,
        grid=(n1, sp // ns),
        in_specs=[tab, tab, blk, blk, kb, kb],
        out_specs=[blk, blk],
        out_shape=[jax.ShapeDtypeStruct((sp, n1, FFT_N2, d), BF16)] * 2,
        compiler_params=_cparams("parallel", "arbitrary"),
        name="fft_stage2_filter",
    )(tabs['g_fwd'], tabs['g_inv'], ar4, ai4, kr, ki)


def _fft_c_kernel(h_ref, dr_ref, di_ref, u_ref, x_ref, sk_ref, o_ref):
    h = h_ref[...]
    n1, tj, d = dr_ref.shape[1:]
    half = n1 // 2
    sk = sk_ref[...]
    dr_all = dr_ref[0].astype(F32)
    di_all = di_ref[0].astype(F32)
    for s0 in range(0, tj, SUBLANES):
        js = slice(s0, s0 + SUBLANES)
        yr, yi = _cplx_mm(h, dr_all[:, js, :].reshape(n1 * SUBLANES, d),
                          di_all[:, js, :].reshape(n1 * SUBLANES, d), False)
        for part, y in enumerate((yr, yi)):
            u = u_ref[part, 0, :, js, :]
            o_ref[part, 0, :, js, :] = x_ref[part, 0, :, js, :] * (y.reshape(half, SUBLANES, d) + u * sk)


def _fft_c(dr, di, u6, uq, x6, xq, skip, tabs):
    sp, n1, n2, d = dr.shape
    half = n1 // 2
    tj = 2 * SUBLANES
    pair = lambda q: pl.BlockSpec((None, 2, 1, half, tj, d), lambda si, j: (q, 0, si, 0, j, 0))
    dblk = pl.BlockSpec((1, n1, tj, d), lambda si, j: (si, 0, j, 0))
    out = pl.pallas_call(
        _fft_c_kernel,
        grid=(sp, n2 // tj),
        in_specs=[pl.BlockSpec((2 * half * SUBLANES, n1 * SUBLANES), lambda si, j: (0, 0)), dblk, dblk,
                  pair(uq), pair(xq), pl.BlockSpec((1, d), lambda si, j: (0, 0))],
        out_specs=pl.BlockSpec((2, 1, half, tj, d), lambda si, j: (0, si, 0, j, 0)),
        out_shape=jax.ShapeDtypeStruct((2, sp, half, n2, d), F32),
        compiler_params=_cparams("parallel", "parallel"),
        name="fft_inverse_stage1_gate",
    )(tabs['h1'], dr, di, u6, x6, skip.reshape(1, d))
    return out


def _spec_b_kernel(gf_ref, ar_ref, ai_ref, kr_ref, ki_ref, *, scale):
    fr, fi = _cplx_mm(gf_ref[0], ar_ref[0, 0], ai_ref[0, 0], False)
    gr, gi = _cplx_mm(gf_ref[0], ar_ref[1, 0], ai_ref[1, 0], False)
    kr_ref[0, 0] = (fr + gr) * scale
    ki_ref[0, 0] = (fi - gi) * scale


def _filter_spectrum_fft(filt, tabs, d):
    nq, n, _ = filt.shape
    n1 = tabs['n1']
    ar, ai = _fft_a(filt.reshape(1, nq, n1 // 2, FFT_N2, d), 0, tabs, cplx=False)
    orders = nq // 2
    ar5 = ar.reshape(orders, 2, n1, FFT_N2, d)
    ai5 = ai.reshape(orders, 2, n1, FFT_N2, d)
    blk = pl.BlockSpec((None, 2, 1, FFT_N2, d), lambda o, k1: (o, 0, k1, 0, 0))
    out = pl.BlockSpec((1, 1, FFT_N2, d), lambda o, k1: (o, k1, 0, 0))
    return pl.pallas_call(
        functools.partial(_spec_b_kernel, scale=1.0 / (2 * n)),
        grid=(orders, n1),
        in_specs=[pl.BlockSpec((1, 2 * FFT_N2, FFT_N2), lambda o, k1: (k1, 0, 0)), blk, blk],
        out_specs=[out, out],
        out_shape=[jax.ShapeDtypeStruct((orders, n1, FFT_N2, d), F32)] * 2,
        compiler_params=_cparams("parallel", "parallel"),
        name="filter_spectrum",
    )(tabs['g_fwd'], ar5, ai5)


def _long_conv_fft(u, uq, x, xq, skip, kr, ki, order, tabs):
    _, b, n, d = u.shape
    half = tabs['n1'] // 2
    ar, ai = _fft_a(u.reshape(u.shape[0], b, half, FFT_N2, d), uq, tabs, cplx=True)
    dr, di = _fft_b(ar, ai, kr, ki, order, tabs)
    six = lambda a: a.reshape(a.shape[0], 2, b // 2, half, FFT_N2, d)
    return _fft_c(dr, di, six(u), uq, six(x), xq, skip, tabs).reshape(b, n, d)


def _dense_spec_kernel(f_ref, filt_ref, kr_ref, ki_ref, *, scale):
    f = f_ref[...]
    n = f.shape[0] // 2
    pf = jnp.dot(f, filt_ref[0, 0].astype(BF16), preferred_element_type=F32)
    pg = jnp.dot(f, filt_ref[0, 1].astype(BF16), preferred_element_type=F32)
    kr_ref[0] = (pf[0:n] + pg[0:n]) * scale
    ki_ref[0] = (pf[n:] - pg[n:]) * scale


def _filter_spectrum_dense(filt, tabs, d):
    nq, n, _ = filt.shape
    orders = nq // 2
    f4 = filt.reshape(orders, 2, n, d)
    out = pl.BlockSpec((1, 2 * n, d), lambda o: (o, 0, 0))
    return pl.pallas_call(
        functools.partial(_dense_spec_kernel, scale=1.0 / (2 * n)),
        grid=(orders,),
        in_specs=[pl.BlockSpec((4 * n, n), lambda o: (0, 0)), pl.BlockSpec((1, 2, n, d), lambda o: (o, 0, 0, 0))],
        out_specs=[out, out],
        out_shape=[jax.ShapeDtypeStruct((orders, 2 * n, d), F32)] * 2,
        compiler_params=_cparams("parallel"),
        name="filter_spectrum_dense",
    )(tabs['f'], f4)


def _dense_conv_kernel(f_ref, h_ref, u_ref, x_ref, kr_ref, ki_ref, sk_ref, o_ref):
    cr, ci = _cplx_mm(f_ref[...], u_ref[0, 0], u_ref[1, 0], False)
    kr = kr_ref[...]
    ki = ki_ref[...]
    yr, yi = _cplx_mm(h_ref[...], cr * kr - ci * ki, cr * ki + ci * kr, False)
    sk = sk_ref[...]
    o_ref[0, 0] = x_ref[0, 0] * (yr + u_ref[0, 0] * sk)
    o_ref[1, 0] = x_ref[1, 0] * (yi + u_ref[1, 0] * sk)


def _long_conv_dense(u, x, skip, kr, ki, tabs):
    b, n, d = u.shape
    sp = b // 2
    pair = pl.BlockSpec((2, 1, n, d), lambda si: (0, si, 0, 0))
    kb = pl.BlockSpec((2 * n, d), lambda si: (0, 0))
    out = pl.pallas_call(
        _dense_conv_kernel,
        grid=(sp,),
        in_specs=[pl.BlockSpec((4 * n, n), lambda si: (0, 0)), pl.BlockSpec((2 * n, 2 * n), lambda si: (0, 0)),
                  pair, pair, kb, kb, pl.BlockSpec((1, d), lambda si: (0, 0))],
        out_specs=pair,
        out_shape=jax.ShapeDtypeStruct((2, sp, n, d), F32),
        compiler_params=_cparams("parallel"),
        name="long_conv_dense",
    )(tabs['f'], tabs['h'], u.reshape(2, sp, n, d), x.reshape(2, sp, n, d), kr, ki, skip.reshape(1, d))
    return out.reshape(b, n, d)


def _hyena_mixer(x, g, mod, op, fft_tabs, dense_tabs):
    b, n, d = x.shape
    z = _proj_conv(x, g, mod, op['w_in'], op['b_in'], op['conv_w'], op['conv_b'])
    filt = _hyena_filters(n, op, d)
    if n <= DENSE_FFT_MAX:
        kr, ki = _filter_spectrum_dense(filt, dense_tabs, d)
        y = _long_conv_dense(z[0], z[1], op['skip'][0], kr[0], ki[0], dense_tabs)
        y = _long_conv_dense(y, z[2], op['skip'][1], kr[1], ki[1], dense_tabs)
    else:
        kr, ki = _filter_spectrum_fft(filt, fft_tabs, d)
        y = _long_conv_fft(z, 0, z, 1, op['skip'][0], kr, ki, 0, fft_tabs)
        y = _long_conv_fft(y[None], 0, z, 2, op['skip'][1], kr, ki, 1, fft_tabs)
    return _out_res(y, x, mod, op['w_out'], op['b_out'])


def _even_mixer(x, ctx, g, mod_l, mod_c, ep, tabs, need_ctx):
    rc, vc, kkc, gc, lwc, ac, kdc, qc, kac, vac = _even_prep(ctx, g, mod_c, ep, tabs, rope=False)
    rl, vl, kkl, gl, lwl, al, kdl, ql, kal, val = _even_prep(x, g, mod_l, ep, tabs, rope=True)
    b = x.shape[0]
    c = rl.shape[-1]
    h0 = jnp.zeros((2, b, HEAD_DIM, c), F32)
    y_ctx, s_ctx = _rwkv(rc, vc, kkc, lwc, ac, kdc, h0)
    y_lat, _ = _rwkv(rl, vl, kkl, lwl, al, kdl, s_ctx)
    b_lat = _attention(ql, kal, val, kac, vac, ep['sink'], local=True)
    x_new = _even_out(y_lat, rl, vl, gl, kdl, b_lat, x, mod_l, ep, tabs)
    if not need_ctx:
        return x_new, None
    b_ctx = _attention(qc, kac, vac, kac, vac, ep['sink'], local=False)
    ctx_new = _even_out(y_ctx, rc, vc, gc, kdc, b_ctx, ctx, mod_c, ep, tabs)
    return x_new, ctx_new


def _rope_tables(n_tokens):
    rows = n_tokens // GRID_W
    row = jnp.repeat(jnp.arange(rows), GRID_W).astype(F32)
    col = jnp.tile(jnp.arange(GRID_W), rows).astype(F32)
    n_freq = HEAD_DIM // 4
    inv = ROPE_THETA ** (-jnp.arange(n_freq, dtype=F32) / n_freq)
    ang = jnp.concatenate([row[:, None] * inv, col[:, None] * inv], axis=-1)
    cos, sin = jnp.cos(ang), jnp.sin(ang)
    reps = LANES // HEAD_DIM
    cos_t = jnp.tile(jnp.concatenate([cos, cos], axis=-1), (1, reps))
    sin_t = jnp.tile(jnp.concatenate([-sin, sin], axis=-1), (1, reps))
    return cos_t, sin_t


def _block_ones(width):
    idx = np.arange(width) // HEAD_DIM
    return jnp.asarray((idx[:, None] == idx[None, :]).astype(np.float32)).astype(BF16)


def _lora_pad(w):
    z = jnp.zeros_like(w[0])
    return jnp.stack([jnp.concatenate([w[0], z], axis=0), jnp.concatenate([z, w[1]], axis=0)], axis=0)


def kernel(x, c, ctx, c_ctx, ada_w, ada_b, norm1_g, norm2_g, ffn_up, ffn_conv_w, ffn_conv_b, ffn_down, ev_w_in, ev_mu_prev, ev_mu_next, ev_w0, ev_w2, ev_a0, ev_a2, ev_g2, ev_k_k, ev_k_a, ev_r_k, ev_ln_w, ev_ln_b, ev_q_norm, ev_k_norm, ev_sink, ev_w_out, od_w_in, od_b_in, od_conv_w, od_conv_b, od_f_w1, od_f_b1, od_f_w2, od_f_b2, od_f_w3, od_f_b3, od_f_freq, od_f_out, od_skip, od_w_out, od_b_out):
    bsz, seq, d = x.shape
    lc = ctx.shape[1]
    depth = ada_w.shape[0]
    a_width = ev_k_k.shape[-1]
    nq = ev_sink.shape[-1] * HEAD_DIM
    nk = B_KV_HEADS * HEAD_DIM

    cond = jnp.zeros((BF16_ROWS, d), F32).at[:bsz].set(c).at[bsz].set(c_ctx)
    mod = _ada_mod(cond, ada_w, ada_b)

    cos_t, sin_t = _rope_tables(seq)
    tabs = {'cos': cos_t, 'sin': sin_t, 'e': _block_ones(nq)}
    fft_tabs = _fft_tables(seq) if seq > DENSE_FFT_MAX else None
    dense_lat = _dense_tables(seq) if seq <= DENSE_FFT_MAX else None
    fft_ctx = _fft_tables(lc) if lc > DENSE_FFT_MAX else None
    dense_ctx = _dense_tables(lc) if lc <= DENSE_FFT_MAX else None

    for layer in range(depth):
        need_ctx = layer < depth - 1
        even = layer % 2 == 0
        j = layer // 2
        mod_l = mod[layer, :bsz].reshape(bsz, 1, 6 * d)
        mod_c = jnp.broadcast_to(mod[layer, bsz].reshape(1, 1, 6 * d), (bsz, 1, 6 * d))
        if even:
            ep = {'w_in': ev_w_in[j].astype(BF16), 'mu_prev': ev_mu_prev[j][None], 'mu_next': ev_mu_next[j][None],
                  'w0': ev_w0[j], 'w2pad': _lora_pad(ev_w2[j]).astype(BF16), 'a0': ev_a0[j],
                  'a2pad': _lora_pad(ev_a2[j]).astype(BF16), 'g2': ev_g2[j].astype(BF16),
                  'k_k': ev_k_k[j][None], 'k_a': ev_k_a[j][None], 'r_k': ev_r_k[j].reshape(1, a_width),
                  'ln_w': ev_ln_w[j][None], 'ln_b': ev_ln_b[j][None],
                  'q_norm_t': jnp.tile(ev_q_norm[j], nq // HEAD_DIM)[None],
                  'k_norm_t': jnp.tile(ev_k_norm[j], nk // HEAD_DIM)[None],
                  'sink': ev_sink[j], 'w_out': ev_w_out[j].astype(BF16), 'nq': nq, 'nk': nk}
            x, ctx_new = _even_mixer(x, ctx, norm1_g[layer], mod_l, mod_c, ep, tabs, need_ctx)
        else:
            emb_pad = HEAD_DIM
            op = {'w_in': od_w_in[j].astype(BF16), 'b_in': od_b_in[j], 'conv_w': od_conv_w[j],
                  'conv_b': od_conv_b[j],
                  'f_w1': jnp.pad(od_f_w1[j], ((0, emb_pad - HY_EMB), (0, 0))), 'f_b1': od_f_b1[j][None],
                  'f_w2': od_f_w2[j], 'f_b2': od_f_b2[j][None], 'f_w3': od_f_w3[j], 'f_b3': od_f_b3[j][None],
                  'f_freq': od_f_freq[j][None], 'f_out': od_f_out[j], 'skip': od_skip[j],
                  'w_out': od_w_out[j].astype(BF16), 'b_out': od_b_out[j]}
            ctx_new = _hyena_mixer(ctx, norm1_g[layer], mod_c, op, fft_ctx, dense_ctx) if need_ctx else None
            x = _hyena_mixer(x, norm1_g[layer], mod_l, op, fft_tabs, dense_lat)
        w_up = ffn_up[layer].astype(BF16)
        w_down = ffn_down[layer].astype(BF16)
        x = _conv_ffn(x, norm2_g[layer], mod_l, w_up, ffn_conv_w[layer], ffn_conv_b[layer], w_down)
        if need_ctx:
            ctx = _conv_ffn(ctx_new, norm2_g[layer], mod_c, w_up, ffn_conv_w[layer], ffn_conv_b[layer], w_down)
    return x
```

```python
import functools
import math

import numpy as np
import jax
import jax.numpy as jnp
from jax import lax
from jax.experimental import pallas as pl
from jax.experimental.pallas import tpu as pltpu

F32 = jnp.float32
BF16 = jnp.bfloat16

HEAD_DIM = 64
GRID_W = 64
DECAY_LORA = 64
ICLR_LORA = 64
GATE_LORA = 128
RWKV_GN_EPS = 64e-5
B_KV_HEADS = 2
WINDOW = 128
BLOCK = 128
ROPE_THETA = 10000.0
MASK_VALUE = -1e30
HY_ORDER = 2
HY_EMB = 33
HY_BANDS = (HY_EMB - 1) // 2
HY_TARGET = 1e-2
HY_FAST_PCT = 0.3
HY_SLOW_PCT = 1.5
HY_MOD_SHIFT = 0.05
NORM_EPS = 1e-6

V7X_VMEM_BYTES = 64 * 1024 * 1024
VMEM_LIMIT_BYTES = V7X_VMEM_BYTES * 3 // 4
LANES = 128
SUBLANES = 8
BF16_ROWS = 16
FFN_ROW_PARTS = 2
RWKV_CHUNK = 64
RWKV_PREP_CHUNKS = 4
RWKV_INV_BASE = 8
RWKV_GRAM_PASSES = 1
RWKV_INV_PASSES = 1
RWKV_REST_PASSES = 1
RWKV_SCAN_PASSES = 1
FFT_N2 = 128
DENSE_FFT_MAX = 512

HIGHEST = lax.Precision.HIGHEST


def _cparams(*sem):
    return pltpu.CompilerParams(dimension_semantics=sem, vmem_limit_bytes=VMEM_LIMIT_BYTES)


def _dot(a, b):
    return jnp.dot(a.astype(BF16), b.astype(BF16), preferred_element_type=F32)


def _dot_nt(a, b):
    return lax.dot_general(a.astype(BF16), b.astype(BF16), (((1,), (1,)), ((), ())), preferred_element_type=F32)


def _dot_tn(a, b):
    return lax.dot_general(a.astype(BF16), b.astype(BF16), (((0,), (0,)), ((), ())), preferred_element_type=F32)


def _dot_hi(a, b):
    return jnp.dot(a, b, preferred_element_type=F32, precision=HIGHEST)


def _dot_nt_hi(a, b):
    return lax.dot_general(a, b, (((1,), (1,)), ((), ())), preferred_element_type=F32, precision=HIGHEST)


def _dot_tn_hi(a, b):
    return lax.dot_general(a, b, (((0,), (0,)), ((), ())), preferred_element_type=F32, precision=HIGHEST)


def _segsum(x, e_ref):
    hi = x.astype(BF16)
    lo = (x - hi.astype(F32)).astype(BF16)
    e = e_ref[...]
    return jnp.dot(hi, e, preferred_element_type=F32) + jnp.dot(lo, e, preferred_element_type=F32)


def _norm_mod(x, g, shift, scale):
    xn = x * lax.rsqrt(jnp.mean(x * x, axis=-1, keepdims=True) + NORM_EPS)
    return (xn * g) * (1.0 + scale) + shift


def _row_tile(t, want):
    tm = min(t, want)
    assert t % tm == 0 and tm % BF16_ROWS == 0
    return tm


def _ada_kernel(c_ref, w_ref, b_ref, o_ref):
    c = c_ref[...]
    s = c * jax.nn.sigmoid(c)
    o_ref[0] = _dot(s, w_ref[0]) + b_ref[0]


def _ada_mod(cond, ada_w, ada_b):
    depth, d, n = ada_w.shape
    tn = n // 4
    rows = cond.shape[0]
    return pl.pallas_call(
        _ada_kernel,
        grid=(depth, n // tn),
        in_specs=[pl.BlockSpec((rows, d), lambda l, j: (0, 0)),
                  pl.BlockSpec((1, d, tn), lambda l, j: (l, 0, j)),
                  pl.BlockSpec((1, 1, tn), lambda l, j: (l, 0, j))],
        out_specs=pl.BlockSpec((1, rows, tn), lambda l, j: (l, 0, j)),
        out_shape=jax.ShapeDtypeStruct((depth, rows, n), F32),
        compiler_params=_cparams("parallel", "parallel"),
        name="ada_mod",
    )(cond, ada_w, ada_b.reshape(depth, 1, n))


def _halo_specs(tm, t, d, nargs):
    r = tm // BF16_ROWS
    last = t // BF16_ROWS - 1
    if nargs == 3:
        prev = pl.BlockSpec((1, BF16_ROWS, d), lambda bi, i, j: (bi, jnp.maximum(i * r - 1, 0), 0))
        nxt = pl.BlockSpec((1, BF16_ROWS, d), lambda bi, i, j: (bi, jnp.minimum((i + 1) * r, last), 0))
    else:
        prev = pl.BlockSpec((1, BF16_ROWS, d), lambda bi, i: (bi, jnp.maximum(i * r - 1, 0), 0))
        nxt = pl.BlockSpec((1, BF16_ROWS, d), lambda bi, i: (bi, jnp.minimum((i + 1) * r, last), 0))
    return prev, nxt


def _fill_h(x_ref, xp_ref, xn_ref, g, shift, scale, h_scr, tm):
    h_scr[0:BF16_ROWS] = _norm_mod(xp_ref[0], g, shift, scale).astype(BF16)
    h_scr[BF16_ROWS:BF16_ROWS + tm] = _norm_mod(x_ref[0], g, shift, scale).astype(BF16)
    h_scr[BF16_ROWS + tm:2 * BF16_ROWS + tm] = _norm_mod(xn_ref[0], g, shift, scale).astype(BF16)


def _conv3_rows(u, u_scr, cw, cb, tm, t_total, part=0, nparts=1):
    i = pl.program_id(1)
    o = BF16_ROWS
    rows = tm // nparts
    base = part * (rows + 2 * o)
    head_ok = (i > 0) if part == 0 else True
    tail_ok = (i < t_total // tm - 1) if part == nparts - 1 else True
    u_scr[base:base + o] = jnp.where(head_ok, u[0:o], 0.0)
    u_scr[base + o:base + o + rows] = u[o:o + rows]
    u_scr[base + o + rows:base + 2 * o + rows] = jnp.where(tail_ok, u[o + rows:], 0.0)
    s = base + o
    return (u_scr[s - 1:s - 1 + rows] * cw[0:1] + u_scr[s:s + rows] * cw[1:2] + u_scr[s + 1:s + 1 + rows] * cw[2:3]
            + cb)


def _proj_conv_kernel(x_ref, xp_ref, xn_ref, g_ref, mod_ref, w_ref, b_ref, cw_ref, cb_ref, o_ref, h_scr, u_scr,
                      *, d, tm, t_total):
    @pl.when(pl.program_id(2) == 0)
    def _():
        m = mod_ref[0]
        _fill_h(x_ref, xp_ref, xn_ref, g_ref[...], m[:, 0:d], m[:, d:2 * d], h_scr, tm)

    nparts = FFN_ROW_PARTS
    rows = tm // nparts
    us = [jnp.dot(h_scr[p * rows:(p + 1) * rows + 2 * BF16_ROWS], w_ref[...], preferred_element_type=F32)
          + b_ref[...] for p in range(nparts)]
    for p in range(nparts):
        o_ref[0, 0, p * rows:(p + 1) * rows] = _conv3_rows(us[p], u_scr, cw_ref[...], cb_ref[...], tm, t_total, p,
                                                           nparts)


def _proj_conv(x, g, mod, w, bias, cw, cb):
    b, t, d = x.shape
    n = w.shape[1]
    tm = _row_tile(t, 512)
    tn = d
    prev, nxt = _halo_specs(tm, t, d, 3)
    return pl.pallas_call(
        functools.partial(_proj_conv_kernel, d=d, tm=tm, t_total=t),
        grid=(b, t // tm, n // tn),
        in_specs=[pl.BlockSpec((1, tm, d), lambda bi, i, j: (bi, i, 0)), prev, nxt,
                  pl.BlockSpec((1, d), lambda bi, i, j: (0, 0)),
                  pl.BlockSpec((1, 1, mod.shape[-1]), lambda bi, i, j: (bi, 0, 0)),
                  pl.BlockSpec((d, tn), lambda bi, i, j: (0, j)),
                  pl.BlockSpec((1, tn), lambda bi, i, j: (0, j)),
                  pl.BlockSpec((3, tn), lambda bi, i, j: (0, j)),
                  pl.BlockSpec((1, tn), lambda bi, i, j: (0, j))],
        out_specs=pl.BlockSpec((1, 1, tm, tn), lambda bi, i, j: (j, bi, i, 0)),
        out_shape=jax.ShapeDtypeStruct((n // tn, b, t, tn), F32),
        scratch_shapes=[pltpu.VMEM((tm + 2 * BF16_ROWS, d), BF16),
                        pltpu.VMEM((tm + 2 * BF16_ROWS * FFN_ROW_PARTS, tn), F32)],
        compiler_params=_cparams("parallel", "parallel", "arbitrary"),
        name="hyena_in_proj",
    )(x, x, x, g.reshape(1, d), mod, w, bias.reshape(1, n), cw, cb.reshape(1, n))


def _ffn_kernel(x_ref, xp_ref, xn_ref, g_ref, mod_ref, wup_ref, cw_ref, cb_ref, wd_ref,
                o_ref, h_scr, ug_scr, uv_scr, acc_scr, *, d, tm, t_total, f, tf):
    m = mod_ref[0]
    _fill_h(x_ref, xp_ref, xn_ref, g_ref[...], m[:, 3 * d:4 * d], m[:, 4 * d:5 * d], h_scr, tm)
    nparts = FFN_ROW_PARTS
    rows = tm // nparts
    nf = f // tf

    def up_proj(j):
        hs = [h_scr[p * rows:(p + 1) * rows + 2 * BF16_ROWS] for p in range(nparts)]
        return ([jnp.dot(h, wup_ref[:, j * tf:(j + 1) * tf], preferred_element_type=F32) for h in hs],
                [jnp.dot(h, wup_ref[:, f + j * tf:f + (j + 1) * tf], preferred_element_type=F32) for h in hs])

    nxt_up = up_proj(0)
    for j in range(nf):
        gs = slice(j * tf, (j + 1) * tf)
        vs = slice(f + j * tf, f + (j + 1) * tf)
        ug, uv = nxt_up
        if j + 1 < nf:
            nxt_up = up_proj(j + 1)
        for p in range(nparts):
            gate = _conv3_rows(ug[p], ug_scr.at[j % 2], cw_ref[:, gs], cb_ref[:, gs], tm, t_total, p, nparts)
            val = _conv3_rows(uv[p], uv_scr.at[j % 2], cw_ref[:, vs], cb_ref[:, vs], tm, t_total, p, nparts)
            act = (gate * jax.nn.sigmoid(gate)) * val
            down = jnp.dot(act.astype(BF16), wd_ref[gs, :], preferred_element_type=F32)
            part = slice(p * rows, (p + 1) * rows)
            if j == 0:
                acc_scr[part] = down
            else:
                acc_scr[part] += down
    o_ref[0] = x_ref[0] + m[:, 5 * d:6 * d] * acc_scr[...]


def _conv_ffn(x, g, mod, w_up, cw, cb, w_down):
    b, t, d = x.shape
    f = w_down.shape[0]
    tm = _row_tile(t, 1024)
    tf = 256 if f % 256 == 0 else 128
    prev, nxt = _halo_specs(tm, t, d, 2)
    resident = lambda *shape: pl.BlockSpec(shape, lambda bi, i: (0,) * len(shape), pipeline_mode=pl.Buffered(1))
    u_rows = tm + 2 * BF16_ROWS * FFN_ROW_PARTS
    return pl.pallas_call(
        functools.partial(_ffn_kernel, d=d, tm=tm, t_total=t, f=f, tf=tf),
        grid=(b, t // tm),
        in_specs=[pl.BlockSpec((1, tm, d), lambda bi, i: (bi, i, 0)), prev, nxt,
                  pl.BlockSpec((1, d), lambda bi, i: (0, 0)),
                  pl.BlockSpec((1, 1, mod.shape[-1]), lambda bi, i: (bi, 0, 0)),
                  resident(d, 2 * f), resident(3, 2 * f), resident(1, 2 * f), resident(f, d)],
        out_specs=pl.BlockSpec((1, tm, d), lambda bi, i: (bi, i, 0)),
        out_shape=jax.ShapeDtypeStruct((b, t, d), F32),
        scratch_shapes=[pltpu.VMEM((tm + 2 * BF16_ROWS, d), BF16),
                        pltpu.VMEM((2, u_rows, tf), F32),
                        pltpu.VMEM((2, u_rows, tf), F32),
                        pltpu.VMEM((tm, d), F32)],
        compiler_params=_cparams("parallel", "parallel"),
        name="conv_ffn",
    )(x, x, x, g.reshape(1, d), mod, w_up, cw, cb.reshape(1, 2 * f), w_down)


def _out_res_kernel(y_ref, x_ref, mod_ref, w_ref, b_ref, o_ref, *, d):
    o = _dot(y_ref[0], w_ref[...]) + b_ref[...]
    o_ref[0] = x_ref[0] + mod_ref[0][:, 2 * d:3 * d] * o


def _out_res(y, x, mod, w, bias):
    b, t, d = x.shape
    tm = _row_tile(t, 512)
    return pl.pallas_call(
        functools.partial(_out_res_kernel, d=d),
        grid=(b, t // tm),
        in_specs=[pl.BlockSpec((1, tm, d), lambda bi, i: (bi, i, 0)),
                  pl.BlockSpec((1, tm, d), lambda bi, i: (bi, i, 0)),
                  pl.BlockSpec((1, 1, mod.shape[-1]), lambda bi, i: (bi, 0, 0)),
                  pl.BlockSpec((d, d), lambda bi, i: (0, 0)),
                  pl.BlockSpec((1, d), lambda bi, i: (0, 0))],
        out_specs=pl.BlockSpec((1, tm, d), lambda bi, i: (bi, i, 0)),
        out_shape=jax.ShapeDtypeStruct((b, t, d), F32),
        compiler_params=_cparams("parallel", "parallel"),
        name="out_proj_residual",
    )(y, x, mod, w, bias.reshape(1, d))


def _even_prep_kernel(x_ref, xp_ref, xn_ref, gn_ref, mod_ref, win_ref, mup_ref, mun_ref, w0_ref, w2_ref, a0_ref,
                      a2_ref, g2_ref, kk_ref, ka_ref, qn_ref, kn_ref, cos_ref, sin_ref, e_ref,
                      r_ref, v_ref, kkn_ref, g_ref, lw_ref, a_ref, kd_ref, q_ref, katt_ref, vatt_ref,
                      h_scr, p_scr, *, tm, d, c, a_in, rope):
    i = pl.program_id(1)
    nt = pl.num_programs(1)
    m = mod_ref[0]
    _fill_h(x_ref, xp_ref, xn_ref, gn_ref[...], m[:, 0:d], m[:, d:2 * d], h_scr, tm)
    p_scr[...] = jnp.dot(h_scr[...], win_ref[...], preferred_element_type=F32)
    o = BF16_ROWS
    pa = p_scr[o:o + tm, 0:a_in]
    rows = lax.broadcasted_iota(jnp.int32, (tm, 1), 0)
    prev = jnp.where((rows == 0) & (i == 0), 0.0, p_scr[o - 1:o - 1 + tm, 0:a_in])
    nxt = jnp.where((rows == tm - 1) & (i == nt - 1), 0.0, p_scr[o + 1:o + 1 + tm, 0:a_in])
    za = pa + mup_ref[...] * (prev - pa) + mun_ref[...] * (nxt - pa)

    r = za[:, 0:c]
    k = za[:, c:2 * c]
    v = za[:, 2 * c:3 * c]
    wd = jnp.tanh(za[:, 3 * c:3 * c + 2 * DECAY_LORA])
    ad = za[:, 3 * c + 2 * DECAY_LORA:3 * c + 2 * DECAY_LORA + 2 * ICLR_LORA]
    gd = za[:, 3 * c + 2 * DECAY_LORA + 2 * ICLR_LORA:a_in]
    r_ref[0] = r
    v_ref[0] = v
    g_ref[0] = _dot(jax.nn.sigmoid(gd), g2_ref[...])
    kkv = k * kk_ref[...]
    ss = _segsum(kkv * kkv, e_ref)
    kkn_ref[0] = kkv / jnp.maximum(jnp.sqrt(ss), 1e-12)
    for dd in range(2):
        y = w0_ref[dd:dd + 1] + _dot(wd, w2_ref[dd])
        lw_ref[dd, 0] = -math.exp(-0.5) * jax.nn.sigmoid(y)
        a = jax.nn.sigmoid(a0_ref[dd:dd + 1] + _dot(ad, a2_ref[dd]))
        a_ref[dd, 0] = a
        kd_ref[dd, 0] = k * (1.0 + (a - 1.0) * ka_ref[...])

    pb = p_scr[o:o + tm, a_in:]
    nq = q_ref.shape[-1]
    nk = katt_ref.shape[-1]
    q = pb[:, 0:nq]
    kat = pb[:, nq:nq + nk]
    vatt_ref[0] = pb[:, nq + nk:]
    half = HEAD_DIM // 2

    def norm_rope(x, gain, e):
        w = x.shape[-1]
        ms = _segsum(x * x, e) * (1.0 / HEAD_DIM)
        xn = x * lax.rsqrt(ms + NORM_EPS) * gain
        if not rope:
            return xn
        lane = lax.broadcasted_iota(jnp.int32, (1, w), 1)
        first = (lane % HEAD_DIM) < half
        swapped = jnp.where(first, pltpu.roll(xn, w - half, 1), pltpu.roll(xn, half, 1))
        reps = w // LANES
        cos = jnp.concatenate([cos_ref[...]] * reps, axis=1) if reps > 1 else cos_ref[...]
        sin = jnp.concatenate([sin_ref[...]] * reps, axis=1) if reps > 1 else sin_ref[...]
        return xn * cos + swapped * sin

    q_ref[0] = norm_rope(q, qn_ref[...], e_ref[...])
    katt_ref[0] = norm_rope(kat, kn_ref[...], e_ref[0:nk, 0:nk])


def _even_prep(x, g, mod, ep, tabs, rope):
    b, t, d = x.shape
    n = ep['w_in'].shape[1]
    c = ep['k_k'].shape[-1]
    a_in = ep['mu_prev'].shape[-1]
    nq = ep['nq']
    nk = ep['nk']
    tm = _row_tile(t, 256)
    prev, nxt = _halo_specs(tm, t, d, 2)
    full = lambda *shape: pl.BlockSpec(shape, lambda bi, i: (0,) * len(shape))
    tok = lambda w: pl.BlockSpec((1, tm, w), lambda bi, i: (bi, i, 0))
    tok2 = lambda w: pl.BlockSpec((2, 1, tm, w), lambda bi, i: (0, bi, i, 0))
    sd = lambda *shape: jax.ShapeDtypeStruct(shape, F32)
    return pl.pallas_call(
        functools.partial(_even_prep_kernel, tm=tm, d=d, c=c, a_in=a_in, rope=rope),
        grid=(b, t // tm),
        in_specs=[tok(d), prev, nxt, full(1, d), pl.BlockSpec((1, 1, mod.shape[-1]), lambda bi, i: (bi, 0, 0)),
                  full(d, n), full(1, a_in), full(1, a_in), full(2, c), full(2, 2 * DECAY_LORA, c),
                  full(2, c), full(2, 2 * ICLR_LORA, c), full(GATE_LORA, c), full(1, c), full(1, c),
                  full(1, nq), full(1, nk),
                  pl.BlockSpec((tm, LANES), lambda bi, i: (i, 0)), pl.BlockSpec((tm, LANES), lambda bi, i: (i, 0)),
                  full(nq, nq)],
        out_specs=[tok(c), tok(c), tok(c), tok(c), tok2(c), tok2(c), tok2(c), tok(nq), tok(nk), tok(nk)],
        out_shape=[sd(b, t, c), sd(b, t, c), sd(b, t, c), sd(b, t, c), sd(2, b, t, c), sd(2, b, t, c),
                   sd(2, b, t, c), sd(b, t, nq), sd(b, t, nk), sd(b, t, nk)],
        scratch_shapes=[pltpu.VMEM((tm + 2 * BF16_ROWS, d), BF16), pltpu.VMEM((tm + 2 * BF16_ROWS, n), F32)],
        compiler_params=_cparams("parallel", "parallel"),
        name="even_in_proj_prep",
    )(x, x, x, g.reshape(1, d), mod, ep['w_in'], ep['mu_prev'], ep['mu_next'], ep['w0'], ep['w2pad'], ep['a0'], ep['a2pad'], ep['g2'], ep['k_k'],
      ep['k_a'], ep['q_norm_t'], ep['k_norm_t'], tabs['cos'][:t], tabs['sin'][:t], tabs['e'])


HEADS_PER_GROUP = 4
GROUP_LANES = HEADS_PER_GROUP * HEAD_DIM


def _split_bf16(x):
    hi = x.astype(BF16)
    return hi, (x - hi.astype(F32)).astype(BF16)


def _block_diag(x, bmask):
    return jnp.where(bmask, jnp.concatenate([x] * HEADS_PER_GROUP, axis=0), jnp.zeros((), x.dtype))


def _head_mm(lhs, rhs, bmask, passes, nt=False):
    dn = (((1,), (1,)), ((), ())) if nt else (((1,), (0,)), ((), ()))
    if passes == 1:
        return lax.dot_general(lhs.astype(BF16), _block_diag(rhs.astype(BF16), bmask), dn, preferred_element_type=F32)
    lh, ll = _split_bf16(lhs)
    rh, rl = _split_bf16(rhs)
    m = lhs.shape[0]
    top = lax.dot_general(jnp.concatenate([lh, ll], axis=0), _block_diag(rh, bmask), dn, preferred_element_type=F32)
    return top[0:m] + top[m:] + lax.dot_general(lh, _block_diag(rl, bmask), dn, preferred_element_type=F32)


def _head_mm_tn(lhs, rhs, lane_head, passes):
    dn = (((0,), (0,)), ((), ()))
    if passes == 1:
        full = lax.dot_general(lhs.astype(BF16), rhs.astype(BF16), dn, preferred_element_type=F32)
    else:
        lh, ll = _split_bf16(lhs)
        rh, rl = _split_bf16(rhs)
        full = (lax.dot_general(jnp.concatenate([lh, ll], axis=0), jnp.concatenate([rh, rh], axis=0), dn,
                                preferred_element_type=F32)
                + lax.dot_general(lh, rl, dn, preferred_element_type=F32))
    out = jnp.where(lane_head == 0, full[0:HEAD_DIM], 0.0)
    for h in range(1, HEADS_PER_GROUP):
        out = out + jnp.where(lane_head == h, full[h * HEAD_DIM:(h + 1) * HEAD_DIM], 0.0)
    return out


def _group_masks():
    r = lax.broadcasted_iota(jnp.int32, (GROUP_LANES, GROUP_LANES), 0)
    c = lax.broadcasted_iota(jnp.int32, (GROUP_LANES, GROUP_LANES), 1)
    bmask = (r // HEAD_DIM) == (c // HEAD_DIM)
    lane_head = lax.broadcasted_iota(jnp.int32, (1, GROUP_LANES), 1) // HEAD_DIM
    return bmask, lane_head


def _rwkv_prep_kernel(r_ref, v_ref, kk_ref, lw_ref, a_ref, kd_ref, r2_ref, yl_ref, t_ref, z_ref, *, cs):
    sgn = 1 - 2 * pl.program_id(0)
    row = lax.broadcasted_iota(jnp.int32, (cs, cs), 0)
    col = lax.broadcasted_iota(jnp.int32, (cs, cs), 1)
    tri = (((row - col) * sgn) >= 0).astype(F32)
    bmask, lane_head = _group_masks()
    t_idx = lax.broadcasted_iota(jnp.int32, (cs, GROUP_LANES), 0)
    s_idx = lax.broadcasted_iota(jnp.int32, (cs, GROUP_LANES), 1) % HEAD_DIM
    diff = (t_idx - s_idx) * sgn
    strict = diff > 0
    incl = diff >= 0
    eye = (diff == 0).astype(F32)
    base = min(RWKV_INV_BASE, cs)
    same_base = (t_idx // base) == (s_idx // base)
    groups = r_ref.shape[-1] // GROUP_LANES

    ch = []
    for c0 in range(0, r_ref.shape[1], cs):
        rows = slice(c0, c0 + cs)
        r, v, kk = r_ref[0, rows], v_ref[0, rows], kk_ref[0, rows]
        lw, a, kd = lw_ref[0, 0, rows], a_ref[0, 0, rows], kd_ref[0, 0, rows]
        g = _dot_hi(tri, lw)
        gp = g - lw
        gref = g[cs // 2:cs // 2 + 1]
        gend = jnp.sum(lw, axis=0, keepdims=True)
        bvec = kk * a
        full = {'a_t': -kk * jnp.exp(gp - gref), 'a_0': -kk * jnp.exp(gp),
                'b_t': bvec * jnp.exp(gref - g), 'b_e': bvec * jnp.exp(gend - g),
                'k_t': kd * jnp.exp(gref - g), 'k_e': kd * jnp.exp(gend - g),
                'r_t': r * jnp.exp(g - gref), 'r_0': r * jnp.exp(g), 'v': v,
                'wend': jnp.broadcast_to(jnp.exp(gend), (cs, v.shape[-1]))}
        for gi in range(groups):
            sl = slice(gi * GROUP_LANES, (gi + 1) * GROUP_LANES)
            c = {k: x[:, sl] for k, x in full.items()}
            c['rows'], c['sl'] = rows, sl
            ch.append(c)

    mm = lambda x, y, passes, nt=False: [_head_mm(p_, q_, bmask, passes, nt) for p_, q_ in zip(x, y)]
    get = lambda k: [c[k] for c in ch]
    ar = [jnp.concatenate([c['a_t'], c['r_t']], axis=0) for c in ch]
    gb = mm(ar, get('b_t'), RWKV_GRAM_PASSES, True)
    gk = mm(ar, get('k_t'), RWKV_GRAM_PASSES, True)
    nmat = [jnp.where(strict, x[0:cs], 0.0) for x in gb]
    g_ak = [jnp.where(strict, x[0:cs], 0.0) for x in gk]
    g_rb = [jnp.where(incl, x[cs:], 0.0) for x in gb]
    g_rk = [jnp.where(incl, x[cs:], 0.0) for x in gk]
    npow = [jnp.where(same_base, x, 0.0) for x in nmat]
    p = [eye + x for x in npow]
    for _ in range(int(math.log2(base)) - 1):
        npow = mm(npow, npow, RWKV_INV_PASSES)
        p = [x + y for x, y in zip(p, mm(npow, p, RWKV_INV_PASSES))]
    m = base
    while m < cs:
        off = ((t_idx // m) != (s_idx // m)) & ((t_idx // (2 * m)) == (s_idx // (2 * m)))
        q = mm([jnp.where(off, x, 0.0) for x in nmat], p, RWKV_INV_PASSES)
        p = [x + y for x, y in zip(p, mm(p, q, RWKV_INV_PASSES))]
        m *= 2
    vg = get('v')
    a2 = mm(p, get('a_0'), RWKV_REST_PASSES)
    u_v = mm(p, mm(g_ak, vg, RWKV_REST_PASSES), RWKV_REST_PASSES)
    r2 = mm(g_rb, a2, RWKV_REST_PASSES)
    yl1 = mm(g_rb, u_v, RWKV_REST_PASSES)
    yl2 = mm(g_rk, vg, RWKV_REST_PASSES)
    tt = [_head_mm_tn(x, c['b_e'], lane_head, RWKV_REST_PASSES) for x, c in zip(a2, ch)]
    zz = [_head_mm_tn(jnp.concatenate([u, c['v']], axis=0), jnp.concatenate([c['b_e'], c['k_e']], axis=0),
                      lane_head, RWKV_REST_PASSES) for u, c in zip(u_v, ch)]
    for i, c in enumerate(ch):
        rows, sl = c['rows'], c['sl']
        r2_ref[0, 0, rows, sl] = (c['r_0'] + r2[i]).astype(r2_ref.dtype)
        yl_ref[0, 0, rows, sl] = yl1[i] + yl2[i]
        t_ref[0, 0, rows, sl] = (eye * c['wend'] + tt[i]).astype(t_ref.dtype)
        z_ref[0, 0, rows, sl] = zz[i]


def _rwkv_scan_kernel(h0_ref, r2f_ref, ylf_ref, tf_ref, zf_ref, r2b_ref, ylb_ref, tb_ref, zb_ref,
                      yf_ref, yb_ref, ht_ref, s_scr):
    ci = pl.program_id(0)

    @pl.when(ci == 0)
    def _():
        s_scr[...] = h0_ref[...]

    bmask, _ = _group_masks()
    ins = ((r2f_ref, ylf_ref, tf_ref, zf_ref, yf_ref), (r2b_ref, ylb_ref, tb_ref, zb_ref, yb_ref))
    for d, (r2_ref, yl_ref, t_ref, z_ref, y_ref) in enumerate(ins):
        for b in range(s_scr.shape[1]):
            for gi in range(s_scr.shape[-1] // GROUP_LANES):
                sl = slice(gi * GROUP_LANES, (gi + 1) * GROUP_LANES)
                s = s_scr[d, b, :, sl]
                y_ref[0, b, :, sl] = yl_ref[0, b, :, sl] + _head_mm(r2_ref[0, b, :, sl], s, bmask, 1, nt=True)
                s_scr[d, b, :, sl] = (_head_mm(s, t_ref[0, b, :, sl], bmask, RWKV_SCAN_PASSES)
                                      + z_ref[0, b, :, sl])

    @pl.when(ci == pl.num_programs(0) - 1)
    def _():
        ht_ref[...] = s_scr[...]


def _rwkv(r, v, kk, lw, a, kd, h0):
    b, t, c = r.shape
    cs = min(RWKV_CHUNK, t)
    assert cs == HEAD_DIM and c % GROUP_LANES == 0
    nc = t // cs
    rows = cs * math.gcd(nc, RWKV_PREP_CHUNKS)
    tok = pl.BlockSpec((1, rows, c), lambda d, bi, ci: (bi, ci, 0))
    tok2 = pl.BlockSpec((1, 1, rows, c), lambda d, bi, ci: (d, bi, ci, 0))
    wide = jax.ShapeDtypeStruct((2, b, t, c), F32)
    mm_operand = jax.ShapeDtypeStruct((2, b, t, c), BF16 if RWKV_SCAN_PASSES == 1 else F32)
    r2, yl, tt, zz = pl.pallas_call(
        functools.partial(_rwkv_prep_kernel, cs=cs),
        grid=(2, b, t // rows),
        in_specs=[tok, tok, tok, tok2, tok2, tok2],
        out_specs=[tok2, tok2, tok2, tok2],
        out_shape=[mm_operand, wide, mm_operand, wide],
        compiler_params=_cparams("parallel", "parallel", "parallel"),
        name="rwkv7_chunk_prep",
    )(r, v, kk, lw, a, kd)

    fwd = pl.BlockSpec((1, b, cs, c), lambda ci: (0, 0, ci, 0))
    bwd = pl.BlockSpec((1, b, cs, c), lambda ci: (1, 0, nc - 1 - ci, 0))
    st = pl.BlockSpec((2, b, HEAD_DIM, c), lambda ci: (0, 0, 0, 0))
    yf, yb, ht = pl.pallas_call(
        _rwkv_scan_kernel,
        grid=(nc,),
        in_specs=[st, fwd, fwd, fwd, fwd, bwd, bwd, bwd, bwd],
        out_specs=[pl.BlockSpec((1, b, cs, c), lambda ci: (0, 0, ci, 0)),
                   pl.BlockSpec((1, b, cs, c), lambda ci: (0, 0, nc - 1 - ci, 0)), st],
        out_shape=[jax.ShapeDtypeStruct((1, b, t, c), F32), jax.ShapeDtypeStruct((1, b, t, c), F32),
                   jax.ShapeDtypeStruct((2, b, HEAD_DIM, c), F32)],
        scratch_shapes=[pltpu.VMEM((2, b, HEAD_DIM, c), F32)],
        compiler_params=_cparams("arbitrary"),
        name="rwkv7_state_scan",
    )(h0, r2, yl, tt, zz, r2, yl, tt, zz)
    return (yf.reshape(b, t, c), yb.reshape(b, t, c)), ht


def _attn_kernel(sink_ref, q_ref, kp_ref, kc_ref, kn_ref, vp_ref, vc_ref, vn_ref, kx_ref, vx_ref, o_ref,
                 *, local, group):
    i = pl.program_id(1)
    nb = pl.num_programs(1)
    scale = HEAD_DIM ** -0.5
    q = q_ref[0]
    if local:
        k_all = jnp.concatenate([kp_ref[0], kc_ref[0], kn_ref[0], kx_ref[0]], axis=0)
        v_all = jnp.concatenate([vp_ref[0], vc_ref[0], vn_ref[0], vx_ref[0]], axis=0)
        nkeys = k_all.shape[0]
        qi = lax.broadcasted_iota(jnp.int32, (BLOCK, nkeys), 0)
        kj = lax.broadcasted_iota(jnp.int32, (BLOCK, nkeys), 1)
        kpos = kj + (i - 1) * BLOCK
        valid = (jnp.abs(kj - BLOCK - qi) <= WINDOW) & (kpos >= 0) & (kpos < nb * BLOCK)
        valid = valid | (kj >= 3 * BLOCK)
    else:
        k_all = kx_ref[0]
        v_all = vx_ref[0]
    heads = range(q.shape[-1] // HEAD_DIM)
    kv = lambda x, h: x[:, (h // group) * HEAD_DIM:(h // group + 1) * HEAD_DIM]
    s = [_dot_nt(q[:, h * HEAD_DIM:(h + 1) * HEAD_DIM], kv(k_all, h)) * scale for h in heads]
    if local:
        s = [jnp.where(valid, x, MASK_VALUE) for x in s]
    sink = [sink_ref[h] for h in heads]
    m = [jnp.maximum(jnp.max(x, axis=-1, keepdims=True), sk) for x, sk in zip(s, sink)]
    e = [jnp.exp(x - y) for x, y in zip(s, m)]
    den = [jnp.sum(x, axis=-1, keepdims=True) + jnp.exp(sk - y) for x, y, sk in zip(e, m, sink)]
    outs = [_dot(x, kv(v_all, h)) / dn for x, h, dn in zip(e, heads, den)]
    o_ref[0] = jnp.concatenate(outs, axis=1)


def _attention(q, k, v, kx, vx, sink, local):
    b, t, nq = q.shape
    nk = k.shape[-1]
    nb = t // BLOCK
    group = (nq // HEAD_DIM) // (nk // HEAD_DIM)
    lx = kx.shape[1]
    kv = lambda f: pl.BlockSpec((1, BLOCK, nk), f)
    pf = lambda bi, i: (bi, jnp.maximum(i - 1, 0), 0)
    cf = lambda bi, i: (bi, i, 0)
    nf = lambda bi, i: (bi, jnp.minimum(i + 1, nb - 1), 0)
    ctx = pl.BlockSpec((1, lx, nk), lambda bi, i: (bi, 0, 0))
    return pl.pallas_call(
        functools.partial(_attn_kernel, local=local, group=group),
        grid=(b, nb),
        in_specs=[pl.BlockSpec(memory_space=pltpu.SMEM),
                  pl.BlockSpec((1, BLOCK, nq), cf), kv(pf), kv(cf), kv(nf), kv(pf), kv(cf), kv(nf), ctx, ctx],
        out_specs=pl.BlockSpec((1, BLOCK, nq), cf),
        out_shape=jax.ShapeDtypeStruct((b, t, nq), F32),
        compiler_params=_cparams("parallel", "parallel"),
        name="window_attention" if local else "context_attention",
    )(sink, q, k, k, k, v, v, v, kx, vx)


def _even_out_kernel(yf_ref, yb_ref, r_ref, v_ref, g_ref, kd_ref, batt_ref, x_ref, mod_ref, lnw_ref, lnb_ref, rk_ref, e_ref,
                     w_ref, o_ref, *, d, c):
    y = yf_ref[0] + yb_ref[0]
    inv = 1.0 / HEAD_DIM
    mu = _segsum(y, e_ref) * inv
    yc = y - mu
    var = _segsum(yc * yc, e_ref) * inv
    yn = yc * lax.rsqrt(var + RWKV_GN_EPS) * lnw_ref[...] + lnb_ref[...]
    bonus = _segsum(r_ref[0] * (kd_ref[0, 0] + kd_ref[1, 0]) * rk_ref[...], e_ref)
    a_out = (yn + bonus * v_ref[0]) * g_ref[0]
    o = _dot(a_out, w_ref[0:c]) + _dot(batt_ref[0], w_ref[c:])
    o_ref[0] = x_ref[0] + mod_ref[0][:, 2 * d:3 * d] * o


def _even_out(y, r, v, g, kd, batt, x, mod, ep, tabs):
    b, t, d = x.shape
    c = r.shape[-1]
    nq = batt.shape[-1]
    tm = _row_tile(t, 512)
    tok = lambda w: pl.BlockSpec((1, tm, w), lambda bi, i: (bi, i, 0))
    tok2 = lambda w: pl.BlockSpec((2, 1, tm, w), lambda bi, i: (0, bi, i, 0))
    full = lambda *shape: pl.BlockSpec(shape, lambda bi, i: (0,) * len(shape))
    return pl.pallas_call(
        functools.partial(_even_out_kernel, d=d, c=c),
        grid=(b, t // tm),
        in_specs=[tok(c), tok(c), tok(c), tok(c), tok(c), tok2(c), tok(nq), tok(d),
                  pl.BlockSpec((1, 1, mod.shape[-1]), lambda bi, i: (bi, 0, 0)),
                  full(1, c), full(1, c), full(1, c), full(c, c), full(c + nq, d)],
        out_specs=tok(d),
        out_shape=jax.ShapeDtypeStruct((b, t, d), F32),
        compiler_params=_cparams("parallel", "parallel"),
        name="even_out_proj",
    )(y[0], y[1], r, v, g, kd, batt, x, mod, ep['ln_w'], ep['ln_b'], ep['r_k'], tabs['e'], ep['w_out'])


def _filter_kernel(z_ref, t_ref, w1_ref, b1_ref, w2_ref, b2_ref, w3_ref, b3_ref, fr_ref, wo_ref, dl_ref, o_ref, *, d):
    fr = fr_ref[...]
    h = jnp.sin(fr * (_dot_hi(z_ref[...], w1_ref[...]) + b1_ref[...]))
    h = jnp.sin(fr * (_dot_hi(h, w2_ref[...]) + b2_ref[...]))
    h = jnp.sin(fr * (_dot_hi(h, w3_ref[...]) + b3_ref[...]))
    filt = _dot_hi(h, wo_ref[...])
    modu = jnp.exp(-t_ref[...] * dl_ref[...]) + HY_MOD_SHIFT
    for q in range(o_ref.shape[0]):
        o_ref[q] = filt[:, q * d:(q + 1) * d] * modu


def _hyena_filters(n, op, d):
    t = np.linspace(0.0, 1.0, n, dtype=np.float32)[:, None]
    ang = (2.0 * math.pi * np.arange(n, dtype=np.float32)[:, None] / np.float32(n)).astype(np.float32)
    f = np.linspace(1e-4, HY_BANDS - 1, HY_BANDS, dtype=np.float32)[None, :]
    zfeat = jnp.concatenate([jnp.asarray(t), jnp.cos(jnp.asarray(f * ang)), -jnp.sin(jnp.asarray(f * ang))], axis=-1)
    emb_pad = op['f_w1'].shape[0]
    zfeat = jnp.pad(zfeat, ((0, 0), (0, emb_pad - HY_EMB)))
    deltas = np.abs(np.linspace(math.log(HY_TARGET) / HY_SLOW_PCT, math.log(HY_TARGET) / HY_FAST_PCT, d,
                                dtype=np.float32))[None, :]
    tn = min(n, 256)
    nq = 2 * HY_ORDER
    hf = op['f_w2'].shape[0]
    full = lambda *shape: pl.BlockSpec(shape, lambda i: (0,) * len(shape))
    return pl.pallas_call(
        functools.partial(_filter_kernel, d=d),
        grid=(n // tn,),
        in_specs=[pl.BlockSpec((tn, emb_pad), lambda i: (i, 0)), pl.BlockSpec((tn, 1), lambda i: (i, 0)),
                  full(emb_pad, hf), full(1, hf), full(hf, hf), full(1, hf), full(hf, hf), full(1, hf), full(1, hf),
                  full(hf, nq * d), full(1, d)],
        out_specs=pl.BlockSpec((nq, tn, d), lambda i: (0, i, 0)),
        out_shape=jax.ShapeDtypeStruct((nq, n, d), F32),
        compiler_params=_cparams("parallel"),
        name="hyena_filter",
    )(zfeat, jnp.asarray(t), op['f_w1'], op['f_b1'], op['f_w2'], op['f_b2'], op['f_w3'], op['f_b3'], op['f_freq'],
      op['f_out'], jnp.asarray(deltas))


def _dft(n, rows, cols, sign=-1.0):
    k = np.arange(rows, dtype=np.float64)[:, None]
    m = np.arange(cols, dtype=np.float64)[None, :]
    ang = sign * 2.0 * np.pi * ((k * m) % n) / n
    return np.cos(ang), np.sin(ang)


def _stack(re, im):
    return jnp.asarray(np.concatenate([re, im], axis=0).astype(np.float32)).astype(BF16)


def _fft_tables(n_seq):
    n = 2 * n_seq
    n1 = n // FFT_N2
    f1r, f1i = _dft(n1, n1, n1 // 2)
    h1r, h1i = _dft(n1, n1 // 2, n1, sign=1.0)
    k1 = np.arange(n1, dtype=np.float64)[:, None, None]
    k2 = np.arange(FFT_N2, dtype=np.float64)[None, :, None]
    j2 = np.arange(FFT_N2, dtype=np.float64)[None, None, :]
    ang = -2.0 * np.pi * (((k2 * j2 * n1) + k1 * j2) % n) / n
    gr, gi = np.cos(ang), np.sin(ang)
    g_fwd = np.concatenate([gr, gi], axis=1)
    g_inv = np.concatenate([np.swapaxes(gr, 1, 2), np.swapaxes(gi, 1, 2)], axis=1)
    eye = np.eye(SUBLANES)
    kron = lambda m: np.kron(m, eye)
    return {'f1': _stack(kron(f1r), kron(f1i)), 'h1': _stack(kron(h1r), kron(h1i)),
            'g_fwd': jnp.asarray(g_fwd.astype(np.float32)).astype(BF16),
            'g_inv': jnp.asarray(g_inv.astype(np.float32)).astype(BF16), 'n1': n1}


def _dense_tables(n_seq):
    n = 2 * n_seq
    fr, fi = _dft(n, n, n_seq)
    hr, hi = _dft(n, n_seq, n, sign=1.0)
    return {'f': _stack(fr, fi), 'h': _stack(hr, hi)}


def _fft_a_kernel(f_ref, zr_ref, zi_ref, ar_ref, ai_ref, *, n1, cplx):
    f = f_ref[...]
    half, tj, d = zr_ref.shape[1:]
    m = n1 * SUBLANES
    res_r, res_i = [], []
    for s0 in range(0, tj, SUBLANES):
        js = slice(s0, s0 + SUBLANES)
        zr = zr_ref[0, :, js, :].reshape(half * SUBLANES, d)
        p = jnp.dot(f, zr.astype(BF16), preferred_element_type=F32)
        if cplx:
            zi = zi_ref[0, :, js, :].reshape(half * SUBLANES, d)
            q = jnp.dot(f, zi.astype(BF16), preferred_element_type=F32)
            res_r.append((p[0:m] - q[m:]).reshape(n1, SUBLANES, d))
            res_i.append((p[m:] + q[0:m]).reshape(n1, SUBLANES, d))
        else:
            res_r.append(p[0:m].reshape(n1, SUBLANES, d))
            res_i.append(p[m:].reshape(n1, SUBLANES, d))
    ar_ref[0] = jnp.concatenate(res_r, axis=1).astype(BF16)
    ai_ref[0] = jnp.concatenate(res_i, axis=1).astype(BF16)


def _fft_a(u5, q, tabs, cplx):
    _, s, half, n2, d = u5.shape
    n1 = tabs['n1']
    assert half == n1 // 2 and n2 == FFT_N2
    sp = s // 2 if cplx else s
    tj = 2 * SUBLANES
    zi_map = (lambda si, j: (q, si + sp, 0, j, 0)) if cplx else (lambda si, j: (q, si, 0, j, 0))
    out = pl.BlockSpec((1, n1, tj, d), lambda si, j: (si, 0, j, 0))
    return pl.pallas_call(
        functools.partial(_fft_a_kernel, n1=n1, cplx=cplx),
        grid=(sp, n2 // tj),
        in_specs=[pl.BlockSpec((2 * n1 * SUBLANES, half * SUBLANES), lambda si, j: (0, 0)),
                  pl.BlockSpec((None, 1, half, tj, d), lambda si, j: (q, si, 0, j, 0)),
                  pl.BlockSpec((None, 1, half, tj, d), zi_map)],
        out_specs=[out, out],
        out_shape=[jax.ShapeDtypeStruct((sp, n1, n2, d), BF16)] * 2,
        compiler_params=_cparams("parallel", "parallel"),
        name="fft_stage1",
    )(tabs['f1'], u5, u5)


def _cplx_mm(s, xr, xi, conj):
    p = jnp.dot(s, xr.astype(BF16), preferred_element_type=F32)
    q = jnp.dot(s, xi.astype(BF16), preferred_element_type=F32)
    m = s.shape[0] // 2
    if conj:
        return p[0:m] + q[m:], q[0:m] - p[m:]
    return p[0:m] - q[m:], p[m:] + q[0:m]


def _fft_b_kernel(gf_ref, gi_ref, ar_ref, ai_ref, kr_ref, ki_ref, dr_ref, di_ref):
    kr = kr_ref[0]
    ki = ki_ref[0]
    seqs = range(ar_ref.shape[0])
    c = [_cplx_mm(gf_ref[0], ar_ref[s, 0], ai_ref[s, 0], False) for s in seqs]
    e = [(cr * kr - ci * ki, cr * ki + ci * kr) for cr, ci in c]
    dd = [_cplx_mm(gi_ref[0], er, ei, True) for er, ei in e]
    for s, (dr, di) in zip(seqs, dd):
        dr_ref[s, 0] = dr.astype(BF16)
        di_ref[s, 0] = di.astype(BF16)


def _fft_b(ar4, ai4, kr, ki, order, tabs):
    sp, n1, _, d = ar4.shape
    ns = 2 if sp % 2 == 0 else 1
    blk = pl.BlockSpec((ns, 1, FFT_N2, d), lambda k1, si: (si, k1, 0, 0))
    tab = pl.BlockSpec((1, 2 * FFT_N2, FFT_N2), lambda k1, si: (k1, 0, 0))
    kb = pl.BlockSpec((None, 1, FFT_N2, d), lambda k1, si: (order, k1, 0, 0))
    return pl.pallas_call(
        _fft_b_kernel,
        grid=(n1, sp // ns),
        in_specs=[tab, tab, blk, blk, kb, kb],
        out_specs=[blk, blk],
        out_shape=[jax.ShapeDtypeStruct((sp, n1, FFT_N2, d), BF16)] * 2,
        compiler_params=_cparams("parallel", "arbitrary"),
        name="fft_stage2_filter",
    )(tabs['g_fwd'], tabs['g_inv'], ar4, ai4, kr, ki)


def _fft_c_kernel(h_ref, dr_ref, di_ref, u_ref, x_ref, sk_ref, o_ref):
    h = h_ref[...]
    n1, tj, d = dr_ref.shape[1:]
    half = n1 // 2
    sk = sk_ref[...]
    dr_all = dr_ref[0].astype(F32)
    di_all = di_ref[0].astype(F32)
    for s0 in range(0, tj, SUBLANES):
        js = slice(s0, s0 + SUBLANES)
        yr, yi = _cplx_mm(h, dr_all[:, js, :].reshape(n1 * SUBLANES, d),
                          di_all[:, js, :].reshape(n1 * SUBLANES, d), False)
        for part, y in enumerate((yr, yi)):
            u = u_ref[part, 0, :, js, :]
            o_ref[part, 0, :, js, :] = x_ref[part, 0, :, js, :] * (y.reshape(half, SUBLANES, d) + u * sk)


def _fft_c(dr, di, u6, uq, x6, xq, skip, tabs):
    sp, n1, n2, d = dr.shape
    half = n1 // 2
    tj = 2 * SUBLANES
    pair = lambda q: pl.BlockSpec((None, 2, 1, half, tj, d), lambda si, j: (q, 0, si, 0, j, 0))
    dblk = pl.BlockSpec((1, n1, tj, d), lambda si, j: (si, 0, j, 0))
    out = pl.pallas_call(
        _fft_c_kernel,
        grid=(sp, n2 // tj),
        in_specs=[pl.BlockSpec((2 * half * SUBLANES, n1 * SUBLANES), lambda si, j: (0, 0)), dblk, dblk,
                  pair(uq), pair(xq), pl.BlockSpec((1, d), lambda si, j: (0, 0))],
        out_specs=pl.BlockSpec((2, 1, half, tj, d), lambda si, j: (0, si, 0, j, 0)),
        out_shape=jax.ShapeDtypeStruct((2, sp, half, n2, d), F32),
        compiler_params=_cparams("parallel", "parallel"),
        name="fft_inverse_stage1_gate",
    )(tabs['h1'], dr, di, u6, x6, skip.reshape(1, d))
    return out


def _spec_b_kernel(gf_ref, ar_ref, ai_ref, kr_ref, ki_ref, *, scale):
    fr, fi = _cplx_mm(gf_ref[0], ar_ref[0, 0], ai_ref[0, 0], False)
    gr, gi = _cplx_mm(gf_ref[0], ar_ref[1, 0], ai_ref[1, 0], False)
    kr_ref[0, 0] = (fr + gr) * scale
    ki_ref[0, 0] = (fi - gi) * scale


def _filter_spectrum_fft(filt, tabs, d):
    nq, n, _ = filt.shape
    n1 = tabs['n1']
    ar, ai = _fft_a(filt.reshape(1, nq, n1 // 2, FFT_N2, d), 0, tabs, cplx=False)
    orders = nq // 2
    ar5 = ar.reshape(orders, 2, n1, FFT_N2, d)
    ai5 = ai.reshape(orders, 2, n1, FFT_N2, d)
    blk = pl.BlockSpec((None, 2, 1, FFT_N2, d), lambda o, k1: (o, 0, k1, 0, 0))
    out = pl.BlockSpec((1, 1, FFT_N2, d), lambda o, k1: (o, k1, 0, 0))
    return pl.pallas_call(
        functools.partial(_spec_b_kernel, scale=1.0 / (2 * n)),
        grid=(orders, n1),
        in_specs=[pl.BlockSpec((1, 2 * FFT_N2, FFT_N2), lambda o, k1: (k1, 0, 0)), blk, blk],
        out_specs=[out, out],
        out_shape=[jax.ShapeDtypeStruct((orders, n1, FFT_N2, d), F32)] * 2,
        compiler_params=_cparams("parallel", "parallel"),
        name="filter_spectrum",
    )(tabs['g_fwd'], ar5, ai5)


def _long_conv_fft(u, uq, x, xq, skip, kr, ki, order, tabs):
    _, b, n, d = u.shape
    half = tabs['n1'] // 2
    ar, ai = _fft_a(u.reshape(u.shape[0], b, half, FFT_N2, d), uq, tabs, cplx=True)
    dr, di = _fft_b(ar, ai, kr, ki, order, tabs)
    six = lambda a: a.reshape(a.shape[0], 2, b // 2, half, FFT_N2, d)
    return _fft_c(dr, di, six(u), uq, six(x), xq, skip, tabs).reshape(b, n, d)


def _dense_spec_kernel(f_ref, filt_ref, kr_ref, ki_ref, *, scale):
    f = f_ref[...]
    n = f.shape[0] // 2
    pf = jnp.dot(f, filt_ref[0, 0].astype(BF16), preferred_element_type=F32)
    pg = jnp.dot(f, filt_ref[0, 1].astype(BF16), preferred_element_type=F32)
    kr_ref[0] = (pf[0:n] + pg[0:n]) * scale
    ki_ref[0] = (pf[n:] - pg[n:]) * scale


def _filter_spectrum_dense(filt, tabs, d):
    nq, n, _ = filt.shape
    orders = nq // 2
    f4 = filt.reshape(orders, 2, n, d)
    out = pl.BlockSpec((1, 2 * n, d), lambda o: (o, 0, 0))
    return pl.pallas_call(
        functools.partial(_dense_spec_kernel, scale=1.0 / (2 * n)),
        grid=(orders,),
        in_specs=[pl.BlockSpec((4 * n, n), lambda o: (0, 0)), pl.BlockSpec((1, 2, n, d), lambda o: (o, 0, 0, 0))],
        out_specs=[out, out],
        out_shape=[jax.ShapeDtypeStruct((orders, 2 * n, d), F32)] * 2,
        compiler_params=_cparams("parallel"),
        name="filter_spectrum_dense",
    )(tabs['f'], f4)


def _dense_conv_kernel(f_ref, h_ref, u_ref, x_ref, kr_ref, ki_ref, sk_ref, o_ref):
    cr, ci = _cplx_mm(f_ref[...], u_ref[0, 0], u_ref[1, 0], False)
    kr = kr_ref[...]
    ki = ki_ref[...]
    yr, yi = _cplx_mm(h_ref[...], cr * kr - ci * ki, cr * ki + ci * kr, False)
    sk = sk_ref[...]
    o_ref[0, 0] = x_ref[0, 0] * (yr + u_ref[0, 0] * sk)
    o_ref[1, 0] = x_ref[1, 0] * (yi + u_ref[1, 0] * sk)


def _long_conv_dense(u, x, skip, kr, ki, tabs):
    b, n, d = u.shape
    sp = b // 2
    pair = pl.BlockSpec((2, 1, n, d), lambda si: (0, si, 0, 0))
    kb = pl.BlockSpec((2 * n, d), lambda si: (0, 0))
    out = pl.pallas_call(
        _dense_conv_kernel,
        grid=(sp,),
        in_specs=[pl.BlockSpec((4 * n, n), lambda si: (0, 0)), pl.BlockSpec((2 * n, 2 * n), lambda si: (0, 0)),
                  pair, pair, kb, kb, pl.BlockSpec((1, d), lambda si: (0, 0))],
        out_specs=pair,
        out_shape=jax.ShapeDtypeStruct((2, sp, n, d), F32),
        compiler_params=_cparams("parallel"),
        name="long_conv_dense",
    )(tabs['f'], tabs['h'], u.reshape(2, sp, n, d), x.reshape(2, sp, n, d), kr, ki, skip.reshape(1, d))
    return out.reshape(b, n, d)


def _hyena_mixer(x, g, mod, op, fft_tabs, dense_tabs):
    b, n, d = x.shape
    z = _proj_conv(x, g, mod, op['w_in'], op['b_in'], op['conv_w'], op['conv_b'])
    filt = _hyena_filters(n, op, d)
    if n <= DENSE_FFT_MAX:
        kr, ki = _filter_spectrum_dense(filt, dense_tabs, d)
        y = _long_conv_dense(z[0], z[1], op['skip'][0], kr[0], ki[0], dense_tabs)
        y = _long_conv_dense(y, z[2], op['skip'][1], kr[1], ki[1], dense_tabs)
    else:
        kr, ki = _filter_spectrum_fft(filt, fft_tabs, d)
        y = _long_conv_fft(z, 0, z, 1, op['skip'][0], kr, ki, 0, fft_tabs)
        y = _long_conv_fft(y[None], 0, z, 2, op['skip'][1], kr, ki, 1, fft_tabs)
    return _out_res(y, x, mod, op['w_out'], op['b_out'])


def _even_mixer(x, ctx, g, mod_l, mod_c, ep, tabs, need_ctx):
    rc, vc, kkc, gc, lwc, ac, kdc, qc, kac, vac = _even_prep(ctx, g, mod_c, ep, tabs, rope=False)
    rl, vl, kkl, gl, lwl, al, kdl, ql, kal, val = _even_prep(x, g, mod_l, ep, tabs, rope=True)
    b = x.shape[0]
    c = rl.shape[-1]
    h0 = jnp.zeros((2, b, HEAD_DIM, c), F32)
    y_ctx, s_ctx = _rwkv(rc, vc, kkc, lwc, ac, kdc, h0)
    y_lat, _ = _rwkv(rl, vl, kkl, lwl, al, kdl, s_ctx)
    b_lat = _attention(ql, kal, val, kac, vac, ep['sink'], local=True)
    x_new = _even_out(y_lat, rl, vl, gl, kdl, b_lat, x, mod_l, ep, tabs)
    if not need_ctx:
        return x_new, None
    b_ctx = _attention(qc, kac, vac, kac, vac, ep['sink'], local=False)
    ctx_new = _even_out(y_ctx, rc, vc, gc, kdc, b_ctx, ctx, mod_c, ep, tabs)
    return x_new, ctx_new


def _rope_tables(n_tokens):
    rows = n_tokens // GRID_W
    row = jnp.repeat(jnp.arange(rows), GRID_W).astype(F32)
    col = jnp.tile(jnp.arange(GRID_W), rows).astype(F32)
    n_freq = HEAD_DIM // 4
    inv = ROPE_THETA ** (-jnp.arange(n_freq, dtype=F32) / n_freq)
    ang = jnp.concatenate([row[:, None] * inv, col[:, None] * inv], axis=-1)
    cos, sin = jnp.cos(ang), jnp.sin(ang)
    reps = LANES // HEAD_DIM
    cos_t = jnp.tile(jnp.concatenate([cos, cos], axis=-1), (1, reps))
    sin_t = jnp.tile(jnp.concatenate([-sin, sin], axis=-1), (1, reps))
    return cos_t, sin_t


def _block_ones(width):
    idx = np.arange(width) // HEAD_DIM
    return jnp.asarray((idx[:, None] == idx[None, :]).astype(np.float32)).astype(BF16)


def _lora_pad(w):
    z = jnp.zeros_like(w[0])
    return jnp.stack([jnp.concatenate([w[0], z], axis=0), jnp.concatenate([z, w[1]], axis=0)], axis=0)


def kernel(x, c, ctx, c_ctx, ada_w, ada_b, norm1_g, norm2_g, ffn_up, ffn_conv_w, ffn_conv_b, ffn_down, ev_w_in, ev_mu_prev, ev_mu_next, ev_w0, ev_w2, ev_a0, ev_a2, ev_g2, ev_k_k, ev_k_a, ev_r_k, ev_ln_w, ev_ln_b, ev_q_norm, ev_k_norm, ev_sink, ev_w_out, od_w_in, od_b_in, od_conv_w, od_conv_b, od_f_w1, od_f_b1, od_f_w2, od_f_b2, od_f_w3, od_f_b3, od_f_freq, od_f_out, od_skip, od_w_out, od_b_out):
    bsz, seq, d = x.shape
    lc = ctx.shape[1]
    depth = ada_w.shape[0]
    a_width = ev_k_k.shape[-1]
    nq = ev_sink.shape[-1] * HEAD_DIM
    nk = B_KV_HEADS * HEAD_DIM

    cond = jnp.zeros((BF16_ROWS, d), F32).at[:bsz].set(c).at[bsz].set(c_ctx)
    mod = _ada_mod(cond, ada_w, ada_b)

    cos_t, sin_t = _rope_tables(seq)
    tabs = {'cos': cos_t, 'sin': sin_t, 'e': _block_ones(nq)}
    fft_tabs = _fft_tables(seq) if seq > DENSE_FFT_MAX else None
    dense_lat = _dense_tables(seq) if seq <= DENSE_FFT_MAX else None
    fft_ctx = _fft_tables(lc) if lc > DENSE_FFT_MAX else None
    dense_ctx = _dense_tables(lc) if lc <= DENSE_FFT_MAX else None

    for layer in range(depth):
        need_ctx = layer < depth - 1
        even = layer % 2 == 0
        j = layer // 2
        mod_l = mod[layer, :bsz].reshape(bsz, 1, 6 * d)
        mod_c = jnp.broadcast_to(mod[layer, bsz].reshape(1, 1, 6 * d), (bsz, 1, 6 * d))
        if even:
            ep = {'w_in': ev_w_in[j].astype(BF16), 'mu_prev': ev_mu_prev[j][None], 'mu_next': ev_mu_next[j][None],
                  'w0': ev_w0[j], 'w2pad': _lora_pad(ev_w2[j]).astype(BF16), 'a0': ev_a0[j],
                  'a2pad': _lora_pad(ev_a2[j]).astype(BF16), 'g2': ev_g2[j].astype(BF16),
                  'k_k': ev_k_k[j][None], 'k_a': ev_k_a[j][None], 'r_k': ev_r_k[j].reshape(1, a_width),
                  'ln_w': ev_ln_w[j][None], 'ln_b': ev_ln_b[j][None],
                  'q_norm_t': jnp.tile(ev_q_norm[j], nq // HEAD_DIM)[None],
                  'k_norm_t': jnp.tile(ev_k_norm[j], nk // HEAD_DIM)[None],
                  'sink': ev_sink[j], 'w_out': ev_w_out[j].astype(BF16), 'nq': nq, 'nk': nk}
            x, ctx_new = _even_mixer(x, ctx, norm1_g[layer], mod_l, mod_c, ep, tabs, need_ctx)
        else:
            emb_pad = HEAD_DIM
            op = {'w_in': od_w_in[j].astype(BF16), 'b_in': od_b_in[j], 'conv_w': od_conv_w[j],
                  'conv_b': od_conv_b[j],
                  'f_w1': jnp.pad(od_f_w1[j], ((0, emb_pad - HY_EMB), (0, 0))), 'f_b1': od_f_b1[j][None],
                  'f_w2': od_f_w2[j], 'f_b2': od_f_b2[j][None], 'f_w3': od_f_w3[j], 'f_b3': od_f_b3[j][None],
                  'f_freq': od_f_freq[j][None], 'f_out': od_f_out[j], 'skip': od_skip[j],
                  'w_out': od_w_out[j].astype(BF16), 'b_out': od_b_out[j]}
            ctx_new = _hyena_mixer(ctx, norm1_g[layer], mod_c, op, fft_ctx, dense_ctx) if need_ctx else None
            x = _hyena_mixer(x, norm1_g[layer], mod_l, op, fft_tabs, dense_lat)
        w_up = ffn_up[layer].astype(BF16)
        w_down = ffn_down[layer].astype(BF16)
        x = _conv_ffn(x, norm2_g[layer], mod_l, w_up, ffn_conv_w[layer], ffn_conv_b[layer], w_down)
        if need_ctx:
            ctx = _conv_ffn(ctx_new, norm2_g[layer], mod_c, w_up, ffn_conv_w[layer], ffn_conv_b[layer], w_down)
    return x
```

```python
import functools
import math

import numpy as np
import jax
import jax.numpy as jnp
from jax import lax
from jax.experimental import pallas as pl
from jax.experimental.pallas import tpu as pltpu

F32 = jnp.float32
BF16 = jnp.bfloat16

HEAD_DIM = 64
GRID_W = 64
DECAY_LORA = 64
ICLR_LORA = 64
GATE_LORA = 128
RWKV_GN_EPS = 64e-5
B_KV_HEADS = 2
WINDOW = 128
BLOCK = 128
ROPE_THETA = 10000.0
MASK_VALUE = -1e30
HY_ORDER = 2
HY_EMB = 33
HY_BANDS = (HY_EMB - 1) // 2
HY_TARGET = 1e-2
HY_FAST_PCT = 0.3
HY_SLOW_PCT = 1.5
HY_MOD_SHIFT = 0.05
NORM_EPS = 1e-6

V7X_VMEM_BYTES = 64 * 1024 * 1024
VMEM_LIMIT_BYTES = V7X_VMEM_BYTES * 3 // 4
LANES = 128
SUBLANES = 8
BF16_ROWS = 16
FFN_ROW_PARTS = 2
RWKV_CHUNK = 64
RWKV_PREP_CHUNKS = 4
RWKV_INV_BASE = 8
RWKV_GRAM_PASSES = 1
RWKV_INV_PASSES = 1
RWKV_REST_PASSES = 1
RWKV_SCAN_PASSES = 1
FFT_N2 = 128
DENSE_FFT_MAX = 512

HIGHEST = lax.Precision.HIGHEST


def _cparams(*sem):
    return pltpu.CompilerParams(dimension_semantics=sem, vmem_limit_bytes=VMEM_LIMIT_BYTES)


def _dot(a, b):
    return jnp.dot(a.astype(BF16), b.astype(BF16), preferred_element_type=F32)


def _dot_nt(a, b):
    return lax.dot_general(a.astype(BF16), b.astype(BF16), (((1,), (1,)), ((), ())), preferred_element_type=F32)


def _dot_tn(a, b):
    return lax.dot_general(a.astype(BF16), b.astype(BF16), (((0,), (0,)), ((), ())), preferred_element_type=F32)


def _dot_hi(a, b):
    return jnp.dot(a, b, preferred_element_type=F32, precision=HIGHEST)


def _dot_nt_hi(a, b):
    return lax.dot_general(a, b, (((1,), (1,)), ((), ())), preferred_element_type=F32, precision=HIGHEST)


def _dot_tn_hi(a, b):
    return lax.dot_general(a, b, (((0,), (0,)), ((), ())), preferred_element_type=F32, precision=HIGHEST)


def _segsum(x, e_ref):
    hi = x.astype(BF16)
    lo = (x - hi.astype(F32)).astype(BF16)
    e = e_ref[...]
    return jnp.dot(hi, e, preferred_element_type=F32) + jnp.dot(lo, e, preferred_element_type=F32)


def _norm_mod(x, g, shift, scale):
    xn = x * lax.rsqrt(jnp.mean(x * x, axis=-1, keepdims=True) + NORM_EPS)
    return (xn * g) * (1.0 + scale) + shift


def _row_tile(t, want):
    tm = min(t, want)
    assert t % tm == 0 and tm % BF16_ROWS == 0
    return tm


def _ada_kernel(c_ref, w_ref, b_ref, o_ref):
    c = c_ref[...]
    s = c * jax.nn.sigmoid(c)
    o_ref[0] = _dot(s, w_ref[0]) + b_ref[0]


def _ada_mod(cond, ada_w, ada_b):
    depth, d, n = ada_w.shape
    tn = n // 4
    rows = cond.shape[0]
    return pl.pallas_call(
        _ada_kernel,
        grid=(depth, n // tn),
        in_specs=[pl.BlockSpec((rows, d), lambda l, j: (0, 0)),
                  pl.BlockSpec((1, d, tn), lambda l, j: (l, 0, j)),
                  pl.BlockSpec((1, 1, tn), lambda l, j: (l, 0, j))],
        out_specs=pl.BlockSpec((1, rows, tn), lambda l, j: (l, 0, j)),
        out_shape=jax.ShapeDtypeStruct((depth, rows, n), F32),
        compiler_params=_cparams("parallel", "parallel"),
        name="ada_mod",
    )(cond, ada_w, ada_b.reshape(depth, 1, n))


def _halo_specs(tm, t, d, nargs):
    r = tm // BF16_ROWS
    last = t // BF16_ROWS - 1
    if nargs == 3:
        prev = pl.BlockSpec((1, BF16_ROWS, d), lambda bi, i, j: (bi, jnp.maximum(i * r - 1, 0), 0))
        nxt = pl.BlockSpec((1, BF16_ROWS, d), lambda bi, i, j: (bi, jnp.minimum((i + 1) * r, last), 0))
    else:
        prev = pl.BlockSpec((1, BF16_ROWS, d), lambda bi, i: (bi, jnp.maximum(i * r - 1, 0), 0))
        nxt = pl.BlockSpec((1, BF16_ROWS, d), lambda bi, i: (bi, jnp.minimum((i + 1) * r, last), 0))
    return prev, nxt


def _fill_h(x_ref, xp_ref, xn_ref, g, shift, scale, h_scr, tm):
    h_scr[0:BF16_ROWS] = _norm_mod(xp_ref[0], g, shift, scale).astype(BF16)
    h_scr[BF16_ROWS:BF16_ROWS + tm] = _norm_mod(x_ref[0], g, shift, scale).astype(BF16)
    h_scr[BF16_ROWS + tm:2 * BF16_ROWS + tm] = _norm_mod(xn_ref[0], g, shift, scale).astype(BF16)


def _conv3_rows(u, u_scr, cw, cb, tm, t_total, part=0, nparts=1):
    i = pl.program_id(1)
    o = BF16_ROWS
    rows = tm // nparts
    base = part * (rows + 2 * o)
    head_ok = (i > 0) if part == 0 else True
    tail_ok = (i < t_total // tm - 1) if part == nparts - 1 else True
    u_scr[base:base + o] = jnp.where(head_ok, u[0:o], 0.0)
    u_scr[base + o:base + o + rows] = u[o:o + rows]
    u_scr[base + o + rows:base + 2 * o + rows] = jnp.where(tail_ok, u[o + rows:], 0.0)
    s = base + o
    return (u_scr[s - 1:s - 1 + rows] * cw[0:1] + u_scr[s:s + rows] * cw[1:2] + u_scr[s + 1:s + 1 + rows] * cw[2:3]
            + cb)


def _proj_conv_kernel(x_ref, xp_ref, xn_ref, g_ref, mod_ref, w_ref, b_ref, cw_ref, cb_ref, o_ref, h_scr, u_scr,
                      *, d, tm, t_total, tn):
    m = mod_ref[0]
    _fill_h(x_ref, xp_ref, xn_ref, g_ref[...], m[:, 0:d], m[:, d:2 * d], h_scr, tm)
    nparts = FFN_ROW_PARTS
    rows = tm // nparts
    n = w_ref.shape[1]

    def project(j):
        cols = slice(j * tn, (j + 1) * tn)
        return [jnp.dot(h_scr[p * rows:(p + 1) * rows + 2 * BF16_ROWS], w_ref[:, cols],
                        preferred_element_type=F32) + b_ref[:, cols] for p in range(nparts)]

    nxt_u = project(0)
    for j in range(n // tn):
        cols = slice(j * tn, (j + 1) * tn)
        us = nxt_u
        if (j + 1) * tn < n:
            nxt_u = project(j + 1)
        q, lane0 = divmod(j * tn, d)
        for p in range(nparts):
            o_ref[q, 0, p * rows:(p + 1) * rows, lane0:lane0 + tn] = _conv3_rows(
                us[p], u_scr.at[j % 2], cw_ref[:, cols], cb_ref[:, cols], tm, t_total, p, nparts)


def _proj_conv(x, g, mod, w, bias, cw, cb):
    b, t, d = x.shape
    n = w.shape[1]
    tm = _row_tile(t, 512)
    tn = d // 2
    prev, nxt = _halo_specs(tm, t, d, 2)
    resident = lambda *shape: pl.BlockSpec(shape, lambda bi, i: (0,) * len(shape), pipeline_mode=pl.Buffered(1))
    return pl.pallas_call(
        functools.partial(_proj_conv_kernel, d=d, tm=tm, t_total=t, tn=tn),
        grid=(b, t // tm),
        in_specs=[pl.BlockSpec((1, tm, d), lambda bi, i: (bi, i, 0)), prev, nxt,
                  pl.BlockSpec((1, d), lambda bi, i: (0, 0)),
                  pl.BlockSpec((1, 1, mod.shape[-1]), lambda bi, i: (bi, 0, 0)),
                  resident(d, n), resident(1, n), resident(3, n), resident(1, n)],
        out_specs=pl.BlockSpec((n // d, 1, tm, d), lambda bi, i: (0, bi, i, 0)),
        out_shape=jax.ShapeDtypeStruct((n // d, b, t, d), F32),
        scratch_shapes=[pltpu.VMEM((tm + 2 * BF16_ROWS, d), BF16),
                        pltpu.VMEM((2, tm + 2 * BF16_ROWS * FFN_ROW_PARTS, tn), F32)],
        compiler_params=_cparams("parallel", "parallel"),
        name="hyena_in_proj",
    )(x, x, x, g.reshape(1, d), mod, w, bias.reshape(1, n), cw, cb.reshape(1, n))


def _ffn_kernel(x_ref, xp_ref, xn_ref, g_ref, mod_ref, wup_ref, cw_ref, cb_ref, wd_ref,
                o_ref, h_scr, ug_scr, uv_scr, acc_scr, *, d, tm, t_total, f, tf):
    m = mod_ref[0]
    _fill_h(x_ref, xp_ref, xn_ref, g_ref[...], m[:, 3 * d:4 * d], m[:, 4 * d:5 * d], h_scr, tm)
    nparts = FFN_ROW_PARTS
    rows = tm // nparts
    nf = f // tf

    def up_proj(j):
        hs = [h_scr[p * rows:(p + 1) * rows + 2 * BF16_ROWS] for p in range(nparts)]
        return ([jnp.dot(h, wup_ref[:, j * tf:(j + 1) * tf], preferred_element_type=F32) for h in hs],
                [jnp.dot(h, wup_ref[:, f + j * tf:f + (j + 1) * tf], preferred_element_type=F32) for h in hs])

    nxt_up = up_proj(0)
    for j in range(nf):
        gs = slice(j * tf, (j + 1) * tf)
        vs = slice(f + j * tf, f + (j + 1) * tf)
        ug, uv = nxt_up
        if j + 1 < nf:
            nxt_up = up_proj(j + 1)
        for p in range(nparts):
            gate = _conv3_rows(ug[p], ug_scr.at[j % 2], cw_ref[:, gs], cb_ref[:, gs], tm, t_total, p, nparts)
            val = _conv3_rows(uv[p], uv_scr.at[j % 2], cw_ref[:, vs], cb_ref[:, vs], tm, t_total, p, nparts)
            act = (gate * jax.nn.sigmoid(gate)) * val
            down = jnp.dot(act.astype(BF16), wd_ref[gs, :], preferred_element_type=F32)
            part = slice(p * rows, (p + 1) * rows)
            if j == 0:
                acc_scr[part] = down
            else:
                acc_scr[part] += down
    o_ref[0] = x_ref[0] + m[:, 5 * d:6 * d] * acc_scr[...]


def _conv_ffn(x, g, mod, w_up, cw, cb, w_down):
    b, t, d = x.shape
    f = w_down.shape[0]
    tm = _row_tile(t, 1024)
    tf = 256 if f % 256 == 0 else 128
    prev, nxt = _halo_specs(tm, t, d, 2)
    resident = lambda *shape: pl.BlockSpec(shape, lambda bi, i: (0,) * len(shape), pipeline_mode=pl.Buffered(1))
    u_rows = tm + 2 * BF16_ROWS * FFN_ROW_PARTS
    return pl.pallas_call(
        functools.partial(_ffn_kernel, d=d, tm=tm, t_total=t, f=f, tf=tf),
        grid=(b, t // tm),
        in_specs=[pl.BlockSpec((1, tm, d), lambda bi, i: (bi, i, 0)), prev, nxt,
                  pl.BlockSpec((1, d), lambda bi, i: (0, 0)),
                  pl.BlockSpec((1, 1, mod.shape[-1]), lambda bi, i: (bi, 0, 0)),
                  resident(d, 2 * f), resident(3, 2 * f), resident(1, 2 * f), resident(f, d)],
        out_specs=pl.BlockSpec((1, tm, d), lambda bi, i: (bi, i, 0)),
        out_shape=jax.ShapeDtypeStruct((b, t, d), F32),
        scratch_shapes=[pltpu.VMEM((tm + 2 * BF16_ROWS, d), BF16),
                        pltpu.VMEM((2, u_rows, tf), F32),
                        pltpu.VMEM((2, u_rows, tf), F32),
                        pltpu.VMEM((tm, d), F32)],
        compiler_params=_cparams("parallel", "parallel"),
        name="conv_ffn",
    )(x, x, x, g.reshape(1, d), mod, w_up, cw, cb.reshape(1, 2 * f), w_down)


def _out_res_kernel(y_ref, x_ref, mod_ref, w_ref, b_ref, o_ref, *, d):
    o = _dot(y_ref[0], w_ref[...]) + b_ref[...]
    o_ref[0] = x_ref[0] + mod_ref[0][:, 2 * d:3 * d] * o


def _out_res(y, x, mod, w, bias):
    b, t, d = x.shape
    tm = _row_tile(t, 512)
    return pl.pallas_call(
        functools.partial(_out_res_kernel, d=d),
        grid=(b, t // tm),
        in_specs=[pl.BlockSpec((1, tm, d), lambda bi, i: (bi, i, 0)),
                  pl.BlockSpec((1, tm, d), lambda bi, i: (bi, i, 0)),
                  pl.BlockSpec((1, 1, mod.shape[-1]), lambda bi, i: (bi, 0, 0)),
                  pl.BlockSpec((d, d), lambda bi, i: (0, 0)),
                  pl.BlockSpec((1, d), lambda bi, i: (0, 0))],
        out_specs=pl.BlockSpec((1, tm, d), lambda bi, i: (bi, i, 0)),
        out_shape=jax.ShapeDtypeStruct((b, t, d), F32),
        compiler_params=_cparams("parallel", "parallel"),
        name="out_proj_residual",
    )(y, x, mod, w, bias.reshape(1, d))


def _even_prep_kernel(x_ref, xp_ref, xn_ref, gn_ref, mod_ref, win_ref, mup_ref, mun_ref, w0_ref, w2_ref, a0_ref,
                      a2_ref, g2_ref, kk_ref, ka_ref, qn_ref, kn_ref, cos_ref, sin_ref, e_ref,
                      r_ref, v_ref, kkn_ref, g_ref, lw_ref, a_ref, kd_ref, q_ref, katt_ref, vatt_ref,
                      h_scr, p_scr, *, tm, d, c, a_in, rope):
    i = pl.program_id(1)
    nt = pl.num_programs(1)
    m = mod_ref[0]
    _fill_h(x_ref, xp_ref, xn_ref, gn_ref[...], m[:, 0:d], m[:, d:2 * d], h_scr, tm)
    p_scr[...] = jnp.dot(h_scr[...], win_ref[...], preferred_element_type=F32)
    o = BF16_ROWS
    pa = p_scr[o:o + tm, 0:a_in]
    rows = lax.broadcasted_iota(jnp.int32, (tm, 1), 0)
    prev = jnp.where((rows == 0) & (i == 0), 0.0, p_scr[o - 1:o - 1 + tm, 0:a_in])
    nxt = jnp.where((rows == tm - 1) & (i == nt - 1), 0.0, p_scr[o + 1:o + 1 + tm, 0:a_in])
    za = pa + mup_ref[...] * (prev - pa) + mun_ref[...] * (nxt - pa)

    r = za[:, 0:c]
    k = za[:, c:2 * c]
    v = za[:, 2 * c:3 * c]
    wd = jnp.tanh(za[:, 3 * c:3 * c + 2 * DECAY_LORA])
    ad = za[:, 3 * c + 2 * DECAY_LORA:3 * c + 2 * DECAY_LORA + 2 * ICLR_LORA]
    gd = za[:, 3 * c + 2 * DECAY_LORA + 2 * ICLR_LORA:a_in]
    r_ref[0] = r
    v_ref[0] = v
    g_ref[0] = _dot(jax.nn.sigmoid(gd), g2_ref[...])
    kkv = k * kk_ref[...]
    ss = _segsum(kkv * kkv, e_ref)
    kkn_ref[0] = kkv / jnp.maximum(jnp.sqrt(ss), 1e-12)
    for dd in range(2):
        y = w0_ref[dd:dd + 1] + _dot(wd, w2_ref[dd])
        lw_ref[dd, 0] = -math.exp(-0.5) * jax.nn.sigmoid(y)
        a = jax.nn.sigmoid(a0_ref[dd:dd + 1] + _dot(ad, a2_ref[dd]))
        a_ref[dd, 0] = a
        kd_ref[dd, 0] = k * (1.0 + (a - 1.0) * ka_ref[...])

    pb = p_scr[o:o + tm, a_in:]
    nq = q_ref.shape[-1]
    nk = katt_ref.shape[-1]
    q = pb[:, 0:nq]
    kat = pb[:, nq:nq + nk]
    vatt_ref[0] = pb[:, nq + nk:]
    half = HEAD_DIM // 2

    def norm_rope(x, gain, e):
        w = x.shape[-1]
        ms = _segsum(x * x, e) * (1.0 / HEAD_DIM)
        xn = x * lax.rsqrt(ms + NORM_EPS) * gain
        if not rope:
            return xn
        lane = lax.broadcasted_iota(jnp.int32, (1, w), 1)
        first = (lane % HEAD_DIM) < half
        swapped = jnp.where(first, pltpu.roll(xn, w - half, 1), pltpu.roll(xn, half, 1))
        reps = w // LANES
        cos = jnp.concatenate([cos_ref[...]] * reps, axis=1) if reps > 1 else cos_ref[...]
        sin = jnp.concatenate([sin_ref[...]] * reps, axis=1) if reps > 1 else sin_ref[...]
        return xn * cos + swapped * sin

    q_ref[0] = norm_rope(q, qn_ref[...], e_ref[...])
    katt_ref[0] = norm_rope(kat, kn_ref[...], e_ref[0:nk, 0:nk])


def _even_prep(x, g, mod, ep, tabs, rope):
    b, t, d = x.shape
    n = ep['w_in'].shape[1]
    c = ep['k_k'].shape[-1]
    a_in = ep['mu_prev'].shape[-1]
    nq = ep['nq']
    nk = ep['nk']
    tm = _row_tile(t, 256)
    prev, nxt = _halo_specs(tm, t, d, 2)
    full = lambda *shape: pl.BlockSpec(shape, lambda bi, i: (0,) * len(shape))
    tok = lambda w: pl.BlockSpec((1, tm, w), lambda bi, i: (bi, i, 0))
    tok2 = lambda w: pl.BlockSpec((2, 1, tm, w), lambda bi, i: (0, bi, i, 0))
    sd = lambda *shape: jax.ShapeDtypeStruct(shape, F32)
    return pl.pallas_call(
        functools.partial(_even_prep_kernel, tm=tm, d=d, c=c, a_in=a_in, rope=rope),
        grid=(b, t // tm),
        in_specs=[tok(d), prev, nxt, full(1, d), pl.BlockSpec((1, 1, mod.shape[-1]), lambda bi, i: (bi, 0, 0)),
                  full(d, n), full(1, a_in), full(1, a_in), full(2, c), full(2, 2 * DECAY_LORA, c),
                  full(2, c), full(2, 2 * ICLR_LORA, c), full(GATE_LORA, c), full(1, c), full(1, c),
                  full(1, nq), full(1, nk),
                  pl.BlockSpec((tm, LANES), lambda bi, i: (i, 0)), pl.BlockSpec((tm, LANES), lambda bi, i: (i, 0)),
                  full(nq, nq)],
        out_specs=[tok(c), tok(c), tok(c), tok(c), tok2(c), tok2(c), tok2(c), tok(nq), tok(nk), tok(nk)],
        out_shape=[sd(b, t, c), sd(b, t, c), sd(b, t, c), sd(b, t, c), sd(2, b, t, c), sd(2, b, t, c),
                   sd(2, b, t, c), sd(b, t, nq), sd(b, t, nk), sd(b, t, nk)],
        scratch_shapes=[pltpu.VMEM((tm + 2 * BF16_ROWS, d), BF16), pltpu.VMEM((tm + 2 * BF16_ROWS, n), F32)],
        compiler_params=_cparams("parallel", "parallel"),
        name="even_in_proj_prep",
    )(x, x, x, g.reshape(1, d), mod, ep['w_in'], ep['mu_prev'], ep['mu_next'], ep['w0'], ep['w2pad'], ep['a0'], ep['a2pad'], ep['g2'], ep['k_k'],
      ep['k_a'], ep['q_norm_t'], ep['k_norm_t'], tabs['cos'][:t], tabs['sin'][:t], tabs['e'])


HEADS_PER_GROUP = 4
GROUP_LANES = HEADS_PER_GROUP * HEAD_DIM


def _split_bf16(x):
    hi = x.astype(BF16)
    return hi, (x - hi.astype(F32)).astype(BF16)


def _block_diag(x, bmask):
    return jnp.where(bmask, jnp.concatenate([x] * HEADS_PER_GROUP, axis=0), jnp.zeros((), x.dtype))


def _head_mm(lhs, rhs, bmask, passes, nt=False):
    dn = (((1,), (1,)), ((), ())) if nt else (((1,), (0,)), ((), ()))
    if passes == 1:
        return lax.dot_general(lhs.astype(BF16), _block_diag(rhs.astype(BF16), bmask), dn, preferred_element_type=F32)
    lh, ll = _split_bf16(lhs)
    rh, rl = _split_bf16(rhs)
    m = lhs.shape[0]
    top = lax.dot_general(jnp.concatenate([lh, ll], axis=0), _block_diag(rh, bmask), dn, preferred_element_type=F32)
    return top[0:m] + top[m:] + lax.dot_general(lh, _block_diag(rl, bmask), dn, preferred_element_type=F32)


def _head_mm_tn(lhs, rhs, lane_head, passes):
    dn = (((0,), (0,)), ((), ()))
    if passes == 1:
        full = lax.dot_general(lhs.astype(BF16), rhs.astype(BF16), dn, preferred_element_type=F32)
    else:
        lh, ll = _split_bf16(lhs)
        rh, rl = _split_bf16(rhs)
        full = (lax.dot_general(jnp.concatenate([lh, ll], axis=0), jnp.concatenate([rh, rh], axis=0), dn,
                                preferred_element_type=F32)
                + lax.dot_general(lh, rl, dn, preferred_element_type=F32))
    out = jnp.where(lane_head == 0, full[0:HEAD_DIM], 0.0)
    for h in range(1, HEADS_PER_GROUP):
        out = out + jnp.where(lane_head == h, full[h * HEAD_DIM:(h + 1) * HEAD_DIM], 0.0)
    return out


def _group_masks():
    r = lax.broadcasted_iota(jnp.int32, (GROUP_LANES, GROUP_LANES), 0)
    c = lax.broadcasted_iota(jnp.int32, (GROUP_LANES, GROUP_LANES), 1)
    bmask = (r // HEAD_DIM) == (c // HEAD_DIM)
    lane_head = lax.broadcasted_iota(jnp.int32, (1, GROUP_LANES), 1) // HEAD_DIM
    return bmask, lane_head


def _rwkv_prep_kernel(r_ref, v_ref, kk_ref, lw_ref, a_ref, kd_ref, r2_ref, yl_ref, t_ref, z_ref, *, cs):
    sgn = 1 - 2 * pl.program_id(0)
    row = lax.broadcasted_iota(jnp.int32, (cs, cs), 0)
    col = lax.broadcasted_iota(jnp.int32, (cs, cs), 1)
    tri = (((row - col) * sgn) >= 0).astype(F32)
    bmask, lane_head = _group_masks()
    t_idx = lax.broadcasted_iota(jnp.int32, (cs, GROUP_LANES), 0)
    s_idx = lax.broadcasted_iota(jnp.int32, (cs, GROUP_LANES), 1) % HEAD_DIM
    diff = (t_idx - s_idx) * sgn
    strict = diff > 0
    incl = diff >= 0
    eye = (diff == 0).astype(F32)
    base = min(RWKV_INV_BASE, cs)
    same_base = (t_idx // base) == (s_idx // base)
    groups = r_ref.shape[-1] // GROUP_LANES

    ch = []
    for c0 in range(0, r_ref.shape[1], cs):
        rows = slice(c0, c0 + cs)
        r, v, kk = r_ref[0, rows], v_ref[0, rows], kk_ref[0, rows]
        lw, a, kd = lw_ref[0, 0, rows], a_ref[0, 0, rows], kd_ref[0, 0, rows]
        g = _dot_hi(tri, lw)
        gp = g - lw
        gref = g[cs // 2:cs // 2 + 1]
        gend = jnp.sum(lw, axis=0, keepdims=True)
        bvec = kk * a
        full = {'a_t': -kk * jnp.exp(gp - gref), 'a_0': -kk * jnp.exp(gp),
                'b_t': bvec * jnp.exp(gref - g), 'b_e': bvec * jnp.exp(gend - g),
                'k_t': kd * jnp.exp(gref - g), 'k_e': kd * jnp.exp(gend - g),
                'r_t': r * jnp.exp(g - gref), 'r_0': r * jnp.exp(g), 'v': v,
                'wend': jnp.broadcast_to(jnp.exp(gend), (cs, v.shape[-1]))}
        for gi in range(groups):
            sl = slice(gi * GROUP_LANES, (gi + 1) * GROUP_LANES)
            c = {k: x[:, sl] for k, x in full.items()}
            c['rows'], c['sl'] = rows, sl
            ch.append(c)

    mm = lambda x, y, passes, nt=False: [_head_mm(p_, q_, bmask, passes, nt) for p_, q_ in zip(x, y)]
    get = lambda k: [c[k] for c in ch]
    ar = [jnp.concatenate([c['a_t'], c['r_t']], axis=0) for c in ch]
    gb = mm(ar, get('b_t'), RWKV_GRAM_PASSES, True)
    gk = mm(ar, get('k_t'), RWKV_GRAM_PASSES, True)
    nmat = [jnp.where(strict, x[0:cs], 0.0) for x in gb]
    g_ak = [jnp.where(strict, x[0:cs], 0.0) for x in gk]
    g_rb = [jnp.where(incl, x[cs:], 0.0) for x in gb]
    g_rk = [jnp.where(incl, x[cs:], 0.0) for x in gk]
    npow = [jnp.where(same_base, x, 0.0) for x in nmat]
    p = [eye + x for x in npow]
    for _ in range(int(math.log2(base)) - 1):
        npow = mm(npow, npow, RWKV_INV_PASSES)
        p = [x + y for x, y in zip(p, mm(npow, p, RWKV_INV_PASSES))]
    m = base
    while m < cs:
        off = ((t_idx // m) != (s_idx // m)) & ((t_idx // (2 * m)) == (s_idx // (2 * m)))
        q = mm([jnp.where(off, x, 0.0) for x in nmat], p, RWKV_INV_PASSES)
        p = [x + y for x, y in zip(p, mm(p, q, RWKV_INV_PASSES))]
        m *= 2
    vg = get('v')
    a2 = mm(p, get('a_0'), RWKV_REST_PASSES)
    u_v = mm(p, mm(g_ak, vg, RWKV_REST_PASSES), RWKV_REST_PASSES)
    r2 = mm(g_rb, a2, RWKV_REST_PASSES)
    yl1 = mm(g_rb, u_v, RWKV_REST_PASSES)
    yl2 = mm(g_rk, vg, RWKV_REST_PASSES)
    tt = [_head_mm_tn(x, c['b_e'], lane_head, RWKV_REST_PASSES) for x, c in zip(a2, ch)]
    zz = [_head_mm_tn(jnp.concatenate([u, c['v']], axis=0), jnp.concatenate([c['b_e'], c['k_e']], axis=0),
                      lane_head, RWKV_REST_PASSES) for u, c in zip(u_v, ch)]
    for i, c in enumerate(ch):
        rows, sl = c['rows'], c['sl']
        r2_ref[0, 0, rows, sl] = (c['r_0'] + r2[i]).astype(r2_ref.dtype)
        yl_ref[0, 0, rows, sl] = yl1[i] + yl2[i]
        t_ref[0, 0, rows, sl] = (eye * c['wend'] + tt[i]).astype(t_ref.dtype)
        z_ref[0, 0, rows, sl] = zz[i]


def _rwkv_scan_kernel(h0_ref, r2f_ref, ylf_ref, tf_ref, zf_ref, r2b_ref, ylb_ref, tb_ref, zb_ref,
                      yf_ref, yb_ref, ht_ref, s_scr):
    ci = pl.program_id(0)

    @pl.when(ci == 0)
    def _():
        s_scr[...] = h0_ref[...]

    bmask, _ = _group_masks()
    ins = ((r2f_ref, ylf_ref, tf_ref, zf_ref, yf_ref), (r2b_ref, ylb_ref, tb_ref, zb_ref, yb_ref))
    for d, (r2_ref, yl_ref, t_ref, z_ref, y_ref) in enumerate(ins):
        for b in range(s_scr.shape[1]):
            for gi in range(s_scr.shape[-1] // GROUP_LANES):
                sl = slice(gi * GROUP_LANES, (gi + 1) * GROUP_LANES)
                s = s_scr[d, b, :, sl]
                y_ref[0, b, :, sl] = yl_ref[0, b, :, sl] + _head_mm(r2_ref[0, b, :, sl], s, bmask, 1, nt=True)
                s_scr[d, b, :, sl] = (_head_mm(s, t_ref[0, b, :, sl], bmask, RWKV_SCAN_PASSES)
                                      + z_ref[0, b, :, sl])

    @pl.when(ci == pl.num_programs(0) - 1)
    def _():
        ht_ref[...] = s_scr[...]


def _rwkv(r, v, kk, lw, a, kd, h0):
    b, t, c = r.shape
    cs = min(RWKV_CHUNK, t)
    assert cs == HEAD_DIM and c % GROUP_LANES == 0
    nc = t // cs
    rows = cs * math.gcd(nc, RWKV_PREP_CHUNKS)
    tok = pl.BlockSpec((1, rows, c), lambda d, bi, ci: (bi, ci, 0))
    tok2 = pl.BlockSpec((1, 1, rows, c), lambda d, bi, ci: (d, bi, ci, 0))
    wide = jax.ShapeDtypeStruct((2, b, t, c), F32)
    mm_operand = jax.ShapeDtypeStruct((2, b, t, c), BF16 if RWKV_SCAN_PASSES == 1 else F32)
    r2, yl, tt, zz = pl.pallas_call(
        functools.partial(_rwkv_prep_kernel, cs=cs),
        grid=(2, b, t // rows),
        in_specs=[tok, tok, tok, tok2, tok2, tok2],
        out_specs=[tok2, tok2, tok2, tok2],
        out_shape=[mm_operand, wide, mm_operand, wide],
        compiler_params=_cparams("parallel", "parallel", "parallel"),
        name="rwkv7_chunk_prep",
    )(r, v, kk, lw, a, kd)

    fwd = pl.BlockSpec((1, b, cs, c), lambda ci: (0, 0, ci, 0))
    bwd = pl.BlockSpec((1, b, cs, c), lambda ci: (1, 0, nc - 1 - ci, 0))
    st = pl.BlockSpec((2, b, HEAD_DIM, c), lambda ci: (0, 0, 0, 0))
    yf, yb, ht = pl.pallas_call(
        _rwkv_scan_kernel,
        grid=(nc,),
        in_specs=[st, fwd, fwd, fwd, fwd, bwd, bwd, bwd, bwd],
        out_specs=[pl.BlockSpec((1, b, cs, c), lambda ci: (0, 0, ci, 0)),
                   pl.BlockSpec((1, b, cs, c), lambda ci: (0, 0, nc - 1 - ci, 0)), st],
        out_shape=[jax.ShapeDtypeStruct((1, b, t, c), F32), jax.ShapeDtypeStruct((1, b, t, c), F32),
                   jax.ShapeDtypeStruct((2, b, HEAD_DIM, c), F32)],
        scratch_shapes=[pltpu.VMEM((2, b, HEAD_DIM, c), F32)],
        compiler_params=_cparams("arbitrary"),
        name="rwkv7_state_scan",
    )(h0, r2, yl, tt, zz, r2, yl, tt, zz)
    return (yf.reshape(b, t, c), yb.reshape(b, t, c)), ht


def _attn_kernel(sink_ref, q_ref, kp_ref, kc_ref, kn_ref, vp_ref, vc_ref, vn_ref, kx_ref, vx_ref, o_ref,
                 *, local, group):
    i = pl.program_id(1)
    nb = pl.num_programs(1)
    scale = HEAD_DIM ** -0.5
    q = q_ref[0]
    if local:
        k_all = jnp.concatenate([kp_ref[0], kc_ref[0], kn_ref[0], kx_ref[0]], axis=0)
        v_all = jnp.concatenate([vp_ref[0], vc_ref[0], vn_ref[0], vx_ref[0]], axis=0)
        nkeys = k_all.shape[0]
        qi = lax.broadcasted_iota(jnp.int32, (BLOCK, nkeys), 0)
        kj = lax.broadcasted_iota(jnp.int32, (BLOCK, nkeys), 1)
        kpos = kj + (i - 1) * BLOCK
        valid = (jnp.abs(kj - BLOCK - qi) <= WINDOW) & (kpos >= 0) & (kpos < nb * BLOCK)
        valid = valid | (kj >= 3 * BLOCK)
    else:
        k_all = kx_ref[0]
        v_all = vx_ref[0]
    heads = range(q.shape[-1] // HEAD_DIM)
    kv = lambda x, h: x[:, (h // group) * HEAD_DIM:(h // group + 1) * HEAD_DIM]
    s = [_dot_nt(q[:, h * HEAD_DIM:(h + 1) * HEAD_DIM], kv(k_all, h)) * scale for h in heads]
    if local:
        s = [jnp.where(valid, x, MASK_VALUE) for x in s]
    sink = [sink_ref[h] for h in heads]
    m = [jnp.maximum(jnp.max(x, axis=-1, keepdims=True), sk) for x, sk in zip(s, sink)]
    e = [jnp.exp(x - y) for x, y in zip(s, m)]
    den = [jnp.sum(x, axis=-1, keepdims=True) + jnp.exp(sk - y) for x, y, sk in zip(e, m, sink)]
    outs = [_dot(x, kv(v_all, h)) / dn for x, h, dn in zip(e, heads, den)]
    o_ref[0] = jnp.concatenate(outs, axis=1)


def _attention(q, k, v, kx, vx, sink, local):
    b, t, nq = q.shape
    nk = k.shape[-1]
    nb = t // BLOCK
    group = (nq // HEAD_DIM) // (nk // HEAD_DIM)
    lx = kx.shape[1]
    kv = lambda f: pl.BlockSpec((1, BLOCK, nk), f)
    pf = lambda bi, i: (bi, jnp.maximum(i - 1, 0), 0)
    cf = lambda bi, i: (bi, i, 0)
    nf = lambda bi, i: (bi, jnp.minimum(i + 1, nb - 1), 0)
    ctx = pl.BlockSpec((1, lx, nk), lambda bi, i: (bi, 0, 0))
    return pl.pallas_call(
        functools.partial(_attn_kernel, local=local, group=group),
        grid=(b, nb),
        in_specs=[pl.BlockSpec(memory_space=pltpu.SMEM),
                  pl.BlockSpec((1, BLOCK, nq), cf), kv(pf), kv(cf), kv(nf), kv(pf), kv(cf), kv(nf), ctx, ctx],
        out_specs=pl.BlockSpec((1, BLOCK, nq), cf),
        out_shape=jax.ShapeDtypeStruct((b, t, nq), F32),
        compiler_params=_cparams("parallel", "parallel"),
        name="window_attention" if local else "context_attention",
    )(sink, q, k, k, k, v, v, v, kx, vx)


def _even_out_kernel(yf_ref, yb_ref, r_ref, v_ref, g_ref, kd_ref, batt_ref, x_ref, mod_ref, lnw_ref, lnb_ref, rk_ref, e_ref,
                     w_ref, o_ref, *, d, c):
    y = yf_ref[0] + yb_ref[0]
    inv = 1.0 / HEAD_DIM
    mu = _segsum(y, e_ref) * inv
    yc = y - mu
    var = _segsum(yc * yc, e_ref) * inv
    yn = yc * lax.rsqrt(var + RWKV_GN_EPS) * lnw_ref[...] + lnb_ref[...]
    bonus = _segsum(r_ref[0] * (kd_ref[0, 0] + kd_ref[1, 0]) * rk_ref[...], e_ref)
    a_out = (yn + bonus * v_ref[0]) * g_ref[0]
    o = _dot(a_out, w_ref[0:c]) + _dot(batt_ref[0], w_ref[c:])
    o_ref[0] = x_ref[0] + mod_ref[0][:, 2 * d:3 * d] * o


def _even_out(y, r, v, g, kd, batt, x, mod, ep, tabs):
    b, t, d = x.shape
    c = r.shape[-1]
    nq = batt.shape[-1]
    tm = _row_tile(t, 512)
    tok = lambda w: pl.BlockSpec((1, tm, w), lambda bi, i: (bi, i, 0))
    tok2 = lambda w: pl.BlockSpec((2, 1, tm, w), lambda bi, i: (0, bi, i, 0))
    full = lambda *shape: pl.BlockSpec(shape, lambda bi, i: (0,) * len(shape))
    return pl.pallas_call(
        functools.partial(_even_out_kernel, d=d, c=c),
        grid=(b, t // tm),
        in_specs=[tok(c), tok(c), tok(c), tok(c), tok(c), tok2(c), tok(nq), tok(d),
                  pl.BlockSpec((1, 1, mod.shape[-1]), lambda bi, i: (bi, 0, 0)),
                  full(1, c), full(1, c), full(1, c), full(c, c), full(c + nq, d)],
        out_specs=tok(d),
        out_shape=jax.ShapeDtypeStruct((b, t, d), F32),
        compiler_params=_cparams("parallel", "parallel"),
        name="even_out_proj",
    )(y[0], y[1], r, v, g, kd, batt, x, mod, ep['ln_w'], ep['ln_b'], ep['r_k'], tabs['e'], ep['w_out'])


def _filter_kernel(z_ref, t_ref, w1_ref, b1_ref, w2_ref, b2_ref, w3_ref, b3_ref, fr_ref, wo_ref, dl_ref, o_ref, *, d):
    fr = fr_ref[...]
    h = jnp.sin(fr * (_dot_hi(z_ref[...], w1_ref[...]) + b1_ref[...]))
    h = jnp.sin(fr * (_dot_hi(h, w2_ref[...]) + b2_ref[...]))
    h = jnp.sin(fr * (_dot_hi(h, w3_ref[...]) + b3_ref[...]))
    filt = _dot_hi(h, wo_ref[...])
    modu = jnp.exp(-t_ref[...] * dl_ref[...]) + HY_MOD_SHIFT
    for q in range(o_ref.shape[0]):
        o_ref[q] = filt[:, q * d:(q + 1) * d] * modu


def _hyena_filters(n, op, d):
    t = np.linspace(0.0, 1.0, n, dtype=np.float32)[:, None]
    ang = (2.0 * math.pi * np.arange(n, dtype=np.float32)[:, None] / np.float32(n)).astype(np.float32)
    f = np.linspace(1e-4, HY_BANDS - 1, HY_BANDS, dtype=np.float32)[None, :]
    zfeat = jnp.concatenate([jnp.asarray(t), jnp.cos(jnp.asarray(f * ang)), -jnp.sin(jnp.asarray(f * ang))], axis=-1)
    emb_pad = op['f_w1'].shape[0]
    zfeat = jnp.pad(zfeat, ((0, 0), (0, emb_pad - HY_EMB)))
    deltas = np.abs(np.linspace(math.log(HY_TARGET) / HY_SLOW_PCT, math.log(HY_TARGET) / HY_FAST_PCT, d,
                                dtype=np.float32))[None, :]
    tn = min(n, 256)
    nq = 2 * HY_ORDER
    hf = op['f_w2'].shape[0]
    full = lambda *shape: pl.BlockSpec(shape, lambda i: (0,) * len(shape))
    return pl.pallas_call(
        functools.partial(_filter_kernel, d=d),
        grid=(n // tn,),
        in_specs=[pl.BlockSpec((tn, emb_pad), lambda i: (i, 0)), pl.BlockSpec((tn, 1), lambda i: (i, 0)),
                  full(emb_pad, hf), full(1, hf), full(hf, hf), full(1, hf), full(hf, hf), full(1, hf), full(1, hf),
                  full(hf, nq * d), full(1, d)],
        out_specs=pl.BlockSpec((nq, tn, d), lambda i: (0, i, 0)),
        out_shape=jax.ShapeDtypeStruct((nq, n, d), F32),
        compiler_params=_cparams("parallel"),
        name="hyena_filter",
    )(zfeat, jnp.asarray(t), op['f_w1'], op['f_b1'], op['f_w2'], op['f_b2'], op['f_w3'], op['f_b3'], op['f_freq'],
      op['f_out'], jnp.asarray(deltas))


def _dft(n, rows, cols, sign=-1.0):
    k = np.arange(rows, dtype=np.float64)[:, None]
    m = np.arange(cols, dtype=np.float64)[None, :]
    ang = sign * 2.0 * np.pi * ((k * m) % n) / n
    return np.cos(ang), np.sin(ang)


def _stack(re, im):
    return jnp.asarray(np.concatenate([re, im], axis=0).astype(np.float32)).astype(BF16)


def _fft_tables(n_seq):
    n = 2 * n_seq
    n1 = n // FFT_N2
    f1r, f1i = _dft(n1, n1, n1 // 2)
    h1r, h1i = _dft(n1, n1 // 2, n1, sign=1.0)
    k1 = np.arange(n1, dtype=np.float64)[:, None, None]
    k2 = np.arange(FFT_N2, dtype=np.float64)[None, :, None]
    j2 = np.arange(FFT_N2, dtype=np.float64)[None, None, :]
    ang = -2.0 * np.pi * (((k2 * j2 * n1) + k1 * j2) % n) / n
    gr, gi = np.cos(ang), np.sin(ang)
    g_fwd = np.concatenate([gr, gi], axis=1)
    g_inv = np.concatenate([np.swapaxes(gr, 1, 2), np.swapaxes(gi, 1, 2)], axis=1)
    eye = np.eye(SUBLANES)
    kron = lambda m: np.kron(m, eye)
    return {'f1': _stack(kron(f1r), kron(f1i)), 'h1': _stack(kron(h1r), kron(h1i)),
            'g_fwd': jnp.asarray(g_fwd.astype(np.float32)).astype(BF16),
            'g_inv': jnp.asarray(g_inv.astype(np.float32)).astype(BF16), 'n1': n1}


def _dense_tables(n_seq):
    n = 2 * n_seq
    fr, fi = _dft(n, n, n_seq)
    hr, hi = _dft(n, n_seq, n, sign=1.0)
    return {'f': _stack(fr, fi), 'h': _stack(hr, hi)}


def _fft_a_kernel(f_ref, zr_ref, zi_ref, ar_ref, ai_ref, *, n1, cplx):
    f = f_ref[...]
    half, tj, d = zr_ref.shape[1:]
    m = n1 * SUBLANES
    res_r, res_i = [], []
    for s0 in range(0, tj, SUBLANES):
        js = slice(s0, s0 + SUBLANES)
        zr = zr_ref[0, :, js, :].reshape(half * SUBLANES, d)
        p = jnp.dot(f, zr.astype(BF16), preferred_element_type=F32)
        if cplx:
            zi = zi_ref[0, :, js, :].reshape(half * SUBLANES, d)
            q = jnp.dot(f, zi.astype(BF16), preferred_element_type=F32)
            res_r.append((p[0:m] - q[m:]).reshape(n1, SUBLANES, d))
            res_i.append((p[m:] + q[0:m]).reshape(n1, SUBLANES, d))
        else:
            res_r.append(p[0:m].reshape(n1, SUBLANES, d))
            res_i.append(p[m:].reshape(n1, SUBLANES, d))
    ar_ref[0] = jnp.concatenate(res_r, axis=1).astype(BF16)
    ai_ref[0] = jnp.concatenate(res_i, axis=1).astype(BF16)


def _fft_a(u5, q, tabs, cplx):
    _, s, half, n2, d = u5.shape
    n1 = tabs['n1']
    assert half == n1 // 2 and n2 == FFT_N2
    sp = s // 2 if cplx else s
    tj = 2 * SUBLANES
    zi_map = (lambda si, j: (q, si + sp, 0, j, 0)) if cplx else (lambda si, j: (q, si, 0, j, 0))
    out = pl.BlockSpec((1, n1, tj, d), lambda si, j: (si, 0, j, 0))
    return pl.pallas_call(
        functools.partial(_fft_a_kernel, n1=n1, cplx=cplx),
        grid=(sp, n2 // tj),
        in_specs=[pl.BlockSpec((2 * n1 * SUBLANES, half * SUBLANES), lambda si, j: (0, 0)),
                  pl.BlockSpec((None, 1, half, tj, d), lambda si, j: (q, si, 0, j, 0)),
                  pl.BlockSpec((None, 1, half, tj, d), zi_map)],
        out_specs=[out, out],
        out_shape=[jax.ShapeDtypeStruct((sp, n1, n2, d), BF16)] * 2,
        compiler_params=_cparams("parallel", "parallel"),
        name="fft_stage1",
    )(tabs['f1'], u5, u5)


def _cplx_mm(s, xr, xi, conj):
    p = jnp.dot(s, xr.astype(BF16), preferred_element_type=F32)
    q = jnp.dot(s, xi.astype(BF16), preferred_element_type=F32)
    m = s.shape[0] // 2
    if conj:
        return p[0:m] + q[m:], q[0:m] - p[m:]
    return p[0:m] - q[m:], p[m:] + q[0:m]


def _fft_b_kernel(gf_ref, gi_ref, ar_ref, ai_ref, kr_ref, ki_ref, dr_ref, di_ref):
    kr = kr_ref[0]
    ki = ki_ref[0]
    seqs = range(ar_ref.shape[0])
    c = [_cplx_mm(gf_ref[0], ar_ref[s, 0], ai_ref[s, 0], False) for s in seqs]
    e = [(cr * kr - ci * ki, cr * ki + ci * kr) for cr, ci in c]
    dd = [_cplx_mm(gi_ref[0], er, ei, True) for er, ei in e]
    for s, (dr, di) in zip(seqs, dd):
        dr_ref[s, 0] = dr.astype(BF16)
        di_ref[s, 0] = di.astype(BF16)


def _fft_b(ar4, ai4, kr, ki, order, tabs):
    sp, n1, _, d = ar4.shape
    ns = 2 if sp % 2 == 0 else 1
    blk = pl.BlockSpec((ns, 1, FFT_N2, d), lambda k1, si: (si, k1, 0, 0))
    tab = pl.BlockSpec((1, 2 * FFT_N2, FFT_N2), lambda k1, si: (k1, 0, 0))
    kb = pl.BlockSpec((None, 1, FFT_N2, d), lambda k1, si: (order, k1, 0, 0))
    return pl.pallas_call(
        _fft_b_kernel,
        grid=(n1, sp // ns),
        in_specs=[tab, tab, blk, blk, kb, kb],
        out_specs=[blk, blk],
        out_shape=[jax.ShapeDtypeStruct((sp, n1, FFT_N2, d), BF16)] * 2,
        compiler_params=_cparams("parallel", "arbitrary"),
        name="fft_stage2_filter",
    )(tabs['g_fwd'], tabs['g_inv'], ar4, ai4, kr, ki)


def _fft_c_kernel(h_ref, dr_ref, di_ref, u_ref, x_ref, sk_ref, o_ref):
    h = h_ref[...]
    n1, tj, d = dr_ref.shape[1:]
    half = n1 // 2
    sk = sk_ref[...]
    dr_all = dr_ref[0].astype(F32)
    di_all = di_ref[0].astype(F32)
    for s0 in range(0, tj, SUBLANES):
        js = slice(s0, s0 + SUBLANES)
        yr, yi = _cplx_mm(h, dr_all[:, js, :].reshape(n1 * SUBLANES, d),
                          di_all[:, js, :].reshape(n1 * SUBLANES, d), False)
        for part, y in enumerate((yr, yi)):
            u = u_ref[part, 0, :, js, :]
            o_ref[part, 0, :, js, :] = x_ref[part, 0, :, js, :] * (y.reshape(half, SUBLANES, d) + u * sk)


def _fft_c(dr, di, u6, uq, x6, xq, skip, tabs):
    sp, n1, n2, d = dr.shape
    half = n1 // 2
    tj = 2 * SUBLANES
    pair = lambda q: pl.BlockSpec((None, 2, 1, half, tj, d), lambda si, j: (q, 0, si, 0, j, 0))
    dblk = pl.BlockSpec((1, n1, tj, d), lambda si, j: (si, 0, j, 0))
    out = pl.pallas_call(
        _fft_c_kernel,
        grid=(sp, n2 // tj),
        in_specs=[pl.BlockSpec((2 * half * SUBLANES, n1 * SUBLANES), lambda si, j: (0, 0)), dblk, dblk,
                  pair(uq), pair(xq), pl.BlockSpec((1, d), lambda si, j: (0, 0))],
        out_specs=pl.BlockSpec((2, 1, half, tj, d), lambda si, j: (0, si, 0, j, 0)),
        out_shape=jax.ShapeDtypeStruct((2, sp, half, n2, d), F32),
        compiler_params=_cparams("parallel", "parallel"),
        name="fft_inverse_stage1_gate",
    )(tabs['h1'], dr, di, u6, x6, skip.reshape(1, d))
    return out


def _spec_b_kernel(gf_ref, ar_ref, ai_ref, kr_ref, ki_ref, *, scale):
    fr, fi = _cplx_mm(gf_ref[0], ar_ref[0, 0], ai_ref[0, 0], False)
    gr, gi = _cplx_mm(gf_ref[0], ar_ref[1, 0], ai_ref[1, 0], False)
    kr_ref[0, 0] = (fr + gr) * scale
    ki_ref[0, 0] = (fi - gi) * scale


def _filter_spectrum_fft(filt, tabs, d):
    nq, n, _ = filt.shape
    n1 = tabs['n1']
    ar, ai = _fft_a(filt.reshape(1, nq, n1 // 2, FFT_N2, d), 0, tabs, cplx=False)
    orders = nq // 2
    ar5 = ar.reshape(orders, 2, n1, FFT_N2, d)
    ai5 = ai.reshape(orders, 2, n1, FFT_N2, d)
    blk = pl.BlockSpec((None, 2, 1, FFT_N2, d), lambda o, k1: (o, 0, k1, 0, 0))
    out = pl.BlockSpec((1, 1, FFT_N2, d), lambda o, k1: (o, k1, 0, 0))
    return pl.pallas_call(
        functools.partial(_spec_b_kernel, scale=1.0 / (2 * n)),
        grid=(orders, n1),
        in_specs=[pl.BlockSpec((1, 2 * FFT_N2, FFT_N2), lambda o, k1: (k1, 0, 0)), blk, blk],
        out_specs=[out, out],
        out_shape=[jax.ShapeDtypeStruct((orders, n1, FFT_N2, d), F32)] * 2,
        compiler_params=_cparams("parallel", "parallel"),
        name="filter_spectrum",
    )(tabs['g_fwd'], ar5, ai5)


def _long_conv_fft(u, uq, x, xq, skip, kr, ki, order, tabs):
    _, b, n, d = u.shape
    half = tabs['n1'] // 2
    ar, ai = _fft_a(u.reshape(u.shape[0], b, half, FFT_N2, d), uq, tabs, cplx=True)
    dr, di = _fft_b(ar, ai, kr, ki, order, tabs)
    six = lambda a: a.reshape(a.shape[0], 2, b // 2, half, FFT_N2, d)
    return _fft_c(dr, di, six(u), uq, six(x), xq, skip, tabs).reshape(b, n, d)


def _dense_spec_kernel(f_ref, filt_ref, kr_ref, ki_ref, *, scale):
    f = f_ref[...]
    n = f.shape[0] // 2
    pf = jnp.dot(f, filt_ref[0, 0].astype(BF16), preferred_element_type=F32)
    pg = jnp.dot(f, filt_ref[0, 1].astype(BF16), preferred_element_type=F32)
    kr_ref[0] = (pf[0:n] + pg[0:n]) * scale
    ki_ref[0] = (pf[n:] - pg[n:]) * scale


def _filter_spectrum_dense(filt, tabs, d):
    nq, n, _ = filt.shape
    orders = nq // 2
    f4 = filt.reshape(orders, 2, n, d)
    out = pl.BlockSpec((1, 2 * n, d), lambda o: (o, 0, 0))
    return pl.pallas_call(
        functools.partial(_dense_spec_kernel, scale=1.0 / (2 * n)),
        grid=(orders,),
        in_specs=[pl.BlockSpec((4 * n, n), lambda o: (0, 0)), pl.BlockSpec((1, 2, n, d), lambda o: (o, 0, 0, 0))],
        out_specs=[out, out],
        out_shape=[jax.ShapeDtypeStruct((orders, 2 * n, d), F32)] * 2,
        compiler_params=_cparams("parallel"),
        name="filter_spectrum_dense",
    )(tabs['f'], f4)


def _dense_conv_kernel(f_ref, h_ref, u_ref, x_ref, kr_ref, ki_ref, sk_ref, o_ref):
    cr, ci = _cplx_mm(f_ref[...], u_ref[0, 0], u_ref[1, 0], False)
    kr = kr_ref[...]
    ki = ki_ref[...]
    yr, yi = _cplx_mm(h_ref[...], cr * kr - ci * ki, cr * ki + ci * kr, False)
    sk = sk_ref[...]
    o_ref[0, 0] = x_ref[0, 0] * (yr + u_ref[0, 0] * sk)
    o_ref[1, 0] = x_ref[1, 0] * (yi + u_ref[1, 0] * sk)


def _long_conv_dense(u, x, skip, kr, ki, tabs):
    b, n, d = u.shape
    sp = b // 2
    pair = pl.BlockSpec((2, 1, n, d), lambda si: (0, si, 0, 0))
    kb = pl.BlockSpec((2 * n, d), lambda si: (0, 0))
    out = pl.pallas_call(
        _dense_conv_kernel,
        grid=(sp,),
        in_specs=[pl.BlockSpec((4 * n, n), lambda si: (0, 0)), pl.BlockSpec((2 * n, 2 * n), lambda si: (0, 0)),
                  pair, pair, kb, kb, pl.BlockSpec((1, d), lambda si: (0, 0))],
        out_specs=pair,
        out_shape=jax.ShapeDtypeStruct((2, sp, n, d), F32),
        compiler_params=_cparams("parallel"),
        name="long_conv_dense",
    )(tabs['f'], tabs['h'], u.reshape(2, sp, n, d), x.reshape(2, sp, n, d), kr, ki, skip.reshape(1, d))
    return out.reshape(b, n, d)


def _hyena_mixer(x, g, mod, op, fft_tabs, dense_tabs):
    b, n, d = x.shape
    z = _proj_conv(x, g, mod, op['w_in'], op['b_in'], op['conv_w'], op['conv_b'])
    filt = _hyena_filters(n, op, d)
    if n <= DENSE_FFT_MAX:
        kr, ki = _filter_spectrum_dense(filt, dense_tabs, d)
        y = _long_conv_dense(z[0], z[1], op['skip'][0], kr[0], ki[0], dense_tabs)
        y = _long_conv_dense(y, z[2], op['skip'][1], kr[1], ki[1], dense_tabs)
    else:
        kr, ki = _filter_spectrum_fft(filt, fft_tabs, d)
        y = _long_conv_fft(z, 0, z, 1, op['skip'][0], kr, ki, 0, fft_tabs)
        y = _long_conv_fft(y[None], 0, z, 2, op['skip'][1], kr, ki, 1, fft_tabs)
    return _out_res(y, x, mod, op['w_out'], op['b_out'])


def _even_mixer(x, ctx, g, mod_l, mod_c, ep, tabs, need_ctx):
    rc, vc, kkc, gc, lwc, ac, kdc, qc, kac, vac = _even_prep(ctx, g, mod_c, ep, tabs, rope=False)
    rl, vl, kkl, gl, lwl, al, kdl, ql, kal, val = _even_prep(x, g, mod_l, ep, tabs, rope=True)
    b = x.shape[0]
    c = rl.shape[-1]
    h0 = jnp.zeros((2, b, HEAD_DIM, c), F32)
    y_ctx, s_ctx = _rwkv(rc, vc, kkc, lwc, ac, kdc, h0)
    y_lat, _ = _rwkv(rl, vl, kkl, lwl, al, kdl, s_ctx)
    b_lat = _attention(ql, kal, val, kac, vac, ep['sink'], local=True)
    x_new = _even_out(y_lat, rl, vl, gl, kdl, b_lat, x, mod_l, ep, tabs)
    if not need_ctx:
        return x_new, None
    b_ctx = _attention(qc, kac, vac, kac, vac, ep['sink'], local=False)
    ctx_new = _even_out(y_ctx, rc, vc, gc, kdc, b_ctx, ctx, mod_c, ep, tabs)
    return x_new, ctx_new


def _rope_tables(n_tokens):
    rows = n_tokens // GRID_W
    row = jnp.repeat(jnp.arange(rows), GRID_W).astype(F32)
    col = jnp.tile(jnp.arange(GRID_W), rows).astype(F32)
    n_freq = HEAD_DIM // 4
    inv = ROPE_THETA ** (-jnp.arange(n_freq, dtype=F32) / n_freq)
    ang = jnp.concatenate([row[:, None] * inv, col[:, None] * inv], axis=-1)
    cos, sin = jnp.cos(ang), jnp.sin(ang)
    reps = LANES // HEAD_DIM
    cos_t = jnp.tile(jnp.concatenate([cos, cos], axis=-1), (1, reps))
    sin_t = jnp.tile(jnp.concatenate([-sin, sin], axis=-1), (1, reps))
    return cos_t, sin_t


def _block_ones(width):
    idx = np.arange(width) // HEAD_DIM
    return jnp.asarray((idx[:, None] == idx[None, :]).astype(np.float32)).astype(BF16)


def _lora_pad(w):
    z = jnp.zeros_like(w[0])
    return jnp.stack([jnp.concatenate([w[0], z], axis=0), jnp.concatenate([z, w[1]], axis=0)], axis=0)


def kernel(x, c, ctx, c_ctx, ada_w, ada_b, norm1_g, norm2_g, ffn_up, ffn_conv_w, ffn_conv_b, ffn_down, ev_w_in, ev_mu_prev, ev_mu_next, ev_w0, ev_w2, ev_a0, ev_a2, ev_g2, ev_k_k, ev_k_a, ev_r_k, ev_ln_w, ev_ln_b, ev_q_norm, ev_k_norm, ev_sink, ev_w_out, od_w_in, od_b_in, od_conv_w, od_conv_b, od_f_w1, od_f_b1, od_f_w2, od_f_b2, od_f_w3, od_f_b3, od_f_freq, od_f_out, od_skip, od_w_out, od_b_out):
    bsz, seq, d = x.shape
    lc = ctx.shape[1]
    depth = ada_w.shape[0]
    a_width = ev_k_k.shape[-1]
    nq = ev_sink.shape[-1] * HEAD_DIM
    nk = B_KV_HEADS * HEAD_DIM

    cond = jnp.zeros((BF16_ROWS, d), F32).at[:bsz].set(c).at[bsz].set(c_ctx)
    mod = _ada_mod(cond, ada_w, ada_b)

    cos_t, sin_t = _rope_tables(seq)
    tabs = {'cos': cos_t, 'sin': sin_t, 'e': _block_ones(nq)}
    fft_tabs = _fft_tables(seq) if seq > DENSE_FFT_MAX else None
    dense_lat = _dense_tables(seq) if seq <= DENSE_FFT_MAX else None
    fft_ctx = _fft_tables(lc) if lc > DENSE_FFT_MAX else None
    dense_ctx = _dense_tables(lc) if lc <= DENSE_FFT_MAX else None

    for layer in range(depth):
        need_ctx = layer < depth - 1
        even = layer % 2 == 0
        j = layer // 2
        mod_l = mod[layer, :bsz].reshape(bsz, 1, 6 * d)
        mod_c = jnp.broadcast_to(mod[layer, bsz].reshape(1, 1, 6 * d), (bsz, 1, 6 * d))
        if even:
            ep = {'w_in': ev_w_in[j].astype(BF16), 'mu_prev': ev_mu_prev[j][None], 'mu_next': ev_mu_next[j][None],
                  'w0': ev_w0[j], 'w2pad': _lora_pad(ev_w2[j]).astype(BF16), 'a0': ev_a0[j],
                  'a2pad': _lora_pad(ev_a2[j]).astype(BF16), 'g2': ev_g2[j].astype(BF16),
                  'k_k': ev_k_k[j][None], 'k_a': ev_k_a[j][None], 'r_k': ev_r_k[j].reshape(1, a_width),
                  'ln_w': ev_ln_w[j][None], 'ln_b': ev_ln_b[j][None],
                  'q_norm_t': jnp.tile(ev_q_norm[j], nq // HEAD_DIM)[None],
                  'k_norm_t': jnp.tile(ev_k_norm[j], nk // HEAD_DIM)[None],
                  'sink': ev_sink[j], 'w_out': ev_w_out[j].astype(BF16), 'nq': nq, 'nk': nk}
            x, ctx_new = _even_mixer(x, ctx, norm1_g[layer], mod_l, mod_c, ep, tabs, need_ctx)
        else:
            emb_pad = HEAD_DIM
            op = {'w_in': od_w_in[j].astype(BF16), 'b_in': od_b_in[j], 'conv_w': od_conv_w[j],
                  'conv_b': od_conv_b[j],
                  'f_w1': jnp.pad(od_f_w1[j], ((0, emb_pad - HY_EMB), (0, 0))), 'f_b1': od_f_b1[j][None],
                  'f_w2': od_f_w2[j], 'f_b2': od_f_b2[j][None], 'f_w3': od_f_w3[j], 'f_b3': od_f_b3[j][None],
                  'f_freq': od_f_freq[j][None], 'f_out': od_f_out[j], 'skip': od_skip[j],
                  'w_out': od_w_out[j].astype(BF16), 'b_out': od_b_out[j]}
            ctx_new = _hyena_mixer(ctx, norm1_g[layer], mod_c, op, fft_ctx, dense_ctx) if need_ctx else None
            x = _hyena_mixer(x, norm1_g[layer], mod_l, op, fft_tabs, dense_lat)
        w_up = ffn_up[layer].astype(BF16)
        w_down = ffn_down[layer].astype(BF16)
        x = _conv_ffn(x, norm2_g[layer], mod_l, w_up, ffn_conv_w[layer], ffn_conv_b[layer], w_down)
        if need_ctx:
            ctx = _conv_ffn(ctx_new, norm2_g[layer], mod_c, w_up, ffn_conv_w[layer], ffn_conv_b[layer], w_down)
    return x
```

```python
import functools
import math

import numpy as np
import jax
import jax.numpy as jnp
from jax import lax
from jax.experimental import pallas as pl
from jax.experimental.pallas import tpu as pltpu

F32 = jnp.float32
BF16 = jnp.bfloat16

HEAD_DIM = 64
GRID_W = 64
DECAY_LORA = 64
ICLR_LORA = 64
GATE_LORA = 128
RWKV_GN_EPS = 64e-5
B_KV_HEADS = 2
WINDOW = 128
BLOCK = 128
ROPE_THETA = 10000.0
MASK_VALUE = -1e30
HY_ORDER = 2
HY_EMB = 33
HY_BANDS = (HY_EMB - 1) // 2
HY_TARGET = 1e-2
HY_FAST_PCT = 0.3
HY_SLOW_PCT = 1.5
HY_MOD_SHIFT = 0.05
NORM_EPS = 1e-6

V7X_VMEM_BYTES = 64 * 1024 * 1024
VMEM_LIMIT_BYTES = V7X_VMEM_BYTES * 3 // 4
LANES = 128
SUBLANES = 8
BF16_ROWS = 16
FFN_ROW_PARTS = 2
EVEN_OUT_PARTS = 4
RWKV_CHUNK = 64
RWKV_PREP_CHUNKS = 4
RWKV_INV_BASE = 8
RWKV_GRAM_PASSES = 1
RWKV_INV_PASSES = 1
RWKV_REST_PASSES = 1
RWKV_SCAN_PASSES = 1
FFT_N2 = 128
DENSE_FFT_MAX = 512

HIGHEST = lax.Precision.HIGHEST


def _cparams(*sem):
    return pltpu.CompilerParams(dimension_semantics=sem, vmem_limit_bytes=VMEM_LIMIT_BYTES)


def _dot(a, b):
    return jnp.dot(a.astype(BF16), b.astype(BF16), preferred_element_type=F32)


def _dot_nt(a, b):
    return lax.dot_general(a.astype(BF16), b.astype(BF16), (((1,), (1,)), ((), ())), preferred_element_type=F32)


def _dot_tn(a, b):
    return lax.dot_general(a.astype(BF16), b.astype(BF16), (((0,), (0,)), ((), ())), preferred_element_type=F32)


def _dot_hi(a, b):
    return jnp.dot(a, b, preferred_element_type=F32, precision=HIGHEST)


def _dot_nt_hi(a, b):
    return lax.dot_general(a, b, (((1,), (1,)), ((), ())), preferred_element_type=F32, precision=HIGHEST)


def _dot_tn_hi(a, b):
    return lax.dot_general(a, b, (((0,), (0,)), ((), ())), preferred_element_type=F32, precision=HIGHEST)


def _segsum(x, e_ref):
    hi = x.astype(BF16)
    lo = (x - hi.astype(F32)).astype(BF16)
    e = e_ref[...]
    return jnp.dot(hi, e, preferred_element_type=F32) + jnp.dot(lo, e, preferred_element_type=F32)


def _norm_mod(x, g, shift, scale):
    xn = x * lax.rsqrt(jnp.mean(x * x, axis=-1, keepdims=True) + NORM_EPS)
    return (xn * g) * (1.0 + scale) + shift


def _row_tile(t, want):
    tm = min(t, want)
    assert t % tm == 0 and tm % BF16_ROWS == 0
    return tm


def _ada_kernel(c_ref, w_ref, b_ref, o_ref):
    c = c_ref[...]
    s = c * jax.nn.sigmoid(c)
    o_ref[0] = _dot(s, w_ref[0]) + b_ref[0]


def _ada_mod(cond, ada_w, ada_b):
    depth, d, n = ada_w.shape
    tn = n // 4
    rows = cond.shape[0]
    return pl.pallas_call(
        _ada_kernel,
        grid=(depth, n // tn),
        in_specs=[pl.BlockSpec((rows, d), lambda l, j: (0, 0)),
                  pl.BlockSpec((1, d, tn), lambda l, j: (l, 0, j)),
                  pl.BlockSpec((1, 1, tn), lambda l, j: (l, 0, j))],
        out_specs=pl.BlockSpec((1, rows, tn), lambda l, j: (l, 0, j)),
        out_shape=jax.ShapeDtypeStruct((depth, rows, n), F32),
        compiler_params=_cparams("parallel", "parallel"),
        name="ada_mod",
    )(cond, ada_w, ada_b.reshape(depth, 1, n))


def _halo_specs(tm, t, d, nargs):
    r = tm // BF16_ROWS
    last = t // BF16_ROWS - 1
    if nargs == 3:
        prev = pl.BlockSpec((1, BF16_ROWS, d), lambda bi, i, j: (bi, jnp.maximum(i * r - 1, 0), 0))
        nxt = pl.BlockSpec((1, BF16_ROWS, d), lambda bi, i, j: (bi, jnp.minimum((i + 1) * r, last), 0))
    else:
        prev = pl.BlockSpec((1, BF16_ROWS, d), lambda bi, i: (bi, jnp.maximum(i * r - 1, 0), 0))
        nxt = pl.BlockSpec((1, BF16_ROWS, d), lambda bi, i: (bi, jnp.minimum((i + 1) * r, last), 0))
    return prev, nxt


def _fill_h(x_ref, xp_ref, xn_ref, g, shift, scale, h_scr, tm):
    h_scr[0:BF16_ROWS] = _norm_mod(xp_ref[0], g, shift, scale).astype(BF16)
    h_scr[BF16_ROWS:BF16_ROWS + tm] = _norm_mod(x_ref[0], g, shift, scale).astype(BF16)
    h_scr[BF16_ROWS + tm:2 * BF16_ROWS + tm] = _norm_mod(xn_ref[0], g, shift, scale).astype(BF16)


def _conv3_rows(u, u_scr, cw, cb, tm, t_total, part=0, nparts=1):
    i = pl.program_id(1)
    o = BF16_ROWS
    rows = tm // nparts
    base = part * (rows + 2 * o)
    head_ok = (i > 0) if part == 0 else True
    tail_ok = (i < t_total // tm - 1) if part == nparts - 1 else True
    u_scr[base:base + o] = jnp.where(head_ok, u[0:o], 0.0)
    u_scr[base + o:base + o + rows] = u[o:o + rows]
    u_scr[base + o + rows:base + 2 * o + rows] = jnp.where(tail_ok, u[o + rows:], 0.0)
    s = base + o
    return (u_scr[s - 1:s - 1 + rows] * cw[0:1] + u_scr[s:s + rows] * cw[1:2] + u_scr[s + 1:s + 1 + rows] * cw[2:3]
            + cb)


def _proj_conv_kernel(x_ref, xp_ref, xn_ref, g_ref, mod_ref, w_ref, b_ref, cw_ref, cb_ref, o_ref, h_scr, u_scr,
                      *, d, tm, t_total, tn):
    m = mod_ref[0]
    _fill_h(x_ref, xp_ref, xn_ref, g_ref[...], m[:, 0:d], m[:, d:2 * d], h_scr, tm)
    nparts = FFN_ROW_PARTS
    rows = tm // nparts
    n = w_ref.shape[1]

    def project(j):
        cols = slice(j * tn, (j + 1) * tn)
        return [jnp.dot(h_scr[p * rows:(p + 1) * rows + 2 * BF16_ROWS], w_ref[:, cols],
                        preferred_element_type=F32) + b_ref[:, cols] for p in range(nparts)]

    nxt_u = project(0)
    for j in range(n // tn):
        cols = slice(j * tn, (j + 1) * tn)
        us = nxt_u
        if (j + 1) * tn < n:
            nxt_u = project(j + 1)
        q, lane0 = divmod(j * tn, d)
        for p in range(nparts):
            o_ref[q, 0, p * rows:(p + 1) * rows, lane0:lane0 + tn] = _conv3_rows(
                us[p], u_scr.at[j % 2], cw_ref[:, cols], cb_ref[:, cols], tm, t_total, p, nparts)


def _proj_conv(x, g, mod, w, bias, cw, cb):
    b, t, d = x.shape
    n = w.shape[1]
    tm = _row_tile(t, 512)
    tn = d // 2
    prev, nxt = _halo_specs(tm, t, d, 2)
    resident = lambda *shape: pl.BlockSpec(shape, lambda bi, i: (0,) * len(shape), pipeline_mode=pl.Buffered(1))
    return pl.pallas_call(
        functools.partial(_proj_conv_kernel, d=d, tm=tm, t_total=t, tn=tn),
        grid=(b, t // tm),
        in_specs=[pl.BlockSpec((1, tm, d), lambda bi, i: (bi, i, 0)), prev, nxt,
                  pl.BlockSpec((1, d), lambda bi, i: (0, 0)),
                  pl.BlockSpec((1, 1, mod.shape[-1]), lambda bi, i: (bi, 0, 0)),
                  resident(d, n), resident(1, n), resident(3, n), resident(1, n)],
        out_specs=pl.BlockSpec((n // d, 1, tm, d), lambda bi, i: (0, bi, i, 0)),
        out_shape=jax.ShapeDtypeStruct((n // d, b, t, d), F32),
        scratch_shapes=[pltpu.VMEM((tm + 2 * BF16_ROWS, d), BF16),
                        pltpu.VMEM((2, tm + 2 * BF16_ROWS * FFN_ROW_PARTS, tn), F32)],
        compiler_params=_cparams("parallel", "parallel"),
        name="hyena_in_proj",
    )(x, x, x, g.reshape(1, d), mod, w, bias.reshape(1, n), cw, cb.reshape(1, n))


def _ffn_kernel(x_ref, xp_ref, xn_ref, g_ref, mod_ref, wup_ref, cw_ref, cb_ref, wd_ref,
                o_ref, h_scr, ug_scr, uv_scr, acc_scr, *, d, tm, t_total, f, tf):
    m = mod_ref[0]
    _fill_h(x_ref, xp_ref, xn_ref, g_ref[...], m[:, 3 * d:4 * d], m[:, 4 * d:5 * d], h_scr, tm)
    nparts = FFN_ROW_PARTS
    rows = tm // nparts
    nf = f // tf

    def up_proj(j):
        hs = [h_scr[p * rows:(p + 1) * rows + 2 * BF16_ROWS] for p in range(nparts)]
        return ([jnp.dot(h, wup_ref[:, j * tf:(j + 1) * tf], preferred_element_type=F32) for h in hs],
                [jnp.dot(h, wup_ref[:, f + j * tf:f + (j + 1) * tf], preferred_element_type=F32) for h in hs])

    nxt_up = up_proj(0)
    for j in range(nf):
        gs = slice(j * tf, (j + 1) * tf)
        vs = slice(f + j * tf, f + (j + 1) * tf)
        ug, uv = nxt_up
        if j + 1 < nf:
            nxt_up = up_proj(j + 1)
        for p in range(nparts):
            gate = _conv3_rows(ug[p], ug_scr.at[j % 2], cw_ref[:, gs], cb_ref[:, gs], tm, t_total, p, nparts)
            val = _conv3_rows(uv[p], uv_scr.at[j % 2], cw_ref[:, vs], cb_ref[:, vs], tm, t_total, p, nparts)
            act = (gate * jax.nn.sigmoid(gate)) * val
            down = jnp.dot(act.astype(BF16), wd_ref[gs, :], preferred_element_type=F32)
            part = slice(p * rows, (p + 1) * rows)
            if j == 0:
                acc_scr[part] = down
            else:
                acc_scr[part] += down
    o_ref[0] = x_ref[0] + m[:, 5 * d:6 * d] * acc_scr[...]


def _conv_ffn(x, g, mod, w_up, cw, cb, w_down):
    b, t, d = x.shape
    f = w_down.shape[0]
    tm = _row_tile(t, 1024)
    tf = 256 if f % 256 == 0 else 128
    prev, nxt = _halo_specs(tm, t, d, 2)
    resident = lambda *shape: pl.BlockSpec(shape, lambda bi, i: (0,) * len(shape), pipeline_mode=pl.Buffered(1))
    u_rows = tm + 2 * BF16_ROWS * FFN_ROW_PARTS
    return pl.pallas_call(
        functools.partial(_ffn_kernel, d=d, tm=tm, t_total=t, f=f, tf=tf),
        grid=(b, t // tm),
        in_specs=[pl.BlockSpec((1, tm, d), lambda bi, i: (bi, i, 0)), prev, nxt,
                  pl.BlockSpec((1, d), lambda bi, i: (0, 0)),
                  pl.BlockSpec((1, 1, mod.shape[-1]), lambda bi, i: (bi, 0, 0)),
                  resident(d, 2 * f), resident(3, 2 * f), resident(1, 2 * f), resident(f, d)],
        out_specs=pl.BlockSpec((1, tm, d), lambda bi, i: (bi, i, 0)),
        out_shape=jax.ShapeDtypeStruct((b, t, d), F32),
        scratch_shapes=[pltpu.VMEM((tm + 2 * BF16_ROWS, d), BF16),
                        pltpu.VMEM((2, u_rows, tf), F32),
                        pltpu.VMEM((2, u_rows, tf), F32),
                        pltpu.VMEM((tm, d), F32)],
        compiler_params=_cparams("parallel", "parallel"),
        name="conv_ffn",
    )(x, x, x, g.reshape(1, d), mod, w_up, cw, cb.reshape(1, 2 * f), w_down)


def _out_res_kernel(y_ref, x_ref, mod_ref, w_ref, b_ref, o_ref, *, d):
    o = _dot(y_ref[0], w_ref[...]) + b_ref[...]
    o_ref[0] = x_ref[0] + mod_ref[0][:, 2 * d:3 * d] * o


def _out_res(y, x, mod, w, bias):
    b, t, d = x.shape
    tm = _row_tile(t, 512)
    return pl.pallas_call(
        functools.partial(_out_res_kernel, d=d),
        grid=(b, t // tm),
        in_specs=[pl.BlockSpec((1, tm, d), lambda bi, i: (bi, i, 0)),
                  pl.BlockSpec((1, tm, d), lambda bi, i: (bi, i, 0)),
                  pl.BlockSpec((1, 1, mod.shape[-1]), lambda bi, i: (bi, 0, 0)),
                  pl.BlockSpec((d, d), lambda bi, i: (0, 0)),
                  pl.BlockSpec((1, d), lambda bi, i: (0, 0))],
        out_specs=pl.BlockSpec((1, tm, d), lambda bi, i: (bi, i, 0)),
        out_shape=jax.ShapeDtypeStruct((b, t, d), F32),
        compiler_params=_cparams("parallel", "parallel"),
        name="out_proj_residual",
    )(y, x, mod, w, bias.reshape(1, d))


def _even_prep_kernel(x_ref, xp_ref, xn_ref, gn_ref, mod_ref, win_ref, mup_ref, mun_ref, w0_ref, w2_ref, a0_ref,
                      a2_ref, g2_ref, kk_ref, ka_ref, qn_ref, kn_ref, cos_ref, sin_ref, e_ref,
                      r_ref, v_ref, kkn_ref, g_ref, lw_ref, a_ref, kd_ref, q_ref, katt_ref, vatt_ref,
                      h_scr, p_scr, *, tm, d, c, a_in, rope):
    i = pl.program_id(1)
    nt = pl.num_programs(1)
    m = mod_ref[0]
    _fill_h(x_ref, xp_ref, xn_ref, gn_ref[...], m[:, 0:d], m[:, d:2 * d], h_scr, tm)
    p_scr[...] = jnp.dot(h_scr[...], win_ref[...], preferred_element_type=F32)
    o = BF16_ROWS
    pa = p_scr[o:o + tm, 0:a_in]
    rows = lax.broadcasted_iota(jnp.int32, (tm, 1), 0)
    prev = jnp.where((rows == 0) & (i == 0), 0.0, p_scr[o - 1:o - 1 + tm, 0:a_in])
    nxt = jnp.where((rows == tm - 1) & (i == nt - 1), 0.0, p_scr[o + 1:o + 1 + tm, 0:a_in])
    za = pa + mup_ref[...] * (prev - pa) + mun_ref[...] * (nxt - pa)

    r = za[:, 0:c]
    k = za[:, c:2 * c]
    v = za[:, 2 * c:3 * c]
    wd = jnp.tanh(za[:, 3 * c:3 * c + 2 * DECAY_LORA])
    ad = za[:, 3 * c + 2 * DECAY_LORA:3 * c + 2 * DECAY_LORA + 2 * ICLR_LORA]
    gd = za[:, 3 * c + 2 * DECAY_LORA + 2 * ICLR_LORA:a_in]
    r_ref[0] = r
    v_ref[0] = v
    g_ref[0] = _dot(jax.nn.sigmoid(gd), g2_ref[...])
    kkv = k * kk_ref[...]
    ss = _segsum(kkv * kkv, e_ref)
    kkn_ref[0] = kkv / jnp.maximum(jnp.sqrt(ss), 1e-12)
    for dd in range(2):
        y = w0_ref[dd:dd + 1] + _dot(wd, w2_ref[dd])
        lw_ref[dd, 0] = -math.exp(-0.5) * jax.nn.sigmoid(y)
        a = jax.nn.sigmoid(a0_ref[dd:dd + 1] + _dot(ad, a2_ref[dd]))
        a_ref[dd, 0] = a
        kd_ref[dd, 0] = k * (1.0 + (a - 1.0) * ka_ref[...])

    pb = p_scr[o:o + tm, a_in:]
    nq = q_ref.shape[-1]
    nk = katt_ref.shape[-1]
    q = pb[:, 0:nq]
    kat = pb[:, nq:nq + nk]
    vatt_ref[0] = pb[:, nq + nk:]
    half = HEAD_DIM // 2

    def norm_rope(x, gain, e):
        w = x.shape[-1]
        ms = _segsum(x * x, e) * (1.0 / HEAD_DIM)
        xn = x * lax.rsqrt(ms + NORM_EPS) * gain
        if not rope:
            return xn
        lane = lax.broadcasted_iota(jnp.int32, (1, w), 1)
        first = (lane % HEAD_DIM) < half
        swapped = jnp.where(first, pltpu.roll(xn, w - half, 1), pltpu.roll(xn, half, 1))
        reps = w // LANES
        cos = jnp.concatenate([cos_ref[...]] * reps, axis=1) if reps > 1 else cos_ref[...]
        sin = jnp.concatenate([sin_ref[...]] * reps, axis=1) if reps > 1 else sin_ref[...]
        return xn * cos + swapped * sin

    q_ref[0] = norm_rope(q, qn_ref[...], e_ref[...])
    katt_ref[0] = norm_rope(kat, kn_ref[...], e_ref[0:nk, 0:nk])


def _even_prep(x, g, mod, ep, tabs, rope):
    b, t, d = x.shape
    n = ep['w_in'].shape[1]
    c = ep['k_k'].shape[-1]
    a_in = ep['mu_prev'].shape[-1]
    nq = ep['nq']
    nk = ep['nk']
    tm = _row_tile(t, 256)
    prev, nxt = _halo_specs(tm, t, d, 2)
    full = lambda *shape: pl.BlockSpec(shape, lambda bi, i: (0,) * len(shape))
    tok = lambda w: pl.BlockSpec((1, tm, w), lambda bi, i: (bi, i, 0))
    tok2 = lambda w: pl.BlockSpec((2, 1, tm, w), lambda bi, i: (0, bi, i, 0))
    sd = lambda *shape: jax.ShapeDtypeStruct(shape, F32)
    return pl.pallas_call(
        functools.partial(_even_prep_kernel, tm=tm, d=d, c=c, a_in=a_in, rope=rope),
        grid=(b, t // tm),
        in_specs=[tok(d), prev, nxt, full(1, d), pl.BlockSpec((1, 1, mod.shape[-1]), lambda bi, i: (bi, 0, 0)),
                  full(d, n), full(1, a_in), full(1, a_in), full(2, c), full(2, 2 * DECAY_LORA, c),
                  full(2, c), full(2, 2 * ICLR_LORA, c), full(GATE_LORA, c), full(1, c), full(1, c),
                  full(1, nq), full(1, nk),
                  pl.BlockSpec((tm, LANES), lambda bi, i: (i, 0)), pl.BlockSpec((tm, LANES), lambda bi, i: (i, 0)),
                  full(nq, nq)],
        out_specs=[tok(c), tok(c), tok(c), tok(c), tok2(c), tok2(c), tok2(c), tok(nq), tok(nk), tok(nk)],
        out_shape=[sd(b, t, c), sd(b, t, c), sd(b, t, c), sd(b, t, c), sd(2, b, t, c), sd(2, b, t, c),
                   sd(2, b, t, c), sd(b, t, nq), sd(b, t, nk), sd(b, t, nk)],
        scratch_shapes=[pltpu.VMEM((tm + 2 * BF16_ROWS, d), BF16), pltpu.VMEM((tm + 2 * BF16_ROWS, n), F32)],
        compiler_params=_cparams("parallel", "parallel"),
        name="even_in_proj_prep",
    )(x, x, x, g.reshape(1, d), mod, ep['w_in'], ep['mu_prev'], ep['mu_next'], ep['w0'], ep['w2pad'], ep['a0'], ep['a2pad'], ep['g2'], ep['k_k'],
      ep['k_a'], ep['q_norm_t'], ep['k_norm_t'], tabs['cos'][:t], tabs['sin'][:t], tabs['e'])


HEADS_PER_GROUP = 4
GROUP_LANES = HEADS_PER_GROUP * HEAD_DIM


def _split_bf16(x):
    hi = x.astype(BF16)
    return hi, (x - hi.astype(F32)).astype(BF16)


def _block_diag(x, bmask):
    return jnp.where(bmask, jnp.concatenate([x] * HEADS_PER_GROUP, axis=0), jnp.zeros((), x.dtype))


def _head_mm(lhs, rhs, bmask, passes, nt=False):
    dn = (((1,), (1,)), ((), ())) if nt else (((1,), (0,)), ((), ()))
    if passes == 1:
        return lax.dot_general(lhs.astype(BF16), _block_diag(rhs.astype(BF16), bmask), dn, preferred_element_type=F32)
    lh, ll = _split_bf16(lhs)
    rh, rl = _split_bf16(rhs)
    m = lhs.shape[0]
    top = lax.dot_general(jnp.concatenate([lh, ll], axis=0), _block_diag(rh, bmask), dn, preferred_element_type=F32)
    return top[0:m] + top[m:] + lax.dot_general(lh, _block_diag(rl, bmask), dn, preferred_element_type=F32)


def _head_mm_tn(lhs, rhs, lane_head, passes):
    dn = (((0,), (0,)), ((), ()))
    if passes == 1:
        full = lax.dot_general(lhs.astype(BF16), rhs.astype(BF16), dn, preferred_element_type=F32)
    else:
        lh, ll = _split_bf16(lhs)
        rh, rl = _split_bf16(rhs)
        full = (lax.dot_general(jnp.concatenate([lh, ll], axis=0), jnp.concatenate([rh, rh], axis=0), dn,
                                preferred_element_type=F32)
                + lax.dot_general(lh, rl, dn, preferred_element_type=F32))
    out = jnp.where(lane_head == 0, full[0:HEAD_DIM], 0.0)
    for h in range(1, HEADS_PER_GROUP):
        out = out + jnp.where(lane_head == h, full[h * HEAD_DIM:(h + 1) * HEAD_DIM], 0.0)
    return out


def _group_masks():
    r = lax.broadcasted_iota(jnp.int32, (GROUP_LANES, GROUP_LANES), 0)
    c = lax.broadcasted_iota(jnp.int32, (GROUP_LANES, GROUP_LANES), 1)
    bmask = (r // HEAD_DIM) == (c // HEAD_DIM)
    lane_head = lax.broadcasted_iota(jnp.int32, (1, GROUP_LANES), 1) // HEAD_DIM
    return bmask, lane_head


def _rwkv_prep_kernel(r_ref, v_ref, kk_ref, lw_ref, a_ref, kd_ref, r2_ref, yl_ref, t_ref, z_ref, *, cs):
    sgn = 1 - 2 * pl.program_id(0)
    row = lax.broadcasted_iota(jnp.int32, (cs, cs), 0)
    col = lax.broadcasted_iota(jnp.int32, (cs, cs), 1)
    tri = (((row - col) * sgn) >= 0).astype(F32)
    bmask, lane_head = _group_masks()
    t_idx = lax.broadcasted_iota(jnp.int32, (cs, GROUP_LANES), 0)
    s_idx = lax.broadcasted_iota(jnp.int32, (cs, GROUP_LANES), 1) % HEAD_DIM
    diff = (t_idx - s_idx) * sgn
    strict = diff > 0
    incl = diff >= 0
    eye = (diff == 0).astype(F32)
    base = min(RWKV_INV_BASE, cs)
    same_base = (t_idx // base) == (s_idx // base)
    groups = r_ref.shape[-1] // GROUP_LANES

    ch = []
    for c0 in range(0, r_ref.shape[1], cs):
        rows = slice(c0, c0 + cs)
        r, v, kk = r_ref[0, rows], v_ref[0, rows], kk_ref[0, rows]
        lw, a, kd = lw_ref[0, 0, rows], a_ref[0, 0, rows], kd_ref[0, 0, rows]
        g = _dot_hi(tri, lw)
        gp = g - lw
        gref = g[cs // 2:cs // 2 + 1]
        gend = jnp.sum(lw, axis=0, keepdims=True)
        bvec = kk * a
        full = {'a_t': -kk * jnp.exp(gp - gref), 'a_0': -kk * jnp.exp(gp),
                'b_t': bvec * jnp.exp(gref - g), 'b_e': bvec * jnp.exp(gend - g),
                'k_t': kd * jnp.exp(gref - g), 'k_e': kd * jnp.exp(gend - g),
                'r_t': r * jnp.exp(g - gref), 'r_0': r * jnp.exp(g), 'v': v,
                'wend': jnp.broadcast_to(jnp.exp(gend), (cs, v.shape[-1]))}
        for gi in range(groups):
            sl = slice(gi * GROUP_LANES, (gi + 1) * GROUP_LANES)
            c = {k: x[:, sl] for k, x in full.items()}
            c['rows'], c['sl'] = rows, sl
            ch.append(c)

    mm = lambda x, y, passes, nt=False: [_head_mm(p_, q_, bmask, passes, nt) for p_, q_ in zip(x, y)]
    get = lambda k: [c[k] for c in ch]
    ar = [jnp.concatenate([c['a_t'], c['r_t']], axis=0) for c in ch]
    gb = mm(ar, get('b_t'), RWKV_GRAM_PASSES, True)
    gk = mm(ar, get('k_t'), RWKV_GRAM_PASSES, True)
    nmat = [jnp.where(strict, x[0:cs], 0.0) for x in gb]
    g_ak = [jnp.where(strict, x[0:cs], 0.0) for x in gk]
    g_rb = [jnp.where(incl, x[cs:], 0.0) for x in gb]
    g_rk = [jnp.where(incl, x[cs:], 0.0) for x in gk]
    npow = [jnp.where(same_base, x, 0.0) for x in nmat]
    p = [eye + x for x in npow]
    for _ in range(int(math.log2(base)) - 1):
        npow = mm(npow, npow, RWKV_INV_PASSES)
        p = [x + y for x, y in zip(p, mm(npow, p, RWKV_INV_PASSES))]
    m = base
    while m < cs:
        off = ((t_idx // m) != (s_idx // m)) & ((t_idx // (2 * m)) == (s_idx // (2 * m)))
        q = mm([jnp.where(off, x, 0.0) for x in nmat], p, RWKV_INV_PASSES)
        p = [x + y for x, y in zip(p, mm(p, q, RWKV_INV_PASSES))]
        m *= 2
    vg = get('v')
    a2 = mm(p, get('a_0'), RWKV_REST_PASSES)
    u_v = mm(p, mm(g_ak, vg, RWKV_REST_PASSES), RWKV_REST_PASSES)
    r2 = mm(g_rb, a2, RWKV_REST_PASSES)
    yl1 = mm(g_rb, u_v, RWKV_REST_PASSES)
    yl2 = mm(g_rk, vg, RWKV_REST_PASSES)
    tt = [_head_mm_tn(x, c['b_e'], lane_head, RWKV_REST_PASSES) for x, c in zip(a2, ch)]
    zz = [_head_mm_tn(jnp.concatenate([u, c['v']], axis=0), jnp.concatenate([c['b_e'], c['k_e']], axis=0),
                      lane_head, RWKV_REST_PASSES) for u, c in zip(u_v, ch)]
    for i, c in enumerate(ch):
        rows, sl = c['rows'], c['sl']
        r2_ref[0, 0, rows, sl] = (c['r_0'] + r2[i]).astype(r2_ref.dtype)
        yl_ref[0, 0, rows, sl] = yl1[i] + yl2[i]
        t_ref[0, 0, rows, sl] = (eye * c['wend'] + tt[i]).astype(t_ref.dtype)
        z_ref[0, 0, rows, sl] = zz[i]


def _rwkv_scan_kernel(h0_ref, r2f_ref, ylf_ref, tf_ref, zf_ref, r2b_ref, ylb_ref, tb_ref, zb_ref,
                      yf_ref, yb_ref, ht_ref, s_scr):
    ci = pl.program_id(0)

    @pl.when(ci == 0)
    def _():
        s_scr[...] = h0_ref[...]

    bmask, _ = _group_masks()
    ins = ((r2f_ref, ylf_ref, tf_ref, zf_ref, yf_ref), (r2b_ref, ylb_ref, tb_ref, zb_ref, yb_ref))
    for d, (r2_ref, yl_ref, t_ref, z_ref, y_ref) in enumerate(ins):
        for b in range(s_scr.shape[1]):
            for gi in range(s_scr.shape[-1] // GROUP_LANES):
                sl = slice(gi * GROUP_LANES, (gi + 1) * GROUP_LANES)
                s = s_scr[d, b, :, sl]
                y_ref[0, b, :, sl] = yl_ref[0, b, :, sl] + _head_mm(r2_ref[0, b, :, sl], s, bmask, 1, nt=True)
                s_scr[d, b, :, sl] = (_head_mm(s, t_ref[0, b, :, sl], bmask, RWKV_SCAN_PASSES)
                                      + z_ref[0, b, :, sl])

    @pl.when(ci == pl.num_programs(0) - 1)
    def _():
        ht_ref[...] = s_scr[...]


def _rwkv(r, v, kk, lw, a, kd, h0):
    b, t, c = r.shape
    cs = min(RWKV_CHUNK, t)
    assert cs == HEAD_DIM and c % GROUP_LANES == 0
    nc = t // cs
    rows = cs * math.gcd(nc, RWKV_PREP_CHUNKS)
    tok = pl.BlockSpec((1, rows, c), lambda d, bi, ci: (bi, ci, 0))
    tok2 = pl.BlockSpec((1, 1, rows, c), lambda d, bi, ci: (d, bi, ci, 0))
    wide = jax.ShapeDtypeStruct((2, b, t, c), F32)
    mm_operand = jax.ShapeDtypeStruct((2, b, t, c), BF16 if RWKV_SCAN_PASSES == 1 else F32)
    r2, yl, tt, zz = pl.pallas_call(
        functools.partial(_rwkv_prep_kernel, cs=cs),
        grid=(2, b, t // rows),
        in_specs=[tok, tok, tok, tok2, tok2, tok2],
        out_specs=[tok2, tok2, tok2, tok2],
        out_shape=[mm_operand, wide, mm_operand, wide],
        compiler_params=_cparams("parallel", "parallel", "parallel"),
        name="rwkv7_chunk_prep",
    )(r, v, kk, lw, a, kd)

    fwd = pl.BlockSpec((1, b, cs, c), lambda ci: (0, 0, ci, 0))
    bwd = pl.BlockSpec((1, b, cs, c), lambda ci: (1, 0, nc - 1 - ci, 0))
    st = pl.BlockSpec((2, b, HEAD_DIM, c), lambda ci: (0, 0, 0, 0))
    yf, yb, ht = pl.pallas_call(
        _rwkv_scan_kernel,
        grid=(nc,),
        in_specs=[st, fwd, fwd, fwd, fwd, bwd, bwd, bwd, bwd],
        out_specs=[pl.BlockSpec((1, b, cs, c), lambda ci: (0, 0, ci, 0)),
                   pl.BlockSpec((1, b, cs, c), lambda ci: (0, 0, nc - 1 - ci, 0)), st],
        out_shape=[jax.ShapeDtypeStruct((1, b, t, c), F32), jax.ShapeDtypeStruct((1, b, t, c), F32),
                   jax.ShapeDtypeStruct((2, b, HEAD_DIM, c), F32)],
        scratch_shapes=[pltpu.VMEM((2, b, HEAD_DIM, c), F32)],
        compiler_params=_cparams("arbitrary"),
        name="rwkv7_state_scan",
    )(h0, r2, yl, tt, zz, r2, yl, tt, zz)
    return (yf.reshape(b, t, c), yb.reshape(b, t, c)), ht


def _attn_kernel(sink_ref, q_ref, kp_ref, kc_ref, kn_ref, vp_ref, vc_ref, vn_ref, kx_ref, vx_ref, o_ref,
                 *, local, group):
    i = pl.program_id(1)
    nb = pl.num_programs(1)
    scale = HEAD_DIM ** -0.5
    assert math.frexp(scale)[0] == 0.5
    q = q_ref[0] * scale
    if local:
        k_all = jnp.concatenate([kp_ref[0], kc_ref[0], kn_ref[0], kx_ref[0]], axis=0)
        v_all = jnp.concatenate([vp_ref[0], vc_ref[0], vn_ref[0], vx_ref[0]], axis=0)
        qi = lax.broadcasted_iota(jnp.int32, (BLOCK, BLOCK), 0)
        kj = lax.broadcasted_iota(jnp.int32, (BLOCK, BLOCK), 1)
        assert WINDOW == BLOCK
        valid_prev = (kj >= qi) & (i > 0)
        valid_next = (kj <= qi) & (i < nb - 1)
    else:
        k_all = kx_ref[0]
        v_all = vx_ref[0]
    heads = range(q.shape[-1] // HEAD_DIM)
    kv = lambda x, h: x[:, (h // group) * HEAD_DIM:(h // group + 1) * HEAD_DIM]
    s = [_dot_nt(q[:, h * HEAD_DIM:(h + 1) * HEAD_DIM], kv(k_all, h)) for h in heads]
    if local:
        s = [jnp.concatenate([jnp.where(valid_prev, x[:, 0:BLOCK], MASK_VALUE), x[:, BLOCK:2 * BLOCK],
                              jnp.where(valid_next, x[:, 2 * BLOCK:3 * BLOCK], MASK_VALUE), x[:, 3 * BLOCK:]],
                             axis=1) for x in s]
    sink = [sink_ref[h] for h in heads]
    m = [jnp.maximum(jnp.max(x, axis=-1, keepdims=True), sk) for x, sk in zip(s, sink)]
    e = [jnp.exp(x - y) for x, y in zip(s, m)]
    den = [jnp.sum(x, axis=-1, keepdims=True) + jnp.exp(sk - y) for x, y, sk in zip(e, m, sink)]
    outs = [_dot(x, kv(v_all, h)) / dn for x, h, dn in zip(e, heads, den)]
    o_ref[0] = jnp.concatenate(outs, axis=1)


def _attention(q, k, v, kx, vx, sink, local):
    b, t, nq = q.shape
    nk = k.shape[-1]
    nb = t // BLOCK
    group = (nq // HEAD_DIM) // (nk // HEAD_DIM)
    lx = kx.shape[1]
    kv = lambda f: pl.BlockSpec((1, BLOCK, nk), f)
    pf = lambda bi, i: (bi, jnp.maximum(i - 1, 0), 0)
    cf = lambda bi, i: (bi, i, 0)
    nf = lambda bi, i: (bi, jnp.minimum(i + 1, nb - 1), 0)
    ctx = pl.BlockSpec((1, lx, nk), lambda bi, i: (bi, 0, 0))
    return pl.pallas_call(
        functools.partial(_attn_kernel, local=local, group=group),
        grid=(b, nb),
        in_specs=[pl.BlockSpec(memory_space=pltpu.SMEM),
                  pl.BlockSpec((1, BLOCK, nq), cf), kv(pf), kv(cf), kv(nf), kv(pf), kv(cf), kv(nf), ctx, ctx],
        out_specs=pl.BlockSpec((1, BLOCK, nq), cf),
        out_shape=jax.ShapeDtypeStruct((b, t, nq), F32),
        compiler_params=_cparams("parallel", "parallel"),
        name="window_attention" if local else "context_attention",
    )(sink, q, k, k, k, v, v, v, kx, vx)


def _even_out_kernel(yf_ref, yb_ref, r_ref, v_ref, g_ref, kd_ref, batt_ref, x_ref, mod_ref, lnw_ref, lnb_ref, rk_ref, e_ref,
                     w_ref, o_ref, *, d, c):
    tm = x_ref.shape[1]
    parts = [slice(r0, r0 + tm // EVEN_OUT_PARTS) for r0 in range(0, tm, tm // EVEN_OUT_PARTS)]
    inv = 1.0 / HEAD_DIM
    y = [yf_ref[0, p] + yb_ref[0, p] for p in parts]
    mu = [_segsum(a, e_ref) * inv for a in y]
    yc = [a - b for a, b in zip(y, mu)]
    var = [_segsum(a * a, e_ref) * inv for a in yc]
    yn = [a * lax.rsqrt(b + RWKV_GN_EPS) * lnw_ref[...] + lnb_ref[...] for a, b in zip(yc, var)]
    bonus = [_segsum(r_ref[0, p] * (kd_ref[0, 0, p] + kd_ref[1, 0, p]) * rk_ref[...], e_ref) for p in parts]
    a_out = [(a + b * v_ref[0, p]) * g_ref[0, p] for a, b, p in zip(yn, bonus, parts)]
    o = [_dot(a, w_ref[0:c]) + _dot(batt_ref[0, p], w_ref[c:]) for a, p in zip(a_out, parts)]
    gate = mod_ref[0][:, 2 * d:3 * d]
    for p, op in zip(parts, o):
        o_ref[0, p] = x_ref[0, p] + gate * op


def _even_out(y, r, v, g, kd, batt, x, mod, ep, tabs):
    b, t, d = x.shape
    c = r.shape[-1]
    nq = batt.shape[-1]
    tm = _row_tile(t, 512)
    tok = lambda w: pl.BlockSpec((1, tm, w), lambda bi, i: (bi, i, 0))
    tok2 = lambda w: pl.BlockSpec((2, 1, tm, w), lambda bi, i: (0, bi, i, 0))
    full = lambda *shape: pl.BlockSpec(shape, lambda bi, i: (0,) * len(shape))
    return pl.pallas_call(
        functools.partial(_even_out_kernel, d=d, c=c),
        grid=(b, t // tm),
        in_specs=[tok(c), tok(c), tok(c), tok(c), tok(c), tok2(c), tok(nq), tok(d),
                  pl.BlockSpec((1, 1, mod.shape[-1]), lambda bi, i: (bi, 0, 0)),
                  full(1, c), full(1, c), full(1, c), full(c, c), full(c + nq, d)],
        out_specs=tok(d),
        out_shape=jax.ShapeDtypeStruct((b, t, d), F32),
        compiler_params=_cparams("parallel", "parallel"),
        name="even_out_proj",
    )(y[0], y[1], r, v, g, kd, batt, x, mod, ep['ln_w'], ep['ln_b'], ep['r_k'], tabs['e'], ep['w_out'])


def _filter_kernel(z_ref, t_ref, w1_ref, b1_ref, w2_ref, b2_ref, w3_ref, b3_ref, fr_ref, wo_ref, dl_ref, o_ref, *, d):
    fr = fr_ref[...]
    h = jnp.sin(fr * (_dot_hi(z_ref[...], w1_ref[...]) + b1_ref[...]))
    h = jnp.sin(fr * (_dot_hi(h, w2_ref[...]) + b2_ref[...]))
    h = jnp.sin(fr * (_dot_hi(h, w3_ref[...]) + b3_ref[...]))
    filt = _dot_hi(h, wo_ref[...])
    modu = jnp.exp(-t_ref[...] * dl_ref[...]) + HY_MOD_SHIFT
    for q in range(o_ref.shape[0]):
        o_ref[q] = filt[:, q * d:(q + 1) * d] * modu


def _hyena_filters(n, op, d):
    t = np.linspace(0.0, 1.0, n, dtype=np.float32)[:, None]
    ang = (2.0 * math.pi * np.arange(n, dtype=np.float32)[:, None] / np.float32(n)).astype(np.float32)
    f = np.linspace(1e-4, HY_BANDS - 1, HY_BANDS, dtype=np.float32)[None, :]
    zfeat = jnp.concatenate([jnp.asarray(t), jnp.cos(jnp.asarray(f * ang)), -jnp.sin(jnp.asarray(f * ang))], axis=-1)
    emb_pad = op['f_w1'].shape[0]
    zfeat = jnp.pad(zfeat, ((0, 0), (0, emb_pad - HY_EMB)))
    deltas = np.abs(np.linspace(math.log(HY_TARGET) / HY_SLOW_PCT, math.log(HY_TARGET) / HY_FAST_PCT, d,
                                dtype=np.float32))[None, :]
    tn = min(n, 256)
    nq = 2 * HY_ORDER
    hf = op['f_w2'].shape[0]
    full = lambda *shape: pl.BlockSpec(shape, lambda i: (0,) * len(shape))
    return pl.pallas_call(
        functools.partial(_filter_kernel, d=d),
        grid=(n // tn,),
        in_specs=[pl.BlockSpec((tn, emb_pad), lambda i: (i, 0)), pl.BlockSpec((tn, 1), lambda i: (i, 0)),
                  full(emb_pad, hf), full(1, hf), full(hf, hf), full(1, hf), full(hf, hf), full(1, hf), full(1, hf),
                  full(hf, nq * d), full(1, d)],
        out_specs=pl.BlockSpec((nq, tn, d), lambda i: (0, i, 0)),
        out_shape=jax.ShapeDtypeStruct((nq, n, d), F32),
        compiler_params=_cparams("parallel"),
        name="hyena_filter",
    )(zfeat, jnp.asarray(t), op['f_w1'], op['f_b1'], op['f_w2'], op['f_b2'], op['f_w3'], op['f_b3'], op['f_freq'],
      op['f_out'], jnp.asarray(deltas))


def _dft(n, rows, cols, sign=-1.0):
    k = np.arange(rows, dtype=np.float64)[:, None]
    m = np.arange(cols, dtype=np.float64)[None, :]
    ang = sign * 2.0 * np.pi * ((k * m) % n) / n
    return np.cos(ang), np.sin(ang)


def _stack(re, im):
    return jnp.asarray(np.concatenate([re, im], axis=0).astype(np.float32)).astype(BF16)


def _fft_tables(n_seq):
    n = 2 * n_seq
    n1 = n // FFT_N2
    f1r, f1i = _dft(n1, n1, n1 // 2)
    h1r, h1i = _dft(n1, n1 // 2, n1, sign=1.0)
    k1 = np.arange(n1, dtype=np.float64)[:, None, None]
    k2 = np.arange(FFT_N2, dtype=np.float64)[None, :, None]
    j2 = np.arange(FFT_N2, dtype=np.float64)[None, None, :]
    ang = -2.0 * np.pi * (((k2 * j2 * n1) + k1 * j2) % n) / n
    gr, gi = np.cos(ang), np.sin(ang)
    g_fwd = np.concatenate([gr, gi], axis=1)
    g_inv = np.concatenate([np.swapaxes(gr, 1, 2), np.swapaxes(gi, 1, 2)], axis=1)
    eye = np.eye(SUBLANES)
    kron = lambda m: np.kron(m, eye)
    return {'f1': _stack(kron(f1r), kron(f1i)), 'h1': _stack(kron(h1r), kron(h1i)),
            'g_fwd': jnp.asarray(g_fwd.astype(np.float32)).astype(BF16),
            'g_inv': jnp.asarray(g_inv.astype(np.float32)).astype(BF16), 'n1': n1}


def _dense_tables(n_seq):
    n = 2 * n_seq
    fr, fi = _dft(n, n, n_seq)
    hr, hi = _dft(n, n_seq, n, sign=1.0)
    return {'f': _stack(fr, fi), 'h': _stack(hr, hi)}


def _fft_a_kernel(f_ref, zr_ref, zi_ref, ar_ref, ai_ref, *, n1, cplx):
    f = f_ref[...]
    half, tj, d = zr_ref.shape[1:]
    m = n1 * SUBLANES
    res_r, res_i = [], []
    for s0 in range(0, tj, SUBLANES):
        js = slice(s0, s0 + SUBLANES)
        zr = zr_ref[0, :, js, :].reshape(half * SUBLANES, d)
        p = jnp.dot(f, zr.astype(BF16), preferred_element_type=F32)
        if cplx:
            zi = zi_ref[0, :, js, :].reshape(half * SUBLANES, d)
            q = jnp.dot(f, zi.astype(BF16), preferred_element_type=F32)
            res_r.append((p[0:m] - q[m:]).reshape(n1, SUBLANES, d))
            res_i.append((p[m:] + q[0:m]).reshape(n1, SUBLANES, d))
        else:
            res_r.append(p[0:m].reshape(n1, SUBLANES, d))
            res_i.append(p[m:].reshape(n1, SUBLANES, d))
    ar_ref[0] = jnp.concatenate(res_r, axis=1).astype(BF16)
    ai_ref[0] = jnp.concatenate(res_i, axis=1).astype(BF16)


def _fft_a(u5, q, tabs, cplx):
    _, s, half, n2, d = u5.shape
    n1 = tabs['n1']
    assert half == n1 // 2 and n2 == FFT_N2
    sp = s // 2 if cplx else s
    tj = 2 * SUBLANES
    zi_map = (lambda si, j: (q, si + sp, 0, j, 0)) if cplx else (lambda si, j: (q, si, 0, j, 0))
    out = pl.BlockSpec((1, n1, tj, d), lambda si, j: (si, 0, j, 0))
    return pl.pallas_call(
        functools.partial(_fft_a_kernel, n1=n1, cplx=cplx),
        grid=(sp, n2 // tj),
        in_specs=[pl.BlockSpec((2 * n1 * SUBLANES, half * SUBLANES), lambda si, j: (0, 0)),
                  pl.BlockSpec((None, 1, half, tj, d), lambda si, j: (q, si, 0, j, 0)),
                  pl.BlockSpec((None, 1, half, tj, d), zi_map)],
        out_specs=[out, out],
        out_shape=[jax.ShapeDtypeStruct((sp, n1, n2, d), BF16)] * 2,
        compiler_params=_cparams("parallel", "parallel"),
        name="fft_stage1",
    )(tabs['f1'], u5, u5)


def _cplx_mm(s, xr, xi, conj):
    p = jnp.dot(s, xr.astype(BF16), preferred_element_type=F32)
    q = jnp.dot(s, xi.astype(BF16), preferred_element_type=F32)
    m = s.shape[0] // 2
    if conj:
        return p[0:m] + q[m:], q[0:m] - p[m:]
    return p[0:m] - q[m:], p[m:] + q[0:m]


def _fft_b_kernel(gf_ref, gi_ref, ar_ref, ai_ref, kr_ref, ki_ref, dr_ref, di_ref):
    kr = kr_ref[0]
    ki = ki_ref[0]
    seqs = range(ar_ref.shape[0])
    c = [_cplx_mm(gf_ref[0], ar_ref[s, 0], ai_ref[s, 0], False) for s in seqs]
    e = [(cr * kr - ci * ki, cr * ki + ci * kr) for cr, ci in c]
    dd = [_cplx_mm(gi_ref[0], er, ei, True) for er, ei in e]
    for s, (dr, di) in zip(seqs, dd):
        dr_ref[s, 0] = dr.astype(BF16)
        di_ref[s, 0] = di.astype(BF16)


def _fft_b(ar4, ai4, kr, ki, order, tabs):
    sp, n1, _, d = ar4.shape
    ns = 2 if sp % 2 == 0 else 1
    blk = pl.BlockSpec((ns, 1, FFT_N2, d), lambda k1, si: (si, k1, 0, 0))
    tab = pl.BlockSpec((1, 2 * FFT_N2, FFT_N2), lambda k1, si: (k1, 0, 0))
    kb = pl.BlockSpec((None, 1, FFT_N2, d), lambda k1, si: (order, k1, 0, 0))
    return pl.pallas_call(
        _fft_b_kernel,
        grid=(n1, sp // ns),
        in_specs=[tab, tab, blk, blk, kb, kb],
        out_specs=[blk, blk],
        out_shape=[jax.ShapeDtypeStruct((sp, n1, FFT_N2, d), BF16)] * 2,
        compiler_params=_cparams("parallel", "arbitrary"),
        name="fft_stage2_filter",
    )(tabs['g_fwd'], tabs['g_inv'], ar4, ai4, kr, ki)


def _fft_c_kernel(h_ref, dr_ref, di_ref, u_ref, x_ref, sk_ref, o_ref):
    h = h_ref[...]
    n1, tj, d = dr_ref.shape[1:]
    half = n1 // 2
    sk = sk_ref[...]
    dr_all = dr_ref[0].astype(F32)
    di_all = di_ref[0].astype(F32)
    for s0 in range(0, tj, SUBLANES):
        js = slice(s0, s0 + SUBLANES)
        yr, yi = _cplx_mm(h, dr_all[:, js, :].reshape(n1 * SUBLANES, d),
                          di_all[:, js, :].reshape(n1 * SUBLANES, d), False)
        for part, y in enumerate((yr, yi)):
            u = u_ref[part, 0, :, js, :]
            o_ref[part, 0, :, js, :] = x_ref[part, 0, :, js, :] * (y.reshape(half, SUBLANES, d) + u * sk)


def _fft_c(dr, di, u6, uq, x6, xq, skip, tabs):
    sp, n1, n2, d = dr.shape
    half = n1 // 2
    tj = 2 * SUBLANES
    pair = lambda q: pl.BlockSpec((None, 2, 1, half, tj, d), lambda si, j: (q, 0, si, 0, j, 0))
    dblk = pl.BlockSpec((1, n1, tj, d), lambda si, j: (si, 0, j, 0))
    out = pl.pallas_call(
        _fft_c_kernel,
        grid=(sp, n2 // tj),
        in_specs=[pl.BlockSpec((2 * half * SUBLANES, n1 * SUBLANES), lambda si, j: (0, 0)), dblk, dblk,
                  pair(uq), pair(xq), pl.BlockSpec((1, d), lambda si, j: (0, 0))],
        out_specs=pl.BlockSpec((2, 1, half, tj, d), lambda si, j: (0, si, 0, j, 0)),
        out_shape=jax.ShapeDtypeStruct((2, sp, half, n2, d), F32),
        compiler_params=_cparams("parallel", "parallel"),
        name="fft_inverse_stage1_gate",
    )(tabs['h1'], dr, di, u6, x6, skip.reshape(1, d))
    return out


def _spec_b_kernel(gf_ref, ar_ref, ai_ref, kr_ref, ki_ref, *, scale):
    fr, fi = _cplx_mm(gf_ref[0], ar_ref[0, 0], ai_ref[0, 0], False)
    gr, gi = _cplx_mm(gf_ref[0], ar_ref[1, 0], ai_ref[1, 0], False)
    kr_ref[0, 0] = (fr + gr) * scale
    ki_ref[0, 0] = (fi - gi) * scale


def _filter_spectrum_fft(filt, tabs, d):
    nq, n, _ = filt.shape
    n1 = tabs['n1']
    ar, ai = _fft_a(filt.reshape(1, nq, n1 // 2, FFT_N2, d), 0, tabs, cplx=False)
    orders = nq // 2
    ar5 = ar.reshape(orders, 2, n1, FFT_N2, d)
    ai5 = ai.reshape(orders, 2, n1, FFT_N2, d)
    blk = pl.BlockSpec((None, 2, 1, FFT_N2, d), lambda o, k1: (o, 0, k1, 0, 0))
    out = pl.BlockSpec((1, 1, FFT_N2, d), lambda o, k1: (o, k1, 0, 0))
    return pl.pallas_call(
        functools.partial(_spec_b_kernel, scale=1.0 / (2 * n)),
        grid=(orders, n1),
        in_specs=[pl.BlockSpec((1, 2 * FFT_N2, FFT_N2), lambda o, k1: (k1, 0, 0)), blk, blk],
        out_specs=[out, out],
        out_shape=[jax.ShapeDtypeStruct((orders, n1, FFT_N2, d), F32)] * 2,
        compiler_params=_cparams("parallel", "parallel"),
        name="filter_spectrum",
    )(tabs['g_fwd'], ar5, ai5)


def _long_conv_fft(u, uq, x, xq, skip, kr, ki, order, tabs):
    _, b, n, d = u.shape
    half = tabs['n1'] // 2
    ar, ai = _fft_a(u.reshape(u.shape[0], b, half, FFT_N2, d), uq, tabs, cplx=True)
    dr, di = _fft_b(ar, ai, kr, ki, order, tabs)
    six = lambda a: a.reshape(a.shape[0], 2, b // 2, half, FFT_N2, d)
    return _fft_c(dr, di, six(u), uq, six(x), xq, skip, tabs).reshape(b, n, d)


def _dense_spec_kernel(f_ref, filt_ref, kr_ref, ki_ref, *, scale):
    f = f_ref[...]
    n = f.shape[0] // 2
    pf = jnp.dot(f, filt_ref[0, 0].astype(BF16), preferred_element_type=F32)
    pg = jnp.dot(f, filt_ref[0, 1].astype(BF16), preferred_element_type=F32)
    kr_ref[0] = (pf[0:n] + pg[0:n]) * scale
    ki_ref[0] = (pf[n:] - pg[n:]) * scale


def _filter_spectrum_dense(filt, tabs, d):
    nq, n, _ = filt.shape
    orders = nq // 2
    f4 = filt.reshape(orders, 2, n, d)
    out = pl.BlockSpec((1, 2 * n, d), lambda o: (o, 0, 0))
    return pl.pallas_call(
        functools.partial(_dense_spec_kernel, scale=1.0 / (2 * n)),
        grid=(orders,),
        in_specs=[pl.BlockSpec((4 * n, n), lambda o: (0, 0)), pl.BlockSpec((1, 2, n, d), lambda o: (o, 0, 0, 0))],
        out_specs=[out, out],
        out_shape=[jax.ShapeDtypeStruct((orders, 2 * n, d), F32)] * 2,
        compiler_params=_cparams("parallel"),
        name="filter_spectrum_dense",
    )(tabs['f'], f4)


def _dense_conv_kernel(f_ref, h_ref, u_ref, x_ref, kr_ref, ki_ref, sk_ref, o_ref):
    cr, ci = _cplx_mm(f_ref[...], u_ref[0, 0], u_ref[1, 0], False)
    kr = kr_ref[...]
    ki = ki_ref[...]
    yr, yi = _cplx_mm(h_ref[...], cr * kr - ci * ki, cr * ki + ci * kr, False)
    sk = sk_ref[...]
    o_ref[0, 0] = x_ref[0, 0] * (yr + u_ref[0, 0] * sk)
    o_ref[1, 0] = x_ref[1, 0] * (yi + u_ref[1, 0] * sk)


def _long_conv_dense(u, x, skip, kr, ki, tabs):
    b, n, d = u.shape
    sp = b // 2
    pair = pl.BlockSpec((2, 1, n, d), lambda si: (0, si, 0, 0))
    kb = pl.BlockSpec((2 * n, d), lambda si: (0, 0))
    out = pl.pallas_call(
        _dense_conv_kernel,
        grid=(sp,),
        in_specs=[pl.BlockSpec((4 * n, n), lambda si: (0, 0)), pl.BlockSpec((2 * n, 2 * n), lambda si: (0, 0)),
                  pair, pair, kb, kb, pl.BlockSpec((1, d), lambda si: (0, 0))],
        out_specs=pair,
        out_shape=jax.ShapeDtypeStruct((2, sp, n, d), F32),
        compiler_params=_cparams("parallel"),
        name="long_conv_dense",
    )(tabs['f'], tabs['h'], u.reshape(2, sp, n, d), x.reshape(2, sp, n, d), kr, ki, skip.reshape(1, d))
    return out.reshape(b, n, d)


def _hyena_mixer(x, g, mod, op, fft_tabs, dense_tabs):
    b, n, d = x.shape
    z = _proj_conv(x, g, mod, op['w_in'], op['b_in'], op['conv_w'], op['conv_b'])
    filt = _hyena_filters(n, op, d)
    if n <= DENSE_FFT_MAX:
        kr, ki = _filter_spectrum_dense(filt, dense_tabs, d)
        y = _long_conv_dense(z[0], z[1], op['skip'][0], kr[0], ki[0], dense_tabs)
        y = _long_conv_dense(y, z[2], op['skip'][1], kr[1], ki[1], dense_tabs)
    else:
        kr, ki = _filter_spectrum_fft(filt, fft_tabs, d)
        y = _long_conv_fft(z, 0, z, 1, op['skip'][0], kr, ki, 0, fft_tabs)
        y = _long_conv_fft(y[None], 0, z, 2, op['skip'][1], kr, ki, 1, fft_tabs)
    return _out_res(y, x, mod, op['w_out'], op['b_out'])


def _even_mixer(x, ctx, g, mod_l, mod_c, ep, tabs, need_ctx):
    rc, vc, kkc, gc, lwc, ac, kdc, qc, kac, vac = _even_prep(ctx, g, mod_c, ep, tabs, rope=False)
    rl, vl, kkl, gl, lwl, al, kdl, ql, kal, val = _even_prep(x, g, mod_l, ep, tabs, rope=True)
    b = x.shape[0]
    c = rl.shape[-1]
    h0 = jnp.zeros((2, b, HEAD_DIM, c), F32)
    y_ctx, s_ctx = _rwkv(rc, vc, kkc, lwc, ac, kdc, h0)
    y_lat, _ = _rwkv(rl, vl, kkl, lwl, al, kdl, s_ctx)
    b_lat = _attention(ql, kal, val, kac, vac, ep['sink'], local=True)
    x_new = _even_out(y_lat, rl, vl, gl, kdl, b_lat, x, mod_l, ep, tabs)
    if not need_ctx:
        return x_new, None
    b_ctx = _attention(qc, kac, vac, kac, vac, ep['sink'], local=False)
    ctx_new = _even_out(y_ctx, rc, vc, gc, kdc, b_ctx, ctx, mod_c, ep, tabs)
    return x_new, ctx_new


def _rope_tables(n_tokens):
    rows = n_tokens // GRID_W
    row = jnp.repeat(jnp.arange(rows), GRID_W).astype(F32)
    col = jnp.tile(jnp.arange(GRID_W), rows).astype(F32)
    n_freq = HEAD_DIM // 4
    inv = ROPE_THETA ** (-jnp.arange(n_freq, dtype=F32) / n_freq)
    ang = jnp.concatenate([row[:, None] * inv, col[:, None] * inv], axis=-1)
    cos, sin = jnp.cos(ang), jnp.sin(ang)
    reps = LANES // HEAD_DIM
    cos_t = jnp.tile(jnp.concatenate([cos, cos], axis=-1), (1, reps))
    sin_t = jnp.tile(jnp.concatenate([-sin, sin], axis=-1), (1, reps))
    return cos_t, sin_t


def _block_ones(width):
    idx = np.arange(width) // HEAD_DIM
    return jnp.asarray((idx[:, None] == idx[None, :]).astype(np.float32)).astype(BF16)


def _lora_pad(w):
    z = jnp.zeros_like(w[0])
    return jnp.stack([jnp.concatenate([w[0], z], axis=0), jnp.concatenate([z, w[1]], axis=0)], axis=0)


def kernel(x, c, ctx, c_ctx, ada_w, ada_b, norm1_g, norm2_g, ffn_up, ffn_conv_w, ffn_conv_b, ffn_down, ev_w_in, ev_mu_prev, ev_mu_next, ev_w0, ev_w2, ev_a0, ev_a2, ev_g2, ev_k_k, ev_k_a, ev_r_k, ev_ln_w, ev_ln_b, ev_q_norm, ev_k_norm, ev_sink, ev_w_out, od_w_in, od_b_in, od_conv_w, od_conv_b, od_f_w1, od_f_b1, od_f_w2, od_f_b2, od_f_w3, od_f_b3, od_f_freq, od_f_out, od_skip, od_w_out, od_b_out):
    bsz, seq, d = x.shape
    lc = ctx.shape[1]
    depth = ada_w.shape[0]
    a_width = ev_k_k.shape[-1]
    nq = ev_sink.shape[-1] * HEAD_DIM
    nk = B_KV_HEADS * HEAD_DIM

    cond = jnp.zeros((BF16_ROWS, d), F32).at[:bsz].set(c).at[bsz].set(c_ctx)
    mod = _ada_mod(cond, ada_w, ada_b)

    cos_t, sin_t = _rope_tables(seq)
    tabs = {'cos': cos_t, 'sin': sin_t, 'e': _block_ones(nq)}
    fft_tabs = _fft_tables(seq) if seq > DENSE_FFT_MAX else None
    dense_lat = _dense_tables(seq) if seq <= DENSE_FFT_MAX else None
    fft_ctx = _fft_tables(lc) if lc > DENSE_FFT_MAX else None
    dense_ctx = _dense_tables(lc) if lc <= DENSE_FFT_MAX else None

    for layer in range(depth):
        need_ctx = layer < depth - 1
        even = layer % 2 == 0
        j = layer // 2
        mod_l = mod[layer, :bsz].reshape(bsz, 1, 6 * d)
        mod_c = jnp.broadcast_to(mod[layer, bsz].reshape(1, 1, 6 * d), (bsz, 1, 6 * d))
        if even:
            ep = {'w_in': ev_w_in[j].astype(BF16), 'mu_prev': ev_mu_prev[j][None], 'mu_next': ev_mu_next[j][None],
                  'w0': ev_w0[j], 'w2pad': _lora_pad(ev_w2[j]).astype(BF16), 'a0': ev_a0[j],
                  'a2pad': _lora_pad(ev_a2[j]).astype(BF16), 'g2': ev_g2[j].astype(BF16),
                  'k_k': ev_k_k[j][None], 'k_a': ev_k_a[j][None], 'r_k': ev_r_k[j].reshape(1, a_width),
                  'ln_w': ev_ln_w[j][None], 'ln_b': ev_ln_b[j][None],
                  'q_norm_t': jnp.tile(ev_q_norm[j], nq // HEAD_DIM)[None],
                  'k_norm_t': jnp.tile(ev_k_norm[j], nk // HEAD_DIM)[None],
                  'sink': ev_sink[j], 'w_out': ev_w_out[j].astype(BF16), 'nq': nq, 'nk': nk}
            x, ctx_new = _even_mixer(x, ctx, norm1_g[layer], mod_l, mod_c, ep, tabs, need_ctx)
        else:
            emb_pad = HEAD_DIM
            op = {'w_in': od_w_in[j].astype(BF16), 'b_in': od_b_in[j], 'conv_w': od_conv_w[j],
                  'conv_b': od_conv_b[j],
                  'f_w1': jnp.pad(od_f_w1[j], ((0, emb_pad - HY_EMB), (0, 0))), 'f_b1': od_f_b1[j][None],
                  'f_w2': od_f_w2[j], 'f_b2': od_f_b2[j][None], 'f_w3': od_f_w3[j], 'f_b3': od_f_b3[j][None],
                  'f_freq': od_f_freq[j][None], 'f_out': od_f_out[j], 'skip': od_skip[j],
                  'w_out': od_w_out[j].astype(BF16), 'b_out': od_b_out[j]}
            ctx_new = _hyena_mixer(ctx, norm1_g[layer], mod_c, op, fft_ctx, dense_ctx) if need_ctx else None
            x = _hyena_mixer(x, norm1_g[layer], mod_l, op, fft_tabs, dense_lat)
        w_up = ffn_up[layer].astype(BF16)
        w_down = ffn_down[layer].astype(BF16)
        x = _conv_ffn(x, norm2_g[layer], mod_l, w_up, ffn_conv_w[layer], ffn_conv_b[layer], w_down)
        if need_ctx:
            ctx = _conv_ffn(ctx_new, norm2_g[layer], mod_c, w_up, ffn_conv_w[layer], ffn_conv_b[layer], w_down)
    return x
```

```python
import functools
import math

import numpy as np
import jax
import jax.numpy as jnp
from jax import lax
from jax.experimental import pallas as pl
from jax.experimental.pallas import tpu as pltpu

F32 = jnp.float32
BF16 = jnp.bfloat16

HEAD_DIM = 64
GRID_W = 64
DECAY_LORA = 64
ICLR_LORA = 64
GATE_LORA = 128
RWKV_GN_EPS = 64e-5
B_KV_HEADS = 2
WINDOW = 128
BLOCK = 128
ROPE_THETA = 10000.0
MASK_VALUE = -1e30
HY_ORDER = 2
HY_EMB = 33
HY_BANDS = (HY_EMB - 1) // 2
HY_TARGET = 1e-2
HY_FAST_PCT = 0.3
HY_SLOW_PCT = 1.5
HY_MOD_SHIFT = 0.05
NORM_EPS = 1e-6

V7X_VMEM_BYTES = 64 * 1024 * 1024
VMEM_LIMIT_BYTES = V7X_VMEM_BYTES * 3 // 4
LANES = 128
SUBLANES = 8
BF16_ROWS = 16
FFN_ROW_PARTS = 2
RWKV_CHUNK = 64
RWKV_PREP_CHUNKS = 4
RWKV_INV_BASE = 8
RWKV_GRAM_PASSES = 1
RWKV_INV_PASSES = 1
RWKV_REST_PASSES = 1
RWKV_SCAN_PASSES = 1
FFT_N2 = 128
DENSE_FFT_MAX = 512

HIGHEST = lax.Precision.HIGHEST


def _cparams(*sem):
    return pltpu.CompilerParams(dimension_semantics=sem, vmem_limit_bytes=VMEM_LIMIT_BYTES)


def _dot(a, b):
    return jnp.dot(a.astype(BF16), b.astype(BF16), preferred_element_type=F32)


def _dot_nt(a, b):
    return lax.dot_general(a.astype(BF16), b.astype(BF16), (((1,), (1,)), ((), ())), preferred_element_type=F32)


def _dot_tn(a, b):
    return lax.dot_general(a.astype(BF16), b.astype(BF16), (((0,), (0,)), ((), ())), preferred_element_type=F32)


def _dot_hi(a, b):
    return jnp.dot(a, b, preferred_element_type=F32, precision=HIGHEST)


def _dot_nt_hi(a, b):
    return lax.dot_general(a, b, (((1,), (1,)), ((), ())), preferred_element_type=F32, precision=HIGHEST)


def _dot_tn_hi(a, b):
    return lax.dot_general(a, b, (((0,), (0,)), ((), ())), preferred_element_type=F32, precision=HIGHEST)


def _segsum(x, e_ref):
    hi = x.astype(BF16)
    lo = (x - hi.astype(F32)).astype(BF16)
    e = e_ref[...]
    return jnp.dot(hi, e, preferred_element_type=F32) + jnp.dot(lo, e, preferred_element_type=F32)


def _norm_mod(x, g, shift, scale):
    xn = x * lax.rsqrt(jnp.mean(x * x, axis=-1, keepdims=True) + NORM_EPS)
    return (xn * g) * (1.0 + scale) + shift


def _row_tile(t, want):
    tm = min(t, want)
    assert t % tm == 0 and tm % BF16_ROWS == 0
    return tm


def _ada_kernel(c_ref, w_ref, b_ref, o_ref):
    c = c_ref[...]
    s = c * jax.nn.sigmoid(c)
    o_ref[0] = _dot(s, w_ref[0]) + b_ref[0]


def _ada_mod(cond, ada_w, ada_b):
    depth, d, n = ada_w.shape
    tn = n // 4
    rows = cond.shape[0]
    return pl.pallas_call(
        _ada_kernel,
        grid=(depth, n // tn),
        in_specs=[pl.BlockSpec((rows, d), lambda l, j: (0, 0)),
                  pl.BlockSpec((1, d, tn), lambda l, j: (l, 0, j)),
                  pl.BlockSpec((1, 1, tn), lambda l, j: (l, 0, j))],
        out_specs=pl.BlockSpec((1, rows, tn), lambda l, j: (l, 0, j)),
        out_shape=jax.ShapeDtypeStruct((depth, rows, n), F32),
        compiler_params=_cparams("parallel", "parallel"),
        name="ada_mod",
    )(cond, ada_w, ada_b.reshape(depth, 1, n))


def _halo_specs(tm, t, d, nargs):
    r = tm // BF16_ROWS
    last = t // BF16_ROWS - 1
    if nargs == 3:
        prev = pl.BlockSpec((1, BF16_ROWS, d), lambda bi, i, j: (bi, jnp.maximum(i * r - 1, 0), 0))
        nxt = pl.BlockSpec((1, BF16_ROWS, d), lambda bi, i, j: (bi, jnp.minimum((i + 1) * r, last), 0))
    else:
        prev = pl.BlockSpec((1, BF16_ROWS, d), lambda bi, i: (bi, jnp.maximum(i * r - 1, 0), 0))
        nxt = pl.BlockSpec((1, BF16_ROWS, d), lambda bi, i: (bi, jnp.minimum((i + 1) * r, last), 0))
    return prev, nxt


def _fill_h(x_ref, xp_ref, xn_ref, g, shift, scale, h_scr, tm):
    h_scr[0:BF16_ROWS] = _norm_mod(xp_ref[0], g, shift, scale).astype(BF16)
    h_scr[BF16_ROWS:BF16_ROWS + tm] = _norm_mod(x_ref[0], g, shift, scale).astype(BF16)
    h_scr[BF16_ROWS + tm:2 * BF16_ROWS + tm] = _norm_mod(xn_ref[0], g, shift, scale).astype(BF16)


def _conv3_rows(u, u_scr, cw, cb, tm, t_total, part=0, nparts=1):
    i = pl.program_id(1)
    o = BF16_ROWS
    rows = tm // nparts
    base = part * (rows + 2 * o)
    head_ok = (i > 0) if part == 0 else True
    tail_ok = (i < t_total // tm - 1) if part == nparts - 1 else True
    u_scr[base:base + o] = jnp.where(head_ok, u[0:o], 0.0)
    u_scr[base + o:base + o + rows] = u[o:o + rows]
    u_scr[base + o + rows:base + 2 * o + rows] = jnp.where(tail_ok, u[o + rows:], 0.0)
    s = base + o
    return (u_scr[s - 1:s - 1 + rows] * cw[0:1] + u_scr[s:s + rows] * cw[1:2] + u_scr[s + 1:s + 1 + rows] * cw[2:3]
            + cb)


def _proj_conv_kernel(x_ref, xp_ref, xn_ref, g_ref, mod_ref, w_ref, b_ref, cw_ref, cb_ref, o_ref, h_scr, u_scr,
                      *, d, tm, t_total, tn):
    m = mod_ref[0]
    _fill_h(x_ref, xp_ref, xn_ref, g_ref[...], m[:, 0:d], m[:, d:2 * d], h_scr, tm)
    nparts = FFN_ROW_PARTS
    rows = tm // nparts
    n = w_ref.shape[1]

    def project(j):
        cols = slice(j * tn, (j + 1) * tn)
        return [jnp.dot(h_scr[p * rows:(p + 1) * rows + 2 * BF16_ROWS], w_ref[:, cols],
                        preferred_element_type=F32) + b_ref[:, cols] for p in range(nparts)]

    nxt_u = project(0)
    for j in range(n // tn):
        cols = slice(j * tn, (j + 1) * tn)
        us = nxt_u
        if (j + 1) * tn < n:
            nxt_u = project(j + 1)
        q, lane0 = divmod(j * tn, d)
        for p in range(nparts):
            o_ref[q, 0, p * rows:(p + 1) * rows, lane0:lane0 + tn] = _conv3_rows(
                us[p], u_scr.at[j % 2], cw_ref[:, cols], cb_ref[:, cols], tm, t_total, p, nparts)


def _proj_conv(x, g, mod, w, bias, cw, cb):
    b, t, d = x.shape
    n = w.shape[1]
    tm = _row_tile(t, 512)
    tn = d // 2
    prev, nxt = _halo_specs(tm, t, d, 2)
    resident = lambda *shape: pl.BlockSpec(shape, lambda bi, i: (0,) * len(shape), pipeline_mode=pl.Buffered(1))
    return pl.pallas_call(
        functools.partial(_proj_conv_kernel, d=d, tm=tm, t_total=t, tn=tn),
        grid=(b, t // tm),
        in_specs=[pl.BlockSpec((1, tm, d), lambda bi, i: (bi, i, 0)), prev, nxt,
                  pl.BlockSpec((1, d), lambda bi, i: (0, 0)),
                  pl.BlockSpec((1, 1, mod.shape[-1]), lambda bi, i: (bi, 0, 0)),
                  resident(d, n), resident(1, n), resident(3, n), resident(1, n)],
        out_specs=pl.BlockSpec((n // d, 1, tm, d), lambda bi, i: (0, bi, i, 0)),
        out_shape=jax.ShapeDtypeStruct((n // d, b, t, d), F32),
        scratch_shapes=[pltpu.VMEM((tm + 2 * BF16_ROWS, d), BF16),
                        pltpu.VMEM((2, tm + 2 * BF16_ROWS * FFN_ROW_PARTS, tn), F32)],
        compiler_params=_cparams("parallel", "parallel"),
        name="hyena_in_proj",
    )(x, x, x, g.reshape(1, d), mod, w, bias.reshape(1, n), cw, cb.reshape(1, n))


def _ffn_kernel(x_ref, xp_ref, xn_ref, g_ref, mod_ref, wup_ref, cw_ref, cb_ref, wd_ref,
                o_ref, h_scr, ug_scr, uv_scr, acc_scr, *, d, tm, t_total, f, tf):
    m = mod_ref[0]
    _fill_h(x_ref, xp_ref, xn_ref, g_ref[...], m[:, 3 * d:4 * d], m[:, 4 * d:5 * d], h_scr, tm)
    nparts = FFN_ROW_PARTS
    rows = tm // nparts
    nf = f // tf

    def up_proj(j):
        hs = [h_scr[p * rows:(p + 1) * rows + 2 * BF16_ROWS] for p in range(nparts)]
        return ([jnp.dot(h, wup_ref[:, j * tf:(j + 1) * tf], preferred_element_type=F32) for h in hs],
                [jnp.dot(h, wup_ref[:, f + j * tf:f + (j + 1) * tf], preferred_element_type=F32) for h in hs])

    nxt_up = up_proj(0)
    for j in range(nf):
        gs = slice(j * tf, (j + 1) * tf)
        vs = slice(f + j * tf, f + (j + 1) * tf)
        ug, uv = nxt_up
        if j + 1 < nf:
            nxt_up = up_proj(j + 1)
        for p in range(nparts):
            gate = _conv3_rows(ug[p], ug_scr.at[j % 2], cw_ref[:, gs], cb_ref[:, gs], tm, t_total, p, nparts)
            val = _conv3_rows(uv[p], uv_scr.at[j % 2], cw_ref[:, vs], cb_ref[:, vs], tm, t_total, p, nparts)
            act = (gate * jax.nn.sigmoid(gate)) * val
            down = jnp.dot(act.astype(BF16), wd_ref[gs, :], preferred_element_type=F32)
            part = slice(p * rows, (p + 1) * rows)
            if j == 0:
                acc_scr[part] = down
            else:
                acc_scr[part] += down
    o_ref[0] = x_ref[0] + m[:, 5 * d:6 * d] * acc_scr[...]


def _conv_ffn(x, g, mod, w_up, cw, cb, w_down):
    b, t, d = x.shape
    f = w_down.shape[0]
    tm = _row_tile(t, 1024)
    tf = 256 if f % 256 == 0 else 128
    prev, nxt = _halo_specs(tm, t, d, 2)
    resident = lambda *shape: pl.BlockSpec(shape, lambda bi, i: (0,) * len(shape), pipeline_mode=pl.Buffered(1))
    u_rows = tm + 2 * BF16_ROWS * FFN_ROW_PARTS
    return pl.pallas_call(
        functools.partial(_ffn_kernel, d=d, tm=tm, t_total=t, f=f, tf=tf),
        grid=(b, t // tm),
        in_specs=[pl.BlockSpec((1, tm, d), lambda bi, i: (bi, i, 0)), prev, nxt,
                  pl.BlockSpec((1, d), lambda bi, i: (0, 0)),
                  pl.BlockSpec((1, 1, mod.shape[-1]), lambda bi, i: (bi, 0, 0)),
                  resident(d, 2 * f), resident(3, 2 * f), resident(1, 2 * f), resident(f, d)],
        out_specs=pl.BlockSpec((1, tm, d), lambda bi, i: (bi, i, 0)),
        out_shape=jax.ShapeDtypeStruct((b, t, d), F32),
        scratch_shapes=[pltpu.VMEM((tm + 2 * BF16_ROWS, d), BF16),
                        pltpu.VMEM((2, u_rows, tf), F32),
                        pltpu.VMEM((2, u_rows, tf), F32),
                        pltpu.VMEM((tm, d), F32)],
        compiler_params=_cparams("parallel", "parallel"),
        name="conv_ffn",
    )(x, x, x, g.reshape(1, d), mod, w_up, cw, cb.reshape(1, 2 * f), w_down)


def _out_res_kernel(y_ref, x_ref, mod_ref, w_ref, b_ref, o_ref, *, d):
    o = _dot(y_ref[0], w_ref[...]) + b_ref[...]
    o_ref[0] = x_ref[0] + mod_ref[0][:, 2 * d:3 * d] * o


def _out_res(y, x, mod, w, bias):
    b, t, d = x.shape
    tm = _row_tile(t, 512)
    return pl.pallas_call(
        functools.partial(_out_res_kernel, d=d),
        grid=(b, t // tm),
        in_specs=[pl.BlockSpec((1, tm, d), lambda bi, i: (bi, i, 0)),
                  pl.BlockSpec((1, tm, d), lambda bi, i: (bi, i, 0)),
                  pl.BlockSpec((1, 1, mod.shape[-1]), lambda bi, i: (bi, 0, 0)),
                  pl.BlockSpec((d, d), lambda bi, i: (0, 0)),
                  pl.BlockSpec((1, d), lambda bi, i: (0, 0))],
        out_specs=pl.BlockSpec((1, tm, d), lambda bi, i: (bi, i, 0)),
        out_shape=jax.ShapeDtypeStruct((b, t, d), F32),
        compiler_params=_cparams("parallel", "parallel"),
        name="out_proj_residual",
    )(y, x, mod, w, bias.reshape(1, d))


def _even_prep_kernel(x_ref, xp_ref, xn_ref, gn_ref, mod_ref, win_ref, mup_ref, mun_ref, w0_ref, w2_ref, a0_ref,
                      a2_ref, g2_ref, kk_ref, ka_ref, qn_ref, kn_ref, cos_ref, sin_ref, e_ref,
                      r_ref, v_ref, kkn_ref, g_ref, lw_ref, a_ref, kd_ref, q_ref, katt_ref, vatt_ref,
                      h_scr, p_scr, *, tm, d, c, a_in, rope):
    i = pl.program_id(1)
    nt = pl.num_programs(1)
    m = mod_ref[0]
    _fill_h(x_ref, xp_ref, xn_ref, gn_ref[...], m[:, 0:d], m[:, d:2 * d], h_scr, tm)
    p_scr[...] = jnp.dot(h_scr[...], win_ref[...], preferred_element_type=F32)
    o = BF16_ROWS
    pa = p_scr[o:o + tm, 0:a_in]
    rows = lax.broadcasted_iota(jnp.int32, (tm, 1), 0)
    prev = jnp.where((rows == 0) & (i == 0), 0.0, p_scr[o - 1:o - 1 + tm, 0:a_in])
    nxt = jnp.where((rows == tm - 1) & (i == nt - 1), 0.0, p_scr[o + 1:o + 1 + tm, 0:a_in])
    za = pa + mup_ref[...] * (prev - pa) + mun_ref[...] * (nxt - pa)

    r = za[:, 0:c]
    k = za[:, c:2 * c]
    v = za[:, 2 * c:3 * c]
    wd = jnp.tanh(za[:, 3 * c:3 * c + 2 * DECAY_LORA])
    ad = za[:, 3 * c + 2 * DECAY_LORA:3 * c + 2 * DECAY_LORA + 2 * ICLR_LORA]
    gd = za[:, 3 * c + 2 * DECAY_LORA + 2 * ICLR_LORA:a_in]
    r_ref[0] = r
    v_ref[0] = v
    g_ref[0] = _dot(jax.nn.sigmoid(gd), g2_ref[...])
    kkv = k * kk_ref[...]
    ss = _segsum(kkv * kkv, e_ref)
    kkn_ref[0] = kkv / jnp.maximum(jnp.sqrt(ss), 1e-12)
    for dd in range(2):
        y = w0_ref[dd:dd + 1] + _dot(wd, w2_ref[dd])
        lw_ref[dd, 0] = -math.exp(-0.5) * jax.nn.sigmoid(y)
        a = jax.nn.sigmoid(a0_ref[dd:dd + 1] + _dot(ad, a2_ref[dd]))
        a_ref[dd, 0] = a
        kd_ref[dd, 0] = k * (1.0 + (a - 1.0) * ka_ref[...])

    pb = p_scr[o:o + tm, a_in:]
    nq = q_ref.shape[-1]
    nk = katt_ref.shape[-1]
    q = pb[:, 0:nq]
    kat = pb[:, nq:nq + nk]
    vatt_ref[0] = pb[:, nq + nk:]
    half = HEAD_DIM // 2

    def norm_rope(x, gain, e):
        w = x.shape[-1]
        ms = _segsum(x * x, e) * (1.0 / HEAD_DIM)
        xn = x * lax.rsqrt(ms + NORM_EPS) * gain
        if not rope:
            return xn
        lane = lax.broadcasted_iota(jnp.int32, (1, w), 1)
        first = (lane % HEAD_DIM) < half
        swapped = jnp.where(first, pltpu.roll(xn, w - half, 1), pltpu.roll(xn, half, 1))
        reps = w // LANES
        cos = jnp.concatenate([cos_ref[...]] * reps, axis=1) if reps > 1 else cos_ref[...]
        sin = jnp.concatenate([sin_ref[...]] * reps, axis=1) if reps > 1 else sin_ref[...]
        return xn * cos + swapped * sin

    q_ref[0] = norm_rope(q, qn_ref[...], e_ref[...])
    katt_ref[0] = norm_rope(kat, kn_ref[...], e_ref[0:nk, 0:nk])


def _even_prep(x, g, mod, ep, tabs, rope):
    b, t, d = x.shape
    n = ep['w_in'].shape[1]
    c = ep['k_k'].shape[-1]
    a_in = ep['mu_prev'].shape[-1]
    nq = ep['nq']
    nk = ep['nk']
    tm = _row_tile(t, 256)
    prev, nxt = _halo_specs(tm, t, d, 2)
    full = lambda *shape: pl.BlockSpec(shape, lambda bi, i: (0,) * len(shape))
    tok = lambda w: pl.BlockSpec((1, tm, w), lambda bi, i: (bi, i, 0))
    tok2 = lambda w: pl.BlockSpec((2, 1, tm, w), lambda bi, i: (0, bi, i, 0))
    sd = lambda *shape: jax.ShapeDtypeStruct(shape, F32)
    return pl.pallas_call(
        functools.partial(_even_prep_kernel, tm=tm, d=d, c=c, a_in=a_in, rope=rope),
        grid=(b, t // tm),
        in_specs=[tok(d), prev, nxt, full(1, d), pl.BlockSpec((1, 1, mod.shape[-1]), lambda bi, i: (bi, 0, 0)),
                  full(d, n), full(1, a_in), full(1, a_in), full(2, c), full(2, 2 * DECAY_LORA, c),
                  full(2, c), full(2, 2 * ICLR_LORA, c), full(GATE_LORA, c), full(1, c), full(1, c),
                  full(1, nq), full(1, nk),
                  pl.BlockSpec((tm, LANES), lambda bi, i: (i, 0)), pl.BlockSpec((tm, LANES), lambda bi, i: (i, 0)),
                  full(nq, nq)],
        out_specs=[tok(c), tok(c), tok(c), tok(c), tok2(c), tok2(c), tok2(c), tok(nq), tok(nk), tok(nk)],
        out_shape=[sd(b, t, c), sd(b, t, c), sd(b, t, c), sd(b, t, c), sd(2, b, t, c), sd(2, b, t, c),
                   sd(2, b, t, c), sd(b, t, nq), sd(b, t, nk), sd(b, t, nk)],
        scratch_shapes=[pltpu.VMEM((tm + 2 * BF16_ROWS, d), BF16), pltpu.VMEM((tm + 2 * BF16_ROWS, n), F32)],
        compiler_params=_cparams("parallel", "parallel"),
        name="even_in_proj_prep",
    )(x, x, x, g.reshape(1, d), mod, ep['w_in'], ep['mu_prev'], ep['mu_next'], ep['w0'], ep['w2pad'], ep['a0'], ep['a2pad'], ep['g2'], ep['k_k'],
      ep['k_a'], ep['q_norm_t'], ep['k_norm_t'], tabs['cos'][:t], tabs['sin'][:t], tabs['e'])


HEADS_PER_GROUP = 2
GROUP_LANES = HEADS_PER_GROUP * HEAD_DIM


def _split_bf16(x):
    hi = x.astype(BF16)
    return hi, (x - hi.astype(F32)).astype(BF16)


def _block_diag(x, bmask):
    return jnp.where(bmask, jnp.concatenate([x] * HEADS_PER_GROUP, axis=0), jnp.zeros((), x.dtype))


def _head_mm(lhs, rhs, bmask, passes, nt=False):
    dn = (((1,), (1,)), ((), ())) if nt else (((1,), (0,)), ((), ()))
    if passes == 1:
        return lax.dot_general(lhs.astype(BF16), _block_diag(rhs.astype(BF16), bmask), dn, preferred_element_type=F32)
    lh, ll = _split_bf16(lhs)
    rh, rl = _split_bf16(rhs)
    m = lhs.shape[0]
    top = lax.dot_general(jnp.concatenate([lh, ll], axis=0), _block_diag(rh, bmask), dn, preferred_element_type=F32)
    return top[0:m] + top[m:] + lax.dot_general(lh, _block_diag(rl, bmask), dn, preferred_element_type=F32)


def _head_mm_tn(lhs, rhs, lane_head, passes):
    dn = (((0,), (0,)), ((), ()))
    if passes == 1:
        full = lax.dot_general(lhs.astype(BF16), rhs.astype(BF16), dn, preferred_element_type=F32)
    else:
        lh, ll = _split_bf16(lhs)
        rh, rl = _split_bf16(rhs)
        full = (lax.dot_general(jnp.concatenate([lh, ll], axis=0), jnp.concatenate([rh, rh], axis=0), dn,
                                preferred_element_type=F32)
                + lax.dot_general(lh, rl, dn, preferred_element_type=F32))
    out = jnp.where(lane_head == 0, full[0:HEAD_DIM], 0.0)
    for h in range(1, HEADS_PER_GROUP):
        out = out + jnp.where(lane_head == h, full[h * HEAD_DIM:(h + 1) * HEAD_DIM], 0.0)
    return out


def _group_masks():
    r = lax.broadcasted_iota(jnp.int32, (GROUP_LANES, GROUP_LANES), 0)
    c = lax.broadcasted_iota(jnp.int32, (GROUP_LANES, GROUP_LANES), 1)
    bmask = (r // HEAD_DIM) == (c // HEAD_DIM)
    lane_head = lax.broadcasted_iota(jnp.int32, (1, GROUP_LANES), 1) // HEAD_DIM
    return bmask, lane_head


def _rwkv_prep_kernel(r_ref, v_ref, kk_ref, lw_ref, a_ref, kd_ref, r2_ref, yl_ref, t_ref, z_ref, *, cs):
    sgn = 1 - 2 * pl.program_id(0)
    row = lax.broadcasted_iota(jnp.int32, (cs, cs), 0)
    col = lax.broadcasted_iota(jnp.int32, (cs, cs), 1)
    tri = (((row - col) * sgn) >= 0).astype(F32)
    bmask, lane_head = _group_masks()
    t_idx = lax.broadcasted_iota(jnp.int32, (cs, GROUP_LANES), 0)
    s_idx = lax.broadcasted_iota(jnp.int32, (cs, GROUP_LANES), 1) % HEAD_DIM
    diff = (t_idx - s_idx) * sgn
    strict = diff > 0
    incl = diff >= 0
    eye = (diff == 0).astype(F32)
    base = min(RWKV_INV_BASE, cs)
    same_base = (t_idx // base) == (s_idx // base)
    groups = r_ref.shape[-1] // GROUP_LANES

    ch = []
    for c0 in range(0, r_ref.shape[1], cs):
        rows = slice(c0, c0 + cs)
        r, v, kk = r_ref[0, rows], v_ref[0, rows], kk_ref[0, rows]
        lw, a, kd = lw_ref[0, 0, rows], a_ref[0, 0, rows], kd_ref[0, 0, rows]
        g = _dot_hi(tri, lw)
        gp = g - lw
        gref = g[cs // 2:cs // 2 + 1]
        gend = jnp.sum(lw, axis=0, keepdims=True)
        bvec = kk * a
        full = {'a_t': -kk * jnp.exp(gp - gref), 'a_0': -kk * jnp.exp(gp),
                'b_t': bvec * jnp.exp(gref - g), 'b_e': bvec * jnp.exp(gend - g),
                'k_t': kd * jnp.exp(gref - g), 'k_e': kd * jnp.exp(gend - g),
                'r_t': r * jnp.exp(g - gref), 'r_0': r * jnp.exp(g), 'v': v,
                'wend': jnp.broadcast_to(jnp.exp(gend), (cs, v.shape[-1]))}
        for gi in range(groups):
            sl = slice(gi * GROUP_LANES, (gi + 1) * GROUP_LANES)
            c = {k: x[:, sl] for k, x in full.items()}
            c['rows'], c['sl'] = rows, sl
            ch.append(c)

    mm = lambda x, y, passes, nt=False: [_head_mm(p_, q_, bmask, passes, nt) for p_, q_ in zip(x, y)]
    get = lambda k: [c[k] for c in ch]
    ar = [jnp.concatenate([c['a_t'], c['r_t']], axis=0) for c in ch]
    gb = mm(ar, get('b_t'), RWKV_GRAM_PASSES, True)
    gk = mm(ar, get('k_t'), RWKV_GRAM_PASSES, True)
    nmat = [jnp.where(strict, x[0:cs], 0.0) for x in gb]
    g_rb = [jnp.where(incl, x[cs:], 0.0) for x in gb]
    g_k = [jnp.concatenate([jnp.where(strict, x[0:cs], 0.0), jnp.where(incl, x[cs:], 0.0)], axis=0) for x in gk]
    npow = [jnp.where(same_base, x, 0.0) for x in nmat]
    p = [eye + x for x in npow]
    for _ in range(int(math.log2(base)) - 1):
        npow = mm(npow, npow, RWKV_INV_PASSES)
        p = [x + y for x, y in zip(p, mm(npow, p, RWKV_INV_PASSES))]
    m = base
    while m < cs:
        off = ((t_idx // m) != (s_idx // m)) & ((t_idx // (2 * m)) == (s_idx // (2 * m)))
        q = mm([jnp.where(off, x, 0.0) for x in nmat], p, RWKV_INV_PASSES)
        p = [x + y for x, y in zip(p, mm(p, q, RWKV_INV_PASSES))]
        m *= 2
    vg = get('v')
    a2 = mm(p, get('a_0'), RWKV_REST_PASSES)
    kv = mm(g_k, vg, RWKV_REST_PASSES)
    u_v = mm(p, [x[0:cs] for x in kv], RWKV_REST_PASSES)
    r2 = mm(g_rb, a2, RWKV_REST_PASSES)
    yl1 = mm(g_rb, u_v, RWKV_REST_PASSES)
    yl2 = [x[cs:] for x in kv]
    tt = [_head_mm_tn(x, c['b_e'], lane_head, RWKV_REST_PASSES) for x, c in zip(a2, ch)]
    zz = [_head_mm_tn(jnp.concatenate([u, c['v']], axis=0), jnp.concatenate([c['b_e'], c['k_e']], axis=0),
                      lane_head, RWKV_REST_PASSES) for u, c in zip(u_v, ch)]
    for i, c in enumerate(ch):
        rows, sl = c['rows'], c['sl']
        r2_ref[0, 0, rows, sl] = (c['r_0'] + r2[i]).astype(r2_ref.dtype)
        yl_ref[0, 0, rows, sl] = yl1[i] + yl2[i]
        t_ref[0, 0, rows, sl] = (eye * c['wend'] + tt[i]).astype(t_ref.dtype)
        z_ref[0, 0, rows, sl] = zz[i]


def _rwkv_scan_kernel(h0_ref, r2f_ref, ylf_ref, tf_ref, zf_ref, r2b_ref, ylb_ref, tb_ref, zb_ref,
                      yf_ref, yb_ref, ht_ref, s_scr):
    ci = pl.program_id(0)

    @pl.when(ci == 0)
    def _():
        s_scr[...] = h0_ref[...]

    bmask, _ = _group_masks()
    ins = ((r2f_ref, ylf_ref, tf_ref, zf_ref, yf_ref), (r2b_ref, ylb_ref, tb_ref, zb_ref, yb_ref))
    for d, (r2_ref, yl_ref, t_ref, z_ref, y_ref) in enumerate(ins):
        for b in range(s_scr.shape[1]):
            for gi in range(s_scr.shape[-1] // GROUP_LANES):
                sl = slice(gi * GROUP_LANES, (gi + 1) * GROUP_LANES)
                s = s_scr[d, b, :, sl]
                y_ref[0, b, :, sl] = yl_ref[0, b, :, sl] + _head_mm(r2_ref[0, b, :, sl], s, bmask, 1, nt=True)
                s_scr[d, b, :, sl] = (_head_mm(s, t_ref[0, b, :, sl], bmask, RWKV_SCAN_PASSES)
                                      + z_ref[0, b, :, sl])

    @pl.when(ci == pl.num_programs(0) - 1)
    def _():
        ht_ref[...] = s_scr[...]


def _rwkv(r, v, kk, lw, a, kd, h0):
    b, t, c = r.shape
    cs = min(RWKV_CHUNK, t)
    assert cs == HEAD_DIM and c % GROUP_LANES == 0
    nc = t // cs
    rows = cs * math.gcd(nc, RWKV_PREP_CHUNKS)
    tok = pl.BlockSpec((1, rows, c), lambda d, bi, ci: (bi, ci, 0))
    tok2 = pl.BlockSpec((1, 1, rows, c), lambda d, bi, ci: (d, bi, ci, 0))
    wide = jax.ShapeDtypeStruct((2, b, t, c), F32)
    mm_operand = jax.ShapeDtypeStruct((2, b, t, c), BF16 if RWKV_SCAN_PASSES == 1 else F32)
    r2, yl, tt, zz = pl.pallas_call(
        functools.partial(_rwkv_prep_kernel, cs=cs),
        grid=(2, b, t // rows),
        in_specs=[tok, tok, tok, tok2, tok2, tok2],
        out_specs=[tok2, tok2, tok2, tok2],
        out_shape=[mm_operand, wide, mm_operand, wide],
        compiler_params=_cparams("parallel", "parallel", "parallel"),
        name="rwkv7_chunk_prep",
    )(r, v, kk, lw, a, kd)

    fwd = pl.BlockSpec((1, b, cs, c), lambda ci: (0, 0, ci, 0))
    bwd = pl.BlockSpec((1, b, cs, c), lambda ci: (1, 0, nc - 1 - ci, 0))
    st = pl.BlockSpec((2, b, HEAD_DIM, c), lambda ci: (0, 0, 0, 0))
    yf, yb, ht = pl.pallas_call(
        _rwkv_scan_kernel,
        grid=(nc,),
        in_specs=[st, fwd, fwd, fwd, fwd, bwd, bwd, bwd, bwd],
        out_specs=[pl.BlockSpec((1, b, cs, c), lambda ci: (0, 0, ci, 0)),
                   pl.BlockSpec((1, b, cs, c), lambda ci: (0, 0, nc - 1 - ci, 0)), st],
        out_shape=[jax.ShapeDtypeStruct((1, b, t, c), F32), jax.ShapeDtypeStruct((1, b, t, c), F32),
                   jax.ShapeDtypeStruct((2, b, HEAD_DIM, c), F32)],
        scratch_shapes=[pltpu.VMEM((2, b, HEAD_DIM, c), F32)],
        compiler_params=_cparams("arbitrary"),
        name="rwkv7_state_scan",
    )(h0, r2, yl, tt, zz, r2, yl, tt, zz)
    return (yf.reshape(b, t, c), yb.reshape(b, t, c)), ht


def _attn_kernel(sink_ref, q_ref, kp_ref, kc_ref, kn_ref, vp_ref, vc_ref, vn_ref, kx_ref, vx_ref, o_ref,
                 *, local, group):
    i = pl.program_id(1)
    nb = pl.num_programs(1)
    scale = HEAD_DIM ** -0.5
    assert math.frexp(scale)[0] == 0.5
    q = q_ref[0] * scale
    if local:
        k_all = jnp.concatenate([kp_ref[0], kc_ref[0], kn_ref[0], kx_ref[0]], axis=0)
        v_all = jnp.concatenate([vp_ref[0], vc_ref[0], vn_ref[0], vx_ref[0]], axis=0)
        qi = lax.broadcasted_iota(jnp.int32, (BLOCK, BLOCK), 0)
        kj = lax.broadcasted_iota(jnp.int32, (BLOCK, BLOCK), 1)
        assert WINDOW == BLOCK
        valid_prev = (kj >= qi) & (i > 0)
        valid_next = (kj <= qi) & (i < nb - 1)
    else:
        k_all = kx_ref[0]
        v_all = vx_ref[0]
    heads = range(q.shape[-1] // HEAD_DIM)
    kv = lambda x, h: x[:, (h // group) * HEAD_DIM:(h // group + 1) * HEAD_DIM]
    s = [_dot_nt(q[:, h * HEAD_DIM:(h + 1) * HEAD_DIM], kv(k_all, h)) for h in heads]
    if local:
        s = [jnp.concatenate([jnp.where(valid_prev, x[:, 0:BLOCK], MASK_VALUE), x[:, BLOCK:2 * BLOCK],
                              jnp.where(valid_next, x[:, 2 * BLOCK:3 * BLOCK], MASK_VALUE), x[:, 3 * BLOCK:]],
                             axis=1) for x in s]
    sink = [sink_ref[h] for h in heads]
    m = [jnp.maximum(jnp.max(x, axis=-1, keepdims=True), sk) for x, sk in zip(s, sink)]
    e = [jnp.exp(x - y) for x, y in zip(s, m)]
    den = [jnp.sum(x, axis=-1, keepdims=True) + jnp.exp(sk - y) for x, y, sk in zip(e, m, sink)]
    outs = [_dot(x, kv(v_all, h)) / dn for x, h, dn in zip(e, heads, den)]
    o_ref[0] = jnp.concatenate(outs, axis=1)


def _attention(q, k, v, kx, vx, sink, local):
    b, t, nq = q.shape
    nk = k.shape[-1]
    nb = t // BLOCK
    group = (nq // HEAD_DIM) // (nk // HEAD_DIM)
    lx = kx.shape[1]
    kv = lambda f: pl.BlockSpec((1, BLOCK, nk), f)
    pf = lambda bi, i: (bi, jnp.maximum(i - 1, 0), 0)
    cf = lambda bi, i: (bi, i, 0)
    nf = lambda bi, i: (bi, jnp.minimum(i + 1, nb - 1), 0)
    ctx = pl.BlockSpec((1, lx, nk), lambda bi, i: (bi, 0, 0))
    return pl.pallas_call(
        functools.partial(_attn_kernel, local=local, group=group),
        grid=(b, nb),
        in_specs=[pl.BlockSpec(memory_space=pltpu.SMEM),
                  pl.BlockSpec((1, BLOCK, nq), cf), kv(pf), kv(cf), kv(nf), kv(pf), kv(cf), kv(nf), ctx, ctx],
        out_specs=pl.BlockSpec((1, BLOCK, nq), cf),
        out_shape=jax.ShapeDtypeStruct((b, t, nq), F32),
        compiler_params=_cparams("parallel", "parallel"),
        name="window_attention" if local else "context_attention",
    )(sink, q, k, k, k, v, v, v, kx, vx)


def _even_out_kernel(yf_ref, yb_ref, r_ref, v_ref, g_ref, kd_ref, batt_ref, x_ref, mod_ref, lnw_ref, lnb_ref, rk_ref, e_ref,
                     w_ref, o_ref, *, d, c):
    y = yf_ref[0] + yb_ref[0]
    inv = 1.0 / HEAD_DIM
    mu = _segsum(y, e_ref) * inv
    yc = y - mu
    var = _segsum(yc * yc, e_ref) * inv
    yn = yc * lax.rsqrt(var + RWKV_GN_EPS) * lnw_ref[...] + lnb_ref[...]
    bonus = _segsum(r_ref[0] * (kd_ref[0, 0] + kd_ref[1, 0]) * rk_ref[...], e_ref)
    a_out = (yn + bonus * v_ref[0]) * g_ref[0]
    o = _dot(a_out, w_ref[0:c]) + _dot(batt_ref[0], w_ref[c:])
    o_ref[0] = x_ref[0] + mod_ref[0][:, 2 * d:3 * d] * o


def _even_out(y, r, v, g, kd, batt, x, mod, ep, tabs):
    b, t, d = x.shape
    c = r.shape[-1]
    nq = batt.shape[-1]
    tm = _row_tile(t, 512)
    tok = lambda w: pl.BlockSpec((1, tm, w), lambda bi, i: (bi, i, 0))
    tok2 = lambda w: pl.BlockSpec((2, 1, tm, w), lambda bi, i: (0, bi, i, 0))
    full = lambda *shape: pl.BlockSpec(shape, lambda bi, i: (0,) * len(shape))
    return pl.pallas_call(
        functools.partial(_even_out_kernel, d=d, c=c),
        grid=(b, t // tm),
        in_specs=[tok(c), tok(c), tok(c), tok(c), tok(c), tok2(c), tok(nq), tok(d),
                  pl.BlockSpec((1, 1, mod.shape[-1]), lambda bi, i: (bi, 0, 0)),
                  full(1, c), full(1, c), full(1, c), full(c, c), full(c + nq, d)],
        out_specs=tok(d),
        out_shape=jax.ShapeDtypeStruct((b, t, d), F32),
        compiler_params=_cparams("parallel", "parallel"),
        name="even_out_proj",
    )(y[0], y[1], r, v, g, kd, batt, x, mod, ep['ln_w'], ep['ln_b'], ep['r_k'], tabs['e'], ep['w_out'])


def _filter_kernel(z_ref, t_ref, w1_ref, b1_ref, w2_ref, b2_ref, w3_ref, b3_ref, fr_ref, wo_ref, dl_ref, o_ref, *, d):
    fr = fr_ref[...]
    h = jnp.sin(fr * (_dot_hi(z_ref[...], w1_ref[...]) + b1_ref[...]))
    h = jnp.sin(fr * (_dot_hi(h, w2_ref[...]) + b2_ref[...]))
    h = jnp.sin(fr * (_dot_hi(h, w3_ref[...]) + b3_ref[...]))
    filt = _dot_hi(h, wo_ref[...])
    modu = jnp.exp(-t_ref[...] * dl_ref[...]) + HY_MOD_SHIFT
    for q in range(o_ref.shape[0]):
        o_ref[q] = filt[:, q * d:(q + 1) * d] * modu


def _hyena_filters(n, op, d):
    t = np.linspace(0.0, 1.0, n, dtype=np.float32)[:, None]
    ang = (2.0 * math.pi * np.arange(n, dtype=np.float32)[:, None] / np.float32(n)).astype(np.float32)
    f = np.linspace(1e-4, HY_BANDS - 1, HY_BANDS, dtype=np.float32)[None, :]
    zfeat = jnp.concatenate([jnp.asarray(t), jnp.cos(jnp.asarray(f * ang)), -jnp.sin(jnp.asarray(f * ang))], axis=-1)
    emb_pad = op['f_w1'].shape[0]
    zfeat = jnp.pad(zfeat, ((0, 0), (0, emb_pad - HY_EMB)))
    deltas = np.abs(np.linspace(math.log(HY_TARGET) / HY_SLOW_PCT, math.log(HY_TARGET) / HY_FAST_PCT, d,
                                dtype=np.float32))[None, :]
    tn = min(n, 256)
    nq = 2 * HY_ORDER
    hf = op['f_w2'].shape[0]
    full = lambda *shape: pl.BlockSpec(shape, lambda i: (0,) * len(shape))
    return pl.pallas_call(
        functools.partial(_filter_kernel, d=d),
        grid=(n // tn,),
        in_specs=[pl.BlockSpec((tn, emb_pad), lambda i: (i, 0)), pl.BlockSpec((tn, 1), lambda i: (i, 0)),
                  full(emb_pad, hf), full(1, hf), full(hf, hf), full(1, hf), full(hf, hf), full(1, hf), full(1, hf),
                  full(hf, nq * d), full(1, d)],
        out_specs=pl.BlockSpec((nq, tn, d), lambda i: (0, i, 0)),
        out_shape=jax.ShapeDtypeStruct((nq, n, d), F32),
        compiler_params=_cparams("parallel"),
        name="hyena_filter",
    )(zfeat, jnp.asarray(t), op['f_w1'], op['f_b1'], op['f_w2'], op['f_b2'], op['f_w3'], op['f_b3'], op['f_freq'],
      op['f_out'], jnp.asarray(deltas))


def _dft(n, rows, cols, sign=-1.0):
    k = np.arange(rows, dtype=np.float64)[:, None]
    m = np.arange(cols, dtype=np.float64)[None, :]
    ang = sign * 2.0 * np.pi * ((k * m) % n) / n
    return np.cos(ang), np.sin(ang)


def _stack(re, im):
    return jnp.asarray(np.concatenate([re, im], axis=0).astype(np.float32)).astype(BF16)


def _fft_tables(n_seq):
    n = 2 * n_seq
    n1 = n // FFT_N2
    f1r, f1i = _dft(n1, n1, n1 // 2)
    h1r, h1i = _dft(n1, n1 // 2, n1, sign=1.0)
    k1 = np.arange(n1, dtype=np.float64)[:, None, None]
    k2 = np.arange(FFT_N2, dtype=np.float64)[None, :, None]
    j2 = np.arange(FFT_N2, dtype=np.float64)[None, None, :]
    ang = -2.0 * np.pi * (((k2 * j2 * n1) + k1 * j2) % n) / n
    gr, gi = np.cos(ang), np.sin(ang)
    g_fwd = np.concatenate([gr, gi], axis=1)
    g_inv = np.concatenate([np.swapaxes(gr, 1, 2), np.swapaxes(gi, 1, 2)], axis=1)
    eye = np.eye(SUBLANES)
    kron = lambda m: np.kron(m, eye)
    return {'f1': _stack(kron(f1r), kron(f1i)), 'h1': _stack(kron(h1r), kron(h1i)),
            'g_fwd': jnp.asarray(g_fwd.astype(np.float32)).astype(BF16),
            'g_inv': jnp.asarray(g_inv.astype(np.float32)).astype(BF16), 'n1': n1}


def _dense_tables(n_seq):
    n = 2 * n_seq
    fr, fi = _dft(n, n, n_seq)
    hr, hi = _dft(n, n_seq, n, sign=1.0)
    return {'f': _stack(fr, fi), 'h': _stack(hr, hi)}


def _fft_a_kernel(f_ref, zr_ref, zi_ref, ar_ref, ai_ref, *, n1, cplx):
    f = f_ref[...]
    half, tj, d = zr_ref.shape[1:]
    m = n1 * SUBLANES
    res_r, res_i = [], []
    for s0 in range(0, tj, SUBLANES):
        js = slice(s0, s0 + SUBLANES)
        zr = zr_ref[0, :, js, :].reshape(half * SUBLANES, d)
        p = jnp.dot(f, zr.astype(BF16), preferred_element_type=F32)
        if cplx:
            zi = zi_ref[0, :, js, :].reshape(half * SUBLANES, d)
            q = jnp.dot(f, zi.astype(BF16), preferred_element_type=F32)
            res_r.append((p[0:m] - q[m:]).reshape(n1, SUBLANES, d))
            res_i.append((p[m:] + q[0:m]).reshape(n1, SUBLANES, d))
        else:
            res_r.append(p[0:m].reshape(n1, SUBLANES, d))
            res_i.append(p[m:].reshape(n1, SUBLANES, d))
    ar_ref[0] = jnp.concatenate(res_r, axis=1).astype(BF16)
    ai_ref[0] = jnp.concatenate(res_i, axis=1).astype(BF16)


def _fft_a(u5, q, tabs, cplx):
    _, s, half, n2, d = u5.shape
    n1 = tabs['n1']
    assert half == n1 // 2 and n2 == FFT_N2
    sp = s // 2 if cplx else s
    tj = 2 * SUBLANES
    zi_map = (lambda si, j: (q, si + sp, 0, j, 0)) if cplx else (lambda si, j: (q, si, 0, j, 0))
    out = pl.BlockSpec((1, n1, tj, d), lambda si, j: (si, 0, j, 0))
    return pl.pallas_call(
        functools.partial(_fft_a_kernel, n1=n1, cplx=cplx),
        grid=(sp, n2 // tj),
        in_specs=[pl.BlockSpec((2 * n1 * SUBLANES, half * SUBLANES), lambda si, j: (0, 0)),
                  pl.BlockSpec((None, 1, half, tj, d), lambda si, j: (q, si, 0, j, 0)),
                  pl.BlockSpec((None, 1, half, tj, d), zi_map)],
        out_specs=[out, out],
        out_shape=[jax.ShapeDtypeStruct((sp, n1, n2, d), BF16)] * 2,
        compiler_params=_cparams("parallel", "parallel"),
        name="fft_stage1",
    )(tabs['f1'], u5, u5)


def _cplx_mm(s, xr, xi, conj):
    p = jnp.dot(s, xr.astype(BF16), preferred_element_type=F32)
    q = jnp.dot(s, xi.astype(BF16), preferred_element_type=F32)
    m = s.shape[0] // 2
    if conj:
        return p[0:m] + q[m:], q[0:m] - p[m:]
    return p[0:m] - q[m:], p[m:] + q[0:m]


def _fft_b_kernel(gf_ref, gi_ref, ar_ref, ai_ref, kr_ref, ki_ref, dr_ref, di_ref):
    kr = kr_ref[0]
    ki = ki_ref[0]
    seqs = range(ar_ref.shape[0])
    c = [_cplx_mm(gf_ref[0], ar_ref[s, 0], ai_ref[s, 0], False) for s in seqs]
    e = [(cr * kr - ci * ki, cr * ki + ci * kr) for cr, ci in c]
    dd = [_cplx_mm(gi_ref[0], er, ei, True) for er, ei in e]
    for s, (dr, di) in zip(seqs, dd):
        dr_ref[s, 0] = dr.astype(BF16)
        di_ref[s, 0] = di.astype(BF16)


def _fft_b(ar4, ai4, kr, ki, order, tabs):
    sp, n1, _, d = ar4.shape
    ns = 2 if sp % 2 == 0 else 1
    blk = pl.BlockSpec((ns, 1, FFT_N2, d), lambda k1, si: (si, k1, 0, 0))
    tab = pl.BlockSpec((1, 2 * FFT_N2, FFT_N2), lambda k1, si: (k1, 0, 0))
    kb = pl.BlockSpec((None, 1, FFT_N2, d), lambda k1, si: (order, k1, 0, 0))
    return pl.pallas_call(
        _fft_b_kernel,
        grid=(n1, sp // ns),
        in_specs=[tab, tab, blk, blk, kb, kb],
        out_specs=[blk, blk],
        out_shape=[jax.ShapeDtypeStruct((sp, n1, FFT_N2, d), BF16)] * 2,
        compiler_params=_cparams("parallel", "arbitrary"),
        name="fft_stage2_filter",
    )(tabs['g_fwd'], tabs['g_inv'], ar4, ai4, kr, ki)


def _fft_c_kernel(h_ref, dr_ref, di_ref, u_ref, x_ref, sk_ref, o_ref):
    h = h_ref[...]
    n1, tj, d = dr_ref.shape[1:]
    half = n1 // 2
    sk = sk_ref[...]
    dr_all = dr_ref[0].astype(F32)
    di_all = di_ref[0].astype(F32)
    for s0 in range(0, tj, SUBLANES):
        js = slice(s0, s0 + SUBLANES)
        yr, yi = _cplx_mm(h, dr_all[:, js, :].reshape(n1 * SUBLANES, d),
                          di_all[:, js, :].reshape(n1 * SUBLANES, d), False)
        for part, y in enumerate((yr, yi)):
            u = u_ref[part, 0, :, js, :]
            o_ref[part, 0, :, js, :] = x_ref[part, 0, :, js, :] * (y.reshape(half, SUBLANES, d) + u * sk)


def _fft_c(dr, di, u6, uq, x6, xq, skip, tabs):
    sp, n1, n2, d = dr.shape
    half = n1 // 2
    tj = 2 * SUBLANES
    pair = lambda q: pl.BlockSpec((None, 2, 1, half, tj, d), lambda si, j: (q, 0, si, 0, j, 0))
    dblk = pl.BlockSpec((1, n1, tj, d), lambda si, j: (si, 0, j, 0))
    out = pl.pallas_call(
        _fft_c_kernel,
        grid=(sp, n2 // tj),
        in_specs=[pl.BlockSpec((2 * half * SUBLANES, n1 * SUBLANES), lambda si, j: (0, 0)), dblk, dblk,
                  pair(uq), pair(xq), pl.BlockSpec((1, d), lambda si, j: (0, 0))],
        out_specs=pl.BlockSpec((2, 1, half, tj, d), lambda si, j: (0, si, 0, j, 0)),
        out_shape=jax.ShapeDtypeStruct((2, sp, half, n2, d), F32),
        compiler_params=_cparams("parallel", "parallel"),
        name="fft_inverse_stage1_gate",
    )(tabs['h1'], dr, di, u6, x6, skip.reshape(1, d))
    return out


def _spec_b_kernel(gf_ref, ar_ref, ai_ref, kr_ref, ki_ref, *, scale):
    fr, fi = _cplx_mm(gf_ref[0], ar_ref[0, 0], ai_ref[0, 0], False)
    gr, gi = _cplx_mm(gf_ref[0], ar_ref[1, 0], ai_ref[1, 0], False)
    kr_ref[0, 0] = (fr + gr) * scale
    ki_ref[0, 0] = (fi - gi) * scale


def _filter_spectrum_fft(filt, tabs, d):
    nq, n, _ = filt.shape
    n1 = tabs['n1']
    ar, ai = _fft_a(filt.reshape(1, nq, n1 // 2, FFT_N2, d), 0, tabs, cplx=False)
    orders = nq // 2
    ar5 = ar.reshape(orders, 2, n1, FFT_N2, d)
    ai5 = ai.reshape(orders, 2, n1, FFT_N2, d)
    blk = pl.BlockSpec((None, 2, 1, FFT_N2, d), lambda o, k1: (o, 0, k1, 0, 0))
    out = pl.BlockSpec((1, 1, FFT_N2, d), lambda o, k1: (o, k1, 0, 0))
    return pl.pallas_call(
        functools.partial(_spec_b_kernel, scale=1.0 / (2 * n)),
        grid=(orders, n1),
        in_specs=[pl.BlockSpec((1, 2 * FFT_N2, FFT_N2), lambda o, k1: (k1, 0, 0)), blk, blk],
        out_specs=[out, out],
        out_shape=[jax.ShapeDtypeStruct((orders, n1, FFT_N2, d), F32)] * 2,
        compiler_params=_cparams("parallel", "parallel"),
        name="filter_spectrum",
    )(tabs['g_fwd'], ar5, ai5)


def _long_conv_fft(u, uq, x, xq, skip, kr, ki, order, tabs):
    _, b, n, d = u.shape
    half = tabs['n1'] // 2
    ar, ai = _fft_a(u.reshape(u.shape[0], b, half, FFT_N2, d), uq, tabs, cplx=True)
    dr, di = _fft_b(ar, ai, kr, ki, order, tabs)
    six = lambda a: a.reshape(a.shape[0], 2, b // 2, half, FFT_N2, d)
    return _fft_c(dr, di, six(u), uq, six(x), xq, skip, tabs).reshape(b, n, d)


def _dense_spec_kernel(f_ref, filt_ref, kr_ref, ki_ref, *, scale):
    f = f_ref[...]
    n = f.shape[0] // 2
    pf = jnp.dot(f, filt_ref[0, 0].astype(BF16), preferred_element_type=F32)
    pg = jnp.dot(f, filt_ref[0, 1].astype(BF16), preferred_element_type=F32)
    kr_ref[0] = (pf[0:n] + pg[0:n]) * scale
    ki_ref[0] = (pf[n:] - pg[n:]) * scale


def _filter_spectrum_dense(filt, tabs, d):
    nq, n, _ = filt.shape
    orders = nq // 2
    f4 = filt.reshape(orders, 2, n, d)
    out = pl.BlockSpec((1, 2 * n, d), lambda o: (o, 0, 0))
    return pl.pallas_call(
        functools.partial(_dense_spec_kernel, scale=1.0 / (2 * n)),
        grid=(orders,),
        in_specs=[pl.BlockSpec((4 * n, n), lambda o: (0, 0)), pl.BlockSpec((1, 2, n, d), lambda o: (o, 0, 0, 0))],
        out_specs=[out, out],
        out_shape=[jax.ShapeDtypeStruct((orders, 2 * n, d), F32)] * 2,
        compiler_params=_cparams("parallel"),
        name="filter_spectrum_dense",
    )(tabs['f'], f4)


def _dense_conv_kernel(f_ref, h_ref, u_ref, x_ref, kr_ref, ki_ref, sk_ref, o_ref):
    cr, ci = _cplx_mm(f_ref[...], u_ref[0, 0], u_ref[1, 0], False)
    kr = kr_ref[...]
    ki = ki_ref[...]
    yr, yi = _cplx_mm(h_ref[...], cr * kr - ci * ki, cr * ki + ci * kr, False)
    sk = sk_ref[...]
    o_ref[0, 0] = x_ref[0, 0] * (yr + u_ref[0, 0] * sk)
    o_ref[1, 0] = x_ref[1, 0] * (yi + u_ref[1, 0] * sk)


def _long_conv_dense(u, x, skip, kr, ki, tabs):
    b, n, d = u.shape
    sp = b // 2
    pair = pl.BlockSpec((2, 1, n, d), lambda si: (0, si, 0, 0))
    kb = pl.BlockSpec((2 * n, d), lambda si: (0, 0))
    out = pl.pallas_call(
        _dense_conv_kernel,
        grid=(sp,),
        in_specs=[pl.BlockSpec((4 * n, n), lambda si: (0, 0)), pl.BlockSpec((2 * n, 2 * n), lambda si: (0, 0)),
                  pair, pair, kb, kb, pl.BlockSpec((1, d), lambda si: (0, 0))],
        out_specs=pair,
        out_shape=jax.ShapeDtypeStruct((2, sp, n, d), F32),
        compiler_params=_cparams("parallel"),
        name="long_conv_dense",
    )(tabs['f'], tabs['h'], u.reshape(2, sp, n, d), x.reshape(2, sp, n, d), kr, ki, skip.reshape(1, d))
    return out.reshape(b, n, d)


def _hyena_mixer(x, g, mod, op, fft_tabs, dense_tabs):
    b, n, d = x.shape
    z = _proj_conv(x, g, mod, op['w_in'], op['b_in'], op['conv_w'], op['conv_b'])
    filt = _hyena_filters(n, op, d)
    if n <= DENSE_FFT_MAX:
        kr, ki = _filter_spectrum_dense(filt, dense_tabs, d)
        y = _long_conv_dense(z[0], z[1], op['skip'][0], kr[0], ki[0], dense_tabs)
        y = _long_conv_dense(y, z[2], op['skip'][1], kr[1], ki[1], dense_tabs)
    else:
        kr, ki = _filter_spectrum_fft(filt, fft_tabs, d)
        y = _long_conv_fft(z, 0, z, 1, op['skip'][0], kr, ki, 0, fft_tabs)
        y = _long_conv_fft(y[None], 0, z, 2, op['skip'][1], kr, ki, 1, fft_tabs)
    return _out_res(y, x, mod, op['w_out'], op['b_out'])


def _even_mixer(x, ctx, g, mod_l, mod_c, ep, tabs, need_ctx):
    rc, vc, kkc, gc, lwc, ac, kdc, qc, kac, vac = _even_prep(ctx, g, mod_c, ep, tabs, rope=False)
    rl, vl, kkl, gl, lwl, al, kdl, ql, kal, val = _even_prep(x, g, mod_l, ep, tabs, rope=True)
    b = x.shape[0]
    c = rl.shape[-1]
    h0 = jnp.zeros((2, b, HEAD_DIM, c), F32)
    y_ctx, s_ctx = _rwkv(rc, vc, kkc, lwc, ac, kdc, h0)
    y_lat, _ = _rwkv(rl, vl, kkl, lwl, al, kdl, s_ctx)
    b_lat = _attention(ql, kal, val, kac, vac, ep['sink'], local=True)
    x_new = _even_out(y_lat, rl, vl, gl, kdl, b_lat, x, mod_l, ep, tabs)
    if not need_ctx:
        return x_new, None
    b_ctx = _attention(qc, kac, vac, kac, vac, ep['sink'], local=False)
    ctx_new = _even_out(y_ctx, rc, vc, gc, kdc, b_ctx, ctx, mod_c, ep, tabs)
    return x_new, ctx_new


def _rope_tables(n_tokens):
    rows = n_tokens // GRID_W
    row = jnp.repeat(jnp.arange(rows), GRID_W).astype(F32)
    col = jnp.tile(jnp.arange(GRID_W), rows).astype(F32)
    n_freq = HEAD_DIM // 4
    inv = ROPE_THETA ** (-jnp.arange(n_freq, dtype=F32) / n_freq)
    ang = jnp.concatenate([row[:, None] * inv, col[:, None] * inv], axis=-1)
    cos, sin = jnp.cos(ang), jnp.sin(ang)
    reps = LANES // HEAD_DIM
    cos_t = jnp.tile(jnp.concatenate([cos, cos], axis=-1), (1, reps))
    sin_t = jnp.tile(jnp.concatenate([-sin, sin], axis=-1), (1, reps))
    return cos_t, sin_t


def _block_ones(width):
    idx = np.arange(width) // HEAD_DIM
    return jnp.asarray((idx[:, None] == idx[None, :]).astype(np.float32)).astype(BF16)


def _lora_pad(w):
    z = jnp.zeros_like(w[0])
    return jnp.stack([jnp.concatenate([w[0], z], axis=0), jnp.concatenate([z, w[1]], axis=0)], axis=0)


def kernel(x, c, ctx, c_ctx, ada_w, ada_b, norm1_g, norm2_g, ffn_up, ffn_conv_w, ffn_conv_b, ffn_down, ev_w_in, ev_mu_prev, ev_mu_next, ev_w0, ev_w2, ev_a0, ev_a2, ev_g2, ev_k_k, ev_k_a, ev_r_k, ev_ln_w, ev_ln_b, ev_q_norm, ev_k_norm, ev_sink, ev_w_out, od_w_in, od_b_in, od_conv_w, od_conv_b, od_f_w1, od_f_b1, od_f_w2, od_f_b2, od_f_w3, od_f_b3, od_f_freq, od_f_out, od_skip, od_w_out, od_b_out):
    bsz, seq, d = x.shape
    lc = ctx.shape[1]
    depth = ada_w.shape[0]
    a_width = ev_k_k.shape[-1]
    nq = ev_sink.shape[-1] * HEAD_DIM
    nk = B_KV_HEADS * HEAD_DIM

    cond = jnp.zeros((BF16_ROWS, d), F32).at[:bsz].set(c).at[bsz].set(c_ctx)
    mod = _ada_mod(cond, ada_w, ada_b)

    cos_t, sin_t = _rope_tables(seq)
    tabs = {'cos': cos_t, 'sin': sin_t, 'e': _block_ones(nq)}
    fft_tabs = _fft_tables(seq) if seq > DENSE_FFT_MAX else None
    dense_lat = _dense_tables(seq) if seq <= DENSE_FFT_MAX else None
    fft_ctx = _fft_tables(lc) if lc > DENSE_FFT_MAX else None
    dense_ctx = _dense_tables(lc) if lc <= DENSE_FFT_MAX else None

    for layer in range(depth):
        need_ctx = layer < depth - 1
        even = layer % 2 == 0
        j = layer // 2
        mod_l = mod[layer, :bsz].reshape(bsz, 1, 6 * d)
        mod_c = jnp.broadcast_to(mod[layer, bsz].reshape(1, 1, 6 * d), (bsz, 1, 6 * d))
        if even:
            ep = {'w_in': ev_w_in[j].astype(BF16), 'mu_prev': ev_mu_prev[j][None], 'mu_next': ev_mu_next[j][None],
                  'w0': ev_w0[j], 'w2pad': _lora_pad(ev_w2[j]).astype(BF16), 'a0': ev_a0[j],
                  'a2pad': _lora_pad(ev_a2[j]).astype(BF16), 'g2': ev_g2[j].astype(BF16),
                  'k_k': ev_k_k[j][None], 'k_a': ev_k_a[j][None], 'r_k': ev_r_k[j].reshape(1, a_width),
                  'ln_w': ev_ln_w[j][None], 'ln_b': ev_ln_b[j][None],
                  'q_norm_t': jnp.tile(ev_q_norm[j], nq // HEAD_DIM)[None],
                  'k_norm_t': jnp.tile(ev_k_norm[j], nk // HEAD_DIM)[None],
                  'sink': ev_sink[j], 'w_out': ev_w_out[j].astype(BF16), 'nq': nq, 'nk': nk}
            x, ctx_new = _even_mixer(x, ctx, norm1_g[layer], mod_l, mod_c, ep, tabs, need_ctx)
        else:
            emb_pad = HEAD_DIM
            op = {'w_in': od_w_in[j].astype(BF16), 'b_in': od_b_in[j], 'conv_w': od_conv_w[j],
                  'conv_b': od_conv_b[j],
                  'f_w1': jnp.pad(od_f_w1[j], ((0, emb_pad - HY_EMB), (0, 0))), 'f_b1': od_f_b1[j][None],
                  'f_w2': od_f_w2[j], 'f_b2': od_f_b2[j][None], 'f_w3': od_f_w3[j], 'f_b3': od_f_b3[j][None],
                  'f_freq': od_f_freq[j][None], 'f_out': od_f_out[j], 'skip': od_skip[j],
                  'w_out': od_w_out[j].astype(BF16), 'b_out': od_b_out[j]}
            ctx_new = _hyena_mixer(ctx, norm1_g[layer], mod_c, op, fft_ctx, dense_ctx) if need_ctx else None
            x = _hyena_mixer(x, norm1_g[layer], mod_l, op, fft_tabs, dense_lat)
        w_up = ffn_up[layer].astype(BF16)
        w_down = ffn_down[layer].astype(BF16)
        x = _conv_ffn(x, norm2_g[layer], mod_l, w_up, ffn_conv_w[layer], ffn_conv_b[layer], w_down)
        if need_ctx:
            ctx = _conv_ffn(ctx_new, norm2_g[layer], mod_c, w_up, ffn_conv_w[layer], ffn_conv_b[layer], w_down)
    return x
```

```python
import functools
import math

import numpy as np
import jax
import jax.numpy as jnp
from jax import lax
from jax.experimental import pallas as pl
from jax.experimental.pallas import tpu as pltpu

F32 = jnp.float32
BF16 = jnp.bfloat16

HEAD_DIM = 64
GRID_W = 64
DECAY_LORA = 64
ICLR_LORA = 64
GATE_LORA = 128
RWKV_GN_EPS = 64e-5
B_KV_HEADS = 2
WINDOW = 128
BLOCK = 128
ROPE_THETA = 10000.0
MASK_VALUE = -1e30
HY_ORDER = 2
HY_EMB = 33
HY_BANDS = (HY_EMB - 1) // 2
HY_TARGET = 1e-2
HY_FAST_PCT = 0.3
HY_SLOW_PCT = 1.5
HY_MOD_SHIFT = 0.05
NORM_EPS = 1e-6

V7X_VMEM_BYTES = 64 * 1024 * 1024
VMEM_LIMIT_BYTES = V7X_VMEM_BYTES * 3 // 4
LANES = 128
SUBLANES = 8
BF16_ROWS = 16
FFN_ROW_PARTS = 2
RWKV_CHUNK = 64
RWKV_PREP_CHUNKS = 8
RWKV_INV_BASE = 8
RWKV_GRAM_PASSES = 1
RWKV_INV_PASSES = 1
RWKV_REST_PASSES = 1
RWKV_SCAN_PASSES = 1
FFT_N2 = 128
DENSE_FFT_MAX = 512

HIGHEST = lax.Precision.HIGHEST


def _cparams(*sem):
    return pltpu.CompilerParams(dimension_semantics=sem, vmem_limit_bytes=VMEM_LIMIT_BYTES)


def _dot(a, b):
    return jnp.dot(a.astype(BF16), b.astype(BF16), preferred_element_type=F32)


def _dot_nt(a, b):
    return lax.dot_general(a.astype(BF16), b.astype(BF16), (((1,), (1,)), ((), ())), preferred_element_type=F32)


def _dot_tn(a, b):
    return lax.dot_general(a.astype(BF16), b.astype(BF16), (((0,), (0,)), ((), ())), preferred_element_type=F32)


def _dot_hi(a, b):
    return jnp.dot(a, b, preferred_element_type=F32, precision=HIGHEST)


def _dot_nt_hi(a, b):
    return lax.dot_general(a, b, (((1,), (1,)), ((), ())), preferred_element_type=F32, precision=HIGHEST)


def _dot_tn_hi(a, b):
    return lax.dot_general(a, b, (((0,), (0,)), ((), ())), preferred_element_type=F32, precision=HIGHEST)


def _segsum(x, e_ref):
    hi = x.astype(BF16)
    lo = (x - hi.astype(F32)).astype(BF16)
    e = e_ref[...]
    return jnp.dot(hi, e, preferred_element_type=F32) + jnp.dot(lo, e, preferred_element_type=F32)


def _norm_mod(x, g, shift, scale):
    xn = x * lax.rsqrt(jnp.mean(x * x, axis=-1, keepdims=True) + NORM_EPS)
    return (xn * g) * (1.0 + scale) + shift


def _row_tile(t, want):
    tm = min(t, want)
    assert t % tm == 0 and tm % BF16_ROWS == 0
    return tm


def _ada_kernel(c_ref, w_ref, b_ref, o_ref):
    c = c_ref[...]
    s = c * jax.nn.sigmoid(c)
    o_ref[0] = _dot(s, w_ref[0]) + b_ref[0]


def _ada_mod(cond, ada_w, ada_b):
    depth, d, n = ada_w.shape
    tn = n // 4
    rows = cond.shape[0]
    return pl.pallas_call(
        _ada_kernel,
        grid=(depth, n // tn),
        in_specs=[pl.BlockSpec((rows, d), lambda l, j: (0, 0)),
                  pl.BlockSpec((1, d, tn), lambda l, j: (l, 0, j)),
                  pl.BlockSpec((1, 1, tn), lambda l, j: (l, 0, j))],
        out_specs=pl.BlockSpec((1, rows, tn), lambda l, j: (l, 0, j)),
        out_shape=jax.ShapeDtypeStruct((depth, rows, n), F32),
        compiler_params=_cparams("parallel", "parallel"),
        name="ada_mod",
    )(cond, ada_w, ada_b.reshape(depth, 1, n))


def _halo_specs(tm, t, d, nargs):
    r = tm // BF16_ROWS
    last = t // BF16_ROWS - 1
    if nargs == 3:
        prev = pl.BlockSpec((1, BF16_ROWS, d), lambda bi, i, j: (bi, jnp.maximum(i * r - 1, 0), 0))
        nxt = pl.BlockSpec((1, BF16_ROWS, d), lambda bi, i, j: (bi, jnp.minimum((i + 1) * r, last), 0))
    else:
        prev = pl.BlockSpec((1, BF16_ROWS, d), lambda bi, i: (bi, jnp.maximum(i * r - 1, 0), 0))
        nxt = pl.BlockSpec((1, BF16_ROWS, d), lambda bi, i: (bi, jnp.minimum((i + 1) * r, last), 0))
    return prev, nxt


def _fill_h(x_ref, xp_ref, xn_ref, g, shift, scale, h_scr, tm):
    h_scr[0:BF16_ROWS] = _norm_mod(xp_ref[0], g, shift, scale).astype(BF16)
    h_scr[BF16_ROWS:BF16_ROWS + tm] = _norm_mod(x_ref[0], g, shift, scale).astype(BF16)
    h_scr[BF16_ROWS + tm:2 * BF16_ROWS + tm] = _norm_mod(xn_ref[0], g, shift, scale).astype(BF16)


def _conv3_rows(u, u_scr, cw, cb, tm, t_total, part=0, nparts=1):
    i = pl.program_id(1)
    o = BF16_ROWS
    rows = tm // nparts
    base = part * (rows + 2 * o)
    head_ok = (i > 0) if part == 0 else True
    tail_ok = (i < t_total // tm - 1) if part == nparts - 1 else True
    u_scr[base:base + o] = jnp.where(head_ok, u[0:o], 0.0)
    u_scr[base + o:base + o + rows] = u[o:o + rows]
    u_scr[base + o + rows:base + 2 * o + rows] = jnp.where(tail_ok, u[o + rows:], 0.0)
    s = base + o
    return (u_scr[s - 1:s - 1 + rows] * cw[0:1] + u_scr[s:s + rows] * cw[1:2] + u_scr[s + 1:s + 1 + rows] * cw[2:3]
            + cb)


def _proj_conv_kernel(x_ref, xp_ref, xn_ref, g_ref, mod_ref, w_ref, b_ref, cw_ref, cb_ref, o_ref, h_scr, u_scr,
                      *, d, tm, t_total, tn):
    m = mod_ref[0]
    _fill_h(x_ref, xp_ref, xn_ref, g_ref[...], m[:, 0:d], m[:, d:2 * d], h_scr, tm)
    nparts = FFN_ROW_PARTS
    rows = tm // nparts
    n = w_ref.shape[1]

    def project(j):
        cols = slice(j * tn, (j + 1) * tn)
        return [jnp.dot(h_scr[p * rows:(p + 1) * rows + 2 * BF16_ROWS], w_ref[:, cols],
                        preferred_element_type=F32) + b_ref[:, cols] for p in range(nparts)]

    nxt_u = project(0)
    for j in range(n // tn):
        cols = slice(j * tn, (j + 1) * tn)
        us = nxt_u
        if (j + 1) * tn < n:
            nxt_u = project(j + 1)
        q, lane0 = divmod(j * tn, d)
        for p in range(nparts):
            o_ref[q, 0, p * rows:(p + 1) * rows, lane0:lane0 + tn] = _conv3_rows(
                us[p], u_scr.at[j % 2], cw_ref[:, cols], cb_ref[:, cols], tm, t_total, p, nparts)


def _proj_conv(x, g, mod, w, bias, cw, cb):
    b, t, d = x.shape
    n = w.shape[1]
    tm = _row_tile(t, 512)
    tn = d // 2
    prev, nxt = _halo_specs(tm, t, d, 2)
    resident = lambda *shape: pl.BlockSpec(shape, lambda bi, i: (0,) * len(shape), pipeline_mode=pl.Buffered(1))
    return pl.pallas_call(
        functools.partial(_proj_conv_kernel, d=d, tm=tm, t_total=t, tn=tn),
        grid=(b, t // tm),
        in_specs=[pl.BlockSpec((1, tm, d), lambda bi, i: (bi, i, 0)), prev, nxt,
                  pl.BlockSpec((1, d), lambda bi, i: (0, 0)),
                  pl.BlockSpec((1, 1, mod.shape[-1]), lambda bi, i: (bi, 0, 0)),
                  resident(d, n), resident(1, n), resident(3, n), resident(1, n)],
        out_specs=pl.BlockSpec((n // d, 1, tm, d), lambda bi, i: (0, bi, i, 0)),
        out_shape=jax.ShapeDtypeStruct((n // d, b, t, d), F32),
        scratch_shapes=[pltpu.VMEM((tm + 2 * BF16_ROWS, d), BF16),
                        pltpu.VMEM((2, tm + 2 * BF16_ROWS * FFN_ROW_PARTS, tn), F32)],
        compiler_params=_cparams("parallel", "parallel"),
        name="hyena_in_proj",
    )(x, x, x, g.reshape(1, d), mod, w, bias.reshape(1, n), cw, cb.reshape(1, n))


def _ffn_kernel(x_ref, xp_ref, xn_ref, g_ref, mod_ref, wup_ref, cw_ref, cb_ref, wd_ref,
                o_ref, h_scr, ug_scr, uv_scr, acc_scr, *, d, tm, t_total, f, tf):
    m = mod_ref[0]
    _fill_h(x_ref, xp_ref, xn_ref, g_ref[...], m[:, 3 * d:4 * d], m[:, 4 * d:5 * d], h_scr, tm)
    nparts = FFN_ROW_PARTS
    rows = tm // nparts
    nf = f // tf

    def up_proj(j):
        hs = [h_scr[p * rows:(p + 1) * rows + 2 * BF16_ROWS] for p in range(nparts)]
        return ([jnp.dot(h, wup_ref[:, j * tf:(j + 1) * tf], preferred_element_type=F32) for h in hs],
                [jnp.dot(h, wup_ref[:, f + j * tf:f + (j + 1) * tf], preferred_element_type=F32) for h in hs])

    nxt_up = up_proj(0)
    for j in range(nf):
        gs = slice(j * tf, (j + 1) * tf)
        vs = slice(f + j * tf, f + (j + 1) * tf)
        ug, uv = nxt_up
        if j + 1 < nf:
            nxt_up = up_proj(j + 1)
        for p in range(nparts):
            gate = _conv3_rows(ug[p], ug_scr.at[j % 2], cw_ref[:, gs], cb_ref[:, gs], tm, t_total, p, nparts)
            val = _conv3_rows(uv[p], uv_scr.at[j % 2], cw_ref[:, vs], cb_ref[:, vs], tm, t_total, p, nparts)
            act = (gate * jax.nn.sigmoid(gate)) * val
            down = jnp.dot(act.astype(BF16), wd_ref[gs, :], preferred_element_type=F32)
            part = slice(p * rows, (p + 1) * rows)
            if j == 0:
                acc_scr[part] = down
            else:
                acc_scr[part] += down
    o_ref[0] = x_ref[0] + m[:, 5 * d:6 * d] * acc_scr[...]


def _conv_ffn(x, g, mod, w_up, cw, cb, w_down):
    b, t, d = x.shape
    f = w_down.shape[0]
    tm = _row_tile(t, 1024)
    tf = 256 if f % 256 == 0 else 128
    prev, nxt = _halo_specs(tm, t, d, 2)
    resident = lambda *shape: pl.BlockSpec(shape, lambda bi, i: (0,) * len(shape), pipeline_mode=pl.Buffered(1))
    u_rows = tm + 2 * BF16_ROWS * FFN_ROW_PARTS
    return pl.pallas_call(
        functools.partial(_ffn_kernel, d=d, tm=tm, t_total=t, f=f, tf=tf),
        grid=(b, t // tm),
        in_specs=[pl.BlockSpec((1, tm, d), lambda bi, i: (bi, i, 0)), prev, nxt,
                  pl.BlockSpec((1, d), lambda bi, i: (0, 0)),
                  pl.BlockSpec((1, 1, mod.shape[-1]), lambda bi, i: (bi, 0, 0)),
                  resident(d, 2 * f), resident(3, 2 * f), resident(1, 2 * f), resident(f, d)],
        out_specs=pl.BlockSpec((1, tm, d), lambda bi, i: (bi, i, 0)),
        out_shape=jax.ShapeDtypeStruct((b, t, d), F32),
        scratch_shapes=[pltpu.VMEM((tm + 2 * BF16_ROWS, d), BF16),
                        pltpu.VMEM((2, u_rows, tf), F32),
                        pltpu.VMEM((2, u_rows, tf), F32),
                        pltpu.VMEM((tm, d), F32)],
        compiler_params=_cparams("parallel", "parallel"),
        name="conv_ffn",
    )(x, x, x, g.reshape(1, d), mod, w_up, cw, cb.reshape(1, 2 * f), w_down)


def _out_res_kernel(y_ref, x_ref, mod_ref, w_ref, b_ref, o_ref, *, d):
    o = _dot(y_ref[0], w_ref[...]) + b_ref[...]
    o_ref[0] = x_ref[0] + mod_ref[0][:, 2 * d:3 * d] * o


def _out_res(y, x, mod, w, bias):
    b, t, d = x.shape
    tm = _row_tile(t, 512)
    return pl.pallas_call(
        functools.partial(_out_res_kernel, d=d),
        grid=(b, t // tm),
        in_specs=[pl.BlockSpec((1, tm, d), lambda bi, i: (bi, i, 0)),
                  pl.BlockSpec((1, tm, d), lambda bi, i: (bi, i, 0)),
                  pl.BlockSpec((1, 1, mod.shape[-1]), lambda bi, i: (bi, 0, 0)),
                  pl.BlockSpec((d, d), lambda bi, i: (0, 0)),
                  pl.BlockSpec((1, d), lambda bi, i: (0, 0))],
        out_specs=pl.BlockSpec((1, tm, d), lambda bi, i: (bi, i, 0)),
        out_shape=jax.ShapeDtypeStruct((b, t, d), F32),
        compiler_params=_cparams("parallel", "parallel"),
        name="out_proj_residual",
    )(y, x, mod, w, bias.reshape(1, d))


def _even_prep_kernel(x_ref, xp_ref, xn_ref, gn_ref, mod_ref, win_ref, mup_ref, mun_ref, w0_ref, w2_ref, a0_ref,
                      a2_ref, g2_ref, kk_ref, ka_ref, qn_ref, kn_ref, cos_ref, sin_ref, e_ref,
                      r_ref, v_ref, kkn_ref, g_ref, lw_ref, a_ref, kd_ref, q_ref, katt_ref, vatt_ref,
                      h_scr, p_scr, *, tm, d, c, a_in, rope):
    i = pl.program_id(1)
    nt = pl.num_programs(1)
    m = mod_ref[0]
    _fill_h(x_ref, xp_ref, xn_ref, gn_ref[...], m[:, 0:d], m[:, d:2 * d], h_scr, tm)
    p_scr[...] = jnp.dot(h_scr[...], win_ref[...], preferred_element_type=F32)
    o = BF16_ROWS
    pa = p_scr[o:o + tm, 0:a_in]
    rows = lax.broadcasted_iota(jnp.int32, (tm, 1), 0)
    prev = jnp.where((rows == 0) & (i == 0), 0.0, p_scr[o - 1:o - 1 + tm, 0:a_in])
    nxt = jnp.where((rows == tm - 1) & (i == nt - 1), 0.0, p_scr[o + 1:o + 1 + tm, 0:a_in])
    za = pa + mup_ref[...] * (prev - pa) + mun_ref[...] * (nxt - pa)

    r = za[:, 0:c]
    k = za[:, c:2 * c]
    v = za[:, 2 * c:3 * c]
    wd = jnp.tanh(za[:, 3 * c:3 * c + 2 * DECAY_LORA])
    ad = za[:, 3 * c + 2 * DECAY_LORA:3 * c + 2 * DECAY_LORA + 2 * ICLR_LORA]
    gd = za[:, 3 * c + 2 * DECAY_LORA + 2 * ICLR_LORA:a_in]
    r_ref[0] = r
    v_ref[0] = v
    g_ref[0] = _dot(jax.nn.sigmoid(gd), g2_ref[...])
    kkv = k * kk_ref[...]
    ss = _segsum(kkv * kkv, e_ref)
    kkn_ref[0] = kkv / jnp.maximum(jnp.sqrt(ss), 1e-12)
    for dd in range(2):
        y = w0_ref[dd:dd + 1] + _dot(wd, w2_ref[dd])
        lw_ref[dd, 0] = -math.exp(-0.5) * jax.nn.sigmoid(y)
        a = jax.nn.sigmoid(a0_ref[dd:dd + 1] + _dot(ad, a2_ref[dd]))
        a_ref[dd, 0] = a
        kd_ref[dd, 0] = k * (1.0 + (a - 1.0) * ka_ref[...])

    pb = p_scr[o:o + tm, a_in:]
    nq = q_ref.shape[-1]
    nk = katt_ref.shape[-1]
    q = pb[:, 0:nq]
    kat = pb[:, nq:nq + nk]
    vatt_ref[0] = pb[:, nq + nk:]
    half = HEAD_DIM // 2

    def norm_rope(x, gain, e):
        w = x.shape[-1]
        ms = _segsum(x * x, e) * (1.0 / HEAD_DIM)
        xn = x * lax.rsqrt(ms + NORM_EPS) * gain
        if not rope:
            return xn
        lane = lax.broadcasted_iota(jnp.int32, (1, w), 1)
        first = (lane % HEAD_DIM) < half
        swapped = jnp.where(first, pltpu.roll(xn, w - half, 1), pltpu.roll(xn, half, 1))
        reps = w // LANES
        cos = jnp.concatenate([cos_ref[...]] * reps, axis=1) if reps > 1 else cos_ref[...]
        sin = jnp.concatenate([sin_ref[...]] * reps, axis=1) if reps > 1 else sin_ref[...]
        return xn * cos + swapped * sin

    q_ref[0] = norm_rope(q, qn_ref[...], e_ref[...])
    katt_ref[0] = norm_rope(kat, kn_ref[...], e_ref[0:nk, 0:nk])


def _even_prep(x, g, mod, ep, tabs, rope):
    b, t, d = x.shape
    n = ep['w_in'].shape[1]
    c = ep['k_k'].shape[-1]
    a_in = ep['mu_prev'].shape[-1]
    nq = ep['nq']
    nk = ep['nk']
    tm = _row_tile(t, 256)
    prev, nxt = _halo_specs(tm, t, d, 2)
    full = lambda *shape: pl.BlockSpec(shape, lambda bi, i: (0,) * len(shape))
    tok = lambda w: pl.BlockSpec((1, tm, w), lambda bi, i: (bi, i, 0))
    tok2 = lambda w: pl.BlockSpec((2, 1, tm, w), lambda bi, i: (0, bi, i, 0))
    sd = lambda *shape: jax.ShapeDtypeStruct(shape, F32)
    return pl.pallas_call(
        functools.partial(_even_prep_kernel, tm=tm, d=d, c=c, a_in=a_in, rope=rope),
        grid=(b, t // tm),
        in_specs=[tok(d), prev, nxt, full(1, d), pl.BlockSpec((1, 1, mod.shape[-1]), lambda bi, i: (bi, 0, 0)),
                  full(d, n), full(1, a_in), full(1, a_in), full(2, c), full(2, 2 * DECAY_LORA, c),
                  full(2, c), full(2, 2 * ICLR_LORA, c), full(GATE_LORA, c), full(1, c), full(1, c),
                  full(1, nq), full(1, nk),
                  pl.BlockSpec((tm, LANES), lambda bi, i: (i, 0)), pl.BlockSpec((tm, LANES), lambda bi, i: (i, 0)),
                  full(nq, nq)],
        out_specs=[tok(c), tok(c), tok(c), tok(c), tok2(c), tok2(c), tok2(c), tok(nq), tok(nk), tok(nk)],
        out_shape=[sd(b, t, c), sd(b, t, c), sd(b, t, c), sd(b, t, c), sd(2, b, t, c), sd(2, b, t, c),
                   sd(2, b, t, c), sd(b, t, nq), sd(b, t, nk), sd(b, t, nk)],
        scratch_shapes=[pltpu.VMEM((tm + 2 * BF16_ROWS, d), BF16), pltpu.VMEM((tm + 2 * BF16_ROWS, n), F32)],
        compiler_params=_cparams("parallel", "parallel"),
        name="even_in_proj_prep",
    )(x, x, x, g.reshape(1, d), mod, ep['w_in'], ep['mu_prev'], ep['mu_next'], ep['w0'], ep['w2pad'], ep['a0'], ep['a2pad'], ep['g2'], ep['k_k'],
      ep['k_a'], ep['q_norm_t'], ep['k_norm_t'], tabs['cos'][:t], tabs['sin'][:t], tabs['e'])


HEADS_PER_GROUP = 2
GROUP_LANES = HEADS_PER_GROUP * HEAD_DIM


def _split_bf16(x):
    hi = x.astype(BF16)
    return hi, (x - hi.astype(F32)).astype(BF16)


def _block_diag(x, bmask):
    return jnp.where(bmask, jnp.concatenate([x] * HEADS_PER_GROUP, axis=0), jnp.zeros((), x.dtype))


def _head_mm(lhs, rhs, bmask, passes, nt=False):
    dn = (((1,), (1,)), ((), ())) if nt else (((1,), (0,)), ((), ()))
    if passes == 1:
        return lax.dot_general(lhs.astype(BF16), _block_diag(rhs.astype(BF16), bmask), dn, preferred_element_type=F32)
    lh, ll = _split_bf16(lhs)
    rh, rl = _split_bf16(rhs)
    m = lhs.shape[0]
    top = lax.dot_general(jnp.concatenate([lh, ll], axis=0), _block_diag(rh, bmask), dn, preferred_element_type=F32)
    return top[0:m] + top[m:] + lax.dot_general(lh, _block_diag(rl, bmask), dn, preferred_element_type=F32)


def _head_mm_tn(lhs, rhs, lane_head, passes):
    dn = (((0,), (0,)), ((), ()))
    if passes == 1:
        full = lax.dot_general(lhs.astype(BF16), rhs.astype(BF16), dn, preferred_element_type=F32)
    else:
        lh, ll = _split_bf16(lhs)
        rh, rl = _split_bf16(rhs)
        full = (lax.dot_general(jnp.concatenate([lh, ll], axis=0), jnp.concatenate([rh, rh], axis=0), dn,
                                preferred_element_type=F32)
                + lax.dot_general(lh, rl, dn, preferred_element_type=F32))
    out = jnp.where(lane_head == 0, full[0:HEAD_DIM], 0.0)
    for h in range(1, HEADS_PER_GROUP):
        out = out + jnp.where(lane_head == h, full[h * HEAD_DIM:(h + 1) * HEAD_DIM], 0.0)
    return out


def _group_masks():
    r = lax.broadcasted_iota(jnp.int32, (GROUP_LANES, GROUP_LANES), 0)
    c = lax.broadcasted_iota(jnp.int32, (GROUP_LANES, GROUP_LANES), 1)
    bmask = (r // HEAD_DIM) == (c // HEAD_DIM)
    lane_head = lax.broadcasted_iota(jnp.int32, (1, GROUP_LANES), 1) // HEAD_DIM
    return bmask, lane_head


def _rwkv_prep_kernel(r_ref, v_ref, kk_ref, lw_ref, a_ref, kd_ref, r2_ref, yl_ref, t_ref, z_ref, *, cs):
    sgn = 1 - 2 * pl.program_id(0)
    row = lax.broadcasted_iota(jnp.int32, (cs, cs), 0)
    col = lax.broadcasted_iota(jnp.int32, (cs, cs), 1)
    tri = (((row - col) * sgn) >= 0).astype(F32)
    bmask, lane_head = _group_masks()
    t_idx = lax.broadcasted_iota(jnp.int32, (cs, GROUP_LANES), 0)
    s_idx = lax.broadcasted_iota(jnp.int32, (cs, GROUP_LANES), 1) % HEAD_DIM
    diff = (t_idx - s_idx) * sgn
    strict = diff > 0
    incl = diff >= 0
    eye = (diff == 0).astype(F32)
    base = min(RWKV_INV_BASE, cs)
    same_base = (t_idx // base) == (s_idx // base)
    groups = r_ref.shape[-1] // GROUP_LANES

    ch = []
    for c0 in range(0, r_ref.shape[1], cs):
        rows = slice(c0, c0 + cs)
        r, v, kk = r_ref[0, rows], v_ref[0, rows], kk_ref[0, rows]
        lw, a, kd = lw_ref[0, 0, rows], a_ref[0, 0, rows], kd_ref[0, 0, rows]
        g = _dot_hi(tri, lw)
        gp = g - lw
        gref = g[cs // 2:cs // 2 + 1]
        gend = jnp.sum(lw, axis=0, keepdims=True)
        bvec = kk * a
        full = {'a_t': -kk * jnp.exp(gp - gref), 'a_0': -kk * jnp.exp(gp),
                'b_t': bvec * jnp.exp(gref - g), 'b_e': bvec * jnp.exp(gend - g),
                'k_t': kd * jnp.exp(gref - g), 'k_e': kd * jnp.exp(gend - g),
                'r_t': r * jnp.exp(g - gref), 'r_0': r * jnp.exp(g), 'v': v,
                'wend': jnp.broadcast_to(jnp.exp(gend), (cs, v.shape[-1]))}
        for gi in range(groups):
            sl = slice(gi * GROUP_LANES, (gi + 1) * GROUP_LANES)
            c = {k: x[:, sl] for k, x in full.items()}
            c['rows'], c['sl'] = rows, sl
            ch.append(c)

    mm = lambda x, y, passes, nt=False: [_head_mm(p_, q_, bmask, passes, nt) for p_, q_ in zip(x, y)]
    get = lambda k: [c[k] for c in ch]
    ar = [jnp.concatenate([c['a_t'], c['r_t']], axis=0) for c in ch]
    gb = mm(ar, get('b_t'), RWKV_GRAM_PASSES, True)
    gk = mm(ar, get('k_t'), RWKV_GRAM_PASSES, True)
    nmat = [jnp.where(strict, x[0:cs], 0.0) for x in gb]
    g_rb = [jnp.where(incl, x[cs:], 0.0) for x in gb]
    g_k = [jnp.concatenate([jnp.where(strict, x[0:cs], 0.0), jnp.where(incl, x[cs:], 0.0)], axis=0) for x in gk]
    npow = [jnp.where(same_base, x, 0.0) for x in nmat]
    p = [eye + x for x in npow]
    for _ in range(int(math.log2(base)) - 1):
        npow = mm(npow, npow, RWKV_INV_PASSES)
        p = [x + y for x, y in zip(p, mm(npow, p, RWKV_INV_PASSES))]
    m = base
    while m < cs:
        off = ((t_idx // m) != (s_idx // m)) & ((t_idx // (2 * m)) == (s_idx // (2 * m)))
        q = mm([jnp.where(off, x, 0.0) for x in nmat], p, RWKV_INV_PASSES)
        p = [x + y for x, y in zip(p, mm(p, q, RWKV_INV_PASSES))]
        m *= 2
    vg = get('v')
    a2 = mm(p, get('a_0'), RWKV_REST_PASSES)
    kv = mm(g_k, vg, RWKV_REST_PASSES)
    u_v = mm(p, [x[0:cs] for x in kv], RWKV_REST_PASSES)
    r2 = mm(g_rb, a2, RWKV_REST_PASSES)
    yl1 = mm(g_rb, u_v, RWKV_REST_PASSES)
    yl2 = [x[cs:] for x in kv]
    tt = [_head_mm_tn(x, c['b_e'], lane_head, RWKV_REST_PASSES) for x, c in zip(a2, ch)]
    zz = [_head_mm_tn(jnp.concatenate([u, c['v']], axis=0), jnp.concatenate([c['b_e'], c['k_e']], axis=0),
                      lane_head, RWKV_REST_PASSES) for u, c in zip(u_v, ch)]
    for i, c in enumerate(ch):
        rows, sl = c['rows'], c['sl']
        r2_ref[0, 0, rows, sl] = (c['r_0'] + r2[i]).astype(r2_ref.dtype)
        yl_ref[0, 0, rows, sl] = yl1[i] + yl2[i]
        t_ref[0, 0, rows, sl] = (eye * c['wend'] + tt[i]).astype(t_ref.dtype)
        z_ref[0, 0, rows, sl] = zz[i]


def _rwkv_scan_kernel(h0_ref, r2f_ref, ylf_ref, tf_ref, zf_ref, r2b_ref, ylb_ref, tb_ref, zb_ref,
                      yf_ref, yb_ref, ht_ref, s_scr):
    ci = pl.program_id(0)

    @pl.when(ci == 0)
    def _():
        s_scr[...] = h0_ref[...]

    bmask, _ = _group_masks()
    ins = ((r2f_ref, ylf_ref, tf_ref, zf_ref, yf_ref), (r2b_ref, ylb_ref, tb_ref, zb_ref, yb_ref))
    for d, (r2_ref, yl_ref, t_ref, z_ref, y_ref) in enumerate(ins):
        for b in range(s_scr.shape[1]):
            for gi in range(s_scr.shape[-1] // GROUP_LANES):
                sl = slice(gi * GROUP_LANES, (gi + 1) * GROUP_LANES)
                s = s_scr[d, b, :, sl]
                y_ref[0, b, :, sl] = yl_ref[0, b, :, sl] + _head_mm(r2_ref[0, b, :, sl], s, bmask, 1, nt=True)
                s_scr[d, b, :, sl] = (_head_mm(s, t_ref[0, b, :, sl], bmask, RWKV_SCAN_PASSES)
                                      + z_ref[0, b, :, sl])

    @pl.when(ci == pl.num_programs(0) - 1)
    def _():
        ht_ref[...] = s_scr[...]


def _rwkv(r, v, kk, lw, a, kd, h0):
    b, t, c = r.shape
    cs = min(RWKV_CHUNK, t)
    assert cs == HEAD_DIM and c % GROUP_LANES == 0
    nc = t // cs
    rows = cs * math.gcd(nc, RWKV_PREP_CHUNKS)
    tok = pl.BlockSpec((1, rows, c), lambda d, bi, ci: (bi, ci, 0))
    tok2 = pl.BlockSpec((1, 1, rows, c), lambda d, bi, ci: (d, bi, ci, 0))
    wide = jax.ShapeDtypeStruct((2, b, t, c), F32)
    mm_operand = jax.ShapeDtypeStruct((2, b, t, c), BF16 if RWKV_SCAN_PASSES == 1 else F32)
    r2, yl, tt, zz = pl.pallas_call(
        functools.partial(_rwkv_prep_kernel, cs=cs),
        grid=(2, b, t // rows),
        in_specs=[tok, tok, tok, tok2, tok2, tok2],
        out_specs=[tok2, tok2, tok2, tok2],
        out_shape=[mm_operand, wide, mm_operand, wide],
        compiler_params=_cparams("parallel", "parallel", "parallel"),
        name="rwkv7_chunk_prep",
    )(r, v, kk, lw, a, kd)

    fwd = pl.BlockSpec((1, b, cs, c), lambda ci: (0, 0, ci, 0))
    bwd = pl.BlockSpec((1, b, cs, c), lambda ci: (1, 0, nc - 1 - ci, 0))
    st = pl.BlockSpec((2, b, HEAD_DIM, c), lambda ci: (0, 0, 0, 0))
    yf, yb, ht = pl.pallas_call(
        _rwkv_scan_kernel,
        grid=(nc,),
        in_specs=[st, fwd, fwd, fwd, fwd, bwd, bwd, bwd, bwd],
        out_specs=[pl.BlockSpec((1, b, cs, c), lambda ci: (0, 0, ci, 0)),
                   pl.BlockSpec((1, b, cs, c), lambda ci: (0, 0, nc - 1 - ci, 0)), st],
        out_shape=[jax.ShapeDtypeStruct((1, b, t, c), F32), jax.ShapeDtypeStruct((1, b, t, c), F32),
                   jax.ShapeDtypeStruct((2, b, HEAD_DIM, c), F32)],
        scratch_shapes=[pltpu.VMEM((2, b, HEAD_DIM, c), F32)],
        compiler_params=_cparams("arbitrary"),
        name="rwkv7_state_scan",
    )(h0, r2, yl, tt, zz, r2, yl, tt, zz)
    return (yf.reshape(b, t, c), yb.reshape(b, t, c)), ht


def _attn_kernel(sink_ref, q_ref, kp_ref, kc_ref, kn_ref, vp_ref, vc_ref, vn_ref, kx_ref, vx_ref, o_ref,
                 *, local, group):
    i = pl.program_id(1)
    nb = pl.num_programs(1)
    scale = HEAD_DIM ** -0.5
    assert math.frexp(scale)[0] == 0.5
    q = q_ref[0] * scale
    if local:
        k_all = jnp.concatenate([kp_ref[0], kc_ref[0], kn_ref[0], kx_ref[0]], axis=0)
        v_all = jnp.concatenate([vp_ref[0], vc_ref[0], vn_ref[0], vx_ref[0]], axis=0)
        qi = lax.broadcasted_iota(jnp.int32, (BLOCK, BLOCK), 0)
        kj = lax.broadcasted_iota(jnp.int32, (BLOCK, BLOCK), 1)
        assert WINDOW == BLOCK
        valid_prev = (kj >= qi) & (i > 0)
        valid_next = (kj <= qi) & (i < nb - 1)
    else:
        k_all = kx_ref[0]
        v_all = vx_ref[0]
    heads = range(q.shape[-1] // HEAD_DIM)
    kv = lambda x, h: x[:, (h // group) * HEAD_DIM:(h // group + 1) * HEAD_DIM]
    s = [_dot_nt(q[:, h * HEAD_DIM:(h + 1) * HEAD_DIM], kv(k_all, h)) for h in heads]
    if local:
        s = [jnp.concatenate([jnp.where(valid_prev, x[:, 0:BLOCK], MASK_VALUE), x[:, BLOCK:2 * BLOCK],
                              jnp.where(valid_next, x[:, 2 * BLOCK:3 * BLOCK], MASK_VALUE), x[:, 3 * BLOCK:]],
                             axis=1) for x in s]
    sink = [sink_ref[h] for h in heads]
    m = [jnp.maximum(jnp.max(x, axis=-1, keepdims=True), sk) for x, sk in zip(s, sink)]
    e = [jnp.exp(x - y) for x, y in zip(s, m)]
    den = [jnp.sum(x, axis=-1, keepdims=True) + jnp.exp(sk - y) for x, y, sk in zip(e, m, sink)]
    outs = [_dot(x, kv(v_all, h)) / dn for x, h, dn in zip(e, heads, den)]
    o_ref[0] = jnp.concatenate(outs, axis=1)


def _attention(q, k, v, kx, vx, sink, local):
    b, t, nq = q.shape
    nk = k.shape[-1]
    nb = t // BLOCK
    group = (nq // HEAD_DIM) // (nk // HEAD_DIM)
    lx = kx.shape[1]
    kv = lambda f: pl.BlockSpec((1, BLOCK, nk), f)
    pf = lambda bi, i: (bi, jnp.maximum(i - 1, 0), 0)
    cf = lambda bi, i: (bi, i, 0)
    nf = lambda bi, i: (bi, jnp.minimum(i + 1, nb - 1), 0)
    ctx = pl.BlockSpec((1, lx, nk), lambda bi, i: (bi, 0, 0))
    return pl.pallas_call(
        functools.partial(_attn_kernel, local=local, group=group),
        grid=(b, nb),
        in_specs=[pl.BlockSpec(memory_space=pltpu.SMEM),
                  pl.BlockSpec((1, BLOCK, nq), cf), kv(pf), kv(cf), kv(nf), kv(pf), kv(cf), kv(nf), ctx, ctx],
        out_specs=pl.BlockSpec((1, BLOCK, nq), cf),
        out_shape=jax.ShapeDtypeStruct((b, t, nq), F32),
        compiler_params=_cparams("parallel", "parallel"),
        name="window_attention" if local else "context_attention",
    )(sink, q, k, k, k, v, v, v, kx, vx)


def _even_out_kernel(yf_ref, yb_ref, r_ref, v_ref, g_ref, kd_ref, batt_ref, x_ref, mod_ref, lnw_ref, lnb_ref, rk_ref, e_ref,
                     w_ref, o_ref, *, d, c):
    y = yf_ref[0] + yb_ref[0]
    inv = 1.0 / HEAD_DIM
    mu = _segsum(y, e_ref) * inv
    yc = y - mu
    var = _segsum(yc * yc, e_ref) * inv
    yn = yc * lax.rsqrt(var + RWKV_GN_EPS) * lnw_ref[...] + lnb_ref[...]
    bonus = _segsum(r_ref[0] * (kd_ref[0, 0] + kd_ref[1, 0]) * rk_ref[...], e_ref)
    a_out = (yn + bonus * v_ref[0]) * g_ref[0]
    o = _dot(a_out, w_ref[0:c]) + _dot(batt_ref[0], w_ref[c:])
    o_ref[0] = x_ref[0] + mod_ref[0][:, 2 * d:3 * d] * o


def _even_out(y, r, v, g, kd, batt, x, mod, ep, tabs):
    b, t, d = x.shape
    c = r.shape[-1]
    nq = batt.shape[-1]
    tm = _row_tile(t, 512)
    tok = lambda w: pl.BlockSpec((1, tm, w), lambda bi, i: (bi, i, 0))
    tok2 = lambda w: pl.BlockSpec((2, 1, tm, w), lambda bi, i: (0, bi, i, 0))
    full = lambda *shape: pl.BlockSpec(shape, lambda bi, i: (0,) * len(shape))
    return pl.pallas_call(
        functools.partial(_even_out_kernel, d=d, c=c),
        grid=(b, t // tm),
        in_specs=[tok(c), tok(c), tok(c), tok(c), tok(c), tok2(c), tok(nq), tok(d),
                  pl.BlockSpec((1, 1, mod.shape[-1]), lambda bi, i: (bi, 0, 0)),
                  full(1, c), full(1, c), full(1, c), full(c, c), full(c + nq, d)],
        out_specs=tok(d),
        out_shape=jax.ShapeDtypeStruct((b, t, d), F32),
        compiler_params=_cparams("parallel", "parallel"),
        name="even_out_proj",
    )(y[0], y[1], r, v, g, kd, batt, x, mod, ep['ln_w'], ep['ln_b'], ep['r_k'], tabs['e'], ep['w_out'])


def _filter_kernel(z_ref, t_ref, w1_ref, b1_ref, w2_ref, b2_ref, w3_ref, b3_ref, fr_ref, wo_ref, dl_ref, o_ref, *, d):
    fr = fr_ref[...]
    h = jnp.sin(fr * (_dot_hi(z_ref[...], w1_ref[...]) + b1_ref[...]))
    h = jnp.sin(fr * (_dot_hi(h, w2_ref[...]) + b2_ref[...]))
    h = jnp.sin(fr * (_dot_hi(h, w3_ref[...]) + b3_ref[...]))
    filt = _dot_hi(h, wo_ref[...])
    modu = jnp.exp(-t_ref[...] * dl_ref[...]) + HY_MOD_SHIFT
    for q in range(o_ref.shape[0]):
        o_ref[q] = filt[:, q * d:(q + 1) * d] * modu


def _hyena_filters(n, op, d):
    t = np.linspace(0.0, 1.0, n, dtype=np.float32)[:, None]
    ang = (2.0 * math.pi * np.arange(n, dtype=np.float32)[:, None] / np.float32(n)).astype(np.float32)
    f = np.linspace(1e-4, HY_BANDS - 1, HY_BANDS, dtype=np.float32)[None, :]
    zfeat = jnp.concatenate([jnp.asarray(t), jnp.cos(jnp.asarray(f * ang)), -jnp.sin(jnp.asarray(f * ang))], axis=-1)
    emb_pad = op['f_w1'].shape[0]
    zfeat = jnp.pad(zfeat, ((0, 0), (0, emb_pad - HY_EMB)))
    deltas = np.abs(np.linspace(math.log(HY_TARGET) / HY_SLOW_PCT, math.log(HY_TARGET) / HY_FAST_PCT, d,
                                dtype=np.float32))[None, :]
    tn = min(n, 256)
    nq = 2 * HY_ORDER
    hf = op['f_w2'].shape[0]
    full = lambda *shape: pl.BlockSpec(shape, lambda i: (0,) * len(shape))
    return pl.pallas_call(
        functools.partial(_filter_kernel, d=d),
        grid=(n // tn,),
        in_specs=[pl.BlockSpec((tn, emb_pad), lambda i: (i, 0)), pl.BlockSpec((tn, 1), lambda i: (i, 0)),
                  full(emb_pad, hf), full(1, hf), full(hf, hf), full(1, hf), full(hf, hf), full(1, hf), full(1, hf),
                  full(hf, nq * d), full(1, d)],
        out_specs=pl.BlockSpec((nq, tn, d), lambda i: (0, i, 0)),
        out_shape=jax.ShapeDtypeStruct((nq, n, d), F32),
        compiler_params=_cparams("parallel"),
        name="hyena_filter",
    )(zfeat, jnp.asarray(t), op['f_w1'], op['f_b1'], op['f_w2'], op['f_b2'], op['f_w3'], op['f_b3'], op['f_freq'],
      op['f_out'], jnp.asarray(deltas))


def _dft(n, rows, cols, sign=-1.0):
    k = np.arange(rows, dtype=np.float64)[:, None]
    m = np.arange(cols, dtype=np.float64)[None, :]
    ang = sign * 2.0 * np.pi * ((k * m) % n) / n
    return np.cos(ang), np.sin(ang)


def _stack(re, im):
    return jnp.asarray(np.concatenate([re, im], axis=0).astype(np.float32)).astype(BF16)


def _fft_tables(n_seq):
    n = 2 * n_seq
    n1 = n // FFT_N2
    f1r, f1i = _dft(n1, n1, n1 // 2)
    h1r, h1i = _dft(n1, n1 // 2, n1, sign=1.0)
    k1 = np.arange(n1, dtype=np.float64)[:, None, None]
    k2 = np.arange(FFT_N2, dtype=np.float64)[None, :, None]
    j2 = np.arange(FFT_N2, dtype=np.float64)[None, None, :]
    ang = -2.0 * np.pi * (((k2 * j2 * n1) + k1 * j2) % n) / n
    gr, gi = np.cos(ang), np.sin(ang)
    g_fwd = np.concatenate([gr, gi], axis=1)
    g_inv = np.concatenate([np.swapaxes(gr, 1, 2), np.swapaxes(gi, 1, 2)], axis=1)
    eye = np.eye(SUBLANES)
    kron = lambda m: np.kron(m, eye)
    return {'f1': _stack(kron(f1r), kron(f1i)), 'h1': _stack(kron(h1r), kron(h1i)),
            'g_fwd': jnp.asarray(g_fwd.astype(np.float32)).astype(BF16),
            'g_inv': jnp.asarray(g_inv.astype(np.float32)).astype(BF16), 'n1': n1}


def _dense_tables(n_seq):
    n = 2 * n_seq
    fr, fi = _dft(n, n, n_seq)
    hr, hi = _dft(n, n_seq, n, sign=1.0)
    return {'f': _stack(fr, fi), 'h': _stack(hr, hi)}


def _fft_a_kernel(f_ref, zr_ref, zi_ref, ar_ref, ai_ref, *, n1, cplx):
    f = f_ref[...]
    half, tj, d = zr_ref.shape[1:]
    m = n1 * SUBLANES
    res_r, res_i = [], []
    for s0 in range(0, tj, SUBLANES):
        js = slice(s0, s0 + SUBLANES)
        zr = zr_ref[0, :, js, :].reshape(half * SUBLANES, d)
        p = jnp.dot(f, zr.astype(BF16), preferred_element_type=F32)
        if cplx:
            zi = zi_ref[0, :, js, :].reshape(half * SUBLANES, d)
            q = jnp.dot(f, zi.astype(BF16), preferred_element_type=F32)
            res_r.append((p[0:m] - q[m:]).reshape(n1, SUBLANES, d))
            res_i.append((p[m:] + q[0:m]).reshape(n1, SUBLANES, d))
        else:
            res_r.append(p[0:m].reshape(n1, SUBLANES, d))
            res_i.append(p[m:].reshape(n1, SUBLANES, d))
    ar_ref[0] = jnp.concatenate(res_r, axis=1).astype(BF16)
    ai_ref[0] = jnp.concatenate(res_i, axis=1).astype(BF16)


def _fft_a(u5, q, tabs, cplx):
    _, s, half, n2, d = u5.shape
    n1 = tabs['n1']
    assert half == n1 // 2 and n2 == FFT_N2
    sp = s // 2 if cplx else s
    tj = 2 * SUBLANES
    zi_map = (lambda si, j: (q, si + sp, 0, j, 0)) if cplx else (lambda si, j: (q, si, 0, j, 0))
    out = pl.BlockSpec((1, n1, tj, d), lambda si, j: (si, 0, j, 0))
    return pl.pallas_call(
        functools.partial(_fft_a_kernel, n1=n1, cplx=cplx),
        grid=(sp, n2 // tj),
        in_specs=[pl.BlockSpec((2 * n1 * SUBLANES, half * SUBLANES), lambda si, j: (0, 0)),
                  pl.BlockSpec((None, 1, half, tj, d), lambda si, j: (q, si, 0, j, 0)),
                  pl.BlockSpec((None, 1, half, tj, d), zi_map)],
        out_specs=[out, out],
        out_shape=[jax.ShapeDtypeStruct((sp, n1, n2, d), BF16)] * 2,
        compiler_params=_cparams("parallel", "parallel"),
        name="fft_stage1",
    )(tabs['f1'], u5, u5)


def _cplx_mm(s, xr, xi, conj):
    p = jnp.dot(s, xr.astype(BF16), preferred_element_type=F32)
    q = jnp.dot(s, xi.astype(BF16), preferred_element_type=F32)
    m = s.shape[0] // 2
    if conj:
        return p[0:m] + q[m:], q[0:m] - p[m:]
    return p[0:m] - q[m:], p[m:] + q[0:m]


def _fft_b_kernel(gf_ref, gi_ref, ar_ref, ai_ref, kr_ref, ki_ref, dr_ref, di_ref):
    kr = kr_ref[0]
    ki = ki_ref[0]
    seqs = range(ar_ref.shape[0])
    c = [_cplx_mm(gf_ref[0], ar_ref[s, 0], ai_ref[s, 0], False) for s in seqs]
    e = [(cr * kr - ci * ki, cr * ki + ci * kr) for cr, ci in c]
    dd = [_cplx_mm(gi_ref[0], er, ei, True) for er, ei in e]
    for s, (dr, di) in zip(seqs, dd):
        dr_ref[s, 0] = dr.astype(BF16)
        di_ref[s, 0] = di.astype(BF16)


def _fft_b(ar4, ai4, kr, ki, order, tabs):
    sp, n1, _, d = ar4.shape
    ns = 2 if sp % 2 == 0 else 1
    blk = pl.BlockSpec((ns, 1, FFT_N2, d), lambda k1, si: (si, k1, 0, 0))
    tab = pl.BlockSpec((1, 2 * FFT_N2, FFT_N2), lambda k1, si: (k1, 0, 0))
    kb = pl.BlockSpec((None, 1, FFT_N2, d), lambda k1, si: (order, k1, 0, 0))
    return pl.pallas_call(
        _fft_b_kernel,
        grid=(n1, sp // ns),
        in_specs=[tab, tab, blk, blk, kb, kb],
        out_specs=[blk, blk],
        out_shape=[jax.ShapeDtypeStruct((sp, n1, FFT_N2, d), BF16)] * 2,
        compiler_params=_cparams("parallel", "arbitrary"),
        name="fft_stage2_filter",
    )(tabs['g_fwd'], tabs['g_inv'], ar4, ai4, kr, ki)


def _fft_c_kernel(h_ref, dr_ref, di_ref, u_ref, x_ref, sk_ref, o_ref):
    h = h_ref[...]
    n1, tj, d = dr_ref.shape[1:]
    half = n1 // 2
    sk = sk_ref[...]
    dr_all = dr_ref[0].astype(F32)
    di_all = di_ref[0].astype(F32)
    for s0 in range(0, tj, SUBLANES):
        js = slice(s0, s0 + SUBLANES)
        yr, yi = _cplx_mm(h, dr_all[:, js, :].reshape(n1 * SUBLANES, d),
                          di_all[:, js, :].reshape(n1 * SUBLANES, d), False)
        for part, y in enumerate((yr, yi)):
            u = u_ref[part, 0, :, js, :]
            o_ref[part, 0, :, js, :] = x_ref[part, 0, :, js, :] * (y.reshape(half, SUBLANES, d) + u * sk)


def _fft_c(dr, di, u6, uq, x6, xq, skip, tabs):
    sp, n1, n2, d = dr.shape
    half = n1 // 2
    tj = 2 * SUBLANES
    pair = lambda q: pl.BlockSpec((None, 2, 1, half, tj, d), lambda si, j: (q, 0, si, 0, j, 0))
    dblk = pl.BlockSpec((1, n1, tj, d), lambda si, j: (si, 0, j, 0))
    out = pl.pallas_call(
        _fft_c_kernel,
        grid=(sp, n2 // tj),
        in_specs=[pl.BlockSpec((2 * half * SUBLANES, n1 * SUBLANES), lambda si, j: (0, 0)), dblk, dblk,
                  pair(uq), pair(xq), pl.BlockSpec((1, d), lambda si, j: (0, 0))],
        out_specs=pl.BlockSpec((2, 1, half, tj, d), lambda si, j: (0, si, 0, j, 0)),
        out_shape=jax.ShapeDtypeStruct((2, sp, half, n2, d), F32),
        compiler_params=_cparams("parallel", "parallel"),
        name="fft_inverse_stage1_gate",
    )(tabs['h1'], dr, di, u6, x6, skip.reshape(1, d))
    return out


def _spec_b_kernel(gf_ref, ar_ref, ai_ref, kr_ref, ki_ref, *, scale):
    fr, fi = _cplx_mm(gf_ref[0], ar_ref[0, 0], ai_ref[0, 0], False)
    gr, gi = _cplx_mm(gf_ref[0], ar_ref[1, 0], ai_ref[1, 0], False)
    kr_ref[0, 0] = (fr + gr) * scale
    ki_ref[0, 0] = (fi - gi) * scale


def _filter_spectrum_fft(filt, tabs, d):
    nq, n, _ = filt.shape
    n1 = tabs['n1']
    ar, ai = _fft_a(filt.reshape(1, nq, n1 // 2, FFT_N2, d), 0, tabs, cplx=False)
    orders = nq // 2
    ar5 = ar.reshape(orders, 2, n1, FFT_N2, d)
    ai5 = ai.reshape(orders, 2, n1, FFT_N2, d)
    blk = pl.BlockSpec((None, 2, 1, FFT_N2, d), lambda o, k1: (o, 0, k1, 0, 0))
    out = pl.BlockSpec((1, 1, FFT_N2, d), lambda o, k1: (o, k1, 0, 0))
    return pl.pallas_call(
        functools.partial(_spec_b_kernel, scale=1.0 / (2 * n)),
        grid=(orders, n1),
        in_specs=[pl.BlockSpec((1, 2 * FFT_N2, FFT_N2), lambda o, k1: (k1, 0, 0)), blk, blk],
        out_specs=[out, out],
        out_shape=[jax.ShapeDtypeStruct((orders, n1, FFT_N2, d), F32)] * 2,
        compiler_params=_cparams("parallel", "parallel"),
        name="filter_spectrum",
    )(tabs['g_fwd'], ar5, ai5)


def _long_conv_fft(u, uq, x, xq, skip, kr, ki, order, tabs):
    _, b, n, d = u.shape
    half = tabs['n1'] // 2
    ar, ai = _fft_a(u.reshape(u.shape[0], b, half, FFT_N2, d), uq, tabs, cplx=True)
    dr, di = _fft_b(ar, ai, kr, ki, order, tabs)
    six = lambda a: a.reshape(a.shape[0], 2, b // 2, half, FFT_N2, d)
    return _fft_c(dr, di, six(u), uq, six(x), xq, skip, tabs).reshape(b, n, d)


def _dense_spec_kernel(f_ref, filt_ref, kr_ref, ki_ref, *, scale):
    f = f_ref[...]
    n = f.shape[0] // 2
    pf = jnp.dot(f, filt_ref[0, 0].astype(BF16), preferred_element_type=F32)
    pg = jnp.dot(f, filt_ref[0, 1].astype(BF16), preferred_element_type=F32)
    kr_ref[0] = (pf[0:n] + pg[0:n]) * scale
    ki_ref[0] = (pf[n:] - pg[n:]) * scale


def _filter_spectrum_dense(filt, tabs, d):
    nq, n, _ = filt.shape
    orders = nq // 2
    f4 = filt.reshape(orders, 2, n, d)
    out = pl.BlockSpec((1, 2 * n, d), lambda o: (o, 0, 0))
    return pl.pallas_call(
        functools.partial(_dense_spec_kernel, scale=1.0 / (2 * n)),
        grid=(orders,),
        in_specs=[pl.BlockSpec((4 * n, n), lambda o: (0, 0)), pl.BlockSpec((1, 2, n, d), lambda o: (o, 0, 0, 0))],
        out_specs=[out, out],
        out_shape=[jax.ShapeDtypeStruct((orders, 2 * n, d), F32)] * 2,
        compiler_params=_cparams("parallel"),
        name="filter_spectrum_dense",
    )(tabs['f'], f4)


def _dense_conv_kernel(f_ref, h_ref, u_ref, x_ref, kr_ref, ki_ref, sk_ref, o_ref):
    cr, ci = _cplx_mm(f_ref[...], u_ref[0, 0], u_ref[1, 0], False)
    kr = kr_ref[...]
    ki = ki_ref[...]
    yr, yi = _cplx_mm(h_ref[...], cr * kr - ci * ki, cr * ki + ci * kr, False)
    sk = sk_ref[...]
    o_ref[0, 0] = x_ref[0, 0] * (yr + u_ref[0, 0] * sk)
    o_ref[1, 0] = x_ref[1, 0] * (yi + u_ref[1, 0] * sk)


def _long_conv_dense(u, x, skip, kr, ki, tabs):
    b, n, d = u.shape
    sp = b // 2
    pair = pl.BlockSpec((2, 1, n, d), lambda si: (0, si, 0, 0))
    kb = pl.BlockSpec((2 * n, d), lambda si: (0, 0))
    out = pl.pallas_call(
        _dense_conv_kernel,
        grid=(sp,),
        in_specs=[pl.BlockSpec((4 * n, n), lambda si: (0, 0)), pl.BlockSpec((2 * n, 2 * n), lambda si: (0, 0)),
                  pair, pair, kb, kb, pl.BlockSpec((1, d), lambda si: (0, 0))],
        out_specs=pair,
        out_shape=jax.ShapeDtypeStruct((2, sp, n, d), F32),
        compiler_params=_cparams("parallel"),
        name="long_conv_dense",
    )(tabs['f'], tabs['h'], u.reshape(2, sp, n, d), x.reshape(2, sp, n, d), kr, ki, skip.reshape(1, d))
    return out.reshape(b, n, d)


def _hyena_mixer(x, g, mod, op, fft_tabs, dense_tabs):
    b, n, d = x.shape
    z = _proj_conv(x, g, mod, op['w_in'], op['b_in'], op['conv_w'], op['conv_b'])
    filt = _hyena_filters(n, op, d)
    if n <= DENSE_FFT_MAX:
        kr, ki = _filter_spectrum_dense(filt, dense_tabs, d)
        y = _long_conv_dense(z[0], z[1], op['skip'][0], kr[0], ki[0], dense_tabs)
        y = _long_conv_dense(y, z[2], op['skip'][1], kr[1], ki[1], dense_tabs)
    else:
        kr, ki = _filter_spectrum_fft(filt, fft_tabs, d)
        y = _long_conv_fft(z, 0, z, 1, op['skip'][0], kr, ki, 0, fft_tabs)
        y = _long_conv_fft(y[None], 0, z, 2, op['skip'][1], kr, ki, 1, fft_tabs)
    return _out_res(y, x, mod, op['w_out'], op['b_out'])


def _even_mixer(x, ctx, g, mod_l, mod_c, ep, tabs, need_ctx):
    rc, vc, kkc, gc, lwc, ac, kdc, qc, kac, vac = _even_prep(ctx, g, mod_c, ep, tabs, rope=False)
    rl, vl, kkl, gl, lwl, al, kdl, ql, kal, val = _even_prep(x, g, mod_l, ep, tabs, rope=True)
    b = x.shape[0]
    c = rl.shape[-1]
    h0 = jnp.zeros((2, b, HEAD_DIM, c), F32)
    y_ctx, s_ctx = _rwkv(rc, vc, kkc, lwc, ac, kdc, h0)
    y_lat, _ = _rwkv(rl, vl, kkl, lwl, al, kdl, s_ctx)
    b_lat = _attention(ql, kal, val, kac, vac, ep['sink'], local=True)
    x_new = _even_out(y_lat, rl, vl, gl, kdl, b_lat, x, mod_l, ep, tabs)
    if not need_ctx:
        return x_new, None
    b_ctx = _attention(qc, kac, vac, kac, vac, ep['sink'], local=False)
    ctx_new = _even_out(y_ctx, rc, vc, gc, kdc, b_ctx, ctx, mod_c, ep, tabs)
    return x_new, ctx_new


def _rope_tables(n_tokens):
    rows = n_tokens // GRID_W
    row = jnp.repeat(jnp.arange(rows), GRID_W).astype(F32)
    col = jnp.tile(jnp.arange(GRID_W), rows).astype(F32)
    n_freq = HEAD_DIM // 4
    inv = ROPE_THETA ** (-jnp.arange(n_freq, dtype=F32) / n_freq)
    ang = jnp.concatenate([row[:, None] * inv, col[:, None] * inv], axis=-1)
    cos, sin = jnp.cos(ang), jnp.sin(ang)
    reps = LANES // HEAD_DIM
    cos_t = jnp.tile(jnp.concatenate([cos, cos], axis=-1), (1, reps))
    sin_t = jnp.tile(jnp.concatenate([-sin, sin], axis=-1), (1, reps))
    return cos_t, sin_t


def _block_ones(width):
    idx = np.arange(width) // HEAD_DIM
    return jnp.asarray((idx[:, None] == idx[None, :]).astype(np.float32)).astype(BF16)


def _lora_pad(w):
    z = jnp.zeros_like(w[0])
    return jnp.stack([jnp.concatenate([w[0], z], axis=0), jnp.concatenate([z, w[1]], axis=0)], axis=0)


def kernel(x, c, ctx, c_ctx, ada_w, ada_b, norm1_g, norm2_g, ffn_up, ffn_conv_w, ffn_conv_b, ffn_down, ev_w_in, ev_mu_prev, ev_mu_next, ev_w0, ev_w2, ev_a0, ev_a2, ev_g2, ev_k_k, ev_k_a, ev_r_k, ev_ln_w, ev_ln_b, ev_q_norm, ev_k_norm, ev_sink, ev_w_out, od_w_in, od_b_in, od_conv_w, od_conv_b, od_f_w1, od_f_b1, od_f_w2, od_f_b2, od_f_w3, od_f_b3, od_f_freq, od_f_out, od_skip, od_w_out, od_b_out):
    bsz, seq, d = x.shape
    lc = ctx.shape[1]
    depth = ada_w.shape[0]
    a_width = ev_k_k.shape[-1]
    nq = ev_sink.shape[-1] * HEAD_DIM
    nk = B_KV_HEADS * HEAD_DIM

    cond = jnp.zeros((BF16_ROWS, d), F32).at[:bsz].set(c).at[bsz].set(c_ctx)
    mod = _ada_mod(cond, ada_w, ada_b)

    cos_t, sin_t = _rope_tables(seq)
    tabs = {'cos': cos_t, 'sin': sin_t, 'e': _block_ones(nq)}
    fft_tabs = _fft_tables(seq) if seq > DENSE_FFT_MAX else None
    dense_lat = _dense_tables(seq) if seq <= DENSE_FFT_MAX else None
    fft_ctx = _fft_tables(lc) if lc > DENSE_FFT_MAX else None
    dense_ctx = _dense_tables(lc) if lc <= DENSE_FFT_MAX else None

    for layer in range(depth):
        need_ctx = layer < depth - 1
        even = layer % 2 == 0
        j = layer // 2
        mod_l = mod[layer, :bsz].reshape(bsz, 1, 6 * d)
        mod_c = jnp.broadcast_to(mod[layer, bsz].reshape(1, 1, 6 * d), (bsz, 1, 6 * d))
        if even:
            ep = {'w_in': ev_w_in[j].astype(BF16), 'mu_prev': ev_mu_prev[j][None], 'mu_next': ev_mu_next[j][None],
                  'w0': ev_w0[j], 'w2pad': _lora_pad(ev_w2[j]).astype(BF16), 'a0': ev_a0[j],
                  'a2pad': _lora_pad(ev_a2[j]).astype(BF16), 'g2': ev_g2[j].astype(BF16),
                  'k_k': ev_k_k[j][None], 'k_a': ev_k_a[j][None], 'r_k': ev_r_k[j].reshape(1, a_width),
                  'ln_w': ev_ln_w[j][None], 'ln_b': ev_ln_b[j][None],
                  'q_norm_t': jnp.tile(ev_q_norm[j], nq // HEAD_DIM)[None],
                  'k_norm_t': jnp.tile(ev_k_norm[j], nk // HEAD_DIM)[None],
                  'sink': ev_sink[j], 'w_out': ev_w_out[j].astype(BF16), 'nq': nq, 'nk': nk}
            x, ctx_new = _even_mixer(x, ctx, norm1_g[layer], mod_l, mod_c, ep, tabs, need_ctx)
        else:
            emb_pad = HEAD_DIM
            op = {'w_in': od_w_in[j].astype(BF16), 'b_in': od_b_in[j], 'conv_w': od_conv_w[j],
                  'conv_b': od_conv_b[j],
                  'f_w1': jnp.pad(od_f_w1[j], ((0, emb_pad - HY_EMB), (0, 0))), 'f_b1': od_f_b1[j][None],
                  'f_w2': od_f_w2[j], 'f_b2': od_f_b2[j][None], 'f_w3': od_f_w3[j], 'f_b3': od_f_b3[j][None],
                  'f_freq': od_f_freq[j][None], 'f_out': od_f_out[j], 'skip': od_skip[j],
                  'w_out': od_w_out[j].astype(BF16), 'b_out': od_b_out[j]}
            ctx_new = _hyena_mixer(ctx, norm1_g[layer], mod_c, op, fft_ctx, dense_ctx) if need_ctx else None
            x = _hyena_mixer(x, norm1_g[layer], mod_l, op, fft_tabs, dense_lat)
        w_up = ffn_up[layer].astype(BF16)
        w_down = ffn_down[layer].astype(BF16)
        x = _conv_ffn(x, norm2_g[layer], mod_l, w_up, ffn_conv_w[layer], ffn_conv_b[layer], w_down)
        if need_ctx:
            ctx = _conv_ffn(ctx_new, norm2_g[layer], mod_c, w_up, ffn_conv_w[layer], ffn_conv_b[layer], w_down)
    return x
```

```python
import functools
import math

import numpy as np
import jax
import jax.numpy as jnp
from jax import lax
from jax.experimental import pallas as pl
from jax.experimental.pallas import tpu as pltpu

F32 = jnp.float32
BF16 = jnp.bfloat16

HEAD_DIM = 64
GRID_W = 64
DECAY_LORA = 64
ICLR_LORA = 64
GATE_LORA = 128
RWKV_GN_EPS = 64e-5
B_KV_HEADS = 2
WINDOW = 128
BLOCK = 128
ROPE_THETA = 10000.0
MASK_VALUE = -1e30
HY_ORDER = 2
HY_EMB = 33
HY_BANDS = (HY_EMB - 1) // 2
HY_TARGET = 1e-2
HY_FAST_PCT = 0.3
HY_SLOW_PCT = 1.5
HY_MOD_SHIFT = 0.05
NORM_EPS = 1e-6

V7X_VMEM_BYTES = 64 * 1024 * 1024
VMEM_LIMIT_BYTES = V7X_VMEM_BYTES * 3 // 4
LANES = 128
V7X_MXU_DIM = 256
SUBLANES = 8
BF16_ROWS = 16
FFN_ROW_PARTS = 2
RWKV_CHUNK = 64
RWKV_PREP_CHUNKS = 8
RWKV_INV_BASE = 8
RWKV_GRAM_PASSES = 1
RWKV_INV_PASSES = 1
RWKV_REST_PASSES = 1
RWKV_SCAN_PASSES = 1
FFT_N2 = 128
DENSE_FFT_MAX = 512

HIGHEST = lax.Precision.HIGHEST


def _cparams(*sem):
    return pltpu.CompilerParams(dimension_semantics=sem, vmem_limit_bytes=VMEM_LIMIT_BYTES)


def _dot(a, b):
    return jnp.dot(a.astype(BF16), b.astype(BF16), preferred_element_type=F32)


def _dot_nt(a, b):
    return lax.dot_general(a.astype(BF16), b.astype(BF16), (((1,), (1,)), ((), ())), preferred_element_type=F32)


def _dot_tn(a, b):
    return lax.dot_general(a.astype(BF16), b.astype(BF16), (((0,), (0,)), ((), ())), preferred_element_type=F32)


def _dot_hi(a, b):
    return jnp.dot(a, b, preferred_element_type=F32, precision=HIGHEST)


def _dot_nt_hi(a, b):
    return lax.dot_general(a, b, (((1,), (1,)), ((), ())), preferred_element_type=F32, precision=HIGHEST)


def _dot_tn_hi(a, b):
    return lax.dot_general(a, b, (((0,), (0,)), ((), ())), preferred_element_type=F32, precision=HIGHEST)


def _segsum(x, e_ref):
    hi = x.astype(BF16)
    lo = (x - hi.astype(F32)).astype(BF16)
    w = x.shape[-1]
    tile = min(w, V7X_MXU_DIM)
    e = e_ref[0:tile, 0:tile]
    both = jnp.concatenate([hi, lo], axis=0)
    parts = [jnp.dot(both[:, c0:c0 + tile], e, preferred_element_type=F32) for c0 in range(0, w, tile)]
    full = parts[0] if len(parts) == 1 else jnp.concatenate(parts, axis=1)
    m = x.shape[0]
    return full[0:m] + full[m:]


def _norm_mod(x, g, shift, scale):
    xn = x * lax.rsqrt(jnp.mean(x * x, axis=-1, keepdims=True) + NORM_EPS)
    return (xn * g) * (1.0 + scale) + shift


def _row_tile(t, want):
    tm = min(t, want)
    assert t % tm == 0 and tm % BF16_ROWS == 0
    return tm


def _ada_kernel(c_ref, w_ref, b_ref, o_ref):
    c = c_ref[...]
    s = c * jax.nn.sigmoid(c)
    o_ref[0] = _dot(s, w_ref[0]) + b_ref[0]


def _ada_mod(cond, ada_w, ada_b):
    depth, d, n = ada_w.shape
    tn = n // 4
    rows = cond.shape[0]
    return pl.pallas_call(
        _ada_kernel,
        grid=(depth, n // tn),
        in_specs=[pl.BlockSpec((rows, d), lambda l, j: (0, 0)),
                  pl.BlockSpec((1, d, tn), lambda l, j: (l, 0, j)),
                  pl.BlockSpec((1, 1, tn), lambda l, j: (l, 0, j))],
        out_specs=pl.BlockSpec((1, rows, tn), lambda l, j: (l, 0, j)),
        out_shape=jax.ShapeDtypeStruct((depth, rows, n), F32),
        compiler_params=_cparams("parallel", "parallel"),
        name="ada_mod",
    )(cond, ada_w, ada_b.reshape(depth, 1, n))


def _halo_specs(tm, t, d, nargs):
    r = tm // BF16_ROWS
    last = t // BF16_ROWS - 1
    if nargs == 3:
        prev = pl.BlockSpec((1, BF16_ROWS, d), lambda bi, i, j: (bi, jnp.maximum(i * r - 1, 0), 0))
        nxt = pl.BlockSpec((1, BF16_ROWS, d), lambda bi, i, j: (bi, jnp.minimum((i + 1) * r, last), 0))
    else:
        prev = pl.BlockSpec((1, BF16_ROWS, d), lambda bi, i: (bi, jnp.maximum(i * r - 1, 0), 0))
        nxt = pl.BlockSpec((1, BF16_ROWS, d), lambda bi, i: (bi, jnp.minimum((i + 1) * r, last), 0))
    return prev, nxt


def _fill_h(x_ref, xp_ref, xn_ref, g, shift, scale, h_scr, tm):
    h_scr[0:BF16_ROWS] = _norm_mod(xp_ref[0], g, shift, scale).astype(BF16)
    h_scr[BF16_ROWS:BF16_ROWS + tm] = _norm_mod(x_ref[0], g, shift, scale).astype(BF16)
    h_scr[BF16_ROWS + tm:2 * BF16_ROWS + tm] = _norm_mod(xn_ref[0], g, shift, scale).astype(BF16)


def _conv3_rows(u, u_scr, cw, cb, tm, t_total, part=0, nparts=1):
    i = pl.program_id(1)
    o = BF16_ROWS
    rows = tm // nparts
    base = part * (rows + 2 * o)
    head_ok = (i > 0) if part == 0 else True
    tail_ok = (i < t_total // tm - 1) if part == nparts - 1 else True
    u_scr[base:base + o] = jnp.where(head_ok, u[0:o], 0.0)
    u_scr[base + o:base + o + rows] = u[o:o + rows]
    u_scr[base + o + rows:base + 2 * o + rows] = jnp.where(tail_ok, u[o + rows:], 0.0)
    s = base + o
    return (u_scr[s - 1:s - 1 + rows] * cw[0:1] + u_scr[s:s + rows] * cw[1:2] + u_scr[s + 1:s + 1 + rows] * cw[2:3]
            + cb)


def _proj_conv_kernel(x_ref, xp_ref, xn_ref, g_ref, mod_ref, w_ref, b_ref, cw_ref, cb_ref, o_ref, h_scr, u_scr,
                      *, d, tm, t_total, tn):
    m = mod_ref[0]
    _fill_h(x_ref, xp_ref, xn_ref, g_ref[...], m[:, 0:d], m[:, d:2 * d], h_scr, tm)
    nparts = FFN_ROW_PARTS
    rows = tm // nparts
    n = w_ref.shape[1]

    def project(j):
        cols = slice(j * tn, (j + 1) * tn)
        return [jnp.dot(h_scr[p * rows:(p + 1) * rows + 2 * BF16_ROWS], w_ref[:, cols],
                        preferred_element_type=F32) + b_ref[:, cols] for p in range(nparts)]

    nxt_u = project(0)
    for j in range(n // tn):
        cols = slice(j * tn, (j + 1) * tn)
        us = nxt_u
        if (j + 1) * tn < n:
            nxt_u = project(j + 1)
        q, lane0 = divmod(j * tn, d)
        for p in range(nparts):
            o_ref[q, 0, p * rows:(p + 1) * rows, lane0:lane0 + tn] = _conv3_rows(
                us[p], u_scr.at[j % 2], cw_ref[:, cols], cb_ref[:, cols], tm, t_total, p, nparts)


def _proj_conv(x, g, mod, w, bias, cw, cb):
    b, t, d = x.shape
    n = w.shape[1]
    tm = _row_tile(t, 512)
    tn = d // 2
    prev, nxt = _halo_specs(tm, t, d, 2)
    resident = lambda *shape: pl.BlockSpec(shape, lambda bi, i: (0,) * len(shape), pipeline_mode=pl.Buffered(1))
    return pl.pallas_call(
        functools.partial(_proj_conv_kernel, d=d, tm=tm, t_total=t, tn=tn),
        grid=(b, t // tm),
        in_specs=[pl.BlockSpec((1, tm, d), lambda bi, i: (bi, i, 0)), prev, nxt,
                  pl.BlockSpec((1, d), lambda bi, i: (0, 0)),
                  pl.BlockSpec((1, 1, mod.shape[-1]), lambda bi, i: (bi, 0, 0)),
                  resident(d, n), resident(1, n), resident(3, n), resident(1, n)],
        out_specs=pl.BlockSpec((n // d, 1, tm, d), lambda bi, i: (0, bi, i, 0)),
        out_shape=jax.ShapeDtypeStruct((n // d, b, t, d), F32),
        scratch_shapes=[pltpu.VMEM((tm + 2 * BF16_ROWS, d), BF16),
                        pltpu.VMEM((2, tm + 2 * BF16_ROWS * FFN_ROW_PARTS, tn), F32)],
        compiler_params=_cparams("parallel", "parallel"),
        name="hyena_in_proj",
    )(x, x, x, g.reshape(1, d), mod, w, bias.reshape(1, n), cw, cb.reshape(1, n))


def _ffn_kernel(x_ref, xp_ref, xn_ref, g_ref, mod_ref, wup_ref, cw_ref, cb_ref, wd_ref,
                o_ref, h_scr, ug_scr, uv_scr, acc_scr, *, d, tm, t_total, f, tf):
    m = mod_ref[0]
    _fill_h(x_ref, xp_ref, xn_ref, g_ref[...], m[:, 3 * d:4 * d], m[:, 4 * d:5 * d], h_scr, tm)
    nparts = FFN_ROW_PARTS
    rows = tm // nparts
    nf = f // tf

    def up_proj(j):
        hs = [h_scr[p * rows:(p + 1) * rows + 2 * BF16_ROWS] for p in range(nparts)]
        return ([jnp.dot(h, wup_ref[:, j * tf:(j + 1) * tf], preferred_element_type=F32) for h in hs],
                [jnp.dot(h, wup_ref[:, f + j * tf:f + (j + 1) * tf], preferred_element_type=F32) for h in hs])

    nxt_up = up_proj(0)
    for j in range(nf):
        gs = slice(j * tf, (j + 1) * tf)
        vs = slice(f + j * tf, f + (j + 1) * tf)
        ug, uv = nxt_up
        if j + 1 < nf:
            nxt_up = up_proj(j + 1)
        for p in range(nparts):
            gate = _conv3_rows(ug[p], ug_scr.at[j % 2], cw_ref[:, gs], cb_ref[:, gs], tm, t_total, p, nparts)
            val = _conv3_rows(uv[p], uv_scr.at[j % 2], cw_ref[:, vs], cb_ref[:, vs], tm, t_total, p, nparts)
            act = (gate * jax.nn.sigmoid(gate)) * val
            down = jnp.dot(act.astype(BF16), wd_ref[gs, :], preferred_element_type=F32)
            part = slice(p * rows, (p + 1) * rows)
            if j == 0:
                acc_scr[part] = down
            else:
                acc_scr[part] += down
    o_ref[0] = x_ref[0] + m[:, 5 * d:6 * d] * acc_scr[...]


def _conv_ffn(x, g, mod, w_up, cw, cb, w_down):
    b, t, d = x.shape
    f = w_down.shape[0]
    tm = _row_tile(t, 1024)
    tf = 256 if f % 256 == 0 else 128
    prev, nxt = _halo_specs(tm, t, d, 2)
    resident = lambda *shape: pl.BlockSpec(shape, lambda bi, i: (0,) * len(shape), pipeline_mode=pl.Buffered(1))
    u_rows = tm + 2 * BF16_ROWS * FFN_ROW_PARTS
    return pl.pallas_call(
        functools.partial(_ffn_kernel, d=d, tm=tm, t_total=t, f=f, tf=tf),
        grid=(b, t // tm),
        in_specs=[pl.BlockSpec((1, tm, d), lambda bi, i: (bi, i, 0)), prev, nxt,
                  pl.BlockSpec((1, d), lambda bi, i: (0, 0)),
                  pl.BlockSpec((1, 1, mod.shape[-1]), lambda bi, i: (bi, 0, 0)),
                  resident(d, 2 * f), resident(3, 2 * f), resident(1, 2 * f), resident(f, d)],
        out_specs=pl.BlockSpec((1, tm, d), lambda bi, i: (bi, i, 0)),
        out_shape=jax.ShapeDtypeStruct((b, t, d), F32),
        scratch_shapes=[pltpu.VMEM((tm + 2 * BF16_ROWS, d), BF16),
                        pltpu.VMEM((2, u_rows, tf), F32),
                        pltpu.VMEM((2, u_rows, tf), F32),
                        pltpu.VMEM((tm, d), F32)],
        compiler_params=_cparams("parallel", "parallel"),
        name="conv_ffn",
    )(x, x, x, g.reshape(1, d), mod, w_up, cw, cb.reshape(1, 2 * f), w_down)


def _out_res_kernel(y_ref, x_ref, mod_ref, w_ref, b_ref, o_ref, *, d):
    o = _dot(y_ref[0], w_ref[...]) + b_ref[...]
    o_ref[0] = x_ref[0] + mod_ref[0][:, 2 * d:3 * d] * o


def _out_res(y, x, mod, w, bias):
    b, t, d = x.shape
    tm = _row_tile(t, 512)
    return pl.pallas_call(
        functools.partial(_out_res_kernel, d=d),
        grid=(b, t // tm),
        in_specs=[pl.BlockSpec((1, tm, d), lambda bi, i: (bi, i, 0)),
                  pl.BlockSpec((1, tm, d), lambda bi, i: (bi, i, 0)),
                  pl.BlockSpec((1, 1, mod.shape[-1]), lambda bi, i: (bi, 0, 0)),
                  pl.BlockSpec((d, d), lambda bi, i: (0, 0)),
                  pl.BlockSpec((1, d), lambda bi, i: (0, 0))],
        out_specs=pl.BlockSpec((1, tm, d), lambda bi, i: (bi, i, 0)),
        out_shape=jax.ShapeDtypeStruct((b, t, d), F32),
        compiler_params=_cparams("parallel", "parallel"),
        name="out_proj_residual",
    )(y, x, mod, w, bias.reshape(1, d))


def _even_prep_kernel(x_ref, xp_ref, xn_ref, gn_ref, mod_ref, win_ref, mup_ref, mun_ref, w0_ref, w2_ref, a0_ref,
                      a2_ref, g2_ref, kk_ref, ka_ref, qn_ref, kn_ref, cos_ref, sin_ref, e_ref,
                      r_ref, v_ref, kkn_ref, g_ref, lw_ref, a_ref, kd_ref, q_ref, katt_ref, vatt_ref,
                      h_scr, p_scr, *, tm, d, c, a_in, rope):
    i = pl.program_id(1)
    nt = pl.num_programs(1)
    m = mod_ref[0]
    _fill_h(x_ref, xp_ref, xn_ref, gn_ref[...], m[:, 0:d], m[:, d:2 * d], h_scr, tm)
    p_scr[...] = jnp.dot(h_scr[...], win_ref[...], preferred_element_type=F32)
    o = BF16_ROWS
    pa = p_scr[o:o + tm, 0:a_in]
    rows = lax.broadcasted_iota(jnp.int32, (tm, 1), 0)
    prev = jnp.where((rows == 0) & (i == 0), 0.0, p_scr[o - 1:o - 1 + tm, 0:a_in])
    nxt = jnp.where((rows == tm - 1) & (i == nt - 1), 0.0, p_scr[o + 1:o + 1 + tm, 0:a_in])
    za = pa + mup_ref[...] * (prev - pa) + mun_ref[...] * (nxt - pa)

    r = za[:, 0:c]
    k = za[:, c:2 * c]
    v = za[:, 2 * c:3 * c]
    wd = jnp.tanh(za[:, 3 * c:3 * c + 2 * DECAY_LORA])
    ad = za[:, 3 * c + 2 * DECAY_LORA:3 * c + 2 * DECAY_LORA + 2 * ICLR_LORA]
    gd = za[:, 3 * c + 2 * DECAY_LORA + 2 * ICLR_LORA:a_in]
    r_ref[0] = r
    v_ref[0] = v
    g_ref[0] = _dot(jax.nn.sigmoid(gd), g2_ref[...])
    kkv = k * kk_ref[...]
    ss = _segsum(kkv * kkv, e_ref)
    kkn_ref[0] = kkv / jnp.maximum(jnp.sqrt(ss), 1e-12)
    for dd in range(2):
        y = w0_ref[dd:dd + 1] + _dot(wd, w2_ref[dd])
        lw_ref[dd, 0] = -math.exp(-0.5) * jax.nn.sigmoid(y)
        a = jax.nn.sigmoid(a0_ref[dd:dd + 1] + _dot(ad, a2_ref[dd]))
        a_ref[dd, 0] = a
        kd_ref[dd, 0] = k * (1.0 + (a - 1.0) * ka_ref[...])

    pb = p_scr[o:o + tm, a_in:]
    nq = q_ref.shape[-1]
    nk = katt_ref.shape[-1]
    q = pb[:, 0:nq]
    kat = pb[:, nq:nq + nk]
    vatt_ref[0] = pb[:, nq + nk:]
    half = HEAD_DIM // 2

    def norm_rope(x, gain, e):
        w = x.shape[-1]
        ms = _segsum(x * x, e) * (1.0 / HEAD_DIM)
        xn = x * lax.rsqrt(ms + NORM_EPS) * gain
        if not rope:
            return xn
        lane = lax.broadcasted_iota(jnp.int32, (1, w), 1)
        first = (lane % HEAD_DIM) < half
        swapped = jnp.where(first, pltpu.roll(xn, w - half, 1), pltpu.roll(xn, half, 1))
        reps = w // LANES
        cos = jnp.concatenate([cos_ref[...]] * reps, axis=1) if reps > 1 else cos_ref[...]
        sin = jnp.concatenate([sin_ref[...]] * reps, axis=1) if reps > 1 else sin_ref[...]
        return xn * cos + swapped * sin

    q_ref[0] = norm_rope(q, qn_ref[...], e_ref[...])
    katt_ref[0] = norm_rope(kat, kn_ref[...], e_ref[0:nk, 0:nk])


def _even_prep(x, g, mod, ep, tabs, rope):
    b, t, d = x.shape
    n = ep['w_in'].shape[1]
    c = ep['k_k'].shape[-1]
    a_in = ep['mu_prev'].shape[-1]
    nq = ep['nq']
    nk = ep['nk']
    tm = _row_tile(t, 256)
    prev, nxt = _halo_specs(tm, t, d, 2)
    full = lambda *shape: pl.BlockSpec(shape, lambda bi, i: (0,) * len(shape))
    tok = lambda w: pl.BlockSpec((1, tm, w), lambda bi, i: (bi, i, 0))
    tok2 = lambda w: pl.BlockSpec((2, 1, tm, w), lambda bi, i: (0, bi, i, 0))
    sd = lambda *shape: jax.ShapeDtypeStruct(shape, F32)
    return pl.pallas_call(
        functools.partial(_even_prep_kernel, tm=tm, d=d, c=c, a_in=a_in, rope=rope),
        grid=(b, t // tm),
        in_specs=[tok(d), prev, nxt, full(1, d), pl.BlockSpec((1, 1, mod.shape[-1]), lambda bi, i: (bi, 0, 0)),
                  full(d, n), full(1, a_in), full(1, a_in), full(2, c), full(2, 2 * DECAY_LORA, c),
                  full(2, c), full(2, 2 * ICLR_LORA, c), full(GATE_LORA, c), full(1, c), full(1, c),
                  full(1, nq), full(1, nk),
                  pl.BlockSpec((tm, LANES), lambda bi, i: (i, 0)), pl.BlockSpec((tm, LANES), lambda bi, i: (i, 0)),
                  full(nq, nq)],
        out_specs=[tok(c), tok(c), tok(c), tok(c), tok2(c), tok2(c), tok2(c), tok(nq), tok(nk), tok(nk)],
        out_shape=[sd(b, t, c), sd(b, t, c), sd(b, t, c), sd(b, t, c), sd(2, b, t, c), sd(2, b, t, c),
                   sd(2, b, t, c), sd(b, t, nq), sd(b, t, nk), sd(b, t, nk)],
        scratch_shapes=[pltpu.VMEM((tm + 2 * BF16_ROWS, d), BF16), pltpu.VMEM((tm + 2 * BF16_ROWS, n), F32)],
        compiler_params=_cparams("parallel", "parallel"),
        name="even_in_proj_prep",
    )(x, x, x, g.reshape(1, d), mod, ep['w_in'], ep['mu_prev'], ep['mu_next'], ep['w0'], ep['w2pad'], ep['a0'], ep['a2pad'], ep['g2'], ep['k_k'],
      ep['k_a'], ep['q_norm_t'], ep['k_norm_t'], tabs['cos'][:t], tabs['sin'][:t], tabs['e'])


HEADS_PER_GROUP = 2
GROUP_LANES = HEADS_PER_GROUP * HEAD_DIM


def _split_bf16(x):
    hi = x.astype(BF16)
    return hi, (x - hi.astype(F32)).astype(BF16)


def _block_diag(x, bmask):
    return jnp.where(bmask, jnp.concatenate([x] * HEADS_PER_GROUP, axis=0), jnp.zeros((), x.dtype))


def _head_mm(lhs, rhs, bmask, passes, nt=False):
    dn = (((1,), (1,)), ((), ())) if nt else (((1,), (0,)), ((), ()))
    if passes == 1:
        return lax.dot_general(lhs.astype(BF16), _block_diag(rhs.astype(BF16), bmask), dn, preferred_element_type=F32)
    lh, ll = _split_bf16(lhs)
    rh, rl = _split_bf16(rhs)
    m = lhs.shape[0]
    top = lax.dot_general(jnp.concatenate([lh, ll], axis=0), _block_diag(rh, bmask), dn, preferred_element_type=F32)
    return top[0:m] + top[m:] + lax.dot_general(lh, _block_diag(rl, bmask), dn, preferred_element_type=F32)


def _head_mm_tn(lhs, rhs, lane_head, passes):
    dn = (((0,), (0,)), ((), ()))
    if passes == 1:
        full = lax.dot_general(lhs.astype(BF16), rhs.astype(BF16), dn, preferred_element_type=F32)
    else:
        lh, ll = _split_bf16(lhs)
        rh, rl = _split_bf16(rhs)
        full = (lax.dot_general(jnp.concatenate([lh, ll], axis=0), jnp.concatenate([rh, rh], axis=0), dn,
                                preferred_element_type=F32)
                + lax.dot_general(lh, rl, dn, preferred_element_type=F32))
    out = jnp.where(lane_head == 0, full[0:HEAD_DIM], 0.0)
    for h in range(1, HEADS_PER_GROUP):
        out = out + jnp.where(lane_head == h, full[h * HEAD_DIM:(h + 1) * HEAD_DIM], 0.0)
    return out


def _group_masks():
    r = lax.broadcasted_iota(jnp.int32, (GROUP_LANES, GROUP_LANES), 0)
    c = lax.broadcasted_iota(jnp.int32, (GROUP_LANES, GROUP_LANES), 1)
    bmask = (r // HEAD_DIM) == (c // HEAD_DIM)
    lane_head = lax.broadcasted_iota(jnp.int32, (1, GROUP_LANES), 1) // HEAD_DIM
    return bmask, lane_head


def _rwkv_prep_kernel(r_ref, v_ref, kk_ref, lw_ref, a_ref, kd_ref, r2_ref, yl_ref, t_ref, z_ref, *, cs):
    sgn = 1 - 2 * pl.program_id(0)
    row = lax.broadcasted_iota(jnp.int32, (cs, cs), 0)
    col = lax.broadcasted_iota(jnp.int32, (cs, cs), 1)
    tri = (((row - col) * sgn) >= 0).astype(F32)
    bmask, lane_head = _group_masks()
    t_idx = lax.broadcasted_iota(jnp.int32, (cs, GROUP_LANES), 0)
    s_idx = lax.broadcasted_iota(jnp.int32, (cs, GROUP_LANES), 1) % HEAD_DIM
    diff = (t_idx - s_idx) * sgn
    strict = diff > 0
    incl = diff >= 0
    eye = (diff == 0).astype(F32)
    base = min(RWKV_INV_BASE, cs)
    same_base = (t_idx // base) == (s_idx // base)
    groups = r_ref.shape[-1] // GROUP_LANES

    ch = []
    for c0 in range(0, r_ref.shape[1], cs):
        rows = slice(c0, c0 + cs)
        r, v, kk = r_ref[0, rows], v_ref[0, rows], kk_ref[0, rows]
        lw, a, kd = lw_ref[0, 0, rows], a_ref[0, 0, rows], kd_ref[0, 0, rows]
        g = _dot_hi(tri, lw)
        gp = g - lw
        gref = g[cs // 2:cs // 2 + 1]
        gend = jnp.sum(lw, axis=0, keepdims=True)
        bvec = kk * a
        full = {'a_t': -kk * jnp.exp(gp - gref), 'a_0': -kk * jnp.exp(gp),
                'b_t': bvec * jnp.exp(gref - g), 'b_e': bvec * jnp.exp(gend - g),
                'k_t': kd * jnp.exp(gref - g), 'k_e': kd * jnp.exp(gend - g),
                'r_t': r * jnp.exp(g - gref), 'r_0': r * jnp.exp(g), 'v': v,
                'wend': jnp.broadcast_to(jnp.exp(gend), (cs, v.shape[-1]))}
        for gi in range(groups):
            sl = slice(gi * GROUP_LANES, (gi + 1) * GROUP_LANES)
            c = {k: x[:, sl] for k, x in full.items()}
            c['rows'], c['sl'] = rows, sl
            ch.append(c)

    mm = lambda x, y, passes, nt=False: [_head_mm(p_, q_, bmask, passes, nt) for p_, q_ in zip(x, y)]
    get = lambda k: [c[k] for c in ch]
    ar = [jnp.concatenate([c['a_t'], c['r_t']], axis=0) for c in ch]
    gb = mm(ar, get('b_t'), RWKV_GRAM_PASSES, True)
    gk = mm(ar, get('k_t'), RWKV_GRAM_PASSES, True)
    nmat = [jnp.where(strict, x[0:cs], 0.0) for x in gb]
    g_rb = [jnp.where(incl, x[cs:], 0.0) for x in gb]
    g_k = [jnp.concatenate([jnp.where(strict, x[0:cs], 0.0), jnp.where(incl, x[cs:], 0.0)], axis=0) for x in gk]
    npow = [jnp.where(same_base, x, 0.0) for x in nmat]
    p = [eye + x for x in npow]
    for _ in range(int(math.log2(base)) - 1):
        npow = mm(npow, npow, RWKV_INV_PASSES)
        p = [x + y for x, y in zip(p, mm(npow, p, RWKV_INV_PASSES))]
    m = base
    while m < cs:
        off = ((t_idx // m) != (s_idx // m)) & ((t_idx // (2 * m)) == (s_idx // (2 * m)))
        q = mm([jnp.where(off, x, 0.0) for x in nmat], p, RWKV_INV_PASSES)
        p = [x + y for x, y in zip(p, mm(p, q, RWKV_INV_PASSES))]
        m *= 2
    vg = get('v')
    a2 = mm(p, get('a_0'), RWKV_REST_PASSES)
    kv = mm(g_k, vg, RWKV_REST_PASSES)
    u_v = mm(p, [x[0:cs] for x in kv], RWKV_REST_PASSES)
    r2 = mm(g_rb, a2, RWKV_REST_PASSES)
    yl1 = mm(g_rb, u_v, RWKV_REST_PASSES)
    yl2 = [x[cs:] for x in kv]
    tt = [_head_mm_tn(x, c['b_e'], lane_head, RWKV_REST_PASSES) for x, c in zip(a2, ch)]
    zz = [_head_mm_tn(jnp.concatenate([u, c['v']], axis=0), jnp.concatenate([c['b_e'], c['k_e']], axis=0),
                      lane_head, RWKV_REST_PASSES) for u, c in zip(u_v, ch)]
    for i, c in enumerate(ch):
        rows, sl = c['rows'], c['sl']
        r2_ref[0, 0, rows, sl] = (c['r_0'] + r2[i]).astype(r2_ref.dtype)
        yl_ref[0, 0, rows, sl] = yl1[i] + yl2[i]
        t_ref[0, 0, rows, sl] = (eye * c['wend'] + tt[i]).astype(t_ref.dtype)
        z_ref[0, 0, rows, sl] = zz[i]


def _rwkv_scan_kernel(h0_ref, r2f_ref, ylf_ref, tf_ref, zf_ref, r2b_ref, ylb_ref, tb_ref, zb_ref,
                      yf_ref, yb_ref, ht_ref, s_scr):
    ci = pl.program_id(0)

    @pl.when(ci == 0)
    def _():
        s_scr[...] = h0_ref[...]

    bmask, _ = _group_masks()
    ins = ((r2f_ref, ylf_ref, tf_ref, zf_ref, yf_ref), (r2b_ref, ylb_ref, tb_ref, zb_ref, yb_ref))
    for d, (r2_ref, yl_ref, t_ref, z_ref, y_ref) in enumerate(ins):
        for b in range(s_scr.shape[1]):
            for gi in range(s_scr.shape[-1] // GROUP_LANES):
                sl = slice(gi * GROUP_LANES, (gi + 1) * GROUP_LANES)
                s = s_scr[d, b, :, sl]
                y_ref[0, b, :, sl] = yl_ref[0, b, :, sl] + _head_mm(r2_ref[0, b, :, sl], s, bmask, 1, nt=True)
                s_scr[d, b, :, sl] = (_head_mm(s, t_ref[0, b, :, sl], bmask, RWKV_SCAN_PASSES)
                                      + z_ref[0, b, :, sl])

    @pl.when(ci == pl.num_programs(0) - 1)
    def _():
        ht_ref[...] = s_scr[...]


def _rwkv(r, v, kk, lw, a, kd, h0):
    b, t, c = r.shape
    cs = min(RWKV_CHUNK, t)
    assert cs == HEAD_DIM and c % GROUP_LANES == 0
    nc = t // cs
    rows = cs * math.gcd(nc, RWKV_PREP_CHUNKS)
    tok = pl.BlockSpec((1, rows, c), lambda d, bi, ci: (bi, ci, 0))
    tok2 = pl.BlockSpec((1, 1, rows, c), lambda d, bi, ci: (d, bi, ci, 0))
    wide = jax.ShapeDtypeStruct((2, b, t, c), F32)
    mm_operand = jax.ShapeDtypeStruct((2, b, t, c), BF16 if RWKV_SCAN_PASSES == 1 else F32)
    r2, yl, tt, zz = pl.pallas_call(
        functools.partial(_rwkv_prep_kernel, cs=cs),
        grid=(2, b, t // rows),
        in_specs=[tok, tok, tok, tok2, tok2, tok2],
        out_specs=[tok2, tok2, tok2, tok2],
        out_shape=[mm_operand, wide, mm_operand, wide],
        compiler_params=_cparams("parallel", "parallel", "parallel"),
        name="rwkv7_chunk_prep",
    )(r, v, kk, lw, a, kd)

    fwd = pl.BlockSpec((1, b, cs, c), lambda ci: (0, 0, ci, 0))
    bwd = pl.BlockSpec((1, b, cs, c), lambda ci: (1, 0, nc - 1 - ci, 0))
    st = pl.BlockSpec((2, b, HEAD_DIM, c), lambda ci: (0, 0, 0, 0))
    yf, yb, ht = pl.pallas_call(
        _rwkv_scan_kernel,
        grid=(nc,),
        in_specs=[st, fwd, fwd, fwd, fwd, bwd, bwd, bwd, bwd],
        out_specs=[pl.BlockSpec((1, b, cs, c), lambda ci: (0, 0, ci, 0)),
                   pl.BlockSpec((1, b, cs, c), lambda ci: (0, 0, nc - 1 - ci, 0)), st],
        out_shape=[jax.ShapeDtypeStruct((1, b, t, c), F32), jax.ShapeDtypeStruct((1, b, t, c), F32),
                   jax.ShapeDtypeStruct((2, b, HEAD_DIM, c), F32)],
        scratch_shapes=[pltpu.VMEM((2, b, HEAD_DIM, c), F32)],
        compiler_params=_cparams("arbitrary"),
        name="rwkv7_state_scan",
    )(h0, r2, yl, tt, zz, r2, yl, tt, zz)
    return (yf.reshape(b, t, c), yb.reshape(b, t, c)), ht


def _attn_kernel(sink_ref, q_ref, kp_ref, kc_ref, kn_ref, vp_ref, vc_ref, vn_ref, kx_ref, vx_ref, o_ref,
                 *, local, group):
    i = pl.program_id(1)
    nb = pl.num_programs(1)
    scale = HEAD_DIM ** -0.5
    assert math.frexp(scale)[0] == 0.5
    q = q_ref[0] * scale
    if local:
        k_all = jnp.concatenate([kp_ref[0], kc_ref[0], kn_ref[0], kx_ref[0]], axis=0)
        v_all = jnp.concatenate([vp_ref[0], vc_ref[0], vn_ref[0], vx_ref[0]], axis=0)
        qi = lax.broadcasted_iota(jnp.int32, (BLOCK, BLOCK), 0)
        kj = lax.broadcasted_iota(jnp.int32, (BLOCK, BLOCK), 1)
        assert WINDOW == BLOCK
        valid_prev = (kj >= qi) & (i > 0)
        valid_next = (kj <= qi) & (i < nb - 1)
    else:
        k_all = kx_ref[0]
        v_all = vx_ref[0]
    heads = range(q.shape[-1] // HEAD_DIM)
    kv = lambda x, h: x[:, (h // group) * HEAD_DIM:(h // group + 1) * HEAD_DIM]
    s = [_dot_nt(q[:, h * HEAD_DIM:(h + 1) * HEAD_DIM], kv(k_all, h)) for h in heads]
    if local:
        s = [jnp.concatenate([jnp.where(valid_prev, x[:, 0:BLOCK], MASK_VALUE), x[:, BLOCK:2 * BLOCK],
                              jnp.where(valid_next, x[:, 2 * BLOCK:3 * BLOCK], MASK_VALUE), x[:, 3 * BLOCK:]],
                             axis=1) for x in s]
    sink = [sink_ref[h] for h in heads]
    m = [jnp.maximum(jnp.max(x, axis=-1, keepdims=True), sk) for x, sk in zip(s, sink)]
    e = [jnp.exp(x - y) for x, y in zip(s, m)]
    den = [jnp.sum(x, axis=-1, keepdims=True) + jnp.exp(sk - y) for x, y, sk in zip(e, m, sink)]
    outs = [_dot(x, kv(v_all, h)) / dn for x, h, dn in zip(e, heads, den)]
    o_ref[0] = jnp.concatenate(outs, axis=1)


def _attention(q, k, v, kx, vx, sink, local):
    b, t, nq = q.shape
    nk = k.shape[-1]
    nb = t // BLOCK
    group = (nq // HEAD_DIM) // (nk // HEAD_DIM)
    lx = kx.shape[1]
    kv = lambda f: pl.BlockSpec((1, BLOCK, nk), f)
    pf = lambda bi, i: (bi, jnp.maximum(i - 1, 0), 0)
    cf = lambda bi, i: (bi, i, 0)
    nf = lambda bi, i: (bi, jnp.minimum(i + 1, nb - 1), 0)
    ctx = pl.BlockSpec((1, lx, nk), lambda bi, i: (bi, 0, 0))
    return pl.pallas_call(
        functools.partial(_attn_kernel, local=local, group=group),
        grid=(b, nb),
        in_specs=[pl.BlockSpec(memory_space=pltpu.SMEM),
                  pl.BlockSpec((1, BLOCK, nq), cf), kv(pf), kv(cf), kv(nf), kv(pf), kv(cf), kv(nf), ctx, ctx],
        out_specs=pl.BlockSpec((1, BLOCK, nq), cf),
        out_shape=jax.ShapeDtypeStruct((b, t, nq), F32),
        compiler_params=_cparams("parallel", "parallel"),
        name="window_attention" if local else "context_attention",
    )(sink, q, k, k, k, v, v, v, kx, vx)


def _even_out_kernel(yf_ref, yb_ref, r_ref, v_ref, g_ref, kd_ref, batt_ref, x_ref, mod_ref, lnw_ref, lnb_ref, rk_ref, e_ref,
                     w_ref, o_ref, *, d, c):
    y = yf_ref[0] + yb_ref[0]
    inv = 1.0 / HEAD_DIM
    mu = _segsum(y, e_ref) * inv
    yc = y - mu
    var = _segsum(yc * yc, e_ref) * inv
    yn = yc * lax.rsqrt(var + RWKV_GN_EPS) * lnw_ref[...] + lnb_ref[...]
    bonus = _segsum(r_ref[0] * (kd_ref[0, 0] + kd_ref[1, 0]) * rk_ref[...], e_ref)
    a_out = (yn + bonus * v_ref[0]) * g_ref[0]
    o = _dot(a_out, w_ref[0:c]) + _dot(batt_ref[0], w_ref[c:])
    o_ref[0] = x_ref[0] + mod_ref[0][:, 2 * d:3 * d] * o


def _even_out(y, r, v, g, kd, batt, x, mod, ep, tabs):
    b, t, d = x.shape
    c = r.shape[-1]
    nq = batt.shape[-1]
    tm = _row_tile(t, 512)
    tok = lambda w: pl.BlockSpec((1, tm, w), lambda bi, i: (bi, i, 0))
    tok2 = lambda w: pl.BlockSpec((2, 1, tm, w), lambda bi, i: (0, bi, i, 0))
    full = lambda *shape: pl.BlockSpec(shape, lambda bi, i: (0,) * len(shape))
    return pl.pallas_call(
        functools.partial(_even_out_kernel, d=d, c=c),
        grid=(b, t // tm),
        in_specs=[tok(c), tok(c), tok(c), tok(c), tok(c), tok2(c), tok(nq), tok(d),
                  pl.BlockSpec((1, 1, mod.shape[-1]), lambda bi, i: (bi, 0, 0)),
                  full(1, c), full(1, c), full(1, c), full(c, c), full(c + nq, d)],
        out_specs=tok(d),
        out_shape=jax.ShapeDtypeStruct((b, t, d), F32),
        compiler_params=_cparams("parallel", "parallel"),
        name="even_out_proj",
    )(y[0], y[1], r, v, g, kd, batt, x, mod, ep['ln_w'], ep['ln_b'], ep['r_k'], tabs['e'], ep['w_out'])


def _filter_kernel(z_ref, t_ref, w1_ref, b1_ref, w2_ref, b2_ref, w3_ref, b3_ref, fr_ref, wo_ref, dl_ref, o_ref, *, d):
    fr = fr_ref[...]
    h = jnp.sin(fr * (_dot_hi(z_ref[...], w1_ref[...]) + b1_ref[...]))
    h = jnp.sin(fr * (_dot_hi(h, w2_ref[...]) + b2_ref[...]))
    h = jnp.sin(fr * (_dot_hi(h, w3_ref[...]) + b3_ref[...]))
    filt = _dot_hi(h, wo_ref[...])
    modu = jnp.exp(-t_ref[...] * dl_ref[...]) + HY_MOD_SHIFT
    for q in range(o_ref.shape[0]):
        o_ref[q] = filt[:, q * d:(q + 1) * d] * modu


def _hyena_filters(n, op, d):
    t = np.linspace(0.0, 1.0, n, dtype=np.float32)[:, None]
    ang = (2.0 * math.pi * np.arange(n, dtype=np.float32)[:, None] / np.float32(n)).astype(np.float32)
    f = np.linspace(1e-4, HY_BANDS - 1, HY_BANDS, dtype=np.float32)[None, :]
    zfeat = jnp.concatenate([jnp.asarray(t), jnp.cos(jnp.asarray(f * ang)), -jnp.sin(jnp.asarray(f * ang))], axis=-1)
    emb_pad = op['f_w1'].shape[0]
    zfeat = jnp.pad(zfeat, ((0, 0), (0, emb_pad - HY_EMB)))
    deltas = np.abs(np.linspace(math.log(HY_TARGET) / HY_SLOW_PCT, math.log(HY_TARGET) / HY_FAST_PCT, d,
                                dtype=np.float32))[None, :]
    tn = min(n, 256)
    nq = 2 * HY_ORDER
    hf = op['f_w2'].shape[0]
    full = lambda *shape: pl.BlockSpec(shape, lambda i: (0,) * len(shape))
    return pl.pallas_call(
        functools.partial(_filter_kernel, d=d),
        grid=(n // tn,),
        in_specs=[pl.BlockSpec((tn, emb_pad), lambda i: (i, 0)), pl.BlockSpec((tn, 1), lambda i: (i, 0)),
                  full(emb_pad, hf), full(1, hf), full(hf, hf), full(1, hf), full(hf, hf), full(1, hf), full(1, hf),
                  full(hf, nq * d), full(1, d)],
        out_specs=pl.BlockSpec((nq, tn, d), lambda i: (0, i, 0)),
        out_shape=jax.ShapeDtypeStruct((nq, n, d), F32),
        compiler_params=_cparams("parallel"),
        name="hyena_filter",
    )(zfeat, jnp.asarray(t), op['f_w1'], op['f_b1'], op['f_w2'], op['f_b2'], op['f_w3'], op['f_b3'], op['f_freq'],
      op['f_out'], jnp.asarray(deltas))


def _dft(n, rows, cols, sign=-1.0):
    k = np.arange(rows, dtype=np.float64)[:, None]
    m = np.arange(cols, dtype=np.float64)[None, :]
    ang = sign * 2.0 * np.pi * ((k * m) % n) / n
    return np.cos(ang), np.sin(ang)


def _stack(re, im):
    return jnp.asarray(np.concatenate([re, im], axis=0).astype(np.float32)).astype(BF16)


def _fft_tables(n_seq):
    n = 2 * n_seq
    n1 = n // FFT_N2
    f1r, f1i = _dft(n1, n1, n1 // 2)
    h1r, h1i = _dft(n1, n1 // 2, n1, sign=1.0)
    k1 = np.arange(n1, dtype=np.float64)[:, None, None]
    k2 = np.arange(FFT_N2, dtype=np.float64)[None, :, None]
    j2 = np.arange(FFT_N2, dtype=np.float64)[None, None, :]
    ang = -2.0 * np.pi * (((k2 * j2 * n1) + k1 * j2) % n) / n
    gr, gi = np.cos(ang), np.sin(ang)
    g_fwd = np.concatenate([gr, gi], axis=1)
    g_inv = np.concatenate([np.swapaxes(gr, 1, 2), np.swapaxes(gi, 1, 2)], axis=1)
    eye = np.eye(SUBLANES)
    kron = lambda m: np.kron(m, eye)
    return {'f1': _stack(kron(f1r), kron(f1i)), 'h1': _stack(kron(h1r), kron(h1i)),
            'g_fwd': jnp.asarray(g_fwd.astype(np.float32)).astype(BF16),
            'g_inv': jnp.asarray(g_inv.astype(np.float32)).astype(BF16), 'n1': n1}


def _dense_tables(n_seq):
    n = 2 * n_seq
    fr, fi = _dft(n, n, n_seq)
    hr, hi = _dft(n, n_seq, n, sign=1.0)
    return {'f': _stack(fr, fi), 'h': _stack(hr, hi)}


def _fft_a_kernel(f_ref, zr_ref, zi_ref, ar_ref, ai_ref, *, n1, cplx):
    f = f_ref[...]
    half, tj, d = zr_ref.shape[1:]
    m = n1 * SUBLANES
    res_r, res_i = [], []
    for s0 in range(0, tj, SUBLANES):
        js = slice(s0, s0 + SUBLANES)
        zr = zr_ref[0, :, js, :].reshape(half * SUBLANES, d)
        p = jnp.dot(f, zr.astype(BF16), preferred_element_type=F32)
        if cplx:
            zi = zi_ref[0, :, js, :].reshape(half * SUBLANES, d)
            q = jnp.dot(f, zi.astype(BF16), preferred_element_type=F32)
            res_r.append((p[0:m] - q[m:]).reshape(n1, SUBLANES, d))
            res_i.append((p[m:] + q[0:m]).reshape(n1, SUBLANES, d))
        else:
            res_r.append(p[0:m].reshape(n1, SUBLANES, d))
            res_i.append(p[m:].reshape(n1, SUBLANES, d))
    ar_ref[0] = jnp.concatenate(res_r, axis=1).astype(BF16)
    ai_ref[0] = jnp.concatenate(res_i, axis=1).astype(BF16)


def _fft_a(u5, q, tabs, cplx):
    _, s, half, n2, d = u5.shape
    n1 = tabs['n1']
    assert half == n1 // 2 and n2 == FFT_N2
    sp = s // 2 if cplx else s
    tj = 2 * SUBLANES
    zi_map = (lambda si, j: (q, si + sp, 0, j, 0)) if cplx else (lambda si, j: (q, si, 0, j, 0))
    out = pl.BlockSpec((1, n1, tj, d), lambda si, j: (si, 0, j, 0))
    return pl.pallas_call(
        functools.partial(_fft_a_kernel, n1=n1, cplx=cplx),
        grid=(sp, n2 // tj),
        in_specs=[pl.BlockSpec((2 * n1 * SUBLANES, half * SUBLANES), lambda si, j: (0, 0)),
                  pl.BlockSpec((None, 1, half, tj, d), lambda si, j: (q, si, 0, j, 0)),
                  pl.BlockSpec((None, 1, half, tj, d), zi_map)],
        out_specs=[out, out],
        out_shape=[jax.ShapeDtypeStruct((sp, n1, n2, d), BF16)] * 2,
        compiler_params=_cparams("parallel", "parallel"),
        name="fft_stage1",
    )(tabs['f1'], u5, u5)


def _cplx_mm(s, xr, xi, conj):
    p = jnp.dot(s, xr.astype(BF16), preferred_element_type=F32)
    q = jnp.dot(s, xi.astype(BF16), preferred_element_type=F32)
    m = s.shape[0] // 2
    if conj:
        return p[0:m] + q[m:], q[0:m] - p[m:]
    return p[0:m] - q[m:], p[m:] + q[0:m]


def _fft_b_kernel(gf_ref, gi_ref, ar_ref, ai_ref, kr_ref, ki_ref, dr_ref, di_ref):
    kr = kr_ref[0]
    ki = ki_ref[0]
    seqs = range(ar_ref.shape[0])
    c = [_cplx_mm(gf_ref[0], ar_ref[s, 0], ai_ref[s, 0], False) for s in seqs]
    e = [(cr * kr - ci * ki, cr * ki + ci * kr) for cr, ci in c]
    dd = [_cplx_mm(gi_ref[0], er, ei, True) for er, ei in e]
    for s, (dr, di) in zip(seqs, dd):
        dr_ref[s, 0] = dr.astype(BF16)
        di_ref[s, 0] = di.astype(BF16)


def _fft_b(ar4, ai4, kr, ki, order, tabs):
    sp, n1, _, d = ar4.shape
    ns = 2 if sp % 2 == 0 else 1
    blk = pl.BlockSpec((ns, 1, FFT_N2, d), lambda k1, si: (si, k1, 0, 0))
    tab = pl.BlockSpec((1, 2 * FFT_N2, FFT_N2), lambda k1, si: (k1, 0, 0))
    kb = pl.BlockSpec((None, 1, FFT_N2, d), lambda k1, si: (order, k1, 0, 0))
    return pl.pallas_call(
        _fft_b_kernel,
        grid=(n1, sp // ns),
        in_specs=[tab, tab, blk, blk, kb, kb],
        out_specs=[blk, blk],
        out_shape=[jax.ShapeDtypeStruct((sp, n1, FFT_N2, d), BF16)] * 2,
        compiler_params=_cparams("parallel", "arbitrary"),
        name="fft_stage2_filter",
    )(tabs['g_fwd'], tabs['g_inv'], ar4, ai4, kr, ki)


def _fft_c_kernel(h_ref, dr_ref, di_ref, u_ref, x_ref, sk_ref, o_ref):
    h = h_ref[...]
    n1, tj, d = dr_ref.shape[1:]
    half = n1 // 2
    sk = sk_ref[...]
    dr_all = dr_ref[0].astype(F32)
    di_all = di_ref[0].astype(F32)
    for s0 in range(0, tj, SUBLANES):
        js = slice(s0, s0 + SUBLANES)
        yr, yi = _cplx_mm(h, dr_all[:, js, :].reshape(n1 * SUBLANES, d),
                          di_all[:, js, :].reshape(n1 * SUBLANES, d), False)
        for part, y in enumerate((yr, yi)):
            u = u_ref[part, 0, :, js, :]
            o_ref[part, 0, :, js, :] = x_ref[part, 0, :, js, :] * (y.reshape(half, SUBLANES, d) + u * sk)


def _fft_c(dr, di, u6, uq, x6, xq, skip, tabs):
    sp, n1, n2, d = dr.shape
    half = n1 // 2
    tj = 2 * SUBLANES
    pair = lambda q: pl.BlockSpec((None, 2, 1, half, tj, d), lambda si, j: (q, 0, si, 0, j, 0))
    dblk = pl.BlockSpec((1, n1, tj, d), lambda si, j: (si, 0, j, 0))
    out = pl.pallas_call(
        _fft_c_kernel,
        grid=(sp, n2 // tj),
        in_specs=[pl.BlockSpec((2 * half * SUBLANES, n1 * SUBLANES), lambda si, j: (0, 0)), dblk, dblk,
                  pair(uq), pair(xq), pl.BlockSpec((1, d), lambda si, j: (0, 0))],
        out_specs=pl.BlockSpec((2, 1, half, tj, d), lambda si, j: (0, si, 0, j, 0)),
        out_shape=jax.ShapeDtypeStruct((2, sp, half, n2, d), F32),
        compiler_params=_cparams("parallel", "parallel"),
        name="fft_inverse_stage1_gate",
    )(tabs['h1'], dr, di, u6, x6, skip.reshape(1, d))
    return out


def _spec_b_kernel(gf_ref, ar_ref, ai_ref, kr_ref, ki_ref, *, scale):
    fr, fi = _cplx_mm(gf_ref[0], ar_ref[0, 0], ai_ref[0, 0], False)
    gr, gi = _cplx_mm(gf_ref[0], ar_ref[1, 0], ai_ref[1, 0], False)
    kr_ref[0, 0] = (fr + gr) * scale
    ki_ref[0, 0] = (fi - gi) * scale


def _filter_spectrum_fft(filt, tabs, d):
    nq, n, _ = filt.shape
    n1 = tabs['n1']
    ar, ai = _fft_a(filt.reshape(1, nq, n1 // 2, FFT_N2, d), 0, tabs, cplx=False)
    orders = nq // 2
    ar5 = ar.reshape(orders, 2, n1, FFT_N2, d)
    ai5 = ai.reshape(orders, 2, n1, FFT_N2, d)
    blk = pl.BlockSpec((None, 2, 1, FFT_N2, d), lambda o, k1: (o, 0, k1, 0, 0))
    out = pl.BlockSpec((1, 1, FFT_N2, d), lambda o, k1: (o, k1, 0, 0))
    return pl.pallas_call(
        functools.partial(_spec_b_kernel, scale=1.0 / (2 * n)),
        grid=(orders, n1),
        in_specs=[pl.BlockSpec((1, 2 * FFT_N2, FFT_N2), lambda o, k1: (k1, 0, 0)), blk, blk],
        out_specs=[out, out],
        out_shape=[jax.ShapeDtypeStruct((orders, n1, FFT_N2, d), F32)] * 2,
        compiler_params=_cparams("parallel", "parallel"),
        name="filter_spectrum",
    )(tabs['g_fwd'], ar5, ai5)


def _long_conv_fft(u, uq, x, xq, skip, kr, ki, order, tabs):
    _, b, n, d = u.shape
    half = tabs['n1'] // 2
    ar, ai = _fft_a(u.reshape(u.shape[0], b, half, FFT_N2, d), uq, tabs, cplx=True)
    dr, di = _fft_b(ar, ai, kr, ki, order, tabs)
    six = lambda a: a.reshape(a.shape[0], 2, b // 2, half, FFT_N2, d)
    return _fft_c(dr, di, six(u), uq, six(x), xq, skip, tabs).reshape(b, n, d)


def _dense_spec_kernel(f_ref, filt_ref, kr_ref, ki_ref, *, scale):
    f = f_ref[...]
    n = f.shape[0] // 2
    pf = jnp.dot(f, filt_ref[0, 0].astype(BF16), preferred_element_type=F32)
    pg = jnp.dot(f, filt_ref[0, 1].astype(BF16), preferred_element_type=F32)
    kr_ref[0] = (pf[0:n] + pg[0:n]) * scale
    ki_ref[0] = (pf[n:] - pg[n:]) * scale


def _filter_spectrum_dense(filt, tabs, d):
    nq, n, _ = filt.shape
    orders = nq // 2
    f4 = filt.reshape(orders, 2, n, d)
    out = pl.BlockSpec((1, 2 * n, d), lambda o: (o, 0, 0))
    return pl.pallas_call(
        functools.partial(_dense_spec_kernel, scale=1.0 / (2 * n)),
        grid=(orders,),
        in_specs=[pl.BlockSpec((4 * n, n), lambda o: (0, 0)), pl.BlockSpec((1, 2, n, d), lambda o: (o, 0, 0, 0))],
        out_specs=[out, out],
        out_shape=[jax.ShapeDtypeStruct((orders, 2 * n, d), F32)] * 2,
        compiler_params=_cparams("parallel"),
        name="filter_spectrum_dense",
    )(tabs['f'], f4)


def _dense_conv_kernel(f_ref, h_ref, u_ref, x_ref, kr_ref, ki_ref, sk_ref, o_ref):
    cr, ci = _cplx_mm(f_ref[...], u_ref[0, 0], u_ref[1, 0], False)
    kr = kr_ref[...]
    ki = ki_ref[...]
    yr, yi = _cplx_mm(h_ref[...], cr * kr - ci * ki, cr * ki + ci * kr, False)
    sk = sk_ref[...]
    o_ref[0, 0] = x_ref[0, 0] * (yr + u_ref[0, 0] * sk)
    o_ref[1, 0] = x_ref[1, 0] * (yi + u_ref[1, 0] * sk)


def _long_conv_dense(u, x, skip, kr, ki, tabs):
    b, n, d = u.shape
    sp = b // 2
    pair = pl.BlockSpec((2, 1, n, d), lambda si: (0, si, 0, 0))
    kb = pl.BlockSpec((2 * n, d), lambda si: (0, 0))
    out = pl.pallas_call(
        _dense_conv_kernel,
        grid=(sp,),
        in_specs=[pl.BlockSpec((4 * n, n), lambda si: (0, 0)), pl.BlockSpec((2 * n, 2 * n), lambda si: (0, 0)),
                  pair, pair, kb, kb, pl.BlockSpec((1, d), lambda si: (0, 0))],
        out_specs=pair,
        out_shape=jax.ShapeDtypeStruct((2, sp, n, d), F32),
        compiler_params=_cparams("parallel"),
        name="long_conv_dense",
    )(tabs['f'], tabs['h'], u.reshape(2, sp, n, d), x.reshape(2, sp, n, d), kr, ki, skip.reshape(1, d))
    return out.reshape(b, n, d)


def _hyena_mixer(x, g, mod, op, fft_tabs, dense_tabs):
    b, n, d = x.shape
    z = _proj_conv(x, g, mod, op['w_in'], op['b_in'], op['conv_w'], op['conv_b'])
    filt = _hyena_filters(n, op, d)
    if n <= DENSE_FFT_MAX:
        kr, ki = _filter_spectrum_dense(filt, dense_tabs, d)
        y = _long_conv_dense(z[0], z[1], op['skip'][0], kr[0], ki[0], dense_tabs)
        y = _long_conv_dense(y, z[2], op['skip'][1], kr[1], ki[1], dense_tabs)
    else:
        kr, ki = _filter_spectrum_fft(filt, fft_tabs, d)
        y = _long_conv_fft(z, 0, z, 1, op['skip'][0], kr, ki, 0, fft_tabs)
        y = _long_conv_fft(y[None], 0, z, 2, op['skip'][1], kr, ki, 1, fft_tabs)
    return _out_res(y, x, mod, op['w_out'], op['b_out'])


def _even_mixer(x, ctx, g, mod_l, mod_c, ep, tabs, need_ctx):
    rc, vc, kkc, gc, lwc, ac, kdc, qc, kac, vac = _even_prep(ctx, g, mod_c, ep, tabs, rope=False)
    rl, vl, kkl, gl, lwl, al, kdl, ql, kal, val = _even_prep(x, g, mod_l, ep, tabs, rope=True)
    b = x.shape[0]
    c = rl.shape[-1]
    h0 = jnp.zeros((2, b, HEAD_DIM, c), F32)
    y_ctx, s_ctx = _rwkv(rc, vc, kkc, lwc, ac, kdc, h0)
    y_lat, _ = _rwkv(rl, vl, kkl, lwl, al, kdl, s_ctx)
    b_lat = _attention(ql, kal, val, kac, vac, ep['sink'], local=True)
    x_new = _even_out(y_lat, rl, vl, gl, kdl, b_lat, x, mod_l, ep, tabs)
    if not need_ctx:
        return x_new, None
    b_ctx = _attention(qc, kac, vac, kac, vac, ep['sink'], local=False)
    ctx_new = _even_out(y_ctx, rc, vc, gc, kdc, b_ctx, ctx, mod_c, ep, tabs)
    return x_new, ctx_new


def _rope_tables(n_tokens):
    rows = n_tokens // GRID_W
    row = jnp.repeat(jnp.arange(rows), GRID_W).astype(F32)
    col = jnp.tile(jnp.arange(GRID_W), rows).astype(F32)
    n_freq = HEAD_DIM // 4
    inv = ROPE_THETA ** (-jnp.arange(n_freq, dtype=F32) / n_freq)
    ang = jnp.concatenate([row[:, None] * inv, col[:, None] * inv], axis=-1)
    cos, sin = jnp.cos(ang), jnp.sin(ang)
    reps = LANES // HEAD_DIM
    cos_t = jnp.tile(jnp.concatenate([cos, cos], axis=-1), (1, reps))
    sin_t = jnp.tile(jnp.concatenate([-sin, sin], axis=-1), (1, reps))
    return cos_t, sin_t


def _block_ones(width):
    idx = np.arange(width) // HEAD_DIM
    return jnp.asarray((idx[:, None] == idx[None, :]).astype(np.float32)).astype(BF16)


def _lora_pad(w):
    z = jnp.zeros_like(w[0])
    return jnp.stack([jnp.concatenate([w[0], z], axis=0), jnp.concatenate([z, w[1]], axis=0)], axis=0)


def kernel(x, c, ctx, c_ctx, ada_w, ada_b, norm1_g, norm2_g, ffn_up, ffn_conv_w, ffn_conv_b, ffn_down, ev_w_in, ev_mu_prev, ev_mu_next, ev_w0, ev_w2, ev_a0, ev_a2, ev_g2, ev_k_k, ev_k_a, ev_r_k, ev_ln_w, ev_ln_b, ev_q_norm, ev_k_norm, ev_sink, ev_w_out, od_w_in, od_b_in, od_conv_w, od_conv_b, od_f_w1, od_f_b1, od_f_w2, od_f_b2, od_f_w3, od_f_b3, od_f_freq, od_f_out, od_skip, od_w_out, od_b_out):
    bsz, seq, d = x.shape
    lc = ctx.shape[1]
    depth = ada_w.shape[0]
    a_width = ev_k_k.shape[-1]
    nq = ev_sink.shape[-1] * HEAD_DIM
    nk = B_KV_HEADS * HEAD_DIM

    cond = jnp.zeros((BF16_ROWS, d), F32).at[:bsz].set(c).at[bsz].set(c_ctx)
    mod = _ada_mod(cond, ada_w, ada_b)

    cos_t, sin_t = _rope_tables(seq)
    tabs = {'cos': cos_t, 'sin': sin_t, 'e': _block_ones(nq)}
    fft_tabs = _fft_tables(seq) if seq > DENSE_FFT_MAX else None
    dense_lat = _dense_tables(seq) if seq <= DENSE_FFT_MAX else None
    fft_ctx = _fft_tables(lc) if lc > DENSE_FFT_MAX else None
    dense_ctx = _dense_tables(lc) if lc <= DENSE_FFT_MAX else None

    for layer in range(depth):
        need_ctx = layer < depth - 1
        even = layer % 2 == 0
        j = layer // 2
        mod_l = mod[layer, :bsz].reshape(bsz, 1, 6 * d)
        mod_c = jnp.broadcast_to(mod[layer, bsz].reshape(1, 1, 6 * d), (bsz, 1, 6 * d))
        if even:
            ep = {'w_in': ev_w_in[j].astype(BF16), 'mu_prev': ev_mu_prev[j][None], 'mu_next': ev_mu_next[j][None],
                  'w0': ev_w0[j], 'w2pad': _lora_pad(ev_w2[j]).astype(BF16), 'a0': ev_a0[j],
                  'a2pad': _lora_pad(ev_a2[j]).astype(BF16), 'g2': ev_g2[j].astype(BF16),
                  'k_k': ev_k_k[j][None], 'k_a': ev_k_a[j][None], 'r_k': ev_r_k[j].reshape(1, a_width),
                  'ln_w': ev_ln_w[j][None], 'ln_b': ev_ln_b[j][None],
                  'q_norm_t': jnp.tile(ev_q_norm[j], nq // HEAD_DIM)[None],
                  'k_norm_t': jnp.tile(ev_k_norm[j], nk // HEAD_DIM)[None],
                  'sink': ev_sink[j], 'w_out': ev_w_out[j].astype(BF16), 'nq': nq, 'nk': nk}
            x, ctx_new = _even_mixer(x, ctx, norm1_g[layer], mod_l, mod_c, ep, tabs, need_ctx)
        else:
            emb_pad = HEAD_DIM
            op = {'w_in': od_w_in[j].astype(BF16), 'b_in': od_b_in[j], 'conv_w': od_conv_w[j],
                  'conv_b': od_conv_b[j],
                  'f_w1': jnp.pad(od_f_w1[j], ((0, emb_pad - HY_EMB), (0, 0))), 'f_b1': od_f_b1[j][None],
                  'f_w2': od_f_w2[j], 'f_b2': od_f_b2[j][None], 'f_w3': od_f_w3[j], 'f_b3': od_f_b3[j][None],
                  'f_freq': od_f_freq[j][None], 'f_out': od_f_out[j], 'skip': od_skip[j],
                  'w_out': od_w_out[j].astype(BF16), 'b_out': od_b_out[j]}
            ctx_new = _hyena_mixer(ctx, norm1_g[layer], mod_c, op, fft_ctx, dense_ctx) if need_ctx else None
            x = _hyena_mixer(x, norm1_g[layer], mod_l, op, fft_tabs, dense_lat)
        w_up = ffn_up[layer].astype(BF16)
        w_down = ffn_down[layer].astype(BF16)
        x = _conv_ffn(x, norm2_g[layer], mod_l, w_up, ffn_conv_w[layer], ffn_conv_b[layer], w_down)
        if need_ctx:
            ctx = _conv_ffn(ctx_new, norm2_g[layer], mod_c, w_up, ffn_conv_w[layer], ffn_conv_b[layer], w_down)
    return x
```

```python
import functools
import math

import numpy as np
import jax
import jax.numpy as jnp
from jax import lax
from jax.experimental import pallas as pl
from jax.experimental.pallas import tpu as pltpu

F32 = jnp.float32
BF16 = jnp.bfloat16

HEAD_DIM = 64
GRID_W = 64
DECAY_LORA = 64
ICLR_LORA = 64
GATE_LORA = 128
RWKV_GN_EPS = 64e-5
B_KV_HEADS = 2
WINDOW = 128
BLOCK = 128
ROPE_THETA = 10000.0
MASK_VALUE = -1e30
HY_ORDER = 2
HY_EMB = 33
HY_BANDS = (HY_EMB - 1) // 2
HY_TARGET = 1e-2
HY_FAST_PCT = 0.3
HY_SLOW_PCT = 1.5
HY_MOD_SHIFT = 0.05
NORM_EPS = 1e-6

V7X_VMEM_BYTES = 64 * 1024 * 1024
VMEM_LIMIT_BYTES = V7X_VMEM_BYTES * 3 // 4
LANES = 128
V7X_MXU_DIM = 256
SUBLANES = 8
BF16_ROWS = 16
FFN_ROW_PARTS = 2
RWKV_CHUNK = 64
RWKV_PREP_CHUNKS = 8
RWKV_INV_BASE = 8
FFT_N2 = 128
DENSE_FFT_MAX = 512

HIGHEST = lax.Precision.HIGHEST


def _cparams(*sem):
    return pltpu.CompilerParams(dimension_semantics=sem, vmem_limit_bytes=VMEM_LIMIT_BYTES)


def _dot(a, b):
    return jnp.dot(a.astype(BF16), b.astype(BF16), preferred_element_type=F32)


def _dot_nt(a, b):
    return lax.dot_general(a.astype(BF16), b.astype(BF16), (((1,), (1,)), ((), ())), preferred_element_type=F32)


def _dot_hi(a, b):
    return jnp.dot(a, b, preferred_element_type=F32, precision=HIGHEST)


def _segsum(x, e_ref):
    hi = x.astype(BF16)
    lo = (x - hi.astype(F32)).astype(BF16)
    w = x.shape[-1]
    tile = min(w, V7X_MXU_DIM)
    e = e_ref[0:tile, 0:tile]
    both = jnp.concatenate([hi, lo], axis=0)
    parts = [jnp.dot(both[:, c0:c0 + tile], e, preferred_element_type=F32) for c0 in range(0, w, tile)]
    full = parts[0] if len(parts) == 1 else jnp.concatenate(parts, axis=1)
    m = x.shape[0]
    return full[0:m] + full[m:]


def _norm_mod(x, g, shift, scale):
    xn = x * lax.rsqrt(jnp.mean(x * x, axis=-1, keepdims=True) + NORM_EPS)
    return (xn * g) * (1.0 + scale) + shift


def _row_tile(t, want):
    tm = min(t, want)
    assert t % tm == 0 and tm % BF16_ROWS == 0
    return tm


def _ada_kernel(c_ref, w_ref, b_ref, o_ref):
    c = c_ref[...]
    s = c * jax.nn.sigmoid(c)
    o_ref[0] = _dot(s, w_ref[0]) + b_ref[0]


def _ada_mod(cond, ada_w, ada_b):
    depth, d, n = ada_w.shape
    tn = n // 4
    rows = cond.shape[0]
    return pl.pallas_call(
        _ada_kernel,
        grid=(depth, n // tn),
        in_specs=[pl.BlockSpec((rows, d), lambda l, j: (0, 0)),
                  pl.BlockSpec((1, d, tn), lambda l, j: (l, 0, j)),
                  pl.BlockSpec((1, 1, tn), lambda l, j: (l, 0, j))],
        out_specs=pl.BlockSpec((1, rows, tn), lambda l, j: (l, 0, j)),
        out_shape=jax.ShapeDtypeStruct((depth, rows, n), F32),
        compiler_params=_cparams("parallel", "parallel"),
        name="ada_mod",
    )(cond, ada_w, ada_b.reshape(depth, 1, n))


def _halo_specs(tm, t, d, nargs):
    r = tm // BF16_ROWS
    last = t // BF16_ROWS - 1
    if nargs == 3:
        prev = pl.BlockSpec((1, BF16_ROWS, d), lambda bi, i, j: (bi, jnp.maximum(i * r - 1, 0), 0))
        nxt = pl.BlockSpec((1, BF16_ROWS, d), lambda bi, i, j: (bi, jnp.minimum((i + 1) * r, last), 0))
    else:
        prev = pl.BlockSpec((1, BF16_ROWS, d), lambda bi, i: (bi, jnp.maximum(i * r - 1, 0), 0))
        nxt = pl.BlockSpec((1, BF16_ROWS, d), lambda bi, i: (bi, jnp.minimum((i + 1) * r, last), 0))
    return prev, nxt


def _fill_h(x_ref, xp_ref, xn_ref, g, shift, scale, h_scr, tm):
    h_scr[0:BF16_ROWS] = _norm_mod(xp_ref[0], g, shift, scale).astype(BF16)
    h_scr[BF16_ROWS:BF16_ROWS + tm] = _norm_mod(x_ref[0], g, shift, scale).astype(BF16)
    h_scr[BF16_ROWS + tm:2 * BF16_ROWS + tm] = _norm_mod(xn_ref[0], g, shift, scale).astype(BF16)


def _conv3_rows(u, u_scr, cw, cb, tm, t_total, part=0, nparts=1):
    i = pl.program_id(1)
    o = BF16_ROWS
    rows = tm // nparts
    base = part * (rows + 2 * o)
    head_ok = (i > 0) if part == 0 else True
    tail_ok = (i < t_total // tm - 1) if part == nparts - 1 else True
    u_scr[base:base + o] = jnp.where(head_ok, u[0:o], 0.0)
    u_scr[base + o:base + o + rows] = u[o:o + rows]
    u_scr[base + o + rows:base + 2 * o + rows] = jnp.where(tail_ok, u[o + rows:], 0.0)
    s = base + o
    return (u_scr[s - 1:s - 1 + rows] * cw[0:1] + u_scr[s:s + rows] * cw[1:2] + u_scr[s + 1:s + 1 + rows] * cw[2:3]
            + cb)


def _proj_conv_kernel(x_ref, xp_ref, xn_ref, g_ref, mod_ref, w_ref, b_ref, cw_ref, cb_ref, o_ref, h_scr, u_scr,
                      *, d, tm, t_total, tn):
    m = mod_ref[0]
    _fill_h(x_ref, xp_ref, xn_ref, g_ref[...], m[:, 0:d], m[:, d:2 * d], h_scr, tm)
    nparts = FFN_ROW_PARTS
    rows = tm // nparts
    n = w_ref.shape[1]

    def project(j):
        cols = slice(j * tn, (j + 1) * tn)
        return [jnp.dot(h_scr[p * rows:(p + 1) * rows + 2 * BF16_ROWS], w_ref[:, cols],
                        preferred_element_type=F32) + b_ref[:, cols] for p in range(nparts)]

    nxt_u = project(0)
    for j in range(n // tn):
        cols = slice(j * tn, (j + 1) * tn)
        us = nxt_u
        if (j + 1) * tn < n:
            nxt_u = project(j + 1)
        q, lane0 = divmod(j * tn, d)
        for p in range(nparts):
            o_ref[q, 0, p * rows:(p + 1) * rows, lane0:lane0 + tn] = _conv3_rows(
                us[p], u_scr.at[j % 2], cw_ref[:, cols], cb_ref[:, cols], tm, t_total, p, nparts)


def _proj_conv(x, g, mod, w, bias, cw, cb):
    b, t, d = x.shape
    n = w.shape[1]
    tm = _row_tile(t, 512)
    tn = d // 2
    prev, nxt = _halo_specs(tm, t, d, 2)
    resident = lambda *shape: pl.BlockSpec(shape, lambda bi, i: (0,) * len(shape), pipeline_mode=pl.Buffered(1))
    return pl.pallas_call(
        functools.partial(_proj_conv_kernel, d=d, tm=tm, t_total=t, tn=tn),
        grid=(b, t // tm),
        in_specs=[pl.BlockSpec((1, tm, d), lambda bi, i: (bi, i, 0)), prev, nxt,
                  pl.BlockSpec((1, d), lambda bi, i: (0, 0)),
                  pl.BlockSpec((1, 1, mod.shape[-1]), lambda bi, i: (bi, 0, 0)),
                  resident(d, n), resident(1, n), resident(3, n), resident(1, n)],
        out_specs=pl.BlockSpec((n // d, 1, tm, d), lambda bi, i: (0, bi, i, 0)),
        out_shape=jax.ShapeDtypeStruct((n // d, b, t, d), F32),
        scratch_shapes=[pltpu.VMEM((tm + 2 * BF16_ROWS, d), BF16),
                        pltpu.VMEM((2, tm + 2 * BF16_ROWS * FFN_ROW_PARTS, tn), F32)],
        compiler_params=_cparams("parallel", "parallel"),
        name="hyena_in_proj",
    )(x, x, x, g.reshape(1, d), mod, w, bias.reshape(1, n), cw, cb.reshape(1, n))


def _ffn_kernel(x_ref, xp_ref, xn_ref, g_ref, mod_ref, wup_ref, cw_ref, cb_ref, wd_ref,
                o_ref, h_scr, ug_scr, uv_scr, acc_scr, *, d, tm, t_total, f, tf):
    m = mod_ref[0]
    _fill_h(x_ref, xp_ref, xn_ref, g_ref[...], m[:, 3 * d:4 * d], m[:, 4 * d:5 * d], h_scr, tm)
    nparts = FFN_ROW_PARTS
    rows = tm // nparts
    nf = f // tf

    def up_proj(j):
        hs = [h_scr[p * rows:(p + 1) * rows + 2 * BF16_ROWS] for p in range(nparts)]
        return ([jnp.dot(h, wup_ref[:, j * tf:(j + 1) * tf], preferred_element_type=F32) for h in hs],
                [jnp.dot(h, wup_ref[:, f + j * tf:f + (j + 1) * tf], preferred_element_type=F32) for h in hs])

    nxt_up = up_proj(0)
    for j in range(nf):
        gs = slice(j * tf, (j + 1) * tf)
        vs = slice(f + j * tf, f + (j + 1) * tf)
        ug, uv = nxt_up
        if j + 1 < nf:
            nxt_up = up_proj(j + 1)
        for p in range(nparts):
            gate = _conv3_rows(ug[p], ug_scr.at[j % 2], cw_ref[:, gs], cb_ref[:, gs], tm, t_total, p, nparts)
            val = _conv3_rows(uv[p], uv_scr.at[j % 2], cw_ref[:, vs], cb_ref[:, vs], tm, t_total, p, nparts)
            act = (gate * jax.nn.sigmoid(gate)) * val
            down = jnp.dot(act.astype(BF16), wd_ref[gs, :], preferred_element_type=F32)
            part = slice(p * rows, (p + 1) * rows)
            if j == 0:
                acc_scr[part] = down
            else:
                acc_scr[part] += down
    o_ref[0] = x_ref[0] + m[:, 5 * d:6 * d] * acc_scr[...]


def _conv_ffn(x, g, mod, w_up, cw, cb, w_down):
    b, t, d = x.shape
    f = w_down.shape[0]
    tm = _row_tile(t, 1024)
    tf = 256 if f % 256 == 0 else 128
    prev, nxt = _halo_specs(tm, t, d, 2)
    resident = lambda *shape: pl.BlockSpec(shape, lambda bi, i: (0,) * len(shape), pipeline_mode=pl.Buffered(1))
    u_rows = tm + 2 * BF16_ROWS * FFN_ROW_PARTS
    return pl.pallas_call(
        functools.partial(_ffn_kernel, d=d, tm=tm, t_total=t, f=f, tf=tf),
        grid=(b, t // tm),
        in_specs=[pl.BlockSpec((1, tm, d), lambda bi, i: (bi, i, 0)), prev, nxt,
                  pl.BlockSpec((1, d), lambda bi, i: (0, 0)),
                  pl.BlockSpec((1, 1, mod.shape[-1]), lambda bi, i: (bi, 0, 0)),
                  resident(d, 2 * f), resident(3, 2 * f), resident(1, 2 * f), resident(f, d)],
        out_specs=pl.BlockSpec((1, tm, d), lambda bi, i: (bi, i, 0)),
        out_shape=jax.ShapeDtypeStruct((b, t, d), F32),
        scratch_shapes=[pltpu.VMEM((tm + 2 * BF16_ROWS, d), BF16),
                        pltpu.VMEM((2, u_rows, tf), F32),
                        pltpu.VMEM((2, u_rows, tf), F32),
                        pltpu.VMEM((tm, d), F32)],
        compiler_params=_cparams("parallel", "parallel"),
        name="conv_ffn",
    )(x, x, x, g.reshape(1, d), mod, w_up, cw, cb.reshape(1, 2 * f), w_down)


def _out_res_kernel(y_ref, x_ref, mod_ref, w_ref, b_ref, o_ref, *, d):
    o = _dot(y_ref[0], w_ref[...]) + b_ref[...]
    o_ref[0] = x_ref[0] + mod_ref[0][:, 2 * d:3 * d] * o


def _out_res(y, x, mod, w, bias):
    b, t, d = x.shape
    tm = _row_tile(t, 512)
    return pl.pallas_call(
        functools.partial(_out_res_kernel, d=d),
        grid=(b, t // tm),
        in_specs=[pl.BlockSpec((1, tm, d), lambda bi, i: (bi, i, 0)),
                  pl.BlockSpec((1, tm, d), lambda bi, i: (bi, i, 0)),
                  pl.BlockSpec((1, 1, mod.shape[-1]), lambda bi, i: (bi, 0, 0)),
                  pl.BlockSpec((d, d), lambda bi, i: (0, 0)),
                  pl.BlockSpec((1, d), lambda bi, i: (0, 0))],
        out_specs=pl.BlockSpec((1, tm, d), lambda bi, i: (bi, i, 0)),
        out_shape=jax.ShapeDtypeStruct((b, t, d), F32),
        compiler_params=_cparams("parallel", "parallel"),
        name="out_proj_residual",
    )(y, x, mod, w, bias.reshape(1, d))


def _even_prep_kernel(x_ref, xp_ref, xn_ref, gn_ref, mod_ref, win_ref, mup_ref, mun_ref, w0_ref, w2_ref, a0_ref,
                      a2_ref, g2_ref, kk_ref, ka_ref, qn_ref, kn_ref, cos_ref, sin_ref, e_ref,
                      r_ref, v_ref, kkn_ref, g_ref, lw_ref, a_ref, kd_ref, q_ref, katt_ref, vatt_ref,
                      h_scr, p_scr, *, tm, d, c, a_in, rope):
    i = pl.program_id(1)
    nt = pl.num_programs(1)
    m = mod_ref[0]
    _fill_h(x_ref, xp_ref, xn_ref, gn_ref[...], m[:, 0:d], m[:, d:2 * d], h_scr, tm)
    p_scr[...] = jnp.dot(h_scr[...], win_ref[...], preferred_element_type=F32)
    o = BF16_ROWS
    pa = p_scr[o:o + tm, 0:a_in]
    rows = lax.broadcasted_iota(jnp.int32, (tm, 1), 0)
    prev = jnp.where((rows == 0) & (i == 0), 0.0, p_scr[o - 1:o - 1 + tm, 0:a_in])
    nxt = jnp.where((rows == tm - 1) & (i == nt - 1), 0.0, p_scr[o + 1:o + 1 + tm, 0:a_in])
    za = pa + mup_ref[...] * (prev - pa) + mun_ref[...] * (nxt - pa)

    r = za[:, 0:c]
    k = za[:, c:2 * c]
    v = za[:, 2 * c:3 * c]
    wd = jnp.tanh(za[:, 3 * c:3 * c + 2 * DECAY_LORA])
    ad = za[:, 3 * c + 2 * DECAY_LORA:3 * c + 2 * DECAY_LORA + 2 * ICLR_LORA]
    gd = za[:, 3 * c + 2 * DECAY_LORA + 2 * ICLR_LORA:a_in]
    r_ref[0] = r
    v_ref[0] = v
    g_ref[0] = _dot(jax.nn.sigmoid(gd), g2_ref[...])
    kkv = k * kk_ref[...]
    ss = _segsum(kkv * kkv, e_ref)
    kkn_ref[0] = kkv / jnp.maximum(jnp.sqrt(ss), 1e-12)
    for dd in range(2):
        y = w0_ref[dd:dd + 1] + _dot(wd, w2_ref[dd])
        lw_ref[dd, 0] = -math.exp(-0.5) * jax.nn.sigmoid(y)
        a = jax.nn.sigmoid(a0_ref[dd:dd + 1] + _dot(ad, a2_ref[dd]))
        a_ref[dd, 0] = a
        kd_ref[dd, 0] = k * (1.0 + (a - 1.0) * ka_ref[...])

    pb = p_scr[o:o + tm, a_in:]
    nq = q_ref.shape[-1]
    nk = katt_ref.shape[-1]
    q = pb[:, 0:nq]
    kat = pb[:, nq:nq + nk]
    vatt_ref[0] = pb[:, nq + nk:]
    half = HEAD_DIM // 2

    def norm_rope(x, gain, e):
        w = x.shape[-1]
        ms = _segsum(x * x, e) * (1.0 / HEAD_DIM)
        xn = x * lax.rsqrt(ms + NORM_EPS) * gain
        if not rope:
            return xn
        lane = lax.broadcasted_iota(jnp.int32, (1, w), 1)
        first = (lane % HEAD_DIM) < half
        swapped = jnp.where(first, pltpu.roll(xn, w - half, 1), pltpu.roll(xn, half, 1))
        reps = w // LANES
        cos = jnp.concatenate([cos_ref[...]] * reps, axis=1) if reps > 1 else cos_ref[...]
        sin = jnp.concatenate([sin_ref[...]] * reps, axis=1) if reps > 1 else sin_ref[...]
        return xn * cos + swapped * sin

    q_ref[0] = norm_rope(q, qn_ref[...], e_ref[...])
    katt_ref[0] = norm_rope(kat, kn_ref[...], e_ref[0:nk, 0:nk])


def _even_prep(x, g, mod, ep, tabs, rope):
    b, t, d = x.shape
    n = ep['w_in'].shape[1]
    c = ep['k_k'].shape[-1]
    a_in = ep['mu_prev'].shape[-1]
    nq = ep['nq']
    nk = ep['nk']
    tm = _row_tile(t, 256)
    prev, nxt = _halo_specs(tm, t, d, 2)
    full = lambda *shape: pl.BlockSpec(shape, lambda bi, i: (0,) * len(shape))
    tok = lambda w: pl.BlockSpec((1, tm, w), lambda bi, i: (bi, i, 0))
    tok2 = lambda w: pl.BlockSpec((2, 1, tm, w), lambda bi, i: (0, bi, i, 0))
    sd = lambda *shape: jax.ShapeDtypeStruct(shape, F32)
    return pl.pallas_call(
        functools.partial(_even_prep_kernel, tm=tm, d=d, c=c, a_in=a_in, rope=rope),
        grid=(b, t // tm),
        in_specs=[tok(d), prev, nxt, full(1, d), pl.BlockSpec((1, 1, mod.shape[-1]), lambda bi, i: (bi, 0, 0)),
                  full(d, n), full(1, a_in), full(1, a_in), full(2, c), full(2, 2 * DECAY_LORA, c),
                  full(2, c), full(2, 2 * ICLR_LORA, c), full(GATE_LORA, c), full(1, c), full(1, c),
                  full(1, nq), full(1, nk),
                  pl.BlockSpec((tm, LANES), lambda bi, i: (i, 0)), pl.BlockSpec((tm, LANES), lambda bi, i: (i, 0)),
                  full(nq, nq)],
        out_specs=[tok(c), tok(c), tok(c), tok(c), tok2(c), tok2(c), tok2(c), tok(nq), tok(nk), tok(nk)],
        out_shape=[sd(b, t, c), sd(b, t, c), sd(b, t, c), sd(b, t, c), sd(2, b, t, c), sd(2, b, t, c),
                   sd(2, b, t, c), sd(b, t, nq), sd(b, t, nk), sd(b, t, nk)],
        scratch_shapes=[pltpu.VMEM((tm + 2 * BF16_ROWS, d), BF16), pltpu.VMEM((tm + 2 * BF16_ROWS, n), F32)],
        compiler_params=_cparams("parallel", "parallel"),
        name="even_in_proj_prep",
    )(x, x, x, g.reshape(1, d), mod, ep['w_in'], ep['mu_prev'], ep['mu_next'], ep['w0'], ep['w2pad'], ep['a0'], ep['a2pad'], ep['g2'], ep['k_k'],
      ep['k_a'], ep['q_norm_t'], ep['k_norm_t'], tabs['cos'][:t], tabs['sin'][:t], tabs['e'])


HEADS_PER_GROUP = 2
GROUP_LANES = HEADS_PER_GROUP * HEAD_DIM


def _block_diag(x, bmask):
    return jnp.where(bmask, jnp.concatenate([x] * HEADS_PER_GROUP, axis=0), jnp.zeros((), x.dtype))


def _head_mm(lhs, rhs, bmask, nt=False):
    dn = (((1,), (1,)), ((), ())) if nt else (((1,), (0,)), ((), ()))
    return lax.dot_general(lhs.astype(BF16), _block_diag(rhs.astype(BF16), bmask), dn, preferred_element_type=F32)


def _head_mm_tn(lhs, rhs, lane_head):
    full = lax.dot_general(lhs.astype(BF16), rhs.astype(BF16), (((0,), (0,)), ((), ())), preferred_element_type=F32)
    out = jnp.where(lane_head == 0, full[0:HEAD_DIM], 0.0)
    for h in range(1, HEADS_PER_GROUP):
        out = out + jnp.where(lane_head == h, full[h * HEAD_DIM:(h + 1) * HEAD_DIM], 0.0)
    return out


def _group_masks():
    r = lax.broadcasted_iota(jnp.int32, (GROUP_LANES, GROUP_LANES), 0)
    c = lax.broadcasted_iota(jnp.int32, (GROUP_LANES, GROUP_LANES), 1)
    bmask = (r // HEAD_DIM) == (c // HEAD_DIM)
    lane_head = lax.broadcasted_iota(jnp.int32, (1, GROUP_LANES), 1) // HEAD_DIM
    return bmask, lane_head


def _rwkv_prep_kernel(r_ref, v_ref, kk_ref, lw_ref, a_ref, kd_ref, r2_ref, yl_ref, t_ref, z_ref, *, cs):
    sgn = 1 - 2 * pl.program_id(0)
    row = lax.broadcasted_iota(jnp.int32, (cs, cs), 0)
    col = lax.broadcasted_iota(jnp.int32, (cs, cs), 1)
    tri = (((row - col) * sgn) >= 0).astype(F32)
    bmask, lane_head = _group_masks()
    t_idx = lax.broadcasted_iota(jnp.int32, (cs, GROUP_LANES), 0)
    s_idx = lax.broadcasted_iota(jnp.int32, (cs, GROUP_LANES), 1) % HEAD_DIM
    diff = (t_idx - s_idx) * sgn
    strict = diff > 0
    incl = diff >= 0
    eye = (diff == 0).astype(F32)
    base = min(RWKV_INV_BASE, cs)
    same_base = (t_idx // base) == (s_idx // base)
    groups = r_ref.shape[-1] // GROUP_LANES

    ch = []
    for c0 in range(0, r_ref.shape[1], cs):
        rows = slice(c0, c0 + cs)
        r, v, kk = r_ref[0, rows], v_ref[0, rows], kk_ref[0, rows]
        lw, a, kd = lw_ref[0, 0, rows], a_ref[0, 0, rows], kd_ref[0, 0, rows]
        g = _dot_hi(tri, lw)
        gp = g - lw
        gref = g[cs // 2:cs // 2 + 1]
        gend = jnp.sum(lw, axis=0, keepdims=True)
        bvec = kk * a
        full = {'a_t': -kk * jnp.exp(gp - gref), 'a_0': -kk * jnp.exp(gp),
                'b_t': bvec * jnp.exp(gref - g), 'b_e': bvec * jnp.exp(gend - g),
                'k_t': kd * jnp.exp(gref - g), 'k_e': kd * jnp.exp(gend - g),
                'r_t': r * jnp.exp(g - gref), 'r_0': r * jnp.exp(g), 'v': v,
                'wend': jnp.broadcast_to(jnp.exp(gend), (cs, v.shape[-1]))}
        for gi in range(groups):
            sl = slice(gi * GROUP_LANES, (gi + 1) * GROUP_LANES)
            c = {k: x[:, sl] for k, x in full.items()}
            c['rows'], c['sl'] = rows, sl
            ch.append(c)

    mm = lambda x, y, nt=False: [_head_mm(p_, q_, bmask, nt) for p_, q_ in zip(x, y)]
    get = lambda k: [c[k] for c in ch]
    ar = [jnp.concatenate([c['a_t'], c['r_t']], axis=0) for c in ch]
    gb = mm(ar, get('b_t'), True)
    gk = mm(ar, get('k_t'), True)
    nmat = [jnp.where(strict, x[0:cs], 0.0) for x in gb]
    g_rb = [jnp.where(incl, x[cs:], 0.0) for x in gb]
    g_k = [jnp.concatenate([jnp.where(strict, x[0:cs], 0.0), jnp.where(incl, x[cs:], 0.0)], axis=0) for x in gk]
    npow = [jnp.where(same_base, x, 0.0) for x in nmat]
    p = [eye + x for x in npow]
    for _ in range(int(math.log2(base)) - 1):
        npow = mm(npow, npow)
        p = [x + y for x, y in zip(p, mm(npow, p))]
    m = base
    while m < cs:
        off = ((t_idx // m) != (s_idx // m)) & ((t_idx // (2 * m)) == (s_idx // (2 * m)))
        q = mm([jnp.where(off, x, 0.0) for x in nmat], p)
        p = [x + y for x, y in zip(p, mm(p, q))]
        m *= 2
    vg = get('v')
    a2 = mm(p, get('a_0'))
    kv = mm(g_k, vg)
    u_v = mm(p, [x[0:cs] for x in kv])
    r2 = mm(g_rb, a2)
    yl1 = mm(g_rb, u_v)
    yl2 = [x[cs:] for x in kv]
    tt = [_head_mm_tn(x, c['b_e'], lane_head) for x, c in zip(a2, ch)]
    zz = [_head_mm_tn(jnp.concatenate([u, c['v']], axis=0), jnp.concatenate([c['b_e'], c['k_e']], axis=0),
                      lane_head) for u, c in zip(u_v, ch)]
    for i, c in enumerate(ch):
        rows, sl = c['rows'], c['sl']
        r2_ref[0, 0, rows, sl] = (c['r_0'] + r2[i]).astype(BF16)
        yl_ref[0, 0, rows, sl] = yl1[i] + yl2[i]
        t_ref[0, 0, rows, sl] = (eye * c['wend'] + tt[i]).astype(BF16)
        z_ref[0, 0, rows, sl] = zz[i]


def _rwkv_scan_kernel(h0_ref, r2f_ref, ylf_ref, tf_ref, zf_ref, r2b_ref, ylb_ref, tb_ref, zb_ref,
                      yf_ref, yb_ref, ht_ref, s_scr):
    ci = pl.program_id(0)

    @pl.when(ci == 0)
    def _():
        s_scr[...] = h0_ref[...]

    bmask, _ = _group_masks()
    ins = ((r2f_ref, ylf_ref, tf_ref, zf_ref, yf_ref), (r2b_ref, ylb_ref, tb_ref, zb_ref, yb_ref))
    for d, (r2_ref, yl_ref, t_ref, z_ref, y_ref) in enumerate(ins):
        for b in range(s_scr.shape[1]):
            for gi in range(s_scr.shape[-1] // GROUP_LANES):
                sl = slice(gi * GROUP_LANES, (gi + 1) * GROUP_LANES)
                s = s_scr[d, b, :, sl]
                y_ref[0, b, :, sl] = yl_ref[0, b, :, sl] + _head_mm(r2_ref[0, b, :, sl], s, bmask, nt=True)
                s_scr[d, b, :, sl] = _head_mm(s, t_ref[0, b, :, sl], bmask) + z_ref[0, b, :, sl]

    @pl.when(ci == pl.num_programs(0) - 1)
    def _():
        ht_ref[...] = s_scr[...]


def _rwkv(r, v, kk, lw, a, kd, h0):
    b, t, c = r.shape
    cs = min(RWKV_CHUNK, t)
    assert cs == HEAD_DIM and c % GROUP_LANES == 0
    nc = t // cs
    rows = cs * math.gcd(nc, RWKV_PREP_CHUNKS)
    tok = pl.BlockSpec((1, rows, c), lambda d, bi, ci: (bi, ci, 0))
    tok2 = pl.BlockSpec((1, 1, rows, c), lambda d, bi, ci: (d, bi, ci, 0))
    wide = jax.ShapeDtypeStruct((2, b, t, c), F32)
    mm_operand = jax.ShapeDtypeStruct((2, b, t, c), BF16)
    r2, yl, tt, zz = pl.pallas_call(
        functools.partial(_rwkv_prep_kernel, cs=cs),
        grid=(2, b, t // rows),
        in_specs=[tok, tok, tok, tok2, tok2, tok2],
        out_specs=[tok2, tok2, tok2, tok2],
        out_shape=[mm_operand, wide, mm_operand, wide],
        compiler_params=_cparams("parallel", "parallel", "parallel"),
        name="rwkv7_chunk_prep",
    )(r, v, kk, lw, a, kd)

    fwd = pl.BlockSpec((1, b, cs, c), lambda ci: (0, 0, ci, 0))
    bwd = pl.BlockSpec((1, b, cs, c), lambda ci: (1, 0, nc - 1 - ci, 0))
    st = pl.BlockSpec((2, b, HEAD_DIM, c), lambda ci: (0, 0, 0, 0))
    yf, yb, ht = pl.pallas_call(
        _rwkv_scan_kernel,
        grid=(nc,),
        in_specs=[st, fwd, fwd, fwd, fwd, bwd, bwd, bwd, bwd],
        out_specs=[pl.BlockSpec((1, b, cs, c), lambda ci: (0, 0, ci, 0)),
                   pl.BlockSpec((1, b, cs, c), lambda ci: (0, 0, nc - 1 - ci, 0)), st],
        out_shape=[jax.ShapeDtypeStruct((1, b, t, c), F32), jax.ShapeDtypeStruct((1, b, t, c), F32),
                   jax.ShapeDtypeStruct((2, b, HEAD_DIM, c), F32)],
        scratch_shapes=[pltpu.VMEM((2, b, HEAD_DIM, c), F32)],
        compiler_params=_cparams("arbitrary"),
        name="rwkv7_state_scan",
    )(h0, r2, yl, tt, zz, r2, yl, tt, zz)
    return (yf.reshape(b, t, c), yb.reshape(b, t, c)), ht


def _attn_kernel(sink_ref, q_ref, kp_ref, kc_ref, kn_ref, vp_ref, vc_ref, vn_ref, kx_ref, vx_ref, o_ref,
                 *, local, group):
    i = pl.program_id(1)
    nb = pl.num_programs(1)
    scale = HEAD_DIM ** -0.5
    assert math.frexp(scale)[0] == 0.5
    q = q_ref[0] * scale
    if local:
        k_all = jnp.concatenate([kp_ref[0], kc_ref[0], kn_ref[0], kx_ref[0]], axis=0)
        v_all = jnp.concatenate([vp_ref[0], vc_ref[0], vn_ref[0], vx_ref[0]], axis=0)
        qi = lax.broadcasted_iota(jnp.int32, (BLOCK, BLOCK), 0)
        kj = lax.broadcasted_iota(jnp.int32, (BLOCK, BLOCK), 1)
        assert WINDOW == BLOCK
        valid_prev = (kj >= qi) & (i > 0)
        valid_next = (kj <= qi) & (i < nb - 1)
    else:
        k_all = kx_ref[0]
        v_all = vx_ref[0]
    heads = range(q.shape[-1] // HEAD_DIM)
    kv = lambda x, h: x[:, (h // group) * HEAD_DIM:(h // group + 1) * HEAD_DIM]
    s = [_dot_nt(q[:, h * HEAD_DIM:(h + 1) * HEAD_DIM], kv(k_all, h)) for h in heads]
    if local:
        s = [jnp.concatenate([jnp.where(valid_prev, x[:, 0:BLOCK], MASK_VALUE), x[:, BLOCK:2 * BLOCK],
                              jnp.where(valid_next, x[:, 2 * BLOCK:3 * BLOCK], MASK_VALUE), x[:, 3 * BLOCK:]],
                             axis=1) for x in s]
    sink = [sink_ref[h] for h in heads]
    m = [jnp.maximum(jnp.max(x, axis=-1, keepdims=True), sk) for x, sk in zip(s, sink)]
    e = [jnp.exp(x - y) for x, y in zip(s, m)]
    den = [jnp.sum(x, axis=-1, keepdims=True) + jnp.exp(sk - y) for x, y, sk in zip(e, m, sink)]
    outs = [_dot(x, kv(v_all, h)) / dn for x, h, dn in zip(e, heads, den)]
    o_ref[0] = jnp.concatenate(outs, axis=1)


def _attention(q, k, v, kx, vx, sink, local):
    b, t, nq = q.shape
    nk = k.shape[-1]
    nb = t // BLOCK
    group = (nq // HEAD_DIM) // (nk // HEAD_DIM)
    lx = kx.shape[1]
    kv = lambda f: pl.BlockSpec((1, BLOCK, nk), f)
    pf = lambda bi, i: (bi, jnp.maximum(i - 1, 0), 0)
    cf = lambda bi, i: (bi, i, 0)
    nf = lambda bi, i: (bi, jnp.minimum(i + 1, nb - 1), 0)
    ctx = pl.BlockSpec((1, lx, nk), lambda bi, i: (bi, 0, 0))
    return pl.pallas_call(
        functools.partial(_attn_kernel, local=local, group=group),
        grid=(b, nb),
        in_specs=[pl.BlockSpec(memory_space=pltpu.SMEM),
                  pl.BlockSpec((1, BLOCK, nq), cf), kv(pf), kv(cf), kv(nf), kv(pf), kv(cf), kv(nf), ctx, ctx],
        out_specs=pl.BlockSpec((1, BLOCK, nq), cf),
        out_shape=jax.ShapeDtypeStruct((b, t, nq), F32),
        compiler_params=_cparams("parallel", "parallel"),
        name="window_attention" if local else "context_attention",
    )(sink, q, k, k, k, v, v, v, kx, vx)


def _even_out_kernel(yf_ref, yb_ref, r_ref, v_ref, g_ref, kd_ref, batt_ref, x_ref, mod_ref, lnw_ref, lnb_ref, rk_ref, e_ref,
                     w_ref, o_ref, *, d, c):
    y = yf_ref[0] + yb_ref[0]
    inv = 1.0 / HEAD_DIM
    mu = _segsum(y, e_ref) * inv
    yc = y - mu
    var = _segsum(yc * yc, e_ref) * inv
    yn = yc * lax.rsqrt(var + RWKV_GN_EPS) * lnw_ref[...] + lnb_ref[...]
    bonus = _segsum(r_ref[0] * (kd_ref[0, 0] + kd_ref[1, 0]) * rk_ref[...], e_ref)
    a_out = (yn + bonus * v_ref[0]) * g_ref[0]
    o = _dot(a_out, w_ref[0:c]) + _dot(batt_ref[0], w_ref[c:])
    o_ref[0] = x_ref[0] + mod_ref[0][:, 2 * d:3 * d] * o


def _even_out(y, r, v, g, kd, batt, x, mod, ep, tabs):
    b, t, d = x.shape
    c = r.shape[-1]
    nq = batt.shape[-1]
    tm = _row_tile(t, 512)
    tok = lambda w: pl.BlockSpec((1, tm, w), lambda bi, i: (bi, i, 0))
    tok2 = lambda w: pl.BlockSpec((2, 1, tm, w), lambda bi, i: (0, bi, i, 0))
    full = lambda *shape: pl.BlockSpec(shape, lambda bi, i: (0,) * len(shape))
    return pl.pallas_call(
        functools.partial(_even_out_kernel, d=d, c=c),
        grid=(b, t // tm),
        in_specs=[tok(c), tok(c), tok(c), tok(c), tok(c), tok2(c), tok(nq), tok(d),
                  pl.BlockSpec((1, 1, mod.shape[-1]), lambda bi, i: (bi, 0, 0)),
                  full(1, c), full(1, c), full(1, c), full(c, c), full(c + nq, d)],
        out_specs=tok(d),
        out_shape=jax.ShapeDtypeStruct((b, t, d), F32),
        compiler_params=_cparams("parallel", "parallel"),
        name="even_out_proj",
    )(y[0], y[1], r, v, g, kd, batt, x, mod, ep['ln_w'], ep['ln_b'], ep['r_k'], tabs['e'], ep['w_out'])


def _filter_kernel(z_ref, t_ref, w1_ref, b1_ref, w2_ref, b2_ref, w3_ref, b3_ref, fr_ref, wo_ref, dl_ref, o_ref, *, d):
    fr = fr_ref[...]
    h = jnp.sin(fr * (_dot_hi(z_ref[...], w1_ref[...]) + b1_ref[...]))
    h = jnp.sin(fr * (_dot_hi(h, w2_ref[...]) + b2_ref[...]))
    h = jnp.sin(fr * (_dot_hi(h, w3_ref[...]) + b3_ref[...]))
    filt = _dot_hi(h, wo_ref[...])
    modu = jnp.exp(-t_ref[...] * dl_ref[...]) + HY_MOD_SHIFT
    for q in range(o_ref.shape[0]):
        o_ref[q] = filt[:, q * d:(q + 1) * d] * modu


def _hyena_filters(n, op, d):
    t = np.linspace(0.0, 1.0, n, dtype=np.float32)[:, None]
    ang = (2.0 * math.pi * np.arange(n, dtype=np.float32)[:, None] / np.float32(n)).astype(np.float32)
    f = np.linspace(1e-4, HY_BANDS - 1, HY_BANDS, dtype=np.float32)[None, :]
    zfeat = jnp.concatenate([jnp.asarray(t), jnp.cos(jnp.asarray(f * ang)), -jnp.sin(jnp.asarray(f * ang))], axis=-1)
    emb_pad = op['f_w1'].shape[0]
    zfeat = jnp.pad(zfeat, ((0, 0), (0, emb_pad - HY_EMB)))
    deltas = np.abs(np.linspace(math.log(HY_TARGET) / HY_SLOW_PCT, math.log(HY_TARGET) / HY_FAST_PCT, d,
                                dtype=np.float32))[None, :]
    tn = min(n, 256)
    nq = 2 * HY_ORDER
    hf = op['f_w2'].shape[0]
    full = lambda *shape: pl.BlockSpec(shape, lambda i: (0,) * len(shape))
    return pl.pallas_call(
        functools.partial(_filter_kernel, d=d),
        grid=(n // tn,),
        in_specs=[pl.BlockSpec((tn, emb_pad), lambda i: (i, 0)), pl.BlockSpec((tn, 1), lambda i: (i, 0)),
                  full(emb_pad, hf), full(1, hf), full(hf, hf), full(1, hf), full(hf, hf), full(1, hf), full(1, hf),
                  full(hf, nq * d), full(1, d)],
        out_specs=pl.BlockSpec((nq, tn, d), lambda i: (0, i, 0)),
        out_shape=jax.ShapeDtypeStruct((nq, n, d), F32),
        compiler_params=_cparams("parallel"),
        name="hyena_filter",
    )(zfeat, jnp.asarray(t), op['f_w1'], op['f_b1'], op['f_w2'], op['f_b2'], op['f_w3'], op['f_b3'], op['f_freq'],
      op['f_out'], jnp.asarray(deltas))


def _dft(n, rows, cols, sign=-1.0):
    k = np.arange(rows, dtype=np.float64)[:, None]
    m = np.arange(cols, dtype=np.float64)[None, :]
    ang = sign * 2.0 * np.pi * ((k * m) % n) / n
    return np.cos(ang), np.sin(ang)


def _stack(re, im):
    return jnp.asarray(np.concatenate([re, im], axis=0).astype(np.float32)).astype(BF16)


def _fft_tables(n_seq):
    n = 2 * n_seq
    n1 = n // FFT_N2
    f1r, f1i = _dft(n1, n1, n1 // 2)
    h1r, h1i = _dft(n1, n1 // 2, n1, sign=1.0)
    k1 = np.arange(n1, dtype=np.float64)[:, None, None]
    k2 = np.arange(FFT_N2, dtype=np.float64)[None, :, None]
    j2 = np.arange(FFT_N2, dtype=np.float64)[None, None, :]
    ang = -2.0 * np.pi * (((k2 * j2 * n1) + k1 * j2) % n) / n
    gr, gi = np.cos(ang), np.sin(ang)
    g_fwd = np.concatenate([gr, gi], axis=1)
    g_inv = np.concatenate([np.swapaxes(gr, 1, 2), np.swapaxes(gi, 1, 2)], axis=1)
    eye = np.eye(SUBLANES)
    kron = lambda m: np.kron(m, eye)
    return {'f1': _stack(kron(f1r), kron(f1i)), 'h1': _stack(kron(h1r), kron(h1i)),
            'g_fwd': jnp.asarray(g_fwd.astype(np.float32)).astype(BF16),
            'g_inv': jnp.asarray(g_inv.astype(np.float32)).astype(BF16), 'n1': n1}


def _dense_tables(n_seq):
    n = 2 * n_seq
    fr, fi = _dft(n, n, n_seq)
    hr, hi = _dft(n, n_seq, n, sign=1.0)
    return {'f': _stack(fr, fi), 'h': _stack(hr, hi)}


def _fft_a_kernel(f_ref, zr_ref, zi_ref, ar_ref, ai_ref, *, n1, cplx):
    f = f_ref[...]
    half, tj, d = zr_ref.shape[1:]
    m = n1 * SUBLANES
    res_r, res_i = [], []
    for s0 in range(0, tj, SUBLANES):
        js = slice(s0, s0 + SUBLANES)
        zr = zr_ref[0, :, js, :].reshape(half * SUBLANES, d)
        p = jnp.dot(f, zr.astype(BF16), preferred_element_type=F32)
        if cplx:
            zi = zi_ref[0, :, js, :].reshape(half * SUBLANES, d)
            q = jnp.dot(f, zi.astype(BF16), preferred_element_type=F32)
            res_r.append((p[0:m] - q[m:]).reshape(n1, SUBLANES, d))
            res_i.append((p[m:] + q[0:m]).reshape(n1, SUBLANES, d))
        else:
            res_r.append(p[0:m].reshape(n1, SUBLANES, d))
            res_i.append(p[m:].reshape(n1, SUBLANES, d))
    ar_ref[0] = jnp.concatenate(res_r, axis=1).astype(BF16)
    ai_ref[0] = jnp.concatenate(res_i, axis=1).astype(BF16)


def _fft_a(u5, q, tabs, cplx):
    _, s, half, n2, d = u5.shape
    n1 = tabs['n1']
    assert half == n1 // 2 and n2 == FFT_N2
    sp = s // 2 if cplx else s
    tj = 2 * SUBLANES
    zi_map = (lambda si, j: (q, si + sp, 0, j, 0)) if cplx else (lambda si, j: (q, si, 0, j, 0))
    out = pl.BlockSpec((1, n1, tj, d), lambda si, j: (si, 0, j, 0))
    return pl.pallas_call(
        functools.partial(_fft_a_kernel, n1=n1, cplx=cplx),
        grid=(sp, n2 // tj),
        in_specs=[pl.BlockSpec((2 * n1 * SUBLANES, half * SUBLANES), lambda si, j: (0, 0)),
                  pl.BlockSpec((None, 1, half, tj, d), lambda si, j: (q, si, 0, j, 0)),
                  pl.BlockSpec((None, 1, half, tj, d), zi_map)],
        out_specs=[out, out],
        out_shape=[jax.ShapeDtypeStruct((sp, n1, n2, d), BF16)] * 2,
        compiler_params=_cparams("parallel", "parallel"),
        name="fft_stage1",
    )(tabs['f1'], u5, u5)


def _cplx_mm(s, xr, xi, conj):
    p = jnp.dot(s, xr.astype(BF16), preferred_element_type=F32)
    q = jnp.dot(s, xi.astype(BF16), preferred_element_type=F32)
    m = s.shape[0] // 2
    if conj:
        return p[0:m] + q[m:], q[0:m] - p[m:]
    return p[0:m] - q[m:], p[m:] + q[0:m]


def _fft_b_kernel(gf_ref, gi_ref, ar_ref, ai_ref, kr_ref, ki_ref, dr_ref, di_ref):
    kr = kr_ref[0]
    ki = ki_ref[0]
    seqs = range(ar_ref.shape[0])
    c = [_cplx_mm(gf_ref[0], ar_ref[s, 0], ai_ref[s, 0], False) for s in seqs]
    e = [(cr * kr - ci * ki, cr * ki + ci * kr) for cr, ci in c]
    dd = [_cplx_mm(gi_ref[0], er, ei, True) for er, ei in e]
    for s, (dr, di) in zip(seqs, dd):
        dr_ref[s, 0] = dr.astype(BF16)
        di_ref[s, 0] = di.astype(BF16)


def _fft_b(ar4, ai4, kr, ki, order, tabs):
    sp, n1, _, d = ar4.shape
    ns = 2 if sp % 2 == 0 else 1
    blk = pl.BlockSpec((ns, 1, FFT_N2, d), lambda k1, si: (si, k1, 0, 0))
    tab = pl.BlockSpec((1, 2 * FFT_N2, FFT_N2), lambda k1, si: (k1, 0, 0))
    kb = pl.BlockSpec((None, 1, FFT_N2, d), lambda k1, si: (order, k1, 0, 0))
    return pl.pallas_call(
        _fft_b_kernel,
        grid=(n1, sp // ns),
        in_specs=[tab, tab, blk, blk, kb, kb],
        out_specs=[blk, blk],
        out_shape=[jax.ShapeDtypeStruct((sp, n1, FFT_N2, d), BF16)] * 2,
        compiler_params=_cparams("parallel", "arbitrary"),
        name="fft_stage2_filter",
    )(tabs['g_fwd'], tabs['g_inv'], ar4, ai4, kr, ki)


def _fft_c_kernel(h_ref, dr_ref, di_ref, u_ref, x_ref, sk_ref, o_ref):
    h = h_ref[...]
    n1, tj, d = dr_ref.shape[1:]
    half = n1 // 2
    sk = sk_ref[...]
    dr_all = dr_ref[0].astype(F32)
    di_all = di_ref[0].astype(F32)
    for s0 in range(0, tj, SUBLANES):
        js = slice(s0, s0 + SUBLANES)
        yr, yi = _cplx_mm(h, dr_all[:, js, :].reshape(n1 * SUBLANES, d),
                          di_all[:, js, :].reshape(n1 * SUBLANES, d), False)
        for part, y in enumerate((yr, yi)):
            u = u_ref[part, 0, :, js, :]
            o_ref[part, 0, :, js, :] = x_ref[part, 0, :, js, :] * (y.reshape(half, SUBLANES, d) + u * sk)


def _fft_c(dr, di, u6, uq, x6, xq, skip, tabs):
    sp, n1, n2, d = dr.shape
    half = n1 // 2
    tj = 2 * SUBLANES
    pair = lambda q: pl.BlockSpec((None, 2, 1, half, tj, d), lambda si, j: (q, 0, si, 0, j, 0))
    dblk = pl.BlockSpec((1, n1, tj, d), lambda si, j: (si, 0, j, 0))
    out = pl.pallas_call(
        _fft_c_kernel,
        grid=(sp, n2 // tj),
        in_specs=[pl.BlockSpec((2 * half * SUBLANES, n1 * SUBLANES), lambda si, j: (0, 0)), dblk, dblk,
                  pair(uq), pair(xq), pl.BlockSpec((1, d), lambda si, j: (0, 0))],
        out_specs=pl.BlockSpec((2, 1, half, tj, d), lambda si, j: (0, si, 0, j, 0)),
        out_shape=jax.ShapeDtypeStruct((2, sp, half, n2, d), F32),
        compiler_params=_cparams("parallel", "parallel"),
        name="fft_inverse_stage1_gate",
    )(tabs['h1'], dr, di, u6, x6, skip.reshape(1, d))
    return out


def _spec_b_kernel(gf_ref, ar_ref, ai_ref, kr_ref, ki_ref, *, scale):
    fr, fi = _cplx_mm(gf_ref[0], ar_ref[0, 0], ai_ref[0, 0], False)
    gr, gi = _cplx_mm(gf_ref[0], ar_ref[1, 0], ai_ref[1, 0], False)
    kr_ref[0, 0] = (fr + gr) * scale
    ki_ref[0, 0] = (fi - gi) * scale


def _filter_spectrum_fft(filt, tabs, d):
    nq, n, _ = filt.shape
    n1 = tabs['n1']
    ar, ai = _fft_a(filt.reshape(1, nq, n1 // 2, FFT_N2, d), 0, tabs, cplx=False)
    orders = nq // 2
    ar5 = ar.reshape(orders, 2, n1, FFT_N2, d)
    ai5 = ai.reshape(orders, 2, n1, FFT_N2, d)
    blk = pl.BlockSpec((None, 2, 1, FFT_N2, d), lambda o, k1: (o, 0, k1, 0, 0))
    out = pl.BlockSpec((1, 1, FFT_N2, d), lambda o, k1: (o, k1, 0, 0))
    return pl.pallas_call(
        functools.partial(_spec_b_kernel, scale=1.0 / (2 * n)),
        grid=(orders, n1),
        in_specs=[pl.BlockSpec((1, 2 * FFT_N2, FFT_N2), lambda o, k1: (k1, 0, 0)), blk, blk],
        out_specs=[out, out],
        out_shape=[jax.ShapeDtypeStruct((orders, n1, FFT_N2, d), F32)] * 2,
        compiler_params=_cparams("parallel", "parallel"),
        name="filter_spectrum",
    )(tabs['g_fwd'], ar5, ai5)


def _long_conv_fft(u, uq, x, xq, skip, kr, ki, order, tabs):
    _, b, n, d = u.shape
    half = tabs['n1'] // 2
    ar, ai = _fft_a(u.reshape(u.shape[0], b, half, FFT_N2, d), uq, tabs, cplx=True)
    dr, di = _fft_b(ar, ai, kr, ki, order, tabs)
    six = lambda a: a.reshape(a.shape[0], 2, b // 2, half, FFT_N2, d)
    return _fft_c(dr, di, six(u), uq, six(x), xq, skip, tabs).reshape(b, n, d)


def _dense_spec_kernel(f_ref, filt_ref, kr_ref, ki_ref, *, scale):
    f = f_ref[...]
    n = f.shape[0] // 2
    pf = jnp.dot(f, filt_ref[0, 0].astype(BF16), preferred_element_type=F32)
    pg = jnp.dot(f, filt_ref[0, 1].astype(BF16), preferred_element_type=F32)
    kr_ref[0] = (pf[0:n] + pg[0:n]) * scale
    ki_ref[0] = (pf[n:] - pg[n:]) * scale


def _filter_spectrum_dense(filt, tabs, d):
    nq, n, _ = filt.shape
    orders = nq // 2
    f4 = filt.reshape(orders, 2, n, d)
    out = pl.BlockSpec((1, 2 * n, d), lambda o: (o, 0, 0))
    return pl.pallas_call(
        functools.partial(_dense_spec_kernel, scale=1.0 / (2 * n)),
        grid=(orders,),
        in_specs=[pl.BlockSpec((4 * n, n), lambda o: (0, 0)), pl.BlockSpec((1, 2, n, d), lambda o: (o, 0, 0, 0))],
        out_specs=[out, out],
        out_shape=[jax.ShapeDtypeStruct((orders, 2 * n, d), F32)] * 2,
        compiler_params=_cparams("parallel"),
        name="filter_spectrum_dense",
    )(tabs['f'], f4)


def _dense_conv_kernel(f_ref, h_ref, u_ref, x_ref, kr_ref, ki_ref, sk_ref, o_ref):
    cr, ci = _cplx_mm(f_ref[...], u_ref[0, 0], u_ref[1, 0], False)
    kr = kr_ref[...]
    ki = ki_ref[...]
    yr, yi = _cplx_mm(h_ref[...], cr * kr - ci * ki, cr * ki + ci * kr, False)
    sk = sk_ref[...]
    o_ref[0, 0] = x_ref[0, 0] * (yr + u_ref[0, 0] * sk)
    o_ref[1, 0] = x_ref[1, 0] * (yi + u_ref[1, 0] * sk)


def _long_conv_dense(u, x, skip, kr, ki, tabs):
    b, n, d = u.shape
    sp = b // 2
    pair = pl.BlockSpec((2, 1, n, d), lambda si: (0, si, 0, 0))
    kb = pl.BlockSpec((2 * n, d), lambda si: (0, 0))
    out = pl.pallas_call(
        _dense_conv_kernel,
        grid=(sp,),
        in_specs=[pl.BlockSpec((4 * n, n), lambda si: (0, 0)), pl.BlockSpec((2 * n, 2 * n), lambda si: (0, 0)),
                  pair, pair, kb, kb, pl.BlockSpec((1, d), lambda si: (0, 0))],
        out_specs=pair,
        out_shape=jax.ShapeDtypeStruct((2, sp, n, d), F32),
        compiler_params=_cparams("parallel"),
        name="long_conv_dense",
    )(tabs['f'], tabs['h'], u.reshape(2, sp, n, d), x.reshape(2, sp, n, d), kr, ki, skip.reshape(1, d))
    return out.reshape(b, n, d)


def _hyena_mixer(x, g, mod, op, fft_tabs, dense_tabs):
    b, n, d = x.shape
    z = _proj_conv(x, g, mod, op['w_in'], op['b_in'], op['conv_w'], op['conv_b'])
    filt = _hyena_filters(n, op, d)
    if n <= DENSE_FFT_MAX:
        kr, ki = _filter_spectrum_dense(filt, dense_tabs, d)
        y = _long_conv_dense(z[0], z[1], op['skip'][0], kr[0], ki[0], dense_tabs)
        y = _long_conv_dense(y, z[2], op['skip'][1], kr[1], ki[1], dense_tabs)
    else:
        kr, ki = _filter_spectrum_fft(filt, fft_tabs, d)
        y = _long_conv_fft(z, 0, z, 1, op['skip'][0], kr, ki, 0, fft_tabs)
        y = _long_conv_fft(y[None], 0, z, 2, op['skip'][1], kr, ki, 1, fft_tabs)
    return _out_res(y, x, mod, op['w_out'], op['b_out'])


def _even_mixer(x, ctx, g, mod_l, mod_c, ep, tabs, need_ctx):
    rc, vc, kkc, gc, lwc, ac, kdc, qc, kac, vac = _even_prep(ctx, g, mod_c, ep, tabs, rope=False)
    rl, vl, kkl, gl, lwl, al, kdl, ql, kal, val = _even_prep(x, g, mod_l, ep, tabs, rope=True)
    b = x.shape[0]
    c = rl.shape[-1]
    h0 = jnp.zeros((2, b, HEAD_DIM, c), F32)
    y_ctx, s_ctx = _rwkv(rc, vc, kkc, lwc, ac, kdc, h0)
    y_lat, _ = _rwkv(rl, vl, kkl, lwl, al, kdl, s_ctx)
    b_lat = _attention(ql, kal, val, kac, vac, ep['sink'], local=True)
    x_new = _even_out(y_lat, rl, vl, gl, kdl, b_lat, x, mod_l, ep, tabs)
    if not need_ctx:
        return x_new, None
    b_ctx = _attention(qc, kac, vac, kac, vac, ep['sink'], local=False)
    ctx_new = _even_out(y_ctx, rc, vc, gc, kdc, b_ctx, ctx, mod_c, ep, tabs)
    return x_new, ctx_new


def _rope_tables(n_tokens):
    rows = n_tokens // GRID_W
    row = jnp.repeat(jnp.arange(rows), GRID_W).astype(F32)
    col = jnp.tile(jnp.arange(GRID_W), rows).astype(F32)
    n_freq = HEAD_DIM // 4
    inv = ROPE_THETA ** (-jnp.arange(n_freq, dtype=F32) / n_freq)
    ang = jnp.concatenate([row[:, None] * inv, col[:, None] * inv], axis=-1)
    cos, sin = jnp.cos(ang), jnp.sin(ang)
    reps = LANES // HEAD_DIM
    cos_t = jnp.tile(jnp.concatenate([cos, cos], axis=-1), (1, reps))
    sin_t = jnp.tile(jnp.concatenate([-sin, sin], axis=-1), (1, reps))
    return cos_t, sin_t


def _block_ones(width):
    idx = np.arange(width) // HEAD_DIM
    return jnp.asarray((idx[:, None] == idx[None, :]).astype(np.float32)).astype(BF16)


def _lora_pad(w):
    z = jnp.zeros_like(w[0])
    return jnp.stack([jnp.concatenate([w[0], z], axis=0), jnp.concatenate([z, w[1]], axis=0)], axis=0)


def kernel(x, c, ctx, c_ctx, ada_w, ada_b, norm1_g, norm2_g, ffn_up, ffn_conv_w, ffn_conv_b, ffn_down, ev_w_in, ev_mu_prev, ev_mu_next, ev_w0, ev_w2, ev_a0, ev_a2, ev_g2, ev_k_k, ev_k_a, ev_r_k, ev_ln_w, ev_ln_b, ev_q_norm, ev_k_norm, ev_sink, ev_w_out, od_w_in, od_b_in, od_conv_w, od_conv_b, od_f_w1, od_f_b1, od_f_w2, od_f_b2, od_f_w3, od_f_b3, od_f_freq, od_f_out, od_skip, od_w_out, od_b_out):
    bsz, seq, d = x.shape
    lc = ctx.shape[1]
    depth = ada_w.shape[0]
    a_width = ev_k_k.shape[-1]
    nq = ev_sink.shape[-1] * HEAD_DIM
    nk = B_KV_HEADS * HEAD_DIM

    cond = jnp.zeros((BF16_ROWS, d), F32).at[:bsz].set(c).at[bsz].set(c_ctx)
    mod = _ada_mod(cond, ada_w, ada_b)

    cos_t, sin_t = _rope_tables(seq)
    tabs = {'cos': cos_t, 'sin': sin_t, 'e': _block_ones(nq)}
    fft_tabs = _fft_tables(seq) if seq > DENSE_FFT_MAX else None
    dense_lat = _dense_tables(seq) if seq <= DENSE_FFT_MAX else None
    fft_ctx = _fft_tables(lc) if lc > DENSE_FFT_MAX else None
    dense_ctx = _dense_tables(lc) if lc <= DENSE_FFT_MAX else None

    for layer in range(depth):
        need_ctx = layer < depth - 1
        even = layer % 2 == 0
        j = layer // 2
        mod_l = mod[layer, :bsz].reshape(bsz, 1, 6 * d)
        mod_c = jnp.broadcast_to(mod[layer, bsz].reshape(1, 1, 6 * d), (bsz, 1, 6 * d))
        if even:
            ep = {'w_in': ev_w_in[j].astype(BF16), 'mu_prev': ev_mu_prev[j][None], 'mu_next': ev_mu_next[j][None],
                  'w0': ev_w0[j], 'w2pad': _lora_pad(ev_w2[j]).astype(BF16), 'a0': ev_a0[j],
                  'a2pad': _lora_pad(ev_a2[j]).astype(BF16), 'g2': ev_g2[j].astype(BF16),
                  'k_k': ev_k_k[j][None], 'k_a': ev_k_a[j][None], 'r_k': ev_r_k[j].reshape(1, a_width),
                  'ln_w': ev_ln_w[j][None], 'ln_b': ev_ln_b[j][None],
                  'q_norm_t': jnp.tile(ev_q_norm[j], nq // HEAD_DIM)[None],
                  'k_norm_t': jnp.tile(ev_k_norm[j], nk // HEAD_DIM)[None],
                  'sink': ev_sink[j], 'w_out': ev_w_out[j].astype(BF16), 'nq': nq, 'nk': nk}
            x, ctx_new = _even_mixer(x, ctx, norm1_g[layer], mod_l, mod_c, ep, tabs, need_ctx)
        else:
            emb_pad = HEAD_DIM
            op = {'w_in': od_w_in[j].astype(BF16), 'b_in': od_b_in[j], 'conv_w': od_conv_w[j],
                  'conv_b': od_conv_b[j],
                  'f_w1': jnp.pad(od_f_w1[j], ((0, emb_pad - HY_EMB), (0, 0))), 'f_b1': od_f_b1[j][None],
                  'f_w2': od_f_w2[j], 'f_b2': od_f_b2[j][None], 'f_w3': od_f_w3[j], 'f_b3': od_f_b3[j][None],
                  'f_freq': od_f_freq[j][None], 'f_out': od_f_out[j], 'skip': od_skip[j],
                  'w_out': od_w_out[j].astype(BF16), 'b_out': od_b_out[j]}
            ctx_new = _hyena_mixer(ctx, norm1_g[layer], mod_c, op, fft_ctx, dense_ctx) if need_ctx else None
            x = _hyena_mixer(x, norm1_g[layer], mod_l, op, fft_tabs, dense_lat)
        w_up = ffn_up[layer].astype(BF16)
        w_down = ffn_down[layer].astype(BF16)
        x = _conv_ffn(x, norm2_g[layer], mod_l, w_up, ffn_conv_w[layer], ffn_conv_b[layer], w_down)
        if need_ctx:
            ctx = _conv_ffn(ctx_new, norm2_g[layer], mod_c, w_up, ffn_conv_w[layer], ffn_conv_b[layer], w_down)
    return x
```

```python
import functools
import math

import numpy as np
import jax
import jax.numpy as jnp
from jax import lax
from jax.experimental import pallas as pl
from jax.experimental.pallas import tpu as pltpu

F32 = jnp.float32
BF16 = jnp.bfloat16

HEAD_DIM = 64
GRID_W = 64
DECAY_LORA = 64
ICLR_LORA = 64
GATE_LORA = 128
RWKV_GN_EPS = 64e-5
B_KV_HEADS = 2
WINDOW = 128
BLOCK = 128
ROPE_THETA = 10000.0
MASK_VALUE = -1e30
HY_ORDER = 2
HY_EMB = 33
HY_BANDS = (HY_EMB - 1) // 2
HY_TARGET = 1e-2
HY_FAST_PCT = 0.3
HY_SLOW_PCT = 1.5
HY_MOD_SHIFT = 0.05
NORM_EPS = 1e-6

V7X_VMEM_BYTES = 64 * 1024 * 1024
VMEM_LIMIT_BYTES = V7X_VMEM_BYTES * 3 // 4
LANES = 128
V7X_MXU_DIM = 256
SUBLANES = 8
BF16_ROWS = 16
FFN_ROW_PARTS = 2
RWKV_CHUNK = 64
RWKV_PREP_CHUNKS = 8
RWKV_INV_BASE = 8
FFT_N2 = 128
DENSE_FFT_MAX = 512

HIGHEST = lax.Precision.HIGHEST


def _cparams(*sem):
    return pltpu.CompilerParams(dimension_semantics=sem, vmem_limit_bytes=VMEM_LIMIT_BYTES)


def _dot(a, b):
    return jnp.dot(a.astype(BF16), b.astype(BF16), preferred_element_type=F32)


def _dot_nt(a, b):
    return lax.dot_general(a.astype(BF16), b.astype(BF16), (((1,), (1,)), ((), ())), preferred_element_type=F32)


def _dot_hi(a, b):
    return jnp.dot(a, b, preferred_element_type=F32, precision=HIGHEST)


def _segsum(x, e_ref):
    hi = x.astype(BF16)
    lo = (x - hi.astype(F32)).astype(BF16)
    w = x.shape[-1]
    tile = min(w, V7X_MXU_DIM)
    e = e_ref[0:tile, 0:tile]
    both = jnp.concatenate([hi, lo], axis=0)
    parts = [jnp.dot(both[:, c0:c0 + tile], e, preferred_element_type=F32) for c0 in range(0, w, tile)]
    full = parts[0] if len(parts) == 1 else jnp.concatenate(parts, axis=1)
    m = x.shape[0]
    return full[0:m] + full[m:]


def _norm_mod(x, g, shift, scale):
    xn = x * lax.rsqrt(jnp.mean(x * x, axis=-1, keepdims=True) + NORM_EPS)
    return (xn * g) * (1.0 + scale) + shift


def _row_tile(t, want):
    tm = min(t, want)
    assert t % tm == 0 and tm % BF16_ROWS == 0
    return tm


def _ada_kernel(c_ref, w_ref, b_ref, o_ref):
    c = c_ref[...]
    s = c * jax.nn.sigmoid(c)
    o_ref[0] = _dot(s, w_ref[0]) + b_ref[0]


def _ada_mod(cond, ada_w, ada_b):
    depth, d, n = ada_w.shape
    tn = n // 4
    rows = cond.shape[0]
    return pl.pallas_call(
        _ada_kernel,
        grid=(depth, n // tn),
        in_specs=[pl.BlockSpec((rows, d), lambda l, j: (0, 0)),
                  pl.BlockSpec((1, d, tn), lambda l, j: (l, 0, j)),
                  pl.BlockSpec((1, 1, tn), lambda l, j: (l, 0, j))],
        out_specs=pl.BlockSpec((1, rows, tn), lambda l, j: (l, 0, j)),
        out_shape=jax.ShapeDtypeStruct((depth, rows, n), F32),
        compiler_params=_cparams("parallel", "parallel"),
        name="ada_mod",
    )(cond, ada_w, ada_b.reshape(depth, 1, n))


def _halo_specs(tm, t, d, nargs):
    r = tm // BF16_ROWS
    last = t // BF16_ROWS - 1
    if nargs == 3:
        prev = pl.BlockSpec((1, BF16_ROWS, d), lambda bi, i, j: (bi, jnp.maximum(i * r - 1, 0), 0))
        nxt = pl.BlockSpec((1, BF16_ROWS, d), lambda bi, i, j: (bi, jnp.minimum((i + 1) * r, last), 0))
    else:
        prev = pl.BlockSpec((1, BF16_ROWS, d), lambda bi, i: (bi, jnp.maximum(i * r - 1, 0), 0))
        nxt = pl.BlockSpec((1, BF16_ROWS, d), lambda bi, i: (bi, jnp.minimum((i + 1) * r, last), 0))
    return prev, nxt


def _fill_h(x_ref, xp_ref, xn_ref, g, shift, scale, h_scr, tm):
    h_scr[0:BF16_ROWS] = _norm_mod(xp_ref[0], g, shift, scale).astype(BF16)
    h_scr[BF16_ROWS:BF16_ROWS + tm] = _norm_mod(x_ref[0], g, shift, scale).astype(BF16)
    h_scr[BF16_ROWS + tm:2 * BF16_ROWS + tm] = _norm_mod(xn_ref[0], g, shift, scale).astype(BF16)


def _conv3_rows(u, u_scr, cw, cb, tm, t_total, part=0, nparts=1):
    i = pl.program_id(1)
    o = BF16_ROWS
    rows = tm // nparts
    base = part * (rows + 2 * o)
    head_ok = (i > 0) if part == 0 else True
    tail_ok = (i < t_total // tm - 1) if part == nparts - 1 else True
    u_scr[base:base + o] = jnp.where(head_ok, u[0:o], 0.0)
    u_scr[base + o:base + o + rows] = u[o:o + rows]
    u_scr[base + o + rows:base + 2 * o + rows] = jnp.where(tail_ok, u[o + rows:], 0.0)
    s = base + o
    return (u_scr[s - 1:s - 1 + rows] * cw[0:1] + u_scr[s:s + rows] * cw[1:2] + u_scr[s + 1:s + 1 + rows] * cw[2:3]
            + cb)


def _proj_conv_kernel(x_ref, xp_ref, xn_ref, g_ref, mod_ref, w_ref, b_ref, cw_ref, cb_ref, o_ref, h_scr, u_scr,
                      *, d, tm, t_total, tn):
    m = mod_ref[0]
    _fill_h(x_ref, xp_ref, xn_ref, g_ref[...], m[:, 0:d], m[:, d:2 * d], h_scr, tm)
    nparts = FFN_ROW_PARTS
    rows = tm // nparts
    n = w_ref.shape[1]

    def project(j):
        cols = slice(j * tn, (j + 1) * tn)
        return [jnp.dot(h_scr[p * rows:(p + 1) * rows + 2 * BF16_ROWS], w_ref[:, cols],
                        preferred_element_type=F32) + b_ref[:, cols] for p in range(nparts)]

    nxt_u = project(0)
    for j in range(n // tn):
        cols = slice(j * tn, (j + 1) * tn)
        us = nxt_u
        if (j + 1) * tn < n:
            nxt_u = project(j + 1)
        q, lane0 = divmod(j * tn, d)
        for p in range(nparts):
            o_ref[q, 0, p * rows:(p + 1) * rows, lane0:lane0 + tn] = _conv3_rows(
                us[p], u_scr.at[j % 2], cw_ref[:, cols], cb_ref[:, cols], tm, t_total, p, nparts).astype(o_ref.dtype)


def _proj_conv(x, g, mod, w, bias, cw, cb):
    b, t, d = x.shape
    n = w.shape[1]
    tm = _row_tile(t, 512)
    tn = d // 2
    prev, nxt = _halo_specs(tm, t, d, 2)
    resident = lambda *shape: pl.BlockSpec(shape, lambda bi, i: (0,) * len(shape), pipeline_mode=pl.Buffered(1))
    return pl.pallas_call(
        functools.partial(_proj_conv_kernel, d=d, tm=tm, t_total=t, tn=tn),
        grid=(b, t // tm),
        in_specs=[pl.BlockSpec((1, tm, d), lambda bi, i: (bi, i, 0)), prev, nxt,
                  pl.BlockSpec((1, d), lambda bi, i: (0, 0)),
                  pl.BlockSpec((1, 1, mod.shape[-1]), lambda bi, i: (bi, 0, 0)),
                  resident(d, n), resident(1, n), resident(3, n), resident(1, n)],
        out_specs=pl.BlockSpec((n // d, 1, tm, d), lambda bi, i: (0, bi, i, 0)),
        out_shape=jax.ShapeDtypeStruct((n // d, b, t, d), BF16),
        scratch_shapes=[pltpu.VMEM((tm + 2 * BF16_ROWS, d), BF16),
                        pltpu.VMEM((2, tm + 2 * BF16_ROWS * FFN_ROW_PARTS, tn), F32)],
        compiler_params=_cparams("parallel", "parallel"),
        name="hyena_in_proj",
    )(x, x, x, g.reshape(1, d), mod, w, bias.reshape(1, n), cw, cb.reshape(1, n))


def _ffn_kernel(x_ref, xp_ref, xn_ref, g_ref, mod_ref, wup_ref, cw_ref, cb_ref, wd_ref,
                o_ref, h_scr, ug_scr, uv_scr, acc_scr, *, d, tm, t_total, f, tf):
    m = mod_ref[0]
    _fill_h(x_ref, xp_ref, xn_ref, g_ref[...], m[:, 3 * d:4 * d], m[:, 4 * d:5 * d], h_scr, tm)
    nparts = FFN_ROW_PARTS
    rows = tm // nparts
    nf = f // tf

    def up_proj(j):
        hs = [h_scr[p * rows:(p + 1) * rows + 2 * BF16_ROWS] for p in range(nparts)]
        return ([jnp.dot(h, wup_ref[:, j * tf:(j + 1) * tf], preferred_element_type=F32) for h in hs],
                [jnp.dot(h, wup_ref[:, f + j * tf:f + (j + 1) * tf], preferred_element_type=F32) for h in hs])

    nxt_up = up_proj(0)
    for j in range(nf):
        gs = slice(j * tf, (j + 1) * tf)
        vs = slice(f + j * tf, f + (j + 1) * tf)
        ug, uv = nxt_up
        if j + 1 < nf:
            nxt_up = up_proj(j + 1)
        for p in range(nparts):
            gate = _conv3_rows(ug[p], ug_scr.at[j % 2], cw_ref[:, gs], cb_ref[:, gs], tm, t_total, p, nparts)
            val = _conv3_rows(uv[p], uv_scr.at[j % 2], cw_ref[:, vs], cb_ref[:, vs], tm, t_total, p, nparts)
            act = (gate * jax.nn.sigmoid(gate)) * val
            down = jnp.dot(act.astype(BF16), wd_ref[gs, :], preferred_element_type=F32)
            part = slice(p * rows, (p + 1) * rows)
            if j == 0:
                acc_scr[part] = down
            else:
                acc_scr[part] += down
    o_ref[0] = x_ref[0] + m[:, 5 * d:6 * d] * acc_scr[...]


def _conv_ffn(x, g, mod, w_up, cw, cb, w_down):
    b, t, d = x.shape
    f = w_down.shape[0]
    tm = _row_tile(t, 1024)
    tf = 256 if f % 256 == 0 else 128
    prev, nxt = _halo_specs(tm, t, d, 2)
    resident = lambda *shape: pl.BlockSpec(shape, lambda bi, i: (0,) * len(shape), pipeline_mode=pl.Buffered(1))
    u_rows = tm + 2 * BF16_ROWS * FFN_ROW_PARTS
    return pl.pallas_call(
        functools.partial(_ffn_kernel, d=d, tm=tm, t_total=t, f=f, tf=tf),
        grid=(b, t // tm),
        in_specs=[pl.BlockSpec((1, tm, d), lambda bi, i: (bi, i, 0)), prev, nxt,
                  pl.BlockSpec((1, d), lambda bi, i: (0, 0)),
                  pl.BlockSpec((1, 1, mod.shape[-1]), lambda bi, i: (bi, 0, 0)),
                  resident(d, 2 * f), resident(3, 2 * f), resident(1, 2 * f), resident(f, d)],
        out_specs=pl.BlockSpec((1, tm, d), lambda bi, i: (bi, i, 0)),
        out_shape=jax.ShapeDtypeStruct((b, t, d), F32),
        scratch_shapes=[pltpu.VMEM((tm + 2 * BF16_ROWS, d), BF16),
                        pltpu.VMEM((2, u_rows, tf), F32),
                        pltpu.VMEM((2, u_rows, tf), F32),
                        pltpu.VMEM((tm, d), F32)],
        compiler_params=_cparams("parallel", "parallel"),
        name="conv_ffn",
    )(x, x, x, g.reshape(1, d), mod, w_up, cw, cb.reshape(1, 2 * f), w_down)


def _out_res_kernel(y_ref, x_ref, mod_ref, w_ref, b_ref, o_ref, *, d):
    o = _dot(y_ref[0], w_ref[...]) + b_ref[...]
    o_ref[0] = x_ref[0] + mod_ref[0][:, 2 * d:3 * d] * o


def _out_res(y, x, mod, w, bias):
    b, t, d = x.shape
    tm = _row_tile(t, 512)
    return pl.pallas_call(
        functools.partial(_out_res_kernel, d=d),
        grid=(b, t // tm),
        in_specs=[pl.BlockSpec((1, tm, d), lambda bi, i: (bi, i, 0)),
                  pl.BlockSpec((1, tm, d), lambda bi, i: (bi, i, 0)),
                  pl.BlockSpec((1, 1, mod.shape[-1]), lambda bi, i: (bi, 0, 0)),
                  pl.BlockSpec((d, d), lambda bi, i: (0, 0)),
                  pl.BlockSpec((1, d), lambda bi, i: (0, 0))],
        out_specs=pl.BlockSpec((1, tm, d), lambda bi, i: (bi, i, 0)),
        out_shape=jax.ShapeDtypeStruct((b, t, d), F32),
        compiler_params=_cparams("parallel", "parallel"),
        name="out_proj_residual",
    )(y, x, mod, w, bias.reshape(1, d))


def _even_prep_kernel(x_ref, xp_ref, xn_ref, gn_ref, mod_ref, win_ref, mup_ref, mun_ref, w0_ref, w2_ref, a0_ref,
                      a2_ref, g2_ref, kk_ref, ka_ref, qn_ref, kn_ref, cos_ref, sin_ref, e_ref,
                      r_ref, v_ref, kkn_ref, g_ref, lw_ref, a_ref, kd_ref, q_ref, katt_ref, vatt_ref,
                      h_scr, p_scr, *, tm, d, c, a_in, rope):
    i = pl.program_id(1)
    nt = pl.num_programs(1)
    m = mod_ref[0]
    _fill_h(x_ref, xp_ref, xn_ref, gn_ref[...], m[:, 0:d], m[:, d:2 * d], h_scr, tm)
    p_scr[...] = jnp.dot(h_scr[...], win_ref[...], preferred_element_type=F32)
    o = BF16_ROWS
    pa = p_scr[o:o + tm, 0:a_in]
    rows = lax.broadcasted_iota(jnp.int32, (tm, 1), 0)
    prev = jnp.where((rows == 0) & (i == 0), 0.0, p_scr[o - 1:o - 1 + tm, 0:a_in])
    nxt = jnp.where((rows == tm - 1) & (i == nt - 1), 0.0, p_scr[o + 1:o + 1 + tm, 0:a_in])
    za = pa + mup_ref[...] * (prev - pa) + mun_ref[...] * (nxt - pa)

    r = za[:, 0:c]
    k = za[:, c:2 * c]
    v = za[:, 2 * c:3 * c]
    wd = jnp.tanh(za[:, 3 * c:3 * c + 2 * DECAY_LORA])
    ad = za[:, 3 * c + 2 * DECAY_LORA:3 * c + 2 * DECAY_LORA + 2 * ICLR_LORA]
    gd = za[:, 3 * c + 2 * DECAY_LORA + 2 * ICLR_LORA:a_in]
    r_ref[0] = r
    v_ref[0] = v
    g_ref[0] = _dot(jax.nn.sigmoid(gd), g2_ref[...])
    kkv = k * kk_ref[...]
    ss = _segsum(kkv * kkv, e_ref)
    kkn_ref[0] = kkv / jnp.maximum(jnp.sqrt(ss), 1e-12)
    for dd in range(2):
        y = w0_ref[dd:dd + 1] + _dot(wd, w2_ref[dd])
        lw_ref[dd, 0] = -math.exp(-0.5) * jax.nn.sigmoid(y)
        a = jax.nn.sigmoid(a0_ref[dd:dd + 1] + _dot(ad, a2_ref[dd]))
        a_ref[dd, 0] = a
        kd_ref[dd, 0] = k * (1.0 + (a - 1.0) * ka_ref[...])

    pb = p_scr[o:o + tm, a_in:]
    nq = q_ref.shape[-1]
    nk = katt_ref.shape[-1]
    q = pb[:, 0:nq]
    kat = pb[:, nq:nq + nk]
    vatt_ref[0] = pb[:, nq + nk:]
    half = HEAD_DIM // 2

    def norm_rope(x, gain, e):
        w = x.shape[-1]
        ms = _segsum(x * x, e) * (1.0 / HEAD_DIM)
        xn = x * lax.rsqrt(ms + NORM_EPS) * gain
        if not rope:
            return xn
        lane = lax.broadcasted_iota(jnp.int32, (1, w), 1)
        first = (lane % HEAD_DIM) < half
        swapped = jnp.where(first, pltpu.roll(xn, w - half, 1), pltpu.roll(xn, half, 1))
        reps = w // LANES
        cos = jnp.concatenate([cos_ref[...]] * reps, axis=1) if reps > 1 else cos_ref[...]
        sin = jnp.concatenate([sin_ref[...]] * reps, axis=1) if reps > 1 else sin_ref[...]
        return xn * cos + swapped * sin

    q_ref[0] = norm_rope(q, qn_ref[...], e_ref[...])
    katt_ref[0] = norm_rope(kat, kn_ref[...], e_ref[0:nk, 0:nk])


def _even_prep(x, g, mod, ep, tabs, rope):
    b, t, d = x.shape
    n = ep['w_in'].shape[1]
    c = ep['k_k'].shape[-1]
    a_in = ep['mu_prev'].shape[-1]
    nq = ep['nq']
    nk = ep['nk']
    tm = _row_tile(t, 256)
    prev, nxt = _halo_specs(tm, t, d, 2)
    full = lambda *shape: pl.BlockSpec(shape, lambda bi, i: (0,) * len(shape))
    tok = lambda w: pl.BlockSpec((1, tm, w), lambda bi, i: (bi, i, 0))
    tok2 = lambda w: pl.BlockSpec((2, 1, tm, w), lambda bi, i: (0, bi, i, 0))
    sd = lambda *shape: jax.ShapeDtypeStruct(shape, F32)
    return pl.pallas_call(
        functools.partial(_even_prep_kernel, tm=tm, d=d, c=c, a_in=a_in, rope=rope),
        grid=(b, t // tm),
        in_specs=[tok(d), prev, nxt, full(1, d), pl.BlockSpec((1, 1, mod.shape[-1]), lambda bi, i: (bi, 0, 0)),
                  full(d, n), full(1, a_in), full(1, a_in), full(2, c), full(2, 2 * DECAY_LORA, c),
                  full(2, c), full(2, 2 * ICLR_LORA, c), full(GATE_LORA, c), full(1, c), full(1, c),
                  full(1, nq), full(1, nk),
                  pl.BlockSpec((tm, LANES), lambda bi, i: (i, 0)), pl.BlockSpec((tm, LANES), lambda bi, i: (i, 0)),
                  full(nq, nq)],
        out_specs=[tok(c), tok(c), tok(c), tok(c), tok2(c), tok2(c), tok2(c), tok(nq), tok(nk), tok(nk)],
        out_shape=[sd(b, t, c), sd(b, t, c), sd(b, t, c), sd(b, t, c), sd(2, b, t, c), sd(2, b, t, c),
                   sd(2, b, t, c), sd(b, t, nq), sd(b, t, nk), sd(b, t, nk)],
        scratch_shapes=[pltpu.VMEM((tm + 2 * BF16_ROWS, d), BF16), pltpu.VMEM((tm + 2 * BF16_ROWS, n), F32)],
        compiler_params=_cparams("parallel", "parallel"),
        name="even_in_proj_prep",
    )(x, x, x, g.reshape(1, d), mod, ep['w_in'], ep['mu_prev'], ep['mu_next'], ep['w0'], ep['w2pad'], ep['a0'], ep['a2pad'], ep['g2'], ep['k_k'],
      ep['k_a'], ep['q_norm_t'], ep['k_norm_t'], tabs['cos'][:t], tabs['sin'][:t], tabs['e'])


HEADS_PER_GROUP = 2
GROUP_LANES = HEADS_PER_GROUP * HEAD_DIM


def _block_diag(x, bmask):
    return jnp.where(bmask, jnp.concatenate([x] * HEADS_PER_GROUP, axis=0), jnp.zeros((), x.dtype))


def _head_mm(lhs, rhs, bmask, nt=False):
    dn = (((1,), (1,)), ((), ())) if nt else (((1,), (0,)), ((), ()))
    return lax.dot_general(lhs.astype(BF16), _block_diag(rhs.astype(BF16), bmask), dn, preferred_element_type=F32)


def _head_mm_tn(lhs, rhs, lane_head):
    full = lax.dot_general(lhs.astype(BF16), rhs.astype(BF16), (((0,), (0,)), ((), ())), preferred_element_type=F32)
    out = jnp.where(lane_head == 0, full[0:HEAD_DIM], 0.0)
    for h in range(1, HEADS_PER_GROUP):
        out = out + jnp.where(lane_head == h, full[h * HEAD_DIM:(h + 1) * HEAD_DIM], 0.0)
    return out


def _group_masks():
    r = lax.broadcasted_iota(jnp.int32, (GROUP_LANES, GROUP_LANES), 0)
    c = lax.broadcasted_iota(jnp.int32, (GROUP_LANES, GROUP_LANES), 1)
    bmask = (r // HEAD_DIM) == (c // HEAD_DIM)
    lane_head = lax.broadcasted_iota(jnp.int32, (1, GROUP_LANES), 1) // HEAD_DIM
    return bmask, lane_head


def _rwkv_prep_kernel(r_ref, v_ref, kk_ref, lw_ref, a_ref, kd_ref, r2_ref, yl_ref, t_ref, z_ref, *, cs):
    sgn = 1 - 2 * pl.program_id(0)
    row = lax.broadcasted_iota(jnp.int32, (cs, cs), 0)
    col = lax.broadcasted_iota(jnp.int32, (cs, cs), 1)
    tri = (((row - col) * sgn) >= 0).astype(F32)
    bmask, lane_head = _group_masks()
    t_idx = lax.broadcasted_iota(jnp.int32, (cs, GROUP_LANES), 0)
    s_idx = lax.broadcasted_iota(jnp.int32, (cs, GROUP_LANES), 1) % HEAD_DIM
    diff = (t_idx - s_idx) * sgn
    strict = diff > 0
    incl = diff >= 0
    eye = (diff == 0).astype(F32)
    base = min(RWKV_INV_BASE, cs)
    same_base = (t_idx // base) == (s_idx // base)
    groups = r_ref.shape[-1] // GROUP_LANES

    ch = []
    for c0 in range(0, r_ref.shape[1], cs):
        rows = slice(c0, c0 + cs)
        r, v, kk = r_ref[0, rows], v_ref[0, rows], kk_ref[0, rows]
        lw, a, kd = lw_ref[0, 0, rows], a_ref[0, 0, rows], kd_ref[0, 0, rows]
        g = _dot_hi(tri, lw)
        gp = g - lw
        gref = g[cs // 2:cs // 2 + 1]
        gend = jnp.sum(lw, axis=0, keepdims=True)
        bvec = kk * a
        full = {'a_t': -kk * jnp.exp(gp - gref), 'a_0': -kk * jnp.exp(gp),
                'b_t': bvec * jnp.exp(gref - g), 'b_e': bvec * jnp.exp(gend - g),
                'k_t': kd * jnp.exp(gref - g), 'k_e': kd * jnp.exp(gend - g),
                'r_t': r * jnp.exp(g - gref), 'r_0': r * jnp.exp(g), 'v': v,
                'wend': jnp.broadcast_to(jnp.exp(gend), (cs, v.shape[-1]))}
        for gi in range(groups):
            sl = slice(gi * GROUP_LANES, (gi + 1) * GROUP_LANES)
            c = {k: x[:, sl] for k, x in full.items()}
            c['rows'], c['sl'] = rows, sl
            ch.append(c)

    mm = lambda x, y, nt=False: [_head_mm(p_, q_, bmask, nt) for p_, q_ in zip(x, y)]
    get = lambda k: [c[k] for c in ch]
    ar = [jnp.concatenate([c['a_t'], c['r_t']], axis=0) for c in ch]
    gb = mm(ar, get('b_t'), True)
    gk = mm(ar, get('k_t'), True)
    nmat = [jnp.where(strict, x[0:cs], 0.0) for x in gb]
    g_rb = [jnp.where(incl, x[cs:], 0.0) for x in gb]
    g_k = [jnp.concatenate([jnp.where(strict, x[0:cs], 0.0), jnp.where(incl, x[cs:], 0.0)], axis=0) for x in gk]
    npow = [jnp.where(same_base, x, 0.0) for x in nmat]
    p = [eye + x for x in npow]
    for _ in range(int(math.log2(base)) - 1):
        npow = mm(npow, npow)
        p = [x + y for x, y in zip(p, mm(npow, p))]
    m = base
    while m < cs:
        off = ((t_idx // m) != (s_idx // m)) & ((t_idx // (2 * m)) == (s_idx // (2 * m)))
        q = mm([jnp.where(off, x, 0.0) for x in nmat], p)
        p = [x + y for x, y in zip(p, mm(p, q))]
        m *= 2
    vg = get('v')
    a2 = mm(p, get('a_0'))
    kv = mm(g_k, vg)
    u_v = mm(p, [x[0:cs] for x in kv])
    r2 = mm(g_rb, a2)
    yl1 = mm(g_rb, u_v)
    yl2 = [x[cs:] for x in kv]
    tt = [_head_mm_tn(x, c['b_e'], lane_head) for x, c in zip(a2, ch)]
    zz = [_head_mm_tn(jnp.concatenate([u, c['v']], axis=0), jnp.concatenate([c['b_e'], c['k_e']], axis=0),
                      lane_head) for u, c in zip(u_v, ch)]
    for i, c in enumerate(ch):
        rows, sl = c['rows'], c['sl']
        r2_ref[0, 0, rows, sl] = (c['r_0'] + r2[i]).astype(BF16)
        yl_ref[0, 0, rows, sl] = yl1[i] + yl2[i]
        t_ref[0, 0, rows, sl] = (eye * c['wend'] + tt[i]).astype(BF16)
        z_ref[0, 0, rows, sl] = zz[i]


def _rwkv_scan_kernel(h0_ref, r2f_ref, ylf_ref, tf_ref, zf_ref, r2b_ref, ylb_ref, tb_ref, zb_ref,
                      yf_ref, yb_ref, ht_ref, s_scr):
    ci = pl.program_id(0)

    @pl.when(ci == 0)
    def _():
        s_scr[...] = h0_ref[...]

    bmask, _ = _group_masks()
    ins = ((r2f_ref, ylf_ref, tf_ref, zf_ref, yf_ref), (r2b_ref, ylb_ref, tb_ref, zb_ref, yb_ref))
    for d, (r2_ref, yl_ref, t_ref, z_ref, y_ref) in enumerate(ins):
        for b in range(s_scr.shape[1]):
            for gi in range(s_scr.shape[-1] // GROUP_LANES):
                sl = slice(gi * GROUP_LANES, (gi + 1) * GROUP_LANES)
                s = s_scr[d, b, :, sl]
                y_ref[0, b, :, sl] = yl_ref[0, b, :, sl] + _head_mm(r2_ref[0, b, :, sl], s, bmask, nt=True)
                s_scr[d, b, :, sl] = _head_mm(s, t_ref[0, b, :, sl], bmask) + z_ref[0, b, :, sl]

    @pl.when(ci == pl.num_programs(0) - 1)
    def _():
        ht_ref[...] = s_scr[...]


def _rwkv(r, v, kk, lw, a, kd, h0):
    b, t, c = r.shape
    cs = min(RWKV_CHUNK, t)
    assert cs == HEAD_DIM and c % GROUP_LANES == 0
    nc = t // cs
    rows = cs * math.gcd(nc, RWKV_PREP_CHUNKS)
    tok = pl.BlockSpec((1, rows, c), lambda d, bi, ci: (bi, ci, 0))
    tok2 = pl.BlockSpec((1, 1, rows, c), lambda d, bi, ci: (d, bi, ci, 0))
    wide = jax.ShapeDtypeStruct((2, b, t, c), F32)
    mm_operand = jax.ShapeDtypeStruct((2, b, t, c), BF16)
    r2, yl, tt, zz = pl.pallas_call(
        functools.partial(_rwkv_prep_kernel, cs=cs),
        grid=(2, b, t // rows),
        in_specs=[tok, tok, tok, tok2, tok2, tok2],
        out_specs=[tok2, tok2, tok2, tok2],
        out_shape=[mm_operand, wide, mm_operand, wide],
        compiler_params=_cparams("parallel", "parallel", "parallel"),
        name="rwkv7_chunk_prep",
    )(r, v, kk, lw, a, kd)

    fwd = pl.BlockSpec((1, b, cs, c), lambda ci: (0, 0, ci, 0))
    bwd = pl.BlockSpec((1, b, cs, c), lambda ci: (1, 0, nc - 1 - ci, 0))
    st = pl.BlockSpec((2, b, HEAD_DIM, c), lambda ci: (0, 0, 0, 0))
    yf, yb, ht = pl.pallas_call(
        _rwkv_scan_kernel,
        grid=(nc,),
        in_specs=[st, fwd, fwd, fwd, fwd, bwd, bwd, bwd, bwd],
        out_specs=[pl.BlockSpec((1, b, cs, c), lambda ci: (0, 0, ci, 0)),
                   pl.BlockSpec((1, b, cs, c), lambda ci: (0, 0, nc - 1 - ci, 0)), st],
        out_shape=[jax.ShapeDtypeStruct((1, b, t, c), F32), jax.ShapeDtypeStruct((1, b, t, c), F32),
                   jax.ShapeDtypeStruct((2, b, HEAD_DIM, c), F32)],
        scratch_shapes=[pltpu.VMEM((2, b, HEAD_DIM, c), F32)],
        compiler_params=_cparams("arbitrary"),
        name="rwkv7_state_scan",
    )(h0, r2, yl, tt, zz, r2, yl, tt, zz)
    return (yf.reshape(b, t, c), yb.reshape(b, t, c)), ht


def _attn_kernel(sink_ref, q_ref, kp_ref, kc_ref, kn_ref, vp_ref, vc_ref, vn_ref, kx_ref, vx_ref, o_ref,
                 *, local, group):
    i = pl.program_id(1)
    nb = pl.num_programs(1)
    scale = HEAD_DIM ** -0.5
    assert math.frexp(scale)[0] == 0.5
    q = q_ref[0] * scale
    if local:
        k_all = jnp.concatenate([kp_ref[0], kc_ref[0], kn_ref[0], kx_ref[0]], axis=0)
        v_all = jnp.concatenate([vp_ref[0], vc_ref[0], vn_ref[0], vx_ref[0]], axis=0)
        qi = lax.broadcasted_iota(jnp.int32, (BLOCK, BLOCK), 0)
        kj = lax.broadcasted_iota(jnp.int32, (BLOCK, BLOCK), 1)
        assert WINDOW == BLOCK
        valid_prev = (kj >= qi) & (i > 0)
        valid_next = (kj <= qi) & (i < nb - 1)
    else:
        k_all = kx_ref[0]
        v_all = vx_ref[0]
    heads = range(q.shape[-1] // HEAD_DIM)
    kv = lambda x, h: x[:, (h // group) * HEAD_DIM:(h // group + 1) * HEAD_DIM]
    s = [_dot_nt(q[:, h * HEAD_DIM:(h + 1) * HEAD_DIM], kv(k_all, h)) for h in heads]
    if local:
        s = [jnp.concatenate([jnp.where(valid_prev, x[:, 0:BLOCK], MASK_VALUE), x[:, BLOCK:2 * BLOCK],
                              jnp.where(valid_next, x[:, 2 * BLOCK:3 * BLOCK], MASK_VALUE), x[:, 3 * BLOCK:]],
                             axis=1) for x in s]
    sink = [sink_ref[h] for h in heads]
    m = [jnp.maximum(jnp.max(x, axis=-1, keepdims=True), sk) for x, sk in zip(s, sink)]
    e = [jnp.exp(x - y) for x, y in zip(s, m)]
    den = [jnp.sum(x, axis=-1, keepdims=True) + jnp.exp(sk - y) for x, y, sk in zip(e, m, sink)]
    outs = [_dot(x, kv(v_all, h)) / dn for x, h, dn in zip(e, heads, den)]
    o_ref[0] = jnp.concatenate(outs, axis=1)


def _attention(q, k, v, kx, vx, sink, local):
    b, t, nq = q.shape
    nk = k.shape[-1]
    nb = t // BLOCK
    group = (nq // HEAD_DIM) // (nk // HEAD_DIM)
    lx = kx.shape[1]
    kv = lambda f: pl.BlockSpec((1, BLOCK, nk), f)
    pf = lambda bi, i: (bi, jnp.maximum(i - 1, 0), 0)
    cf = lambda bi, i: (bi, i, 0)
    nf = lambda bi, i: (bi, jnp.minimum(i + 1, nb - 1), 0)
    ctx = pl.BlockSpec((1, lx, nk), lambda bi, i: (bi, 0, 0))
    return pl.pallas_call(
        functools.partial(_attn_kernel, local=local, group=group),
        grid=(b, nb),
        in_specs=[pl.BlockSpec(memory_space=pltpu.SMEM),
                  pl.BlockSpec((1, BLOCK, nq), cf), kv(pf), kv(cf), kv(nf), kv(pf), kv(cf), kv(nf), ctx, ctx],
        out_specs=pl.BlockSpec((1, BLOCK, nq), cf),
        out_shape=jax.ShapeDtypeStruct((b, t, nq), F32),
        compiler_params=_cparams("parallel", "parallel"),
        name="window_attention" if local else "context_attention",
    )(sink, q, k, k, k, v, v, v, kx, vx)


def _even_out_kernel(yf_ref, yb_ref, r_ref, v_ref, g_ref, kd_ref, batt_ref, x_ref, mod_ref, lnw_ref, lnb_ref, rk_ref, e_ref,
                     w_ref, o_ref, *, d, c):
    y = yf_ref[0] + yb_ref[0]
    inv = 1.0 / HEAD_DIM
    mu = _segsum(y, e_ref) * inv
    yc = y - mu
    var = _segsum(yc * yc, e_ref) * inv
    yn = yc * lax.rsqrt(var + RWKV_GN_EPS) * lnw_ref[...] + lnb_ref[...]
    bonus = _segsum(r_ref[0] * (kd_ref[0, 0] + kd_ref[1, 0]) * rk_ref[...], e_ref)
    a_out = (yn + bonus * v_ref[0]) * g_ref[0]
    o = _dot(a_out, w_ref[0:c]) + _dot(batt_ref[0], w_ref[c:])
    o_ref[0] = x_ref[0] + mod_ref[0][:, 2 * d:3 * d] * o


def _even_out(y, r, v, g, kd, batt, x, mod, ep, tabs):
    b, t, d = x.shape
    c = r.shape[-1]
    nq = batt.shape[-1]
    tm = _row_tile(t, 512)
    tok = lambda w: pl.BlockSpec((1, tm, w), lambda bi, i: (bi, i, 0))
    tok2 = lambda w: pl.BlockSpec((2, 1, tm, w), lambda bi, i: (0, bi, i, 0))
    full = lambda *shape: pl.BlockSpec(shape, lambda bi, i: (0,) * len(shape))
    return pl.pallas_call(
        functools.partial(_even_out_kernel, d=d, c=c),
        grid=(b, t // tm),
        in_specs=[tok(c), tok(c), tok(c), tok(c), tok(c), tok2(c), tok(nq), tok(d),
                  pl.BlockSpec((1, 1, mod.shape[-1]), lambda bi, i: (bi, 0, 0)),
                  full(1, c), full(1, c), full(1, c), full(c, c), full(c + nq, d)],
        out_specs=tok(d),
        out_shape=jax.ShapeDtypeStruct((b, t, d), F32),
        compiler_params=_cparams("parallel", "parallel"),
        name="even_out_proj",
    )(y[0], y[1], r, v, g, kd, batt, x, mod, ep['ln_w'], ep['ln_b'], ep['r_k'], tabs['e'], ep['w_out'])


def _filter_kernel(z_ref, t_ref, w1_ref, b1_ref, w2_ref, b2_ref, w3_ref, b3_ref, fr_ref, wo_ref, dl_ref, o_ref, *, d):
    fr = fr_ref[...]
    h = jnp.sin(fr * (_dot_hi(z_ref[...], w1_ref[...]) + b1_ref[...]))
    h = jnp.sin(fr * (_dot_hi(h, w2_ref[...]) + b2_ref[...]))
    h = jnp.sin(fr * (_dot_hi(h, w3_ref[...]) + b3_ref[...]))
    filt = _dot_hi(h, wo_ref[...])
    modu = jnp.exp(-t_ref[...] * dl_ref[...]) + HY_MOD_SHIFT
    for q in range(o_ref.shape[0]):
        o_ref[q] = filt[:, q * d:(q + 1) * d] * modu


def _hyena_filters(n, op, d):
    t = np.linspace(0.0, 1.0, n, dtype=np.float32)[:, None]
    ang = (2.0 * math.pi * np.arange(n, dtype=np.float32)[:, None] / np.float32(n)).astype(np.float32)
    f = np.linspace(1e-4, HY_BANDS - 1, HY_BANDS, dtype=np.float32)[None, :]
    zfeat = jnp.concatenate([jnp.asarray(t), jnp.cos(jnp.asarray(f * ang)), -jnp.sin(jnp.asarray(f * ang))], axis=-1)
    emb_pad = op['f_w1'].shape[0]
    zfeat = jnp.pad(zfeat, ((0, 0), (0, emb_pad - HY_EMB)))
    deltas = np.abs(np.linspace(math.log(HY_TARGET) / HY_SLOW_PCT, math.log(HY_TARGET) / HY_FAST_PCT, d,
                                dtype=np.float32))[None, :]
    tn = min(n, 256)
    nq = 2 * HY_ORDER
    hf = op['f_w2'].shape[0]
    full = lambda *shape: pl.BlockSpec(shape, lambda i: (0,) * len(shape))
    return pl.pallas_call(
        functools.partial(_filter_kernel, d=d),
        grid=(n // tn,),
        in_specs=[pl.BlockSpec((tn, emb_pad), lambda i: (i, 0)), pl.BlockSpec((tn, 1), lambda i: (i, 0)),
                  full(emb_pad, hf), full(1, hf), full(hf, hf), full(1, hf), full(hf, hf), full(1, hf), full(1, hf),
                  full(hf, nq * d), full(1, d)],
        out_specs=pl.BlockSpec((nq, tn, d), lambda i: (0, i, 0)),
        out_shape=jax.ShapeDtypeStruct((nq, n, d), F32),
        compiler_params=_cparams("parallel"),
        name="hyena_filter",
    )(zfeat, jnp.asarray(t), op['f_w1'], op['f_b1'], op['f_w2'], op['f_b2'], op['f_w3'], op['f_b3'], op['f_freq'],
      op['f_out'], jnp.asarray(deltas))


def _dft(n, rows, cols, sign=-1.0):
    k = np.arange(rows, dtype=np.float64)[:, None]
    m = np.arange(cols, dtype=np.float64)[None, :]
    ang = sign * 2.0 * np.pi * ((k * m) % n) / n
    return np.cos(ang), np.sin(ang)


def _stack(re, im):
    return jnp.asarray(np.concatenate([re, im], axis=0).astype(np.float32)).astype(BF16)


def _fft_tables(n_seq):
    n = 2 * n_seq
    n1 = n // FFT_N2
    f1r, f1i = _dft(n1, n1, n1 // 2)
    h1r, h1i = _dft(n1, n1 // 2, n1, sign=1.0)
    k1 = np.arange(n1, dtype=np.float64)[:, None, None]
    k2 = np.arange(FFT_N2, dtype=np.float64)[None, :, None]
    j2 = np.arange(FFT_N2, dtype=np.float64)[None, None, :]
    ang = -2.0 * np.pi * (((k2 * j2 * n1) + k1 * j2) % n) / n
    gr, gi = np.cos(ang), np.sin(ang)
    g_fwd = np.concatenate([gr, gi], axis=1)
    g_inv = np.concatenate([np.swapaxes(gr, 1, 2), np.swapaxes(gi, 1, 2)], axis=1)
    eye = np.eye(SUBLANES)
    kron = lambda m: np.kron(m, eye)
    return {'f1': _stack(kron(f1r), kron(f1i)), 'h1': _stack(kron(h1r), kron(h1i)),
            'g_fwd': jnp.asarray(g_fwd.astype(np.float32)).astype(BF16),
            'g_inv': jnp.asarray(g_inv.astype(np.float32)).astype(BF16), 'n1': n1}


def _dense_tables(n_seq):
    n = 2 * n_seq
    fr, fi = _dft(n, n, n_seq)
    hr, hi = _dft(n, n_seq, n, sign=1.0)
    return {'f': _stack(fr, fi), 'h': _stack(hr, hi)}


def _fft_a_kernel(f_ref, zr_ref, zi_ref, ar_ref, ai_ref, *, n1, cplx):
    f = f_ref[...]
    half, tj, d = zr_ref.shape[1:]
    m = n1 * SUBLANES
    res_r, res_i = [], []
    zr_all = zr_ref[0].astype(F32)
    zi_all = zi_ref[0].astype(F32) if cplx else None
    for s0 in range(0, tj, SUBLANES):
        js = slice(s0, s0 + SUBLANES)
        zr = zr_all[:, js, :].reshape(half * SUBLANES, d)
        p = jnp.dot(f, zr.astype(BF16), preferred_element_type=F32)
        if cplx:
            zi = zi_all[:, js, :].reshape(half * SUBLANES, d)
            q = jnp.dot(f, zi.astype(BF16), preferred_element_type=F32)
            res_r.append((p[0:m] - q[m:]).reshape(n1, SUBLANES, d))
            res_i.append((p[m:] + q[0:m]).reshape(n1, SUBLANES, d))
        else:
            res_r.append(p[0:m].reshape(n1, SUBLANES, d))
            res_i.append(p[m:].reshape(n1, SUBLANES, d))
    ar_ref[0] = jnp.concatenate(res_r, axis=1).astype(BF16)
    ai_ref[0] = jnp.concatenate(res_i, axis=1).astype(BF16)


def _fft_a(u5, q, tabs, cplx):
    _, s, half, n2, d = u5.shape
    n1 = tabs['n1']
    assert half == n1 // 2 and n2 == FFT_N2
    sp = s // 2 if cplx else s
    tj = 2 * SUBLANES
    zi_map = (lambda si, j: (q, si + sp, 0, j, 0)) if cplx else (lambda si, j: (q, si, 0, j, 0))
    out = pl.BlockSpec((1, n1, tj, d), lambda si, j: (si, 0, j, 0))
    return pl.pallas_call(
        functools.partial(_fft_a_kernel, n1=n1, cplx=cplx),
        grid=(sp, n2 // tj),
        in_specs=[pl.BlockSpec((2 * n1 * SUBLANES, half * SUBLANES), lambda si, j: (0, 0)),
                  pl.BlockSpec((None, 1, half, tj, d), lambda si, j: (q, si, 0, j, 0)),
                  pl.BlockSpec((None, 1, half, tj, d), zi_map)],
        out_specs=[out, out],
        out_shape=[jax.ShapeDtypeStruct((sp, n1, n2, d), BF16)] * 2,
        compiler_params=_cparams("parallel", "parallel"),
        name="fft_stage1",
    )(tabs['f1'], u5, u5)


def _cplx_mm(s, xr, xi, conj):
    p = jnp.dot(s, xr.astype(BF16), preferred_element_type=F32)
    q = jnp.dot(s, xi.astype(BF16), preferred_element_type=F32)
    m = s.shape[0] // 2
    if conj:
        return p[0:m] + q[m:], q[0:m] - p[m:]
    return p[0:m] - q[m:], p[m:] + q[0:m]


def _fft_b_kernel(gf_ref, gi_ref, ar_ref, ai_ref, kr_ref, ki_ref, dr_ref, di_ref):
    kr = kr_ref[0]
    ki = ki_ref[0]
    seqs = range(ar_ref.shape[0])
    c = [_cplx_mm(gf_ref[0], ar_ref[s, 0], ai_ref[s, 0], False) for s in seqs]
    e = [(cr * kr - ci * ki, cr * ki + ci * kr) for cr, ci in c]
    dd = [_cplx_mm(gi_ref[0], er, ei, True) for er, ei in e]
    for s, (dr, di) in zip(seqs, dd):
        dr_ref[s, 0] = dr.astype(BF16)
        di_ref[s, 0] = di.astype(BF16)


def _fft_b(ar4, ai4, kr, ki, order, tabs):
    sp, n1, _, d = ar4.shape
    ns = 2 if sp % 2 == 0 else 1
    blk = pl.BlockSpec((ns, 1, FFT_N2, d), lambda k1, si: (si, k1, 0, 0))
    tab = pl.BlockSpec((1, 2 * FFT_N2, FFT_N2), lambda k1, si: (k1, 0, 0))
    kb = pl.BlockSpec((None, 1, FFT_N2, d), lambda k1, si: (order, k1, 0, 0))
    return pl.pallas_call(
        _fft_b_kernel,
        grid=(n1, sp // ns),
        in_specs=[tab, tab, blk, blk, kb, kb],
        out_specs=[blk, blk],
        out_shape=[jax.ShapeDtypeStruct((sp, n1, FFT_N2, d), BF16)] * 2,
        compiler_params=_cparams("parallel", "arbitrary"),
        name="fft_stage2_filter",
    )(tabs['g_fwd'], tabs['g_inv'], ar4, ai4, kr, ki)


def _fft_c_kernel(h_ref, dr_ref, di_ref, u_ref, x_ref, sk_ref, o_ref):
    h = h_ref[...]
    n1, tj, d = dr_ref.shape[1:]
    half = n1 // 2
    sk = sk_ref[...]
    dr_all = dr_ref[0].astype(F32)
    di_all = di_ref[0].astype(F32)
    u_all = [u_ref[part, 0].astype(F32) for part in range(2)]
    x_all = [x_ref[part, 0].astype(F32) for part in range(2)]
    outs = [[], []]
    for s0 in range(0, tj, SUBLANES):
        js = slice(s0, s0 + SUBLANES)
        yr, yi = _cplx_mm(h, dr_all[:, js, :].reshape(n1 * SUBLANES, d),
                          di_all[:, js, :].reshape(n1 * SUBLANES, d), False)
        for part, y in enumerate((yr, yi)):
            outs[part].append(x_all[part][:, js, :] * (y.reshape(half, SUBLANES, d) + u_all[part][:, js, :] * sk))
    for part in range(2):
        o_ref[part, 0] = jnp.concatenate(outs[part], axis=1).astype(o_ref.dtype)


def _fft_c(dr, di, u6, uq, x6, xq, skip, tabs):
    sp, n1, n2, d = dr.shape
    half = n1 // 2
    tj = 2 * SUBLANES
    pair = lambda q: pl.BlockSpec((None, 2, 1, half, tj, d), lambda si, j: (q, 0, si, 0, j, 0))
    dblk = pl.BlockSpec((1, n1, tj, d), lambda si, j: (si, 0, j, 0))
    out = pl.pallas_call(
        _fft_c_kernel,
        grid=(sp, n2 // tj),
        in_specs=[pl.BlockSpec((2 * half * SUBLANES, n1 * SUBLANES), lambda si, j: (0, 0)), dblk, dblk,
                  pair(uq), pair(xq), pl.BlockSpec((1, d), lambda si, j: (0, 0))],
        out_specs=pl.BlockSpec((2, 1, half, tj, d), lambda si, j: (0, si, 0, j, 0)),
        out_shape=jax.ShapeDtypeStruct((2, sp, half, n2, d), BF16),
        compiler_params=_cparams("parallel", "parallel"),
        name="fft_inverse_stage1_gate",
    )(tabs['h1'], dr, di, u6, x6, skip.reshape(1, d))
    return out


def _spec_b_kernel(gf_ref, ar_ref, ai_ref, kr_ref, ki_ref, *, scale):
    fr, fi = _cplx_mm(gf_ref[0], ar_ref[0, 0], ai_ref[0, 0], False)
    gr, gi = _cplx_mm(gf_ref[0], ar_ref[1, 0], ai_ref[1, 0], False)
    kr_ref[0, 0] = (fr + gr) * scale
    ki_ref[0, 0] = (fi - gi) * scale


def _filter_spectrum_fft(filt, tabs, d):
    nq, n, _ = filt.shape
    n1 = tabs['n1']
    ar, ai = _fft_a(filt.reshape(1, nq, n1 // 2, FFT_N2, d), 0, tabs, cplx=False)
    orders = nq // 2
    ar5 = ar.reshape(orders, 2, n1, FFT_N2, d)
    ai5 = ai.reshape(orders, 2, n1, FFT_N2, d)
    blk = pl.BlockSpec((None, 2, 1, FFT_N2, d), lambda o, k1: (o, 0, k1, 0, 0))
    out = pl.BlockSpec((1, 1, FFT_N2, d), lambda o, k1: (o, k1, 0, 0))
    return pl.pallas_call(
        functools.partial(_spec_b_kernel, scale=1.0 / (2 * n)),
        grid=(orders, n1),
        in_specs=[pl.BlockSpec((1, 2 * FFT_N2, FFT_N2), lambda o, k1: (k1, 0, 0)), blk, blk],
        out_specs=[out, out],
        out_shape=[jax.ShapeDtypeStruct((orders, n1, FFT_N2, d), F32)] * 2,
        compiler_params=_cparams("parallel", "parallel"),
        name="filter_spectrum",
    )(tabs['g_fwd'], ar5, ai5)


def _long_conv_fft(u, uq, x, xq, skip, kr, ki, order, tabs):
    _, b, n, d = u.shape
    half = tabs['n1'] // 2
    ar, ai = _fft_a(u.reshape(u.shape[0], b, half, FFT_N2, d), uq, tabs, cplx=True)
    dr, di = _fft_b(ar, ai, kr, ki, order, tabs)
    six = lambda a: a.reshape(a.shape[0], 2, b // 2, half, FFT_N2, d)
    return _fft_c(dr, di, six(u), uq, six(x), xq, skip, tabs).reshape(b, n, d)


def _dense_spec_kernel(f_ref, filt_ref, kr_ref, ki_ref, *, scale):
    f = f_ref[...]
    n = f.shape[0] // 2
    pf = jnp.dot(f, filt_ref[0, 0].astype(BF16), preferred_element_type=F32)
    pg = jnp.dot(f, filt_ref[0, 1].astype(BF16), preferred_element_type=F32)
    kr_ref[0] = (pf[0:n] + pg[0:n]) * scale
    ki_ref[0] = (pf[n:] - pg[n:]) * scale


def _filter_spectrum_dense(filt, tabs, d):
    nq, n, _ = filt.shape
    orders = nq // 2
    f4 = filt.reshape(orders, 2, n, d)
    out = pl.BlockSpec((1, 2 * n, d), lambda o: (o, 0, 0))
    return pl.pallas_call(
        functools.partial(_dense_spec_kernel, scale=1.0 / (2 * n)),
        grid=(orders,),
        in_specs=[pl.BlockSpec((4 * n, n), lambda o: (0, 0)), pl.BlockSpec((1, 2, n, d), lambda o: (o, 0, 0, 0))],
        out_specs=[out, out],
        out_shape=[jax.ShapeDtypeStruct((orders, 2 * n, d), F32)] * 2,
        compiler_params=_cparams("parallel"),
        name="filter_spectrum_dense",
    )(tabs['f'], f4)


def _dense_conv_kernel(f_ref, h_ref, u_ref, x_ref, kr_ref, ki_ref, sk_ref, o_ref):
    cr, ci = _cplx_mm(f_ref[...], u_ref[0, 0], u_ref[1, 0], False)
    kr = kr_ref[...]
    ki = ki_ref[...]
    yr, yi = _cplx_mm(h_ref[...], cr * kr - ci * ki, cr * ki + ci * kr, False)
    sk = sk_ref[...]
    o_ref[0, 0] = x_ref[0, 0] * (yr + u_ref[0, 0] * sk)
    o_ref[1, 0] = x_ref[1, 0] * (yi + u_ref[1, 0] * sk)


def _long_conv_dense(u, x, skip, kr, ki, tabs):
    b, n, d = u.shape
    sp = b // 2
    pair = pl.BlockSpec((2, 1, n, d), lambda si: (0, si, 0, 0))
    kb = pl.BlockSpec((2 * n, d), lambda si: (0, 0))
    out = pl.pallas_call(
        _dense_conv_kernel,
        grid=(sp,),
        in_specs=[pl.BlockSpec((4 * n, n), lambda si: (0, 0)), pl.BlockSpec((2 * n, 2 * n), lambda si: (0, 0)),
                  pair, pair, kb, kb, pl.BlockSpec((1, d), lambda si: (0, 0))],
        out_specs=pair,
        out_shape=jax.ShapeDtypeStruct((2, sp, n, d), F32),
        compiler_params=_cparams("parallel"),
        name="long_conv_dense",
    )(tabs['f'], tabs['h'], u.reshape(2, sp, n, d), x.reshape(2, sp, n, d), kr, ki, skip.reshape(1, d))
    return out.reshape(b, n, d)


def _hyena_mixer(x, g, mod, op, fft_tabs, dense_tabs):
    b, n, d = x.shape
    z = _proj_conv(x, g, mod, op['w_in'], op['b_in'], op['conv_w'], op['conv_b'])
    filt = _hyena_filters(n, op, d)
    if n <= DENSE_FFT_MAX:
        kr, ki = _filter_spectrum_dense(filt, dense_tabs, d)
        y = _long_conv_dense(z[0], z[1], op['skip'][0], kr[0], ki[0], dense_tabs)
        y = _long_conv_dense(y, z[2], op['skip'][1], kr[1], ki[1], dense_tabs)
    else:
        kr, ki = _filter_spectrum_fft(filt, fft_tabs, d)
        y = _long_conv_fft(z, 0, z, 1, op['skip'][0], kr, ki, 0, fft_tabs)
        y = _long_conv_fft(y[None], 0, z, 2, op['skip'][1], kr, ki, 1, fft_tabs)
    return _out_res(y, x, mod, op['w_out'], op['b_out'])


def _even_mixer(x, ctx, g, mod_l, mod_c, ep, tabs, need_ctx):
    rc, vc, kkc, gc, lwc, ac, kdc, qc, kac, vac = _even_prep(ctx, g, mod_c, ep, tabs, rope=False)
    rl, vl, kkl, gl, lwl, al, kdl, ql, kal, val = _even_prep(x, g, mod_l, ep, tabs, rope=True)
    b = x.shape[0]
    c = rl.shape[-1]
    h0 = jnp.zeros((2, b, HEAD_DIM, c), F32)
    y_ctx, s_ctx = _rwkv(rc, vc, kkc, lwc, ac, kdc, h0)
    y_lat, _ = _rwkv(rl, vl, kkl, lwl, al, kdl, s_ctx)
    b_lat = _attention(ql, kal, val, kac, vac, ep['sink'], local=True)
    x_new = _even_out(y_lat, rl, vl, gl, kdl, b_lat, x, mod_l, ep, tabs)
    if not need_ctx:
        return x_new, None
    b_ctx = _attention(qc, kac, vac, kac, vac, ep['sink'], local=False)
    ctx_new = _even_out(y_ctx, rc, vc, gc, kdc, b_ctx, ctx, mod_c, ep, tabs)
    return x_new, ctx_new


def _rope_tables(n_tokens):
    rows = n_tokens // GRID_W
    row = jnp.repeat(jnp.arange(rows), GRID_W).astype(F32)
    col = jnp.tile(jnp.arange(GRID_W), rows).astype(F32)
    n_freq = HEAD_DIM // 4
    inv = ROPE_THETA ** (-jnp.arange(n_freq, dtype=F32) / n_freq)
    ang = jnp.concatenate([row[:, None] * inv, col[:, None] * inv], axis=-1)
    cos, sin = jnp.cos(ang), jnp.sin(ang)
    reps = LANES // HEAD_DIM
    cos_t = jnp.tile(jnp.concatenate([cos, cos], axis=-1), (1, reps))
    sin_t = jnp.tile(jnp.concatenate([-sin, sin], axis=-1), (1, reps))
    return cos_t, sin_t


def _block_ones(width):
    idx = np.arange(width) // HEAD_DIM
    return jnp.asarray((idx[:, None] == idx[None, :]).astype(np.float32)).astype(BF16)


def _lora_pad(w):
    z = jnp.zeros_like(w[0])
    return jnp.stack([jnp.concatenate([w[0], z], axis=0), jnp.concatenate([z, w[1]], axis=0)], axis=0)


def kernel(x, c, ctx, c_ctx, ada_w, ada_b, norm1_g, norm2_g, ffn_up, ffn_conv_w, ffn_conv_b, ffn_down, ev_w_in, ev_mu_prev, ev_mu_next, ev_w0, ev_w2, ev_a0, ev_a2, ev_g2, ev_k_k, ev_k_a, ev_r_k, ev_ln_w, ev_ln_b, ev_q_norm, ev_k_norm, ev_sink, ev_w_out, od_w_in, od_b_in, od_conv_w, od_conv_b, od_f_w1, od_f_b1, od_f_w2, od_f_b2, od_f_w3, od_f_b3, od_f_freq, od_f_out, od_skip, od_w_out, od_b_out):
    bsz, seq, d = x.shape
    lc = ctx.shape[1]
    depth = ada_w.shape[0]
    a_width = ev_k_k.shape[-1]
    nq = ev_sink.shape[-1] * HEAD_DIM
    nk = B_KV_HEADS * HEAD_DIM

    cond = jnp.zeros((BF16_ROWS, d), F32).at[:bsz].set(c).at[bsz].set(c_ctx)
    mod = _ada_mod(cond, ada_w, ada_b)

    cos_t, sin_t = _rope_tables(seq)
    tabs = {'cos': cos_t, 'sin': sin_t, 'e': _block_ones(nq)}
    fft_tabs = _fft_tables(seq) if seq > DENSE_FFT_MAX else None
    dense_lat = _dense_tables(seq) if seq <= DENSE_FFT_MAX else None
    fft_ctx = _fft_tables(lc) if lc > DENSE_FFT_MAX else None
    dense_ctx = _dense_tables(lc) if lc <= DENSE_FFT_MAX else None

    for layer in range(depth):
        need_ctx = layer < depth - 1
        even = layer % 2 == 0
        j = layer // 2
        mod_l = mod[layer, :bsz].reshape(bsz, 1, 6 * d)
        mod_c = jnp.broadcast_to(mod[layer, bsz].reshape(1, 1, 6 * d), (bsz, 1, 6 * d))
        if even:
            ep = {'w_in': ev_w_in[j].astype(BF16), 'mu_prev': ev_mu_prev[j][None], 'mu_next': ev_mu_next[j][None],
                  'w0': ev_w0[j], 'w2pad': _lora_pad(ev_w2[j]).astype(BF16), 'a0': ev_a0[j],
                  'a2pad': _lora_pad(ev_a2[j]).astype(BF16), 'g2': ev_g2[j].astype(BF16),
                  'k_k': ev_k_k[j][None], 'k_a': ev_k_a[j][None], 'r_k': ev_r_k[j].reshape(1, a_width),
                  'ln_w': ev_ln_w[j][None], 'ln_b': ev_ln_b[j][None],
                  'q_norm_t': jnp.tile(ev_q_norm[j], nq // HEAD_DIM)[None],
                  'k_norm_t': jnp.tile(ev_k_norm[j], nk // HEAD_DIM)[None],
                  'sink': ev_sink[j], 'w_out': ev_w_out[j].astype(BF16), 'nq': nq, 'nk': nk}
            x, ctx_new = _even_mixer(x, ctx, norm1_g[layer], mod_l, mod_c, ep, tabs, need_ctx)
        else:
            emb_pad = HEAD_DIM
            op = {'w_in': od_w_in[j].astype(BF16), 'b_in': od_b_in[j], 'conv_w': od_conv_w[j],
                  'conv_b': od_conv_b[j],
                  'f_w1': jnp.pad(od_f_w1[j], ((0, emb_pad - HY_EMB), (0, 0))), 'f_b1': od_f_b1[j][None],
                  'f_w2': od_f_w2[j], 'f_b2': od_f_b2[j][None], 'f_w3': od_f_w3[j], 'f_b3': od_f_b3[j][None],
                  'f_freq': od_f_freq[j][None], 'f_out': od_f_out[j], 'skip': od_skip[j],
                  'w_out': od_w_out[j].astype(BF16), 'b_out': od_b_out[j]}
            ctx_new = _hyena_mixer(ctx, norm1_g[layer], mod_c, op, fft_ctx, dense_ctx) if need_ctx else None
            x = _hyena_mixer(x, norm1_g[layer], mod_l, op, fft_tabs, dense_lat)
        w_up = ffn_up[layer].astype(BF16)
        w_down = ffn_down[layer].astype(BF16)
        x = _conv_ffn(x, norm2_g[layer], mod_l, w_up, ffn_conv_w[layer], ffn_conv_b[layer], w_down)
        if need_ctx:
            ctx = _conv_ffn(ctx_new, norm2_g[layer], mod_c, w_up, ffn_conv_w[layer], ffn_conv_b[layer], w_down)
    return x
```

```python
import functools
import math

import numpy as np
import jax
import jax.numpy as jnp
from jax import lax
from jax.experimental import pallas as pl
from jax.experimental.pallas import tpu as pltpu

F32 = jnp.float32
BF16 = jnp.bfloat16

HEAD_DIM = 64
GRID_W = 64
DECAY_LORA = 64
ICLR_LORA = 64
GATE_LORA = 128
RWKV_GN_EPS = 64e-5
B_KV_HEADS = 2
WINDOW = 128
BLOCK = 128
ROPE_THETA = 10000.0
MASK_VALUE = -1e30
HY_ORDER = 2
HY_EMB = 33
HY_BANDS = (HY_EMB - 1) // 2
HY_TARGET = 1e-2
HY_FAST_PCT = 0.3
HY_SLOW_PCT = 1.5
HY_MOD_SHIFT = 0.05
NORM_EPS = 1e-6

V7X_VMEM_BYTES = 64 * 1024 * 1024
VMEM_LIMIT_BYTES = V7X_VMEM_BYTES * 3 // 4
LANES = 128
V7X_MXU_DIM = 256
SUBLANES = 8
BF16_ROWS = 16
FFN_ROW_PARTS = 2
RWKV_CHUNK = 64
RWKV_PREP_CHUNKS = 8
RWKV_INV_BASE = 8
FFT_N2 = 128
DENSE_FFT_MAX = 512

HIGHEST = lax.Precision.HIGHEST


def _cparams(*sem):
    return pltpu.CompilerParams(dimension_semantics=sem, vmem_limit_bytes=VMEM_LIMIT_BYTES)


def _dot(a, b):
    return jnp.dot(a.astype(BF16), b.astype(BF16), preferred_element_type=F32)


def _dot_nt(a, b):
    return lax.dot_general(a.astype(BF16), b.astype(BF16), (((1,), (1,)), ((), ())), preferred_element_type=F32)


def _dot_hi(a, b):
    return jnp.dot(a, b, preferred_element_type=F32, precision=HIGHEST)


def _segsum(x, e_ref):
    hi = x.astype(BF16)
    lo = (x - hi.astype(F32)).astype(BF16)
    w = x.shape[-1]
    tile = min(w, V7X_MXU_DIM)
    e = e_ref[0:tile, 0:tile]
    both = jnp.concatenate([hi, lo], axis=0)
    parts = [jnp.dot(both[:, c0:c0 + tile], e, preferred_element_type=F32) for c0 in range(0, w, tile)]
    full = parts[0] if len(parts) == 1 else jnp.concatenate(parts, axis=1)
    m = x.shape[0]
    return full[0:m] + full[m:]


def _norm_mod(x, g, shift, scale):
    xn = x * lax.rsqrt(jnp.mean(x * x, axis=-1, keepdims=True) + NORM_EPS)
    return (xn * g) * (1.0 + scale) + shift


def _row_tile(t, want):
    tm = min(t, want)
    assert t % tm == 0 and tm % BF16_ROWS == 0
    return tm


def _ada_kernel(c_ref, w_ref, b_ref, o_ref):
    c = c_ref[...]
    s = c * jax.nn.sigmoid(c)
    o_ref[0] = _dot(s, w_ref[0]) + b_ref[0]


def _ada_mod(cond, ada_w, ada_b):
    depth, d, n = ada_w.shape
    tn = n // 4
    rows = cond.shape[0]
    return pl.pallas_call(
        _ada_kernel,
        grid=(depth, n // tn),
        in_specs=[pl.BlockSpec((rows, d), lambda l, j: (0, 0)),
                  pl.BlockSpec((1, d, tn), lambda l, j: (l, 0, j)),
                  pl.BlockSpec((1, 1, tn), lambda l, j: (l, 0, j))],
        out_specs=pl.BlockSpec((1, rows, tn), lambda l, j: (l, 0, j)),
        out_shape=jax.ShapeDtypeStruct((depth, rows, n), F32),
        compiler_params=_cparams("parallel", "parallel"),
        name="ada_mod",
    )(cond, ada_w, ada_b.reshape(depth, 1, n))


def _halo_specs(tm, t, d, nargs):
    r = tm // BF16_ROWS
    last = t // BF16_ROWS - 1
    if nargs == 3:
        prev = pl.BlockSpec((1, BF16_ROWS, d), lambda bi, i, j: (bi, jnp.maximum(i * r - 1, 0), 0))
        nxt = pl.BlockSpec((1, BF16_ROWS, d), lambda bi, i, j: (bi, jnp.minimum((i + 1) * r, last), 0))
    else:
        prev = pl.BlockSpec((1, BF16_ROWS, d), lambda bi, i: (bi, jnp.maximum(i * r - 1, 0), 0))
        nxt = pl.BlockSpec((1, BF16_ROWS, d), lambda bi, i: (bi, jnp.minimum((i + 1) * r, last), 0))
    return prev, nxt


def _fill_h(x_ref, xp_ref, xn_ref, g, shift, scale, h_scr, tm):
    h_scr[0:BF16_ROWS] = _norm_mod(xp_ref[0], g, shift, scale).astype(BF16)
    h_scr[BF16_ROWS:BF16_ROWS + tm] = _norm_mod(x_ref[0], g, shift, scale).astype(BF16)
    h_scr[BF16_ROWS + tm:2 * BF16_ROWS + tm] = _norm_mod(xn_ref[0], g, shift, scale).astype(BF16)


def _conv3_rows(u, u_scr, cw, cb, tm, t_total, part=0, nparts=1):
    i = pl.program_id(1)
    o = BF16_ROWS
    rows = tm // nparts
    base = part * (rows + 2 * o)
    head_ok = (i > 0) if part == 0 else True
    tail_ok = (i < t_total // tm - 1) if part == nparts - 1 else True
    u_scr[base:base + o] = jnp.where(head_ok, u[0:o], 0.0)
    u_scr[base + o:base + o + rows] = u[o:o + rows]
    u_scr[base + o + rows:base + 2 * o + rows] = jnp.where(tail_ok, u[o + rows:], 0.0)
    s = base + o
    return (u_scr[s - 1:s - 1 + rows] * cw[0:1] + u_scr[s:s + rows] * cw[1:2] + u_scr[s + 1:s + 1 + rows] * cw[2:3]
            + cb)


def _proj_conv_kernel(x_ref, xp_ref, xn_ref, g_ref, mod_ref, w_ref, b_ref, cw_ref, cb_ref, o_ref, h_scr, u_scr,
                      *, d, tm, t_total, tn):
    m = mod_ref[0]
    _fill_h(x_ref, xp_ref, xn_ref, g_ref[...], m[:, 0:d], m[:, d:2 * d], h_scr, tm)
    nparts = FFN_ROW_PARTS
    rows = tm // nparts
    n = w_ref.shape[1]

    def project(j):
        cols = slice(j * tn, (j + 1) * tn)
        return [jnp.dot(h_scr[p * rows:(p + 1) * rows + 2 * BF16_ROWS], w_ref[:, cols],
                        preferred_element_type=F32) + b_ref[:, cols] for p in range(nparts)]

    nxt_u = project(0)
    for j in range(n // tn):
        cols = slice(j * tn, (j + 1) * tn)
        us = nxt_u
        if (j + 1) * tn < n:
            nxt_u = project(j + 1)
        q, lane0 = divmod(j * tn, d)
        for p in range(nparts):
            o_ref[q, 0, p * rows:(p + 1) * rows, lane0:lane0 + tn] = _conv3_rows(
                us[p], u_scr.at[j % 2], cw_ref[:, cols], cb_ref[:, cols], tm, t_total, p, nparts).astype(o_ref.dtype)


def _proj_conv(x, g, mod, w, bias, cw, cb):
    b, t, d = x.shape
    n = w.shape[1]
    tm = _row_tile(t, 512)
    tn = d // 2
    prev, nxt = _halo_specs(tm, t, d, 2)
    resident = lambda *shape: pl.BlockSpec(shape, lambda bi, i: (0,) * len(shape), pipeline_mode=pl.Buffered(1))
    return pl.pallas_call(
        functools.partial(_proj_conv_kernel, d=d, tm=tm, t_total=t, tn=tn),
        grid=(b, t // tm),
        in_specs=[pl.BlockSpec((1, tm, d), lambda bi, i: (bi, i, 0)), prev, nxt,
                  pl.BlockSpec((1, d), lambda bi, i: (0, 0)),
                  pl.BlockSpec((1, 1, mod.shape[-1]), lambda bi, i: (bi, 0, 0)),
                  resident(d, n), resident(1, n), resident(3, n), resident(1, n)],
        out_specs=pl.BlockSpec((n // d, 1, tm, d), lambda bi, i: (0, bi, i, 0)),
        out_shape=jax.ShapeDtypeStruct((n // d, b, t, d), BF16),
        scratch_shapes=[pltpu.VMEM((tm + 2 * BF16_ROWS, d), BF16),
                        pltpu.VMEM((2, tm + 2 * BF16_ROWS * FFN_ROW_PARTS, tn), F32)],
        compiler_params=_cparams("parallel", "parallel"),
        name="hyena_in_proj",
    )(x, x, x, g.reshape(1, d), mod, w, bias.reshape(1, n), cw, cb.reshape(1, n))


def _ffn_kernel(x_ref, xp_ref, xn_ref, g_ref, mod_ref, wup_ref, cw_ref, cb_ref, wd_ref,
                o_ref, h_scr, ug_scr, uv_scr, acc_scr, *, d, tm, t_total, f, tf):
    m = mod_ref[0]
    _fill_h(x_ref, xp_ref, xn_ref, g_ref[...], m[:, 3 * d:4 * d], m[:, 4 * d:5 * d], h_scr, tm)
    nparts = FFN_ROW_PARTS
    rows = tm // nparts
    nf = f // tf

    def up_proj(j):
        hs = [h_scr[p * rows:(p + 1) * rows + 2 * BF16_ROWS] for p in range(nparts)]
        return ([jnp.dot(h, wup_ref[:, j * tf:(j + 1) * tf], preferred_element_type=F32) for h in hs],
                [jnp.dot(h, wup_ref[:, f + j * tf:f + (j + 1) * tf], preferred_element_type=F32) for h in hs])

    nxt_up = up_proj(0)
    for j in range(nf):
        gs = slice(j * tf, (j + 1) * tf)
        vs = slice(f + j * tf, f + (j + 1) * tf)
        ug, uv = nxt_up
        if j + 1 < nf:
            nxt_up = up_proj(j + 1)
        for p in range(nparts):
            gate = _conv3_rows(ug[p], ug_scr.at[j % 2], cw_ref[:, gs], cb_ref[:, gs], tm, t_total, p, nparts)
            val = _conv3_rows(uv[p], uv_scr.at[j % 2], cw_ref[:, vs], cb_ref[:, vs], tm, t_total, p, nparts)
            act = (gate * jax.nn.sigmoid(gate)) * val
            down = jnp.dot(act.astype(BF16), wd_ref[gs, :], preferred_element_type=F32)
            part = slice(p * rows, (p + 1) * rows)
            if j == 0:
                acc_scr[part] = down
            else:
                acc_scr[part] += down
    o_ref[0] = x_ref[0] + m[:, 5 * d:6 * d] * acc_scr[...]


def _conv_ffn(x, g, mod, w_up, cw, cb, w_down):
    b, t, d = x.shape
    f = w_down.shape[0]
    tm = _row_tile(t, 1024)
    tf = 256 if f % 256 == 0 else 128
    prev, nxt = _halo_specs(tm, t, d, 2)
    resident = lambda *shape: pl.BlockSpec(shape, lambda bi, i: (0,) * len(shape), pipeline_mode=pl.Buffered(1))
    u_rows = tm + 2 * BF16_ROWS * FFN_ROW_PARTS
    return pl.pallas_call(
        functools.partial(_ffn_kernel, d=d, tm=tm, t_total=t, f=f, tf=tf),
        grid=(b, t // tm),
        in_specs=[pl.BlockSpec((1, tm, d), lambda bi, i: (bi, i, 0)), prev, nxt,
                  pl.BlockSpec((1, d), lambda bi, i: (0, 0)),
                  pl.BlockSpec((1, 1, mod.shape[-1]), lambda bi, i: (bi, 0, 0)),
                  resident(d, 2 * f), resident(3, 2 * f), resident(1, 2 * f), resident(f, d)],
        out_specs=pl.BlockSpec((1, tm, d), lambda bi, i: (bi, i, 0)),
        out_shape=jax.ShapeDtypeStruct((b, t, d), F32),
        scratch_shapes=[pltpu.VMEM((tm + 2 * BF16_ROWS, d), BF16),
                        pltpu.VMEM((2, u_rows, tf), F32),
                        pltpu.VMEM((2, u_rows, tf), F32),
                        pltpu.VMEM((tm, d), F32)],
        compiler_params=_cparams("parallel", "parallel"),
        name="conv_ffn",
    )(x, x, x, g.reshape(1, d), mod, w_up, cw, cb.reshape(1, 2 * f), w_down)


def _out_res_kernel(y_ref, x_ref, mod_ref, w_ref, b_ref, o_ref, *, d):
    o = _dot(y_ref[0], w_ref[...]) + b_ref[...]
    o_ref[0] = x_ref[0] + mod_ref[0][:, 2 * d:3 * d] * o


def _out_res(y, x, mod, w, bias):
    b, t, d = x.shape
    tm = _row_tile(t, 512)
    return pl.pallas_call(
        functools.partial(_out_res_kernel, d=d),
        grid=(b, t // tm),
        in_specs=[pl.BlockSpec((1, tm, d), lambda bi, i: (bi, i, 0)),
                  pl.BlockSpec((1, tm, d), lambda bi, i: (bi, i, 0)),
                  pl.BlockSpec((1, 1, mod.shape[-1]), lambda bi, i: (bi, 0, 0)),
                  pl.BlockSpec((d, d), lambda bi, i: (0, 0)),
                  pl.BlockSpec((1, d), lambda bi, i: (0, 0))],
        out_specs=pl.BlockSpec((1, tm, d), lambda bi, i: (bi, i, 0)),
        out_shape=jax.ShapeDtypeStruct((b, t, d), F32),
        compiler_params=_cparams("parallel", "parallel"),
        name="out_proj_residual",
    )(y, x, mod, w, bias.reshape(1, d))


def _even_prep_kernel(x_ref, xp_ref, xn_ref, gn_ref, mod_ref, win_ref, mup_ref, mun_ref, w0_ref, w2_ref, a0_ref,
                      a2_ref, g2_ref, kk_ref, ka_ref, qn_ref, kn_ref, cos_ref, sin_ref, e_ref,
                      r_ref, v_ref, kkn_ref, g_ref, lw_ref, a_ref, kd_ref, q_ref, katt_ref, vatt_ref,
                      h_scr, p_scr, *, tm, d, c, a_in, rope):
    i = pl.program_id(1)
    nt = pl.num_programs(1)
    m = mod_ref[0]
    _fill_h(x_ref, xp_ref, xn_ref, gn_ref[...], m[:, 0:d], m[:, d:2 * d], h_scr, tm)
    p_scr[...] = jnp.dot(h_scr[...], win_ref[...], preferred_element_type=F32)
    o = BF16_ROWS
    pa = p_scr[o:o + tm, 0:a_in]
    rows = lax.broadcasted_iota(jnp.int32, (tm, 1), 0)
    prev = jnp.where((rows == 0) & (i == 0), 0.0, p_scr[o - 1:o - 1 + tm, 0:a_in])
    nxt = jnp.where((rows == tm - 1) & (i == nt - 1), 0.0, p_scr[o + 1:o + 1 + tm, 0:a_in])
    za = pa + mup_ref[...] * (prev - pa) + mun_ref[...] * (nxt - pa)

    r = za[:, 0:c]
    k = za[:, c:2 * c]
    v = za[:, 2 * c:3 * c]
    wd = jnp.tanh(za[:, 3 * c:3 * c + 2 * DECAY_LORA])
    ad = za[:, 3 * c + 2 * DECAY_LORA:3 * c + 2 * DECAY_LORA + 2 * ICLR_LORA]
    gd = za[:, 3 * c + 2 * DECAY_LORA + 2 * ICLR_LORA:a_in]
    r_ref[0] = r
    v_ref[0] = v
    g_ref[0] = _dot(jax.nn.sigmoid(gd), g2_ref[...])
    kkv = k * kk_ref[...]
    ss = _segsum(kkv * kkv, e_ref)
    kkn_ref[0] = kkv / jnp.maximum(jnp.sqrt(ss), 1e-12)
    for dd in range(2):
        y = w0_ref[dd:dd + 1] + _dot(wd, w2_ref[dd])
        lw_ref[dd, 0] = -math.exp(-0.5) * jax.nn.sigmoid(y)
        a = jax.nn.sigmoid(a0_ref[dd:dd + 1] + _dot(ad, a2_ref[dd]))
        a_ref[dd, 0] = a
        kd_ref[dd, 0] = k * (1.0 + (a - 1.0) * ka_ref[...])

    pb = p_scr[o:o + tm, a_in:]
    nq = q_ref.shape[-1]
    nk = katt_ref.shape[-1]
    q = pb[:, 0:nq]
    kat = pb[:, nq:nq + nk]
    vatt_ref[0] = pb[:, nq + nk:]
    half = HEAD_DIM // 2

    def norm_rope(x, gain, e):
        w = x.shape[-1]
        ms = _segsum(x * x, e) * (1.0 / HEAD_DIM)
        xn = x * lax.rsqrt(ms + NORM_EPS) * gain
        if not rope:
            return xn
        lane = lax.broadcasted_iota(jnp.int32, (1, w), 1)
        first = (lane % HEAD_DIM) < half
        swapped = jnp.where(first, pltpu.roll(xn, w - half, 1), pltpu.roll(xn, half, 1))
        reps = w // LANES
        cos = jnp.concatenate([cos_ref[...]] * reps, axis=1) if reps > 1 else cos_ref[...]
        sin = jnp.concatenate([sin_ref[...]] * reps, axis=1) if reps > 1 else sin_ref[...]
        return xn * cos + swapped * sin

    q_ref[0] = norm_rope(q, qn_ref[...], e_ref[...])
    katt_ref[0] = norm_rope(kat, kn_ref[...], e_ref[0:nk, 0:nk])


def _even_prep(x, g, mod, ep, tabs, rope):
    b, t, d = x.shape
    n = ep['w_in'].shape[1]
    c = ep['k_k'].shape[-1]
    a_in = ep['mu_prev'].shape[-1]
    nq = ep['nq']
    nk = ep['nk']
    tm = _row_tile(t, 256)
    prev, nxt = _halo_specs(tm, t, d, 2)
    full = lambda *shape: pl.BlockSpec(shape, lambda bi, i: (0,) * len(shape))
    tok = lambda w: pl.BlockSpec((1, tm, w), lambda bi, i: (bi, i, 0))
    tok2 = lambda w: pl.BlockSpec((2, 1, tm, w), lambda bi, i: (0, bi, i, 0))
    sd = lambda *shape: jax.ShapeDtypeStruct(shape, F32)
    return pl.pallas_call(
        functools.partial(_even_prep_kernel, tm=tm, d=d, c=c, a_in=a_in, rope=rope),
        grid=(b, t // tm),
        in_specs=[tok(d), prev, nxt, full(1, d), pl.BlockSpec((1, 1, mod.shape[-1]), lambda bi, i: (bi, 0, 0)),
                  full(d, n), full(1, a_in), full(1, a_in), full(2, c), full(2, 2 * DECAY_LORA, c),
                  full(2, c), full(2, 2 * ICLR_LORA, c), full(GATE_LORA, c), full(1, c), full(1, c),
                  full(1, nq), full(1, nk),
                  pl.BlockSpec((tm, LANES), lambda bi, i: (i, 0)), pl.BlockSpec((tm, LANES), lambda bi, i: (i, 0)),
                  full(nq, nq)],
        out_specs=[tok(c), tok(c), tok(c), tok(c), tok2(c), tok2(c), tok2(c), tok(nq), tok(nk), tok(nk)],
        out_shape=[sd(b, t, c), sd(b, t, c), sd(b, t, c), sd(b, t, c), sd(2, b, t, c), sd(2, b, t, c),
                   sd(2, b, t, c), sd(b, t, nq), sd(b, t, nk), sd(b, t, nk)],
        scratch_shapes=[pltpu.VMEM((tm + 2 * BF16_ROWS, d), BF16), pltpu.VMEM((tm + 2 * BF16_ROWS, n), F32)],
        compiler_params=_cparams("parallel", "parallel"),
        name="even_in_proj_prep",
    )(x, x, x, g.reshape(1, d), mod, ep['w_in'], ep['mu_prev'], ep['mu_next'], ep['w0'], ep['w2pad'], ep['a0'], ep['a2pad'], ep['g2'], ep['k_k'],
      ep['k_a'], ep['q_norm_t'], ep['k_norm_t'], tabs['cos'][:t], tabs['sin'][:t], tabs['e'])


HEADS_PER_GROUP = 2
GROUP_LANES = HEADS_PER_GROUP * HEAD_DIM


def _block_diag(x, bmask):
    return jnp.where(bmask, jnp.concatenate([x] * HEADS_PER_GROUP, axis=0), jnp.zeros((), x.dtype))


def _head_mm(lhs, rhs, bmask, nt=False):
    dn = (((1,), (1,)), ((), ())) if nt else (((1,), (0,)), ((), ()))
    return lax.dot_general(lhs.astype(BF16), _block_diag(rhs.astype(BF16), bmask), dn, preferred_element_type=F32)


def _head_mm_tn(lhs, rhs, lane_head):
    full = lax.dot_general(lhs.astype(BF16), rhs.astype(BF16), (((0,), (0,)), ((), ())), preferred_element_type=F32)
    out = jnp.where(lane_head == 0, full[0:HEAD_DIM], 0.0)
    for h in range(1, HEADS_PER_GROUP):
        out = out + jnp.where(lane_head == h, full[h * HEAD_DIM:(h + 1) * HEAD_DIM], 0.0)
    return out


def _group_masks():
    r = lax.broadcasted_iota(jnp.int32, (GROUP_LANES, GROUP_LANES), 0)
    c = lax.broadcasted_iota(jnp.int32, (GROUP_LANES, GROUP_LANES), 1)
    bmask = (r // HEAD_DIM) == (c // HEAD_DIM)
    lane_head = lax.broadcasted_iota(jnp.int32, (1, GROUP_LANES), 1) // HEAD_DIM
    return bmask, lane_head


def _rwkv_prep_kernel(r_ref, v_ref, kk_ref, lw_ref, a_ref, kd_ref, r2_ref, yl_ref, t_ref, z_ref, *, cs):
    sgn = 1 - 2 * pl.program_id(0)
    row = lax.broadcasted_iota(jnp.int32, (cs, cs), 0)
    col = lax.broadcasted_iota(jnp.int32, (cs, cs), 1)
    tri = (((row - col) * sgn) >= 0).astype(F32)
    bmask, lane_head = _group_masks()
    t_idx = lax.broadcasted_iota(jnp.int32, (cs, GROUP_LANES), 0)
    s_idx = lax.broadcasted_iota(jnp.int32, (cs, GROUP_LANES), 1) % HEAD_DIM
    diff = (t_idx - s_idx) * sgn
    strict = diff > 0
    incl = diff >= 0
    eye = (diff == 0).astype(F32)
    base = min(RWKV_INV_BASE, cs)
    same_base = (t_idx // base) == (s_idx // base)
    groups = r_ref.shape[-1] // GROUP_LANES

    ch = []
    for c0 in range(0, r_ref.shape[1], cs):
        rows = slice(c0, c0 + cs)
        r, v, kk = r_ref[0, rows], v_ref[0, rows], kk_ref[0, rows]
        lw, a, kd = lw_ref[0, 0, rows], a_ref[0, 0, rows], kd_ref[0, 0, rows]
        g = _dot_hi(tri, lw)
        gp = g - lw
        gref = g[cs // 2:cs // 2 + 1]
        gend = jnp.sum(lw, axis=0, keepdims=True)
        bvec = kk * a
        full = {'a_t': -kk * jnp.exp(gp - gref), 'a_0': -kk * jnp.exp(gp),
                'b_t': bvec * jnp.exp(gref - g), 'b_e': bvec * jnp.exp(gend - g),
                'k_t': kd * jnp.exp(gref - g), 'k_e': kd * jnp.exp(gend - g),
                'r_t': r * jnp.exp(g - gref), 'r_0': r * jnp.exp(g), 'v': v,
                'wend': jnp.broadcast_to(jnp.exp(gend), (cs, v.shape[-1]))}
        for gi in range(groups):
            sl = slice(gi * GROUP_LANES, (gi + 1) * GROUP_LANES)
            c = {k: x[:, sl] for k, x in full.items()}
            c['rows'], c['sl'] = rows, sl
            ch.append(c)

    mm = lambda x, y, nt=False: [_head_mm(p_, q_, bmask, nt) for p_, q_ in zip(x, y)]
    get = lambda k: [c[k] for c in ch]
    ar = [jnp.concatenate([c['a_t'], c['r_t']], axis=0) for c in ch]
    gb = mm(ar, get('b_t'), True)
    gk = mm(ar, get('k_t'), True)
    nmat = [jnp.where(strict, x[0:cs], 0.0) for x in gb]
    g_rb = [jnp.where(incl, x[cs:], 0.0) for x in gb]
    g_k = [jnp.concatenate([jnp.where(strict, x[0:cs], 0.0), jnp.where(incl, x[cs:], 0.0)], axis=0) for x in gk]
    npow = [jnp.where(same_base, x, 0.0) for x in nmat]
    p = [eye + x for x in npow]
    for _ in range(int(math.log2(base)) - 1):
        npow = mm(npow, npow)
        p = [x + y for x, y in zip(p, mm(npow, p))]
    m = base
    while m < cs:
        off = ((t_idx // m) != (s_idx // m)) & ((t_idx // (2 * m)) == (s_idx // (2 * m)))
        q = mm([jnp.where(off, x, 0.0) for x in nmat], p)
        p = [x + y for x, y in zip(p, mm(p, q))]
        m *= 2
    vg = get('v')
    a2 = mm(p, get('a_0'))
    kv = mm(g_k, vg)
    u_v = mm(p, [x[0:cs] for x in kv])
    r2 = mm(g_rb, a2)
    yl1 = mm(g_rb, u_v)
    yl2 = [x[cs:] for x in kv]
    tt = [_head_mm_tn(x, c['b_e'], lane_head) for x, c in zip(a2, ch)]
    zz = [_head_mm_tn(jnp.concatenate([u, c['v']], axis=0), jnp.concatenate([c['b_e'], c['k_e']], axis=0),
                      lane_head) for u, c in zip(u_v, ch)]
    for i, c in enumerate(ch):
        rows, sl = c['rows'], c['sl']
        r2_ref[0, 0, rows, sl] = (c['r_0'] + r2[i]).astype(BF16)
        yl_ref[0, 0, rows, sl] = yl1[i] + yl2[i]
        t_ref[0, 0, rows, sl] = (eye * c['wend'] + tt[i]).astype(BF16)
        z_ref[0, 0, rows, sl] = zz[i]


def _rwkv_scan_kernel(h0_ref, r2f_ref, ylf_ref, tf_ref, zf_ref, r2b_ref, ylb_ref, tb_ref, zb_ref,
                      yf_ref, yb_ref, ht_ref, s_scr):
    ci = pl.program_id(0)

    @pl.when(ci == 0)
    def _():
        s_scr[...] = h0_ref[...]

    bmask, _ = _group_masks()
    ins = ((r2f_ref, ylf_ref, tf_ref, zf_ref, yf_ref), (r2b_ref, ylb_ref, tb_ref, zb_ref, yb_ref))
    for d, (r2_ref, yl_ref, t_ref, z_ref, y_ref) in enumerate(ins):
        for b in range(s_scr.shape[1]):
            for gi in range(s_scr.shape[-1] // GROUP_LANES):
                sl = slice(gi * GROUP_LANES, (gi + 1) * GROUP_LANES)
                s = s_scr[d, b, :, sl]
                y_ref[0, b, :, sl] = yl_ref[0, b, :, sl] + _head_mm(r2_ref[0, b, :, sl], s, bmask, nt=True)
                s_scr[d, b, :, sl] = _head_mm(s, t_ref[0, b, :, sl], bmask) + z_ref[0, b, :, sl]

    @pl.when(ci == pl.num_programs(0) - 1)
    def _():
        ht_ref[...] = s_scr[...]


def _rwkv(r, v, kk, lw, a, kd, h0):
    b, t, c = r.shape
    cs = min(RWKV_CHUNK, t)
    assert cs == HEAD_DIM and c % GROUP_LANES == 0
    nc = t // cs
    rows = cs * math.gcd(nc, RWKV_PREP_CHUNKS)
    tok = pl.BlockSpec((1, rows, c), lambda d, bi, ci: (bi, ci, 0))
    tok2 = pl.BlockSpec((1, 1, rows, c), lambda d, bi, ci: (d, bi, ci, 0))
    wide = jax.ShapeDtypeStruct((2, b, t, c), F32)
    mm_operand = jax.ShapeDtypeStruct((2, b, t, c), BF16)
    r2, yl, tt, zz = pl.pallas_call(
        functools.partial(_rwkv_prep_kernel, cs=cs),
        grid=(2, b, t // rows),
        in_specs=[tok, tok, tok, tok2, tok2, tok2],
        out_specs=[tok2, tok2, tok2, tok2],
        out_shape=[mm_operand, wide, mm_operand, wide],
        compiler_params=_cparams("parallel", "parallel", "parallel"),
        name="rwkv7_chunk_prep",
    )(r, v, kk, lw, a, kd)

    fwd = pl.BlockSpec((1, b, cs, c), lambda ci: (0, 0, ci, 0))
    bwd = pl.BlockSpec((1, b, cs, c), lambda ci: (1, 0, nc - 1 - ci, 0))
    st = pl.BlockSpec((2, b, HEAD_DIM, c), lambda ci: (0, 0, 0, 0))
    yf, yb, ht = pl.pallas_call(
        _rwkv_scan_kernel,
        grid=(nc,),
        in_specs=[st, fwd, fwd, fwd, fwd, bwd, bwd, bwd, bwd],
        out_specs=[pl.BlockSpec((1, b, cs, c), lambda ci: (0, 0, ci, 0)),
                   pl.BlockSpec((1, b, cs, c), lambda ci: (0, 0, nc - 1 - ci, 0)), st],
        out_shape=[jax.ShapeDtypeStruct((1, b, t, c), F32), jax.ShapeDtypeStruct((1, b, t, c), F32),
                   jax.ShapeDtypeStruct((2, b, HEAD_DIM, c), F32)],
        scratch_shapes=[pltpu.VMEM((2, b, HEAD_DIM, c), F32)],
        compiler_params=_cparams("arbitrary"),
        name="rwkv7_state_scan",
    )(h0, r2, yl, tt, zz, r2, yl, tt, zz)
    return (yf.reshape(b, t, c), yb.reshape(b, t, c)), ht


def _attn_kernel(sink_ref, q_ref, kp_ref, kc_ref, kn_ref, vp_ref, vc_ref, vn_ref, kx_ref, vx_ref, o_ref,
                 *, local, group):
    i = pl.program_id(1)
    nb = pl.num_programs(1)
    scale = HEAD_DIM ** -0.5
    assert math.frexp(scale)[0] == 0.5
    q = q_ref[0] * scale
    if local:
        k_all = jnp.concatenate([kp_ref[0], kc_ref[0], kn_ref[0], kx_ref[0]], axis=0)
        v_all = jnp.concatenate([vp_ref[0], vc_ref[0], vn_ref[0], vx_ref[0]], axis=0)
        qi = lax.broadcasted_iota(jnp.int32, (BLOCK, BLOCK), 0)
        kj = lax.broadcasted_iota(jnp.int32, (BLOCK, BLOCK), 1)
        assert WINDOW == BLOCK
        valid_prev = (kj >= qi) & (i > 0)
        valid_next = (kj <= qi) & (i < nb - 1)
    else:
        k_all = kx_ref[0]
        v_all = vx_ref[0]
    heads = range(q.shape[-1] // HEAD_DIM)
    kv = lambda x, h: x[:, (h // group) * HEAD_DIM:(h // group + 1) * HEAD_DIM]
    s = [_dot_nt(q[:, h * HEAD_DIM:(h + 1) * HEAD_DIM], kv(k_all, h)) for h in heads]
    if local:
        s = [jnp.concatenate([jnp.where(valid_prev, x[:, 0:BLOCK], MASK_VALUE), x[:, BLOCK:2 * BLOCK],
                              jnp.where(valid_next, x[:, 2 * BLOCK:3 * BLOCK], MASK_VALUE), x[:, 3 * BLOCK:]],
                             axis=1) for x in s]
    sink = [sink_ref[h] for h in heads]
    m = [jnp.maximum(jnp.max(x, axis=-1, keepdims=True), sk) for x, sk in zip(s, sink)]
    e = [jnp.exp(x - y) for x, y in zip(s, m)]
    den = [jnp.sum(x, axis=-1, keepdims=True) + jnp.exp(sk - y) for x, y, sk in zip(e, m, sink)]
    outs = [_dot(x, kv(v_all, h)) / dn for x, h, dn in zip(e, heads, den)]
    o_ref[0] = jnp.concatenate(outs, axis=1)


def _attention(q, k, v, kx, vx, sink, local):
    b, t, nq = q.shape
    nk = k.shape[-1]
    nb = t // BLOCK
    group = (nq // HEAD_DIM) // (nk // HEAD_DIM)
    lx = kx.shape[1]
    kv = lambda f: pl.BlockSpec((1, BLOCK, nk), f)
    pf = lambda bi, i: (bi, jnp.maximum(i - 1, 0), 0)
    cf = lambda bi, i: (bi, i, 0)
    nf = lambda bi, i: (bi, jnp.minimum(i + 1, nb - 1), 0)
    ctx = pl.BlockSpec((1, lx, nk), lambda bi, i: (bi, 0, 0))
    return pl.pallas_call(
        functools.partial(_attn_kernel, local=local, group=group),
        grid=(b, nb),
        in_specs=[pl.BlockSpec(memory_space=pltpu.SMEM),
                  pl.BlockSpec((1, BLOCK, nq), cf), kv(pf), kv(cf), kv(nf), kv(pf), kv(cf), kv(nf), ctx, ctx],
        out_specs=pl.BlockSpec((1, BLOCK, nq), cf),
        out_shape=jax.ShapeDtypeStruct((b, t, nq), F32),
        compiler_params=_cparams("parallel", "parallel"),
        name="window_attention" if local else "context_attention",
    )(sink, q, k, k, k, v, v, v, kx, vx)


def _even_out_kernel(yf_ref, yb_ref, r_ref, v_ref, g_ref, kd_ref, batt_ref, x_ref, mod_ref, lnw_ref, lnb_ref, rk_ref, e_ref,
                     w_ref, o_ref, *, d, c):
    y = yf_ref[0] + yb_ref[0]
    inv = 1.0 / HEAD_DIM
    mu = _segsum(y, e_ref) * inv
    yc = y - mu
    var = _segsum(yc * yc, e_ref) * inv
    yn = yc * lax.rsqrt(var + RWKV_GN_EPS) * lnw_ref[...] + lnb_ref[...]
    bonus = _segsum(r_ref[0] * (kd_ref[0, 0] + kd_ref[1, 0]) * rk_ref[...], e_ref)
    a_out = (yn + bonus * v_ref[0]) * g_ref[0]
    o = _dot(a_out, w_ref[0:c]) + _dot(batt_ref[0], w_ref[c:])
    o_ref[0] = x_ref[0] + mod_ref[0][:, 2 * d:3 * d] * o


def _even_out(y, r, v, g, kd, batt, x, mod, ep, tabs):
    b, t, d = x.shape
    c = r.shape[-1]
    nq = batt.shape[-1]
    tm = _row_tile(t, 512)
    tok = lambda w: pl.BlockSpec((1, tm, w), lambda bi, i: (bi, i, 0))
    tok2 = lambda w: pl.BlockSpec((2, 1, tm, w), lambda bi, i: (0, bi, i, 0))
    full = lambda *shape: pl.BlockSpec(shape, lambda bi, i: (0,) * len(shape))
    return pl.pallas_call(
        functools.partial(_even_out_kernel, d=d, c=c),
        grid=(b, t // tm),
        in_specs=[tok(c), tok(c), tok(c), tok(c), tok(c), tok2(c), tok(nq), tok(d),
                  pl.BlockSpec((1, 1, mod.shape[-1]), lambda bi, i: (bi, 0, 0)),
                  full(1, c), full(1, c), full(1, c), full(c, c), full(c + nq, d)],
        out_specs=tok(d),
        out_shape=jax.ShapeDtypeStruct((b, t, d), F32),
        compiler_params=_cparams("parallel", "parallel"),
        name="even_out_proj",
    )(y[0], y[1], r, v, g, kd, batt, x, mod, ep['ln_w'], ep['ln_b'], ep['r_k'], tabs['e'], ep['w_out'])


def _filter_kernel(z_ref, t_ref, w1_ref, b1_ref, w2_ref, b2_ref, w3_ref, b3_ref, fr_ref, wo_ref, dl_ref, o_ref, *, d):
    fr = fr_ref[...]
    h = jnp.sin(fr * (_dot_hi(z_ref[...], w1_ref[...]) + b1_ref[...]))
    h = jnp.sin(fr * (_dot_hi(h, w2_ref[...]) + b2_ref[...]))
    h = jnp.sin(fr * (_dot_hi(h, w3_ref[...]) + b3_ref[...]))
    filt = _dot_hi(h, wo_ref[...])
    modu = jnp.exp(-t_ref[...] * dl_ref[...]) + HY_MOD_SHIFT
    for q in range(o_ref.shape[0]):
        o_ref[q] = filt[:, q * d:(q + 1) * d] * modu


def _hyena_filters(n, op, d):
    t = np.linspace(0.0, 1.0, n, dtype=np.float32)[:, None]
    ang = (2.0 * math.pi * np.arange(n, dtype=np.float32)[:, None] / np.float32(n)).astype(np.float32)
    f = np.linspace(1e-4, HY_BANDS - 1, HY_BANDS, dtype=np.float32)[None, :]
    zfeat = jnp.concatenate([jnp.asarray(t), jnp.cos(jnp.asarray(f * ang)), -jnp.sin(jnp.asarray(f * ang))], axis=-1)
    emb_pad = op['f_w1'].shape[0]
    zfeat = jnp.pad(zfeat, ((0, 0), (0, emb_pad - HY_EMB)))
    deltas = np.abs(np.linspace(math.log(HY_TARGET) / HY_SLOW_PCT, math.log(HY_TARGET) / HY_FAST_PCT, d,
                                dtype=np.float32))[None, :]
    tn = min(n, 256)
    nq = 2 * HY_ORDER
    hf = op['f_w2'].shape[0]
    full = lambda *shape: pl.BlockSpec(shape, lambda i: (0,) * len(shape))
    return pl.pallas_call(
        functools.partial(_filter_kernel, d=d),
        grid=(n // tn,),
        in_specs=[pl.BlockSpec((tn, emb_pad), lambda i: (i, 0)), pl.BlockSpec((tn, 1), lambda i: (i, 0)),
                  full(emb_pad, hf), full(1, hf), full(hf, hf), full(1, hf), full(hf, hf), full(1, hf), full(1, hf),
                  full(hf, nq * d), full(1, d)],
        out_specs=pl.BlockSpec((nq, tn, d), lambda i: (0, i, 0)),
        out_shape=jax.ShapeDtypeStruct((nq, n, d), F32),
        compiler_params=_cparams("parallel"),
        name="hyena_filter",
    )(zfeat, jnp.asarray(t), op['f_w1'], op['f_b1'], op['f_w2'], op['f_b2'], op['f_w3'], op['f_b3'], op['f_freq'],
      op['f_out'], jnp.asarray(deltas))


def _dft(n, rows, cols, sign=-1.0):
    k = np.arange(rows, dtype=np.float64)[:, None]
    m = np.arange(cols, dtype=np.float64)[None, :]
    ang = sign * 2.0 * np.pi * ((k * m) % n) / n
    return np.cos(ang), np.sin(ang)


def _stack(re, im):
    return jnp.asarray(np.concatenate([re, im], axis=0).astype(np.float32)).astype(BF16)


def _fft_tables(n_seq):
    n = 2 * n_seq
    n1 = n // FFT_N2
    f1r, f1i = _dft(n1, n1, n1 // 2)
    h1r, h1i = _dft(n1, n1 // 2, n1, sign=1.0)
    k1 = np.arange(n1, dtype=np.float64)[:, None, None]
    k2 = np.arange(FFT_N2, dtype=np.float64)[None, :, None]
    j2 = np.arange(FFT_N2, dtype=np.float64)[None, None, :]
    ang = -2.0 * np.pi * (((k2 * j2 * n1) + k1 * j2) % n) / n
    gr, gi = np.cos(ang), np.sin(ang)
    g_fwd = np.concatenate([gr, gi], axis=1)
    g_inv = np.concatenate([np.swapaxes(gr, 1, 2), np.swapaxes(gi, 1, 2)], axis=1)
    eye = np.eye(SUBLANES)
    kron = lambda m: np.kron(m, eye)
    return {'f1': _stack(kron(f1r), kron(f1i)), 'h1': _stack(kron(h1r), kron(h1i)),
            'g_fwd': jnp.asarray(g_fwd.astype(np.float32)).astype(BF16),
            'g_inv': jnp.asarray(g_inv.astype(np.float32)).astype(BF16), 'n1': n1}


def _dense_tables(n_seq):
    n = 2 * n_seq
    fr, fi = _dft(n, n, n_seq)
    hr, hi = _dft(n, n_seq, n, sign=1.0)
    return {'f': _stack(fr, fi), 'h': _stack(hr, hi)}


def _fft_a_kernel(f_ref, zr_ref, zi_ref, ar_ref, ai_ref, *, n1, cplx):
    f = f_ref[...]
    half, tj, d = zr_ref.shape[1:]
    m = n1 * SUBLANES
    res_r, res_i = [], []
    zr_all = zr_ref[0].astype(F32)
    zi_all = zi_ref[0].astype(F32) if cplx else None
    for s0 in range(0, tj, SUBLANES):
        js = slice(s0, s0 + SUBLANES)
        zr = zr_all[:, js, :].reshape(half * SUBLANES, d)
        p = jnp.dot(f, zr.astype(BF16), preferred_element_type=F32)
        if cplx:
            zi = zi_all[:, js, :].reshape(half * SUBLANES, d)
            q = jnp.dot(f, zi.astype(BF16), preferred_element_type=F32)
            res_r.append((p[0:m] - q[m:]).reshape(n1, SUBLANES, d))
            res_i.append((p[m:] + q[0:m]).reshape(n1, SUBLANES, d))
        else:
            res_r.append(p[0:m].reshape(n1, SUBLANES, d))
            res_i.append(p[m:].reshape(n1, SUBLANES, d))
    ar_ref[0] = jnp.concatenate(res_r, axis=1).astype(BF16)
    ai_ref[0] = jnp.concatenate(res_i, axis=1).astype(BF16)


def _fft_a(u5, q, tabs, cplx):
    _, s, half, n2, d = u5.shape
    n1 = tabs['n1']
    assert half == n1 // 2 and n2 == FFT_N2
    sp = s // 2 if cplx else s
    tj = 2 * SUBLANES
    zi_map = (lambda si, j: (q, si + sp, 0, j, 0)) if cplx else (lambda si, j: (q, si, 0, j, 0))
    out = pl.BlockSpec((1, n1, tj, d), lambda si, j: (si, 0, j, 0))
    return pl.pallas_call(
        functools.partial(_fft_a_kernel, n1=n1, cplx=cplx),
        grid=(sp, n2 // tj),
        in_specs=[pl.BlockSpec((2 * n1 * SUBLANES, half * SUBLANES), lambda si, j: (0, 0)),
                  pl.BlockSpec((None, 1, half, tj, d), lambda si, j: (q, si, 0, j, 0)),
                  pl.BlockSpec((None, 1, half, tj, d), zi_map)],
        out_specs=[out, out],
        out_shape=[jax.ShapeDtypeStruct((sp, n1, n2, d), BF16)] * 2,
        compiler_params=_cparams("parallel", "parallel"),
        name="fft_stage1",
    )(tabs['f1'], u5, u5)


def _cplx_mm(s, xr, xi, conj):
    p = jnp.dot(s, xr.astype(BF16), preferred_element_type=F32)
    q = jnp.dot(s, xi.astype(BF16), preferred_element_type=F32)
    m = s.shape[0] // 2
    if conj:
        return p[0:m] + q[m:], q[0:m] - p[m:]
    return p[0:m] - q[m:], p[m:] + q[0:m]


def _fft_b_kernel(gf_ref, gi_ref, ar_ref, ai_ref, kr_ref, ki_ref, dr_ref, di_ref):
    kr = kr_ref[0]
    ki = ki_ref[0]
    seqs = range(ar_ref.shape[0])
    c = [_cplx_mm(gf_ref[0], ar_ref[s, 0], ai_ref[s, 0], False) for s in seqs]
    e = [(cr * kr - ci * ki, cr * ki + ci * kr) for cr, ci in c]
    dd = [_cplx_mm(gi_ref[0], er, ei, True) for er, ei in e]
    for s, (dr, di) in zip(seqs, dd):
        dr_ref[s, 0] = dr.astype(BF16)
        di_ref[s, 0] = di.astype(BF16)


def _fft_b(ar4, ai4, kr, ki, order, tabs):
    sp, n1, _, d = ar4.shape
    ns = 2 if sp % 2 == 0 else 1
    blk = pl.BlockSpec((ns, 1, FFT_N2, d), lambda k1, si: (si, k1, 0, 0))
    tab = pl.BlockSpec((1, 2 * FFT_N2, FFT_N2), lambda k1, si: (k1, 0, 0))
    kb = pl.BlockSpec((None, 1, FFT_N2, d), lambda k1, si: (order, k1, 0, 0))
    return pl.pallas_call(
        _fft_b_kernel,
        grid=(n1, sp // ns),
        in_specs=[tab, tab, blk, blk, kb, kb],
        out_specs=[blk, blk],
        out_shape=[jax.ShapeDtypeStruct((sp, n1, FFT_N2, d), BF16)] * 2,
        compiler_params=_cparams("parallel", "arbitrary"),
        name="fft_stage2_filter",
    )(tabs['g_fwd'], tabs['g_inv'], ar4, ai4, kr, ki)


def _fft_c_kernel(h_ref, dr_ref, di_ref, u_ref, x_ref, sk_ref, *rest, chain):
    if chain:
        f_ref, o_ref, ar_ref, ai_ref = rest
    else:
        (o_ref,) = rest
    h = h_ref[...]
    n1, tj, d = dr_ref.shape[1:]
    half = n1 // 2
    sk = sk_ref[...]
    dr_all = dr_ref[0].astype(F32)
    di_all = di_ref[0].astype(F32)
    u_all = [u_ref[part, 0].astype(F32) for part in range(2)]
    x_all = [x_ref[part, 0].astype(F32) for part in range(2)]
    outs = [[], []]
    for s0 in range(0, tj, SUBLANES):
        js = slice(s0, s0 + SUBLANES)
        yr, yi = _cplx_mm(h, dr_all[:, js, :].reshape(n1 * SUBLANES, d),
                          di_all[:, js, :].reshape(n1 * SUBLANES, d), False)
        for part, y in enumerate((yr, yi)):
            outs[part].append(x_all[part][:, js, :] * (y.reshape(half, SUBLANES, d) + u_all[part][:, js, :] * sk))
    for part in range(2):
        o_ref[part, 0] = jnp.concatenate(outs[part], axis=1).astype(o_ref.dtype)
    if chain:
        f = f_ref[...]
        m = n1 * SUBLANES
        res_r, res_i = [], []
        for zr, zi in zip(outs[0], outs[1]):
            p = jnp.dot(f, zr.reshape(half * SUBLANES, d).astype(BF16), preferred_element_type=F32)
            q = jnp.dot(f, zi.reshape(half * SUBLANES, d).astype(BF16), preferred_element_type=F32)
            res_r.append((p[0:m] - q[m:]).reshape(n1, SUBLANES, d))
            res_i.append((p[m:] + q[0:m]).reshape(n1, SUBLANES, d))
        ar_ref[0] = jnp.concatenate(res_r, axis=1).astype(BF16)
        ai_ref[0] = jnp.concatenate(res_i, axis=1).astype(BF16)


def _fft_c(dr, di, u6, uq, x6, xq, skip, tabs, chain=False):
    sp, n1, n2, d = dr.shape
    half = n1 // 2
    tj = 2 * SUBLANES
    pair = lambda q: pl.BlockSpec((None, 2, 1, half, tj, d), lambda si, j: (q, 0, si, 0, j, 0))
    dblk = pl.BlockSpec((1, n1, tj, d), lambda si, j: (si, 0, j, 0))
    in_specs = [pl.BlockSpec((2 * half * SUBLANES, n1 * SUBLANES), lambda si, j: (0, 0)), dblk, dblk,
                pair(uq), pair(xq), pl.BlockSpec((1, d), lambda si, j: (0, 0))]
    out_specs = [pl.BlockSpec((2, 1, half, tj, d), lambda si, j: (0, si, 0, j, 0))]
    out_shape = [jax.ShapeDtypeStruct((2, sp, half, n2, d), BF16)]
    args = [tabs['h1'], dr, di, u6, x6, skip.reshape(1, d)]
    if chain:
        in_specs.append(pl.BlockSpec((2 * n1 * SUBLANES, half * SUBLANES), lambda si, j: (0, 0)))
        out_specs += [dblk, dblk]
        out_shape += [jax.ShapeDtypeStruct((sp, n1, n2, d), BF16)] * 2
        args.append(tabs['f1'])
    return pl.pallas_call(
        functools.partial(_fft_c_kernel, chain=chain),
        grid=(sp, n2 // tj),
        in_specs=in_specs,
        out_specs=out_specs,
        out_shape=out_shape,
        compiler_params=_cparams("parallel", "parallel"),
        name="fft_inverse_stage1_gate_chain" if chain else "fft_inverse_stage1_gate",
    )(*args)


def _spec_b_kernel(gf_ref, ar_ref, ai_ref, kr_ref, ki_ref, *, scale):
    fr, fi = _cplx_mm(gf_ref[0], ar_ref[0, 0], ai_ref[0, 0], False)
    gr, gi = _cplx_mm(gf_ref[0], ar_ref[1, 0], ai_ref[1, 0], False)
    kr_ref[0, 0] = (fr + gr) * scale
    ki_ref[0, 0] = (fi - gi) * scale


def _filter_spectrum_fft(filt, tabs, d):
    nq, n, _ = filt.shape
    n1 = tabs['n1']
    ar, ai = _fft_a(filt.reshape(1, nq, n1 // 2, FFT_N2, d), 0, tabs, cplx=False)
    orders = nq // 2
    ar5 = ar.reshape(orders, 2, n1, FFT_N2, d)
    ai5 = ai.reshape(orders, 2, n1, FFT_N2, d)
    blk = pl.BlockSpec((None, 2, 1, FFT_N2, d), lambda o, k1: (o, 0, k1, 0, 0))
    out = pl.BlockSpec((1, 1, FFT_N2, d), lambda o, k1: (o, k1, 0, 0))
    return pl.pallas_call(
        functools.partial(_spec_b_kernel, scale=1.0 / (2 * n)),
        grid=(orders, n1),
        in_specs=[pl.BlockSpec((1, 2 * FFT_N2, FFT_N2), lambda o, k1: (k1, 0, 0)), blk, blk],
        out_specs=[out, out],
        out_shape=[jax.ShapeDtypeStruct((orders, n1, FFT_N2, d), F32)] * 2,
        compiler_params=_cparams("parallel", "parallel"),
        name="filter_spectrum",
    )(tabs['g_fwd'], ar5, ai5)


def _long_conv_fft(u, uq, x, xq, skip, kr, ki, order, tabs, stage1=None, chain=False):
    _, b, n, d = u.shape
    half = tabs['n1'] // 2
    ar, ai = stage1 if stage1 is not None else _fft_a(u.reshape(u.shape[0], b, half, FFT_N2, d), uq, tabs, cplx=True)
    dr, di = _fft_b(ar, ai, kr, ki, order, tabs)
    six = lambda a: a.reshape(a.shape[0], 2, b // 2, half, FFT_N2, d)
    res = _fft_c(dr, di, six(u), uq, six(x), xq, skip, tabs, chain)
    if chain:
        return res[0].reshape(b, n, d), (res[1], res[2])
    return res[0].reshape(b, n, d)


def _dense_spec_kernel(f_ref, filt_ref, kr_ref, ki_ref, *, scale):
    f = f_ref[...]
    n = f.shape[0] // 2
    pf = jnp.dot(f, filt_ref[0, 0].astype(BF16), preferred_element_type=F32)
    pg = jnp.dot(f, filt_ref[0, 1].astype(BF16), preferred_element_type=F32)
    kr_ref[0] = (pf[0:n] + pg[0:n]) * scale
    ki_ref[0] = (pf[n:] - pg[n:]) * scale


def _filter_spectrum_dense(filt, tabs, d):
    nq, n, _ = filt.shape
    orders = nq // 2
    f4 = filt.reshape(orders, 2, n, d)
    out = pl.BlockSpec((1, 2 * n, d), lambda o: (o, 0, 0))
    return pl.pallas_call(
        functools.partial(_dense_spec_kernel, scale=1.0 / (2 * n)),
        grid=(orders,),
        in_specs=[pl.BlockSpec((4 * n, n), lambda o: (0, 0)), pl.BlockSpec((1, 2, n, d), lambda o: (o, 0, 0, 0))],
        out_specs=[out, out],
        out_shape=[jax.ShapeDtypeStruct((orders, 2 * n, d), F32)] * 2,
        compiler_params=_cparams("parallel"),
        name="filter_spectrum_dense",
    )(tabs['f'], f4)


def _dense_conv_kernel(f_ref, h_ref, u_ref, x_ref, kr_ref, ki_ref, sk_ref, o_ref):
    cr, ci = _cplx_mm(f_ref[...], u_ref[0, 0], u_ref[1, 0], False)
    kr = kr_ref[...]
    ki = ki_ref[...]
    yr, yi = _cplx_mm(h_ref[...], cr * kr - ci * ki, cr * ki + ci * kr, False)
    sk = sk_ref[...]
    o_ref[0, 0] = x_ref[0, 0] * (yr + u_ref[0, 0] * sk)
    o_ref[1, 0] = x_ref[1, 0] * (yi + u_ref[1, 0] * sk)


def _long_conv_dense(u, x, skip, kr, ki, tabs):
    b, n, d = u.shape
    sp = b // 2
    pair = pl.BlockSpec((2, 1, n, d), lambda si: (0, si, 0, 0))
    kb = pl.BlockSpec((2 * n, d), lambda si: (0, 0))
    out = pl.pallas_call(
        _dense_conv_kernel,
        grid=(sp,),
        in_specs=[pl.BlockSpec((4 * n, n), lambda si: (0, 0)), pl.BlockSpec((2 * n, 2 * n), lambda si: (0, 0)),
                  pair, pair, kb, kb, pl.BlockSpec((1, d), lambda si: (0, 0))],
        out_specs=pair,
        out_shape=jax.ShapeDtypeStruct((2, sp, n, d), F32),
        compiler_params=_cparams("parallel"),
        name="long_conv_dense",
    )(tabs['f'], tabs['h'], u.reshape(2, sp, n, d), x.reshape(2, sp, n, d), kr, ki, skip.reshape(1, d))
    return out.reshape(b, n, d)


def _hyena_mixer(x, g, mod, op, fft_tabs, dense_tabs):
    b, n, d = x.shape
    z = _proj_conv(x, g, mod, op['w_in'], op['b_in'], op['conv_w'], op['conv_b'])
    filt = _hyena_filters(n, op, d)
    if n <= DENSE_FFT_MAX:
        kr, ki = _filter_spectrum_dense(filt, dense_tabs, d)
        y = _long_conv_dense(z[0], z[1], op['skip'][0], kr[0], ki[0], dense_tabs)
        y = _long_conv_dense(y, z[2], op['skip'][1], kr[1], ki[1], dense_tabs)
    else:
        kr, ki = _filter_spectrum_fft(filt, fft_tabs, d)
        y, stage1 = _long_conv_fft(z, 0, z, 1, op['skip'][0], kr, ki, 0, fft_tabs, chain=True)
        y = _long_conv_fft(y[None], 0, z, 2, op['skip'][1], kr, ki, 1, fft_tabs, stage1=stage1)
    return _out_res(y, x, mod, op['w_out'], op['b_out'])


def _even_mixer(x, ctx, g, mod_l, mod_c, ep, tabs, need_ctx):
    rc, vc, kkc, gc, lwc, ac, kdc, qc, kac, vac = _even_prep(ctx, g, mod_c, ep, tabs, rope=False)
    rl, vl, kkl, gl, lwl, al, kdl, ql, kal, val = _even_prep(x, g, mod_l, ep, tabs, rope=True)
    b = x.shape[0]
    c = rl.shape[-1]
    h0 = jnp.zeros((2, b, HEAD_DIM, c), F32)
    y_ctx, s_ctx = _rwkv(rc, vc, kkc, lwc, ac, kdc, h0)
    y_lat, _ = _rwkv(rl, vl, kkl, lwl, al, kdl, s_ctx)
    b_lat = _attention(ql, kal, val, kac, vac, ep['sink'], local=True)
    x_new = _even_out(y_lat, rl, vl, gl, kdl, b_lat, x, mod_l, ep, tabs)
    if not need_ctx:
        return x_new, None
    b_ctx = _attention(qc, kac, vac, kac, vac, ep['sink'], local=False)
    ctx_new = _even_out(y_ctx, rc, vc, gc, kdc, b_ctx, ctx, mod_c, ep, tabs)
    return x_new, ctx_new


def _rope_tables(n_tokens):
    rows = n_tokens // GRID_W
    row = jnp.repeat(jnp.arange(rows), GRID_W).astype(F32)
    col = jnp.tile(jnp.arange(GRID_W), rows).astype(F32)
    n_freq = HEAD_DIM // 4
    inv = ROPE_THETA ** (-jnp.arange(n_freq, dtype=F32) / n_freq)
    ang = jnp.concatenate([row[:, None] * inv, col[:, None] * inv], axis=-1)
    cos, sin = jnp.cos(ang), jnp.sin(ang)
    reps = LANES // HEAD_DIM
    cos_t = jnp.tile(jnp.concatenate([cos, cos], axis=-1), (1, reps))
    sin_t = jnp.tile(jnp.concatenate([-sin, sin], axis=-1), (1, reps))
    return cos_t, sin_t


def _block_ones(width):
    idx = np.arange(width) // HEAD_DIM
    return jnp.asarray((idx[:, None] == idx[None, :]).astype(np.float32)).astype(BF16)


def _lora_pad(w):
    z = jnp.zeros_like(w[0])
    return jnp.stack([jnp.concatenate([w[0], z], axis=0), jnp.concatenate([z, w[1]], axis=0)], axis=0)


def kernel(x, c, ctx, c_ctx, ada_w, ada_b, norm1_g, norm2_g, ffn_up, ffn_conv_w, ffn_conv_b, ffn_down, ev_w_in, ev_mu_prev, ev_mu_next, ev_w0, ev_w2, ev_a0, ev_a2, ev_g2, ev_k_k, ev_k_a, ev_r_k, ev_ln_w, ev_ln_b, ev_q_norm, ev_k_norm, ev_sink, ev_w_out, od_w_in, od_b_in, od_conv_w, od_conv_b, od_f_w1, od_f_b1, od_f_w2, od_f_b2, od_f_w3, od_f_b3, od_f_freq, od_f_out, od_skip, od_w_out, od_b_out):
    bsz, seq, d = x.shape
    lc = ctx.shape[1]
    depth = ada_w.shape[0]
    a_width = ev_k_k.shape[-1]
    nq = ev_sink.shape[-1] * HEAD_DIM
    nk = B_KV_HEADS * HEAD_DIM

    cond = jnp.zeros((BF16_ROWS, d), F32).at[:bsz].set(c).at[bsz].set(c_ctx)
    mod = _ada_mod(cond, ada_w, ada_b)

    cos_t, sin_t = _rope_tables(seq)
    tabs = {'cos': cos_t, 'sin': sin_t, 'e': _block_ones(nq)}
    fft_tabs = _fft_tables(seq) if seq > DENSE_FFT_MAX else None
    dense_lat = _dense_tables(seq) if seq <= DENSE_FFT_MAX else None
    fft_ctx = _fft_tables(lc) if lc > DENSE_FFT_MAX else None
    dense_ctx = _dense_tables(lc) if lc <= DENSE_FFT_MAX else None

    for layer in range(depth):
        need_ctx = layer < depth - 1
        even = layer % 2 == 0
        j = layer // 2
        mod_l = mod[layer, :bsz].reshape(bsz, 1, 6 * d)
        mod_c = jnp.broadcast_to(mod[layer, bsz].reshape(1, 1, 6 * d), (bsz, 1, 6 * d))
        if even:
            ep = {'w_in': ev_w_in[j].astype(BF16), 'mu_prev': ev_mu_prev[j][None], 'mu_next': ev_mu_next[j][None],
                  'w0': ev_w0[j], 'w2pad': _lora_pad(ev_w2[j]).astype(BF16), 'a0': ev_a0[j],
                  'a2pad': _lora_pad(ev_a2[j]).astype(BF16), 'g2': ev_g2[j].astype(BF16),
                  'k_k': ev_k_k[j][None], 'k_a': ev_k_a[j][None], 'r_k': ev_r_k[j].reshape(1, a_width),
                  'ln_w': ev_ln_w[j][None], 'ln_b': ev_ln_b[j][None],
                  'q_norm_t': jnp.tile(ev_q_norm[j], nq // HEAD_DIM)[None],
                  'k_norm_t': jnp.tile(ev_k_norm[j], nk // HEAD_DIM)[None],
                  'sink': ev_sink[j], 'w_out': ev_w_out[j].astype(BF16), 'nq': nq, 'nk': nk}
            x, ctx_new = _even_mixer(x, ctx, norm1_g[layer], mod_l, mod_c, ep, tabs, need_ctx)
        else:
            emb_pad = HEAD_DIM
            op = {'w_in': od_w_in[j].astype(BF16), 'b_in': od_b_in[j], 'conv_w': od_conv_w[j],
                  'conv_b': od_conv_b[j],
                  'f_w1': jnp.pad(od_f_w1[j], ((0, emb_pad - HY_EMB), (0, 0))), 'f_b1': od_f_b1[j][None],
                  'f_w2': od_f_w2[j], 'f_b2': od_f_b2[j][None], 'f_w3': od_f_w3[j], 'f_b3': od_f_b3[j][None],
                  'f_freq': od_f_freq[j][None], 'f_out': od_f_out[j], 'skip': od_skip[j],
                  'w_out': od_w_out[j].astype(BF16), 'b_out': od_b_out[j]}
            ctx_new = _hyena_mixer(ctx, norm1_g[layer], mod_c, op, fft_ctx, dense_ctx) if need_ctx else None
            x = _hyena_mixer(x, norm1_g[layer], mod_l, op, fft_tabs, dense_lat)
        w_up = ffn_up[layer].astype(BF16)
        w_down = ffn_down[layer].astype(BF16)
        x = _conv_ffn(x, norm2_g[layer], mod_l, w_up, ffn_conv_w[layer], ffn_conv_b[layer], w_down)
        if need_ctx:
            ctx = _conv_ffn(ctx_new, norm2_g[layer], mod_c, w_up, ffn_conv_w[layer], ffn_conv_b[layer], w_down)
    return x
```
